```python
import jax, jax.numpy as jnp
from jax import lax
import numpy as np

D_MODEL = 2048
BATCH = 8
SEQ = 2048
DEPTH = 1

CHUNK = 64
HEAD_DIM = 128
N_HEADS_SB = 8
N_HEADS_CA = 8
W_SB = N_HEADS_SB * HEAD_DIM
W_CA = N_HEADS_CA * HEAD_DIM
LEFT_CHUNKS = 8
BAND = (LEFT_CHUNKS + 1) * CHUNK
REL_CLIP = 128
N_REL = REL_CLIP + CHUNK
Q_BLOCK = 128
D_FF = -(-8 * D_MODEL // (3 * 256)) * 256
D_PLE = 256
EPS = 1e-6
NEG = -1e30
IN_COLS = 3 * W_SB + 3 * W_CA + 2 * D_MODEL

kernel_name = "hybrid_stickbreak_chunkrel_block"


def rmsnorm(x, g):
    xf = x.astype(jnp.float32)
    y = xf * lax.rsqrt(jnp.mean(xf * xf, axis=-1, keepdims=True) + EPS)
    return (y * g.astype(jnp.float32)).astype(x.dtype)


def stick_breaking_attention(q, k, v):
    B, S, H, Dh = q.shape
    scale = Dh ** -0.5
    outs = []
    for qb in range(S // Q_BLOCK):
        t0 = qb * Q_BLOCK
        t1 = t0 + Q_BLOCK
        kb = k[:, :t1]
        vb = v[:, :t1]
        z = jnp.einsum('bqhd,bkhd->bhqk', q[:, t0:t1], kb).astype(jnp.float32) * scale
        past = jnp.arange(t1)[None, :] < jnp.arange(t0, t1)[:, None]
        log_keep = jnp.where(past, jax.nn.log_sigmoid(-z), 0.0)
        between = lax.cumsum(log_keep, axis=3, reverse=True) - log_keep
        a = jnp.where(past, jnp.exp(jax.nn.log_sigmoid(z) + between), 0.0)
        outs.append(jnp.einsum('bhqk,bkhd->bqhd', a.astype(v.dtype), vb))
    return jnp.concatenate(outs, axis=1)


def chunked_relpos_attention(q, k, v, rel_bias):
    B, S, H, Dh = q.shape
    nc = S // CHUNK
    pad = LEFT_CHUNKS * CHUNK
    scale = Dh ** -0.5
    kp = jnp.pad(k, ((0, 0), (pad, 0), (0, 0), (0, 0)))
    vp = jnp.pad(v, ((0, 0), (pad, 0), (0, 0), (0, 0)))
    qc = jnp.moveaxis(q.reshape(B, nc, CHUNK, H, Dh), 1, 0)
    s_loc = jnp.arange(BAND)[None, :]
    rel = s_loc - (jnp.arange(CHUNK)[:, None] + pad)
    rel_idx = jnp.clip(rel, -REL_CLIP, CHUNK - 1) + REL_CLIP
    bias = rel_bias.astype(jnp.float32)[:, rel_idx]

    def one_chunk(args):
        c, qblk = args
        start = c * CHUNK
        kb = lax.dynamic_slice_in_dim(kp, start, BAND, axis=1)
        vb = lax.dynamic_slice_in_dim(vp, start, BAND, axis=1)
        valid = (start + s_loc) >= pad
        z = jnp.einsum('bqhd,bkhd->bhqk', qblk, kb).astype(jnp.float32) * scale + bias
        w = jax.nn.softmax(jnp.where(valid, z, NEG), axis=-1)
        return jnp.einsum('bhqk,bkhd->bqhd', w.astype(v.dtype), vb)

    out = lax.map(one_chunk, (jnp.arange(nc), qc))
    return jnp.moveaxis(out, 0, 1).reshape(B, S, H, Dh)


def _fwd_setup_inputs(seed: int = 0) -> dict:
    key = jax.random.key(seed)
    ks = jax.random.split(key, 16)
    f32 = jnp.float32

    def w(k, shape, fan_in):
        return jax.random.normal(k, shape, f32) * fan_in ** -0.5

    def gain(k, shape):
        return 1.0 + 0.05 * jax.random.normal(k, shape, f32)

    return {
        "x": jax.random.normal(ks[0], (BATCH, SEQ, D_MODEL), f32),
        "p": jax.random.normal(ks[1], (DEPTH, BATCH, SEQ, D_PLE), f32),
        "w_in": w(ks[2], (DEPTH, D_MODEL, IN_COLS), D_MODEL),
        "w_sb_out": w(ks[3], (DEPTH, W_SB, D_MODEL), W_SB),
        "w_ca_out": w(ks[4], (DEPTH, W_CA, D_MODEL), W_CA),
        "w_mix_out": w(ks[5], (DEPTH, D_MODEL, D_MODEL), D_MODEL),
        "rel_bias": 0.3 * jax.random.normal(ks[6], (DEPTH, N_HEADS_CA, N_REL), f32),
        "g_mix": gain(ks[7], (DEPTH, D_MODEL)),
        "g_ffn": gain(ks[8], (DEPTH, D_MODEL)),
        "g_ple": gain(ks[9], (DEPTH, D_MODEL)),
        "g_final": gain(ks[10], (D_MODEL,)),
        "w_ffn_in": w(ks[11], (DEPTH, D_MODEL, 2 * D_FF), D_MODEL),
        "w_ffn_out": w(ks[12], (DEPTH, D_FF, D_MODEL), D_FF),
        "w_ple_in": w(ks[13], (DEPTH, D_PLE, D_MODEL), D_PLE),
        "w_ple_gate": w(ks[14], (DEPTH, D_MODEL, D_MODEL), D_MODEL),
    }


def _fwd_reference(x, p, w_in, w_sb_out, w_ca_out, w_mix_out, rel_bias, g_mix, g_ffn, g_ple, g_final,
              w_ffn_in, w_ffn_out, w_ple_in, w_ple_gate):
    B, S, _ = x.shape
    splits = [W_SB, 2 * W_SB, 3 * W_SB, 3 * W_SB + W_CA, 3 * W_SB + 2 * W_CA, 3 * W_SB + 3 * W_CA,
              3 * W_SB + 3 * W_CA + D_MODEL]
    for i in range(DEPTH):
        h = rmsnorm(x, g_mix[i])
        proj = h @ w_in[i]
        q_sb, k_sb, v_sb, q_ca, k_ca, v_ca, gate_sb, gate_ca = jnp.split(proj, splits, axis=-1)
        y_sb = stick_breaking_attention(q_sb.reshape(B, S, N_HEADS_SB, HEAD_DIM),
                                        k_sb.reshape(B, S, N_HEADS_SB, HEAD_DIM),
                                        v_sb.reshape(B, S, N_HEADS_SB, HEAD_DIM)).reshape(B, S, W_SB)
        y_ca = chunked_relpos_attention(q_ca.reshape(B, S, N_HEADS_CA, HEAD_DIM),
                                        k_ca.reshape(B, S, N_HEADS_CA, HEAD_DIM),
                                        v_ca.reshape(B, S, N_HEADS_CA, HEAD_DIM),
                                        rel_bias[i]).reshape(B, S, W_CA)
        merged = (jax.nn.sigmoid(gate_sb) * (y_sb @ w_sb_out[i])
                  + jax.nn.sigmoid(gate_ca) * (y_ca @ w_ca_out[i]))
        x = x + merged @ w_mix_out[i]
        h = rmsnorm(x, g_ffn[i])
        g_ff, u_ff = jnp.split(h @ w_ffn_in[i], 2, axis=-1)
        x = x + (jax.nn.silu(g_ff) * u_ff) @ w_ffn_out[i]
        h = rmsnorm(x, g_ple[i])
        x = x + jax.nn.sigmoid(h @ w_ple_gate[i]) * (p[i] @ w_ple_in[i])
    return rmsnorm(x, g_final)


import jax as _jax
import jax.numpy as _jnp

TWIN_FORMAT = 'train_step'
FWD_PARAMS = ['x', 'p', 'w_in', 'w_sb_out', 'w_ca_out', 'w_mix_out', 'rel_bias', 'g_mix', 'g_ffn', 'g_ple', 'g_final', 'w_ffn_in', 'w_ffn_out', 'w_ple_in', 'w_ple_gate']
TWIN_WEIGHTS = ['w_in', 'w_sb_out', 'w_ca_out', 'w_mix_out', 'rel_bias', 'g_mix', 'g_ffn', 'g_ple', 'g_final', 'w_ffn_in', 'w_ffn_out', 'w_ple_in', 'w_ple_gate']
TWIN_DIFF_INPUT = 'x'
TWIN_INPUTS = ['x', 'p', 'w_in', 'w_sb_out', 'w_ca_out', 'w_mix_out', 'rel_bias', 'g_mix', 'g_ffn', 'g_ple', 'g_final', 'w_ffn_in', 'w_ffn_out', 'w_ple_in', 'w_ple_gate', 'loss_target', 'm_w_in', 'm_w_sb_out', 'm_w_ca_out', 'm_w_mix_out', 'm_rel_bias', 'm_g_mix', 'm_g_ffn', 'm_g_ple', 'm_g_final', 'm_w_ffn_in', 'm_w_ffn_out', 'm_w_ple_in', 'm_w_ple_gate', 'v_w_in', 'v_w_sb_out', 'v_w_ca_out', 'v_w_mix_out', 'v_rel_bias', 'v_g_mix', 'v_g_ffn', 'v_g_ple', 'v_g_final', 'v_w_ffn_in', 'v_w_ffn_out', 'v_w_ple_in', 'v_w_ple_gate']
TWIN_OUTPUTS = ['loss', 'grad_x', 'grad_w_in', 'grad_w_sb_out', 'grad_w_ca_out', 'grad_w_mix_out', 'grad_rel_bias', 'grad_g_mix', 'grad_g_ffn', 'grad_g_ple', 'grad_g_final', 'grad_w_ffn_in', 'grad_w_ffn_out', 'grad_w_ple_in', 'grad_w_ple_gate', 'delta_w_in', 'delta_w_sb_out', 'delta_w_ca_out', 'delta_w_mix_out', 'delta_rel_bias', 'delta_g_mix', 'delta_g_ffn', 'delta_g_ple', 'delta_g_final', 'delta_w_ffn_in', 'delta_w_ffn_out', 'delta_w_ple_in', 'delta_w_ple_gate', 'new_m_w_in', 'new_m_w_sb_out', 'new_m_w_ca_out', 'new_m_w_mix_out', 'new_m_rel_bias', 'new_m_g_mix', 'new_m_g_ffn', 'new_m_g_ple', 'new_m_g_final', 'new_m_w_ffn_in', 'new_m_w_ffn_out', 'new_m_w_ple_in', 'new_m_w_ple_gate', 'new_v_w_in', 'new_v_w_sb_out', 'new_v_w_ca_out', 'new_v_w_mix_out', 'new_v_rel_bias', 'new_v_g_mix', 'new_v_g_ffn', 'new_v_g_ple', 'new_v_g_final', 'new_v_w_ffn_in', 'new_v_w_ffn_out', 'new_v_w_ple_in', 'new_v_w_ple_gate']
TWIN_LEAF_KINDS = {'loss': 'loss', 'grad_x': 'grad_x', 'grad_w_in': 'grad_w', 'grad_w_sb_out': 'grad_w', 'grad_w_ca_out': 'grad_w', 'grad_w_mix_out': 'grad_w', 'grad_rel_bias': 'grad_w', 'grad_g_mix': 'grad_w', 'grad_g_ffn': 'grad_w', 'grad_g_ple': 'grad_w', 'grad_g_final': 'grad_w', 'grad_w_ffn_in': 'grad_w', 'grad_w_ffn_out': 'grad_w', 'grad_w_ple_in': 'grad_w', 'grad_w_ple_gate': 'grad_w', 'delta_w_in': 'delta_w', 'delta_w_sb_out': 'delta_w', 'delta_w_ca_out': 'delta_w', 'delta_w_mix_out': 'delta_w', 'delta_rel_bias': 'delta_w', 'delta_g_mix': 'delta_w', 'delta_g_ffn': 'delta_w', 'delta_g_ple': 'delta_w', 'delta_g_final': 'delta_w', 'delta_w_ffn_in': 'delta_w', 'delta_w_ffn_out': 'delta_w', 'delta_w_ple_in': 'delta_w', 'delta_w_ple_gate': 'delta_w', 'new_m_w_in': 'new_m', 'new_m_w_sb_out': 'new_m', 'new_m_w_ca_out': 'new_m', 'new_m_w_mix_out': 'new_m', 'new_m_rel_bias': 'new_m', 'new_m_g_mix': 'new_m', 'new_m_g_ffn': 'new_m', 'new_m_g_ple': 'new_m', 'new_m_g_final': 'new_m', 'new_m_w_ffn_in': 'new_m', 'new_m_w_ffn_out': 'new_m', 'new_m_w_ple_in': 'new_m', 'new_m_w_ple_gate': 'new_m', 'new_v_w_in': 'new_v', 'new_v_w_sb_out': 'new_v', 'new_v_w_ca_out': 'new_v', 'new_v_w_mix_out': 'new_v', 'new_v_rel_bias': 'new_v', 'new_v_g_mix': 'new_v', 'new_v_g_ffn': 'new_v', 'new_v_g_ple': 'new_v', 'new_v_g_final': 'new_v', 'new_v_w_ffn_in': 'new_v', 'new_v_w_ffn_out': 'new_v', 'new_v_w_ple_in': 'new_v', 'new_v_w_ple_gate': 'new_v'}


def _forward(args):
    return _fwd_reference(*[args[k] for k in FWD_PARAMS])


def _output_shape():
    out = _jax.eval_shape(lambda: _forward(_fwd_setup_inputs(0)))
    return out.shape, out.dtype

N_MICROBATCH = 1
ADAM_LR = 0.001
ADAM_B1 = 0.9
ADAM_B2 = 0.999
ADAM_EPS = 1e-08
ADAM_WD = 0.01
ADAM_STEP = 10
PER_EXAMPLE_BATCH_AXIS = {'x': 0, 'p': 1, 'loss_target': 0}
SHARED_INPUTS = []
_WEIGHT_DTYPES = {'w_in': _jnp.float32, 'w_sb_out': _jnp.float32, 'w_ca_out': _jnp.float32, 'w_mix_out': _jnp.float32, 'rel_bias': _jnp.float32, 'g_mix': _jnp.float32, 'g_ffn': _jnp.float32, 'g_ple': _jnp.float32, 'g_final': _jnp.float32, 'w_ffn_in': _jnp.float32, 'w_ffn_out': _jnp.float32, 'w_ple_in': _jnp.float32, 'w_ple_gate': _jnp.float32}
MOMENT_SCALE = {'w_in': 1.270550e-02, 'w_sb_out': 2.184159e-02, 'w_ca_out': 5.065257e-03, 'w_mix_out': 2.204280e-02, 'rel_bias': 5.324104e-03, 'g_mix': 2.853047e-02, 'g_ffn': 4.319636e-02, 'g_ple': 1.008324e-02, 'g_final': 8.010373e+00, 'w_ffn_in': 1.765900e-02, 'w_ffn_out': 2.885114e-02, 'w_ple_in': 2.615895e-02, 'w_ple_gate': 1.002776e-02}


def _to_microbatches(a, axis):
    t = _jnp.moveaxis(a, axis, 0)
    t = t.reshape((N_MICROBATCH, t.shape[0] // N_MICROBATCH) + t.shape[1:])
    return _jnp.moveaxis(t, 1, axis + 1)


def setup_inputs(seed: int = 0) -> dict:
    inp = _fwd_setup_inputs(seed)
    key = _jax.random.fold_in(_jax.random.key(seed), 7919)
    shape, _ = _output_shape()
    out = dict(inp)
    out["loss_target"] = _jax.random.normal(_jax.random.fold_in(key, 0), shape, _jnp.float32)
    for i, name in enumerate(TWIN_WEIGHTS):
        w = inp[name].astype(_jnp.float32)
        if MOMENT_SCALE is None:
            s = _jnp.sqrt(_jnp.mean(_jnp.square(w)) + 1e-30)
        else:
            s = MOMENT_SCALE[name]
        km, kv = _jax.random.split(_jax.random.fold_in(key, i + 1))
        out[name] = w
        out["m_" + name] = s * _jax.random.normal(km, w.shape, _jnp.float32)
        out["v_" + name] = (s * s) * _jax.random.uniform(kv, w.shape, _jnp.float32, 0.5, 1.5)
    if N_MICROBATCH > 1:
        for name, axis in PER_EXAMPLE_BATCH_AXIS.items():
            out[name] = _to_microbatches(out[name], axis)
    return {'x': out['x'], 'p': out['p'], 'w_in': out['w_in'], 'w_sb_out': out['w_sb_out'], 'w_ca_out': out['w_ca_out'], 'w_mix_out': out['w_mix_out'], 'rel_bias': out['rel_bias'], 'g_mix': out['g_mix'], 'g_ffn': out['g_ffn'], 'g_ple': out['g_ple'], 'g_final': out['g_final'], 'w_ffn_in': out['w_ffn_in'], 'w_ffn_out': out['w_ffn_out'], 'w_ple_in': out['w_ple_in'], 'w_ple_gate': out['w_ple_gate'], 'loss_target': out['loss_target'], 'm_w_in': out['m_w_in'], 'm_w_sb_out': out['m_w_sb_out'], 'm_w_ca_out': out['m_w_ca_out'], 'm_w_mix_out': out['m_w_mix_out'], 'm_rel_bias': out['m_rel_bias'], 'm_g_mix': out['m_g_mix'], 'm_g_ffn': out['m_g_ffn'], 'm_g_ple': out['m_g_ple'], 'm_g_final': out['m_g_final'], 'm_w_ffn_in': out['m_w_ffn_in'], 'm_w_ffn_out': out['m_w_ffn_out'], 'm_w_ple_in': out['m_w_ple_in'], 'm_w_ple_gate': out['m_w_ple_gate'], 'v_w_in': out['v_w_in'], 'v_w_sb_out': out['v_w_sb_out'], 'v_w_ca_out': out['v_w_ca_out'], 'v_w_mix_out': out['v_w_mix_out'], 'v_rel_bias': out['v_rel_bias'], 'v_g_mix': out['v_g_mix'], 'v_g_ffn': out['v_g_ffn'], 'v_g_ple': out['v_g_ple'], 'v_g_final': out['v_g_final'], 'v_w_ffn_in': out['v_w_ffn_in'], 'v_w_ffn_out': out['v_w_ffn_out'], 'v_w_ple_in': out['v_w_ple_in'], 'v_w_ple_gate': out['v_w_ple_gate']}


def _loss(weights, diff, rest, loss_target):
    with _jax.named_scope("forward"):
        args = {**rest, TWIN_DIFF_INPUT: diff, **{k: w.astype(_WEIGHT_DTYPES[k]) for k, w in weights.items()}}
        y = _forward(args)
    with _jax.named_scope("loss_head"):
        err = _jnp.square(y.astype(_jnp.float32) - loss_target)
        return 0.5 * _jnp.sum(_jnp.mean(err, axis=-1)) if err.ndim else 0.5 * err


def _adamw(w, g, m, v):
    m = ADAM_B1 * m + (1.0 - ADAM_B1) * g
    v = ADAM_B2 * v + (1.0 - ADAM_B2) * _jnp.square(g)
    m_hat = m / (1.0 - ADAM_B1 ** ADAM_STEP)
    v_hat = v / (1.0 - ADAM_B2 ** ADAM_STEP)
    delta = -ADAM_LR * (m_hat / (_jnp.sqrt(v_hat) + ADAM_EPS) + ADAM_WD * w)
    return delta, m, v


def reference(x, p, w_in, w_sb_out, w_ca_out, w_mix_out, rel_bias, g_mix, g_ffn, g_ple, g_final, w_ffn_in, w_ffn_out, w_ple_in, w_ple_gate, loss_target, m_w_in, m_w_sb_out, m_w_ca_out, m_w_mix_out, m_rel_bias, m_g_mix, m_g_ffn, m_g_ple, m_g_final, m_w_ffn_in, m_w_ffn_out, m_w_ple_in, m_w_ple_gate, v_w_in, v_w_sb_out, v_w_ca_out, v_w_mix_out, v_rel_bias, v_g_mix, v_g_ffn, v_g_ple, v_g_final, v_w_ffn_in, v_w_ffn_out, v_w_ple_in, v_w_ple_gate):
    given = dict(x=x, p=p, w_in=w_in, w_sb_out=w_sb_out, w_ca_out=w_ca_out, w_mix_out=w_mix_out, rel_bias=rel_bias, g_mix=g_mix, g_ffn=g_ffn, g_ple=g_ple, g_final=g_final, w_ffn_in=w_ffn_in, w_ffn_out=w_ffn_out, w_ple_in=w_ple_in, w_ple_gate=w_ple_gate, loss_target=loss_target, m_w_in=m_w_in, m_w_sb_out=m_w_sb_out, m_w_ca_out=m_w_ca_out, m_w_mix_out=m_w_mix_out, m_rel_bias=m_rel_bias, m_g_mix=m_g_mix, m_g_ffn=m_g_ffn, m_g_ple=m_g_ple, m_g_final=m_g_final, m_w_ffn_in=m_w_ffn_in, m_w_ffn_out=m_w_ffn_out, m_w_ple_in=m_w_ple_in, m_w_ple_gate=m_w_ple_gate, v_w_in=v_w_in, v_w_sb_out=v_w_sb_out, v_w_ca_out=v_w_ca_out, v_w_mix_out=v_w_mix_out, v_rel_bias=v_rel_bias, v_g_mix=v_g_mix, v_g_ffn=v_g_ffn, v_g_ple=v_g_ple, v_g_final=v_g_final, v_w_ffn_in=v_w_ffn_in, v_w_ffn_out=v_w_ffn_out, v_w_ple_in=v_w_ple_in, v_w_ple_gate=v_w_ple_gate)
    weights = {n: given[n] for n in TWIN_WEIGHTS}
    shared = {n: given[n] for n in SHARED_INPUTS}
    per_example = {n: given[n] for n in ['x', 'p']}
    grad_fn = _jax.value_and_grad(_loss, argnums=(0, 1))

    def one_microbatch(ex, loss_target):
        ex = dict(ex)
        diff = ex.pop(TWIN_DIFF_INPUT)
        return grad_fn(weights, diff, {**shared, **ex}, loss_target)

    if N_MICROBATCH == 1:
        loss, (grad_w, grad_x) = one_microbatch(per_example, given["loss_target"])
    else:
        def body(carry, xs):
            loss_sum, grad_sum = carry
            l_k, (gw_k, gx_k) = one_microbatch(xs[0], xs[1])
            with _jax.named_scope("update"):
                return (loss_sum + l_k, _jax.tree.map(_jnp.add, grad_sum, gw_k)), gx_k

        init = (_jnp.zeros((), _jnp.float32), _jax.tree.map(_jnp.zeros_like, weights))
        (loss, grad_w), grad_x = _jax.lax.scan(body, init, (per_example, given["loss_target"]))
    with _jax.named_scope("update"):
        delta_w, new_m, new_v = {}, {}, {}
        for n in TWIN_WEIGHTS:
            delta_w[n], new_m[n], new_v[n] = _adamw(weights[n], grad_w[n], given["m_" + n], given["v_" + n])
    return (loss, grad_x, *[grad_w[n] for n in TWIN_WEIGHTS], *[delta_w[n] for n in TWIN_WEIGHTS],
            *[new_m[n] for n in TWIN_WEIGHTS], *[new_v[n] for n in TWIN_WEIGHTS])
```

```python
import functools
import math

import jax
import jax.numpy as jnp
import numpy as np
from jax import lax
from jax.experimental import pallas as pl
from jax.experimental.pallas import tpu as pltpu

F32 = jnp.float32
BF16 = jnp.bfloat16

HEAD_DIM = 128
CHUNK = 64
LEFT_CHUNKS = 8
REL_CLIP = 128
N_REL = REL_CLIP + CHUNK
BAND = (LEFT_CHUNKS + 2) * CHUNK
PAD = (LEFT_CHUNKS + 1) * CHUNK
SB_BLOCK = 128
EPS = 1e-6
NEG = -1e30

ADAM_LR = 0.001
ADAM_B1 = 0.9
ADAM_B2 = 0.999
ADAM_EPS = 1e-08
ADAM_WD = 0.01
ADAM_STEP = 10

VMEM_LIMIT = 48 * 1024 * 1024
MESH = pl.DeviceIdType.MESH
N_CHIPS = 4


def _pick(dim, prefs):
    for t in prefs:
        if dim % t == 0:
            return t
    raise ValueError(f"no tile for {dim}")


def _cparams(sem=None):
    return pltpu.CompilerParams(dimension_semantics=sem, vmem_limit_bytes=VMEM_LIMIT)


def _sigmoid(v):
    return 1.0 / (1.0 + jnp.exp(-v))


def _dot(a, b, dims):
    return lax.dot_general(a, b, (dims, ((), ())), preferred_element_type=F32)


def _dot_nn(a, b):
    return _dot(a, b, ((1,), (0,)))


def _dot_nt(a, b):
    return _dot(a, b, ((1,), (1,)))


def _dot_tn(a, b):
    return _dot(a, b, ((0,), (0,)))


def _mm(a, b, mode, out_dtypes, *, name, n=None, b_col_off=0, resid=None):
    if mode == "nn":
        m, k = a.shape
        n = b.shape[1] if n is None else n
    elif mode == "nt":
        m, k = a.shape
        n = b.shape[0]
    else:
        k, m = a.shape
        n = b.shape[1]
    tm = _pick(m, (512, 256, 128))
    tn = _pick(math.gcd(n, b_col_off) if b_col_off else n, (1024, 512, 256, 128))
    tk = _pick(k, (1024, 512, 256, 128))
    nk = k // tk
    boff = b_col_off // tn
    n_out = len(out_dtypes)
    has_resid = resid is not None

    def body(*refs):
        a_ref, b_ref = refs[0], refs[1]
        r_ref = refs[2] if has_resid else None
        o_refs = refs[2 + has_resid: 2 + has_resid + n_out]
        acc_ref = refs[-1]
        kk = pl.program_id(2)

        @pl.when(kk == 0)
        def _():
            acc_ref[...] = jnp.zeros_like(acc_ref)

        av = a_ref[...].astype(BF16)
        bv = b_ref[...].astype(BF16)
        if mode == "nn":
            acc_ref[...] += _dot_nn(av, bv)
        elif mode == "nt":
            acc_ref[...] += _dot_nt(av, bv)
        else:
            acc_ref[...] += _dot_tn(av, bv)

        @pl.when(kk == nk - 1)
        def _():
            r = acc_ref[...]
            if has_resid:
                r = r + r_ref[...]
            for o_ref in o_refs:
                o_ref[...] = r.astype(o_ref.dtype)

    if mode == "nn":
        a_spec = pl.BlockSpec((tm, tk), lambda i, j, kk: (i, kk))
        b_spec = pl.BlockSpec((tk, tn), lambda i, j, kk: (kk, j + boff))
    elif mode == "nt":
        a_spec = pl.BlockSpec((tm, tk), lambda i, j, kk: (i, kk))
        b_spec = pl.BlockSpec((tn, tk), lambda i, j, kk: (j, kk))
    else:
        a_spec = pl.BlockSpec((tk, tm), lambda i, j, kk: (kk, i))
        b_spec = pl.BlockSpec((tk, tn), lambda i, j, kk: (kk, j))
    o_spec = pl.BlockSpec((tm, tn), lambda i, j, kk: (i, j))
    in_specs = [a_spec, b_spec] + ([o_spec] if has_resid else [])
    args = [a, b] + ([resid] if has_resid else [])
    outs = pl.pallas_call(
        body, name=name,
        grid=(m // tm, n // tn, nk),
        in_specs=in_specs,
        out_specs=[o_spec] * n_out,
        out_shape=[jax.ShapeDtypeStruct((m, n), dt) for dt in out_dtypes],
        scratch_shapes=[pltpu.VMEM((tm, tn), F32)],
        compiler_params=_cparams(("parallel", "parallel", "arbitrary")),
    )(*args)
    return outs[0] if n_out == 1 else tuple(outs)


def _row_tile(s):
    return _pick(s, (256, 128))


def _rms_fwd(x, g, *, name):
    s, d = x.shape
    tr = _row_tile(s)

    def body(x_ref, g_ref, o_ref):
        xv = x_ref[...]
        r = lax.rsqrt(jnp.mean(xv * xv, axis=1, keepdims=True) + EPS)
        o_ref[...] = (xv * r * g_ref[...]).astype(o_ref.dtype)

    return pl.pallas_call(
        body, name=name, grid=(s // tr,),
        in_specs=[pl.BlockSpec((tr, d), lambda i: (i, 0)), pl.BlockSpec((1, d), lambda i: (0, 0))],
        out_specs=pl.BlockSpec((tr, d), lambda i: (i, 0)),
        out_shape=jax.ShapeDtypeStruct((s, d), BF16),
        compiler_params=_cparams(("parallel",)),
    )(x, g)


def _rms_bwd(x, g, dh, dres, *, name):
    s, d = x.shape
    tr = _row_tile(s)

    def body(x_ref, g_ref, dh_ref, dres_ref, dx_ref, dg_ref):
        i = pl.program_id(0)
        xv = x_ref[...]
        r = lax.rsqrt(jnp.mean(xv * xv, axis=1, keepdims=True) + EPS)
        xhat = xv * r
        dhv = dh_ref[...]
        dxhat = dhv * g_ref[...]
        proj = jnp.mean(dxhat * xhat, axis=1, keepdims=True)
        dx_ref[...] = dres_ref[...] + r * (dxhat - xhat * proj)

        @pl.when(i == 0)
        def _():
            dg_ref[...] = jnp.zeros_like(dg_ref)

        dg_ref[...] += jnp.sum(dhv * xhat, axis=0, keepdims=True)

    row = pl.BlockSpec((tr, d), lambda i: (i, 0))
    vec = pl.BlockSpec((1, d), lambda i: (0, 0))
    return pl.pallas_call(
        body, name=name, grid=(s // tr,),
        in_specs=[row, vec, row, row],
        out_specs=[row, vec],
        out_shape=[jax.ShapeDtypeStruct((s, d), F32), jax.ShapeDtypeStruct((1, d), F32)],
        compiler_params=_cparams(("arbitrary",)),
    )(x, g, dh, dres)


def _final_loss(x, g, target, *, name):
    s, d = x.shape
    tr = _row_tile(s)

    def body(x_ref, g_ref, t_ref, dx_ref, dg_ref, loss_ref):
        i = pl.program_id(0)
        xv = x_ref[...]
        gv = g_ref[...]
        r = lax.rsqrt(jnp.mean(xv * xv, axis=1, keepdims=True) + EPS)
        xhat = xv * r
        err = xhat * gv - t_ref[...]
        dy = err * (1.0 / d)
        dxhat = dy * gv
        proj = jnp.mean(dxhat * xhat, axis=1, keepdims=True)
        dx_ref[...] = r * (dxhat - xhat * proj)

        @pl.when(i == 0)
        def _():
            dg_ref[...] = jnp.zeros_like(dg_ref)
            loss_ref[...] = jnp.zeros_like(loss_ref)

        dg_ref[...] += jnp.sum(dy * xhat, axis=0, keepdims=True)
        part = 0.5 * jnp.sum(jnp.mean(err * err, axis=1, keepdims=True), axis=0, keepdims=True)
        loss_ref[...] += jnp.broadcast_to(part, loss_ref.shape)

    row = pl.BlockSpec((tr, d), lambda i: (i, 0))
    vec = pl.BlockSpec((1, d), lambda i: (0, 0))
    return pl.pallas_call(
        body, name=name, grid=(s // tr,),
        in_specs=[row, vec, row],
        out_specs=[row, vec, pl.BlockSpec((1, 128), lambda i: (0, 0))],
        out_shape=[jax.ShapeDtypeStruct((s, d), F32), jax.ShapeDtypeStruct((1, d), F32),
                   jax.ShapeDtypeStruct((1, 128), F32)],
        compiler_params=_cparams(("arbitrary",)),
    )(x, g, target)


def _ew(body, ins, in_blocks, outs, out_blocks, grid, *, name):
    return pl.pallas_call(
        body, name=name, grid=grid,
        in_specs=[pl.BlockSpec(bs, im) for bs, im in in_blocks],
        out_specs=[pl.BlockSpec(bs, im) for bs, im in out_blocks],
        out_shape=outs,
        compiler_params=_cparams(("parallel",) * len(grid)),
    )(*ins)


def _gate_merge_fwd(gates, o_sb, o_ca, *, name):
    s, d = o_sb.shape
    tr, tc = _row_tile(s), _pick(d, (1024, 512, 256, 128))
    nc = d // tc

    def body(gs_ref, gc_ref, os_ref, oc_ref, m_ref):
        m = _sigmoid(gs_ref[...]) * os_ref[...] + _sigmoid(gc_ref[...]) * oc_ref[...]
        m_ref[...] = m.astype(m_ref.dtype)

    blk = ((tr, tc), lambda i, j: (i, j))
    return _ew(body, [gates, gates, o_sb, o_ca],
               [blk, ((tr, tc), lambda i, j: (i, j + nc)), blk, blk],
               [jax.ShapeDtypeStruct((s, d), BF16)], [blk], (s // tr, nc), name=name)[0]


def _gate_merge_bwd(dmerged, gates, o_sb, o_ca, *, name):
    s, d = o_sb.shape
    tr, tc = _row_tile(s), _pick(d, (1024, 512, 256, 128))
    nc = d // tc

    def body(dm_ref, gs_ref, gc_ref, os_ref, oc_ref, dgs_ref, dgc_ref, dos_ref, doc_ref):
        dm = dm_ref[...]
        ss = _sigmoid(gs_ref[...])
        sc = _sigmoid(gc_ref[...])
        dgs_ref[...] = (dm * os_ref[...] * ss * (1.0 - ss)).astype(dgs_ref.dtype)
        dgc_ref[...] = (dm * oc_ref[...] * sc * (1.0 - sc)).astype(dgc_ref.dtype)
        dos_ref[...] = (dm * ss).astype(dos_ref.dtype)
        doc_ref[...] = (dm * sc).astype(doc_ref.dtype)

    blk = ((tr, tc), lambda i, j: (i, j))
    sd = jax.ShapeDtypeStruct((s, d), BF16)
    return _ew(body, [dmerged, gates, gates, o_sb, o_ca],
               [blk, blk, ((tr, tc), lambda i, j: (i, j + nc)), blk, blk],
               [sd, sd, sd, sd], [blk, blk, blk, blk], (s // tr, nc), name=name)


def _swiglu_fwd(gu, *, name):
    s, f2 = gu.shape
    f = f2 // 2
    tr, tc = _row_tile(s), _pick(f, (512, 256, 128))
    nc = f // tc

    def body(g_ref, u_ref, a_ref):
        gv = g_ref[...]
        a_ref[...] = (gv * _sigmoid(gv) * u_ref[...]).astype(a_ref.dtype)

    blk = ((tr, tc), lambda i, j: (i, j))
    return _ew(body, [gu, gu], [blk, ((tr, tc), lambda i, j: (i, j + nc))],
               [jax.ShapeDtypeStruct((s, f), BF16)], [blk], (s // tr, nc), name=name)[0]


def _swiglu_bwd(dact, gu, *, name):
    s, f2 = gu.shape
    f = f2 // 2
    tr, tc = _row_tile(s), _pick(f, (512, 256, 128))
    nc = f // tc

    def body(da_ref, g_ref, u_ref, dg_ref, du_ref):
        da = da_ref[...]
        gv = g_ref[...]
        sg = _sigmoid(gv)
        dg_ref[...] = (da * u_ref[...] * sg * (1.0 + gv * (1.0 - sg))).astype(dg_ref.dtype)
        du_ref[...] = (da * gv * sg).astype(du_ref.dtype)

    blk = ((tr, tc), lambda i, j: (i, j))
    hi = ((tr, tc), lambda i, j: (i, j + nc))
    sd = jax.ShapeDtypeStruct((s, f), BF16)
    dg, du = _ew(body, [dact, gu, gu], [blk, blk, hi], [sd, sd], [blk, blk], (s // tr, nc), name=name)
    return dg, du


def _ple_fwd(x, t, pe, *, name):
    s, d = x.shape
    tr, tc = _row_tile(s), _pick(d, (1024, 512, 256, 128))

    def body(x_ref, t_ref, p_ref, o_ref):
        o_ref[...] = x_ref[...] + _sigmoid(t_ref[...]) * p_ref[...]

    blk = ((tr, tc), lambda i, j: (i, j))
    return _ew(body, [x, t, pe], [blk, blk, blk],
               [jax.ShapeDtypeStruct((s, d), F32)], [blk], (s // tr, d // tc), name=name)[0]


def _ple_bwd(dx, t, pe, *, name):
    s, d = dx.shape
    tr, tc = _row_tile(s), _pick(d, (1024, 512, 256, 128))

    def body(dx_ref, t_ref, p_ref, dt_ref, dp_ref):
        dxv = dx_ref[...]
        sg = _sigmoid(t_ref[...])
        dt_ref[...] = (dxv * p_ref[...] * sg * (1.0 - sg)).astype(dt_ref.dtype)
        dp_ref[...] = (dxv * sg).astype(dp_ref.dtype)

    blk = ((tr, tc), lambda i, j: (i, j))
    sd = jax.ShapeDtypeStruct((s, d), BF16)
    return _ew(body, [dx, t, pe], [blk, blk, blk], [sd, sd], [blk, blk], (s // tr, d // tc), name=name)


def _sb_masks():
    row = lax.broadcasted_iota(jnp.int32, (SB_BLOCK, SB_BLOCK), 0)
    col = lax.broadcasted_iota(jnp.int32, (SB_BLOCK, SB_BLOCK), 1)
    return row, col


def _split_dot(v, tri):
    hi = v.astype(BF16)
    lo = (v - hi.astype(F32)).astype(BF16)
    return _dot_nn(hi, tri) + _dot_nn(lo, tri)


def _sb_block(q, kj, scale, valid):
    z = _dot_nt(q, kj) * scale
    t = jnp.log(1.0 + jnp.exp(-jnp.abs(z)))
    ls = jnp.minimum(z, 0.0) - t
    lk = jnp.where(valid, -jnp.maximum(z, 0.0) - t, 0.0)
    return z, ls, lk


def _sb_specs(h_count, s, col0):
    q_spec = pl.BlockSpec((SB_BLOCK, HEAD_DIM), lambda h, i: (i, col0 + h))
    k_spec = pl.BlockSpec((s, HEAD_DIM), lambda h, i: (0, col0 + h_count + h))
    v_spec = pl.BlockSpec((s, HEAD_DIM), lambda h, i: (0, col0 + 2 * h_count + h))
    return q_spec, k_spec, v_spec


def _sb_fwd(qkv, n_heads, col0, *, name):
    s = qkv.shape[0]
    nq = s // SB_BLOCK
    scale = HEAD_DIM ** -0.5

    def body(q_ref, k_ref, v_ref, o_ref):
        i = pl.program_id(1)
        q = q_ref[...]
        row, col = _sb_masks()
        upper = (row > col).astype(BF16)
        diag = col < row

        def step(jj, carry):
            run, acc = carry
            j = i - jj
            off = pl.multiple_of(j * SB_BLOCK, SB_BLOCK)
            kj = k_ref[pl.ds(off, SB_BLOCK), :]
            vj = v_ref[pl.ds(off, SB_BLOCK), :]
            valid = jnp.logical_or(j < i, diag)
            _, ls, lk = _sb_block(q, kj, scale, valid)
            between = _split_dot(lk, upper) + run
            a = jnp.where(valid, jnp.exp(ls + between), 0.0)
            acc = acc + _dot_nn(a.astype(BF16), vj)
            run = run + jnp.sum(lk, axis=1, keepdims=True)
            return run, acc

        init = (jnp.zeros((SB_BLOCK, 1), F32), jnp.zeros((SB_BLOCK, HEAD_DIM), F32))
        _, acc = lax.fori_loop(0, i + 1, step, init)
        o_ref[...] = acc.astype(o_ref.dtype)

    q_spec, k_spec, v_spec = _sb_specs(n_heads, s, col0)
    return pl.pallas_call(
        body, name=name, grid=(n_heads, nq),
        in_specs=[q_spec, k_spec, v_spec],
        out_specs=pl.BlockSpec((SB_BLOCK, HEAD_DIM), lambda h, i: (i, h)),
        out_shape=jax.ShapeDtypeStruct((s, n_heads * HEAD_DIM), BF16),
        compiler_params=_cparams(("parallel", "arbitrary")),
    )(qkv, qkv, qkv)


def _sb_bwd(qkv, dy, n_heads, col0, *, name):
    s = qkv.shape[0]
    nq = s // SB_BLOCK
    scale = HEAD_DIM ** -0.5

    def body(q_ref, k_ref, v_ref, dy_ref, dq_ref, dk_ref, dv_ref, e_scr, dk_acc, dv_acc):
        i = pl.program_id(1)
        q = q_ref[...]
        dyv = dy_ref[...]
        row, col = _sb_masks()
        upper = (row > col).astype(BF16)
        lower = (row < col).astype(BF16)
        diag = col < row

        @pl.when(i == 0)
        def _():
            dk_acc[...] = jnp.zeros_like(dk_acc)
            dv_acc[...] = jnp.zeros_like(dv_acc)

        def pass1(jj, run):
            j = i - jj
            off = pl.multiple_of(j * SB_BLOCK, SB_BLOCK)
            kj = k_ref[pl.ds(off, SB_BLOCK), :]
            vj = v_ref[pl.ds(off, SB_BLOCK), :]
            valid = jnp.logical_or(j < i, diag)
            _, ls, lk = _sb_block(q, kj, scale, valid)
            between = _split_dot(lk, upper) + run
            a = jnp.where(valid, jnp.exp(ls + between), 0.0)
            da = _dot_nt(dyv, vj)
            e_scr[j] = a * da
            dv_acc[pl.ds(off, SB_BLOCK), :] += _dot_tn(a.astype(BF16), dyv)
            return run + jnp.sum(lk, axis=1, keepdims=True)

        lax.fori_loop(0, i + 1, pass1, jnp.zeros((SB_BLOCK, 1), F32))

        def pass2(j, carry):
            run, dq = carry
            off = pl.multiple_of(j * SB_BLOCK, SB_BLOCK)
            kj = k_ref[pl.ds(off, SB_BLOCK), :]
            valid = jnp.logical_or(j < i, diag)
            z = _dot_nt(q, kj) * scale
            sg = _sigmoid(z)
            e = e_scr[j]
            before = _split_dot(e, lower) + run
            dz = jnp.where(valid, e * (1.0 - sg) - sg * before, 0.0) * scale
            dzb = dz.astype(BF16)
            dq = dq + _dot_nn(dzb, kj)
            dk_acc[pl.ds(off, SB_BLOCK), :] += _dot_tn(dzb, q)
            return run + jnp.sum(e, axis=1, keepdims=True), dq

        init = (jnp.zeros((SB_BLOCK, 1), F32), jnp.zeros((SB_BLOCK, HEAD_DIM), F32))
        _, dq = lax.fori_loop(0, i + 1, pass2, init)
        dq_ref[...] = dq.astype(dq_ref.dtype)

        @pl.when(i == nq - 1)
        def _():
            dk_ref[...] = dk_acc[...].astype(dk_ref.dtype)
            dv_ref[...] = dv_acc[...].astype(dv_ref.dtype)

    q_spec, k_spec, v_spec = _sb_specs(n_heads, s, col0)
    blk = pl.BlockSpec((SB_BLOCK, HEAD_DIM), lambda h, i: (i, h))
    full = pl.BlockSpec((s, HEAD_DIM), lambda h, i: (0, h))
    sd = jax.ShapeDtypeStruct((s, n_heads * HEAD_DIM), BF16)
    return pl.pallas_call(
        body, name=name, grid=(n_heads, nq),
        in_specs=[q_spec, k_spec, v_spec, blk],
        out_specs=[blk, full, full],
        out_shape=[sd, sd, sd],
        scratch_shapes=[pltpu.VMEM((nq, SB_BLOCK, SB_BLOCK), F32),
                        pltpu.VMEM((s, HEAD_DIM), F32), pltpu.VMEM((s, HEAD_DIM), F32)],
        compiler_params=_cparams(("parallel", "arbitrary")),
    )(qkv, qkv, qkv, dy)


def _rel_index():
    qi = np.arange(CHUNK)[:, None]
    si = np.arange(BAND)[None, :]
    rel = (si - CHUNK) - (qi + LEFT_CHUNKS * CHUNK)
    return np.clip(rel, -REL_CLIP, CHUNK - 1) + REL_CLIP, np.broadcast_to(si >= CHUNK, (CHUNK, BAND))


def _band_bias(rel_bias):
    idx, seen = _rel_index()
    return jnp.where(jnp.asarray(seen)[None], rel_bias[:, jnp.asarray(idx)], NEG)


def _band_bias_grad(dbias):
    h = dbias.shape[0]
    width = BAND + CHUNK
    flipped = jnp.pad(dbias[:, ::-1, :], ((0, 0), (0, 0), (0, CHUNK)))
    skew = flipped.reshape(h, CHUNK * width)[:, :CHUNK * (width - 1)].reshape(h, CHUNK, width - 1)
    diag = jnp.sum(skew, axis=1)
    first = width - 1 - N_REL
    clipped = jnp.sum(diag[:, :first + 1], axis=1, keepdims=True)
    return jnp.concatenate([clipped, diag[:, first + 1:]], axis=1)


def _ca_load_padded(k_ref, v_ref, kp, vp, s):
    kp[pl.ds(0, PAD), :] = jnp.zeros((PAD, HEAD_DIM), kp.dtype)
    vp[pl.ds(0, PAD), :] = jnp.zeros((PAD, HEAD_DIM), vp.dtype)
    kp[pl.ds(PAD, s), :] = k_ref[...]
    vp[pl.ds(PAD, s), :] = v_ref[...]


def _ca_weights(q, kb, bias, off, scale):
    z = _dot_nt(q, kb) * scale + bias
    pos = off + lax.broadcasted_iota(jnp.int32, (CHUNK, BAND), 1)
    z = jnp.where(pos >= PAD, z, NEG)
    p = jnp.exp(z - jnp.max(z, axis=1, keepdims=True))
    return p / jnp.sum(p, axis=1, keepdims=True)


def _ca_specs(h_count, s, col0):
    q_spec = pl.BlockSpec((CHUNK, HEAD_DIM), lambda h, c: (c, col0 + h))
    k_spec = pl.BlockSpec((s, HEAD_DIM), lambda h, c: (0, col0 + h_count + h))
    v_spec = pl.BlockSpec((s, HEAD_DIM), lambda h, c: (0, col0 + 2 * h_count + h))
    b_spec = pl.BlockSpec((1, CHUNK, BAND), lambda h, c: (h, 0, 0))
    return q_spec, k_spec, v_spec, b_spec


def _ca_fwd(qkv, bias, n_heads, col0, *, name):
    s = qkv.shape[0]
    nc = s // CHUNK
    scale = HEAD_DIM ** -0.5

    def body(q_ref, k_ref, v_ref, b_ref, o_ref, kp, vp):
        c = pl.program_id(1)

        @pl.when(c == 0)
        def _():
            _ca_load_padded(k_ref, v_ref, kp, vp, s)

        off = pl.multiple_of(c * CHUNK, CHUNK)
        w = _ca_weights(q_ref[...], kp[pl.ds(off, BAND), :], b_ref[0], off, scale)
        o_ref[...] = _dot_nn(w.astype(BF16), vp[pl.ds(off, BAND), :]).astype(o_ref.dtype)

    q_spec, k_spec, v_spec, b_spec = _ca_specs(n_heads, s, col0)
    return pl.pallas_call(
        body, name=name, grid=(n_heads, nc),
        in_specs=[q_spec, k_spec, v_spec, b_spec],
        out_specs=pl.BlockSpec((CHUNK, HEAD_DIM), lambda h, c: (c, h)),
        out_shape=jax.ShapeDtypeStruct((s, n_heads * HEAD_DIM), BF16),
        scratch_shapes=[pltpu.VMEM((s + PAD, HEAD_DIM), BF16), pltpu.VMEM((s + PAD, HEAD_DIM), BF16)],
        compiler_params=_cparams(("parallel", "arbitrary")),
    )(qkv, qkv, qkv, bias)


def _ca_bwd(qkv, bias, dy, n_heads, col0, *, name):
    s = qkv.shape[0]
    nc = s // CHUNK
    scale = HEAD_DIM ** -0.5

    def body(q_ref, k_ref, v_ref, b_ref, dy_ref, dq_ref, dk_ref, dv_ref, db_ref, kp, vp, dkp, dvp):
        c = pl.program_id(1)

        @pl.when(c == 0)
        def _():
            _ca_load_padded(k_ref, v_ref, kp, vp, s)
            dkp[...] = jnp.zeros_like(dkp)
            dvp[...] = jnp.zeros_like(dvp)
            db_ref[...] = jnp.zeros_like(db_ref)

        off = pl.multiple_of(c * CHUNK, CHUNK)
        q = q_ref[...]
        dyv = dy_ref[...]
        kb = kp[pl.ds(off, BAND), :]
        w = _ca_weights(q, kb, b_ref[0], off, scale)
        dw = _dot_nt(dyv, vp[pl.ds(off, BAND), :])
        dvp[pl.ds(off, BAND), :] += _dot_tn(w.astype(BF16), dyv)
        dz = w * (dw - jnp.sum(w * dw, axis=1, keepdims=True))
        db_ref[0] += dz
        dzs = (dz * scale).astype(BF16)
        dq_ref[...] = _dot_nn(dzs, kb).astype(dq_ref.dtype)
        dkp[pl.ds(off, BAND), :] += _dot_tn(dzs, q)

        @pl.when(c == nc - 1)
        def _():
            dk_ref[...] = dkp[pl.ds(PAD, s), :].astype(dk_ref.dtype)
            dv_ref[...] = dvp[pl.ds(PAD, s), :].astype(dv_ref.dtype)

    q_spec, k_spec, v_spec, b_spec = _ca_specs(n_heads, s, col0)
    blk = pl.BlockSpec((CHUNK, HEAD_DIM), lambda h, c: (c, h))
    full = pl.BlockSpec((s, HEAD_DIM), lambda h, c: (0, h))
    sd = jax.ShapeDtypeStruct((s, n_heads * HEAD_DIM), BF16)
    return pl.pallas_call(
        body, name=name, grid=(n_heads, nc),
        in_specs=[q_spec, k_spec, v_spec, b_spec, blk],
        out_specs=[blk, full, full, b_spec],
        out_shape=[sd, sd, sd, jax.ShapeDtypeStruct((n_heads, CHUNK, BAND), F32)],
        scratch_shapes=[pltpu.VMEM((s + PAD, HEAD_DIM), BF16), pltpu.VMEM((s + PAD, HEAD_DIM), BF16),
                        pltpu.VMEM((s + PAD, HEAD_DIM), F32), pltpu.VMEM((s + PAD, HEAD_DIM), F32)],
        compiler_params=_cparams(("parallel", "arbitrary")),
    )(qkv, qkv, qkv, bias, dy)


def _local_step(x, p, target, w, small):
    d = x.shape[1]
    n_sb = w["w_sb_out"].shape[0] // HEAD_DIM
    n_ca = w["w_ca_out"].shape[0] // HEAD_DIM
    qkv_cols = 3 * HEAD_DIM * (n_sb + n_ca)
    ca_col0 = 3 * n_sb
    both = (F32, BF16)

    h1 = _rms_fwd(x, small["g_mix"], name="rms_mix")
    qkv = _mm(h1, w["w_in"], "nn", (BF16,), name="proj_qkv", n=qkv_cols)
    gates = _mm(h1, w["w_in"], "nn", (F32,), name="proj_gates", n=2 * d, b_col_off=qkv_cols)
    bias = _band_bias(small["rel_bias"])
    y_sb = _sb_fwd(qkv, n_sb, 0, name="sb_fwd")
    y_ca = _ca_fwd(qkv, bias, n_ca, ca_col0, name="ca_fwd")
    o_sb = _mm(y_sb, w["w_sb_out"], "nn", (F32,), name="sb_out")
    o_ca = _mm(y_ca, w["w_ca_out"], "nn", (F32,), name="ca_out")
    merged = _gate_merge_fwd(gates, o_sb, o_ca, name="gate_merge")
    x1 = _mm(merged, w["w_mix_out"], "nn", (F32,), name="mix_out", resid=x)
    h2 = _rms_fwd(x1, small["g_ffn"], name="rms_ffn")
    gu = _mm(h2, w["w_ffn_in"], "nn", (F32,), name="ffn_in")
    act = _swiglu_fwd(gu, name="swiglu")
    x2 = _mm(act, w["w_ffn_out"], "nn", (F32,), name="ffn_out", resid=x1)
    h3 = _rms_fwd(x2, small["g_ple"], name="rms_ple")
    t = _mm(h3, w["w_ple_gate"], "nn", (F32,), name="ple_gate")
    pe = _mm(p, w["w_ple_in"], "nn", (F32,), name="ple_in")
    x3 = _ple_fwd(x2, t, pe, name="ple_add")

    gw, gs = {}, {}
    dx3, gs["g_final"], loss = _final_loss(x3, small["g_final"], target, name="final_loss")
    dt, dpe = _ple_bwd(dx3, t, pe, name="ple_bwd")
    gw["w_ple_in"] = _mm(p, dpe, "tn", both, name="dw_ple_in")
    gw["w_ple_gate"] = _mm(h3, dt, "tn", both, name="dw_ple_gate")
    dh3 = _mm(dt, w["w_ple_gate"], "nt", (F32,), name="dh_ple")
    dx2, gs["g_ple"] = _rms_bwd(x2, small["g_ple"], dh3, dx3, name="rms_ple_bwd")
    gw["w_ffn_out"] = _mm(act, dx2, "tn", both, name="dw_ffn_out")
    dact = _mm(dx2, w["w_ffn_out"], "nt", (F32,), name="dact")
    dg_ff, du_ff = _swiglu_bwd(dact, gu, name="swiglu_bwd")
    dgu = jnp.concatenate([dg_ff, du_ff], axis=1)
    gw["w_ffn_in"] = _mm(h2, dgu, "tn", both, name="dw_ffn_in")
    dh2 = _mm(dgu, w["w_ffn_in"], "nt", (F32,), name="dh_ffn")
    dx1, gs["g_ffn"] = _rms_bwd(x1, small["g_ffn"], dh2, dx2, name="rms_ffn_bwd")
    gw["w_mix_out"] = _mm(merged, dx1, "tn", both, name="dw_mix_out")
    dmerged = _mm(dx1, w["w_mix_out"], "nt", (F32,), name="dmerged")
    dg_sb, dg_ca, do_sb, do_ca = _gate_merge_bwd(dmerged, gates, o_sb, o_ca, name="gate_merge_bwd")
    gw["w_sb_out"] = _mm(y_sb, do_sb, "tn", both, name="dw_sb_out")
    gw["w_ca_out"] = _mm(y_ca, do_ca, "tn", both, name="dw_ca_out")
    dy_sb = _mm(do_sb, w["w_sb_out"], "nt", (BF16,), name="dy_sb")
    dy_ca = _mm(do_ca, w["w_ca_out"], "nt", (BF16,), name="dy_ca")
    dq_sb, dk_sb, dv_sb = _sb_bwd(qkv, dy_sb, n_sb, 0, name="sb_bwd")
    dq_ca, dk_ca, dv_ca, dbias = _ca_bwd(qkv, bias, dy_ca, n_ca, ca_col0, name="ca_bwd")
    gs["rel_bias"] = _band_bias_grad(dbias)
    dproj = jnp.concatenate([dq_sb, dk_sb, dv_sb, dq_ca, dk_ca, dv_ca, dg_sb, dg_ca], axis=1)
    gw["w_in"] = _mm(h1, dproj, "tn", both, name="dw_in")
    dh1 = _mm(dproj, w["w_in"], "nt", (F32,), name="dh_mix")
    grad_x, gs["g_mix"] = _rms_bwd(x, small["g_mix"], dh1, dx1, name="rms_mix_bwd")
    return loss, grad_x, gw, gs


HBM = pl.BlockSpec(memory_space=pltpu.HBM)


def _position():
    x, y, c = lax.axis_index("x"), lax.axis_index("y"), lax.axis_index("c")
    chips = [(1 - x, y), (x, 1 - y), (1 - x, 1 - y)]
    return x, y, c, chips


def _aligned(v, m):
    return v if isinstance(v, int) else pl.multiple_of(v, m)


def _piece_dims(shape, axis):
    k, n = shape
    return (k // 2, n // N_CHIPS) if axis == 1 else (k // N_CHIPS // 2, n)


def _piece(ref, shape, axis, j, h):
    pr, pc = _piece_dims(shape, axis)
    if axis == 1:
        return ref.at[pl.ds(_aligned(h * pr, 16), pr), pl.ds(_aligned(j * pc, 128), pc)]
    return ref.at[pl.ds(_aligned((2 * j + h) * pr, 16), pr), :]


def _shard_half(ref, h):
    rows = ref.shape[0] // 2
    return ref.at[pl.ds(_aligned(h * rows, 16), rows), :]


def _remote(src, dst, send_sems, recv_sems, k, to):
    return pltpu.make_async_remote_copy(src_ref=src, dst_ref=dst, send_sem=send_sems.at[k],
                                        recv_sem=recv_sems.at[k], device_id=to, device_id_type=MESH)


def _all_gather(shard, axis, *, name):
    ks, ns = shard.shape
    shape = (ks, ns * N_CHIPS) if axis == 1 else (ks * N_CHIPS, ns)

    def body(shard_ref, full_ref, send_sems, recv_sems, local_sem):
        x, y, c, chips = _position()
        me = 2 * x + y
        sibling = (x, y, 1 - c)
        piece = functools.partial(_piece, full_ref, shape, axis)
        own = [pltpu.make_async_copy(_shard_half(shard_ref, h), piece(me, h), local_sem.at[h]) for h in range(2)]
        for cp in own:
            cp.start()
        first = [_remote(_shard_half(shard_ref, c), piece(me, c), send_sems, recv_sems, k, (px, py, c))
                 for k, (px, py) in enumerate(chips)]
        for cp in first:
            cp.start()
        passed = []
        for k, (px, py) in enumerate(chips):
            landed = piece(2 * px + py, c)
            _remote(landed, landed, send_sems, recv_sems, k, sibling).wait_recv()
            fwd = _remote(landed, landed, send_sems, recv_sems, 3 + k, sibling)
            fwd.start()
            passed.append(fwd)
        for k, (px, py) in enumerate(chips):
            other = piece(2 * px + py, 1 - c)
            _remote(other, other, send_sems, recv_sems, 3 + k, sibling).wait_recv()
        for cp in first + passed:
            cp.wait_send()
        for cp in own:
            cp.wait()

    return pl.pallas_call(
        body, name=name,
        in_specs=[HBM], out_specs=HBM,
        out_shape=jax.ShapeDtypeStruct(shape, shard.dtype),
        scratch_shapes=[pltpu.SemaphoreType.DMA((6,)), pltpu.SemaphoreType.DMA((6,)), pltpu.SemaphoreType.DMA((2,))],
    )(shard)


def _pair_exchange(g32, g16, axis, *, name):
    shape = g32.shape
    pr, pc = _piece_dims(shape, axis)

    def body(g32_ref, g16_ref, own_ref, sib_ref, send_sems, recv_sems, local_sems):
        x, y, c, _ = _position()
        sibling = (x, y, 1 - c)
        local = [pltpu.make_async_copy(_piece(g32_ref, shape, axis, j, c), own_ref.at[j], local_sems.at[j])
                 for j in range(N_CHIPS)]
        sends = [_remote(_piece(g16_ref, shape, axis, j, 1 - c), sib_ref.at[j], send_sems, recv_sems, j, sibling)
                 for j in range(N_CHIPS)]
        for cp in local + sends:
            cp.start()
        for cp in sends:
            cp.wait()
        for cp in local:
            cp.wait()

    return pl.pallas_call(
        body, name=name,
        in_specs=[HBM, HBM], out_specs=[HBM, HBM],
        out_shape=[jax.ShapeDtypeStruct((N_CHIPS, pr, pc), F32), jax.ShapeDtypeStruct((N_CHIPS, pr, pc), BF16)],
        scratch_shapes=[pltpu.SemaphoreType.DMA((N_CHIPS,)), pltpu.SemaphoreType.DMA((N_CHIPS,)),
                        pltpu.SemaphoreType.DMA((N_CHIPS,))],
    )(g32, g16)


def _pair_add(own, sib, *, name):
    _, pr, pc = own.shape
    tr, tc = _pick(pr, (256, 128, 64, 32, 16)), _pick(pc, (1024, 512, 256, 128))

    def body(a_ref, b_ref, o32_ref, o16_ref):
        r = a_ref[...] + b_ref[...].astype(F32)
        o32_ref[...] = r
        o16_ref[...] = r.astype(o16_ref.dtype)

    blk = ((1, tr, tc), lambda j, i, k: (j, i, k))
    return _ew(body, [own, sib], [blk, blk],
               [jax.ShapeDtypeStruct(own.shape, F32), jax.ShapeDtypeStruct(own.shape, BF16)], [blk, blk],
               (N_CHIPS, pr // tr, pc // tc), name=name)


def _chip_exchange(s32, s16, *, name):
    _, pr, pc = s32.shape

    def body(s32_ref, s16_ref, mine_ref, recv_ref, send_sems, recv_sems, local_sem):
        x, y, c, chips = _position()
        local = pltpu.make_async_copy(s32_ref.at[2 * x + y], mine_ref, local_sem)
        local.start()
        sends = [_remote(s16_ref.at[2 * px + py], recv_ref.at[k], send_sems, recv_sems, k, (px, py, c))
                 for k, (px, py) in enumerate(chips)]
        for cp in sends:
            cp.start()
        for cp in sends:
            cp.wait()
        local.wait()

    return pl.pallas_call(
        body, name=name,
        in_specs=[HBM, HBM], out_specs=[HBM, HBM],
        out_shape=[jax.ShapeDtypeStruct((pr, pc), F32), jax.ShapeDtypeStruct((3, pr, pc), BF16)],
        scratch_shapes=[pltpu.SemaphoreType.DMA((3,)), pltpu.SemaphoreType.DMA((3,)), pltpu.SemaphoreType.DMA],
    )(s32, s16)


def _chip_sum(mine, recv, *, name):
    pr, pc = mine.shape
    tr, tc = _pick(pr, (256, 128, 64, 32, 16)), _pick(pc, (1024, 512, 256, 128))

    def body(m_ref, r_ref, o_ref):
        o_ref[...] = ((m_ref[...] + r_ref[0].astype(F32)) + r_ref[1].astype(F32)) + r_ref[2].astype(F32)

    return _ew(body, [mine, recv], [((tr, tc), lambda i, k: (i, k)), ((3, tr, tc), lambda i, k: (0, i, k))],
               [jax.ShapeDtypeStruct((pr, pc), F32)], [((tr, tc), lambda i, k: (i, k))],
               (pr // tr, pc // tc), name=name)[0]


def _pair_share(half, *, name):
    pr, pc = half.shape

    def body(half_ref, out_ref, send_sems, recv_sems, local_sem):
        x, y, c, _ = _position()
        local = pltpu.make_async_copy(half_ref, _shard_half(out_ref, c), local_sem)
        local.start()
        send = _remote(half_ref, _shard_half(out_ref, c), send_sems, recv_sems, 0, (x, y, 1 - c))
        send.start()
        other = _shard_half(out_ref, 1 - c)
        _remote(other, other, send_sems, recv_sems, 0, (x, y, 1 - c)).wait_recv()
        send.wait_send()
        local.wait()

    return pl.pallas_call(
        body, name=name,
        in_specs=[HBM], out_specs=HBM,
        out_shape=jax.ShapeDtypeStruct((2 * pr, pc), F32),
        scratch_shapes=[pltpu.SemaphoreType.DMA((1,)), pltpu.SemaphoreType.DMA((1,)), pltpu.SemaphoreType.DMA],
    )(half)


def _reduce_scatter(g32, g16, axis, *, name):
    own, sib = _pair_exchange(g32, g16, axis, name=name + "_pair")
    s32, s16 = _pair_add(own, sib, name=name + "_pair_add")
    mine, recv = _chip_exchange(s32, s16, name=name + "_chips")
    return _pair_share(_chip_sum(mine, recv, name=name + "_chip_sum"), name=name + "_share")


def _small_all_reduce(vec, *, name):
    r = vec.shape[0]

    def body(vec_ref, out_ref, slots, send_sems, recv_sems):
        x, y, c, _ = _position()
        me = 4 * x + 2 * y + c
        slots[me] = vec_ref[...]
        sends = []
        for k in range(1, 8):
            to = (x ^ (k >> 2), y ^ ((k >> 1) & 1), c ^ (k & 1))
            cp = _remote(slots.at[me], slots.at[me], send_sems, recv_sems, k - 1, to)
            cp.start()
            sends.append(cp)
        for k in range(1, 8):
            frm = 4 * (x ^ (k >> 2)) + 2 * (y ^ ((k >> 1) & 1)) + (c ^ (k & 1))
            _remote(slots.at[frm], slots.at[frm], send_sems, recv_sems, k - 1, (x, y, c)).wait_recv()
        for cp in sends:
            cp.wait_send()
        total = slots[0]
        for d in range(1, 8):
            total = total + slots[d]
        out_ref[...] = total

    return pl.pallas_call(
        body, name=name,
        in_specs=[pl.BlockSpec(memory_space=pltpu.VMEM)], out_specs=pl.BlockSpec(memory_space=pltpu.VMEM),
        out_shape=jax.ShapeDtypeStruct((r, 128), F32),
        scratch_shapes=[pltpu.VMEM((8, r, 128), F32), pltpu.SemaphoreType.DMA((7,)), pltpu.SemaphoreType.DMA((7,))],
    )(vec)


def _cast_bf16(w, *, name):
    r, c = w.shape
    tr, tc = _pick(r, (512, 256, 128, 64)), _pick(c, (1024, 512, 256, 128))

    def body(w_ref, o_ref):
        o_ref[...] = w_ref[...].astype(o_ref.dtype)

    blk = ((tr, tc), lambda i, j: (i, j))
    return _ew(body, [w], [blk], [jax.ShapeDtypeStruct((r, c), BF16)], [blk], (r // tr, c // tc), name=name)[0]


def _adamw(w, g, m, v, *, name):
    r, c = w.shape
    tr, tc = _pick(r, (256, 128, 64, 32, 16, 8)), _pick(c, (1024, 512, 256, 128))

    def body(w_ref, g_ref, m_ref, v_ref, d_ref, nm_ref, nv_ref):
        gv = g_ref[...]
        nm = ADAM_B1 * m_ref[...] + (1.0 - ADAM_B1) * gv
        nv = ADAM_B2 * v_ref[...] + (1.0 - ADAM_B2) * (gv * gv)
        m_hat = nm / (1.0 - ADAM_B1 ** ADAM_STEP)
        v_hat = nv / (1.0 - ADAM_B2 ** ADAM_STEP)
        d_ref[...] = -ADAM_LR * (m_hat / (jnp.sqrt(v_hat) + ADAM_EPS) + ADAM_WD * w_ref[...])
        nm_ref[...] = nm
        nv_ref[...] = nv

    blk = ((tr, tc), lambda i, j: (i, j))
    sd = jax.ShapeDtypeStruct((r, c), F32)
    return _ew(body, [w, g, m, v], [blk] * 4, [sd, sd, sd], [blk] * 3, (r // tr, c // tc), name=name)


BIG = (("w_in", 1), ("w_sb_out", 1), ("w_ca_out", 1), ("w_mix_out", 0), ("w_ffn_in", 1), ("w_ffn_out", 0),
       ("w_ple_in", 1), ("w_ple_gate", 0))
SMALL = ("rel_bias", "g_mix", "g_ffn", "g_ple", "g_final")
ORDER = ("w_in", "w_sb_out", "w_ca_out", "w_mix_out", "rel_bias", "g_mix", "g_ffn", "g_ple", "g_final",
         "w_ffn_in", "w_ffn_out", "w_ple_in", "w_ple_gate")


def _pack(parts):
    flat = jnp.concatenate([a.reshape(-1) for a in parts])
    rows = -(-flat.shape[0] // 1024) * 8
    return jnp.pad(flat, (0, rows * 128 - flat.shape[0])).reshape(rows, 128)


def _unpack(packed, like):
    flat, out, at = packed.reshape(-1), [], 0
    for a in like:
        out.append(flat[at:at + a.size].reshape(a.shape))
        at += a.size
    return out


def kernel(x, p, w_in, w_sb_out, w_ca_out, w_mix_out, rel_bias, g_mix, g_ffn, g_ple, g_final, w_ffn_in, w_ffn_out, w_ple_in, w_ple_gate, loss_target, m_w_in, m_w_sb_out, m_w_ca_out, m_w_mix_out, m_rel_bias, m_g_mix, m_g_ffn, m_g_ple, m_g_final, m_w_ffn_in, m_w_ffn_out, m_w_ple_in, m_w_ple_gate, v_w_in, v_w_sb_out, v_w_ca_out, v_w_mix_out, v_rel_bias, v_g_mix, v_g_ffn, v_g_ple, v_g_final, v_w_ffn_in, v_w_ffn_out, v_w_ple_in, v_w_ple_gate):
    weights = dict(w_in=w_in, w_sb_out=w_sb_out, w_ca_out=w_ca_out, w_mix_out=w_mix_out, rel_bias=rel_bias,
                   g_mix=g_mix, g_ffn=g_ffn, g_ple=g_ple, g_final=g_final, w_ffn_in=w_ffn_in,
                   w_ffn_out=w_ffn_out, w_ple_in=w_ple_in, w_ple_gate=w_ple_gate)
    m_in = dict(w_in=m_w_in, w_sb_out=m_w_sb_out, w_ca_out=m_w_ca_out, w_mix_out=m_w_mix_out, rel_bias=m_rel_bias,
                g_mix=m_g_mix, g_ffn=m_g_ffn, g_ple=m_g_ple, g_final=m_g_final, w_ffn_in=m_w_ffn_in,
                w_ffn_out=m_w_ffn_out, w_ple_in=m_w_ple_in, w_ple_gate=m_w_ple_gate)
    v_in = dict(w_in=v_w_in, w_sb_out=v_w_sb_out, w_ca_out=v_w_ca_out, w_mix_out=v_w_mix_out, rel_bias=v_rel_bias,
                g_mix=v_g_mix, g_ffn=v_g_ffn, g_ple=v_g_ple, g_final=v_g_final, w_ffn_in=v_w_ffn_in,
                w_ffn_out=v_w_ffn_out, w_ple_in=v_w_ple_in, w_ple_gate=v_w_ple_gate)

    full = {n: _all_gather(_cast_bf16(weights[n][0], name="cast_" + n), axis, name="gather_" + n) for n, axis in BIG}
    small = dict(rel_bias=rel_bias[0], g_mix=g_mix, g_ffn=g_ffn, g_ple=g_ple, g_final=g_final.reshape(1, -1))
    loss, grad_x, gw, gs = _local_step(x[0], p[0, 0], loss_target[0], full, small)

    grads, delta, new_m, new_v = {}, {}, {}, {}
    for n, axis in BIG:
        g = _reduce_scatter(gw[n][0], gw[n][1], axis, name="rs_" + n)
        d, nm, nv = _adamw(weights[n][0], g, m_in[n][0], v_in[n][0], name="adamw_" + n)
        grads[n], delta[n], new_m[n], new_v[n] = g[None], d[None], nm[None], nv[None]

    like = [weights[n] for n in SMALL]
    reduced = _small_all_reduce(_pack([gs[n] for n in SMALL] + [loss[:, :1]]), name="small_all_reduce")
    g_small = _unpack(reduced, like + [loss[:, :1]])
    total_loss = g_small[-1].reshape(())
    g_packed = _pack(g_small[:-1])
    d_s, m_s, v_s = _adamw(_pack(like), g_packed, _pack([m_in[n] for n in SMALL]), _pack([v_in[n] for n in SMALL]),
                           name="adamw_small")
    for n, g, d, nm, nv in zip(SMALL, g_small[:-1], _unpack(d_s, like), _unpack(m_s, like), _unpack(v_s, like)):
        grads[n], delta[n], new_m[n], new_v[n] = g, d, nm, nv

    return (total_loss, grad_x[None], *[grads[n] for n in ORDER], *[delta[n] for n in ORDER],
            *[new_m[n] for n in ORDER], *[new_v[n] for n in ORDER])
```

```python
import functools
import math

import jax
import jax.numpy as jnp
import numpy as np
from jax import lax
from jax.experimental import pallas as pl
from jax.experimental.pallas import tpu as pltpu

F32 = jnp.float32
BF16 = jnp.bfloat16

HEAD_DIM = 128
CHUNK = 64
LEFT_CHUNKS = 8
REL_CLIP = 128
N_REL = REL_CLIP + CHUNK
BAND = (LEFT_CHUNKS + 2) * CHUNK
PAD = (LEFT_CHUNKS + 1) * CHUNK
SB_BLOCK = 128
EPS = 1e-6
NEG = -1e30

ADAM_LR = 0.001
ADAM_B1 = 0.9
ADAM_B2 = 0.999
ADAM_EPS = 1e-08
ADAM_WD = 0.01
ADAM_STEP = 10

VMEM_LIMIT = 48 * 1024 * 1024
MESH = pl.DeviceIdType.MESH
N_CHIPS = 4


def _pick(dim, prefs):
    for t in prefs:
        if dim % t == 0:
            return t
    raise ValueError(f"no tile for {dim}")


def _cparams(sem=None):
    return pltpu.CompilerParams(dimension_semantics=sem, vmem_limit_bytes=VMEM_LIMIT)


def _sigmoid(v):
    return 1.0 / (1.0 + jnp.exp(-v))


def _dot(a, b, dims):
    return lax.dot_general(a, b, (dims, ((), ())), preferred_element_type=F32)


def _dot_nn(a, b):
    return _dot(a, b, ((1,), (0,)))


def _dot_nt(a, b):
    return _dot(a, b, ((1,), (1,)))


def _dot_tn(a, b):
    return _dot(a, b, ((0,), (0,)))


def _mm(a, b, mode, out_dtypes, *, name, n=None, b_col_off=0, resid=None):
    if mode == "nn":
        m, k = a.shape
        n = b.shape[1] if n is None else n
    elif mode == "nt":
        m, k = a.shape
        n = b.shape[0]
    else:
        k, m = a.shape
        n = b.shape[1]
    tm = _pick(m, (512, 256, 128))
    tn = _pick(math.gcd(n, b_col_off) if b_col_off else n, (1024, 512, 256, 128))
    tk = _pick(k, (1024, 512, 256, 128))
    nk = k // tk
    boff = b_col_off // tn
    n_out = len(out_dtypes)
    has_resid = resid is not None

    def body(*refs):
        a_ref, b_ref = refs[0], refs[1]
        r_ref = refs[2] if has_resid else None
        o_refs = refs[2 + has_resid: 2 + has_resid + n_out]
        acc_ref = refs[-1]
        kk = pl.program_id(2)

        @pl.when(kk == 0)
        def _():
            acc_ref[...] = jnp.zeros_like(acc_ref)

        av = a_ref[...].astype(BF16)
        bv = b_ref[...].astype(BF16)
        if mode == "nn":
            acc_ref[...] += _dot_nn(av, bv)
        elif mode == "nt":
            acc_ref[...] += _dot_nt(av, bv)
        else:
            acc_ref[...] += _dot_tn(av, bv)

        @pl.when(kk == nk - 1)
        def _():
            r = acc_ref[...]
            if has_resid:
                r = r + r_ref[...]
            for o_ref in o_refs:
                o_ref[...] = r.astype(o_ref.dtype)

    if mode == "nn":
        a_spec = pl.BlockSpec((tm, tk), lambda i, j, kk: (i, kk))
        b_spec = pl.BlockSpec((tk, tn), lambda i, j, kk: (kk, j + boff))
    elif mode == "nt":
        a_spec = pl.BlockSpec((tm, tk), lambda i, j, kk: (i, kk))
        b_spec = pl.BlockSpec((tn, tk), lambda i, j, kk: (j, kk))
    else:
        a_spec = pl.BlockSpec((tk, tm), lambda i, j, kk: (kk, i))
        b_spec = pl.BlockSpec((tk, tn), lambda i, j, kk: (kk, j))
    o_spec = pl.BlockSpec((tm, tn), lambda i, j, kk: (i, j))
    in_specs = [a_spec, b_spec] + ([o_spec] if has_resid else [])
    args = [a, b] + ([resid] if has_resid else [])
    outs = pl.pallas_call(
        body, name=name,
        grid=(m // tm, n // tn, nk),
        in_specs=in_specs,
        out_specs=[o_spec] * n_out,
        out_shape=[jax.ShapeDtypeStruct((m, n), dt) for dt in out_dtypes],
        scratch_shapes=[pltpu.VMEM((tm, tn), F32)],
        compiler_params=_cparams(("parallel", "parallel", "arbitrary")),
    )(*args)
    return outs[0] if n_out == 1 else tuple(outs)


def _row_tile(s):
    return _pick(s, (256, 128))


def _rms_fwd(x, g, *, name):
    s, d = x.shape
    tr = _row_tile(s)

    def body(x_ref, g_ref, o_ref):
        xv = x_ref[...]
        r = lax.rsqrt(jnp.mean(xv * xv, axis=1, keepdims=True) + EPS)
        o_ref[...] = (xv * r * g_ref[...]).astype(o_ref.dtype)

    return pl.pallas_call(
        body, name=name, grid=(s // tr,),
        in_specs=[pl.BlockSpec((tr, d), lambda i: (i, 0)), pl.BlockSpec((1, d), lambda i: (0, 0))],
        out_specs=pl.BlockSpec((tr, d), lambda i: (i, 0)),
        out_shape=jax.ShapeDtypeStruct((s, d), BF16),
        compiler_params=_cparams(("parallel",)),
    )(x, g)


def _rms_bwd(x, g, dh, dres, *, name):
    s, d = x.shape
    tr = _row_tile(s)

    def body(x_ref, g_ref, dh_ref, dres_ref, dx_ref, dg_ref):
        i = pl.program_id(0)
        xv = x_ref[...]
        r = lax.rsqrt(jnp.mean(xv * xv, axis=1, keepdims=True) + EPS)
        xhat = xv * r
        dhv = dh_ref[...]
        dxhat = dhv * g_ref[...]
        proj = jnp.mean(dxhat * xhat, axis=1, keepdims=True)
        dx_ref[...] = dres_ref[...] + r * (dxhat - xhat * proj)

        @pl.when(i == 0)
        def _():
            dg_ref[...] = jnp.zeros_like(dg_ref)

        dg_ref[...] += jnp.sum(dhv * xhat, axis=0, keepdims=True)

    row = pl.BlockSpec((tr, d), lambda i: (i, 0))
    vec = pl.BlockSpec((1, d), lambda i: (0, 0))
    return pl.pallas_call(
        body, name=name, grid=(s // tr,),
        in_specs=[row, vec, row, row],
        out_specs=[row, vec],
        out_shape=[jax.ShapeDtypeStruct((s, d), F32), jax.ShapeDtypeStruct((1, d), F32)],
        compiler_params=_cparams(("arbitrary",)),
    )(x, g, dh, dres)


def _final_loss(x, g, target, *, name):
    s, d = x.shape
    tr = _row_tile(s)

    def body(x_ref, g_ref, t_ref, dx_ref, dg_ref, loss_ref):
        i = pl.program_id(0)
        xv = x_ref[...]
        gv = g_ref[...]
        r = lax.rsqrt(jnp.mean(xv * xv, axis=1, keepdims=True) + EPS)
        xhat = xv * r
        err = xhat * gv - t_ref[...]
        dy = err * (1.0 / d)
        dxhat = dy * gv
        proj = jnp.mean(dxhat * xhat, axis=1, keepdims=True)
        dx_ref[...] = r * (dxhat - xhat * proj)

        @pl.when(i == 0)
        def _():
            dg_ref[...] = jnp.zeros_like(dg_ref)
            loss_ref[...] = jnp.zeros_like(loss_ref)

        dg_ref[...] += jnp.sum(dy * xhat, axis=0, keepdims=True)
        part = 0.5 * jnp.sum(jnp.mean(err * err, axis=1, keepdims=True), axis=0, keepdims=True)
        loss_ref[...] += jnp.broadcast_to(part, loss_ref.shape)

    row = pl.BlockSpec((tr, d), lambda i: (i, 0))
    vec = pl.BlockSpec((1, d), lambda i: (0, 0))
    return pl.pallas_call(
        body, name=name, grid=(s // tr,),
        in_specs=[row, vec, row],
        out_specs=[row, vec, pl.BlockSpec((1, 128), lambda i: (0, 0))],
        out_shape=[jax.ShapeDtypeStruct((s, d), F32), jax.ShapeDtypeStruct((1, d), F32),
                   jax.ShapeDtypeStruct((1, 128), F32)],
        compiler_params=_cparams(("arbitrary",)),
    )(x, g, target)


def _ew(body, ins, in_blocks, outs, out_blocks, grid, *, name):
    return pl.pallas_call(
        body, name=name, grid=grid,
        in_specs=[pl.BlockSpec(bs, im) for bs, im in in_blocks],
        out_specs=[pl.BlockSpec(bs, im) for bs, im in out_blocks],
        out_shape=outs,
        compiler_params=_cparams(("parallel",) * len(grid)),
    )(*ins)


def _gate_merge_fwd(gates, o_sb, o_ca, *, name):
    s, d = o_sb.shape
    tr, tc = _row_tile(s), _pick(d, (1024, 512, 256, 128))
    nc = d // tc

    def body(gs_ref, gc_ref, os_ref, oc_ref, m_ref):
        m = _sigmoid(gs_ref[...]) * os_ref[...] + _sigmoid(gc_ref[...]) * oc_ref[...]
        m_ref[...] = m.astype(m_ref.dtype)

    blk = ((tr, tc), lambda i, j: (i, j))
    return _ew(body, [gates, gates, o_sb, o_ca],
               [blk, ((tr, tc), lambda i, j: (i, j + nc)), blk, blk],
               [jax.ShapeDtypeStruct((s, d), BF16)], [blk], (s // tr, nc), name=name)[0]


def _gate_merge_bwd(dmerged, gates, o_sb, o_ca, *, name):
    s, d = o_sb.shape
    tr, tc = _row_tile(s), _pick(d, (1024, 512, 256, 128))
    nc = d // tc

    def body(dm_ref, gs_ref, gc_ref, os_ref, oc_ref, dgs_ref, dgc_ref, dos_ref, doc_ref):
        dm = dm_ref[...]
        ss = _sigmoid(gs_ref[...])
        sc = _sigmoid(gc_ref[...])
        dgs_ref[...] = (dm * os_ref[...] * ss * (1.0 - ss)).astype(dgs_ref.dtype)
        dgc_ref[...] = (dm * oc_ref[...] * sc * (1.0 - sc)).astype(dgc_ref.dtype)
        dos_ref[...] = (dm * ss).astype(dos_ref.dtype)
        doc_ref[...] = (dm * sc).astype(doc_ref.dtype)

    blk = ((tr, tc), lambda i, j: (i, j))
    sd = jax.ShapeDtypeStruct((s, d), BF16)
    return _ew(body, [dmerged, gates, gates, o_sb, o_ca],
               [blk, blk, ((tr, tc), lambda i, j: (i, j + nc)), blk, blk],
               [sd, sd, sd, sd], [blk, blk, blk, blk], (s // tr, nc), name=name)


def _swiglu_fwd(gu, *, name):
    s, f2 = gu.shape
    f = f2 // 2
    tr, tc = _row_tile(s), _pick(f, (512, 256, 128))
    nc = f // tc

    def body(g_ref, u_ref, a_ref):
        gv = g_ref[...]
        a_ref[...] = (gv * _sigmoid(gv) * u_ref[...]).astype(a_ref.dtype)

    blk = ((tr, tc), lambda i, j: (i, j))
    return _ew(body, [gu, gu], [blk, ((tr, tc), lambda i, j: (i, j + nc))],
               [jax.ShapeDtypeStruct((s, f), BF16)], [blk], (s // tr, nc), name=name)[0]


def _swiglu_bwd(dact, gu, *, name):
    s, f2 = gu.shape
    f = f2 // 2
    tr, tc = _row_tile(s), _pick(f, (512, 256, 128))
    nc = f // tc

    def body(da_ref, g_ref, u_ref, dg_ref, du_ref):
        da = da_ref[...]
        gv = g_ref[...]
        sg = _sigmoid(gv)
        dg_ref[...] = (da * u_ref[...] * sg * (1.0 + gv * (1.0 - sg))).astype(dg_ref.dtype)
        du_ref[...] = (da * gv * sg).astype(du_ref.dtype)

    blk = ((tr, tc), lambda i, j: (i, j))
    hi = ((tr, tc), lambda i, j: (i, j + nc))
    sd = jax.ShapeDtypeStruct((s, f), BF16)
    dg, du = _ew(body, [dact, gu, gu], [blk, blk, hi], [sd, sd], [blk, blk], (s // tr, nc), name=name)
    return dg, du


def _ple_fwd(x, t, pe, *, name):
    s, d = x.shape
    tr, tc = _row_tile(s), _pick(d, (1024, 512, 256, 128))

    def body(x_ref, t_ref, p_ref, o_ref):
        o_ref[...] = x_ref[...] + _sigmoid(t_ref[...]) * p_ref[...]

    blk = ((tr, tc), lambda i, j: (i, j))
    return _ew(body, [x, t, pe], [blk, blk, blk],
               [jax.ShapeDtypeStruct((s, d), F32)], [blk], (s // tr, d // tc), name=name)[0]


def _ple_bwd(dx, t, pe, *, name):
    s, d = dx.shape
    tr, tc = _row_tile(s), _pick(d, (1024, 512, 256, 128))

    def body(dx_ref, t_ref, p_ref, dt_ref, dp_ref):
        dxv = dx_ref[...]
        sg = _sigmoid(t_ref[...])
        dt_ref[...] = (dxv * p_ref[...] * sg * (1.0 - sg)).astype(dt_ref.dtype)
        dp_ref[...] = (dxv * sg).astype(dp_ref.dtype)

    blk = ((tr, tc), lambda i, j: (i, j))
    sd = jax.ShapeDtypeStruct((s, d), BF16)
    return _ew(body, [dx, t, pe], [blk, blk, blk], [sd, sd], [blk, blk], (s // tr, d // tc), name=name)


def _sb_masks():
    row = lax.broadcasted_iota(jnp.int32, (SB_BLOCK, SB_BLOCK), 0)
    col = lax.broadcasted_iota(jnp.int32, (SB_BLOCK, SB_BLOCK), 1)
    return row, col


def _split_dot(v, tri):
    hi = v.astype(BF16)
    lo = (v - hi.astype(F32)).astype(BF16)
    return _dot_nn(hi, tri) + _dot_nn(lo, tri)


def _sb_block(q, kj, scale, valid):
    z = _dot_nt(q, kj) * scale
    t = jnp.log(1.0 + jnp.exp(-jnp.abs(z)))
    ls = jnp.minimum(z, 0.0) - t
    lk = jnp.where(valid, -jnp.maximum(z, 0.0) - t, 0.0)
    return z, ls, lk


def _sb_specs(h_count, s, col0):
    q_spec = pl.BlockSpec((SB_BLOCK, HEAD_DIM), lambda h, i: (i, col0 + h))
    k_spec = pl.BlockSpec((s, HEAD_DIM), lambda h, i: (0, col0 + h_count + h))
    v_spec = pl.BlockSpec((s, HEAD_DIM), lambda h, i: (0, col0 + 2 * h_count + h))
    return q_spec, k_spec, v_spec


def _sb_fwd(qkv, n_heads, col0, *, name):
    s = qkv.shape[0]
    nq = s // SB_BLOCK
    scale = HEAD_DIM ** -0.5

    def body(q_ref, k_ref, v_ref, o_ref):
        i = pl.program_id(1)
        q = q_ref[...]
        row, col = _sb_masks()
        upper = (row > col).astype(BF16)
        diag = col < row

        def step(jj, carry):
            run, acc = carry
            j = i - jj
            off = pl.multiple_of(j * SB_BLOCK, SB_BLOCK)
            kj = k_ref[pl.ds(off, SB_BLOCK), :]
            vj = v_ref[pl.ds(off, SB_BLOCK), :]
            valid = jnp.logical_or(j < i, diag)
            _, ls, lk = _sb_block(q, kj, scale, valid)
            between = _split_dot(lk, upper) + run
            a = jnp.where(valid, jnp.exp(ls + between), 0.0)
            acc = acc + _dot_nn(a.astype(BF16), vj)
            run = run + jnp.sum(lk, axis=1, keepdims=True)
            return run, acc

        init = (jnp.zeros((SB_BLOCK, 1), F32), jnp.zeros((SB_BLOCK, HEAD_DIM), F32))
        _, acc = lax.fori_loop(0, i + 1, step, init)
        o_ref[...] = acc.astype(o_ref.dtype)

    q_spec, k_spec, v_spec = _sb_specs(n_heads, s, col0)
    return pl.pallas_call(
        body, name=name, grid=(n_heads, nq),
        in_specs=[q_spec, k_spec, v_spec],
        out_specs=pl.BlockSpec((SB_BLOCK, HEAD_DIM), lambda h, i: (i, h)),
        out_shape=jax.ShapeDtypeStruct((s, n_heads * HEAD_DIM), BF16),
        compiler_params=_cparams(("parallel", "arbitrary")),
    )(qkv, qkv, qkv)


def _sb_bwd(qkv, dy, n_heads, col0, *, name):
    s = qkv.shape[0]
    nq = s // SB_BLOCK
    scale = HEAD_DIM ** -0.5

    def body(q_ref, k_ref, v_ref, dy_ref, dq_ref, dk_ref, dv_ref, e_scr, dk_acc, dv_acc):
        i = pl.program_id(1)
        q = q_ref[...]
        dyv = dy_ref[...]
        row, col = _sb_masks()
        upper = (row > col).astype(BF16)
        lower = (row < col).astype(BF16)
        diag = col < row

        @pl.when(i == 0)
        def _():
            dk_acc[...] = jnp.zeros_like(dk_acc)
            dv_acc[...] = jnp.zeros_like(dv_acc)

        def pass1(jj, run):
            j = i - jj
            off = pl.multiple_of(j * SB_BLOCK, SB_BLOCK)
            kj = k_ref[pl.ds(off, SB_BLOCK), :]
            vj = v_ref[pl.ds(off, SB_BLOCK), :]
            valid = jnp.logical_or(j < i, diag)
            _, ls, lk = _sb_block(q, kj, scale, valid)
            between = _split_dot(lk, upper) + run
            a = jnp.where(valid, jnp.exp(ls + between), 0.0)
            da = _dot_nt(dyv, vj)
            e_scr[j] = a * da
            dv_acc[pl.ds(off, SB_BLOCK), :] += _dot_tn(a.astype(BF16), dyv)
            return run + jnp.sum(lk, axis=1, keepdims=True)

        lax.fori_loop(0, i + 1, pass1, jnp.zeros((SB_BLOCK, 1), F32))

        def pass2(j, carry):
            run, dq = carry
            off = pl.multiple_of(j * SB_BLOCK, SB_BLOCK)
            kj = k_ref[pl.ds(off, SB_BLOCK), :]
            valid = jnp.logical_or(j < i, diag)
            z = _dot_nt(q, kj) * scale
            sg = _sigmoid(z)
            e = e_scr[j]
            before = _split_dot(e, lower) + run
            dz = jnp.where(valid, e * (1.0 - sg) - sg * before, 0.0) * scale
            dzb = dz.astype(BF16)
            dq = dq + _dot_nn(dzb, kj)
            dk_acc[pl.ds(off, SB_BLOCK), :] += _dot_tn(dzb, q)
            return run + jnp.sum(e, axis=1, keepdims=True), dq

        init = (jnp.zeros((SB_BLOCK, 1), F32), jnp.zeros((SB_BLOCK, HEAD_DIM), F32))
        _, dq = lax.fori_loop(0, i + 1, pass2, init)
        dq_ref[...] = dq.astype(dq_ref.dtype)

        @pl.when(i == nq - 1)
        def _():
            dk_ref[...] = dk_acc[...].astype(dk_ref.dtype)
            dv_ref[...] = dv_acc[...].astype(dv_ref.dtype)

    q_spec, k_spec, v_spec = _sb_specs(n_heads, s, col0)
    blk = pl.BlockSpec((SB_BLOCK, HEAD_DIM), lambda h, i: (i, h))
    full = pl.BlockSpec((s, HEAD_DIM), lambda h, i: (0, h))
    sd = jax.ShapeDtypeStruct((s, n_heads * HEAD_DIM), BF16)
    return pl.pallas_call(
        body, name=name, grid=(n_heads, nq),
        in_specs=[q_spec, k_spec, v_spec, blk],
        out_specs=[blk, full, full],
        out_shape=[sd, sd, sd],
        scratch_shapes=[pltpu.VMEM((nq, SB_BLOCK, SB_BLOCK), F32),
                        pltpu.VMEM((s, HEAD_DIM), F32), pltpu.VMEM((s, HEAD_DIM), F32)],
        compiler_params=_cparams(("parallel", "arbitrary")),
    )(qkv, qkv, qkv, dy)


def _rel_index():
    qi = np.arange(CHUNK)[:, None]
    si = np.arange(BAND)[None, :]
    rel = (si - CHUNK) - (qi + LEFT_CHUNKS * CHUNK)
    return np.clip(rel, -REL_CLIP, CHUNK - 1) + REL_CLIP, np.broadcast_to(si >= CHUNK, (CHUNK, BAND))


def _band_bias(rel_bias):
    idx, seen = _rel_index()
    return jnp.where(jnp.asarray(seen)[None], rel_bias[:, jnp.asarray(idx)], NEG)


def _band_bias_grad(dbias):
    h = dbias.shape[0]
    width = BAND + CHUNK
    flipped = jnp.pad(dbias[:, ::-1, :], ((0, 0), (0, 0), (0, CHUNK)))
    skew = flipped.reshape(h, CHUNK * width)[:, :CHUNK * (width - 1)].reshape(h, CHUNK, width - 1)
    diag = jnp.sum(skew, axis=1)
    first = width - 1 - N_REL
    clipped = jnp.sum(diag[:, :first + 1], axis=1, keepdims=True)
    return jnp.concatenate([clipped, diag[:, first + 1:]], axis=1)


def _ca_load_padded(k_ref, v_ref, kp, vp, s):
    kp[pl.ds(0, PAD), :] = jnp.zeros((PAD, HEAD_DIM), kp.dtype)
    vp[pl.ds(0, PAD), :] = jnp.zeros((PAD, HEAD_DIM), vp.dtype)
    kp[pl.ds(PAD, s), :] = k_ref[...]
    vp[pl.ds(PAD, s), :] = v_ref[...]


def _ca_weights(q, kb, bias, off, scale):
    z = _dot_nt(q, kb) * scale + bias
    pos = off + lax.broadcasted_iota(jnp.int32, (CHUNK, BAND), 1)
    z = jnp.where(pos >= PAD, z, NEG)
    p = jnp.exp(z - jnp.max(z, axis=1, keepdims=True))
    return p / jnp.sum(p, axis=1, keepdims=True)


def _ca_specs(h_count, s, col0):
    q_spec = pl.BlockSpec((CHUNK, HEAD_DIM), lambda h, c: (c, col0 + h))
    k_spec = pl.BlockSpec((s, HEAD_DIM), lambda h, c: (0, col0 + h_count + h))
    v_spec = pl.BlockSpec((s, HEAD_DIM), lambda h, c: (0, col0 + 2 * h_count + h))
    b_spec = pl.BlockSpec((1, CHUNK, BAND), lambda h, c: (h, 0, 0))
    return q_spec, k_spec, v_spec, b_spec


def _ca_fwd(qkv, bias, n_heads, col0, *, name):
    s = qkv.shape[0]
    nc = s // CHUNK
    scale = HEAD_DIM ** -0.5

    def body(q_ref, k_ref, v_ref, b_ref, o_ref, kp, vp):
        c = pl.program_id(1)

        @pl.when(c == 0)
        def _():
            _ca_load_padded(k_ref, v_ref, kp, vp, s)

        off = pl.multiple_of(c * CHUNK, CHUNK)
        w = _ca_weights(q_ref[...], kp[pl.ds(off, BAND), :], b_ref[0], off, scale)
        o_ref[...] = _dot_nn(w.astype(BF16), vp[pl.ds(off, BAND), :]).astype(o_ref.dtype)

    q_spec, k_spec, v_spec, b_spec = _ca_specs(n_heads, s, col0)
    return pl.pallas_call(
        body, name=name, grid=(n_heads, nc),
        in_specs=[q_spec, k_spec, v_spec, b_spec],
        out_specs=pl.BlockSpec((CHUNK, HEAD_DIM), lambda h, c: (c, h)),
        out_shape=jax.ShapeDtypeStruct((s, n_heads * HEAD_DIM), BF16),
        scratch_shapes=[pltpu.VMEM((s + PAD, HEAD_DIM), BF16), pltpu.VMEM((s + PAD, HEAD_DIM), BF16)],
        compiler_params=_cparams(("parallel", "arbitrary")),
    )(qkv, qkv, qkv, bias)


def _ca_bwd(qkv, bias, dy, n_heads, col0, *, name):
    s = qkv.shape[0]
    nc = s // CHUNK
    scale = HEAD_DIM ** -0.5

    def body(q_ref, k_ref, v_ref, b_ref, dy_ref, dq_ref, dk_ref, dv_ref, db_ref, kp, vp, dkp, dvp):
        c = pl.program_id(1)

        @pl.when(c == 0)
        def _():
            _ca_load_padded(k_ref, v_ref, kp, vp, s)
            dkp[...] = jnp.zeros_like(dkp)
            dvp[...] = jnp.zeros_like(dvp)
            db_ref[...] = jnp.zeros_like(db_ref)

        off = pl.multiple_of(c * CHUNK, CHUNK)
        q = q_ref[...]
        dyv = dy_ref[...]
        kb = kp[pl.ds(off, BAND), :]
        w = _ca_weights(q, kb, b_ref[0], off, scale)
        dw = _dot_nt(dyv, vp[pl.ds(off, BAND), :])
        dvp[pl.ds(off, BAND), :] += _dot_tn(w.astype(BF16), dyv)
        dz = w * (dw - jnp.sum(w * dw, axis=1, keepdims=True))
        db_ref[0] += dz
        dzs = (dz * scale).astype(BF16)
        dq_ref[...] = _dot_nn(dzs, kb).astype(dq_ref.dtype)
        dkp[pl.ds(off, BAND), :] += _dot_tn(dzs, q)

        @pl.when(c == nc - 1)
        def _():
            dk_ref[...] = dkp[pl.ds(PAD, s), :].astype(dk_ref.dtype)
            dv_ref[...] = dvp[pl.ds(PAD, s), :].astype(dv_ref.dtype)

    q_spec, k_spec, v_spec, b_spec = _ca_specs(n_heads, s, col0)
    blk = pl.BlockSpec((CHUNK, HEAD_DIM), lambda h, c: (c, h))
    full = pl.BlockSpec((s, HEAD_DIM), lambda h, c: (0, h))
    sd = jax.ShapeDtypeStruct((s, n_heads * HEAD_DIM), BF16)
    return pl.pallas_call(
        body, name=name, grid=(n_heads, nc),
        in_specs=[q_spec, k_spec, v_spec, b_spec, blk],
        out_specs=[blk, full, full, b_spec],
        out_shape=[sd, sd, sd, jax.ShapeDtypeStruct((n_heads, CHUNK, BAND), F32)],
        scratch_shapes=[pltpu.VMEM((s + PAD, HEAD_DIM), BF16), pltpu.VMEM((s + PAD, HEAD_DIM), BF16),
                        pltpu.VMEM((s + PAD, HEAD_DIM), F32), pltpu.VMEM((s + PAD, HEAD_DIM), F32)],
        compiler_params=_cparams(("parallel", "arbitrary")),
    )(qkv, qkv, qkv, bias, dy)


def _local_step(x, p, target, w, small):
    d = x.shape[1]
    n_sb = w["w_sb_out"].shape[0] // HEAD_DIM
    n_ca = w["w_ca_out"].shape[0] // HEAD_DIM
    qkv_cols = 3 * HEAD_DIM * (n_sb + n_ca)
    ca_col0 = 3 * n_sb
    both = (F32, BF16)

    h1 = _rms_fwd(x, small["g_mix"], name="rms_mix")
    qkv = _mm(h1, w["w_in"], "nn", (BF16,), name="proj_qkv", n=qkv_cols)
    gates = _mm(h1, w["w_in"], "nn", (F32,), name="proj_gates", n=2 * d, b_col_off=qkv_cols)
    bias = _band_bias(small["rel_bias"])
    y_sb = _sb_fwd(qkv, n_sb, 0, name="sb_fwd")
    y_ca = _ca_fwd(qkv, bias, n_ca, ca_col0, name="ca_fwd")
    o_sb = _mm(y_sb, w["w_sb_out"], "nn", (F32,), name="sb_out")
    o_ca = _mm(y_ca, w["w_ca_out"], "nn", (F32,), name="ca_out")
    merged = _gate_merge_fwd(gates, o_sb, o_ca, name="gate_merge")
    x1 = _mm(merged, w["w_mix_out"], "nn", (F32,), name="mix_out", resid=x)
    h2 = _rms_fwd(x1, small["g_ffn"], name="rms_ffn")
    gu = _mm(h2, w["w_ffn_in"], "nn", (F32,), name="ffn_in")
    act = _swiglu_fwd(gu, name="swiglu")
    x2 = _mm(act, w["w_ffn_out"], "nn", (F32,), name="ffn_out", resid=x1)
    h3 = _rms_fwd(x2, small["g_ple"], name="rms_ple")
    t = _mm(h3, w["w_ple_gate"], "nn", (F32,), name="ple_gate")
    pe = _mm(p, w["w_ple_in"], "nn", (F32,), name="ple_in")
    x3 = _ple_fwd(x2, t, pe, name="ple_add")

    gw, gs = {}, {}
    dx3, gs["g_final"], loss = _final_loss(x3, small["g_final"], target, name="final_loss")
    dt, dpe = _ple_bwd(dx3, t, pe, name="ple_bwd")
    gw["w_ple_in"] = _mm(p, dpe, "tn", both, name="dw_ple_in")
    gw["w_ple_gate"] = _mm(h3, dt, "tn", both, name="dw_ple_gate")
    dh3 = _mm(dt, w["w_ple_gate"], "nt", (F32,), name="dh_ple")
    dx2, gs["g_ple"] = _rms_bwd(x2, small["g_ple"], dh3, dx3, name="rms_ple_bwd")
    gw["w_ffn_out"] = _mm(act, dx2, "tn", both, name="dw_ffn_out")
    dact = _mm(dx2, w["w_ffn_out"], "nt", (F32,), name="dact")
    dg_ff, du_ff = _swiglu_bwd(dact, gu, name="swiglu_bwd")
    dgu = jnp.concatenate([dg_ff, du_ff], axis=1)
    gw["w_ffn_in"] = _mm(h2, dgu, "tn", both, name="dw_ffn_in")
    dh2 = _mm(dgu, w["w_ffn_in"], "nt", (F32,), name="dh_ffn")
    dx1, gs["g_ffn"] = _rms_bwd(x1, small["g_ffn"], dh2, dx2, name="rms_ffn_bwd")
    gw["w_mix_out"] = _mm(merged, dx1, "tn", both, name="dw_mix_out")
    dmerged = _mm(dx1, w["w_mix_out"], "nt", (F32,), name="dmerged")
    dg_sb, dg_ca, do_sb, do_ca = _gate_merge_bwd(dmerged, gates, o_sb, o_ca, name="gate_merge_bwd")
    gw["w_sb_out"] = _mm(y_sb, do_sb, "tn", both, name="dw_sb_out")
    gw["w_ca_out"] = _mm(y_ca, do_ca, "tn", both, name="dw_ca_out")
    dy_sb = _mm(do_sb, w["w_sb_out"], "nt", (BF16,), name="dy_sb")
    dy_ca = _mm(do_ca, w["w_ca_out"], "nt", (BF16,), name="dy_ca")
    dq_sb, dk_sb, dv_sb = _sb_bwd(qkv, dy_sb, n_sb, 0, name="sb_bwd")
    dq_ca, dk_ca, dv_ca, dbias = _ca_bwd(qkv, bias, dy_ca, n_ca, ca_col0, name="ca_bwd")
    gs["rel_bias"] = _band_bias_grad(dbias)
    dproj = jnp.concatenate([dq_sb, dk_sb, dv_sb, dq_ca, dk_ca, dv_ca, dg_sb, dg_ca], axis=1)
    gw["w_in"] = _mm(h1, dproj, "tn", both, name="dw_in")
    dh1 = _mm(dproj, w["w_in"], "nt", (F32,), name="dh_mix")
    grad_x, gs["g_mix"] = _rms_bwd(x, small["g_mix"], dh1, dx1, name="rms_mix_bwd")
    return loss, grad_x, gw, gs


HBM = pl.BlockSpec(memory_space=pltpu.HBM)


def _position():
    x, y, c = lax.axis_index("x"), lax.axis_index("y"), lax.axis_index("c")
    chips = [(1 - x, y), (x, 1 - y), (1 - x, 1 - y)]
    return x, y, c, chips


def _aligned(v, m):
    return v if isinstance(v, int) else pl.multiple_of(v, m)


def _piece_dims(shape, axis):
    k, n = shape
    return (k // 2, n // N_CHIPS) if axis == 1 else (k // N_CHIPS // 2, n)


def _piece(ref, shape, axis, j, h):
    pr, pc = _piece_dims(shape, axis)
    if axis == 1:
        return ref.at[pl.ds(_aligned(h * pr, 16), pr), pl.ds(_aligned(j * pc, 128), pc)]
    return ref.at[pl.ds(_aligned((2 * j + h) * pr, 16), pr), :]


def _shard_half(ref, h):
    rows = ref.shape[0] // 2
    return ref.at[pl.ds(_aligned(h * rows, 16), rows), :]


def _remote(src, dst, send_sems, recv_sems, k, to):
    return pltpu.make_async_remote_copy(src_ref=src, dst_ref=dst, send_sem=send_sems.at[k],
                                        recv_sem=recv_sems.at[k], device_id=to, device_id_type=MESH)


def _prefetch_call(body, scalars, ins, in_specs, out_shape, out_specs, grid, *, name):
    spec = pltpu.PrefetchScalarGridSpec(num_scalar_prefetch=1, grid=grid, in_specs=in_specs, out_specs=out_specs)
    return pl.pallas_call(body, name=name, grid_spec=spec, out_shape=out_shape,
                          compiler_params=_cparams(("parallel",) * len(grid)))(scalars, *ins)


def _slab_tiles(pr, pc):
    tc = pc if pc <= 4096 else _pick(pc, (2048, 1024, 512, 256, 128))
    tr = next(t for t in (1024, 512, 256, 128, 64, 32, 16) if pr % t == 0 and t * tc <= 512 * 1024)
    return tr, tc


def _cast_place(w, axis, pos, *, name):
    ks, ns = w.shape
    shape = (ks, ns * N_CHIPS) if axis == 1 else (ks * N_CHIPS, ns)
    tr, tc = _slab_tiles(ks, ns)
    nr, nc = ks // tr, ns // tc

    def body(pos_ref, w_ref, o_ref):
        o_ref[...] = w_ref[...].astype(o_ref.dtype)

    if axis == 1:
        out_map = lambda i, j, pos_ref: (i, pos_ref[0] * nc + j)
    else:
        out_map = lambda i, j, pos_ref: (pos_ref[0] * nr + i, j)
    return _prefetch_call(body, pos, [w], [pl.BlockSpec((tr, tc), lambda i, j, pos_ref: (i, j))],
                          jax.ShapeDtypeStruct(shape, BF16), pl.BlockSpec((tr, tc), out_map), (nr, nc), name=name)


def _all_gather(full, axis, *, name):
    shape = full.shape

    def body(in_ref, full_ref, send_sems, recv_sems):
        x, y, c, chips = _position()
        me = 2 * x + y
        sibling = (x, y, 1 - c)
        piece = functools.partial(_piece, full_ref, shape, axis)
        first = [_remote(_piece(in_ref, shape, axis, me, c), piece(me, c), send_sems, recv_sems, k, (px, py, c))
                 for k, (px, py) in enumerate(chips)]
        for cp in first:
            cp.start()
        passed = []
        for k, (px, py) in enumerate(chips):
            landed = piece(2 * px + py, c)
            _remote(landed, landed, send_sems, recv_sems, k, sibling).wait_recv()
            fwd = _remote(landed, landed, send_sems, recv_sems, 3 + k, sibling)
            fwd.start()
            passed.append(fwd)
        for k, (px, py) in enumerate(chips):
            other = piece(2 * px + py, 1 - c)
            _remote(other, other, send_sems, recv_sems, 3 + k, sibling).wait_recv()
        for cp in first + passed:
            cp.wait_send()

    return pl.pallas_call(
        body, name=name,
        in_specs=[HBM], out_specs=HBM,
        out_shape=jax.ShapeDtypeStruct(shape, full.dtype),
        input_output_aliases={0: 0},
        scratch_shapes=[pltpu.SemaphoreType.DMA((6,)), pltpu.SemaphoreType.DMA((6,))],
    )(full)


def _pair_exchange(g16, axis, *, name):
    shape = g16.shape
    pr, pc = _piece_dims(shape, axis)

    def body(g16_ref, sib_ref, send_sems, recv_sems):
        x, y, c, _ = _position()
        sends = [_remote(_piece(g16_ref, shape, axis, j, 1 - c), sib_ref.at[j], send_sems, recv_sems, j, (x, y, 1 - c))
                 for j in range(N_CHIPS)]
        for cp in sends:
            cp.start()
        for cp in sends:
            cp.wait()

    return pl.pallas_call(
        body, name=name,
        in_specs=[HBM], out_specs=HBM,
        out_shape=jax.ShapeDtypeStruct((N_CHIPS, pr, pc), BF16),
        scratch_shapes=[pltpu.SemaphoreType.DMA((N_CHIPS,)), pltpu.SemaphoreType.DMA((N_CHIPS,))],
    )(g16)


def _pair_add(g32, sib, axis, pos, *, name):
    _, pr, pc = sib.shape
    tr, tc = _slab_tiles(pr, pc)
    nr, nc = pr // tr, pc // tc

    def body(pos_ref, g_ref, b_ref, o32_ref, o16_ref):
        r = g_ref[...] + b_ref[0].astype(F32)
        o32_ref[0] = r
        o16_ref[0] = r.astype(o16_ref.dtype)

    if axis == 1:
        g_map = lambda j, i, k, pos_ref: (pos_ref[1] * nr + i, j * nc + k)
    else:
        g_map = lambda j, i, k, pos_ref: ((2 * j + pos_ref[1]) * nr + i, k)
    blk = pl.BlockSpec((1, tr, tc), lambda j, i, k, pos_ref: (j, i, k))
    return _prefetch_call(body, pos, [g32, sib], [pl.BlockSpec((tr, tc), g_map), blk],
                          [jax.ShapeDtypeStruct(sib.shape, F32), jax.ShapeDtypeStruct(sib.shape, BF16)],
                          [blk, blk], (N_CHIPS, nr, nc), name=name)


def _chip_exchange(s16, *, name):
    _, pr, pc = s16.shape

    def body(s16_ref, recv_ref, send_sems, recv_sems):
        x, y, c, chips = _position()
        sends = [_remote(s16_ref.at[2 * px + py], recv_ref.at[k], send_sems, recv_sems, k, (px, py, c))
                 for k, (px, py) in enumerate(chips)]
        for cp in sends:
            cp.start()
        for cp in sends:
            cp.wait()

    return pl.pallas_call(
        body, name=name,
        in_specs=[HBM], out_specs=HBM,
        out_shape=jax.ShapeDtypeStruct((3, pr, pc), BF16),
        scratch_shapes=[pltpu.SemaphoreType.DMA((3,)), pltpu.SemaphoreType.DMA((3,))],
    )(s16)


def _chip_sum(s32, recv, pos, *, name):
    _, pr, pc = s32.shape
    tr, tc = _slab_tiles(pr, pc)
    nr, nc = pr // tr, pc // tc

    def body(pos_ref, m_ref, r_ref, o_ref):
        o_ref[...] = ((m_ref[0] + r_ref[0].astype(F32)) + r_ref[1].astype(F32)) + r_ref[2].astype(F32)

    return _prefetch_call(
        body, pos, [s32, recv],
        [pl.BlockSpec((1, tr, tc), lambda i, k, pos_ref: (pos_ref[0], i, k)),
         pl.BlockSpec((3, tr, tc), lambda i, k, pos_ref: (0, i, k))],
        jax.ShapeDtypeStruct((2 * pr, pc), F32),
        pl.BlockSpec((tr, tc), lambda i, k, pos_ref: (pos_ref[1] * nr + i, k)), (nr, nc), name=name)


def _pair_share(shard, *, name):
    def body(in_ref, out_ref, send_sems, recv_sems):
        x, y, c, _ = _position()
        send = _remote(_shard_half(in_ref, c), _shard_half(out_ref, c), send_sems, recv_sems, 0, (x, y, 1 - c))
        send.start()
        other = _shard_half(out_ref, 1 - c)
        _remote(other, other, send_sems, recv_sems, 0, (x, y, 1 - c)).wait_recv()
        send.wait_send()

    return pl.pallas_call(
        body, name=name,
        in_specs=[HBM], out_specs=HBM,
        out_shape=jax.ShapeDtypeStruct(shard.shape, F32),
        input_output_aliases={0: 0},
        scratch_shapes=[pltpu.SemaphoreType.DMA((1,)), pltpu.SemaphoreType.DMA((1,))],
    )(shard)


def _reduce_scatter(g32, g16, axis, pos, *, name):
    sib = _pair_exchange(g16, axis, name=name + "_pair")
    s32, s16 = _pair_add(g32, sib, axis, pos, name=name + "_pair_add")
    recv = _chip_exchange(s16, name=name + "_chips")
    return _pair_share(_chip_sum(s32, recv, pos, name=name + "_chip_sum"), name=name + "_share")


def _small_all_reduce(vec, *, name):
    r = vec.shape[0]

    def body(vec_ref, out_ref, slots, send_sems, recv_sems):
        x, y, c, _ = _position()
        me = 4 * x + 2 * y + c
        slots[me] = vec_ref[...]
        sends = []
        for k in range(1, 8):
            to = (x ^ (k >> 2), y ^ ((k >> 1) & 1), c ^ (k & 1))
            cp = _remote(slots.at[me], slots.at[me], send_sems, recv_sems, k - 1, to)
            cp.start()
            sends.append(cp)
        for k in range(1, 8):
            frm = 4 * (x ^ (k >> 2)) + 2 * (y ^ ((k >> 1) & 1)) + (c ^ (k & 1))
            _remote(slots.at[frm], slots.at[frm], send_sems, recv_sems, k - 1, (x, y, c)).wait_recv()
        for cp in sends:
            cp.wait_send()
        total = slots[0]
        for d in range(1, 8):
            total = total + slots[d]
        out_ref[...] = total

    return pl.pallas_call(
        body, name=name,
        in_specs=[pl.BlockSpec(memory_space=pltpu.VMEM)], out_specs=pl.BlockSpec(memory_space=pltpu.VMEM),
        out_shape=jax.ShapeDtypeStruct((r, 128), F32),
        scratch_shapes=[pltpu.VMEM((8, r, 128), F32), pltpu.SemaphoreType.DMA((7,)), pltpu.SemaphoreType.DMA((7,))],
    )(vec)


def _adamw(w, g, m, v, *, name):
    r, c = w.shape
    tc = c if c <= 4096 else _pick(c, (2048, 1024, 512, 256, 128))
    tr = next(t for t in (512, 256, 128, 64, 32, 16, 8) if r % t == 0 and t * tc <= 256 * 1024)

    def body(w_ref, g_ref, m_ref, v_ref, d_ref, nm_ref, nv_ref):
        gv = g_ref[...]
        nm = ADAM_B1 * m_ref[...] + (1.0 - ADAM_B1) * gv
        nv = ADAM_B2 * v_ref[...] + (1.0 - ADAM_B2) * (gv * gv)
        m_hat = nm / (1.0 - ADAM_B1 ** ADAM_STEP)
        v_hat = nv / (1.0 - ADAM_B2 ** ADAM_STEP)
        d_ref[...] = -ADAM_LR * (m_hat / (jnp.sqrt(v_hat) + ADAM_EPS) + ADAM_WD * w_ref[...])
        nm_ref[...] = nm
        nv_ref[...] = nv

    blk = ((tr, tc), lambda i, j: (i, j))
    sd = jax.ShapeDtypeStruct((r, c), F32)
    return _ew(body, [w, g, m, v], [blk] * 4, [sd, sd, sd], [blk] * 3, (r // tr, c // tc), name=name)


BIG = (("w_in", 1), ("w_sb_out", 1), ("w_ca_out", 1), ("w_mix_out", 0), ("w_ffn_in", 1), ("w_ffn_out", 0),
       ("w_ple_in", 1), ("w_ple_gate", 0))
SMALL = ("rel_bias", "g_mix", "g_ffn", "g_ple", "g_final")
ORDER = ("w_in", "w_sb_out", "w_ca_out", "w_mix_out", "rel_bias", "g_mix", "g_ffn", "g_ple", "g_final",
         "w_ffn_in", "w_ffn_out", "w_ple_in", "w_ple_gate")


def _pack(parts):
    flat = jnp.concatenate([a.reshape(-1) for a in parts])
    rows = -(-flat.shape[0] // 1024) * 8
    return jnp.pad(flat, (0, rows * 128 - flat.shape[0])).reshape(rows, 128)


def _unpack(packed, like):
    flat, out, at = packed.reshape(-1), [], 0
    for a in like:
        out.append(flat[at:at + a.size].reshape(a.shape))
        at += a.size
    return out


def kernel(x, p, w_in, w_sb_out, w_ca_out, w_mix_out, rel_bias, g_mix, g_ffn, g_ple, g_final, w_ffn_in, w_ffn_out, w_ple_in, w_ple_gate, loss_target, m_w_in, m_w_sb_out, m_w_ca_out, m_w_mix_out, m_rel_bias, m_g_mix, m_g_ffn, m_g_ple, m_g_final, m_w_ffn_in, m_w_ffn_out, m_w_ple_in, m_w_ple_gate, v_w_in, v_w_sb_out, v_w_ca_out, v_w_mix_out, v_rel_bias, v_g_mix, v_g_ffn, v_g_ple, v_g_final, v_w_ffn_in, v_w_ffn_out, v_w_ple_in, v_w_ple_gate):
    weights = dict(w_in=w_in, w_sb_out=w_sb_out, w_ca_out=w_ca_out, w_mix_out=w_mix_out, rel_bias=rel_bias,
                   g_mix=g_mix, g_ffn=g_ffn, g_ple=g_ple, g_final=g_final, w_ffn_in=w_ffn_in,
                   w_ffn_out=w_ffn_out, w_ple_in=w_ple_in, w_ple_gate=w_ple_gate)
    m_in = dict(w_in=m_w_in, w_sb_out=m_w_sb_out, w_ca_out=m_w_ca_out, w_mix_out=m_w_mix_out, rel_bias=m_rel_bias,
                g_mix=m_g_mix, g_ffn=m_g_ffn, g_ple=m_g_ple, g_final=m_g_final, w_ffn_in=m_w_ffn_in,
                w_ffn_out=m_w_ffn_out, w_ple_in=m_w_ple_in, w_ple_gate=m_w_ple_gate)
    v_in = dict(w_in=v_w_in, w_sb_out=v_w_sb_out, w_ca_out=v_w_ca_out, w_mix_out=v_w_mix_out, rel_bias=v_rel_bias,
                g_mix=v_g_mix, g_ffn=v_g_ffn, g_ple=v_g_ple, g_final=v_g_final, w_ffn_in=v_w_ffn_in,
                w_ffn_out=v_w_ffn_out, w_ple_in=v_w_ple_in, w_ple_gate=v_w_ple_gate)

    pos = jnp.stack([2 * lax.axis_index("x") + lax.axis_index("y"), lax.axis_index("c")]).astype(jnp.int32)
    full = {n: _all_gather(_cast_place(weights[n][0], axis, pos, name="cast_" + n), axis, name="gather_" + n)
            for n, axis in BIG}
    small = dict(rel_bias=rel_bias[0], g_mix=g_mix, g_ffn=g_ffn, g_ple=g_ple, g_final=g_final.reshape(1, -1))
    loss, grad_x, gw, gs = _local_step(x[0], p[0, 0], loss_target[0], full, small)

    grads, delta, new_m, new_v = {}, {}, {}, {}
    for n, axis in BIG:
        g = _reduce_scatter(gw[n][0], gw[n][1], axis, pos, name="rs_" + n)
        d, nm, nv = _adamw(weights[n][0], g, m_in[n][0], v_in[n][0], name="adamw_" + n)
        grads[n], delta[n], new_m[n], new_v[n] = g[None], d[None], nm[None], nv[None]

    like = [weights[n] for n in SMALL]
    reduced = _small_all_reduce(_pack([gs[n] for n in SMALL] + [loss[:, :1]]), name="small_all_reduce")
    g_small = _unpack(reduced, like + [loss[:, :1]])
    total_loss = g_small[-1].reshape(())
    g_packed = _pack(g_small[:-1])
    d_s, m_s, v_s = _adamw(_pack(like), g_packed, _pack([m_in[n] for n in SMALL]), _pack([v_in[n] for n in SMALL]),
                           name="adamw_small")
    for n, g, d, nm, nv in zip(SMALL, g_small[:-1], _unpack(d_s, like), _unpack(m_s, like), _unpack(v_s, like)):
        grads[n], delta[n], new_m[n], new_v[n] = g, d, nm, nv

    return (total_loss, grad_x[None], *[grads[n] for n in ORDER], *[delta[n] for n in ORDER],
            *[new_m[n] for n in ORDER], *[new_v[n] for n in ORDER])
```

```python
import functools
import math

import jax
import jax.numpy as jnp
import numpy as np
from jax import lax
from jax.experimental import pallas as pl
from jax.experimental.pallas import tpu as pltpu

F32 = jnp.float32
BF16 = jnp.bfloat16

HEAD_DIM = 128
CHUNK = 64
LEFT_CHUNKS = 8
REL_CLIP = 128
N_REL = REL_CLIP + CHUNK
BAND = (LEFT_CHUNKS + 2) * CHUNK
PAD = (LEFT_CHUNKS + 1) * CHUNK
SB_BLOCK = 128
EPS = 1e-6
NEG = -1e30

ADAM_LR = 0.001
ADAM_B1 = 0.9
ADAM_B2 = 0.999
ADAM_EPS = 1e-08
ADAM_WD = 0.01
ADAM_STEP = 10

VMEM_LIMIT = 48 * 1024 * 1024
MESH = pl.DeviceIdType.MESH
N_CHIPS = 4


def _pick(dim, prefs):
    for t in prefs:
        if dim % t == 0:
            return t
    raise ValueError(f"no tile for {dim}")


def _cparams(sem=None):
    return pltpu.CompilerParams(dimension_semantics=sem, vmem_limit_bytes=VMEM_LIMIT)


def _sigmoid(v):
    return 1.0 / (1.0 + jnp.exp(-v))


def _dot(a, b, dims):
    return lax.dot_general(a, b, (dims, ((), ())), preferred_element_type=F32)


def _dot_nn(a, b):
    return _dot(a, b, ((1,), (0,)))


def _dot_nt(a, b):
    return _dot(a, b, ((1,), (1,)))


def _dot_tn(a, b):
    return _dot(a, b, ((0,), (0,)))


HBM = pl.BlockSpec(memory_space=pltpu.HBM)


class _Ride:
    def __init__(self):
        self.items = []

    def add(self, ins, outs, aliases, n_sems, start, finish, sink):
        self.items.append((ins, outs, aliases, n_sems, start, finish, sink))


def _call(body, args, *, name, grid, in_specs, out_specs, out_shape, scratch_shapes=(), sem=None, ride=None):
    items = ride.items if ride is not None else []
    n_in, n_out, n_scr = len(args), len(out_shape), len(scratch_shapes)
    r_ins = [a for it in items for a in it[0]]
    r_outs = [o for it in items for o in it[1]]
    aliases, a, b = {}, n_in, n_out
    for it in items:
        aliases.update({a + i: b + o for i, o in it[2].items()})
        a, b = a + len(it[0]), b + len(it[1])
    sems = [pltpu.SemaphoreType.DMA((it[3],)) for it in items for _ in range(2)]

    def wrapped(*refs):
        ins, rin = refs[:n_in], refs[n_in:n_in + len(r_ins)]
        at = n_in + len(r_ins)
        outs, rout = refs[at:at + n_out], refs[at + n_out:at + n_out + len(r_outs)]
        at += n_out + len(r_outs)
        scr, rsem = refs[at:at + n_scr], refs[at + n_scr:]

        def each(which):
            a = b = 0
            for q, it in enumerate(items):
                it[which](rin[a:a + len(it[0])], rout[b:b + len(it[1])], rsem[2 * q], rsem[2 * q + 1])
                a, b = a + len(it[0]), b + len(it[1])

        if items:
            ids = [pl.program_id(d) for d in range(len(grid))]
            first = functools.reduce(jnp.logical_and, [i == 0 for i in ids])
            last = functools.reduce(jnp.logical_and, [i == g - 1 for i, g in zip(ids, grid)])
            pl.when(first)(lambda: each(4))
        body(*ins, *outs, *scr)
        if items:
            pl.when(last)(lambda: each(5))

    res = pl.pallas_call(
        wrapped, name=name, grid=grid,
        in_specs=list(in_specs) + [HBM] * len(r_ins),
        out_specs=list(out_specs) + [HBM] * len(r_outs),
        out_shape=list(out_shape) + r_outs,
        scratch_shapes=list(scratch_shapes) + sems,
        input_output_aliases=aliases,
        compiler_params=_cparams(("arbitrary",) * len(grid) if items else sem),
    )(*args, *r_ins)
    b = n_out
    for it in items:
        it[6](res[b:b + len(it[1])])
        b += len(it[1])
    return list(res[:n_out])


def _mm(a, b, mode, out_dtypes, *, name, n=None, b_col_off=0, resid=None, ride=None):
    if mode == "nn":
        m, k = a.shape
        n = b.shape[1] if n is None else n
    elif mode == "nt":
        m, k = a.shape
        n = b.shape[0]
    else:
        k, m = a.shape
        n = b.shape[1]
    tm = _pick(m, (512, 256, 128))
    tn = _pick(math.gcd(n, b_col_off) if b_col_off else n, (1024, 512, 256, 128))
    tk = _pick(k, (1024, 512, 256, 128))
    nk = k // tk
    boff = b_col_off // tn
    n_out = len(out_dtypes)
    has_resid = resid is not None

    def body(*refs):
        a_ref, b_ref = refs[0], refs[1]
        r_ref = refs[2] if has_resid else None
        o_refs = refs[2 + has_resid: 2 + has_resid + n_out]
        acc_ref = refs[-1]
        kk = pl.program_id(2)

        @pl.when(kk == 0)
        def _():
            acc_ref[...] = jnp.zeros_like(acc_ref)

        av = a_ref[...].astype(BF16)
        bv = b_ref[...].astype(BF16)
        if mode == "nn":
            acc_ref[...] += _dot_nn(av, bv)
        elif mode == "nt":
            acc_ref[...] += _dot_nt(av, bv)
        else:
            acc_ref[...] += _dot_tn(av, bv)

        @pl.when(kk == nk - 1)
        def _():
            r = acc_ref[...]
            if has_resid:
                r = r + r_ref[...]
            for o_ref in o_refs:
                o_ref[...] = r.astype(o_ref.dtype)

    if mode == "nn":
        a_spec = pl.BlockSpec((tm, tk), lambda i, j, kk: (i, kk))
        b_spec = pl.BlockSpec((tk, tn), lambda i, j, kk: (kk, j + boff))
    elif mode == "nt":
        a_spec = pl.BlockSpec((tm, tk), lambda i, j, kk: (i, kk))
        b_spec = pl.BlockSpec((tn, tk), lambda i, j, kk: (j, kk))
    else:
        a_spec = pl.BlockSpec((tk, tm), lambda i, j, kk: (kk, i))
        b_spec = pl.BlockSpec((tk, tn), lambda i, j, kk: (kk, j))
    o_spec = pl.BlockSpec((tm, tn), lambda i, j, kk: (i, j))
    in_specs = [a_spec, b_spec] + ([o_spec] if has_resid else [])
    args = [a, b] + ([resid] if has_resid else [])
    outs = _call(
        body, args, name=name,
        grid=(m // tm, n // tn, nk),
        in_specs=in_specs,
        out_specs=[o_spec] * n_out,
        out_shape=[jax.ShapeDtypeStruct((m, n), dt) for dt in out_dtypes],
        scratch_shapes=[pltpu.VMEM((tm, tn), F32)],
        sem=("parallel", "parallel", "arbitrary"), ride=ride)
    return outs[0] if n_out == 1 else tuple(outs)


def _row_tile(s):
    return _pick(s, (256, 128))


def _rms_fwd(x, g, *, name):
    s, d = x.shape
    tr = _row_tile(s)

    def body(x_ref, g_ref, o_ref):
        xv = x_ref[...]
        r = lax.rsqrt(jnp.mean(xv * xv, axis=1, keepdims=True) + EPS)
        o_ref[...] = (xv * r * g_ref[...]).astype(o_ref.dtype)

    return pl.pallas_call(
        body, name=name, grid=(s // tr,),
        in_specs=[pl.BlockSpec((tr, d), lambda i: (i, 0)), pl.BlockSpec((1, d), lambda i: (0, 0))],
        out_specs=pl.BlockSpec((tr, d), lambda i: (i, 0)),
        out_shape=jax.ShapeDtypeStruct((s, d), BF16),
        compiler_params=_cparams(("parallel",)),
    )(x, g)


def _rms_bwd(x, g, dh, dres, *, name):
    s, d = x.shape
    tr = _row_tile(s)

    def body(x_ref, g_ref, dh_ref, dres_ref, dx_ref, dg_ref):
        i = pl.program_id(0)
        xv = x_ref[...]
        r = lax.rsqrt(jnp.mean(xv * xv, axis=1, keepdims=True) + EPS)
        xhat = xv * r
        dhv = dh_ref[...]
        dxhat = dhv * g_ref[...]
        proj = jnp.mean(dxhat * xhat, axis=1, keepdims=True)
        dx_ref[...] = dres_ref[...] + r * (dxhat - xhat * proj)

        @pl.when(i == 0)
        def _():
            dg_ref[...] = jnp.zeros_like(dg_ref)

        dg_ref[...] += jnp.sum(dhv * xhat, axis=0, keepdims=True)

    row = pl.BlockSpec((tr, d), lambda i: (i, 0))
    vec = pl.BlockSpec((1, d), lambda i: (0, 0))
    return pl.pallas_call(
        body, name=name, grid=(s // tr,),
        in_specs=[row, vec, row, row],
        out_specs=[row, vec],
        out_shape=[jax.ShapeDtypeStruct((s, d), F32), jax.ShapeDtypeStruct((1, d), F32)],
        compiler_params=_cparams(("arbitrary",)),
    )(x, g, dh, dres)


def _final_loss(x, g, target, *, name):
    s, d = x.shape
    tr = _row_tile(s)

    def body(x_ref, g_ref, t_ref, dx_ref, dg_ref, loss_ref):
        i = pl.program_id(0)
        xv = x_ref[...]
        gv = g_ref[...]
        r = lax.rsqrt(jnp.mean(xv * xv, axis=1, keepdims=True) + EPS)
        xhat = xv * r
        err = xhat * gv - t_ref[...]
        dy = err * (1.0 / d)
        dxhat = dy * gv
        proj = jnp.mean(dxhat * xhat, axis=1, keepdims=True)
        dx_ref[...] = r * (dxhat - xhat * proj)

        @pl.when(i == 0)
        def _():
            dg_ref[...] = jnp.zeros_like(dg_ref)
            loss_ref[...] = jnp.zeros_like(loss_ref)

        dg_ref[...] += jnp.sum(dy * xhat, axis=0, keepdims=True)
        part = 0.5 * jnp.sum(jnp.mean(err * err, axis=1, keepdims=True), axis=0, keepdims=True)
        loss_ref[...] += jnp.broadcast_to(part, loss_ref.shape)

    row = pl.BlockSpec((tr, d), lambda i: (i, 0))
    vec = pl.BlockSpec((1, d), lambda i: (0, 0))
    return pl.pallas_call(
        body, name=name, grid=(s // tr,),
        in_specs=[row, vec, row],
        out_specs=[row, vec, pl.BlockSpec((1, 128), lambda i: (0, 0))],
        out_shape=[jax.ShapeDtypeStruct((s, d), F32), jax.ShapeDtypeStruct((1, d), F32),
                   jax.ShapeDtypeStruct((1, 128), F32)],
        compiler_params=_cparams(("arbitrary",)),
    )(x, g, target)


def _ew(body, ins, in_blocks, outs, out_blocks, grid, *, name):
    return pl.pallas_call(
        body, name=name, grid=grid,
        in_specs=[pl.BlockSpec(bs, im) for bs, im in in_blocks],
        out_specs=[pl.BlockSpec(bs, im) for bs, im in out_blocks],
        out_shape=outs,
        compiler_params=_cparams(("parallel",) * len(grid)),
    )(*ins)


def _gate_merge_fwd(gates, o_sb, o_ca, *, name):
    s, d = o_sb.shape
    tr, tc = _row_tile(s), _pick(d, (1024, 512, 256, 128))
    nc = d // tc

    def body(gs_ref, gc_ref, os_ref, oc_ref, m_ref):
        m = _sigmoid(gs_ref[...]) * os_ref[...] + _sigmoid(gc_ref[...]) * oc_ref[...]
        m_ref[...] = m.astype(m_ref.dtype)

    blk = ((tr, tc), lambda i, j: (i, j))
    return _ew(body, [gates, gates, o_sb, o_ca],
               [blk, ((tr, tc), lambda i, j: (i, j + nc)), blk, blk],
               [jax.ShapeDtypeStruct((s, d), BF16)], [blk], (s // tr, nc), name=name)[0]


def _gate_merge_bwd(dmerged, gates, o_sb, o_ca, *, name):
    s, d = o_sb.shape
    tr, tc = _row_tile(s), _pick(d, (1024, 512, 256, 128))
    nc = d // tc

    def body(dm_ref, gs_ref, gc_ref, os_ref, oc_ref, dgs_ref, dgc_ref, dos_ref, doc_ref):
        dm = dm_ref[...]
        ss = _sigmoid(gs_ref[...])
        sc = _sigmoid(gc_ref[...])
        dgs_ref[...] = (dm * os_ref[...] * ss * (1.0 - ss)).astype(dgs_ref.dtype)
        dgc_ref[...] = (dm * oc_ref[...] * sc * (1.0 - sc)).astype(dgc_ref.dtype)
        dos_ref[...] = (dm * ss).astype(dos_ref.dtype)
        doc_ref[...] = (dm * sc).astype(doc_ref.dtype)

    blk = ((tr, tc), lambda i, j: (i, j))
    sd = jax.ShapeDtypeStruct((s, d), BF16)
    return _ew(body, [dmerged, gates, gates, o_sb, o_ca],
               [blk, blk, ((tr, tc), lambda i, j: (i, j + nc)), blk, blk],
               [sd, sd, sd, sd], [blk, blk, blk, blk], (s // tr, nc), name=name)


def _swiglu_fwd(gu, *, name):
    s, f2 = gu.shape
    f = f2 // 2
    tr, tc = _row_tile(s), _pick(f, (512, 256, 128))
    nc = f // tc

    def body(g_ref, u_ref, a_ref):
        gv = g_ref[...]
        a_ref[...] = (gv * _sigmoid(gv) * u_ref[...]).astype(a_ref.dtype)

    blk = ((tr, tc), lambda i, j: (i, j))
    return _ew(body, [gu, gu], [blk, ((tr, tc), lambda i, j: (i, j + nc))],
               [jax.ShapeDtypeStruct((s, f), BF16)], [blk], (s // tr, nc), name=name)[0]


def _swiglu_bwd(dact, gu, *, name):
    s, f2 = gu.shape
    f = f2 // 2
    tr, tc = _row_tile(s), _pick(f, (512, 256, 128))
    nc = f // tc

    def body(da_ref, g_ref, u_ref, dg_ref, du_ref):
        da = da_ref[...]
        gv = g_ref[...]
        sg = _sigmoid(gv)
        dg_ref[...] = (da * u_ref[...] * sg * (1.0 + gv * (1.0 - sg))).astype(dg_ref.dtype)
        du_ref[...] = (da * gv * sg).astype(du_ref.dtype)

    blk = ((tr, tc), lambda i, j: (i, j))
    hi = ((tr, tc), lambda i, j: (i, j + nc))
    sd = jax.ShapeDtypeStruct((s, f), BF16)
    dg, du = _ew(body, [dact, gu, gu], [blk, blk, hi], [sd, sd], [blk, blk], (s // tr, nc), name=name)
    return dg, du


def _ple_fwd(x, t, pe, *, name):
    s, d = x.shape
    tr, tc = _row_tile(s), _pick(d, (1024, 512, 256, 128))

    def body(x_ref, t_ref, p_ref, o_ref):
        o_ref[...] = x_ref[...] + _sigmoid(t_ref[...]) * p_ref[...]

    blk = ((tr, tc), lambda i, j: (i, j))
    return _ew(body, [x, t, pe], [blk, blk, blk],
               [jax.ShapeDtypeStruct((s, d), F32)], [blk], (s // tr, d // tc), name=name)[0]


def _ple_bwd(dx, t, pe, *, name):
    s, d = dx.shape
    tr, tc = _row_tile(s), _pick(d, (1024, 512, 256, 128))

    def body(dx_ref, t_ref, p_ref, dt_ref, dp_ref):
        dxv = dx_ref[...]
        sg = _sigmoid(t_ref[...])
        dt_ref[...] = (dxv * p_ref[...] * sg * (1.0 - sg)).astype(dt_ref.dtype)
        dp_ref[...] = (dxv * sg).astype(dp_ref.dtype)

    blk = ((tr, tc), lambda i, j: (i, j))
    sd = jax.ShapeDtypeStruct((s, d), BF16)
    return _ew(body, [dx, t, pe], [blk, blk, blk], [sd, sd], [blk, blk], (s // tr, d // tc), name=name)


def _sb_masks():
    row = lax.broadcasted_iota(jnp.int32, (SB_BLOCK, SB_BLOCK), 0)
    col = lax.broadcasted_iota(jnp.int32, (SB_BLOCK, SB_BLOCK), 1)
    return row, col


def _split_dot(v, tri):
    hi = v.astype(BF16)
    lo = (v - hi.astype(F32)).astype(BF16)
    return _dot_nn(hi, tri) + _dot_nn(lo, tri)


def _sb_block(q, kj, scale, valid):
    z = _dot_nt(q, kj) * scale
    t = jnp.log(1.0 + jnp.exp(-jnp.abs(z)))
    ls = jnp.minimum(z, 0.0) - t
    lk = jnp.where(valid, -jnp.maximum(z, 0.0) - t, 0.0)
    return z, ls, lk


def _sb_specs(h_count, s, col0):
    q_spec = pl.BlockSpec((SB_BLOCK, HEAD_DIM), lambda h, i: (i, col0 + h))
    k_spec = pl.BlockSpec((s, HEAD_DIM), lambda h, i: (0, col0 + h_count + h))
    v_spec = pl.BlockSpec((s, HEAD_DIM), lambda h, i: (0, col0 + 2 * h_count + h))
    return q_spec, k_spec, v_spec


def _sb_fwd(qkv, n_heads, col0, *, name, ride=None):
    s = qkv.shape[0]
    nq = s // SB_BLOCK
    scale = HEAD_DIM ** -0.5

    def body(q_ref, k_ref, v_ref, o_ref):
        i = pl.program_id(1)
        q = q_ref[...]
        row, col = _sb_masks()
        upper = (row > col).astype(BF16)
        diag = col < row

        def step(jj, carry):
            run, acc = carry
            j = i - jj
            off = pl.multiple_of(j * SB_BLOCK, SB_BLOCK)
            kj = k_ref[pl.ds(off, SB_BLOCK), :]
            vj = v_ref[pl.ds(off, SB_BLOCK), :]
            valid = jnp.logical_or(j < i, diag)
            _, ls, lk = _sb_block(q, kj, scale, valid)
            between = _split_dot(lk, upper) + run
            a = jnp.where(valid, jnp.exp(ls + between), 0.0)
            acc = acc + _dot_nn(a.astype(BF16), vj)
            run = run + jnp.sum(lk, axis=1, keepdims=True)
            return run, acc

        init = (jnp.zeros((SB_BLOCK, 1), F32), jnp.zeros((SB_BLOCK, HEAD_DIM), F32))
        _, acc = lax.fori_loop(0, i + 1, step, init)
        o_ref[...] = acc.astype(o_ref.dtype)

    q_spec, k_spec, v_spec = _sb_specs(n_heads, s, col0)
    return _call(
        body, [qkv, qkv, qkv], name=name, grid=(n_heads, nq),
        in_specs=[q_spec, k_spec, v_spec],
        out_specs=[pl.BlockSpec((SB_BLOCK, HEAD_DIM), lambda h, i: (i, h))],
        out_shape=[jax.ShapeDtypeStruct((s, n_heads * HEAD_DIM), BF16)],
        sem=("parallel", "arbitrary"), ride=ride)[0]


def _sb_bwd(qkv, dy, n_heads, col0, *, name, ride=None):
    s = qkv.shape[0]
    nq = s // SB_BLOCK
    scale = HEAD_DIM ** -0.5

    def body(q_ref, k_ref, v_ref, dy_ref, dq_ref, dk_ref, dv_ref, e_scr, dk_acc, dv_acc):
        i = pl.program_id(1)
        q = q_ref[...]
        dyv = dy_ref[...]
        row, col = _sb_masks()
        upper = (row > col).astype(BF16)
        lower = (row < col).astype(BF16)
        diag = col < row

        @pl.when(i == 0)
        def _():
            dk_acc[...] = jnp.zeros_like(dk_acc)
            dv_acc[...] = jnp.zeros_like(dv_acc)

        def pass1(jj, run):
            j = i - jj
            off = pl.multiple_of(j * SB_BLOCK, SB_BLOCK)
            kj = k_ref[pl.ds(off, SB_BLOCK), :]
            vj = v_ref[pl.ds(off, SB_BLOCK), :]
            valid = jnp.logical_or(j < i, diag)
            _, ls, lk = _sb_block(q, kj, scale, valid)
            between = _split_dot(lk, upper) + run
            a = jnp.where(valid, jnp.exp(ls + between), 0.0)
            da = _dot_nt(dyv, vj)
            e_scr[j] = a * da
            dv_acc[pl.ds(off, SB_BLOCK), :] += _dot_tn(a.astype(BF16), dyv)
            return run + jnp.sum(lk, axis=1, keepdims=True)

        lax.fori_loop(0, i + 1, pass1, jnp.zeros((SB_BLOCK, 1), F32))

        def pass2(j, carry):
            run, dq = carry
            off = pl.multiple_of(j * SB_BLOCK, SB_BLOCK)
            kj = k_ref[pl.ds(off, SB_BLOCK), :]
            valid = jnp.logical_or(j < i, diag)
            z = _dot_nt(q, kj) * scale
            sg = _sigmoid(z)
            e = e_scr[j]
            before = _split_dot(e, lower) + run
            dz = jnp.where(valid, e * (1.0 - sg) - sg * before, 0.0) * scale
            dzb = dz.astype(BF16)
            dq = dq + _dot_nn(dzb, kj)
            dk_acc[pl.ds(off, SB_BLOCK), :] += _dot_tn(dzb, q)
            return run + jnp.sum(e, axis=1, keepdims=True), dq

        init = (jnp.zeros((SB_BLOCK, 1), F32), jnp.zeros((SB_BLOCK, HEAD_DIM), F32))
        _, dq = lax.fori_loop(0, i + 1, pass2, init)
        dq_ref[...] = dq.astype(dq_ref.dtype)

        @pl.when(i == nq - 1)
        def _():
            dk_ref[...] = dk_acc[...].astype(dk_ref.dtype)
            dv_ref[...] = dv_acc[...].astype(dv_ref.dtype)

    q_spec, k_spec, v_spec = _sb_specs(n_heads, s, col0)
    blk = pl.BlockSpec((SB_BLOCK, HEAD_DIM), lambda h, i: (i, h))
    full = pl.BlockSpec((s, HEAD_DIM), lambda h, i: (0, h))
    sd = jax.ShapeDtypeStruct((s, n_heads * HEAD_DIM), BF16)
    return _call(
        body, [qkv, qkv, qkv, dy], name=name, grid=(n_heads, nq),
        in_specs=[q_spec, k_spec, v_spec, blk],
        out_specs=[blk, full, full],
        out_shape=[sd, sd, sd],
        scratch_shapes=[pltpu.VMEM((nq, SB_BLOCK, SB_BLOCK), F32),
                        pltpu.VMEM((s, HEAD_DIM), F32), pltpu.VMEM((s, HEAD_DIM), F32)],
        sem=("parallel", "arbitrary"), ride=ride)


def _band_bias(rel_bias):
    h = rel_bias.shape[0]
    width = BAND + CHUNK
    first = width - 1 - N_REL
    line = jnp.concatenate([jnp.broadcast_to(rel_bias[:, :1], (h, first)), rel_bias], axis=1)
    tiled = jnp.broadcast_to(line[:, None, :], (h, CHUNK, width - 1)).reshape(h, CHUNK * (width - 1))
    skew = jnp.pad(tiled, ((0, 0), (0, CHUNK))).reshape(h, CHUNK, width)[:, ::-1, :BAND]
    seen = jnp.arange(BAND) >= CHUNK
    return jnp.where(seen[None, None, :], skew, NEG)


def _band_bias_grad(dbias):
    h = dbias.shape[0]
    width = BAND + CHUNK
    flipped = jnp.pad(dbias[:, ::-1, :], ((0, 0), (0, 0), (0, CHUNK)))
    skew = flipped.reshape(h, CHUNK * width)[:, :CHUNK * (width - 1)].reshape(h, CHUNK, width - 1)
    diag = jnp.sum(skew, axis=1)
    first = width - 1 - N_REL
    clipped = jnp.sum(diag[:, :first + 1], axis=1, keepdims=True)
    return jnp.concatenate([clipped, diag[:, first + 1:]], axis=1)


def _ca_load_padded(k_ref, v_ref, kp, vp, s):
    kp[pl.ds(0, PAD), :] = jnp.zeros((PAD, HEAD_DIM), kp.dtype)
    vp[pl.ds(0, PAD), :] = jnp.zeros((PAD, HEAD_DIM), vp.dtype)
    kp[pl.ds(PAD, s), :] = k_ref[...]
    vp[pl.ds(PAD, s), :] = v_ref[...]


def _ca_weights(q, kb, bias, off, scale):
    z = _dot_nt(q, kb) * scale + bias
    pos = off + lax.broadcasted_iota(jnp.int32, (CHUNK, BAND), 1)
    z = jnp.where(pos >= PAD, z, NEG)
    p = jnp.exp(z - jnp.max(z, axis=1, keepdims=True))
    return p / jnp.sum(p, axis=1, keepdims=True)


def _ca_specs(h_count, s, col0):
    q_spec = pl.BlockSpec((CHUNK, HEAD_DIM), lambda h, c: (c, col0 + h))
    k_spec = pl.BlockSpec((s, HEAD_DIM), lambda h, c: (0, col0 + h_count + h))
    v_spec = pl.BlockSpec((s, HEAD_DIM), lambda h, c: (0, col0 + 2 * h_count + h))
    b_spec = pl.BlockSpec((1, CHUNK, BAND), lambda h, c: (h, 0, 0))
    return q_spec, k_spec, v_spec, b_spec


def _ca_fwd(qkv, bias, n_heads, col0, *, name, ride=None):
    s = qkv.shape[0]
    nc = s // CHUNK
    scale = HEAD_DIM ** -0.5

    def body(q_ref, k_ref, v_ref, b_ref, o_ref, kp, vp):
        c = pl.program_id(1)

        @pl.when(c == 0)
        def _():
            _ca_load_padded(k_ref, v_ref, kp, vp, s)

        off = pl.multiple_of(c * CHUNK, CHUNK)
        w = _ca_weights(q_ref[...], kp[pl.ds(off, BAND), :], b_ref[0], off, scale)
        o_ref[...] = _dot_nn(w.astype(BF16), vp[pl.ds(off, BAND), :]).astype(o_ref.dtype)

    q_spec, k_spec, v_spec, b_spec = _ca_specs(n_heads, s, col0)
    return _call(
        body, [qkv, qkv, qkv, bias], name=name, grid=(n_heads, nc),
        in_specs=[q_spec, k_spec, v_spec, b_spec],
        out_specs=[pl.BlockSpec((CHUNK, HEAD_DIM), lambda h, c: (c, h))],
        out_shape=[jax.ShapeDtypeStruct((s, n_heads * HEAD_DIM), BF16)],
        scratch_shapes=[pltpu.VMEM((s + PAD, HEAD_DIM), BF16), pltpu.VMEM((s + PAD, HEAD_DIM), BF16)],
        sem=("parallel", "arbitrary"), ride=ride)[0]


def _ca_bwd(qkv, bias, dy, n_heads, col0, *, name, ride=None):
    s = qkv.shape[0]
    nc = s // CHUNK
    scale = HEAD_DIM ** -0.5

    def body(q_ref, k_ref, v_ref, b_ref, dy_ref, dq_ref, dk_ref, dv_ref, db_ref, kp, vp, dkp, dvp):
        c = pl.program_id(1)

        @pl.when(c == 0)
        def _():
            _ca_load_padded(k_ref, v_ref, kp, vp, s)
            dkp[...] = jnp.zeros_like(dkp)
            dvp[...] = jnp.zeros_like(dvp)
            db_ref[...] = jnp.zeros_like(db_ref)

        off = pl.multiple_of(c * CHUNK, CHUNK)
        q = q_ref[...]
        dyv = dy_ref[...]
        kb = kp[pl.ds(off, BAND), :]
        w = _ca_weights(q, kb, b_ref[0], off, scale)
        dw = _dot_nt(dyv, vp[pl.ds(off, BAND), :])
        dvp[pl.ds(off, BAND), :] += _dot_tn(w.astype(BF16), dyv)
        dz = w * (dw - jnp.sum(w * dw, axis=1, keepdims=True))
        db_ref[0] += dz
        dzs = (dz * scale).astype(BF16)
        dq_ref[...] = _dot_nn(dzs, kb).astype(dq_ref.dtype)
        dkp[pl.ds(off, BAND), :] += _dot_tn(dzs, q)

        @pl.when(c == nc - 1)
        def _():
            dk_ref[...] = dkp[pl.ds(PAD, s), :].astype(dk_ref.dtype)
            dv_ref[...] = dvp[pl.ds(PAD, s), :].astype(dv_ref.dtype)

    q_spec, k_spec, v_spec, b_spec = _ca_specs(n_heads, s, col0)
    blk = pl.BlockSpec((CHUNK, HEAD_DIM), lambda h, c: (c, h))
    full = pl.BlockSpec((s, HEAD_DIM), lambda h, c: (0, h))
    sd = jax.ShapeDtypeStruct((s, n_heads * HEAD_DIM), BF16)
    return _call(
        body, [qkv, qkv, qkv, bias, dy], name=name, grid=(n_heads, nc),
        in_specs=[q_spec, k_spec, v_spec, b_spec, blk],
        out_specs=[blk, full, full, b_spec],
        out_shape=[sd, sd, sd, jax.ShapeDtypeStruct((n_heads, CHUNK, BAND), F32)],
        scratch_shapes=[pltpu.VMEM((s + PAD, HEAD_DIM), BF16), pltpu.VMEM((s + PAD, HEAD_DIM), BF16),
                        pltpu.VMEM((s + PAD, HEAD_DIM), F32), pltpu.VMEM((s + PAD, HEAD_DIM), F32)],
        sem=("parallel", "arbitrary"), ride=ride)


EARLY = ("w_sb_out", "w_ca_out", "w_mix_out")
LATE = ("w_ffn_in", "w_ffn_out", "w_ple_gate", "w_ple_in")


def _step(x, p, target, small, comm):
    w = comm.w
    d = x.shape[1]
    n_sb = w["w_sb_out"].shape[0] // HEAD_DIM
    n_ca = w["w_ca_out"].shape[0] // HEAD_DIM
    qkv_cols = 3 * HEAD_DIM * (n_sb + n_ca)
    ca_col0 = 3 * n_sb
    both = (F32, BF16)

    h1 = _rms_fwd(x, small["g_mix"], name="rms_mix")
    qkv = _mm(h1, w["w_in"], "nn", (BF16,), name="proj_qkv", n=qkv_cols, ride=comm.gather(EARLY, False))
    gates = _mm(h1, w["w_in"], "nn", (F32,), name="proj_gates", n=2 * d, b_col_off=qkv_cols,
                ride=comm.gather(EARLY, True))
    bias = _band_bias(small["rel_bias"])
    y_sb = _sb_fwd(qkv, n_sb, 0, name="sb_fwd", ride=comm.gather(LATE, False))
    y_ca = _ca_fwd(qkv, bias, n_ca, ca_col0, name="ca_fwd", ride=comm.gather(LATE, True))
    o_sb = _mm(y_sb, w["w_sb_out"], "nn", (F32,), name="sb_out")
    o_ca = _mm(y_ca, w["w_ca_out"], "nn", (F32,), name="ca_out")
    merged = _gate_merge_fwd(gates, o_sb, o_ca, name="gate_merge")
    x1 = _mm(merged, w["w_mix_out"], "nn", (F32,), name="mix_out", resid=x)
    h2 = _rms_fwd(x1, small["g_ffn"], name="rms_ffn")
    gu = _mm(h2, w["w_ffn_in"], "nn", (F32,), name="ffn_in")
    act = _swiglu_fwd(gu, name="swiglu")
    x2 = _mm(act, w["w_ffn_out"], "nn", (F32,), name="ffn_out", resid=x1)
    h3 = _rms_fwd(x2, small["g_ple"], name="rms_ple")
    t = _mm(h3, w["w_ple_gate"], "nn", (F32,), name="ple_gate")
    pe = _mm(p, w["w_ple_in"], "nn", (F32,), name="ple_in")
    x3 = _ple_fwd(x2, t, pe, name="ple_add")

    gs = {}
    dx3, gs["g_final"], loss = _final_loss(x3, small["g_final"], target, name="final_loss")
    dt, dpe = _ple_bwd(dx3, t, pe, name="ple_bwd")
    comm.grad("w_ple_in", *_mm(p, dpe, "tn", both, name="dw_ple_in"))
    comm.grad("w_ple_gate", *_mm(h3, dt, "tn", both, name="dw_ple_gate"))
    ple = ("w_ple_in", "w_ple_gate")
    dh3 = _mm(dt, w["w_ple_gate"], "nt", (F32,), name="dh_ple", ride=comm.pair(ple))
    dx2, gs["g_ple"] = _rms_bwd(x2, small["g_ple"], dh3, dx3, name="rms_ple_bwd")
    comm.add(ple)
    comm.grad("w_ffn_out", *_mm(act, dx2, "tn", both, name="dw_ffn_out", ride=comm.chips(ple)))
    dact = _mm(dx2, w["w_ffn_out"], "nt", (F32,), name="dact", ride=comm.pair(("w_ffn_out",)))
    dg_ff, du_ff = _swiglu_bwd(dact, gu, name="swiglu_bwd")
    dgu = jnp.concatenate([dg_ff, du_ff], axis=1)
    comm.sum(ple)
    comm.add(("w_ffn_out",))
    comm.grad("w_ffn_in", *_mm(h2, dgu, "tn", both, name="dw_ffn_in",
                               ride=comm.share(ple, comm.chips(("w_ffn_out",)))))
    dh2 = _mm(dgu, w["w_ffn_in"], "nt", (F32,), name="dh_ffn", ride=comm.pair(("w_ffn_in",)))
    dx1, gs["g_ffn"] = _rms_bwd(x1, small["g_ffn"], dh2, dx2, name="rms_ffn_bwd")
    comm.add(("w_ffn_in",))
    comm.sum(("w_ffn_out",))
    comm.grad("w_mix_out", *_mm(merged, dx1, "tn", both, name="dw_mix_out", ride=comm.share(("w_ffn_out",))))
    dmerged = _mm(dx1, w["w_mix_out"], "nt", (F32,), name="dmerged", ride=comm.pair(("w_mix_out",)))
    dg_sb, dg_ca, do_sb, do_ca = _gate_merge_bwd(dmerged, gates, o_sb, o_ca, name="gate_merge_bwd")
    comm.add(("w_mix_out",))
    comm.grad("w_sb_out", *_mm(y_sb, do_sb, "tn", both, name="dw_sb_out"))
    comm.grad("w_ca_out", *_mm(y_ca, do_ca, "tn", both, name="dw_ca_out"))
    outs = ("w_sb_out", "w_ca_out")
    dy_sb = _mm(do_sb, w["w_sb_out"], "nt", (BF16,), name="dy_sb", ride=comm.pair(outs))
    dy_ca = _mm(do_ca, w["w_ca_out"], "nt", (BF16,), name="dy_ca")
    comm.add(outs)
    behind_sb = ("w_ffn_in", "w_mix_out") + outs
    dq_sb, dk_sb, dv_sb = _sb_bwd(qkv, dy_sb, n_sb, 0, name="sb_bwd", ride=comm.chips(behind_sb))
    comm.sum(behind_sb)
    dq_ca, dk_ca, dv_ca, dbias = _ca_bwd(qkv, bias, dy_ca, n_ca, ca_col0, name="ca_bwd", ride=comm.share(behind_sb))
    gs["rel_bias"] = _band_bias_grad(dbias)
    dproj = jnp.concatenate([dq_sb, dk_sb, dv_sb, dq_ca, dk_ca, dv_ca, dg_sb, dg_ca], axis=1)
    comm.grad("w_in", *_mm(h1, dproj, "tn", both, name="dw_in"))
    dh1 = _mm(dproj, w["w_in"], "nt", (F32,), name="dh_mix", ride=comm.pair(("w_in",)))
    grad_x, gs["g_mix"] = _rms_bwd(x, small["g_mix"], dh1, dx1, name="rms_mix_bwd")
    comm.add(("w_in",))
    _run(comm.chips(("w_in",)), name="rs_chips_w_in")
    comm.sum(("w_in",))
    _run(comm.share(("w_in",)), name="rs_share_w_in")
    return loss, grad_x, gs


def _position():
    x, y, c = lax.axis_index("x"), lax.axis_index("y"), lax.axis_index("c")
    chips = [(1 - x, y), (x, 1 - y), (1 - x, 1 - y)]
    return x, y, c, chips


def _aligned(v, m):
    return v if isinstance(v, int) else pl.multiple_of(v, m)


def _piece_dims(shape, axis):
    k, n = shape
    return (k // 2, n // N_CHIPS) if axis == 1 else (k // N_CHIPS // 2, n)


def _piece(ref, shape, axis, j, h):
    pr, pc = _piece_dims(shape, axis)
    if axis == 1:
        return ref.at[pl.ds(_aligned(h * pr, 16), pr), pl.ds(_aligned(j * pc, 128), pc)]
    return ref.at[pl.ds(_aligned((2 * j + h) * pr, 16), pr), :]


def _shard_half(ref, h):
    rows = ref.shape[0] // 2
    return ref.at[pl.ds(_aligned(h * rows, 16), rows), :]


def _remote(src, dst, send_sems, recv_sems, k, to):
    return pltpu.make_async_remote_copy(src_ref=src, dst_ref=dst, send_sem=send_sems.at[k],
                                        recv_sem=recv_sems.at[k], device_id=to, device_id_type=MESH)


def _prefetch_call(body, scalars, ins, in_specs, out_shape, out_specs, grid, *, name):
    spec = pltpu.PrefetchScalarGridSpec(num_scalar_prefetch=1, grid=grid, in_specs=in_specs, out_specs=out_specs)
    return pl.pallas_call(body, name=name, grid_spec=spec, out_shape=out_shape,
                          compiler_params=_cparams(("parallel",) * len(grid)))(scalars, *ins)


def _slab_tiles(pr, pc):
    tc = pc if pc <= 4096 else _pick(pc, (2048, 1024, 512, 256, 128))
    tr = next(t for t in (1024, 512, 256, 128, 64, 32, 16) if pr % t == 0 and t * tc <= 512 * 1024)
    return tr, tc


def _cast_place(w, axis, pos, *, name):
    ks, ns = w.shape
    shape = (ks, ns * N_CHIPS) if axis == 1 else (ks * N_CHIPS, ns)
    tr, tc = _slab_tiles(ks, ns)
    nr, nc = ks // tr, ns // tc

    def body(pos_ref, w_ref, o_ref):
        o_ref[...] = w_ref[...].astype(o_ref.dtype)

    if axis == 1:
        out_map = lambda i, j, pos_ref: (i, pos_ref[0] * nc + j)
    else:
        out_map = lambda i, j, pos_ref: (pos_ref[0] * nr + i, j)
    return _prefetch_call(body, pos, [w], [pl.BlockSpec((tr, tc), lambda i, j, pos_ref: (i, j))],
                          jax.ShapeDtypeStruct(shape, BF16), pl.BlockSpec((tr, tc), out_map), (nr, nc), name=name)


def _run(ride, *, name):
    if ride is None:
        return

    def body(o_ref):
        o_ref[...] = jnp.zeros_like(o_ref)

    _call(body, [], name=name, grid=(1,), in_specs=[], out_specs=[pl.BlockSpec((8, 128), lambda i: (0, 0))],
          out_shape=[jax.ShapeDtypeStruct((8, 128), F32)], ride=ride)


def _ride_gather(ride, w, n, axis, to_sibling):
    shape = w[n].shape

    def copies(ins, outs, send_sems, recv_sems, arriving):
        x, y, c, chips = _position()
        me = 2 * x + y
        out = []
        for k, (px, py) in enumerate(chips):
            to = (x, y, 1 - c) if to_sibling else (px, py, c)
            if arriving:
                lands = _piece(outs[0], shape, axis, 2 * px + py, 1 - c if to_sibling else c)
                out.append(_remote(lands, lands, send_sems, recv_sems, k, to))
            else:
                chip = 2 * px + py if to_sibling else me
                out.append(_remote(_piece(ins[0], shape, axis, chip, c), _piece(outs[0], shape, axis, chip, c),
                                   send_sems, recv_sems, k, to))
        return out

    def start(*refs):
        for cp in copies(*refs, arriving=False):
            cp.start()

    def finish(*refs):
        for cp in copies(*refs, arriving=True):
            cp.wait_recv()
        for cp in copies(*refs, arriving=False):
            cp.wait_send()

    ride.add([w[n]], [jax.ShapeDtypeStruct(shape, w[n].dtype)], {0: 0}, 3, start, finish,
             lambda outs: w.__setitem__(n, outs[0]))


def _ride_pair(ride, st, axis):
    shape = st["g16"].shape
    pr, pc = _piece_dims(shape, axis)

    def copies(ins, outs, send_sems, recv_sems):
        x, y, c, _ = _position()
        return [_remote(_piece(ins[0], shape, axis, j, 1 - c), outs[0].at[j], send_sems, recv_sems, j, (x, y, 1 - c))
                for j in range(N_CHIPS)]

    def start(*refs):
        for cp in copies(*refs):
            cp.start()

    def finish(*refs):
        for cp in copies(*refs):
            cp.wait()

    ride.add([st["g16"]], [jax.ShapeDtypeStruct((N_CHIPS, pr, pc), BF16)], {}, N_CHIPS, start, finish,
             lambda outs: st.__setitem__("sib", outs[0]))


def _ride_chips(ride, st):
    _, pr, pc = st["s16"].shape

    def copies(ins, outs, send_sems, recv_sems):
        x, y, c, chips = _position()
        return [_remote(ins[0].at[2 * px + py], outs[0].at[k], send_sems, recv_sems, k, (px, py, c))
                for k, (px, py) in enumerate(chips)]

    def start(*refs):
        for cp in copies(*refs):
            cp.start()

    def finish(*refs):
        for cp in copies(*refs):
            cp.wait()

    ride.add([st["s16"]], [jax.ShapeDtypeStruct((3, pr, pc), BF16)], {}, 3, start, finish,
             lambda outs: st.__setitem__("recv", outs[0]))


def _ride_share(ride, st):
    def sent(ins, outs, send_sems, recv_sems):
        x, y, c, _ = _position()
        return _remote(_shard_half(ins[0], c), _shard_half(outs[0], c), send_sems, recv_sems, 0, (x, y, 1 - c))

    def landed(ins, outs, send_sems, recv_sems):
        x, y, c, _ = _position()
        other = _shard_half(outs[0], 1 - c)
        return _remote(other, other, send_sems, recv_sems, 0, (x, y, 1 - c))

    def start(*refs):
        sent(*refs).start()

    def finish(*refs):
        landed(*refs).wait_recv()
        sent(*refs).wait_send()

    ride.add([st["shard"]], [jax.ShapeDtypeStruct(st["shard"].shape, F32)], {0: 0}, 1, start, finish,
             lambda outs: st.__setitem__("g", outs[0]))


def _pair_add(g32, sib, axis, pos, *, name):
    _, pr, pc = sib.shape
    tr, tc = _slab_tiles(pr, pc)
    nr, nc = pr // tr, pc // tc

    def body(pos_ref, g_ref, b_ref, o32_ref, o16_ref):
        r = g_ref[...] + b_ref[0].astype(F32)
        o32_ref[0] = r
        o16_ref[0] = r.astype(o16_ref.dtype)

    if axis == 1:
        g_map = lambda j, i, k, pos_ref: (pos_ref[1] * nr + i, j * nc + k)
    else:
        g_map = lambda j, i, k, pos_ref: ((2 * j + pos_ref[1]) * nr + i, k)
    blk = pl.BlockSpec((1, tr, tc), lambda j, i, k, pos_ref: (j, i, k))
    return _prefetch_call(body, pos, [g32, sib], [pl.BlockSpec((tr, tc), g_map), blk],
                          [jax.ShapeDtypeStruct(sib.shape, F32), jax.ShapeDtypeStruct(sib.shape, BF16)],
                          [blk, blk], (N_CHIPS, nr, nc), name=name)


def _chip_sum(s32, recv, pos, *, name):
    _, pr, pc = s32.shape
    tr, tc = _slab_tiles(pr, pc)
    nr, nc = pr // tr, pc // tc

    def body(pos_ref, m_ref, r_ref, o_ref):
        o_ref[...] = ((m_ref[0] + r_ref[0].astype(F32)) + r_ref[1].astype(F32)) + r_ref[2].astype(F32)

    return _prefetch_call(
        body, pos, [s32, recv],
        [pl.BlockSpec((1, tr, tc), lambda i, k, pos_ref: (pos_ref[0], i, k)),
         pl.BlockSpec((3, tr, tc), lambda i, k, pos_ref: (0, i, k))],
        jax.ShapeDtypeStruct((2 * pr, pc), F32),
        pl.BlockSpec((tr, tc), lambda i, k, pos_ref: (pos_ref[1] * nr + i, k)), (nr, nc), name=name)


class _Comm:
    def __init__(self, pos, w):
        self.pos, self.w, self.st = pos, w, {n: {} for n, _ in BIG}

    def gather(self, names, to_sibling):
        ride = _Ride()
        for n in names:
            _ride_gather(ride, self.w, n, AXIS[n], to_sibling)
        return ride

    def grad(self, n, g32, g16):
        self.st[n].update(g32=g32, g16=g16)

    def pair(self, names, ride=None):
        ride = _Ride() if ride is None else ride
        for n in names:
            _ride_pair(ride, self.st[n], AXIS[n])
        return ride

    def add(self, names):
        for n in names:
            st = self.st[n]
            st["s32"], st["s16"] = _pair_add(st["g32"], st["sib"], AXIS[n], self.pos, name="rs_add_" + n)

    def chips(self, names, ride=None):
        ride = _Ride() if ride is None else ride
        for n in names:
            _ride_chips(ride, self.st[n])
        return ride

    def sum(self, names):
        for n in names:
            st = self.st[n]
            st["shard"] = _chip_sum(st["s32"], st["recv"], self.pos, name="rs_sum_" + n)

    def share(self, names, ride=None):
        ride = _Ride() if ride is None else ride
        for n in names:
            _ride_share(ride, self.st[n])
        return ride

    def result(self, n):
        return self.st[n]["g"]


class _NoComm:
    def __init__(self, w):
        self.w, self.st = w, {}

    def grad(self, n, g32, g16):
        self.st[n] = (g32, g16)

    def result(self, n):
        return self.st[n]

    def add(self, names):
        pass

    sum = add

    def gather(self, names, to_sibling=False, ride=None):
        return None

    pair = chips = share = gather


def _small_all_reduce(vec, *, name):
    r = vec.shape[0]

    def body(vec_ref, out_ref, slots, send_sems, recv_sems):
        x, y, c, _ = _position()
        me = 4 * x + 2 * y + c
        slots[me] = vec_ref[...]
        sends = []
        for k in range(1, 8):
            to = (x ^ (k >> 2), y ^ ((k >> 1) & 1), c ^ (k & 1))
            cp = _remote(slots.at[me], slots.at[me], send_sems, recv_sems, k - 1, to)
            cp.start()
            sends.append(cp)
        for k in range(1, 8):
            frm = 4 * (x ^ (k >> 2)) + 2 * (y ^ ((k >> 1) & 1)) + (c ^ (k & 1))
            _remote(slots.at[frm], slots.at[frm], send_sems, recv_sems, k - 1, (x, y, c)).wait_recv()
        for cp in sends:
            cp.wait_send()
        total = slots[0]
        for d in range(1, 8):
            total = total + slots[d]
        out_ref[...] = total

    return pl.pallas_call(
        body, name=name,
        in_specs=[pl.BlockSpec(memory_space=pltpu.VMEM)], out_specs=pl.BlockSpec(memory_space=pltpu.VMEM),
        out_shape=jax.ShapeDtypeStruct((r, 128), F32),
        scratch_shapes=[pltpu.VMEM((8, r, 128), F32), pltpu.SemaphoreType.DMA((7,)), pltpu.SemaphoreType.DMA((7,))],
    )(vec)


def _adamw(w, g, m, v, *, name):
    r, c = w.shape
    tc = c if c <= 4096 else _pick(c, (2048, 1024, 512, 256, 128))
    tr = next(t for t in (512, 256, 128, 64, 32, 16, 8) if r % t == 0 and t * tc <= 256 * 1024)

    def body(w_ref, g_ref, m_ref, v_ref, d_ref, nm_ref, nv_ref):
        gv = g_ref[...]
        nm = ADAM_B1 * m_ref[...] + (1.0 - ADAM_B1) * gv
        nv = ADAM_B2 * v_ref[...] + (1.0 - ADAM_B2) * (gv * gv)
        m_hat = nm / (1.0 - ADAM_B1 ** ADAM_STEP)
        v_hat = nv / (1.0 - ADAM_B2 ** ADAM_STEP)
        d_ref[...] = -ADAM_LR * (m_hat / (jnp.sqrt(v_hat) + ADAM_EPS) + ADAM_WD * w_ref[...])
        nm_ref[...] = nm
        nv_ref[...] = nv

    blk = ((tr, tc), lambda i, j: (i, j))
    sd = jax.ShapeDtypeStruct((r, c), F32)
    return _ew(body, [w, g, m, v], [blk] * 4, [sd, sd, sd], [blk] * 3, (r // tr, c // tc), name=name)


BIG = (("w_in", 1), ("w_sb_out", 1), ("w_ca_out", 1), ("w_mix_out", 0), ("w_ffn_in", 1), ("w_ffn_out", 0),
       ("w_ple_in", 1), ("w_ple_gate", 0))
AXIS = dict(BIG)
SMALL = ("rel_bias", "g_mix", "g_ffn", "g_ple", "g_final")
ORDER = ("w_in", "w_sb_out", "w_ca_out", "w_mix_out", "rel_bias", "g_mix", "g_ffn", "g_ple", "g_final",
         "w_ffn_in", "w_ffn_out", "w_ple_in", "w_ple_gate")


def _pack(parts):
    flat = jnp.concatenate([a.reshape(-1) for a in parts])
    rows = -(-flat.shape[0] // 1024) * 8
    return jnp.pad(flat, (0, rows * 128 - flat.shape[0])).reshape(rows, 128)


def _unpack(packed, like):
    flat, out, at = packed.reshape(-1), [], 0
    for a in like:
        out.append(flat[at:at + a.size].reshape(a.shape))
        at += a.size
    return out


def kernel(x, p, w_in, w_sb_out, w_ca_out, w_mix_out, rel_bias, g_mix, g_ffn, g_ple, g_final, w_ffn_in, w_ffn_out, w_ple_in, w_ple_gate, loss_target, m_w_in, m_w_sb_out, m_w_ca_out, m_w_mix_out, m_rel_bias, m_g_mix, m_g_ffn, m_g_ple, m_g_final, m_w_ffn_in, m_w_ffn_out, m_w_ple_in, m_w_ple_gate, v_w_in, v_w_sb_out, v_w_ca_out, v_w_mix_out, v_rel_bias, v_g_mix, v_g_ffn, v_g_ple, v_g_final, v_w_ffn_in, v_w_ffn_out, v_w_ple_in, v_w_ple_gate):
    weights = dict(w_in=w_in, w_sb_out=w_sb_out, w_ca_out=w_ca_out, w_mix_out=w_mix_out, rel_bias=rel_bias,
                   g_mix=g_mix, g_ffn=g_ffn, g_ple=g_ple, g_final=g_final, w_ffn_in=w_ffn_in,
                   w_ffn_out=w_ffn_out, w_ple_in=w_ple_in, w_ple_gate=w_ple_gate)
    m_in = dict(w_in=m_w_in, w_sb_out=m_w_sb_out, w_ca_out=m_w_ca_out, w_mix_out=m_w_mix_out, rel_bias=m_rel_bias,
                g_mix=m_g_mix, g_ffn=m_g_ffn, g_ple=m_g_ple, g_final=m_g_final, w_ffn_in=m_w_ffn_in,
                w_ffn_out=m_w_ffn_out, w_ple_in=m_w_ple_in, w_ple_gate=m_w_ple_gate)
    v_in = dict(w_in=v_w_in, w_sb_out=v_w_sb_out, w_ca_out=v_w_ca_out, w_mix_out=v_w_mix_out, rel_bias=v_rel_bias,
                g_mix=v_g_mix, g_ffn=v_g_ffn, g_ple=v_g_ple, g_final=v_g_final, w_ffn_in=v_w_ffn_in,
                w_ffn_out=v_w_ffn_out, w_ple_in=v_w_ple_in, w_ple_gate=v_w_ple_gate)

    pos = jnp.stack([2 * lax.axis_index("x") + lax.axis_index("y"), lax.axis_index("c")]).astype(jnp.int32)
    placed = {n: _cast_place(weights[n][0], axis, pos, name="cast_" + n) for n, axis in BIG}
    comm = _Comm(pos, placed)
    _run(comm.gather(("w_in",), False), name="gather_w_in_chips")
    _run(comm.gather(("w_in",), True), name="gather_w_in_pair")
    small = dict(rel_bias=rel_bias[0], g_mix=g_mix, g_ffn=g_ffn, g_ple=g_ple, g_final=g_final.reshape(1, -1))
    loss, grad_x, gs = _step(x[0], p[0, 0], loss_target[0], small, comm)

    grads, delta, new_m, new_v = {}, {}, {}, {}
    for n, _ in BIG:
        g = comm.result(n)
        d, nm, nv = _adamw(weights[n][0], g, m_in[n][0], v_in[n][0], name="adamw_" + n)
        grads[n], delta[n], new_m[n], new_v[n] = g[None], d[None], nm[None], nv[None]

    like = [weights[n] for n in SMALL]
    reduced = _small_all_reduce(_pack([gs[n] for n in SMALL] + [loss[:, :1]]), name="small_all_reduce")
    g_small = _unpack(reduced, like + [loss[:, :1]])
    total_loss = g_small[-1].reshape(())
    g_packed = _pack(g_small[:-1])
    d_s, m_s, v_s = _adamw(_pack(like), g_packed, _pack([m_in[n] for n in SMALL]), _pack([v_in[n] for n in SMALL]),
                           name="adamw_small")
    for n, g, d, nm, nv in zip(SMALL, g_small[:-1], _unpack(d_s, like), _unpack(m_s, like), _unpack(v_s, like)):
        grads[n], delta[n], new_m[n], new_v[n] = g, d, nm, nv

    return (total_loss, grad_x[None], *[grads[n] for n in ORDER], *[delta[n] for n in ORDER],
            *[new_m[n] for n in ORDER], *[new_v[n] for n in ORDER])
```

```python
import functools
import math

import jax
import jax.numpy as jnp
import numpy as np
from jax import lax
from jax.experimental import pallas as pl
from jax.experimental.pallas import tpu as pltpu

F32 = jnp.float32
BF16 = jnp.bfloat16

HEAD_DIM = 128
CHUNK = 64
LEFT_CHUNKS = 8
REL_CLIP = 128
N_REL = REL_CLIP + CHUNK
BAND = (LEFT_CHUNKS + 2) * CHUNK
PAD = (LEFT_CHUNKS + 1) * CHUNK
CA_PER_STEP = 4
CA_ROWS = CA_PER_STEP * CHUNK
SB_BLOCK = 128
SB_KEYS = 512
SB_GROUPS = SB_KEYS // SB_BLOCK
EPS = 1e-6
NEG = -1e30

ADAM_LR = 0.001
ADAM_B1 = 0.9
ADAM_B2 = 0.999
ADAM_EPS = 1e-08
ADAM_WD = 0.01
ADAM_STEP = 10

VMEM_LIMIT = 48 * 1024 * 1024
MM_VMEM_BUDGET = 36 * 1024 * 1024
V7X_HBM_BYTES_PER_S = 3.7e12
GRID_STEP_S = 0.35e-6
MESH = pl.DeviceIdType.MESH
N_CHIPS = 4


def _pick(dim, prefs):
    for t in prefs:
        if dim % t == 0:
            return t
    raise ValueError(f"no tile for {dim}")


def _cparams(sem=None):
    return pltpu.CompilerParams(dimension_semantics=sem, vmem_limit_bytes=VMEM_LIMIT)


def _sigmoid(v):
    return 1.0 / (1.0 + jnp.exp(-v))


def _dot(a, b, dims):
    return lax.dot_general(a, b, (dims, ((), ())), preferred_element_type=F32)


def _dot_nn(a, b):
    return _dot(a, b, ((1,), (0,)))


def _dot_nt(a, b):
    return _dot(a, b, ((1,), (1,)))


def _dot_tn(a, b):
    return _dot(a, b, ((0,), (0,)))


HBM = pl.BlockSpec(memory_space=pltpu.HBM)


class _Ride:
    def __init__(self):
        self.items = []

    def add(self, ins, outs, aliases, n_sems, start, finish, sink):
        self.items.append((ins, outs, aliases, n_sems, start, finish, sink))


def _call(body, args, *, name, grid, in_specs, out_specs, out_shape, scratch_shapes=(), sem=None, ride=None):
    items = ride.items if ride is not None else []
    n_in, n_out, n_scr = len(args), len(out_shape), len(scratch_shapes)
    r_ins = [a for it in items for a in it[0]]
    r_outs = [o for it in items for o in it[1]]
    aliases, a, b = {}, n_in, n_out
    for it in items:
        aliases.update({a + i: b + o for i, o in it[2].items()})
        a, b = a + len(it[0]), b + len(it[1])
    sems = [pltpu.SemaphoreType.DMA((it[3],)) for it in items for _ in range(2)]

    def wrapped(*refs):
        ins, rin = refs[:n_in], refs[n_in:n_in + len(r_ins)]
        at = n_in + len(r_ins)
        outs, rout = refs[at:at + n_out], refs[at + n_out:at + n_out + len(r_outs)]
        at += n_out + len(r_outs)
        scr, rsem = refs[at:at + n_scr], refs[at + n_scr:]

        def each(which):
            a = b = 0
            for q, it in enumerate(items):
                it[which](rin[a:a + len(it[0])], rout[b:b + len(it[1])], rsem[2 * q], rsem[2 * q + 1])
                a, b = a + len(it[0]), b + len(it[1])

        if items:
            ids = [pl.program_id(d) for d in range(len(grid))]
            first = functools.reduce(jnp.logical_and, [i == 0 for i in ids])
            last = functools.reduce(jnp.logical_and, [i == g - 1 for i, g in zip(ids, grid)])
            pl.when(first)(lambda: each(4))
        body(*ins, *outs, *scr)
        if items:
            pl.when(last)(lambda: each(5))

    res = pl.pallas_call(
        wrapped, name=name, grid=grid,
        in_specs=list(in_specs) + [HBM] * len(r_ins),
        out_specs=list(out_specs) + [HBM] * len(r_outs),
        out_shape=list(out_shape) + r_outs,
        scratch_shapes=list(scratch_shapes) + sems,
        input_output_aliases=aliases,
        compiler_params=_cparams(("arbitrary",) * len(grid) if items else sem),
    )(*args, *r_ins)
    b = n_out
    for it in items:
        it[6](res[b:b + len(it[1])])
        b += len(it[1])
    return list(res[:n_out])


def _mm_tiles(m, n_align, n, k, a_bytes, b_bytes, out_bytes):
    best = None
    tks = sorted({t for t in (k, k // 2, k // 4, 2048, 1024, 512, 256, 128) if t <= k and k % t == 0 and t % 128 == 0})
    for tm in (t for t in (2048, 1024, 512, 256, 128) if m % t == 0):
        for tn in (t for t in (2048, 1024, 512, 256, 128) if n_align % t == 0):
            for tk in tks:
                nk = k // tk
                vmem = 2 * (tm * tk * a_bytes + tk * tn * b_bytes + tm * tn * out_bytes) + tm * tn * 4
                if vmem > MM_VMEM_BUDGET:
                    continue
                traffic = m * k * a_bytes * (n // tn if nk > 1 else 1) + k * n * b_bytes * (m // tm)
                traffic += tm * tk * a_bytes + tk * tn * b_bytes + tm * tn * out_bytes
                traffic += m * n * 4 * nk if nk > 1 else 0
                cost = traffic / V7X_HBM_BYTES_PER_S + (m // tm) * (n // tn) * nk * GRID_STEP_S
                if best is None or cost < best[0]:
                    best = (cost, tm, tn, tk)
    return best[1:]


def _mm(a, b, mode, out_dtypes, *, name, n=None, b_col_off=0, resid=None, ride=None):
    if mode == "nn":
        m, k = a.shape
        n = b.shape[1] if n is None else n
    elif mode == "nt":
        m, k = a.shape
        n = b.shape[0]
    else:
        k, m = a.shape
        n = b.shape[1]
    n_out = len(out_dtypes)
    has_resid = resid is not None
    out_bytes = sum(jnp.dtype(dt).itemsize for dt in out_dtypes) + (4 if has_resid else 0)
    tm, tn, tk = _mm_tiles(m, math.gcd(n, b_col_off) if b_col_off else n, n, k,
                           a.dtype.itemsize, b.dtype.itemsize, out_bytes)
    nk = k // tk
    boff = b_col_off // tn
    dot = {"nn": _dot_nn, "nt": _dot_nt, "tn": _dot_tn}[mode]

    def body(*refs):
        a_ref, b_ref = refs[0], refs[1]
        r_ref = refs[2] if has_resid else None
        o_refs = refs[2 + has_resid: 2 + has_resid + n_out]

        def finish(r):
            if has_resid:
                r = r + r_ref[...]
            for o_ref in o_refs:
                o_ref[...] = r.astype(o_ref.dtype)

        part = dot(a_ref[...].astype(BF16), b_ref[...].astype(BF16))
        if nk == 1:
            finish(part)
            return
        acc_ref = refs[-1]
        kk = pl.program_id(2)

        @pl.when(kk == 0)
        def _():
            acc_ref[...] = part

        @pl.when(kk > 0)
        def _():
            acc_ref[...] += part

        @pl.when(kk == nk - 1)
        def _():
            finish(acc_ref[...])

    if mode == "nn":
        a_spec = pl.BlockSpec((tm, tk), lambda i, j, kk: (i, kk))
        b_spec = pl.BlockSpec((tk, tn), lambda i, j, kk: (kk, j + boff))
    elif mode == "nt":
        a_spec = pl.BlockSpec((tm, tk), lambda i, j, kk: (i, kk))
        b_spec = pl.BlockSpec((tn, tk), lambda i, j, kk: (j, kk))
    else:
        a_spec = pl.BlockSpec((tk, tm), lambda i, j, kk: (kk, i))
        b_spec = pl.BlockSpec((tk, tn), lambda i, j, kk: (kk, j))
    o_spec = pl.BlockSpec((tm, tn), lambda i, j, kk: (i, j))
    in_specs = [a_spec, b_spec] + ([o_spec] if has_resid else [])
    args = [a, b] + ([resid] if has_resid else [])
    outs = _call(
        body, args, name=name,
        grid=(m // tm, n // tn, nk),
        in_specs=in_specs,
        out_specs=[o_spec] * n_out,
        out_shape=[jax.ShapeDtypeStruct((m, n), dt) for dt in out_dtypes],
        scratch_shapes=[pltpu.VMEM((tm, tn), F32)] if nk > 1 else [],
        sem=("parallel", "parallel", "arbitrary"), ride=ride)
    return outs[0] if n_out == 1 else tuple(outs)


def _row_tile(s):
    return _pick(s, (256, 128))


def _rms_fwd(x, g, *, name):
    s, d = x.shape
    tr = _row_tile(s)

    def body(x_ref, g_ref, o_ref):
        xv = x_ref[...]
        r = lax.rsqrt(jnp.mean(xv * xv, axis=1, keepdims=True) + EPS)
        o_ref[...] = (xv * r * g_ref[...]).astype(o_ref.dtype)

    return pl.pallas_call(
        body, name=name, grid=(s // tr,),
        in_specs=[pl.BlockSpec((tr, d), lambda i: (i, 0)), pl.BlockSpec((1, d), lambda i: (0, 0))],
        out_specs=pl.BlockSpec((tr, d), lambda i: (i, 0)),
        out_shape=jax.ShapeDtypeStruct((s, d), BF16),
        compiler_params=_cparams(("parallel",)),
    )(x, g)


def _rms_bwd(x, g, dh, dres, *, name):
    s, d = x.shape
    tr = _row_tile(s)

    def body(x_ref, g_ref, dh_ref, dres_ref, dx_ref, dg_ref):
        i = pl.program_id(0)
        xv = x_ref[...]
        r = lax.rsqrt(jnp.mean(xv * xv, axis=1, keepdims=True) + EPS)
        xhat = xv * r
        dhv = dh_ref[...]
        dxhat = dhv * g_ref[...]
        proj = jnp.mean(dxhat * xhat, axis=1, keepdims=True)
        dx_ref[...] = dres_ref[...] + r * (dxhat - xhat * proj)

        @pl.when(i == 0)
        def _():
            dg_ref[...] = jnp.zeros_like(dg_ref)

        dg_ref[...] += jnp.sum(dhv * xhat, axis=0, keepdims=True)

    row = pl.BlockSpec((tr, d), lambda i: (i, 0))
    vec = pl.BlockSpec((1, d), lambda i: (0, 0))
    return pl.pallas_call(
        body, name=name, grid=(s // tr,),
        in_specs=[row, vec, row, row],
        out_specs=[row, vec],
        out_shape=[jax.ShapeDtypeStruct((s, d), F32), jax.ShapeDtypeStruct((1, d), F32)],
        compiler_params=_cparams(("arbitrary",)),
    )(x, g, dh, dres)


def _final_loss(x, g, target, *, name):
    s, d = x.shape
    tr = _row_tile(s)

    def body(x_ref, g_ref, t_ref, dx_ref, dg_ref, loss_ref):
        i = pl.program_id(0)
        xv = x_ref[...]
        gv = g_ref[...]
        r = lax.rsqrt(jnp.mean(xv * xv, axis=1, keepdims=True) + EPS)
        xhat = xv * r
        err = xhat * gv - t_ref[...]
        dy = err * (1.0 / d)
        dxhat = dy * gv
        proj = jnp.mean(dxhat * xhat, axis=1, keepdims=True)
        dx_ref[...] = r * (dxhat - xhat * proj)

        @pl.when(i == 0)
        def _():
            dg_ref[...] = jnp.zeros_like(dg_ref)
            loss_ref[...] = jnp.zeros_like(loss_ref)

        dg_ref[...] += jnp.sum(dy * xhat, axis=0, keepdims=True)
        part = 0.5 * jnp.sum(jnp.mean(err * err, axis=1, keepdims=True), axis=0, keepdims=True)
        loss_ref[...] += jnp.broadcast_to(part, loss_ref.shape)

    row = pl.BlockSpec((tr, d), lambda i: (i, 0))
    vec = pl.BlockSpec((1, d), lambda i: (0, 0))
    return pl.pallas_call(
        body, name=name, grid=(s // tr,),
        in_specs=[row, vec, row],
        out_specs=[row, vec, pl.BlockSpec((1, 128), lambda i: (0, 0))],
        out_shape=[jax.ShapeDtypeStruct((s, d), F32), jax.ShapeDtypeStruct((1, d), F32),
                   jax.ShapeDtypeStruct((1, 128), F32)],
        compiler_params=_cparams(("arbitrary",)),
    )(x, g, target)


def _ew(body, ins, in_blocks, outs, out_blocks, grid, *, name):
    return pl.pallas_call(
        body, name=name, grid=grid,
        in_specs=[pl.BlockSpec(bs, im) for bs, im in in_blocks],
        out_specs=[pl.BlockSpec(bs, im) for bs, im in out_blocks],
        out_shape=outs,
        compiler_params=_cparams(("parallel",) * len(grid)),
    )(*ins)


def _gate_merge_fwd(gates, o_sb, o_ca, *, name):
    s, d = o_sb.shape
    tr, tc = _row_tile(s), _pick(d, (1024, 512, 256, 128))
    nc = d // tc

    def body(gs_ref, gc_ref, os_ref, oc_ref, m_ref):
        m = _sigmoid(gs_ref[...]) * os_ref[...] + _sigmoid(gc_ref[...]) * oc_ref[...]
        m_ref[...] = m.astype(m_ref.dtype)

    blk = ((tr, tc), lambda i, j: (i, j))
    return _ew(body, [gates, gates, o_sb, o_ca],
               [blk, ((tr, tc), lambda i, j: (i, j + nc)), blk, blk],
               [jax.ShapeDtypeStruct((s, d), BF16)], [blk], (s // tr, nc), name=name)[0]


def _gate_merge_bwd(dmerged, gates, o_sb, o_ca, *, name):
    s, d = o_sb.shape
    tr, tc = _row_tile(s), _pick(d, (1024, 512, 256, 128))
    nc = d // tc

    def body(dm_ref, gs_ref, gc_ref, os_ref, oc_ref, dgs_ref, dgc_ref, dos_ref, doc_ref):
        dm = dm_ref[...]
        ss = _sigmoid(gs_ref[...])
        sc = _sigmoid(gc_ref[...])
        dgs_ref[...] = (dm * os_ref[...] * ss * (1.0 - ss)).astype(dgs_ref.dtype)
        dgc_ref[...] = (dm * oc_ref[...] * sc * (1.0 - sc)).astype(dgc_ref.dtype)
        dos_ref[...] = (dm * ss).astype(dos_ref.dtype)
        doc_ref[...] = (dm * sc).astype(doc_ref.dtype)

    blk = ((tr, tc), lambda i, j: (i, j))
    sd = jax.ShapeDtypeStruct((s, d), BF16)
    return _ew(body, [dmerged, gates, gates, o_sb, o_ca],
               [blk, blk, ((tr, tc), lambda i, j: (i, j + nc)), blk, blk],
               [sd, sd, sd, sd], [blk, blk, blk, blk], (s // tr, nc), name=name)


def _swiglu_fwd(gu, *, name):
    s, f2 = gu.shape
    f = f2 // 2
    tr, tc = _row_tile(s), _pick(f, (512, 256, 128))
    nc = f // tc

    def body(g_ref, u_ref, a_ref):
        gv = g_ref[...]
        a_ref[...] = (gv * _sigmoid(gv) * u_ref[...]).astype(a_ref.dtype)

    blk = ((tr, tc), lambda i, j: (i, j))
    return _ew(body, [gu, gu], [blk, ((tr, tc), lambda i, j: (i, j + nc))],
               [jax.ShapeDtypeStruct((s, f), BF16)], [blk], (s // tr, nc), name=name)[0]


def _swiglu_bwd(dact, gu, *, name):
    s, f2 = gu.shape
    f = f2 // 2
    tr, tc = _row_tile(s), _pick(f, (512, 256, 128))
    nc = f // tc

    def body(da_ref, g_ref, u_ref, dg_ref, du_ref):
        da = da_ref[...]
        gv = g_ref[...]
        sg = _sigmoid(gv)
        dg_ref[...] = (da * u_ref[...] * sg * (1.0 + gv * (1.0 - sg))).astype(dg_ref.dtype)
        du_ref[...] = (da * gv * sg).astype(du_ref.dtype)

    blk = ((tr, tc), lambda i, j: (i, j))
    hi = ((tr, tc), lambda i, j: (i, j + nc))
    sd = jax.ShapeDtypeStruct((s, f), BF16)
    dg, du = _ew(body, [dact, gu, gu], [blk, blk, hi], [sd, sd], [blk, blk], (s // tr, nc), name=name)
    return dg, du


def _ple_fwd(x, t, pe, *, name):
    s, d = x.shape
    tr, tc = _row_tile(s), _pick(d, (1024, 512, 256, 128))

    def body(x_ref, t_ref, p_ref, o_ref):
        o_ref[...] = x_ref[...] + _sigmoid(t_ref[...]) * p_ref[...]

    blk = ((tr, tc), lambda i, j: (i, j))
    return _ew(body, [x, t, pe], [blk, blk, blk],
               [jax.ShapeDtypeStruct((s, d), F32)], [blk], (s // tr, d // tc), name=name)[0]


def _ple_bwd(dx, t, pe, *, name):
    s, d = dx.shape
    tr, tc = _row_tile(s), _pick(d, (1024, 512, 256, 128))

    def body(dx_ref, t_ref, p_ref, dt_ref, dp_ref):
        dxv = dx_ref[...]
        sg = _sigmoid(t_ref[...])
        dt_ref[...] = (dxv * p_ref[...] * sg * (1.0 - sg)).astype(dt_ref.dtype)
        dp_ref[...] = (dxv * sg).astype(dp_ref.dtype)

    blk = ((tr, tc), lambda i, j: (i, j))
    sd = jax.ShapeDtypeStruct((s, d), BF16)
    return _ew(body, [dx, t, pe], [blk, blk, blk], [sd, sd], [blk, blk], (s // tr, d // tc), name=name)


def _sb_tri(later):
    row = lax.broadcasted_iota(jnp.int32, (SB_BLOCK, SB_BLOCK), 0)
    col = lax.broadcasted_iota(jnp.int32, (SB_BLOCK, SB_BLOCK), 1)
    tri = (row > col) if later else (row < col)
    return jnp.concatenate([tri.astype(BF16), jnp.ones((SB_BLOCK, SB_BLOCK), BF16)], axis=1)


def _sb_valid(i, j):
    qi = i * SB_BLOCK + lax.broadcasted_iota(jnp.int32, (SB_BLOCK, SB_KEYS), 0)
    ki = j * SB_KEYS + lax.broadcasted_iota(jnp.int32, (SB_BLOCK, SB_KEYS), 1)
    return ki < qi


def _sb_scan(v, tri, run, later):
    hi = v.astype(BF16)
    lo = (v - hi.astype(F32)).astype(BF16)
    outs = [None] * SB_GROUPS
    for b in (reversed(range(SB_GROUPS)) if later else range(SB_GROUPS)):
        cols = slice(b * SB_BLOCK, (b + 1) * SB_BLOCK)
        r = _dot_nn(hi[:, cols], tri) + _dot_nn(lo[:, cols], tri)
        outs[b] = r[:, :SB_BLOCK] + run
        run = run + r[:, SB_BLOCK:]
    return jnp.concatenate(outs, axis=1), run


def _sb_scores(q, kj, scale, valid):
    z = _dot_nt(q, kj) * scale
    t = jnp.log(1.0 + jnp.exp(-jnp.abs(z)))
    return jnp.minimum(z, 0.0) - t, jnp.where(valid, -jnp.maximum(z, 0.0) - t, 0.0)


def _sb_specs(h_count, s, col0):
    q_spec = pl.BlockSpec((SB_BLOCK, HEAD_DIM), lambda h, i: (i, col0 + h))
    k_spec = pl.BlockSpec((s, HEAD_DIM), lambda h, i: (0, col0 + h_count + h))
    v_spec = pl.BlockSpec((s, HEAD_DIM), lambda h, i: (0, col0 + 2 * h_count + h))
    return q_spec, k_spec, v_spec


def _sb_fwd(qkv, n_heads, col0, *, name, ride=None):
    s = qkv.shape[0]
    nq = s // SB_BLOCK
    scale = HEAD_DIM ** -0.5

    def body(q_ref, k_ref, v_ref, o_ref):
        i = pl.program_id(1)
        steps = i // SB_GROUPS + 1
        q = q_ref[...]
        tri = _sb_tri(later=True)

        def step(jj, carry):
            run, acc = carry
            j = steps - 1 - jj
            off = pl.multiple_of(j * SB_KEYS, SB_KEYS)
            valid = _sb_valid(i, j)
            ls, lk = _sb_scores(q, k_ref[pl.ds(off, SB_KEYS), :], scale, valid)
            between, run = _sb_scan(lk, tri, run, later=True)
            a = jnp.where(valid, jnp.exp(ls + between), 0.0)
            return run, acc + _dot_nn(a.astype(BF16), v_ref[pl.ds(off, SB_KEYS), :])

        init = (jnp.zeros((SB_BLOCK, SB_BLOCK), F32), jnp.zeros((SB_BLOCK, HEAD_DIM), F32))
        _, acc = lax.fori_loop(0, steps, step, init)
        o_ref[...] = acc.astype(o_ref.dtype)

    q_spec, k_spec, v_spec = _sb_specs(n_heads, s, col0)
    return _call(
        body, [qkv, qkv, qkv], name=name, grid=(n_heads, nq),
        in_specs=[q_spec, k_spec, v_spec],
        out_specs=[pl.BlockSpec((SB_BLOCK, HEAD_DIM), lambda h, i: (i, h))],
        out_shape=[jax.ShapeDtypeStruct((s, n_heads * HEAD_DIM), BF16)],
        sem=("parallel", "arbitrary"), ride=ride)[0]


def _sb_bwd(qkv, dy, n_heads, col0, *, name, ride=None):
    s = qkv.shape[0]
    nq = s // SB_BLOCK
    scale = HEAD_DIM ** -0.5

    def body(q_ref, k_ref, v_ref, dy_ref, dq_ref, dk_ref, dv_ref, e_scr, dk_acc, dv_acc):
        i = pl.program_id(1)
        steps = i // SB_GROUPS + 1
        q = q_ref[...]
        dyv = dy_ref[...]

        @pl.when(i == 0)
        def _():
            dk_acc[...] = jnp.zeros_like(dk_acc)
            dv_acc[...] = jnp.zeros_like(dv_acc)

        tri_later = _sb_tri(later=True)

        def pass1(jj, run):
            j = steps - 1 - jj
            off = pl.multiple_of(j * SB_KEYS, SB_KEYS)
            valid = _sb_valid(i, j)
            ls, lk = _sb_scores(q, k_ref[pl.ds(off, SB_KEYS), :], scale, valid)
            between, run = _sb_scan(lk, tri_later, run, later=True)
            a = jnp.where(valid, jnp.exp(ls + between), 0.0)
            e_scr[j] = a * _dot_nt(dyv, v_ref[pl.ds(off, SB_KEYS), :])
            dv_acc[pl.ds(off, SB_KEYS), :] += _dot_tn(a.astype(BF16), dyv)
            return run

        lax.fori_loop(0, steps, pass1, jnp.zeros((SB_BLOCK, SB_BLOCK), F32))

        tri_earlier = _sb_tri(later=False)

        def pass2(j, carry):
            run, dq = carry
            off = pl.multiple_of(j * SB_KEYS, SB_KEYS)
            kj = k_ref[pl.ds(off, SB_KEYS), :]
            sg = _sigmoid(_dot_nt(q, kj) * scale)
            e = e_scr[j]
            before, run = _sb_scan(e, tri_earlier, run, later=False)
            dz = jnp.where(_sb_valid(i, j), e * (1.0 - sg) - sg * before, 0.0) * scale
            dzb = dz.astype(BF16)
            dk_acc[pl.ds(off, SB_KEYS), :] += _dot_tn(dzb, q)
            return run, dq + _dot_nn(dzb, kj)

        init = (jnp.zeros((SB_BLOCK, SB_BLOCK), F32), jnp.zeros((SB_BLOCK, HEAD_DIM), F32))
        _, dq = lax.fori_loop(0, steps, pass2, init)
        dq_ref[...] = dq.astype(dq_ref.dtype)

        @pl.when(i == nq - 1)
        def _():
            dk_ref[...] = dk_acc[...].astype(dk_ref.dtype)
            dv_ref[...] = dv_acc[...].astype(dv_ref.dtype)

    q_spec, k_spec, v_spec = _sb_specs(n_heads, s, col0)
    blk = pl.BlockSpec((SB_BLOCK, HEAD_DIM), lambda h, i: (i, h))
    full = pl.BlockSpec((s, HEAD_DIM), lambda h, i: (0, h))
    sd = jax.ShapeDtypeStruct((s, n_heads * HEAD_DIM), BF16)
    return _call(
        body, [qkv, qkv, qkv, dy], name=name, grid=(n_heads, nq),
        in_specs=[q_spec, k_spec, v_spec, blk],
        out_specs=[blk, full, full],
        out_shape=[sd, sd, sd],
        scratch_shapes=[pltpu.VMEM((s // SB_KEYS, SB_BLOCK, SB_KEYS), F32),
                        pltpu.VMEM((s, HEAD_DIM), F32), pltpu.VMEM((s, HEAD_DIM), F32)],
        sem=("parallel", "arbitrary"), ride=ride)


def _band_bias(rel_bias):
    h = rel_bias.shape[0]
    width = BAND + CHUNK
    first = width - 1 - N_REL
    line = jnp.concatenate([jnp.broadcast_to(rel_bias[:, :1], (h, first)), rel_bias], axis=1)
    tiled = jnp.broadcast_to(line[:, None, :], (h, CHUNK, width - 1)).reshape(h, CHUNK * (width - 1))
    skew = jnp.pad(tiled, ((0, 0), (0, CHUNK))).reshape(h, CHUNK, width)[:, ::-1, :BAND]
    seen = jnp.arange(BAND) >= CHUNK
    return jnp.where(seen[None, None, :], skew, NEG)


def _band_bias_grad(dbias):
    h = dbias.shape[0]
    width = BAND + CHUNK
    flipped = jnp.pad(dbias[:, ::-1, :], ((0, 0), (0, 0), (0, CHUNK)))
    skew = flipped.reshape(h, CHUNK * width)[:, :CHUNK * (width - 1)].reshape(h, CHUNK, width - 1)
    diag = jnp.sum(skew, axis=1)
    first = width - 1 - N_REL
    clipped = jnp.sum(diag[:, :first + 1], axis=1, keepdims=True)
    return jnp.concatenate([clipped, diag[:, first + 1:]], axis=1)


def _ca_load_padded(k_ref, v_ref, kp, vp, s):
    kp[pl.ds(0, PAD), :] = jnp.zeros((PAD, HEAD_DIM), kp.dtype)
    vp[pl.ds(0, PAD), :] = jnp.zeros((PAD, HEAD_DIM), vp.dtype)
    kp[pl.ds(PAD, s), :] = k_ref[...]
    vp[pl.ds(PAD, s), :] = v_ref[...]


def _ca_weights(q, kb, bias, off, scale):
    z = _dot_nt(q, kb) * scale + bias
    pos = off + lax.broadcasted_iota(jnp.int32, (CHUNK, BAND), 1)
    z = jnp.where(pos >= PAD, z, NEG)
    p = jnp.exp(z - jnp.max(z, axis=1, keepdims=True))
    return p / jnp.sum(p, axis=1, keepdims=True)


def _ca_specs(h_count, s, col0):
    q_spec = pl.BlockSpec((CA_ROWS, HEAD_DIM), lambda h, c: (c, col0 + h))
    k_spec = pl.BlockSpec((s, HEAD_DIM), lambda h, c: (0, col0 + h_count + h))
    v_spec = pl.BlockSpec((s, HEAD_DIM), lambda h, c: (0, col0 + 2 * h_count + h))
    b_spec = pl.BlockSpec((1, CHUNK, BAND), lambda h, c: (h, 0, 0))
    return q_spec, k_spec, v_spec, b_spec


def _ca_fwd(qkv, bias, n_heads, col0, *, name, ride=None):
    s = qkv.shape[0]
    nc = s // CA_ROWS
    scale = HEAD_DIM ** -0.5

    def body(q_ref, k_ref, v_ref, b_ref, o_ref, kp, vp):
        c = pl.program_id(1)

        @pl.when(c == 0)
        def _():
            _ca_load_padded(k_ref, v_ref, kp, vp, s)

        for u in range(CA_PER_STEP):
            rows = pl.ds(u * CHUNK, CHUNK)
            off = pl.multiple_of((c * CA_PER_STEP + u) * CHUNK, CHUNK)
            w = _ca_weights(q_ref[rows, :], kp[pl.ds(off, BAND), :], b_ref[0], off, scale)
            o_ref[rows, :] = _dot_nn(w.astype(BF16), vp[pl.ds(off, BAND), :]).astype(o_ref.dtype)

    q_spec, k_spec, v_spec, b_spec = _ca_specs(n_heads, s, col0)
    return _call(
        body, [qkv, qkv, qkv, bias], name=name, grid=(n_heads, nc),
        in_specs=[q_spec, k_spec, v_spec, b_spec],
        out_specs=[pl.BlockSpec((CA_ROWS, HEAD_DIM), lambda h, c: (c, h))],
        out_shape=[jax.ShapeDtypeStruct((s, n_heads * HEAD_DIM), BF16)],
        scratch_shapes=[pltpu.VMEM((s + PAD, HEAD_DIM), BF16), pltpu.VMEM((s + PAD, HEAD_DIM), BF16)],
        sem=("parallel", "arbitrary"), ride=ride)[0]


def _ca_bwd(qkv, bias, dy, n_heads, col0, *, name, ride=None):
    s = qkv.shape[0]
    nc = s // CA_ROWS
    scale = HEAD_DIM ** -0.5

    def body(q_ref, k_ref, v_ref, b_ref, dy_ref, dq_ref, dk_ref, dv_ref, db_ref, kp, vp, dkp, dvp):
        c = pl.program_id(1)

        @pl.when(c == 0)
        def _():
            _ca_load_padded(k_ref, v_ref, kp, vp, s)
            dkp[...] = jnp.zeros_like(dkp)
            dvp[...] = jnp.zeros_like(dvp)
            db_ref[...] = jnp.zeros_like(db_ref)

        dbias = jnp.zeros((CHUNK, BAND), F32)
        for u in range(CA_PER_STEP):
            rows = pl.ds(u * CHUNK, CHUNK)
            off = pl.multiple_of((c * CA_PER_STEP + u) * CHUNK, CHUNK)
            q = q_ref[rows, :]
            dyv = dy_ref[rows, :]
            kb = kp[pl.ds(off, BAND), :]
            w = _ca_weights(q, kb, b_ref[0], off, scale)
            dw = _dot_nt(dyv, vp[pl.ds(off, BAND), :])
            dvp[pl.ds(off, BAND), :] += _dot_tn(w.astype(BF16), dyv)
            dz = w * (dw - jnp.sum(w * dw, axis=1, keepdims=True))
            dbias = dbias + dz
            dzs = (dz * scale).astype(BF16)
            dq_ref[rows, :] = _dot_nn(dzs, kb).astype(dq_ref.dtype)
            dkp[pl.ds(off, BAND), :] += _dot_tn(dzs, q)
        db_ref[0] += dbias

        @pl.when(c == nc - 1)
        def _():
            dk_ref[...] = dkp[pl.ds(PAD, s), :].astype(dk_ref.dtype)
            dv_ref[...] = dvp[pl.ds(PAD, s), :].astype(dv_ref.dtype)

    q_spec, k_spec, v_spec, b_spec = _ca_specs(n_heads, s, col0)
    blk = pl.BlockSpec((CA_ROWS, HEAD_DIM), lambda h, c: (c, h))
    full = pl.BlockSpec((s, HEAD_DIM), lambda h, c: (0, h))
    sd = jax.ShapeDtypeStruct((s, n_heads * HEAD_DIM), BF16)
    return _call(
        body, [qkv, qkv, qkv, bias, dy], name=name, grid=(n_heads, nc),
        in_specs=[q_spec, k_spec, v_spec, b_spec, blk],
        out_specs=[blk, full, full, b_spec],
        out_shape=[sd, sd, sd, jax.ShapeDtypeStruct((n_heads, CHUNK, BAND), F32)],
        scratch_shapes=[pltpu.VMEM((s + PAD, HEAD_DIM), BF16), pltpu.VMEM((s + PAD, HEAD_DIM), BF16),
                        pltpu.VMEM((s + PAD, HEAD_DIM), F32), pltpu.VMEM((s + PAD, HEAD_DIM), F32)],
        sem=("parallel", "arbitrary"), ride=ride)


EARLY = ("w_sb_out", "w_ca_out", "w_mix_out")
LATE = ("w_ffn_in", "w_ffn_out", "w_ple_gate", "w_ple_in")


def _step(x, p, target, small, comm):
    w = comm.w
    d = x.shape[1]
    n_sb = w["w_sb_out"].shape[0] // HEAD_DIM
    n_ca = w["w_ca_out"].shape[0] // HEAD_DIM
    qkv_cols = 3 * HEAD_DIM * (n_sb + n_ca)
    ca_col0 = 3 * n_sb
    both = (F32, BF16)

    h1 = _rms_fwd(x, small["g_mix"], name="rms_mix")
    qkv = _mm(h1, w["w_in"], "nn", (BF16,), name="proj_qkv", n=qkv_cols, ride=comm.gather(EARLY, False))
    gates = _mm(h1, w["w_in"], "nn", (F32,), name="proj_gates", n=2 * d, b_col_off=qkv_cols,
                ride=comm.gather(EARLY, True))
    bias = _band_bias(small["rel_bias"])
    y_sb = _sb_fwd(qkv, n_sb, 0, name="sb_fwd", ride=comm.gather(LATE, False))
    y_ca = _ca_fwd(qkv, bias, n_ca, ca_col0, name="ca_fwd", ride=comm.gather(LATE, True))
    o_sb = _mm(y_sb, w["w_sb_out"], "nn", (F32,), name="sb_out")
    o_ca = _mm(y_ca, w["w_ca_out"], "nn", (F32,), name="ca_out")
    merged = _gate_merge_fwd(gates, o_sb, o_ca, name="gate_merge")
    x1 = _mm(merged, w["w_mix_out"], "nn", (F32,), name="mix_out", resid=x)
    h2 = _rms_fwd(x1, small["g_ffn"], name="rms_ffn")
    gu = _mm(h2, w["w_ffn_in"], "nn", (F32,), name="ffn_in")
    act = _swiglu_fwd(gu, name="swiglu")
    x2 = _mm(act, w["w_ffn_out"], "nn", (F32,), name="ffn_out", resid=x1)
    h3 = _rms_fwd(x2, small["g_ple"], name="rms_ple")
    t = _mm(h3, w["w_ple_gate"], "nn", (F32,), name="ple_gate")
    pe = _mm(p, w["w_ple_in"], "nn", (F32,), name="ple_in")
    x3 = _ple_fwd(x2, t, pe, name="ple_add")

    gs = {}
    dx3, gs["g_final"], loss = _final_loss(x3, small["g_final"], target, name="final_loss")
    dt, dpe = _ple_bwd(dx3, t, pe, name="ple_bwd")
    comm.grad("w_ple_in", *_mm(p, dpe, "tn", both, name="dw_ple_in"))
    comm.grad("w_ple_gate", *_mm(h3, dt, "tn", both, name="dw_ple_gate"))
    ple = ("w_ple_in", "w_ple_gate")
    dh3 = _mm(dt, w["w_ple_gate"], "nt", (F32,), name="dh_ple", ride=comm.pair(ple))
    dx2, gs["g_ple"] = _rms_bwd(x2, small["g_ple"], dh3, dx3, name="rms_ple_bwd")
    comm.add(ple)
    comm.grad("w_ffn_out", *_mm(act, dx2, "tn", both, name="dw_ffn_out", ride=comm.chips(ple)))
    dact = _mm(dx2, w["w_ffn_out"], "nt", (F32,), name="dact", ride=comm.pair(("w_ffn_out",)))
    dg_ff, du_ff = _swiglu_bwd(dact, gu, name="swiglu_bwd")
    dgu = jnp.concatenate([dg_ff, du_ff], axis=1)
    comm.sum(ple)
    comm.add(("w_ffn_out",))
    comm.grad("w_ffn_in", *_mm(h2, dgu, "tn", both, name="dw_ffn_in",
                               ride=comm.share(ple, comm.chips(("w_ffn_out",)))))
    dh2 = _mm(dgu, w["w_ffn_in"], "nt", (F32,), name="dh_ffn", ride=comm.pair(("w_ffn_in",)))
    dx1, gs["g_ffn"] = _rms_bwd(x1, small["g_ffn"], dh2, dx2, name="rms_ffn_bwd")
    comm.add(("w_ffn_in",))
    comm.sum(("w_ffn_out",))
    comm.grad("w_mix_out", *_mm(merged, dx1, "tn", both, name="dw_mix_out", ride=comm.share(("w_ffn_out",))))
    dmerged = _mm(dx1, w["w_mix_out"], "nt", (F32,), name="dmerged", ride=comm.pair(("w_mix_out",)))
    dg_sb, dg_ca, do_sb, do_ca = _gate_merge_bwd(dmerged, gates, o_sb, o_ca, name="gate_merge_bwd")
    comm.add(("w_mix_out",))
    comm.grad("w_sb_out", *_mm(y_sb, do_sb, "tn", both, name="dw_sb_out"))
    comm.grad("w_ca_out", *_mm(y_ca, do_ca, "tn", both, name="dw_ca_out"))
    outs = ("w_sb_out", "w_ca_out")
    dy_sb = _mm(do_sb, w["w_sb_out"], "nt", (BF16,), name="dy_sb", ride=comm.pair(outs))
    dy_ca = _mm(do_ca, w["w_ca_out"], "nt", (BF16,), name="dy_ca")
    comm.add(outs)
    behind_sb = ("w_ffn_in", "w_mix_out") + outs
    dq_sb, dk_sb, dv_sb = _sb_bwd(qkv, dy_sb, n_sb, 0, name="sb_bwd", ride=comm.chips(behind_sb))
    comm.sum(behind_sb)
    dq_ca, dk_ca, dv_ca, dbias = _ca_bwd(qkv, bias, dy_ca, n_ca, ca_col0, name="ca_bwd", ride=comm.share(behind_sb))
    gs["rel_bias"] = _band_bias_grad(dbias)
    dproj = jnp.concatenate([dq_sb, dk_sb, dv_sb, dq_ca, dk_ca, dv_ca, dg_sb, dg_ca], axis=1)
    comm.grad("w_in", *_mm(h1, dproj, "tn", both, name="dw_in"))
    dh1 = _mm(dproj, w["w_in"], "nt", (F32,), name="dh_mix", ride=comm.pair(("w_in",)))
    grad_x, gs["g_mix"] = _rms_bwd(x, small["g_mix"], dh1, dx1, name="rms_mix_bwd")
    comm.add(("w_in",))
    _run(comm.chips(("w_in",)), name="rs_chips_w_in")
    comm.sum(("w_in",))
    _run(comm.share(("w_in",)), name="rs_share_w_in")
    return loss, grad_x, gs


def _position():
    x, y, c = lax.axis_index("x"), lax.axis_index("y"), lax.axis_index("c")
    chips = [(1 - x, y), (x, 1 - y), (1 - x, 1 - y)]
    return x, y, c, chips


def _aligned(v, m):
    return v if isinstance(v, int) else pl.multiple_of(v, m)


def _piece_dims(shape, axis):
    k, n = shape
    return (k // 2, n // N_CHIPS) if axis == 1 else (k // N_CHIPS // 2, n)


def _piece(ref, shape, axis, j, h):
    pr, pc = _piece_dims(shape, axis)
    if axis == 1:
        return ref.at[pl.ds(_aligned(h * pr, 16), pr), pl.ds(_aligned(j * pc, 128), pc)]
    return ref.at[pl.ds(_aligned((2 * j + h) * pr, 16), pr), :]


def _shard_half(ref, h):
    rows = ref.shape[0] // 2
    return ref.at[pl.ds(_aligned(h * rows, 16), rows), :]


def _remote(src, dst, send_sems, recv_sems, k, to):
    return pltpu.make_async_remote_copy(src_ref=src, dst_ref=dst, send_sem=send_sems.at[k],
                                        recv_sem=recv_sems.at[k], device_id=to, device_id_type=MESH)


def _prefetch_call(body, scalars, ins, in_specs, out_shape, out_specs, grid, *, name):
    spec = pltpu.PrefetchScalarGridSpec(num_scalar_prefetch=1, grid=grid, in_specs=in_specs, out_specs=out_specs)
    return pl.pallas_call(body, name=name, grid_spec=spec, out_shape=out_shape,
                          compiler_params=_cparams(("parallel",) * len(grid)))(scalars, *ins)


def _slab_tiles(pr, pc):
    tc = pc if pc <= 4096 else _pick(pc, (2048, 1024, 512, 256, 128))
    tr = next(t for t in (1024, 512, 256, 128, 64, 32, 16) if pr % t == 0 and t * tc <= 512 * 1024)
    return tr, tc


def _cast_place(w, axis, pos, *, name):
    ks, ns = w.shape
    shape = (ks, ns * N_CHIPS) if axis == 1 else (ks * N_CHIPS, ns)
    tr, tc = _slab_tiles(ks, ns)
    nr, nc = ks // tr, ns // tc

    def body(pos_ref, w_ref, o_ref):
        o_ref[...] = w_ref[...].astype(o_ref.dtype)

    if axis == 1:
        out_map = lambda i, j, pos_ref: (i, pos_ref[0] * nc + j)
    else:
        out_map = lambda i, j, pos_ref: (pos_ref[0] * nr + i, j)
    return _prefetch_call(body, pos, [w], [pl.BlockSpec((tr, tc), lambda i, j, pos_ref: (i, j))],
                          jax.ShapeDtypeStruct(shape, BF16), pl.BlockSpec((tr, tc), out_map), (nr, nc), name=name)


def _run(ride, *, name):
    if ride is None:
        return

    def body(o_ref):
        o_ref[...] = jnp.zeros_like(o_ref)

    _call(body, [], name=name, grid=(1,), in_specs=[], out_specs=[pl.BlockSpec((8, 128), lambda i: (0, 0))],
          out_shape=[jax.ShapeDtypeStruct((8, 128), F32)], ride=ride)


def _ride_gather(ride, w, n, axis, to_sibling):
    shape = w[n].shape

    def copies(ins, outs, send_sems, recv_sems, arriving):
        x, y, c, chips = _position()
        me = 2 * x + y
        out = []
        for k, (px, py) in enumerate(chips):
            to = (x, y, 1 - c) if to_sibling else (px, py, c)
            if arriving:
                lands = _piece(outs[0], shape, axis, 2 * px + py, 1 - c if to_sibling else c)
                out.append(_remote(lands, lands, send_sems, recv_sems, k, to))
            else:
                chip = 2 * px + py if to_sibling else me
                out.append(_remote(_piece(ins[0], shape, axis, chip, c), _piece(outs[0], shape, axis, chip, c),
                                   send_sems, recv_sems, k, to))
        return out

    def start(*refs):
        for cp in copies(*refs, arriving=False):
            cp.start()

    def finish(*refs):
        for cp in copies(*refs, arriving=True):
            cp.wait_recv()
        for cp in copies(*refs, arriving=False):
            cp.wait_send()

    ride.add([w[n]], [jax.ShapeDtypeStruct(shape, w[n].dtype)], {0: 0}, 3, start, finish,
             lambda outs: w.__setitem__(n, outs[0]))


def _ride_pair(ride, st, axis):
    shape = st["g16"].shape
    pr, pc = _piece_dims(shape, axis)

    def copies(ins, outs, send_sems, recv_sems):
        x, y, c, _ = _position()
        return [_remote(_piece(ins[0], shape, axis, j, 1 - c), outs[0].at[j], send_sems, recv_sems, j, (x, y, 1 - c))
                for j in range(N_CHIPS)]

    def start(*refs):
        for cp in copies(*refs):
            cp.start()

    def finish(*refs):
        for cp in copies(*refs):
            cp.wait()

    ride.add([st["g16"]], [jax.ShapeDtypeStruct((N_CHIPS, pr, pc), BF16)], {}, N_CHIPS, start, finish,
             lambda outs: st.__setitem__("sib", outs[0]))


def _ride_chips(ride, st):
    _, pr, pc = st["s16"].shape

    def copies(ins, outs, send_sems, recv_sems):
        x, y, c, chips = _position()
        return [_remote(ins[0].at[2 * px + py], outs[0].at[k], send_sems, recv_sems, k, (px, py, c))
                for k, (px, py) in enumerate(chips)]

    def start(*refs):
        for cp in copies(*refs):
            cp.start()

    def finish(*refs):
        for cp in copies(*refs):
            cp.wait()

    ride.add([st["s16"]], [jax.ShapeDtypeStruct((3, pr, pc), BF16)], {}, 3, start, finish,
             lambda outs: st.__setitem__("recv", outs[0]))


def _ride_share(ride, st):
    def sent(ins, outs, send_sems, recv_sems):
        x, y, c, _ = _position()
        return _remote(_shard_half(ins[0], c), _shard_half(outs[0], c), send_sems, recv_sems, 0, (x, y, 1 - c))

    def landed(ins, outs, send_sems, recv_sems):
        x, y, c, _ = _position()
        other = _shard_half(outs[0], 1 - c)
        return _remote(other, other, send_sems, recv_sems, 0, (x, y, 1 - c))

    def start(*refs):
        sent(*refs).start()

    def finish(*refs):
        landed(*refs).wait_recv()
        sent(*refs).wait_send()

    ride.add([st["shard"]], [jax.ShapeDtypeStruct(st["shard"].shape, F32)], {0: 0}, 1, start, finish,
             lambda outs: st.__setitem__("g", outs[0]))


def _pair_add(g32, sib, axis, pos, *, name):
    _, pr, pc = sib.shape
    tr, tc = _slab_tiles(pr, pc)
    nr, nc = pr // tr, pc // tc

    def body(pos_ref, g_ref, b_ref, o32_ref, o16_ref):
        r = g_ref[...] + b_ref[0].astype(F32)
        o32_ref[0] = r
        o16_ref[0] = r.astype(o16_ref.dtype)

    if axis == 1:
        g_map = lambda j, i, k, pos_ref: (pos_ref[1] * nr + i, j * nc + k)
    else:
        g_map = lambda j, i, k, pos_ref: ((2 * j + pos_ref[1]) * nr + i, k)
    blk = pl.BlockSpec((1, tr, tc), lambda j, i, k, pos_ref: (j, i, k))
    return _prefetch_call(body, pos, [g32, sib], [pl.BlockSpec((tr, tc), g_map), blk],
                          [jax.ShapeDtypeStruct(sib.shape, F32), jax.ShapeDtypeStruct(sib.shape, BF16)],
                          [blk, blk], (N_CHIPS, nr, nc), name=name)


def _chip_sum(s32, recv, pos, *, name):
    _, pr, pc = s32.shape
    tr, tc = _slab_tiles(pr, pc)
    nr, nc = pr // tr, pc // tc

    def body(pos_ref, m_ref, r_ref, o_ref):
        o_ref[...] = ((m_ref[0] + r_ref[0].astype(F32)) + r_ref[1].astype(F32)) + r_ref[2].astype(F32)

    return _prefetch_call(
        body, pos, [s32, recv],
        [pl.BlockSpec((1, tr, tc), lambda i, k, pos_ref: (pos_ref[0], i, k)),
         pl.BlockSpec((3, tr, tc), lambda i, k, pos_ref: (0, i, k))],
        jax.ShapeDtypeStruct((2 * pr, pc), F32),
        pl.BlockSpec((tr, tc), lambda i, k, pos_ref: (pos_ref[1] * nr + i, k)), (nr, nc), name=name)


class _Comm:
    def __init__(self, pos, w):
        self.pos, self.w, self.st = pos, w, {n: {} for n, _ in BIG}

    def gather(self, names, to_sibling):
        ride = _Ride()
        for n in names:
            _ride_gather(ride, self.w, n, AXIS[n], to_sibling)
        return ride

    def grad(self, n, g32, g16):
        self.st[n].update(g32=g32, g16=g16)

    def pair(self, names, ride=None):
        ride = _Ride() if ride is None else ride
        for n in names:
            _ride_pair(ride, self.st[n], AXIS[n])
        return ride

    def add(self, names):
        for n in names:
            st = self.st[n]
            st["s32"], st["s16"] = _pair_add(st["g32"], st["sib"], AXIS[n], self.pos, name="rs_add_" + n)

    def chips(self, names, ride=None):
        ride = _Ride() if ride is None else ride
        for n in names:
            _ride_chips(ride, self.st[n])
        return ride

    def sum(self, names):
        for n in names:
            st = self.st[n]
            st["shard"] = _chip_sum(st["s32"], st["recv"], self.pos, name="rs_sum_" + n)

    def share(self, names, ride=None):
        ride = _Ride() if ride is None else ride
        for n in names:
            _ride_share(ride, self.st[n])
        return ride

    def result(self, n):
        return self.st[n]["g"]


class _NoComm:
    def __init__(self, w):
        self.w, self.st = w, {}

    def grad(self, n, g32, g16):
        self.st[n] = (g32, g16)

    def result(self, n):
        return self.st[n]

    def add(self, names):
        pass

    sum = add

    def gather(self, names, to_sibling=False, ride=None):
        return None

    pair = chips = share = gather


def _small_all_reduce(vec, *, name):
    r = vec.shape[0]

    def body(vec_ref, out_ref, slots, send_sems, recv_sems):
        x, y, c, _ = _position()
        me = 4 * x + 2 * y + c
        slots[me] = vec_ref[...]
        sends = []
        for k in range(1, 8):
            to = (x ^ (k >> 2), y ^ ((k >> 1) & 1), c ^ (k & 1))
            cp = _remote(slots.at[me], slots.at[me], send_sems, recv_sems, k - 1, to)
            cp.start()
            sends.append(cp)
        for k in range(1, 8):
            frm = 4 * (x ^ (k >> 2)) + 2 * (y ^ ((k >> 1) & 1)) + (c ^ (k & 1))
            _remote(slots.at[frm], slots.at[frm], send_sems, recv_sems, k - 1, (x, y, c)).wait_recv()
        for cp in sends:
            cp.wait_send()
        total = slots[0]
        for d in range(1, 8):
            total = total + slots[d]
        out_ref[...] = total

    return pl.pallas_call(
        body, name=name,
        in_specs=[pl.BlockSpec(memory_space=pltpu.VMEM)], out_specs=pl.BlockSpec(memory_space=pltpu.VMEM),
        out_shape=jax.ShapeDtypeStruct((r, 128), F32),
        scratch_shapes=[pltpu.VMEM((8, r, 128), F32), pltpu.SemaphoreType.DMA((7,)), pltpu.SemaphoreType.DMA((7,))],
    )(vec)


def _adamw(w, g, m, v, *, name):
    r, c = w.shape
    tc = c if c <= 4096 else _pick(c, (2048, 1024, 512, 256, 128))
    tr = next(t for t in (512, 256, 128, 64, 32, 16, 8) if r % t == 0 and t * tc <= 256 * 1024)

    def body(w_ref, g_ref, m_ref, v_ref, d_ref, nm_ref, nv_ref):
        gv = g_ref[...]
        nm = ADAM_B1 * m_ref[...] + (1.0 - ADAM_B1) * gv
        nv = ADAM_B2 * v_ref[...] + (1.0 - ADAM_B2) * (gv * gv)
        m_hat = nm / (1.0 - ADAM_B1 ** ADAM_STEP)
        v_hat = nv / (1.0 - ADAM_B2 ** ADAM_STEP)
        d_ref[...] = -ADAM_LR * (m_hat / (jnp.sqrt(v_hat) + ADAM_EPS) + ADAM_WD * w_ref[...])
        nm_ref[...] = nm
        nv_ref[...] = nv

    blk = ((tr, tc), lambda i, j: (i, j))
    sd = jax.ShapeDtypeStruct((r, c), F32)
    return _ew(body, [w, g, m, v], [blk] * 4, [sd, sd, sd], [blk] * 3, (r // tr, c // tc), name=name)


BIG = (("w_in", 1), ("w_sb_out", 1), ("w_ca_out", 1), ("w_mix_out", 0), ("w_ffn_in", 1), ("w_ffn_out", 0),
       ("w_ple_in", 1), ("w_ple_gate", 0))
AXIS = dict(BIG)
SMALL = ("rel_bias", "g_mix", "g_ffn", "g_ple", "g_final")
ORDER = ("w_in", "w_sb_out", "w_ca_out", "w_mix_out", "rel_bias", "g_mix", "g_ffn", "g_ple", "g_final",
         "w_ffn_in", "w_ffn_out", "w_ple_in", "w_ple_gate")


def _pack(parts):
    flat = jnp.concatenate([a.reshape(-1) for a in parts])
    rows = -(-flat.shape[0] // 1024) * 8
    return jnp.pad(flat, (0, rows * 128 - flat.shape[0])).reshape(rows, 128)


def _unpack(packed, like):
    flat, out, at = packed.reshape(-1), [], 0
    for a in like:
        out.append(flat[at:at + a.size].reshape(a.shape))
        at += a.size
    return out


def kernel(x, p, w_in, w_sb_out, w_ca_out, w_mix_out, rel_bias, g_mix, g_ffn, g_ple, g_final, w_ffn_in, w_ffn_out, w_ple_in, w_ple_gate, loss_target, m_w_in, m_w_sb_out, m_w_ca_out, m_w_mix_out, m_rel_bias, m_g_mix, m_g_ffn, m_g_ple, m_g_final, m_w_ffn_in, m_w_ffn_out, m_w_ple_in, m_w_ple_gate, v_w_in, v_w_sb_out, v_w_ca_out, v_w_mix_out, v_rel_bias, v_g_mix, v_g_ffn, v_g_ple, v_g_final, v_w_ffn_in, v_w_ffn_out, v_w_ple_in, v_w_ple_gate):
    weights = dict(w_in=w_in, w_sb_out=w_sb_out, w_ca_out=w_ca_out, w_mix_out=w_mix_out, rel_bias=rel_bias,
                   g_mix=g_mix, g_ffn=g_ffn, g_ple=g_ple, g_final=g_final, w_ffn_in=w_ffn_in,
                   w_ffn_out=w_ffn_out, w_ple_in=w_ple_in, w_ple_gate=w_ple_gate)
    m_in = dict(w_in=m_w_in, w_sb_out=m_w_sb_out, w_ca_out=m_w_ca_out, w_mix_out=m_w_mix_out, rel_bias=m_rel_bias,
                g_mix=m_g_mix, g_ffn=m_g_ffn, g_ple=m_g_ple, g_final=m_g_final, w_ffn_in=m_w_ffn_in,
                w_ffn_out=m_w_ffn_out, w_ple_in=m_w_ple_in, w_ple_gate=m_w_ple_gate)
    v_in = dict(w_in=v_w_in, w_sb_out=v_w_sb_out, w_ca_out=v_w_ca_out, w_mix_out=v_w_mix_out, rel_bias=v_rel_bias,
                g_mix=v_g_mix, g_ffn=v_g_ffn, g_ple=v_g_ple, g_final=v_g_final, w_ffn_in=v_w_ffn_in,
                w_ffn_out=v_w_ffn_out, w_ple_in=v_w_ple_in, w_ple_gate=v_w_ple_gate)

    pos = jnp.stack([2 * lax.axis_index("x") + lax.axis_index("y"), lax.axis_index("c")]).astype(jnp.int32)
    placed = {n: _cast_place(weights[n][0], axis, pos, name="cast_" + n) for n, axis in BIG}
    comm = _Comm(pos, placed)
    _run(comm.gather(("w_in",), False), name="gather_w_in_chips")
    _run(comm.gather(("w_in",), True), name="gather_w_in_pair")
    small = dict(rel_bias=rel_bias[0], g_mix=g_mix, g_ffn=g_ffn, g_ple=g_ple, g_final=g_final.reshape(1, -1))
    loss, grad_x, gs = _step(x[0], p[0, 0], loss_target[0], small, comm)

    grads, delta, new_m, new_v = {}, {}, {}, {}
    for n, _ in BIG:
        g = comm.result(n)
        d, nm, nv = _adamw(weights[n][0], g, m_in[n][0], v_in[n][0], name="adamw_" + n)
        grads[n], delta[n], new_m[n], new_v[n] = g[None], d[None], nm[None], nv[None]

    like = [weights[n] for n in SMALL]
    reduced = _small_all_reduce(_pack([gs[n] for n in SMALL] + [loss[:, :1]]), name="small_all_reduce")
    g_small = _unpack(reduced, like + [loss[:, :1]])
    total_loss = g_small[-1].reshape(())
    g_packed = _pack(g_small[:-1])
    d_s, m_s, v_s = _adamw(_pack(like), g_packed, _pack([m_in[n] for n in SMALL]), _pack([v_in[n] for n in SMALL]),
                           name="adamw_small")
    for n, g, d, nm, nv in zip(SMALL, g_small[:-1], _unpack(d_s, like), _unpack(m_s, like), _unpack(v_s, like)):
        grads[n], delta[n], new_m[n], new_v[n] = g, d, nm, nv

    return (total_loss, grad_x[None], *[grads[n] for n in ORDER], *[delta[n] for n in ORDER],
            *[new_m[n] for n in ORDER], *[new_v[n] for n in ORDER])
```

```python
import functools
import math

import jax
import jax.numpy as jnp
import numpy as np
from jax import lax
from jax.experimental import pallas as pl
from jax.experimental.pallas import tpu as pltpu

F32 = jnp.float32
BF16 = jnp.bfloat16

HEAD_DIM = 128
CHUNK = 64
LEFT_CHUNKS = 8
REL_CLIP = 128
N_REL = REL_CLIP + CHUNK
BAND = (LEFT_CHUNKS + 2) * CHUNK
PAD = (LEFT_CHUNKS + 1) * CHUNK
CA_PER_STEP = 4
CA_ROWS = CA_PER_STEP * CHUNK
SB_BLOCK = 128
SB_KEYS = 512
SB_GROUPS = SB_KEYS // SB_BLOCK
EPS = 1e-6
NEG = -1e30

ADAM_LR = 0.001
ADAM_B1 = 0.9
ADAM_B2 = 0.999
ADAM_EPS = 1e-08
ADAM_WD = 0.01
ADAM_STEP = 10

VMEM_LIMIT = 48 * 1024 * 1024
MM_VMEM_BUDGET = 36 * 1024 * 1024
V7X_HBM_BYTES_PER_S = 3.7e12
GRID_STEP_S = 0.35e-6
MESH = pl.DeviceIdType.MESH
N_CHIPS = 4


def _pick(dim, prefs):
    for t in prefs:
        if dim % t == 0:
            return t
    raise ValueError(f"no tile for {dim}")


def _cparams(sem=None):
    return pltpu.CompilerParams(dimension_semantics=sem, vmem_limit_bytes=VMEM_LIMIT)


def _sigmoid(v):
    return 1.0 / (1.0 + jnp.exp(-v))


def _dot(a, b, dims):
    return lax.dot_general(a, b, (dims, ((), ())), preferred_element_type=F32)


def _dot_nn(a, b):
    return _dot(a, b, ((1,), (0,)))


def _dot_nt(a, b):
    return _dot(a, b, ((1,), (1,)))


def _dot_tn(a, b):
    return _dot(a, b, ((0,), (0,)))


HBM = pl.BlockSpec(memory_space=pltpu.HBM)


class _Ride:
    def __init__(self):
        self.items = []

    def add(self, ins, outs, aliases, n_sems, start, finish, sink):
        self.items.append((ins, outs, aliases, n_sems, start, finish, sink))


def _call(body, args, *, name, grid, in_specs, out_specs, out_shape, scratch_shapes=(), sem=None, ride=None):
    items = ride.items if ride is not None else []
    n_in, n_out, n_scr = len(args), len(out_shape), len(scratch_shapes)
    r_ins = [a for it in items for a in it[0]]
    r_outs = [o for it in items for o in it[1]]
    aliases, a, b = {}, n_in, n_out
    for it in items:
        aliases.update({a + i: b + o for i, o in it[2].items()})
        a, b = a + len(it[0]), b + len(it[1])
    sems = [pltpu.SemaphoreType.DMA((it[3],)) for it in items for _ in range(2)]

    def wrapped(*refs):
        ins, rin = refs[:n_in], refs[n_in:n_in + len(r_ins)]
        at = n_in + len(r_ins)
        outs, rout = refs[at:at + n_out], refs[at + n_out:at + n_out + len(r_outs)]
        at += n_out + len(r_outs)
        scr, rsem = refs[at:at + n_scr], refs[at + n_scr:]

        def each(which):
            a = b = 0
            for q, it in enumerate(items):
                it[which](rin[a:a + len(it[0])], rout[b:b + len(it[1])], rsem[2 * q], rsem[2 * q + 1])
                a, b = a + len(it[0]), b + len(it[1])

        if items:
            ids = [pl.program_id(d) for d in range(len(grid))]
            first = functools.reduce(jnp.logical_and, [i == 0 for i in ids])
            last = functools.reduce(jnp.logical_and, [i == g - 1 for i, g in zip(ids, grid)])
            pl.when(first)(lambda: each(4))
        body(*ins, *outs, *scr)
        if items:
            pl.when(last)(lambda: each(5))

    res = pl.pallas_call(
        wrapped, name=name, grid=grid,
        in_specs=list(in_specs) + [HBM] * len(r_ins),
        out_specs=list(out_specs) + [HBM] * len(r_outs),
        out_shape=list(out_shape) + r_outs,
        scratch_shapes=list(scratch_shapes) + sems,
        input_output_aliases=aliases,
        compiler_params=_cparams(("arbitrary",) * len(grid) if items else sem),
    )(*args, *r_ins)
    b = n_out
    for it in items:
        it[6](res[b:b + len(it[1])])
        b += len(it[1])
    return list(res[:n_out])


def _mm_tiles(m, n_align, n, k, a_bytes, b_bytes, out_bytes):
    best = None
    tks = sorted({t for t in (k, k // 2, k // 4, 2048, 1024, 512, 256, 128) if t <= k and k % t == 0 and t % 128 == 0})
    for tm in (t for t in (2048, 1024, 512, 256, 128) if m % t == 0):
        for tn in (t for t in (2048, 1024, 512, 256, 128) if n_align % t == 0):
            for tk in tks:
                nk = k // tk
                vmem = 2 * (tm * tk * a_bytes + tk * tn * b_bytes + tm * tn * out_bytes) + tm * tn * 4
                if vmem > MM_VMEM_BUDGET:
                    continue
                traffic = m * k * a_bytes * (n // tn if nk > 1 else 1) + k * n * b_bytes * (m // tm)
                traffic += tm * tk * a_bytes + tk * tn * b_bytes + tm * tn * out_bytes
                traffic += m * n * 4 * nk if nk > 1 else 0
                cost = traffic / V7X_HBM_BYTES_PER_S + (m // tm) * (n // tn) * nk * GRID_STEP_S
                if best is None or cost < best[0]:
                    best = (cost, tm, tn, tk)
    return best[1:]


def _mm(a, b, mode, out_dtypes, *, name, n=None, b_col_off=0, resid=None, ride=None):
    if mode == "nn":
        m, k = a.shape
        n = b.shape[1] if n is None else n
    elif mode == "nt":
        m, k = a.shape
        n = b.shape[0]
    else:
        k, m = a.shape
        n = b.shape[1]
    n_out = len(out_dtypes)
    has_resid = resid is not None
    out_bytes = sum(jnp.dtype(dt).itemsize for dt in out_dtypes) + (4 if has_resid else 0)
    tm, tn, tk = _mm_tiles(m, math.gcd(n, b_col_off) if b_col_off else n, n, k,
                           a.dtype.itemsize, b.dtype.itemsize, out_bytes)
    nk = k // tk
    boff = b_col_off // tn
    dot = {"nn": _dot_nn, "nt": _dot_nt, "tn": _dot_tn}[mode]

    def body(*refs):
        a_ref, b_ref = refs[0], refs[1]
        r_ref = refs[2] if has_resid else None
        o_refs = refs[2 + has_resid: 2 + has_resid + n_out]

        def finish(r):
            if has_resid:
                r = r + r_ref[...]
            for o_ref in o_refs:
                o_ref[...] = r.astype(o_ref.dtype)

        part = dot(a_ref[...].astype(BF16), b_ref[...].astype(BF16))
        if nk == 1:
            finish(part)
            return
        acc_ref = refs[-1]
        kk = pl.program_id(2)

        @pl.when(kk == 0)
        def _():
            acc_ref[...] = part

        @pl.when(kk > 0)
        def _():
            acc_ref[...] += part

        @pl.when(kk == nk - 1)
        def _():
            finish(acc_ref[...])

    if mode == "nn":
        a_spec = pl.BlockSpec((tm, tk), lambda i, j, kk: (i, kk))
        b_spec = pl.BlockSpec((tk, tn), lambda i, j, kk: (kk, j + boff))
    elif mode == "nt":
        a_spec = pl.BlockSpec((tm, tk), lambda i, j, kk: (i, kk))
        b_spec = pl.BlockSpec((tn, tk), lambda i, j, kk: (j, kk))
    else:
        a_spec = pl.BlockSpec((tk, tm), lambda i, j, kk: (kk, i))
        b_spec = pl.BlockSpec((tk, tn), lambda i, j, kk: (kk, j))
    o_spec = pl.BlockSpec((tm, tn), lambda i, j, kk: (i, j))
    in_specs = [a_spec, b_spec] + ([o_spec] if has_resid else [])
    args = [a, b] + ([resid] if has_resid else [])
    outs = _call(
        body, args, name=name,
        grid=(m // tm, n // tn, nk),
        in_specs=in_specs,
        out_specs=[o_spec] * n_out,
        out_shape=[jax.ShapeDtypeStruct((m, n), dt) for dt in out_dtypes],
        scratch_shapes=[pltpu.VMEM((tm, tn), F32)] if nk > 1 else [],
        sem=("parallel", "parallel", "arbitrary"), ride=ride)
    return outs[0] if n_out == 1 else tuple(outs)


def _row_tile(s):
    return _pick(s, (256, 128))


def _rms_fwd(x, g, *, name):
    s, d = x.shape
    tr = _row_tile(s)

    def body(x_ref, g_ref, o_ref):
        xv = x_ref[...]
        r = lax.rsqrt(jnp.mean(xv * xv, axis=1, keepdims=True) + EPS)
        o_ref[...] = (xv * r * g_ref[...]).astype(o_ref.dtype)

    return pl.pallas_call(
        body, name=name, grid=(s // tr,),
        in_specs=[pl.BlockSpec((tr, d), lambda i: (i, 0)), pl.BlockSpec((1, d), lambda i: (0, 0))],
        out_specs=pl.BlockSpec((tr, d), lambda i: (i, 0)),
        out_shape=jax.ShapeDtypeStruct((s, d), BF16),
        compiler_params=_cparams(("parallel",)),
    )(x, g)


def _rms_bwd(x, g, dh, dres, *, name):
    s, d = x.shape
    tr = _row_tile(s)

    def body(x_ref, g_ref, dh_ref, dres_ref, dx_ref, dg_ref):
        i = pl.program_id(0)
        xv = x_ref[...]
        r = lax.rsqrt(jnp.mean(xv * xv, axis=1, keepdims=True) + EPS)
        xhat = xv * r
        dhv = dh_ref[...]
        dxhat = dhv * g_ref[...]
        proj = jnp.mean(dxhat * xhat, axis=1, keepdims=True)
        dx_ref[...] = dres_ref[...] + r * (dxhat - xhat * proj)

        @pl.when(i == 0)
        def _():
            dg_ref[...] = jnp.zeros_like(dg_ref)

        dg_ref[...] += jnp.sum(dhv * xhat, axis=0, keepdims=True)

    row = pl.BlockSpec((tr, d), lambda i: (i, 0))
    vec = pl.BlockSpec((1, d), lambda i: (0, 0))
    return pl.pallas_call(
        body, name=name, grid=(s // tr,),
        in_specs=[row, vec, row, row],
        out_specs=[row, vec],
        out_shape=[jax.ShapeDtypeStruct((s, d), F32), jax.ShapeDtypeStruct((1, d), F32)],
        compiler_params=_cparams(("arbitrary",)),
    )(x, g, dh, dres)


def _final_loss(x, g, target, *, name):
    s, d = x.shape
    tr = _row_tile(s)

    def body(x_ref, g_ref, t_ref, dx_ref, dg_ref, loss_ref):
        i = pl.program_id(0)
        xv = x_ref[...]
        gv = g_ref[...]
        r = lax.rsqrt(jnp.mean(xv * xv, axis=1, keepdims=True) + EPS)
        xhat = xv * r
        err = xhat * gv - t_ref[...]
        dy = err * (1.0 / d)
        dxhat = dy * gv
        proj = jnp.mean(dxhat * xhat, axis=1, keepdims=True)
        dx_ref[...] = r * (dxhat - xhat * proj)

        @pl.when(i == 0)
        def _():
            dg_ref[...] = jnp.zeros_like(dg_ref)
            loss_ref[...] = jnp.zeros_like(loss_ref)

        dg_ref[...] += jnp.sum(dy * xhat, axis=0, keepdims=True)
        part = 0.5 * jnp.sum(jnp.mean(err * err, axis=1, keepdims=True), axis=0, keepdims=True)
        loss_ref[...] += jnp.broadcast_to(part, loss_ref.shape)

    row = pl.BlockSpec((tr, d), lambda i: (i, 0))
    vec = pl.BlockSpec((1, d), lambda i: (0, 0))
    return pl.pallas_call(
        body, name=name, grid=(s // tr,),
        in_specs=[row, vec, row],
        out_specs=[row, vec, pl.BlockSpec((1, 128), lambda i: (0, 0))],
        out_shape=[jax.ShapeDtypeStruct((s, d), F32), jax.ShapeDtypeStruct((1, d), F32),
                   jax.ShapeDtypeStruct((1, 128), F32)],
        compiler_params=_cparams(("arbitrary",)),
    )(x, g, target)


def _ew(body, ins, in_blocks, outs, out_blocks, grid, *, name, ride=None):
    return _call(body, ins, name=name, grid=grid,
                 in_specs=[pl.BlockSpec(bs, im) for bs, im in in_blocks],
                 out_specs=[pl.BlockSpec(bs, im) for bs, im in out_blocks],
                 out_shape=outs, sem=("parallel",) * len(grid), ride=ride)


def _gate_merge_fwd(gates, o_sb, o_ca, *, name):
    s, d = o_sb.shape
    tr, tc = _row_tile(s), _pick(d, (1024, 512, 256, 128))
    nc = d // tc

    def body(gs_ref, gc_ref, os_ref, oc_ref, m_ref):
        m = _sigmoid(gs_ref[...]) * os_ref[...] + _sigmoid(gc_ref[...]) * oc_ref[...]
        m_ref[...] = m.astype(m_ref.dtype)

    blk = ((tr, tc), lambda i, j: (i, j))
    return _ew(body, [gates, gates, o_sb, o_ca],
               [blk, ((tr, tc), lambda i, j: (i, j + nc)), blk, blk],
               [jax.ShapeDtypeStruct((s, d), BF16)], [blk], (s // tr, nc), name=name)[0]


def _gate_merge_bwd(dmerged, gates, o_sb, o_ca, *, name):
    s, d = o_sb.shape
    tr, tc = _row_tile(s), _pick(d, (1024, 512, 256, 128))
    nc = d // tc

    def body(dm_ref, gs_ref, gc_ref, os_ref, oc_ref, dgs_ref, dgc_ref, dos_ref, doc_ref):
        dm = dm_ref[...]
        ss = _sigmoid(gs_ref[...])
        sc = _sigmoid(gc_ref[...])
        dgs_ref[...] = (dm * os_ref[...] * ss * (1.0 - ss)).astype(dgs_ref.dtype)
        dgc_ref[...] = (dm * oc_ref[...] * sc * (1.0 - sc)).astype(dgc_ref.dtype)
        dos_ref[...] = (dm * ss).astype(dos_ref.dtype)
        doc_ref[...] = (dm * sc).astype(doc_ref.dtype)

    blk = ((tr, tc), lambda i, j: (i, j))
    sd = jax.ShapeDtypeStruct((s, d), BF16)
    return _ew(body, [dmerged, gates, gates, o_sb, o_ca],
               [blk, blk, ((tr, tc), lambda i, j: (i, j + nc)), blk, blk],
               [sd, sd, sd, sd], [blk, blk, blk, blk], (s // tr, nc), name=name)


def _swiglu_fwd(gu, *, name):
    s, f2 = gu.shape
    f = f2 // 2
    tr, tc = _row_tile(s), _pick(f, (512, 256, 128))
    nc = f // tc

    def body(g_ref, u_ref, a_ref):
        gv = g_ref[...]
        a_ref[...] = (gv * _sigmoid(gv) * u_ref[...]).astype(a_ref.dtype)

    blk = ((tr, tc), lambda i, j: (i, j))
    return _ew(body, [gu, gu], [blk, ((tr, tc), lambda i, j: (i, j + nc))],
               [jax.ShapeDtypeStruct((s, f), BF16)], [blk], (s // tr, nc), name=name)[0]


def _swiglu_bwd(dact, gu, *, name):
    s, f2 = gu.shape
    f = f2 // 2
    tr, tc = 128, _pick(f, (512, 256, 128))

    def body(da_ref, gu_ref, o_ref):
        for at in range(0, f, tc):
            da = da_ref[:, at:at + tc]
            gv = gu_ref[:, at:at + tc]
            sg = _sigmoid(gv)
            o_ref[:, at:at + tc] = (da * gu_ref[:, f + at:f + at + tc] * sg * (1.0 + gv * (1.0 - sg))).astype(o_ref.dtype)
            o_ref[:, f + at:f + at + tc] = (da * gv * sg).astype(o_ref.dtype)

    row = lambda i: (i, 0)
    return _ew(body, [dact, gu], [((tr, f), row), ((tr, f2), row)], [jax.ShapeDtypeStruct((s, f2), BF16)],
               [((tr, f2), row)], (s // tr,), name=name)[0]


def _ple_fwd(x, t, pe, *, name):
    s, d = x.shape
    tr, tc = _row_tile(s), _pick(d, (1024, 512, 256, 128))

    def body(x_ref, t_ref, p_ref, o_ref):
        o_ref[...] = x_ref[...] + _sigmoid(t_ref[...]) * p_ref[...]

    blk = ((tr, tc), lambda i, j: (i, j))
    return _ew(body, [x, t, pe], [blk, blk, blk],
               [jax.ShapeDtypeStruct((s, d), F32)], [blk], (s // tr, d // tc), name=name)[0]


def _ple_bwd(dx, t, pe, *, name):
    s, d = dx.shape
    tr, tc = _row_tile(s), _pick(d, (1024, 512, 256, 128))

    def body(dx_ref, t_ref, p_ref, dt_ref, dp_ref):
        dxv = dx_ref[...]
        sg = _sigmoid(t_ref[...])
        dt_ref[...] = (dxv * p_ref[...] * sg * (1.0 - sg)).astype(dt_ref.dtype)
        dp_ref[...] = (dxv * sg).astype(dp_ref.dtype)

    blk = ((tr, tc), lambda i, j: (i, j))
    sd = jax.ShapeDtypeStruct((s, d), BF16)
    return _ew(body, [dx, t, pe], [blk, blk, blk], [sd, sd], [blk, blk], (s // tr, d // tc), name=name)


def _sb_tri(later):
    row = lax.broadcasted_iota(jnp.int32, (SB_BLOCK, SB_BLOCK), 0)
    col = lax.broadcasted_iota(jnp.int32, (SB_BLOCK, SB_BLOCK), 1)
    tri = (row > col) if later else (row < col)
    return jnp.concatenate([tri.astype(BF16), jnp.ones((SB_BLOCK, SB_BLOCK), BF16)], axis=1)


def _sb_valid(i, j):
    qi = i * SB_BLOCK + lax.broadcasted_iota(jnp.int32, (SB_BLOCK, SB_KEYS), 0)
    ki = j * SB_KEYS + lax.broadcasted_iota(jnp.int32, (SB_BLOCK, SB_KEYS), 1)
    return ki < qi


def _sb_scan(v, tri, run, later):
    hi = v.astype(BF16)
    lo = (v - hi.astype(F32)).astype(BF16)
    outs = [None] * SB_GROUPS
    for b in (reversed(range(SB_GROUPS)) if later else range(SB_GROUPS)):
        cols = slice(b * SB_BLOCK, (b + 1) * SB_BLOCK)
        r = _dot_nn(hi[:, cols], tri) + _dot_nn(lo[:, cols], tri)
        outs[b] = r[:, :SB_BLOCK] + run
        run = run + r[:, SB_BLOCK:]
    return jnp.concatenate(outs, axis=1), run


def _sb_scores(q, kj, scale, valid):
    z = _dot_nt(q, kj) * scale
    t = jnp.log(1.0 + jnp.exp(-jnp.abs(z)))
    return jnp.minimum(z, 0.0) - t, jnp.where(valid, -jnp.maximum(z, 0.0) - t, 0.0)


def _sb_specs(h_count, s, col0):
    q_spec = pl.BlockSpec((SB_BLOCK, HEAD_DIM), lambda h, i: (i, col0 + h))
    k_spec = pl.BlockSpec((s, HEAD_DIM), lambda h, i: (0, col0 + h_count + h))
    v_spec = pl.BlockSpec((s, HEAD_DIM), lambda h, i: (0, col0 + 2 * h_count + h))
    return q_spec, k_spec, v_spec


def _sb_fwd(qkv, n_heads, col0, *, name, ride=None):
    s = qkv.shape[0]
    nq = s // SB_BLOCK
    scale = HEAD_DIM ** -0.5

    def body(q_ref, k_ref, v_ref, o_ref):
        i = pl.program_id(1)
        steps = i // SB_GROUPS + 1
        q = q_ref[...]
        tri = _sb_tri(later=True)

        def step(jj, carry):
            run, acc = carry
            j = steps - 1 - jj
            off = pl.multiple_of(j * SB_KEYS, SB_KEYS)
            valid = _sb_valid(i, j)
            ls, lk = _sb_scores(q, k_ref[pl.ds(off, SB_KEYS), :], scale, valid)
            between, run = _sb_scan(lk, tri, run, later=True)
            a = jnp.where(valid, jnp.exp(ls + between), 0.0)
            return run, acc + _dot_nn(a.astype(BF16), v_ref[pl.ds(off, SB_KEYS), :])

        init = (jnp.zeros((SB_BLOCK, SB_BLOCK), F32), jnp.zeros((SB_BLOCK, HEAD_DIM), F32))
        _, acc = lax.fori_loop(0, steps, step, init)
        o_ref[...] = acc.astype(o_ref.dtype)

    q_spec, k_spec, v_spec = _sb_specs(n_heads, s, col0)
    return _call(
        body, [qkv, qkv, qkv], name=name, grid=(n_heads, nq),
        in_specs=[q_spec, k_spec, v_spec],
        out_specs=[pl.BlockSpec((SB_BLOCK, HEAD_DIM), lambda h, i: (i, h))],
        out_shape=[jax.ShapeDtypeStruct((s, n_heads * HEAD_DIM), BF16)],
        sem=("parallel", "arbitrary"), ride=ride)[0]


def _sb_bwd(qkv, dy, n_heads, col0, *, name, ride=None):
    s = qkv.shape[0]
    nq = s // SB_BLOCK
    scale = HEAD_DIM ** -0.5

    def body(q_ref, k_ref, v_ref, dy_ref, dq_ref, dk_ref, dv_ref, e_scr, dk_acc, dv_acc):
        i = pl.program_id(1)
        steps = i // SB_GROUPS + 1
        q = q_ref[...]
        dyv = dy_ref[...]

        @pl.when(i == 0)
        def _():
            dk_acc[...] = jnp.zeros_like(dk_acc)
            dv_acc[...] = jnp.zeros_like(dv_acc)

        tri_later = _sb_tri(later=True)

        def pass1(jj, run):
            j = steps - 1 - jj
            off = pl.multiple_of(j * SB_KEYS, SB_KEYS)
            valid = _sb_valid(i, j)
            ls, lk = _sb_scores(q, k_ref[pl.ds(off, SB_KEYS), :], scale, valid)
            between, run = _sb_scan(lk, tri_later, run, later=True)
            a = jnp.where(valid, jnp.exp(ls + between), 0.0)
            e_scr[j] = a * _dot_nt(dyv, v_ref[pl.ds(off, SB_KEYS), :])
            dv_acc[pl.ds(off, SB_KEYS), :] += _dot_tn(a.astype(BF16), dyv)
            return run

        lax.fori_loop(0, steps, pass1, jnp.zeros((SB_BLOCK, SB_BLOCK), F32))

        tri_earlier = _sb_tri(later=False)

        def pass2(j, carry):
            run, dq = carry
            off = pl.multiple_of(j * SB_KEYS, SB_KEYS)
            kj = k_ref[pl.ds(off, SB_KEYS), :]
            sg = _sigmoid(_dot_nt(q, kj) * scale)
            e = e_scr[j]
            before, run = _sb_scan(e, tri_earlier, run, later=False)
            dz = jnp.where(_sb_valid(i, j), e * (1.0 - sg) - sg * before, 0.0) * scale
            dzb = dz.astype(BF16)
            dk_acc[pl.ds(off, SB_KEYS), :] += _dot_tn(dzb, q)
            return run, dq + _dot_nn(dzb, kj)

        init = (jnp.zeros((SB_BLOCK, SB_BLOCK), F32), jnp.zeros((SB_BLOCK, HEAD_DIM), F32))
        _, dq = lax.fori_loop(0, steps, pass2, init)
        dq_ref[...] = dq.astype(dq_ref.dtype)

        @pl.when(i == nq - 1)
        def _():
            dk_ref[...] = dk_acc[...].astype(dk_ref.dtype)
            dv_ref[...] = dv_acc[...].astype(dv_ref.dtype)

    q_spec, k_spec, v_spec = _sb_specs(n_heads, s, col0)
    blk = pl.BlockSpec((SB_BLOCK, HEAD_DIM), lambda h, i: (i, h))
    full = pl.BlockSpec((s, HEAD_DIM), lambda h, i: (0, h))
    sd = jax.ShapeDtypeStruct((s, n_heads * HEAD_DIM), BF16)
    return _call(
        body, [qkv, qkv, qkv, dy], name=name, grid=(n_heads, nq),
        in_specs=[q_spec, k_spec, v_spec, blk],
        out_specs=[blk, full, full],
        out_shape=[sd, sd, sd],
        scratch_shapes=[pltpu.VMEM((s // SB_KEYS, SB_BLOCK, SB_KEYS), F32),
                        pltpu.VMEM((s, HEAD_DIM), F32), pltpu.VMEM((s, HEAD_DIM), F32)],
        sem=("parallel", "arbitrary"), ride=ride)


def _band_bias(rel_bias):
    h = rel_bias.shape[0]
    width = BAND + CHUNK
    first = width - 1 - N_REL
    line = jnp.concatenate([jnp.broadcast_to(rel_bias[:, :1], (h, first)), rel_bias], axis=1)
    tiled = jnp.broadcast_to(line[:, None, :], (h, CHUNK, width - 1)).reshape(h, CHUNK * (width - 1))
    skew = jnp.pad(tiled, ((0, 0), (0, CHUNK))).reshape(h, CHUNK, width)[:, ::-1, :BAND]
    seen = jnp.arange(BAND) >= CHUNK
    return jnp.where(seen[None, None, :], skew, NEG)


def _band_bias_grad(dbias):
    h = dbias.shape[0]
    width = BAND + CHUNK
    flipped = jnp.pad(dbias[:, ::-1, :], ((0, 0), (0, 0), (0, CHUNK)))
    skew = flipped.reshape(h, CHUNK * width)[:, :CHUNK * (width - 1)].reshape(h, CHUNK, width - 1)
    diag = jnp.sum(skew, axis=1)
    first = width - 1 - N_REL
    clipped = jnp.sum(diag[:, :first + 1], axis=1, keepdims=True)
    return jnp.concatenate([clipped, diag[:, first + 1:]], axis=1)


def _ca_load_padded(k_ref, v_ref, kp, vp, s):
    kp[pl.ds(0, PAD), :] = jnp.zeros((PAD, HEAD_DIM), kp.dtype)
    vp[pl.ds(0, PAD), :] = jnp.zeros((PAD, HEAD_DIM), vp.dtype)
    kp[pl.ds(PAD, s), :] = k_ref[...]
    vp[pl.ds(PAD, s), :] = v_ref[...]


def _ca_weights(q, kb, bias, off, scale):
    z = _dot_nt(q, kb) * scale + bias
    pos = off + lax.broadcasted_iota(jnp.int32, (CHUNK, BAND), 1)
    z = jnp.where(pos >= PAD, z, NEG)
    p = jnp.exp(z - jnp.max(z, axis=1, keepdims=True))
    return p / jnp.sum(p, axis=1, keepdims=True)


def _ca_specs(h_count, s, col0):
    q_spec = pl.BlockSpec((CA_ROWS, HEAD_DIM), lambda h, c: (c, col0 + h))
    k_spec = pl.BlockSpec((s, HEAD_DIM), lambda h, c: (0, col0 + h_count + h))
    v_spec = pl.BlockSpec((s, HEAD_DIM), lambda h, c: (0, col0 + 2 * h_count + h))
    b_spec = pl.BlockSpec((1, CHUNK, BAND), lambda h, c: (h, 0, 0))
    return q_spec, k_spec, v_spec, b_spec


def _ca_fwd(qkv, bias, n_heads, col0, *, name, ride=None):
    s = qkv.shape[0]
    nc = s // CA_ROWS
    scale = HEAD_DIM ** -0.5

    def body(q_ref, k_ref, v_ref, b_ref, o_ref, kp, vp):
        c = pl.program_id(1)

        @pl.when(c == 0)
        def _():
            _ca_load_padded(k_ref, v_ref, kp, vp, s)

        for u in range(CA_PER_STEP):
            rows = pl.ds(u * CHUNK, CHUNK)
            off = pl.multiple_of((c * CA_PER_STEP + u) * CHUNK, CHUNK)
            w = _ca_weights(q_ref[rows, :], kp[pl.ds(off, BAND), :], b_ref[0], off, scale)
            o_ref[rows, :] = _dot_nn(w.astype(BF16), vp[pl.ds(off, BAND), :]).astype(o_ref.dtype)

    q_spec, k_spec, v_spec, b_spec = _ca_specs(n_heads, s, col0)
    return _call(
        body, [qkv, qkv, qkv, bias], name=name, grid=(n_heads, nc),
        in_specs=[q_spec, k_spec, v_spec, b_spec],
        out_specs=[pl.BlockSpec((CA_ROWS, HEAD_DIM), lambda h, c: (c, h))],
        out_shape=[jax.ShapeDtypeStruct((s, n_heads * HEAD_DIM), BF16)],
        scratch_shapes=[pltpu.VMEM((s + PAD, HEAD_DIM), BF16), pltpu.VMEM((s + PAD, HEAD_DIM), BF16)],
        sem=("parallel", "arbitrary"), ride=ride)[0]


def _ca_bwd(qkv, bias, dy, n_heads, col0, *, name, ride=None):
    s = qkv.shape[0]
    nc = s // CA_ROWS
    scale = HEAD_DIM ** -0.5

    def body(q_ref, k_ref, v_ref, b_ref, dy_ref, dq_ref, dk_ref, dv_ref, db_ref, kp, vp, dkp, dvp):
        c = pl.program_id(1)

        @pl.when(c == 0)
        def _():
            _ca_load_padded(k_ref, v_ref, kp, vp, s)
            dkp[...] = jnp.zeros_like(dkp)
            dvp[...] = jnp.zeros_like(dvp)
            db_ref[...] = jnp.zeros_like(db_ref)

        dbias = jnp.zeros((CHUNK, BAND), F32)
        for u in range(CA_PER_STEP):
            rows = pl.ds(u * CHUNK, CHUNK)
            off = pl.multiple_of((c * CA_PER_STEP + u) * CHUNK, CHUNK)
            q = q_ref[rows, :]
            dyv = dy_ref[rows, :]
            kb = kp[pl.ds(off, BAND), :]
            w = _ca_weights(q, kb, b_ref[0], off, scale)
            dw = _dot_nt(dyv, vp[pl.ds(off, BAND), :])
            dvp[pl.ds(off, BAND), :] += _dot_tn(w.astype(BF16), dyv)
            dz = w * (dw - jnp.sum(w * dw, axis=1, keepdims=True))
            dbias = dbias + dz
            dzs = (dz * scale).astype(BF16)
            dq_ref[rows, :] = _dot_nn(dzs, kb).astype(dq_ref.dtype)
            dkp[pl.ds(off, BAND), :] += _dot_tn(dzs, q)
        db_ref[0] += dbias

        @pl.when(c == nc - 1)
        def _():
            dk_ref[...] = dkp[pl.ds(PAD, s), :].astype(dk_ref.dtype)
            dv_ref[...] = dvp[pl.ds(PAD, s), :].astype(dv_ref.dtype)

    q_spec, k_spec, v_spec, b_spec = _ca_specs(n_heads, s, col0)
    blk = pl.BlockSpec((CA_ROWS, HEAD_DIM), lambda h, c: (c, h))
    full = pl.BlockSpec((s, HEAD_DIM), lambda h, c: (0, h))
    sd = jax.ShapeDtypeStruct((s, n_heads * HEAD_DIM), BF16)
    return _call(
        body, [qkv, qkv, qkv, bias, dy], name=name, grid=(n_heads, nc),
        in_specs=[q_spec, k_spec, v_spec, b_spec, blk],
        out_specs=[blk, full, full, b_spec],
        out_shape=[sd, sd, sd, jax.ShapeDtypeStruct((n_heads, CHUNK, BAND), F32)],
        scratch_shapes=[pltpu.VMEM((s + PAD, HEAD_DIM), BF16), pltpu.VMEM((s + PAD, HEAD_DIM), BF16),
                        pltpu.VMEM((s + PAD, HEAD_DIM), F32), pltpu.VMEM((s + PAD, HEAD_DIM), F32)],
        sem=("parallel", "arbitrary"), ride=ride)


EARLY = ("w_sb_out", "w_ca_out", "w_mix_out")


def _step(x, p, target, small, comm):
    w = comm.w
    d = x.shape[1]
    n_sb = w["w_sb_out"].shape[0] // HEAD_DIM
    n_ca = w["w_ca_out"].shape[0] // HEAD_DIM
    qkv_cols = 3 * HEAD_DIM * (n_sb + n_ca)
    ca_col0 = 3 * n_sb
    both = (F32, BF16)

    h1 = _rms_fwd(x, small["g_mix"], name="rms_mix")
    qkv = _mm(h1, w["w_in"], "nn", (BF16,), name="proj_qkv", n=qkv_cols, ride=comm.gather(EARLY, False))
    gates = _mm(h1, w["w_in"], "nn", (F32,), name="proj_gates", n=2 * d, b_col_off=qkv_cols,
                ride=comm.gather(("w_ple_gate",), False, comm.gather(EARLY, True)))
    bias = _band_bias(small["rel_bias"])
    y_sb = _sb_fwd(qkv, n_sb, 0, name="sb_fwd",
                   ride=comm.gather(("w_ffn_in",), False, comm.gather(("w_ple_gate",), True)))
    y_ca = _ca_fwd(qkv, bias, n_ca, ca_col0, name="ca_fwd",
                   ride=comm.gather(("w_ffn_out",), False, comm.gather(("w_ffn_in",), True)))
    o_sb = _mm(y_sb, w["w_sb_out"], "nn", (F32,), name="sb_out", ride=comm.gather(("w_ple_in",), False))
    o_ca = _mm(y_ca, w["w_ca_out"], "nn", (F32,), name="ca_out", ride=comm.gather(("w_ffn_out",), True))
    merged = _gate_merge_fwd(gates, o_sb, o_ca, name="gate_merge")
    x1 = _mm(merged, w["w_mix_out"], "nn", (F32,), name="mix_out", resid=x, ride=comm.gather(("w_ple_in",), True))
    h2 = _rms_fwd(x1, small["g_ffn"], name="rms_ffn")
    gu = _mm(h2, w["w_ffn_in"], "nn", (F32,), name="ffn_in")
    act = _swiglu_fwd(gu, name="swiglu")
    x2 = _mm(act, w["w_ffn_out"], "nn", (F32,), name="ffn_out", resid=x1)
    h3 = _rms_fwd(x2, small["g_ple"], name="rms_ple")
    t = _mm(h3, w["w_ple_gate"], "nn", (F32,), name="ple_gate")
    pe = _mm(p, w["w_ple_in"], "nn", (F32,), name="ple_in")
    x3 = _ple_fwd(x2, t, pe, name="ple_add")

    gs = {}
    dx3, gs["g_final"], loss = _final_loss(x3, small["g_final"], target, name="final_loss")
    dt, dpe = _ple_bwd(dx3, t, pe, name="ple_bwd")
    comm.grad("w_ple_in", *_mm(p, dpe, "tn", both, name="dw_ple_in"))
    comm.grad("w_ple_gate", *_mm(h3, dt, "tn", both, name="dw_ple_gate"))
    ple = ("w_ple_in", "w_ple_gate")
    dh3 = _mm(dt, w["w_ple_gate"], "nt", (F32,), name="dh_ple", ride=comm.pair(ple))
    dx2, gs["g_ple"] = _rms_bwd(x2, small["g_ple"], dh3, dx3, name="rms_ple_bwd")
    comm.add(ple)
    comm.grad("w_ffn_out", *_mm(act, dx2, "tn", both, name="dw_ffn_out", ride=comm.chips(ple)))
    dact = _mm(dx2, w["w_ffn_out"], "nt", (F32,), name="dact", ride=comm.pair(("w_ffn_out",)))
    dgu = _swiglu_bwd(dact, gu, name="swiglu_bwd")
    comm.sum(ple)
    comm.add(("w_ffn_out",))
    comm.grad("w_ffn_in", *_mm(h2, dgu, "tn", both, name="dw_ffn_in",
                               ride=comm.share(ple, comm.chips(("w_ffn_out",)))))
    dh2 = _mm(dgu, w["w_ffn_in"], "nt", (F32,), name="dh_ffn", ride=comm.pair(("w_ffn_in",)))
    dx1, gs["g_ffn"] = _rms_bwd(x1, small["g_ffn"], dh2, dx2, name="rms_ffn_bwd")
    comm.add(("w_ffn_in",))
    comm.sum(("w_ffn_out",))
    comm.grad("w_mix_out", *_mm(merged, dx1, "tn", both, name="dw_mix_out", ride=comm.share(("w_ffn_out",))))
    dmerged = _mm(dx1, w["w_mix_out"], "nt", (F32,), name="dmerged", ride=comm.pair(("w_mix_out",)))
    dg_sb, dg_ca, do_sb, do_ca = _gate_merge_bwd(dmerged, gates, o_sb, o_ca, name="gate_merge_bwd")
    comm.add(("w_mix_out",))
    comm.grad("w_sb_out", *_mm(y_sb, do_sb, "tn", both, name="dw_sb_out"))
    comm.grad("w_ca_out", *_mm(y_ca, do_ca, "tn", both, name="dw_ca_out"))
    outs = ("w_sb_out", "w_ca_out")
    dy_sb = _mm(do_sb, w["w_sb_out"], "nt", (BF16,), name="dy_sb", ride=comm.pair(outs))
    dy_ca = _mm(do_ca, w["w_ca_out"], "nt", (BF16,), name="dy_ca")
    comm.add(outs)
    behind_sb = ("w_ffn_in", "w_mix_out") + outs
    dq_sb, dk_sb, dv_sb = _sb_bwd(qkv, dy_sb, n_sb, 0, name="sb_bwd", ride=comm.chips(behind_sb))
    comm.sum(behind_sb)
    dq_ca, dk_ca, dv_ca, dbias = _ca_bwd(qkv, bias, dy_ca, n_ca, ca_col0, name="ca_bwd", ride=comm.share(behind_sb))
    gs["rel_bias"] = _band_bias_grad(dbias)
    dproj = jnp.concatenate([dq_sb, dk_sb, dv_sb, dq_ca, dk_ca, dv_ca, dg_sb, dg_ca], axis=1)
    comm.grad("w_in", *_mm(h1, dproj, "tn", both, name="dw_in"))
    dh1 = _mm(dproj, w["w_in"], "nt", (F32,), name="dh_mix", ride=comm.pair(("w_in",)))
    grad_x, gs["g_mix"] = _rms_bwd(x, small["g_mix"], dh1, dx1, name="rms_mix_bwd")
    comm.add(("w_in",))
    return loss, grad_x, gs


def _position():
    x, y, c = lax.axis_index("x"), lax.axis_index("y"), lax.axis_index("c")
    chips = [(1 - x, y), (x, 1 - y), (1 - x, 1 - y)]
    return x, y, c, chips


def _aligned(v, m):
    return v if isinstance(v, int) else pl.multiple_of(v, m)


def _piece_dims(shape, axis):
    k, n = shape
    return (k // 2, n // N_CHIPS) if axis == 1 else (k // N_CHIPS // 2, n)


def _piece(ref, shape, axis, j, h):
    pr, pc = _piece_dims(shape, axis)
    if axis == 1:
        return ref.at[pl.ds(_aligned(h * pr, 16), pr), pl.ds(_aligned(j * pc, 128), pc)]
    return ref.at[pl.ds(_aligned((2 * j + h) * pr, 16), pr), :]


def _shard_half(ref, h):
    rows = ref.shape[0] // 2
    return ref.at[pl.ds(_aligned(h * rows, 16), rows), :]


def _remote(src, dst, send_sems, recv_sems, k, to):
    return pltpu.make_async_remote_copy(src_ref=src, dst_ref=dst, send_sem=send_sems.at[k],
                                        recv_sem=recv_sems.at[k], device_id=to, device_id_type=MESH)


def _prefetch_call(body, scalars, ins, in_specs, out_shape, out_specs, grid, *, name):
    spec = pltpu.PrefetchScalarGridSpec(num_scalar_prefetch=1, grid=grid, in_specs=in_specs, out_specs=out_specs)
    return pl.pallas_call(body, name=name, grid_spec=spec, out_shape=out_shape,
                          compiler_params=_cparams(("parallel",) * len(grid)))(scalars, *ins)


def _slab_tiles(pr, pc):
    tc = pc if pc <= 4096 else _pick(pc, (2048, 1024, 512, 256, 128))
    tr = next(t for t in (1024, 512, 256, 128, 64, 32, 16) if pr % t == 0 and t * tc <= 512 * 1024)
    return tr, tc


def _cast_place(w, axis, pos, *, name):
    ks, ns = w.shape
    shape = (ks, ns * N_CHIPS) if axis == 1 else (ks * N_CHIPS, ns)
    tr, tc = _slab_tiles(ks, ns)
    nr, nc = ks // tr, ns // tc

    def body(pos_ref, w_ref, o_ref):
        o_ref[...] = w_ref[...].astype(o_ref.dtype)

    if axis == 1:
        out_map = lambda i, j, pos_ref: (i, pos_ref[0] * nc + j)
    else:
        out_map = lambda i, j, pos_ref: (pos_ref[0] * nr + i, j)
    return _prefetch_call(body, pos, [w], [pl.BlockSpec((tr, tc), lambda i, j, pos_ref: (i, j))],
                          jax.ShapeDtypeStruct(shape, BF16), pl.BlockSpec((tr, tc), out_map), (nr, nc), name=name)


def _run(ride, *, name):
    if ride is None:
        return

    def body(o_ref):
        o_ref[...] = jnp.zeros_like(o_ref)

    _call(body, [], name=name, grid=(1,), in_specs=[], out_specs=[pl.BlockSpec((8, 128), lambda i: (0, 0))],
          out_shape=[jax.ShapeDtypeStruct((8, 128), F32)], ride=ride)


def _ride_gather(ride, w, n, axis, to_sibling):
    shape = w[n].shape

    def copies(ins, outs, send_sems, recv_sems, arriving):
        x, y, c, chips = _position()
        me = 2 * x + y
        out = []
        for k, (px, py) in enumerate(chips):
            to = (x, y, 1 - c) if to_sibling else (px, py, c)
            if arriving:
                lands = _piece(outs[0], shape, axis, 2 * px + py, 1 - c if to_sibling else c)
                out.append(_remote(lands, lands, send_sems, recv_sems, k, to))
            else:
                chip = 2 * px + py if to_sibling else me
                out.append(_remote(_piece(ins[0], shape, axis, chip, c), _piece(outs[0], shape, axis, chip, c),
                                   send_sems, recv_sems, k, to))
        return out

    def start(*refs):
        for cp in copies(*refs, arriving=False):
            cp.start()

    def finish(*refs):
        for cp in copies(*refs, arriving=True):
            cp.wait_recv()
        for cp in copies(*refs, arriving=False):
            cp.wait_send()

    ride.add([w[n]], [jax.ShapeDtypeStruct(shape, w[n].dtype)], {0: 0}, 3, start, finish,
             lambda outs: w.__setitem__(n, outs[0]))


def _ride_pair(ride, st, axis):
    shape = st["g16"].shape
    pr, pc = _piece_dims(shape, axis)

    def copies(ins, outs, send_sems, recv_sems):
        x, y, c, _ = _position()
        return [_remote(_piece(ins[0], shape, axis, j, 1 - c), outs[0].at[j], send_sems, recv_sems, j, (x, y, 1 - c))
                for j in range(N_CHIPS)]

    def start(*refs):
        for cp in copies(*refs):
            cp.start()

    def finish(*refs):
        for cp in copies(*refs):
            cp.wait()

    ride.add([st["g16"]], [jax.ShapeDtypeStruct((N_CHIPS, pr, pc), BF16)], {}, N_CHIPS, start, finish,
             lambda outs: st.__setitem__("sib", outs[0]))


def _ride_chips(ride, st, rows=None):
    _, pr, pc = st["s16"].shape
    r0, nr = (0, pr) if rows is None else rows

    def copies(ins, outs, send_sems, recv_sems):
        x, y, c, chips = _position()
        return [_remote(ins[0].at[2 * px + py, pl.ds(r0, nr), :], outs[0].at[k, pl.ds(r0, nr), :],
                        send_sems, recv_sems, k, (px, py, c)) for k, (px, py) in enumerate(chips)]

    def start(*refs):
        for cp in copies(*refs):
            cp.start()

    def finish(*refs):
        for cp in copies(*refs):
            cp.wait()

    ins, aliases = ([st["s16"], st["recv"]], {1: 0}) if "recv" in st else ([st["s16"]], {})
    ride.add(ins, [jax.ShapeDtypeStruct((3, pr, pc), BF16)], aliases, 3, start, finish,
             lambda outs: st.__setitem__("recv", outs[0]))


def _ride_share(ride, st):
    def sent(ins, outs, send_sems, recv_sems):
        x, y, c, _ = _position()
        return _remote(_shard_half(ins[0], c), _shard_half(outs[0], c), send_sems, recv_sems, 0, (x, y, 1 - c))

    def landed(ins, outs, send_sems, recv_sems):
        x, y, c, _ = _position()
        other = _shard_half(outs[0], 1 - c)
        return _remote(other, other, send_sems, recv_sems, 0, (x, y, 1 - c))

    def start(*refs):
        sent(*refs).start()

    def finish(*refs):
        landed(*refs).wait_recv()
        sent(*refs).wait_send()

    ride.add([st["shard"]], [jax.ShapeDtypeStruct(st["shard"].shape, F32)], {0: 0}, 1, start, finish,
             lambda outs: st.__setitem__("g", outs[0]))


def _pair_add(g32, sib, axis, pos, *, name):
    _, pr, pc = sib.shape
    tr, tc = _slab_tiles(pr, pc)
    nr, nc = pr // tr, pc // tc

    def body(pos_ref, g_ref, b_ref, o32_ref, o16_ref):
        r = g_ref[...] + b_ref[0].astype(F32)
        o32_ref[0] = r
        o16_ref[0] = r.astype(o16_ref.dtype)

    if axis == 1:
        g_map = lambda j, i, k, pos_ref: (pos_ref[1] * nr + i, j * nc + k)
    else:
        g_map = lambda j, i, k, pos_ref: ((2 * j + pos_ref[1]) * nr + i, k)
    blk = pl.BlockSpec((1, tr, tc), lambda j, i, k, pos_ref: (j, i, k))
    return _prefetch_call(body, pos, [g32, sib], [pl.BlockSpec((tr, tc), g_map), blk],
                          [jax.ShapeDtypeStruct(sib.shape, F32), jax.ShapeDtypeStruct(sib.shape, BF16)],
                          [blk, blk], (N_CHIPS, nr, nc), name=name)


def _chip_sum(s32, recv, pos, *, name):
    _, pr, pc = s32.shape
    tr, tc = _slab_tiles(pr, pc)
    nr, nc = pr // tr, pc // tc

    def body(pos_ref, m_ref, r_ref, o_ref):
        o_ref[...] = ((m_ref[0] + r_ref[0].astype(F32)) + r_ref[1].astype(F32)) + r_ref[2].astype(F32)

    return _prefetch_call(
        body, pos, [s32, recv],
        [pl.BlockSpec((1, tr, tc), lambda i, k, pos_ref: (pos_ref[0], i, k)),
         pl.BlockSpec((3, tr, tc), lambda i, k, pos_ref: (0, i, k))],
        jax.ShapeDtypeStruct((2 * pr, pc), F32),
        pl.BlockSpec((tr, tc), lambda i, k, pos_ref: (pos_ref[1] * nr + i, k)), (nr, nc), name=name)


class _Comm:
    def __init__(self, pos, w):
        self.pos, self.w, self.st = pos, w, {n: {} for n, _ in BIG}

    def gather(self, names, to_sibling, ride=None):
        ride = _Ride() if ride is None else ride
        for n in names:
            _ride_gather(ride, self.w, n, AXIS[n], to_sibling)
        return ride

    def grad(self, n, g32, g16):
        self.st[n].update(g32=g32, g16=g16)

    def pair(self, names, ride=None):
        ride = _Ride() if ride is None else ride
        for n in names:
            _ride_pair(ride, self.st[n], AXIS[n])
        return ride

    def add(self, names):
        for n in names:
            st = self.st[n]
            st["s32"], st["s16"] = _pair_add(st["g32"], st["sib"], AXIS[n], self.pos, name="rs_add_" + n)

    def chips(self, names, ride=None, rows=None):
        ride = _Ride() if ride is None else ride
        for n in names:
            _ride_chips(ride, self.st[n], rows)
        return ride

    def sum(self, names):
        for n in names:
            st = self.st[n]
            st["shard"] = _chip_sum(st["s32"], st["recv"], self.pos, name="rs_sum_" + n)

    def share(self, names, ride=None):
        ride = _Ride() if ride is None else ride
        for n in names:
            _ride_share(ride, self.st[n])
        return ride

    def result(self, n):
        return self.st[n]["g"]


class _NoComm:
    def __init__(self, w):
        self.w, self.st = w, {}

    def grad(self, n, g32, g16):
        self.st[n] = (g32, g16)

    def result(self, n):
        return self.st[n]

    def add(self, names):
        pass

    sum = add

    def gather(self, names, *args, **kwargs):
        return None

    pair = chips = share = gather


def _small_all_reduce(vec, *, name):
    r = vec.shape[0]

    def body(vec_ref, out_ref, slots, send_sems, recv_sems):
        x, y, c, _ = _position()
        me = 4 * x + 2 * y + c
        slots[me] = vec_ref[...]
        sends = []
        for k in range(1, 8):
            to = (x ^ (k >> 2), y ^ ((k >> 1) & 1), c ^ (k & 1))
            cp = _remote(slots.at[me], slots.at[me], send_sems, recv_sems, k - 1, to)
            cp.start()
            sends.append(cp)
        for k in range(1, 8):
            frm = 4 * (x ^ (k >> 2)) + 2 * (y ^ ((k >> 1) & 1)) + (c ^ (k & 1))
            _remote(slots.at[frm], slots.at[frm], send_sems, recv_sems, k - 1, (x, y, c)).wait_recv()
        for cp in sends:
            cp.wait_send()
        total = slots[0]
        for d in range(1, 8):
            total = total + slots[d]
        out_ref[...] = total

    return pl.pallas_call(
        body, name=name,
        in_specs=[pl.BlockSpec(memory_space=pltpu.VMEM)], out_specs=pl.BlockSpec(memory_space=pltpu.VMEM),
        out_shape=jax.ShapeDtypeStruct((r, 128), F32),
        scratch_shapes=[pltpu.VMEM((8, r, 128), F32), pltpu.SemaphoreType.DMA((7,)), pltpu.SemaphoreType.DMA((7,))],
    )(vec)


def _adamw(w, g, m, v, *, name, ride=None):
    r, c = w.shape
    tc = c if c <= 4096 else _pick(c, (2048, 1024, 512, 256, 128))
    tr = next(t for t in (512, 256, 128, 64, 32, 16, 8) if r % t == 0 and t * tc <= 256 * 1024)

    def body(w_ref, g_ref, m_ref, v_ref, d_ref, nm_ref, nv_ref):
        gv = g_ref[...]
        nm = ADAM_B1 * m_ref[...] + (1.0 - ADAM_B1) * gv
        nv = ADAM_B2 * v_ref[...] + (1.0 - ADAM_B2) * (gv * gv)
        m_hat = nm / (1.0 - ADAM_B1 ** ADAM_STEP)
        v_hat = nv / (1.0 - ADAM_B2 ** ADAM_STEP)
        d_ref[...] = -ADAM_LR * (m_hat / (jnp.sqrt(v_hat) + ADAM_EPS) + ADAM_WD * w_ref[...])
        nm_ref[...] = nm
        nv_ref[...] = nv

    blk = ((tr, tc), lambda i, j: (i, j))
    sd = jax.ShapeDtypeStruct((r, c), F32)
    return _ew(body, [w, g, m, v], [blk] * 4, [sd, sd, sd], [blk] * 3, (r // tr, c // tc), name=name, ride=ride)


BIG = (("w_in", 1), ("w_sb_out", 1), ("w_ca_out", 1), ("w_mix_out", 0), ("w_ffn_in", 1), ("w_ffn_out", 0),
       ("w_ple_in", 1), ("w_ple_gate", 0))
AXIS = dict(BIG)
TAIL_PARTS = 8
TAIL_HOSTS = {"w_ffn_in": 2, "w_ffn_out": 1}
SMALL = ("rel_bias", "g_mix", "g_ffn", "g_ple", "g_final")
ORDER = ("w_in", "w_sb_out", "w_ca_out", "w_mix_out", "rel_bias", "g_mix", "g_ffn", "g_ple", "g_final",
         "w_ffn_in", "w_ffn_out", "w_ple_in", "w_ple_gate")


def _pack(parts):
    flat = jnp.concatenate([a.reshape(-1) for a in parts])
    rows = -(-flat.shape[0] // 1024) * 8
    return jnp.pad(flat, (0, rows * 128 - flat.shape[0])).reshape(rows, 128)


def _unpack(packed, like):
    flat, out, at = packed.reshape(-1), [], 0
    for a in like:
        out.append(flat[at:at + a.size].reshape(a.shape))
        at += a.size
    return out


def kernel(x, p, w_in, w_sb_out, w_ca_out, w_mix_out, rel_bias, g_mix, g_ffn, g_ple, g_final, w_ffn_in, w_ffn_out, w_ple_in, w_ple_gate, loss_target, m_w_in, m_w_sb_out, m_w_ca_out, m_w_mix_out, m_rel_bias, m_g_mix, m_g_ffn, m_g_ple, m_g_final, m_w_ffn_in, m_w_ffn_out, m_w_ple_in, m_w_ple_gate, v_w_in, v_w_sb_out, v_w_ca_out, v_w_mix_out, v_rel_bias, v_g_mix, v_g_ffn, v_g_ple, v_g_final, v_w_ffn_in, v_w_ffn_out, v_w_ple_in, v_w_ple_gate):
    weights = dict(w_in=w_in, w_sb_out=w_sb_out, w_ca_out=w_ca_out, w_mix_out=w_mix_out, rel_bias=rel_bias,
                   g_mix=g_mix, g_ffn=g_ffn, g_ple=g_ple, g_final=g_final, w_ffn_in=w_ffn_in,
                   w_ffn_out=w_ffn_out, w_ple_in=w_ple_in, w_ple_gate=w_ple_gate)
    m_in = dict(w_in=m_w_in, w_sb_out=m_w_sb_out, w_ca_out=m_w_ca_out, w_mix_out=m_w_mix_out, rel_bias=m_rel_bias,
                g_mix=m_g_mix, g_ffn=m_g_ffn, g_ple=m_g_ple, g_final=m_g_final, w_ffn_in=m_w_ffn_in,
                w_ffn_out=m_w_ffn_out, w_ple_in=m_w_ple_in, w_ple_gate=m_w_ple_gate)
    v_in = dict(w_in=v_w_in, w_sb_out=v_w_sb_out, w_ca_out=v_w_ca_out, w_mix_out=v_w_mix_out, rel_bias=v_rel_bias,
                g_mix=v_g_mix, g_ffn=v_g_ffn, g_ple=v_g_ple, g_final=v_g_final, w_ffn_in=v_w_ffn_in,
                w_ffn_out=v_w_ffn_out, w_ple_in=v_w_ple_in, w_ple_gate=v_w_ple_gate)

    pos = jnp.stack([2 * lax.axis_index("x") + lax.axis_index("y"), lax.axis_index("c")]).astype(jnp.int32)
    placed = {n: _cast_place(weights[n][0], axis, pos, name="cast_" + n) for n, axis in BIG}
    comm = _Comm(pos, placed)
    _run(comm.gather(("w_in",), False), name="gather_w_in_chips")
    _run(comm.gather(("w_in",), True), name="gather_w_in_pair")
    small = dict(rel_bias=rel_bias[0], g_mix=g_mix, g_ffn=g_ffn, g_ple=g_ple, g_final=g_final.reshape(1, -1))
    loss, grad_x, gs = _step(x[0], p[0, 0], loss_target[0], small, comm)

    grads, delta, new_m, new_v = {}, {}, {}, {}
    rows = comm.st["w_in"]["s16"].shape[1]
    at = 0
    for n in [n for n, _ in BIG if n != "w_in"] + ["w_in"]:
        ride = None
        if n in TAIL_HOSTS:
            count = rows * TAIL_HOSTS[n] // TAIL_PARTS
            ride = comm.chips(("w_in",), rows=(at, count))
            at += count
        if n == "w_in":
            _run(comm.chips(("w_in",), rows=(at, rows - at)), name="rs_chips_w_in")
            comm.sum(("w_in",))
            _run(comm.share(("w_in",)), name="rs_share_w_in")
        g = comm.result(n)
        d, nm, nv = _adamw(weights[n][0], g, m_in[n][0], v_in[n][0], name="adamw_" + n, ride=ride)
        grads[n], delta[n], new_m[n], new_v[n] = g[None], d[None], nm[None], nv[None]

    like = [weights[n] for n in SMALL]
    reduced = _small_all_reduce(_pack([gs[n] for n in SMALL] + [loss[:, :1]]), name="small_all_reduce")
    g_small = _unpack(reduced, like + [loss[:, :1]])
    total_loss = g_small[-1].reshape(())
    g_packed = _pack(g_small[:-1])
    d_s, m_s, v_s = _adamw(_pack(like), g_packed, _pack([m_in[n] for n in SMALL]), _pack([v_in[n] for n in SMALL]),
                           name="adamw_small")
    for n, g, d, nm, nv in zip(SMALL, g_small[:-1], _unpack(d_s, like), _unpack(m_s, like), _unpack(v_s, like)):
        grads[n], delta[n], new_m[n], new_v[n] = g, d, nm, nv

    return (total_loss, grad_x[None], *[grads[n] for n in ORDER], *[delta[n] for n in ORDER],
            *[new_m[n] for n in ORDER], *[new_v[n] for n in ORDER])
```

```python
import functools
import math

import jax
import jax.numpy as jnp
import numpy as np
from jax import lax
from jax.experimental import pallas as pl
from jax.experimental.pallas import tpu as pltpu

F32 = jnp.float32
BF16 = jnp.bfloat16

HEAD_DIM = 128
CHUNK = 64
LEFT_CHUNKS = 8
REL_CLIP = 128
N_REL = REL_CLIP + CHUNK
BAND = (LEFT_CHUNKS + 2) * CHUNK
PAD = (LEFT_CHUNKS + 1) * CHUNK
CA_PER_STEP = 4
CA_ROWS = CA_PER_STEP * CHUNK
SB_BLOCK = 128
SB_KEYS = 512
SB_GROUPS = SB_KEYS // SB_BLOCK
SB_ROWS = 512
EPS = 1e-6
NEG = -1e30

ADAM_LR = 0.001
ADAM_B1 = 0.9
ADAM_B2 = 0.999
ADAM_EPS = 1e-08
ADAM_WD = 0.01
ADAM_STEP = 10

VMEM_LIMIT = 48 * 1024 * 1024
MM_VMEM_BUDGET = 36 * 1024 * 1024
V7X_HBM_BYTES_PER_S = 3.7e12
GRID_STEP_S = 0.35e-6
MESH = pl.DeviceIdType.MESH
N_CHIPS = 4


def _pick(dim, prefs):
    for t in prefs:
        if dim % t == 0:
            return t
    raise ValueError(f"no tile for {dim}")


def _cparams(sem=None):
    return pltpu.CompilerParams(dimension_semantics=sem, vmem_limit_bytes=VMEM_LIMIT)


def _sigmoid(v):
    return 1.0 / (1.0 + jnp.exp(-v))


def _dot(a, b, dims):
    return lax.dot_general(a, b, (dims, ((), ())), preferred_element_type=F32)


def _dot_nn(a, b):
    return _dot(a, b, ((1,), (0,)))


def _dot_nt(a, b):
    return _dot(a, b, ((1,), (1,)))


def _dot_tn(a, b):
    return _dot(a, b, ((0,), (0,)))


HBM = pl.BlockSpec(memory_space=pltpu.HBM)


class _Ride:
    def __init__(self):
        self.items = []

    def add(self, ins, outs, aliases, n_sems, start, finish, sink):
        self.items.append((ins, outs, aliases, n_sems, start, finish, sink))


def _call(body, args, *, name, grid, in_specs, out_specs, out_shape, scratch_shapes=(), sem=None, ride=None):
    items = ride.items if ride is not None else []
    n_in, n_out, n_scr = len(args), len(out_shape), len(scratch_shapes)
    r_ins = [a for it in items for a in it[0]]
    r_outs = [o for it in items for o in it[1]]
    aliases, a, b = {}, n_in, n_out
    for it in items:
        aliases.update({a + i: b + o for i, o in it[2].items()})
        a, b = a + len(it[0]), b + len(it[1])
    sems = [pltpu.SemaphoreType.DMA((it[3],)) for it in items for _ in range(2)]

    def wrapped(*refs):
        ins, rin = refs[:n_in], refs[n_in:n_in + len(r_ins)]
        at = n_in + len(r_ins)
        outs, rout = refs[at:at + n_out], refs[at + n_out:at + n_out + len(r_outs)]
        at += n_out + len(r_outs)
        scr, rsem = refs[at:at + n_scr], refs[at + n_scr:]

        def each(which):
            a = b = 0
            for q, it in enumerate(items):
                it[which](rin[a:a + len(it[0])], rout[b:b + len(it[1])], rsem[2 * q], rsem[2 * q + 1])
                a, b = a + len(it[0]), b + len(it[1])

        if items:
            ids = [pl.program_id(d) for d in range(len(grid))]
            first = functools.reduce(jnp.logical_and, [i == 0 for i in ids])
            last = functools.reduce(jnp.logical_and, [i == g - 1 for i, g in zip(ids, grid)])
            pl.when(first)(lambda: each(4))
        body(*ins, *outs, *scr)
        if items:
            pl.when(last)(lambda: each(5))

    res = pl.pallas_call(
        wrapped, name=name, grid=grid,
        in_specs=list(in_specs) + [HBM] * len(r_ins),
        out_specs=list(out_specs) + [HBM] * len(r_outs),
        out_shape=list(out_shape) + r_outs,
        scratch_shapes=list(scratch_shapes) + sems,
        input_output_aliases=aliases,
        compiler_params=_cparams(("arbitrary",) * len(grid) if items else sem),
    )(*args, *r_ins)
    b = n_out
    for it in items:
        it[6](res[b:b + len(it[1])])
        b += len(it[1])
    return list(res[:n_out])


def _mm_tiles(m, n_align, n, k, a_bytes, b_bytes, out_bytes):
    best = None
    tks = sorted({t for t in (k, k // 2, k // 4, 2048, 1024, 512, 256, 128) if t <= k and k % t == 0 and t % 128 == 0})
    for tm in (t for t in (2048, 1024, 512, 256, 128) if m % t == 0):
        for tn in (t for t in (2048, 1024, 512, 256, 128) if n_align % t == 0):
            for tk in tks:
                nk = k // tk
                vmem = 2 * (tm * tk * a_bytes + tk * tn * b_bytes + tm * tn * out_bytes) + tm * tn * 4
                if vmem > MM_VMEM_BUDGET:
                    continue
                traffic = m * k * a_bytes * (n // tn if nk > 1 else 1) + k * n * b_bytes * (m // tm)
                traffic += tm * tk * a_bytes + tk * tn * b_bytes + tm * tn * out_bytes
                traffic += m * n * 4 * nk if nk > 1 else 0
                cost = traffic / V7X_HBM_BYTES_PER_S + (m // tm) * (n // tn) * nk * GRID_STEP_S
                if best is None or cost < best[0]:
                    best = (cost, tm, tn, tk)
    return best[1:]


def _mm(a, b, mode, out_dtypes, *, name, n=None, b_col_off=0, resid=None, ride=None):
    if mode == "nn":
        m, k = a.shape
        n = b.shape[1] if n is None else n
    elif mode == "nt":
        m, k = a.shape
        n = b.shape[0]
    else:
        k, m = a.shape
        n = b.shape[1]
    n_out = len(out_dtypes)
    has_resid = resid is not None
    out_bytes = sum(jnp.dtype(dt).itemsize for dt in out_dtypes) + (4 if has_resid else 0)
    tm, tn, tk = _mm_tiles(m, math.gcd(n, b_col_off) if b_col_off else n, n, k,
                           a.dtype.itemsize, b.dtype.itemsize, out_bytes)
    nk = k // tk
    boff = b_col_off // tn
    dot = {"nn": _dot_nn, "nt": _dot_nt, "tn": _dot_tn}[mode]

    def body(*refs):
        a_ref, b_ref = refs[0], refs[1]
        r_ref = refs[2] if has_resid else None
        o_refs = refs[2 + has_resid: 2 + has_resid + n_out]

        def finish(r):
            if has_resid:
                r = r + r_ref[...]
            for o_ref in o_refs:
                o_ref[...] = r.astype(o_ref.dtype)

        part = dot(a_ref[...].astype(BF16), b_ref[...].astype(BF16))
        if nk == 1:
            finish(part)
            return
        acc_ref = refs[-1]
        kk = pl.program_id(2)

        @pl.when(kk == 0)
        def _():
            acc_ref[...] = part

        @pl.when(kk > 0)
        def _():
            acc_ref[...] += part

        @pl.when(kk == nk - 1)
        def _():
            finish(acc_ref[...])

    if mode == "nn":
        a_spec = pl.BlockSpec((tm, tk), lambda i, j, kk: (i, kk))
        b_spec = pl.BlockSpec((tk, tn), lambda i, j, kk: (kk, j + boff))
    elif mode == "nt":
        a_spec = pl.BlockSpec((tm, tk), lambda i, j, kk: (i, kk))
        b_spec = pl.BlockSpec((tn, tk), lambda i, j, kk: (j, kk))
    else:
        a_spec = pl.BlockSpec((tk, tm), lambda i, j, kk: (kk, i))
        b_spec = pl.BlockSpec((tk, tn), lambda i, j, kk: (kk, j))
    o_spec = pl.BlockSpec((tm, tn), lambda i, j, kk: (i, j))
    in_specs = [a_spec, b_spec] + ([o_spec] if has_resid else [])
    args = [a, b] + ([resid] if has_resid else [])
    outs = _call(
        body, args, name=name,
        grid=(m // tm, n // tn, nk),
        in_specs=in_specs,
        out_specs=[o_spec] * n_out,
        out_shape=[jax.ShapeDtypeStruct((m, n), dt) for dt in out_dtypes],
        scratch_shapes=[pltpu.VMEM((tm, tn), F32)] if nk > 1 else [],
        sem=("parallel", "parallel", "arbitrary"), ride=ride)
    return outs[0] if n_out == 1 else tuple(outs)


def _row_tile(s):
    return _pick(s, (256, 128))


def _rms_fwd(x, g, *, name):
    s, d = x.shape
    tr = _row_tile(s)

    def body(x_ref, g_ref, o_ref):
        xv = x_ref[...]
        r = lax.rsqrt(jnp.mean(xv * xv, axis=1, keepdims=True) + EPS)
        o_ref[...] = (xv * r * g_ref[...]).astype(o_ref.dtype)

    return pl.pallas_call(
        body, name=name, grid=(s // tr,),
        in_specs=[pl.BlockSpec((tr, d), lambda i: (i, 0)), pl.BlockSpec((1, d), lambda i: (0, 0))],
        out_specs=pl.BlockSpec((tr, d), lambda i: (i, 0)),
        out_shape=jax.ShapeDtypeStruct((s, d), BF16),
        compiler_params=_cparams(("parallel",)),
    )(x, g)


def _rms_bwd(x, g, dh, dres, *, name):
    s, d = x.shape
    tr = _row_tile(s)

    def body(x_ref, g_ref, dh_ref, dres_ref, dx_ref, dg_ref):
        i = pl.program_id(0)
        xv = x_ref[...]
        r = lax.rsqrt(jnp.mean(xv * xv, axis=1, keepdims=True) + EPS)
        xhat = xv * r
        dhv = dh_ref[...]
        dxhat = dhv * g_ref[...]
        proj = jnp.mean(dxhat * xhat, axis=1, keepdims=True)
        dx_ref[...] = dres_ref[...] + r * (dxhat - xhat * proj)

        @pl.when(i == 0)
        def _():
            dg_ref[...] = jnp.zeros_like(dg_ref)

        dg_ref[...] += jnp.sum(dhv * xhat, axis=0, keepdims=True)

    row = pl.BlockSpec((tr, d), lambda i: (i, 0))
    vec = pl.BlockSpec((1, d), lambda i: (0, 0))
    return pl.pallas_call(
        body, name=name, grid=(s // tr,),
        in_specs=[row, vec, row, row],
        out_specs=[row, vec],
        out_shape=[jax.ShapeDtypeStruct((s, d), F32), jax.ShapeDtypeStruct((1, d), F32)],
        compiler_params=_cparams(("arbitrary",)),
    )(x, g, dh, dres)


def _final_loss(x, g, target, *, name):
    s, d = x.shape
    tr = _row_tile(s)

    def body(x_ref, g_ref, t_ref, dx_ref, dg_ref, loss_ref):
        i = pl.program_id(0)
        xv = x_ref[...]
        gv = g_ref[...]
        r = lax.rsqrt(jnp.mean(xv * xv, axis=1, keepdims=True) + EPS)
        xhat = xv * r
        err = xhat * gv - t_ref[...]
        dy = err * (1.0 / d)
        dxhat = dy * gv
        proj = jnp.mean(dxhat * xhat, axis=1, keepdims=True)
        dx_ref[...] = r * (dxhat - xhat * proj)

        @pl.when(i == 0)
        def _():
            dg_ref[...] = jnp.zeros_like(dg_ref)
            loss_ref[...] = jnp.zeros_like(loss_ref)

        dg_ref[...] += jnp.sum(dy * xhat, axis=0, keepdims=True)
        part = 0.5 * jnp.sum(jnp.mean(err * err, axis=1, keepdims=True), axis=0, keepdims=True)
        loss_ref[...] += jnp.broadcast_to(part, loss_ref.shape)

    row = pl.BlockSpec((tr, d), lambda i: (i, 0))
    vec = pl.BlockSpec((1, d), lambda i: (0, 0))
    return pl.pallas_call(
        body, name=name, grid=(s // tr,),
        in_specs=[row, vec, row],
        out_specs=[row, vec, pl.BlockSpec((1, 128), lambda i: (0, 0))],
        out_shape=[jax.ShapeDtypeStruct((s, d), F32), jax.ShapeDtypeStruct((1, d), F32),
                   jax.ShapeDtypeStruct((1, 128), F32)],
        compiler_params=_cparams(("arbitrary",)),
    )(x, g, target)


def _ew(body, ins, in_blocks, outs, out_blocks, grid, *, name, ride=None):
    return _call(body, ins, name=name, grid=grid,
                 in_specs=[pl.BlockSpec(bs, im) for bs, im in in_blocks],
                 out_specs=[pl.BlockSpec(bs, im) for bs, im in out_blocks],
                 out_shape=outs, sem=("parallel",) * len(grid), ride=ride)


def _gate_merge_fwd(gates, o_sb, o_ca, *, name):
    s, d = o_sb.shape
    tr, tc = _row_tile(s), _pick(d, (1024, 512, 256, 128))
    nc = d // tc

    def body(gs_ref, gc_ref, os_ref, oc_ref, m_ref):
        m = _sigmoid(gs_ref[...]) * os_ref[...] + _sigmoid(gc_ref[...]) * oc_ref[...]
        m_ref[...] = m.astype(m_ref.dtype)

    blk = ((tr, tc), lambda i, j: (i, j))
    return _ew(body, [gates, gates, o_sb, o_ca],
               [blk, ((tr, tc), lambda i, j: (i, j + nc)), blk, blk],
               [jax.ShapeDtypeStruct((s, d), BF16)], [blk], (s // tr, nc), name=name)[0]


def _gate_merge_bwd(dmerged, gates, o_sb, o_ca, *, name):
    s, d = o_sb.shape
    tr, tc = _row_tile(s), _pick(d, (1024, 512, 256, 128))
    nc = d // tc

    def body(dm_ref, gs_ref, gc_ref, os_ref, oc_ref, dgs_ref, dgc_ref, dos_ref, doc_ref):
        dm = dm_ref[...]
        ss = _sigmoid(gs_ref[...])
        sc = _sigmoid(gc_ref[...])
        dgs_ref[...] = (dm * os_ref[...] * ss * (1.0 - ss)).astype(dgs_ref.dtype)
        dgc_ref[...] = (dm * oc_ref[...] * sc * (1.0 - sc)).astype(dgc_ref.dtype)
        dos_ref[...] = (dm * ss).astype(dos_ref.dtype)
        doc_ref[...] = (dm * sc).astype(doc_ref.dtype)

    blk = ((tr, tc), lambda i, j: (i, j))
    sd = jax.ShapeDtypeStruct((s, d), BF16)
    return _ew(body, [dmerged, gates, gates, o_sb, o_ca],
               [blk, blk, ((tr, tc), lambda i, j: (i, j + nc)), blk, blk],
               [sd, sd, sd, sd], [blk, blk, blk, blk], (s // tr, nc), name=name)


def _swiglu_fwd(gu, *, name, ride=None):
    s, f2 = gu.shape
    f = f2 // 2
    tr, tc = _row_tile(s), _pick(f, (512, 256, 128))
    nc = f // tc

    def body(g_ref, u_ref, a_ref):
        gv = g_ref[...]
        a_ref[...] = (gv * _sigmoid(gv) * u_ref[...]).astype(a_ref.dtype)

    blk = ((tr, tc), lambda i, j: (i, j))
    return _ew(body, [gu, gu], [blk, ((tr, tc), lambda i, j: (i, j + nc))],
               [jax.ShapeDtypeStruct((s, f), BF16)], [blk], (s // tr, nc), name=name, ride=ride)[0]


def _swiglu_bwd(dact, gu, *, name):
    s, f2 = gu.shape
    f = f2 // 2
    tr, tc = 128, _pick(f, (512, 256, 128))

    def body(da_ref, gu_ref, o_ref):
        for at in range(0, f, tc):
            da = da_ref[:, at:at + tc]
            gv = gu_ref[:, at:at + tc]
            sg = _sigmoid(gv)
            o_ref[:, at:at + tc] = (da * gu_ref[:, f + at:f + at + tc] * sg * (1.0 + gv * (1.0 - sg))).astype(o_ref.dtype)
            o_ref[:, f + at:f + at + tc] = (da * gv * sg).astype(o_ref.dtype)

    row = lambda i: (i, 0)
    return _ew(body, [dact, gu], [((tr, f), row), ((tr, f2), row)], [jax.ShapeDtypeStruct((s, f2), BF16)],
               [((tr, f2), row)], (s // tr,), name=name)[0]


def _ple_fwd(x, t, pe, *, name):
    s, d = x.shape
    tr, tc = _row_tile(s), _pick(d, (1024, 512, 256, 128))

    def body(x_ref, t_ref, p_ref, o_ref):
        o_ref[...] = x_ref[...] + _sigmoid(t_ref[...]) * p_ref[...]

    blk = ((tr, tc), lambda i, j: (i, j))
    return _ew(body, [x, t, pe], [blk, blk, blk],
               [jax.ShapeDtypeStruct((s, d), F32)], [blk], (s // tr, d // tc), name=name)[0]


def _ple_bwd(dx, t, pe, *, name):
    s, d = dx.shape
    tr, tc = _row_tile(s), _pick(d, (1024, 512, 256, 128))

    def body(dx_ref, t_ref, p_ref, dt_ref, dp_ref):
        dxv = dx_ref[...]
        sg = _sigmoid(t_ref[...])
        dt_ref[...] = (dxv * p_ref[...] * sg * (1.0 - sg)).astype(dt_ref.dtype)
        dp_ref[...] = (dxv * sg).astype(dp_ref.dtype)

    blk = ((tr, tc), lambda i, j: (i, j))
    sd = jax.ShapeDtypeStruct((s, d), BF16)
    return _ew(body, [dx, t, pe], [blk, blk, blk], [sd, sd], [blk, blk], (s // tr, d // tc), name=name)


def _sb_tri(later):
    row = lax.broadcasted_iota(jnp.int32, (SB_BLOCK, SB_BLOCK), 0)
    col = lax.broadcasted_iota(jnp.int32, (SB_BLOCK, SB_BLOCK), 1)
    tri = (row > col) if later else (row < col)
    return jnp.concatenate([tri.astype(BF16), jnp.ones((SB_BLOCK, SB_BLOCK), BF16)], axis=1)


def _sb_valid(i, j):
    qi = i * SB_ROWS + lax.broadcasted_iota(jnp.int32, (SB_ROWS, SB_KEYS), 0)
    ki = j * SB_KEYS + lax.broadcasted_iota(jnp.int32, (SB_ROWS, SB_KEYS), 1)
    return ki < qi


def _sb_scan(v, tri, run, later):
    hi = v.astype(BF16)
    lo = (v - hi.astype(F32)).astype(BF16)
    outs = [None] * SB_GROUPS
    for b in (reversed(range(SB_GROUPS)) if later else range(SB_GROUPS)):
        cols = slice(b * SB_BLOCK, (b + 1) * SB_BLOCK)
        r = _dot_nn(hi[:, cols], tri) + _dot_nn(lo[:, cols], tri)
        outs[b] = r[:, :SB_BLOCK] + run
        run = run + r[:, SB_BLOCK:]
    return jnp.concatenate(outs, axis=1), run


def _sb_scores(q, kj, scale, valid):
    z = _dot_nt(q, kj) * scale
    t = jnp.log(1.0 + jnp.exp(-jnp.abs(z)))
    return jnp.minimum(z, 0.0) - t, jnp.where(valid, -jnp.maximum(z, 0.0) - t, 0.0)


def _sb_specs(h_count, s, col0):
    q_spec = pl.BlockSpec((SB_ROWS, HEAD_DIM), lambda h, i: (i, col0 + h))
    k_spec = pl.BlockSpec((s, HEAD_DIM), lambda h, i: (0, col0 + h_count + h))
    v_spec = pl.BlockSpec((s, HEAD_DIM), lambda h, i: (0, col0 + 2 * h_count + h))
    return q_spec, k_spec, v_spec


def _sb_fwd(qkv, n_heads, col0, *, name, ride=None):
    s = qkv.shape[0]
    nq = s // SB_ROWS
    scale = HEAD_DIM ** -0.5

    def body(q_ref, k_ref, v_ref, o_ref):
        i = pl.program_id(1)
        steps = ((i + 1) * SB_ROWS - 1) // SB_KEYS + 1
        q = q_ref[...]
        tri = _sb_tri(later=True)

        def step(jj, carry):
            run, acc = carry
            j = steps - 1 - jj
            off = pl.multiple_of(j * SB_KEYS, SB_KEYS)
            valid = _sb_valid(i, j)
            ls, lk = _sb_scores(q, k_ref[pl.ds(off, SB_KEYS), :], scale, valid)
            between, run = _sb_scan(lk, tri, run, later=True)
            a = jnp.where(valid, jnp.exp(ls + between), 0.0)
            return run, acc + _dot_nn(a.astype(BF16), v_ref[pl.ds(off, SB_KEYS), :])

        init = (jnp.zeros((SB_ROWS, SB_BLOCK), F32), jnp.zeros((SB_ROWS, HEAD_DIM), F32))
        _, acc = lax.fori_loop(0, steps, step, init)
        o_ref[...] = acc.astype(o_ref.dtype)

    q_spec, k_spec, v_spec = _sb_specs(n_heads, s, col0)
    return _call(
        body, [qkv, qkv, qkv], name=name, grid=(n_heads, nq),
        in_specs=[q_spec, k_spec, v_spec],
        out_specs=[pl.BlockSpec((SB_ROWS, HEAD_DIM), lambda h, i: (i, h))],
        out_shape=[jax.ShapeDtypeStruct((s, n_heads * HEAD_DIM), BF16)],
        sem=("parallel", "arbitrary"), ride=ride)[0]


def _sb_bwd(qkv, dy, n_heads, col0, *, name, ride=None):
    s = qkv.shape[0]
    nq = s // SB_ROWS
    scale = HEAD_DIM ** -0.5

    def body(q_ref, k_ref, v_ref, dy_ref, dq_ref, dk_ref, dv_ref, e_scr, dk_acc, dv_acc):
        i = pl.program_id(1)
        steps = ((i + 1) * SB_ROWS - 1) // SB_KEYS + 1
        q = q_ref[...]
        dyv = dy_ref[...]

        @pl.when(i == 0)
        def _():
            dk_acc[...] = jnp.zeros_like(dk_acc)
            dv_acc[...] = jnp.zeros_like(dv_acc)

        tri_later = _sb_tri(later=True)

        def pass1(jj, run):
            j = steps - 1 - jj
            off = pl.multiple_of(j * SB_KEYS, SB_KEYS)
            valid = _sb_valid(i, j)
            ls, lk = _sb_scores(q, k_ref[pl.ds(off, SB_KEYS), :], scale, valid)
            between, run = _sb_scan(lk, tri_later, run, later=True)
            a = jnp.where(valid, jnp.exp(ls + between), 0.0)
            e_scr[j] = a * _dot_nt(dyv, v_ref[pl.ds(off, SB_KEYS), :])
            dv_acc[pl.ds(off, SB_KEYS), :] += _dot_tn(a.astype(BF16), dyv)
            return run

        lax.fori_loop(0, steps, pass1, jnp.zeros((SB_ROWS, SB_BLOCK), F32))

        tri_earlier = _sb_tri(later=False)

        def pass2(j, carry):
            run, dq = carry
            off = pl.multiple_of(j * SB_KEYS, SB_KEYS)
            kj = k_ref[pl.ds(off, SB_KEYS), :]
            sg = _sigmoid(_dot_nt(q, kj) * scale)
            e = e_scr[j]
            before, run = _sb_scan(e, tri_earlier, run, later=False)
            dz = jnp.where(_sb_valid(i, j), e * (1.0 - sg) - sg * before, 0.0) * scale
            dzb = dz.astype(BF16)
            dk_acc[pl.ds(off, SB_KEYS), :] += _dot_tn(dzb, q)
            return run, dq + _dot_nn(dzb, kj)

        init = (jnp.zeros((SB_ROWS, SB_BLOCK), F32), jnp.zeros((SB_ROWS, HEAD_DIM), F32))
        _, dq = lax.fori_loop(0, steps, pass2, init)
        dq_ref[...] = dq.astype(dq_ref.dtype)

        @pl.when(i == nq - 1)
        def _():
            dk_ref[...] = dk_acc[...].astype(dk_ref.dtype)
            dv_ref[...] = dv_acc[...].astype(dv_ref.dtype)

    q_spec, k_spec, v_spec = _sb_specs(n_heads, s, col0)
    blk = pl.BlockSpec((SB_ROWS, HEAD_DIM), lambda h, i: (i, h))
    full = pl.BlockSpec((s, HEAD_DIM), lambda h, i: (0, h))
    sd = jax.ShapeDtypeStruct((s, n_heads * HEAD_DIM), BF16)
    return _call(
        body, [qkv, qkv, qkv, dy], name=name, grid=(n_heads, nq),
        in_specs=[q_spec, k_spec, v_spec, blk],
        out_specs=[blk, full, full],
        out_shape=[sd, sd, sd],
        scratch_shapes=[pltpu.VMEM((s // SB_KEYS, SB_ROWS, SB_KEYS), F32),
                        pltpu.VMEM((s, HEAD_DIM), F32), pltpu.VMEM((s, HEAD_DIM), F32)],
        sem=("parallel", "arbitrary"), ride=ride)


def _band_bias(rel_bias):
    h = rel_bias.shape[0]
    width = BAND + CHUNK
    first = width - 1 - N_REL
    line = jnp.concatenate([jnp.broadcast_to(rel_bias[:, :1], (h, first)), rel_bias], axis=1)
    tiled = jnp.broadcast_to(line[:, None, :], (h, CHUNK, width - 1)).reshape(h, CHUNK * (width - 1))
    skew = jnp.pad(tiled, ((0, 0), (0, CHUNK))).reshape(h, CHUNK, width)[:, ::-1, :BAND]
    seen = jnp.arange(BAND) >= CHUNK
    return jnp.where(seen[None, None, :], skew, NEG)


def _band_bias_grad(dbias):
    h = dbias.shape[0]
    width = BAND + CHUNK
    flipped = jnp.pad(dbias[:, ::-1, :], ((0, 0), (0, 0), (0, CHUNK)))
    skew = flipped.reshape(h, CHUNK * width)[:, :CHUNK * (width - 1)].reshape(h, CHUNK, width - 1)
    diag = jnp.sum(skew, axis=1)
    first = width - 1 - N_REL
    clipped = jnp.sum(diag[:, :first + 1], axis=1, keepdims=True)
    return jnp.concatenate([clipped, diag[:, first + 1:]], axis=1)


def _ca_load_padded(k_ref, v_ref, kp, vp, s):
    kp[pl.ds(0, PAD), :] = jnp.zeros((PAD, HEAD_DIM), kp.dtype)
    vp[pl.ds(0, PAD), :] = jnp.zeros((PAD, HEAD_DIM), vp.dtype)
    kp[pl.ds(PAD, s), :] = k_ref[...]
    vp[pl.ds(PAD, s), :] = v_ref[...]


def _ca_weights(q, kb, bias, off, scale):
    z = _dot_nt(q, kb) * scale + bias
    pos = off + lax.broadcasted_iota(jnp.int32, (CHUNK, BAND), 1)
    z = jnp.where(pos >= PAD, z, NEG)
    p = jnp.exp(z - jnp.max(z, axis=1, keepdims=True))
    return p / jnp.sum(p, axis=1, keepdims=True)


def _ca_specs(h_count, s, col0):
    q_spec = pl.BlockSpec((CA_ROWS, HEAD_DIM), lambda h, c: (c, col0 + h))
    k_spec = pl.BlockSpec((s, HEAD_DIM), lambda h, c: (0, col0 + h_count + h))
    v_spec = pl.BlockSpec((s, HEAD_DIM), lambda h, c: (0, col0 + 2 * h_count + h))
    b_spec = pl.BlockSpec((1, CHUNK, BAND), lambda h, c: (h, 0, 0))
    return q_spec, k_spec, v_spec, b_spec


def _ca_fwd(qkv, bias, n_heads, col0, *, name, ride=None):
    s = qkv.shape[0]
    nc = s // CA_ROWS
    scale = HEAD_DIM ** -0.5

    def body(q_ref, k_ref, v_ref, b_ref, o_ref, kp, vp):
        c = pl.program_id(1)

        @pl.when(c == 0)
        def _():
            _ca_load_padded(k_ref, v_ref, kp, vp, s)

        for u in range(CA_PER_STEP):
            rows = pl.ds(u * CHUNK, CHUNK)
            off = pl.multiple_of((c * CA_PER_STEP + u) * CHUNK, CHUNK)
            w = _ca_weights(q_ref[rows, :], kp[pl.ds(off, BAND), :], b_ref[0], off, scale)
            o_ref[rows, :] = _dot_nn(w.astype(BF16), vp[pl.ds(off, BAND), :]).astype(o_ref.dtype)

    q_spec, k_spec, v_spec, b_spec = _ca_specs(n_heads, s, col0)
    return _call(
        body, [qkv, qkv, qkv, bias], name=name, grid=(n_heads, nc),
        in_specs=[q_spec, k_spec, v_spec, b_spec],
        out_specs=[pl.BlockSpec((CA_ROWS, HEAD_DIM), lambda h, c: (c, h))],
        out_shape=[jax.ShapeDtypeStruct((s, n_heads * HEAD_DIM), BF16)],
        scratch_shapes=[pltpu.VMEM((s + PAD, HEAD_DIM), BF16), pltpu.VMEM((s + PAD, HEAD_DIM), BF16)],
        sem=("parallel", "arbitrary"), ride=ride)[0]


def _ca_bwd(qkv, bias, dy, n_heads, col0, *, name, ride=None):
    s = qkv.shape[0]
    nc = s // CA_ROWS
    scale = HEAD_DIM ** -0.5

    def body(q_ref, k_ref, v_ref, b_ref, dy_ref, dq_ref, dk_ref, dv_ref, db_ref, kp, vp, dkp, dvp):
        c = pl.program_id(1)

        @pl.when(c == 0)
        def _():
            _ca_load_padded(k_ref, v_ref, kp, vp, s)
            dkp[...] = jnp.zeros_like(dkp)
            dvp[...] = jnp.zeros_like(dvp)
            db_ref[...] = jnp.zeros_like(db_ref)

        dbias = jnp.zeros((CHUNK, BAND), F32)
        for u in range(CA_PER_STEP):
            rows = pl.ds(u * CHUNK, CHUNK)
            off = pl.multiple_of((c * CA_PER_STEP + u) * CHUNK, CHUNK)
            q = q_ref[rows, :]
            dyv = dy_ref[rows, :]
            kb = kp[pl.ds(off, BAND), :]
            w = _ca_weights(q, kb, b_ref[0], off, scale)
            dw = _dot_nt(dyv, vp[pl.ds(off, BAND), :])
            dvp[pl.ds(off, BAND), :] += _dot_tn(w.astype(BF16), dyv)
            dz = w * (dw - jnp.sum(w * dw, axis=1, keepdims=True))
            dbias = dbias + dz
            dzs = (dz * scale).astype(BF16)
            dq_ref[rows, :] = _dot_nn(dzs, kb).astype(dq_ref.dtype)
            dkp[pl.ds(off, BAND), :] += _dot_tn(dzs, q)
        db_ref[0] += dbias

        @pl.when(c == nc - 1)
        def _():
            dk_ref[...] = dkp[pl.ds(PAD, s), :].astype(dk_ref.dtype)
            dv_ref[...] = dvp[pl.ds(PAD, s), :].astype(dv_ref.dtype)

    q_spec, k_spec, v_spec, b_spec = _ca_specs(n_heads, s, col0)
    blk = pl.BlockSpec((CA_ROWS, HEAD_DIM), lambda h, c: (c, h))
    full = pl.BlockSpec((s, HEAD_DIM), lambda h, c: (0, h))
    sd = jax.ShapeDtypeStruct((s, n_heads * HEAD_DIM), BF16)
    return _call(
        body, [qkv, qkv, qkv, bias, dy], name=name, grid=(n_heads, nc),
        in_specs=[q_spec, k_spec, v_spec, b_spec, blk],
        out_specs=[blk, full, full, b_spec],
        out_shape=[sd, sd, sd, jax.ShapeDtypeStruct((n_heads, CHUNK, BAND), F32)],
        scratch_shapes=[pltpu.VMEM((s + PAD, HEAD_DIM), BF16), pltpu.VMEM((s + PAD, HEAD_DIM), BF16),
                        pltpu.VMEM((s + PAD, HEAD_DIM), F32), pltpu.VMEM((s + PAD, HEAD_DIM), F32)],
        sem=("parallel", "arbitrary"), ride=ride)


EARLY = ("w_sb_out", "w_ca_out", "w_mix_out")


def _step(x, p, target, small, comm):
    w = comm.w
    d = x.shape[1]
    n_sb = w["w_sb_out"].shape[0] // HEAD_DIM
    n_ca = w["w_ca_out"].shape[0] // HEAD_DIM
    qkv_cols = 3 * HEAD_DIM * (n_sb + n_ca)
    ca_col0 = 3 * n_sb
    both = (F32, BF16)

    h1 = _rms_fwd(x, small["g_mix"], name="rms_mix")
    qkv = _mm(h1, w["w_in"], "nn", (BF16,), name="proj_qkv", n=qkv_cols, ride=comm.gather(EARLY, False))
    gates = _mm(h1, w["w_in"], "nn", (F32,), name="proj_gates", n=2 * d, b_col_off=qkv_cols,
                ride=comm.gather(EARLY, True))
    bias = _band_bias(small["rel_bias"])
    y_sb = _sb_fwd(qkv, n_sb, 0, name="sb_fwd", ride=comm.gather(("w_ffn_in",), False, part=(0, 2)))
    y_ca = _ca_fwd(qkv, bias, n_ca, ca_col0, name="ca_fwd", ride=comm.gather(("w_ffn_in",), False, part=(1, 2)))
    o_sb = _mm(y_sb, w["w_sb_out"], "nn", (F32,), name="sb_out", ride=comm.gather(("w_ffn_in",), True))
    o_ca = _mm(y_ca, w["w_ca_out"], "nn", (F32,), name="ca_out")
    merged = _gate_merge_fwd(gates, o_sb, o_ca, name="gate_merge")
    x1 = _mm(merged, w["w_mix_out"], "nn", (F32,), name="mix_out", resid=x)
    h2 = _rms_fwd(x1, small["g_ffn"], name="rms_ffn")
    gu = _mm(h2, w["w_ffn_in"], "nn", (F32,), name="ffn_in", ride=comm.gather(("w_ffn_out",), False))
    act = _swiglu_fwd(gu, name="swiglu", ride=comm.gather(("w_ple_gate", "w_ple_in"), False, comm.gather(("w_ffn_out",), True)))
    x2 = _mm(act, w["w_ffn_out"], "nn", (F32,), name="ffn_out", resid=x1,
             ride=comm.gather(("w_ple_gate", "w_ple_in"), True))
    h3 = _rms_fwd(x2, small["g_ple"], name="rms_ple")
    t = _mm(h3, w["w_ple_gate"], "nn", (F32,), name="ple_gate")
    pe = _mm(p, w["w_ple_in"], "nn", (F32,), name="ple_in")
    x3 = _ple_fwd(x2, t, pe, name="ple_add")

    gs = {}
    dx3, gs["g_final"], loss = _final_loss(x3, small["g_final"], target, name="final_loss")
    dt, dpe = _ple_bwd(dx3, t, pe, name="ple_bwd")
    comm.grad("w_ple_in", *_mm(p, dpe, "tn", both, name="dw_ple_in"))
    comm.grad("w_ple_gate", *_mm(h3, dt, "tn", both, name="dw_ple_gate"))
    ple = ("w_ple_in", "w_ple_gate")
    dh3 = _mm(dt, w["w_ple_gate"], "nt", (F32,), name="dh_ple", ride=comm.pair(ple))
    dx2, gs["g_ple"] = _rms_bwd(x2, small["g_ple"], dh3, dx3, name="rms_ple_bwd")
    comm.add(ple)
    comm.grad("w_ffn_out", *_mm(act, dx2, "tn", both, name="dw_ffn_out", ride=comm.chips(ple)))
    dact = _mm(dx2, w["w_ffn_out"], "nt", (F32,), name="dact", ride=comm.pair(("w_ffn_out",)))
    dgu = _swiglu_bwd(dact, gu, name="swiglu_bwd")
    comm.sum(ple)
    comm.add(("w_ffn_out",))
    comm.grad("w_ffn_in", *_mm(h2, dgu, "tn", both, name="dw_ffn_in",
                               ride=comm.share(ple, comm.chips(("w_ffn_out",)))))
    dh2 = _mm(dgu, w["w_ffn_in"], "nt", (F32,), name="dh_ffn", ride=comm.pair(("w_ffn_in",)))
    dx1, gs["g_ffn"] = _rms_bwd(x1, small["g_ffn"], dh2, dx2, name="rms_ffn_bwd")
    comm.add(("w_ffn_in",))
    comm.sum(("w_ffn_out",))
    comm.grad("w_mix_out", *_mm(merged, dx1, "tn", both, name="dw_mix_out", ride=comm.share(("w_ffn_out",))))
    dmerged = _mm(dx1, w["w_mix_out"], "nt", (F32,), name="dmerged", ride=comm.pair(("w_mix_out",)))
    dg_sb, dg_ca, do_sb, do_ca = _gate_merge_bwd(dmerged, gates, o_sb, o_ca, name="gate_merge_bwd")
    comm.add(("w_mix_out",))
    comm.grad("w_sb_out", *_mm(y_sb, do_sb, "tn", both, name="dw_sb_out"))
    comm.grad("w_ca_out", *_mm(y_ca, do_ca, "tn", both, name="dw_ca_out"))
    outs = ("w_sb_out", "w_ca_out")
    dy_sb = _mm(do_sb, w["w_sb_out"], "nt", (BF16,), name="dy_sb", ride=comm.pair(outs))
    dy_ca = _mm(do_ca, w["w_ca_out"], "nt", (BF16,), name="dy_ca")
    comm.add(outs)
    dq_sb, dk_sb, dv_sb = _sb_bwd(qkv, dy_sb, n_sb, 0, name="sb_bwd", ride=comm.chips(("w_ffn_in",)))
    comm.sum(("w_ffn_in",))
    late = ("w_mix_out",) + outs
    dq_ca, dk_ca, dv_ca, dbias = _ca_bwd(qkv, bias, dy_ca, n_ca, ca_col0, name="ca_bwd",
                                         ride=comm.chips(late, comm.share(("w_ffn_in",))))
    comm.sum(late)
    gs["rel_bias"] = _band_bias_grad(dbias)
    dproj = jnp.concatenate([dq_sb, dk_sb, dv_sb, dq_ca, dk_ca, dv_ca, dg_sb, dg_ca], axis=1)
    comm.grad("w_in", *_mm(h1, dproj, "tn", both, name="dw_in", ride=comm.share(late)))
    dh1 = _mm(dproj, w["w_in"], "nt", (F32,), name="dh_mix", ride=comm.pair(("w_in",)))
    grad_x, gs["g_mix"] = _rms_bwd(x, small["g_mix"], dh1, dx1, name="rms_mix_bwd")
    comm.add(("w_in",))
    return loss, grad_x, gs


def _position():
    x, y, c = lax.axis_index("x"), lax.axis_index("y"), lax.axis_index("c")
    chips = [(1 - x, y), (x, 1 - y), (1 - x, 1 - y)]
    return x, y, c, chips


def _aligned(v, m):
    return v if isinstance(v, int) else pl.multiple_of(v, m)


def _piece_dims(shape, axis):
    k, n = shape
    return (k // 2, n // N_CHIPS) if axis == 1 else (k // N_CHIPS // 2, n)


def _piece(ref, shape, axis, j, h, part=(0, 1)):
    pr, pc = _piece_dims(shape, axis)
    nr = pr // part[1]
    r0 = part[0] * nr
    if axis == 1:
        return ref.at[pl.ds(_aligned(h * pr + r0, 16), nr), pl.ds(_aligned(j * pc, 128), pc)]
    return ref.at[pl.ds(_aligned((2 * j + h) * pr + r0, 16), nr), :]


def _shard_half(ref, h):
    rows = ref.shape[0] // 2
    return ref.at[pl.ds(_aligned(h * rows, 16), rows), :]


def _remote(src, dst, send_sems, recv_sems, k, to):
    return pltpu.make_async_remote_copy(src_ref=src, dst_ref=dst, send_sem=send_sems.at[k],
                                        recv_sem=recv_sems.at[k], device_id=to, device_id_type=MESH)


def _prefetch_call(body, scalars, ins, in_specs, out_shape, out_specs, grid, *, name):
    spec = pltpu.PrefetchScalarGridSpec(num_scalar_prefetch=1, grid=grid, in_specs=in_specs, out_specs=out_specs)
    return pl.pallas_call(body, name=name, grid_spec=spec, out_shape=out_shape,
                          compiler_params=_cparams(("parallel",) * len(grid)))(scalars, *ins)


def _slab_tiles(pr, pc):
    tc = pc if pc <= 4096 else _pick(pc, (2048, 1024, 512, 256, 128))
    tr = next(t for t in (1024, 512, 256, 128, 64, 32, 16) if pr % t == 0 and t * tc <= 512 * 1024)
    return tr, tc


def _cast_place(w, axis, pos, *, name):
    ks, ns = w.shape
    shape = (ks, ns * N_CHIPS) if axis == 1 else (ks * N_CHIPS, ns)
    tr, tc = _slab_tiles(ks, ns)
    nr, nc = ks // tr, ns // tc

    def body(pos_ref, w_ref, o_ref):
        o_ref[...] = w_ref[...].astype(o_ref.dtype)

    if axis == 1:
        out_map = lambda i, j, pos_ref: (i, pos_ref[0] * nc + j)
    else:
        out_map = lambda i, j, pos_ref: (pos_ref[0] * nr + i, j)
    return _prefetch_call(body, pos, [w], [pl.BlockSpec((tr, tc), lambda i, j, pos_ref: (i, j))],
                          jax.ShapeDtypeStruct(shape, BF16), pl.BlockSpec((tr, tc), out_map), (nr, nc), name=name)


def _run(ride, *, name):
    if ride is None:
        return

    def body(o_ref):
        o_ref[...] = jnp.zeros_like(o_ref)

    _call(body, [], name=name, grid=(1,), in_specs=[], out_specs=[pl.BlockSpec((8, 128), lambda i: (0, 0))],
          out_shape=[jax.ShapeDtypeStruct((8, 128), F32)], ride=ride)


def _ride_gather(ride, w, n, axis, to_sibling, part=(0, 1)):
    shape = w[n].shape
    piece = functools.partial(_piece, shape=shape, axis=axis, part=part)

    def copies(ins, outs, send_sems, recv_sems, arriving):
        x, y, c, chips = _position()
        me = 2 * x + y
        out = []
        for k, (px, py) in enumerate(chips):
            to = (x, y, 1 - c) if to_sibling else (px, py, c)
            if arriving:
                lands = piece(outs[0], j=2 * px + py, h=1 - c if to_sibling else c)
                out.append(_remote(lands, lands, send_sems, recv_sems, k, to))
            else:
                chip = 2 * px + py if to_sibling else me
                out.append(_remote(piece(ins[0], j=chip, h=c), piece(outs[0], j=chip, h=c), send_sems, recv_sems, k, to))
        return out

    def start(*refs):
        for cp in copies(*refs, arriving=False):
            cp.start()

    def finish(*refs):
        for cp in copies(*refs, arriving=True):
            cp.wait_recv()
        for cp in copies(*refs, arriving=False):
            cp.wait_send()

    ride.add([w[n]], [jax.ShapeDtypeStruct(shape, w[n].dtype)], {0: 0}, 3, start, finish,
             lambda outs: w.__setitem__(n, outs[0]))


def _ride_pair(ride, st, axis):
    shape = st["g16"].shape
    pr, pc = _piece_dims(shape, axis)

    def copies(ins, outs, send_sems, recv_sems):
        x, y, c, _ = _position()
        return [_remote(_piece(ins[0], shape, axis, j, 1 - c), outs[0].at[j], send_sems, recv_sems, j, (x, y, 1 - c))
                for j in range(N_CHIPS)]

    def start(*refs):
        for cp in copies(*refs):
            cp.start()

    def finish(*refs):
        for cp in copies(*refs):
            cp.wait()

    ride.add([st["g16"]], [jax.ShapeDtypeStruct((N_CHIPS, pr, pc), BF16)], {}, N_CHIPS, start, finish,
             lambda outs: st.__setitem__("sib", outs[0]))


def _ride_chips(ride, st, rows=None):
    _, pr, pc = st["s16"].shape
    r0, nr = (0, pr) if rows is None else rows

    def copies(ins, outs, send_sems, recv_sems):
        x, y, c, chips = _position()
        return [_remote(ins[0].at[2 * px + py, pl.ds(r0, nr), :], outs[0].at[k, pl.ds(r0, nr), :],
                        send_sems, recv_sems, k, (px, py, c)) for k, (px, py) in enumerate(chips)]

    def start(*refs):
        for cp in copies(*refs):
            cp.start()

    def finish(*refs):
        for cp in copies(*refs):
            cp.wait()

    ins, aliases = ([st["s16"], st["recv"]], {1: 0}) if "recv" in st else ([st["s16"]], {})
    ride.add(ins, [jax.ShapeDtypeStruct((3, pr, pc), BF16)], aliases, 3, start, finish,
             lambda outs: st.__setitem__("recv", outs[0]))


def _ride_share(ride, st):
    def sent(ins, outs, send_sems, recv_sems):
        x, y, c, _ = _position()
        return _remote(_shard_half(ins[0], c), _shard_half(outs[0], c), send_sems, recv_sems, 0, (x, y, 1 - c))

    def landed(ins, outs, send_sems, recv_sems):
        x, y, c, _ = _position()
        other = _shard_half(outs[0], 1 - c)
        return _remote(other, other, send_sems, recv_sems, 0, (x, y, 1 - c))

    def start(*refs):
        sent(*refs).start()

    def finish(*refs):
        landed(*refs).wait_recv()
        sent(*refs).wait_send()

    ride.add([st["shard"]], [jax.ShapeDtypeStruct(st["shard"].shape, F32)], {0: 0}, 1, start, finish,
             lambda outs: st.__setitem__("g", outs[0]))


def _pair_add(g32, sib, axis, pos, *, name):
    _, pr, pc = sib.shape
    tr, tc = _slab_tiles(pr, pc)
    nr, nc = pr // tr, pc // tc

    def body(pos_ref, g_ref, b_ref, o32_ref, o16_ref):
        r = g_ref[...] + b_ref[0].astype(F32)
        o32_ref[0] = r
        o16_ref[0] = r.astype(o16_ref.dtype)

    if axis == 1:
        g_map = lambda j, i, k, pos_ref: (pos_ref[1] * nr + i, j * nc + k)
    else:
        g_map = lambda j, i, k, pos_ref: ((2 * j + pos_ref[1]) * nr + i, k)
    blk = pl.BlockSpec((1, tr, tc), lambda j, i, k, pos_ref: (j, i, k))
    return _prefetch_call(body, pos, [g32, sib], [pl.BlockSpec((tr, tc), g_map), blk],
                          [jax.ShapeDtypeStruct(sib.shape, F32), jax.ShapeDtypeStruct(sib.shape, BF16)],
                          [blk, blk], (N_CHIPS, nr, nc), name=name)


def _chip_sum(s32, recv, pos, *, name):
    _, pr, pc = s32.shape
    tr, tc = _slab_tiles(pr, pc)
    nr, nc = pr // tr, pc // tc

    def body(pos_ref, m_ref, r_ref, o_ref):
        o_ref[...] = ((m_ref[0] + r_ref[0].astype(F32)) + r_ref[1].astype(F32)) + r_ref[2].astype(F32)

    return _prefetch_call(
        body, pos, [s32, recv],
        [pl.BlockSpec((1, tr, tc), lambda i, k, pos_ref: (pos_ref[0], i, k)),
         pl.BlockSpec((3, tr, tc), lambda i, k, pos_ref: (0, i, k))],
        jax.ShapeDtypeStruct((2 * pr, pc), F32),
        pl.BlockSpec((tr, tc), lambda i, k, pos_ref: (pos_ref[1] * nr + i, k)), (nr, nc), name=name)


class _Comm:
    def __init__(self, pos, w):
        self.pos, self.w, self.st = pos, w, {n: {} for n, _ in BIG}

    def gather(self, names, to_sibling, ride=None, part=(0, 1)):
        ride = _Ride() if ride is None else ride
        for n in names:
            _ride_gather(ride, self.w, n, AXIS[n], to_sibling, part)
        return ride

    def grad(self, n, g32, g16):
        self.st[n].update(g32=g32, g16=g16)

    def pair(self, names, ride=None):
        ride = _Ride() if ride is None else ride
        for n in names:
            _ride_pair(ride, self.st[n], AXIS[n])
        return ride

    def add(self, names):
        for n in names:
            st = self.st[n]
            st["s32"], st["s16"] = _pair_add(st["g32"], st["sib"], AXIS[n], self.pos, name="rs_add_" + n)

    def chips(self, names, ride=None, rows=None):
        ride = _Ride() if ride is None else ride
        for n in names:
            _ride_chips(ride, self.st[n], rows)
        return ride

    def sum(self, names):
        for n in names:
            st = self.st[n]
            st["shard"] = _chip_sum(st["s32"], st["recv"], self.pos, name="rs_sum_" + n)

    def share(self, names, ride=None):
        ride = _Ride() if ride is None else ride
        for n in names:
            _ride_share(ride, self.st[n])
        return ride

    def result(self, n):
        return self.st[n]["g"]


class _NoComm:
    def __init__(self, w):
        self.w, self.st = w, {}

    def grad(self, n, g32, g16):
        self.st[n] = (g32, g16)

    def result(self, n):
        return self.st[n]

    def add(self, names):
        pass

    sum = add

    def gather(self, names, *args, **kwargs):
        return None

    pair = chips = share = gather


def _small_all_reduce(vec, *, name):
    r = vec.shape[0]

    def body(vec_ref, out_ref, slots, send_sems, recv_sems):
        x, y, c, _ = _position()
        me = 4 * x + 2 * y + c
        slots[me] = vec_ref[...]
        sends = []
        for k in range(1, 8):
            to = (x ^ (k >> 2), y ^ ((k >> 1) & 1), c ^ (k & 1))
            cp = _remote(slots.at[me], slots.at[me], send_sems, recv_sems, k - 1, to)
            cp.start()
            sends.append(cp)
        for k in range(1, 8):
            frm = 4 * (x ^ (k >> 2)) + 2 * (y ^ ((k >> 1) & 1)) + (c ^ (k & 1))
            _remote(slots.at[frm], slots.at[frm], send_sems, recv_sems, k - 1, (x, y, c)).wait_recv()
        for cp in sends:
            cp.wait_send()
        total = slots[0]
        for d in range(1, 8):
            total = total + slots[d]
        out_ref[...] = total

    return pl.pallas_call(
        body, name=name,
        in_specs=[pl.BlockSpec(memory_space=pltpu.VMEM)], out_specs=pl.BlockSpec(memory_space=pltpu.VMEM),
        out_shape=jax.ShapeDtypeStruct((r, 128), F32),
        scratch_shapes=[pltpu.VMEM((8, r, 128), F32), pltpu.SemaphoreType.DMA((7,)), pltpu.SemaphoreType.DMA((7,))],
    )(vec)


def _adamw(w, g, m, v, *, name, ride=None):
    r, c = w.shape
    tc = c if c <= 4096 else _pick(c, (2048, 1024, 512, 256, 128))
    tr = next(t for t in (512, 256, 128, 64, 32, 16, 8) if r % t == 0 and t * tc <= 256 * 1024)

    def body(w_ref, g_ref, m_ref, v_ref, d_ref, nm_ref, nv_ref):
        gv = g_ref[...]
        nm = ADAM_B1 * m_ref[...] + (1.0 - ADAM_B1) * gv
        nv = ADAM_B2 * v_ref[...] + (1.0 - ADAM_B2) * (gv * gv)
        m_hat = nm / (1.0 - ADAM_B1 ** ADAM_STEP)
        v_hat = nv / (1.0 - ADAM_B2 ** ADAM_STEP)
        d_ref[...] = -ADAM_LR * (m_hat / (jnp.sqrt(v_hat) + ADAM_EPS) + ADAM_WD * w_ref[...])
        nm_ref[...] = nm
        nv_ref[...] = nv

    blk = ((tr, tc), lambda i, j: (i, j))
    sd = jax.ShapeDtypeStruct((r, c), F32)
    return _ew(body, [w, g, m, v], [blk] * 4, [sd, sd, sd], [blk] * 3, (r // tr, c // tc), name=name, ride=ride)


BIG = (("w_in", 1), ("w_sb_out", 1), ("w_ca_out", 1), ("w_mix_out", 0), ("w_ffn_in", 1), ("w_ffn_out", 0),
       ("w_ple_in", 1), ("w_ple_gate", 0))
AXIS = dict(BIG)
TAIL_PARTS = 8
TAIL_HOSTS = {"w_ffn_in": 2, "w_ffn_out": 1}
SMALL = ("rel_bias", "g_mix", "g_ffn", "g_ple", "g_final")
ORDER = ("w_in", "w_sb_out", "w_ca_out", "w_mix_out", "rel_bias", "g_mix", "g_ffn", "g_ple", "g_final",
         "w_ffn_in", "w_ffn_out", "w_ple_in", "w_ple_gate")


def _pack(parts):
    flat = jnp.concatenate([a.reshape(-1) for a in parts])
    rows = -(-flat.shape[0] // 1024) * 8
    return jnp.pad(flat, (0, rows * 128 - flat.shape[0])).reshape(rows, 128)


def _unpack(packed, like):
    flat, out, at = packed.reshape(-1), [], 0
    for a in like:
        out.append(flat[at:at + a.size].reshape(a.shape))
        at += a.size
    return out


def kernel(x, p, w_in, w_sb_out, w_ca_out, w_mix_out, rel_bias, g_mix, g_ffn, g_ple, g_final, w_ffn_in, w_ffn_out, w_ple_in, w_ple_gate, loss_target, m_w_in, m_w_sb_out, m_w_ca_out, m_w_mix_out, m_rel_bias, m_g_mix, m_g_ffn, m_g_ple, m_g_final, m_w_ffn_in, m_w_ffn_out, m_w_ple_in, m_w_ple_gate, v_w_in, v_w_sb_out, v_w_ca_out, v_w_mix_out, v_rel_bias, v_g_mix, v_g_ffn, v_g_ple, v_g_final, v_w_ffn_in, v_w_ffn_out, v_w_ple_in, v_w_ple_gate):
    weights = dict(w_in=w_in, w_sb_out=w_sb_out, w_ca_out=w_ca_out, w_mix_out=w_mix_out, rel_bias=rel_bias,
                   g_mix=g_mix, g_ffn=g_ffn, g_ple=g_ple, g_final=g_final, w_ffn_in=w_ffn_in,
                   w_ffn_out=w_ffn_out, w_ple_in=w_ple_in, w_ple_gate=w_ple_gate)
    m_in = dict(w_in=m_w_in, w_sb_out=m_w_sb_out, w_ca_out=m_w_ca_out, w_mix_out=m_w_mix_out, rel_bias=m_rel_bias,
                g_mix=m_g_mix, g_ffn=m_g_ffn, g_ple=m_g_ple, g_final=m_g_final, w_ffn_in=m_w_ffn_in,
                w_ffn_out=m_w_ffn_out, w_ple_in=m_w_ple_in, w_ple_gate=m_w_ple_gate)
    v_in = dict(w_in=v_w_in, w_sb_out=v_w_sb_out, w_ca_out=v_w_ca_out, w_mix_out=v_w_mix_out, rel_bias=v_rel_bias,
                g_mix=v_g_mix, g_ffn=v_g_ffn, g_ple=v_g_ple, g_final=v_g_final, w_ffn_in=v_w_ffn_in,
                w_ffn_out=v_w_ffn_out, w_ple_in=v_w_ple_in, w_ple_gate=v_w_ple_gate)

    pos = jnp.stack([2 * lax.axis_index("x") + lax.axis_index("y"), lax.axis_index("c")]).astype(jnp.int32)
    placed = {n: _cast_place(weights[n][0], axis, pos, name="cast_" + n) for n, axis in BIG}
    comm = _Comm(pos, placed)
    _run(comm.gather(("w_in",), False), name="gather_w_in_chips")
    _run(comm.gather(("w_in",), True), name="gather_w_in_pair")
    small = dict(rel_bias=rel_bias[0], g_mix=g_mix, g_ffn=g_ffn, g_ple=g_ple, g_final=g_final.reshape(1, -1))
    loss, grad_x, gs = _step(x[0], p[0, 0], loss_target[0], small, comm)

    grads, delta, new_m, new_v = {}, {}, {}, {}
    rows = comm.st["w_in"]["s16"].shape[1]
    at = 0
    for n in [n for n, _ in BIG if n != "w_in"] + ["w_in"]:
        ride = None
        if n in TAIL_HOSTS:
            count = rows * TAIL_HOSTS[n] // TAIL_PARTS
            ride = comm.chips(("w_in",), rows=(at, count))
            at += count
        if n == "w_in":
            _run(comm.chips(("w_in",), rows=(at, rows - at)), name="rs_chips_w_in")
            comm.sum(("w_in",))
            _run(comm.share(("w_in",)), name="rs_share_w_in")
        g = comm.result(n)
        d, nm, nv = _adamw(weights[n][0], g, m_in[n][0], v_in[n][0], name="adamw_" + n, ride=ride)
        grads[n], delta[n], new_m[n], new_v[n] = g[None], d[None], nm[None], nv[None]

    like = [weights[n] for n in SMALL]
    reduced = _small_all_reduce(_pack([gs[n] for n in SMALL] + [loss[:, :1]]), name="small_all_reduce")
    g_small = _unpack(reduced, like + [loss[:, :1]])
    total_loss = g_small[-1].reshape(())
    g_packed = _pack(g_small[:-1])
    d_s, m_s, v_s = _adamw(_pack(like), g_packed, _pack([m_in[n] for n in SMALL]), _pack([v_in[n] for n in SMALL]),
                           name="adamw_small")
    for n, g, d, nm, nv in zip(SMALL, g_small[:-1], _unpack(d_s, like), _unpack(m_s, like), _unpack(v_s, like)):
        grads[n], delta[n], new_m[n], new_v[n] = g, d, nm, nv

    return (total_loss, grad_x[None], *[grads[n] for n in ORDER], *[delta[n] for n in ORDER],
            *[new_m[n] for n in ORDER], *[new_v[n] for n in ORDER])
```

```python
import functools
import math

import jax
import jax.numpy as jnp
import numpy as np
from jax import lax
from jax.experimental import pallas as pl
from jax.experimental.pallas import tpu as pltpu

F32 = jnp.float32
BF16 = jnp.bfloat16

HEAD_DIM = 128
CHUNK = 64
LEFT_CHUNKS = 8
REL_CLIP = 128
N_REL = REL_CLIP + CHUNK
BAND = (LEFT_CHUNKS + 2) * CHUNK
CA_PER_STEP = 4
CA_ROWS = CA_PER_STEP * CHUNK
CA_BAND = BAND + CA_PER_STEP * CHUNK
CA_PAD = BAND
SB_BLOCK = 128
SB_KEYS = 512
SB_GROUPS = SB_KEYS // SB_BLOCK
SB_ROWS = 512
EPS = 1e-6
NEG = -1e30

ADAM_LR = 0.001
ADAM_B1 = 0.9
ADAM_B2 = 0.999
ADAM_EPS = 1e-08
ADAM_WD = 0.01
ADAM_STEP = 10

VMEM_LIMIT = 48 * 1024 * 1024
MM_VMEM_BUDGET = 36 * 1024 * 1024
V7X_HBM_BYTES_PER_S = 3.7e12
GRID_STEP_S = 0.35e-6
MESH = pl.DeviceIdType.MESH
N_CHIPS = 4


def _pick(dim, prefs):
    for t in prefs:
        if dim % t == 0:
            return t
    raise ValueError(f"no tile for {dim}")


def _cparams(sem=None):
    return pltpu.CompilerParams(dimension_semantics=sem, vmem_limit_bytes=VMEM_LIMIT)


def _sigmoid(v):
    return 1.0 / (1.0 + jnp.exp(-v))


def _dot(a, b, dims):
    return lax.dot_general(a, b, (dims, ((), ())), preferred_element_type=F32)


def _dot_nn(a, b):
    return _dot(a, b, ((1,), (0,)))


def _dot_nt(a, b):
    return _dot(a, b, ((1,), (1,)))


def _dot_tn(a, b):
    return _dot(a, b, ((0,), (0,)))


HBM = pl.BlockSpec(memory_space=pltpu.HBM)


class _Ride:
    def __init__(self):
        self.items = []

    def add(self, ins, outs, aliases, n_sems, start, finish, sink):
        self.items.append((ins, outs, aliases, n_sems, start, finish, sink))


def _call(body, args, *, name, grid, in_specs, out_specs, out_shape, scratch_shapes=(), sem=None, ride=None):
    items = ride.items if ride is not None else []
    n_in, n_out, n_scr = len(args), len(out_shape), len(scratch_shapes)
    r_ins = [a for it in items for a in it[0]]
    r_outs = [o for it in items for o in it[1]]
    aliases, a, b = {}, n_in, n_out
    for it in items:
        aliases.update({a + i: b + o for i, o in it[2].items()})
        a, b = a + len(it[0]), b + len(it[1])
    sems = [pltpu.SemaphoreType.DMA((it[3],)) for it in items for _ in range(2)]

    def wrapped(*refs):
        ins, rin = refs[:n_in], refs[n_in:n_in + len(r_ins)]
        at = n_in + len(r_ins)
        outs, rout = refs[at:at + n_out], refs[at + n_out:at + n_out + len(r_outs)]
        at += n_out + len(r_outs)
        scr, rsem = refs[at:at + n_scr], refs[at + n_scr:]

        def each(which):
            a = b = 0
            for q, it in enumerate(items):
                it[which](rin[a:a + len(it[0])], rout[b:b + len(it[1])], rsem[2 * q], rsem[2 * q + 1])
                a, b = a + len(it[0]), b + len(it[1])

        if items:
            ids = [pl.program_id(d) for d in range(len(grid))]
            first = functools.reduce(jnp.logical_and, [i == 0 for i in ids])
            last = functools.reduce(jnp.logical_and, [i == g - 1 for i, g in zip(ids, grid)])
            pl.when(first)(lambda: each(4))
        body(*ins, *outs, *scr)
        if items:
            pl.when(last)(lambda: each(5))

    res = pl.pallas_call(
        wrapped, name=name, grid=grid,
        in_specs=list(in_specs) + [HBM] * len(r_ins),
        out_specs=list(out_specs) + [HBM] * len(r_outs),
        out_shape=list(out_shape) + r_outs,
        scratch_shapes=list(scratch_shapes) + sems,
        input_output_aliases=aliases,
        compiler_params=_cparams(("arbitrary",) * len(grid) if items else sem),
    )(*args, *r_ins)
    b = n_out
    for it in items:
        it[6](res[b:b + len(it[1])])
        b += len(it[1])
    return list(res[:n_out])


def _mm_tiles(m, n_align, n, k, a_bytes, b_bytes, out_bytes):
    best = None
    tks = sorted({t for t in (k, k // 2, k // 4, 2048, 1024, 512, 256, 128) if t <= k and k % t == 0 and t % 128 == 0})
    for tm in (t for t in (2048, 1024, 512, 256, 128) if m % t == 0):
        for tn in (t for t in (2048, 1024, 512, 256, 128) if n_align % t == 0):
            for tk in tks:
                nk = k // tk
                vmem = 2 * (tm * tk * a_bytes + tk * tn * b_bytes + tm * tn * out_bytes) + tm * tn * 4
                if vmem > MM_VMEM_BUDGET:
                    continue
                traffic = m * k * a_bytes * (n // tn if nk > 1 else 1) + k * n * b_bytes * (m // tm)
                traffic += tm * tk * a_bytes + tk * tn * b_bytes + tm * tn * out_bytes
                traffic += m * n * 4 * nk if nk > 1 else 0
                cost = traffic / V7X_HBM_BYTES_PER_S + (m // tm) * (n // tn) * nk * GRID_STEP_S
                if best is None or cost < best[0]:
                    best = (cost, tm, tn, tk)
    return best[1:]


def _mm(a, b, mode, out_dtypes, *, name, n=None, b_col_off=0, resid=None, ride=None):
    if mode == "nn":
        m, k = a.shape
        n = b.shape[1] if n is None else n
    elif mode == "nt":
        m, k = a.shape
        n = b.shape[0]
    else:
        k, m = a.shape
        n = b.shape[1]
    n_out = len(out_dtypes)
    has_resid = resid is not None
    out_bytes = sum(jnp.dtype(dt).itemsize for dt in out_dtypes) + (4 if has_resid else 0)
    tm, tn, tk = _mm_tiles(m, math.gcd(n, b_col_off) if b_col_off else n, n, k,
                           a.dtype.itemsize, b.dtype.itemsize, out_bytes)
    nk = k // tk
    boff = b_col_off // tn
    dot = {"nn": _dot_nn, "nt": _dot_nt, "tn": _dot_tn}[mode]

    def body(*refs):
        a_ref, b_ref = refs[0], refs[1]
        r_ref = refs[2] if has_resid else None
        o_refs = refs[2 + has_resid: 2 + has_resid + n_out]

        def finish(r):
            if has_resid:
                r = r + r_ref[...]
            for o_ref in o_refs:
                o_ref[...] = r.astype(o_ref.dtype)

        part = dot(a_ref[...].astype(BF16), b_ref[...].astype(BF16))
        if nk == 1:
            finish(part)
            return
        acc_ref = refs[-1]
        kk = pl.program_id(2)

        @pl.when(kk == 0)
        def _():
            acc_ref[...] = part

        @pl.when(kk > 0)
        def _():
            acc_ref[...] += part

        @pl.when(kk == nk - 1)
        def _():
            finish(acc_ref[...])

    if mode == "nn":
        a_spec = pl.BlockSpec((tm, tk), lambda i, j, kk: (i, kk))
        b_spec = pl.BlockSpec((tk, tn), lambda i, j, kk: (kk, j + boff))
    elif mode == "nt":
        a_spec = pl.BlockSpec((tm, tk), lambda i, j, kk: (i, kk))
        b_spec = pl.BlockSpec((tn, tk), lambda i, j, kk: (j, kk))
    else:
        a_spec = pl.BlockSpec((tk, tm), lambda i, j, kk: (kk, i))
        b_spec = pl.BlockSpec((tk, tn), lambda i, j, kk: (kk, j))
    o_spec = pl.BlockSpec((tm, tn), lambda i, j, kk: (i, j))
    in_specs = [a_spec, b_spec] + ([o_spec] if has_resid else [])
    args = [a, b] + ([resid] if has_resid else [])
    outs = _call(
        body, args, name=name,
        grid=(m // tm, n // tn, nk),
        in_specs=in_specs,
        out_specs=[o_spec] * n_out,
        out_shape=[jax.ShapeDtypeStruct((m, n), dt) for dt in out_dtypes],
        scratch_shapes=[pltpu.VMEM((tm, tn), F32)] if nk > 1 else [],
        sem=("parallel", "parallel", "arbitrary"), ride=ride)
    return outs[0] if n_out == 1 else tuple(outs)


def _row_tile(s):
    return _pick(s, (256, 128))


def _rms_fwd(x, g, *, name, ride=None):
    s, d = x.shape
    tr = _row_tile(s)

    def body(x_ref, g_ref, o_ref):
        xv = x_ref[...]
        r = lax.rsqrt(jnp.mean(xv * xv, axis=1, keepdims=True) + EPS)
        o_ref[...] = (xv * r * g_ref[...]).astype(o_ref.dtype)

    return _call(
        body, [x, g], name=name, grid=(s // tr,),
        in_specs=[pl.BlockSpec((tr, d), lambda i: (i, 0)), pl.BlockSpec((1, d), lambda i: (0, 0))],
        out_specs=[pl.BlockSpec((tr, d), lambda i: (i, 0))],
        out_shape=[jax.ShapeDtypeStruct((s, d), BF16)], sem=("parallel",), ride=ride)[0]


def _rms_bwd(x, g, dh, dres, *, name):
    s, d = x.shape
    tr = _row_tile(s)

    def body(x_ref, g_ref, dh_ref, dres_ref, dx_ref, dg_ref):
        i = pl.program_id(0)
        xv = x_ref[...]
        r = lax.rsqrt(jnp.mean(xv * xv, axis=1, keepdims=True) + EPS)
        xhat = xv * r
        dhv = dh_ref[...]
        dxhat = dhv * g_ref[...]
        proj = jnp.mean(dxhat * xhat, axis=1, keepdims=True)
        dx_ref[...] = dres_ref[...] + r * (dxhat - xhat * proj)

        @pl.when(i == 0)
        def _():
            dg_ref[...] = jnp.zeros_like(dg_ref)

        dg_ref[...] += jnp.sum(dhv * xhat, axis=0, keepdims=True)

    row = pl.BlockSpec((tr, d), lambda i: (i, 0))
    vec = pl.BlockSpec((1, d), lambda i: (0, 0))
    return pl.pallas_call(
        body, name=name, grid=(s // tr,),
        in_specs=[row, vec, row, row],
        out_specs=[row, vec],
        out_shape=[jax.ShapeDtypeStruct((s, d), F32), jax.ShapeDtypeStruct((1, d), F32)],
        compiler_params=_cparams(("arbitrary",)),
    )(x, g, dh, dres)


def _final_loss(x, g, target, *, name):
    s, d = x.shape
    tr = _row_tile(s)

    def body(x_ref, g_ref, t_ref, dx_ref, dg_ref, loss_ref):
        i = pl.program_id(0)
        xv = x_ref[...]
        gv = g_ref[...]
        r = lax.rsqrt(jnp.mean(xv * xv, axis=1, keepdims=True) + EPS)
        xhat = xv * r
        err = xhat * gv - t_ref[...]
        dy = err * (1.0 / d)
        dxhat = dy * gv
        proj = jnp.mean(dxhat * xhat, axis=1, keepdims=True)
        dx_ref[...] = r * (dxhat - xhat * proj)

        @pl.when(i == 0)
        def _():
            dg_ref[...] = jnp.zeros_like(dg_ref)
            loss_ref[...] = jnp.zeros_like(loss_ref)

        dg_ref[...] += jnp.sum(dy * xhat, axis=0, keepdims=True)
        part = 0.5 * jnp.sum(jnp.mean(err * err, axis=1, keepdims=True), axis=0, keepdims=True)
        loss_ref[...] += jnp.broadcast_to(part, loss_ref.shape)

    row = pl.BlockSpec((tr, d), lambda i: (i, 0))
    vec = pl.BlockSpec((1, d), lambda i: (0, 0))
    return pl.pallas_call(
        body, name=name, grid=(s // tr,),
        in_specs=[row, vec, row],
        out_specs=[row, vec, pl.BlockSpec((1, 128), lambda i: (0, 0))],
        out_shape=[jax.ShapeDtypeStruct((s, d), F32), jax.ShapeDtypeStruct((1, d), F32),
                   jax.ShapeDtypeStruct((1, 128), F32)],
        compiler_params=_cparams(("arbitrary",)),
    )(x, g, target)


def _ew(body, ins, in_blocks, outs, out_blocks, grid, *, name, ride=None):
    return _call(body, ins, name=name, grid=grid,
                 in_specs=[pl.BlockSpec(bs, im) for bs, im in in_blocks],
                 out_specs=[pl.BlockSpec(bs, im) for bs, im in out_blocks],
                 out_shape=outs, sem=("parallel",) * len(grid), ride=ride)


def _gate_merge_fwd(gates, o_sb, o_ca, *, name, ride=None):
    s, d = o_sb.shape
    tr, tc = _row_tile(s), _pick(d, (1024, 512, 256, 128))
    nc = d // tc

    def body(gs_ref, gc_ref, os_ref, oc_ref, m_ref):
        m = _sigmoid(gs_ref[...]) * os_ref[...] + _sigmoid(gc_ref[...]) * oc_ref[...]
        m_ref[...] = m.astype(m_ref.dtype)

    blk = ((tr, tc), lambda i, j: (i, j))
    return _ew(body, [gates, gates, o_sb, o_ca],
               [blk, ((tr, tc), lambda i, j: (i, j + nc)), blk, blk],
               [jax.ShapeDtypeStruct((s, d), BF16)], [blk], (s // tr, nc), name=name, ride=ride)[0]


def _gate_merge_bwd(dmerged, gates, o_sb, o_ca, *, name):
    s, d = o_sb.shape
    tr, tc = _row_tile(s), _pick(d, (1024, 512, 256, 128))
    nc = d // tc

    def body(dm_ref, gs_ref, gc_ref, os_ref, oc_ref, dgs_ref, dgc_ref, dos_ref, doc_ref):
        dm = dm_ref[...]
        ss = _sigmoid(gs_ref[...])
        sc = _sigmoid(gc_ref[...])
        dgs_ref[...] = (dm * os_ref[...] * ss * (1.0 - ss)).astype(dgs_ref.dtype)
        dgc_ref[...] = (dm * oc_ref[...] * sc * (1.0 - sc)).astype(dgc_ref.dtype)
        dos_ref[...] = (dm * ss).astype(dos_ref.dtype)
        doc_ref[...] = (dm * sc).astype(doc_ref.dtype)

    blk = ((tr, tc), lambda i, j: (i, j))
    sd = jax.ShapeDtypeStruct((s, d), BF16)
    return _ew(body, [dmerged, gates, gates, o_sb, o_ca],
               [blk, blk, ((tr, tc), lambda i, j: (i, j + nc)), blk, blk],
               [sd, sd, sd, sd], [blk, blk, blk, blk], (s // tr, nc), name=name)


def _swiglu_fwd(gu, *, name, ride=None):
    s, f2 = gu.shape
    f = f2 // 2
    tr, tc = _row_tile(s), _pick(f, (512, 256, 128))
    nc = f // tc

    def body(g_ref, u_ref, a_ref):
        gv = g_ref[...]
        a_ref[...] = (gv * _sigmoid(gv) * u_ref[...]).astype(a_ref.dtype)

    blk = ((tr, tc), lambda i, j: (i, j))
    return _ew(body, [gu, gu], [blk, ((tr, tc), lambda i, j: (i, j + nc))],
               [jax.ShapeDtypeStruct((s, f), BF16)], [blk], (s // tr, nc), name=name, ride=ride)[0]


def _swiglu_bwd(dact, gu, *, name):
    s, f2 = gu.shape
    f = f2 // 2
    tr, tc = 128, _pick(f, (512, 256, 128))

    def body(da_ref, gu_ref, o_ref):
        for at in range(0, f, tc):
            da = da_ref[:, at:at + tc]
            gv = gu_ref[:, at:at + tc]
            sg = _sigmoid(gv)
            o_ref[:, at:at + tc] = (da * gu_ref[:, f + at:f + at + tc] * sg * (1.0 + gv * (1.0 - sg))).astype(o_ref.dtype)
            o_ref[:, f + at:f + at + tc] = (da * gv * sg).astype(o_ref.dtype)

    row = lambda i: (i, 0)
    return _ew(body, [dact, gu], [((tr, f), row), ((tr, f2), row)], [jax.ShapeDtypeStruct((s, f2), BF16)],
               [((tr, f2), row)], (s // tr,), name=name)[0]


def _ple_fwd(x, t, pe, *, name):
    s, d = x.shape
    tr, tc = _row_tile(s), _pick(d, (1024, 512, 256, 128))

    def body(x_ref, t_ref, p_ref, o_ref):
        o_ref[...] = x_ref[...] + _sigmoid(t_ref[...]) * p_ref[...]

    blk = ((tr, tc), lambda i, j: (i, j))
    return _ew(body, [x, t, pe], [blk, blk, blk],
               [jax.ShapeDtypeStruct((s, d), F32)], [blk], (s // tr, d // tc), name=name)[0]


def _ple_bwd(dx, t, pe, *, name):
    s, d = dx.shape
    tr, tc = _row_tile(s), _pick(d, (1024, 512, 256, 128))

    def body(dx_ref, t_ref, p_ref, dt_ref, dp_ref):
        dxv = dx_ref[...]
        sg = _sigmoid(t_ref[...])
        dt_ref[...] = (dxv * p_ref[...] * sg * (1.0 - sg)).astype(dt_ref.dtype)
        dp_ref[...] = (dxv * sg).astype(dp_ref.dtype)

    blk = ((tr, tc), lambda i, j: (i, j))
    sd = jax.ShapeDtypeStruct((s, d), BF16)
    return _ew(body, [dx, t, pe], [blk, blk, blk], [sd, sd], [blk, blk], (s // tr, d // tc), name=name)


def _sb_tri(later):
    row = lax.broadcasted_iota(jnp.int32, (SB_BLOCK, SB_BLOCK), 0)
    col = lax.broadcasted_iota(jnp.int32, (SB_BLOCK, SB_BLOCK), 1)
    tri = (row > col) if later else (row < col)
    return jnp.concatenate([tri.astype(BF16), jnp.ones((SB_BLOCK, SB_BLOCK), BF16)], axis=1)


def _sb_valid(i, j):
    qi = i * SB_ROWS + lax.broadcasted_iota(jnp.int32, (SB_ROWS, SB_KEYS), 0)
    ki = j * SB_KEYS + lax.broadcasted_iota(jnp.int32, (SB_ROWS, SB_KEYS), 1)
    return ki < qi


def _sb_scan(v, tri, run, later):
    hi = v.astype(BF16)
    lo = (v - hi.astype(F32)).astype(BF16)
    outs = [None] * SB_GROUPS
    for b in (reversed(range(SB_GROUPS)) if later else range(SB_GROUPS)):
        cols = slice(b * SB_BLOCK, (b + 1) * SB_BLOCK)
        r = _dot_nn(hi[:, cols], tri) + _dot_nn(lo[:, cols], tri)
        outs[b] = r[:, :SB_BLOCK] + run
        run = run + r[:, SB_BLOCK:]
    return jnp.concatenate(outs, axis=1), run


def _sb_scores(q, kj, scale, valid):
    z = _dot_nt(q, kj) * scale
    t = jnp.log(1.0 + jnp.exp(-jnp.abs(z)))
    return jnp.minimum(z, 0.0) - t, jnp.where(valid, -jnp.maximum(z, 0.0) - t, 0.0)


def _sb_specs(h_count, s, col0):
    q_spec = pl.BlockSpec((SB_ROWS, HEAD_DIM), lambda h, i: (i, col0 + h))
    k_spec = pl.BlockSpec((s, HEAD_DIM), lambda h, i: (0, col0 + h_count + h))
    v_spec = pl.BlockSpec((s, HEAD_DIM), lambda h, i: (0, col0 + 2 * h_count + h))
    return q_spec, k_spec, v_spec


def _sb_fwd(qkv, n_heads, col0, *, name, ride=None):
    s = qkv.shape[0]
    nq = s // SB_ROWS
    scale = HEAD_DIM ** -0.5

    def body(q_ref, k_ref, v_ref, o_ref):
        i = pl.program_id(1)
        steps = ((i + 1) * SB_ROWS - 1) // SB_KEYS + 1
        q = q_ref[...]
        tri = _sb_tri(later=True)

        def step(jj, carry):
            run, acc = carry
            j = steps - 1 - jj
            off = pl.multiple_of(j * SB_KEYS, SB_KEYS)
            valid = _sb_valid(i, j)
            ls, lk = _sb_scores(q, k_ref[pl.ds(off, SB_KEYS), :], scale, valid)
            between, run = _sb_scan(lk, tri, run, later=True)
            a = jnp.where(valid, jnp.exp(ls + between), 0.0)
            return run, acc + _dot_nn(a.astype(BF16), v_ref[pl.ds(off, SB_KEYS), :])

        init = (jnp.zeros((SB_ROWS, SB_BLOCK), F32), jnp.zeros((SB_ROWS, HEAD_DIM), F32))
        _, acc = lax.fori_loop(0, steps, step, init)
        o_ref[...] = acc.astype(o_ref.dtype)

    q_spec, k_spec, v_spec = _sb_specs(n_heads, s, col0)
    return _call(
        body, [qkv, qkv, qkv], name=name, grid=(n_heads, nq),
        in_specs=[q_spec, k_spec, v_spec],
        out_specs=[pl.BlockSpec((SB_ROWS, HEAD_DIM), lambda h, i: (i, h))],
        out_shape=[jax.ShapeDtypeStruct((s, n_heads * HEAD_DIM), BF16)],
        sem=("parallel", "arbitrary"), ride=ride)[0]


def _sb_bwd(qkv, dy, n_heads, col0, *, name, ride=None):
    s = qkv.shape[0]
    nq = s // SB_ROWS
    scale = HEAD_DIM ** -0.5

    def body(q_ref, k_ref, v_ref, dy_ref, dq_ref, dk_ref, dv_ref, e_scr, dk_acc, dv_acc):
        i = pl.program_id(1)
        steps = ((i + 1) * SB_ROWS - 1) // SB_KEYS + 1
        q = q_ref[...]
        dyv = dy_ref[...]

        @pl.when(i == 0)
        def _():
            dk_acc[...] = jnp.zeros_like(dk_acc)
            dv_acc[...] = jnp.zeros_like(dv_acc)

        tri_later = _sb_tri(later=True)

        def pass1(jj, run):
            j = steps - 1 - jj
            off = pl.multiple_of(j * SB_KEYS, SB_KEYS)
            valid = _sb_valid(i, j)
            ls, lk = _sb_scores(q, k_ref[pl.ds(off, SB_KEYS), :], scale, valid)
            between, run = _sb_scan(lk, tri_later, run, later=True)
            a = jnp.where(valid, jnp.exp(ls + between), 0.0)
            e_scr[j] = a * _dot_nt(dyv, v_ref[pl.ds(off, SB_KEYS), :])
            dv_acc[pl.ds(off, SB_KEYS), :] += _dot_tn(a.astype(BF16), dyv)
            return run

        lax.fori_loop(0, steps, pass1, jnp.zeros((SB_ROWS, SB_BLOCK), F32))

        tri_earlier = _sb_tri(later=False)

        def pass2(j, carry):
            run, dq = carry
            off = pl.multiple_of(j * SB_KEYS, SB_KEYS)
            kj = k_ref[pl.ds(off, SB_KEYS), :]
            sg = _sigmoid(_dot_nt(q, kj) * scale)
            e = e_scr[j]
            before, run = _sb_scan(e, tri_earlier, run, later=False)
            dz = jnp.where(_sb_valid(i, j), e * (1.0 - sg) - sg * before, 0.0) * scale
            dzb = dz.astype(BF16)
            dk_acc[pl.ds(off, SB_KEYS), :] += _dot_tn(dzb, q)
            return run, dq + _dot_nn(dzb, kj)

        init = (jnp.zeros((SB_ROWS, SB_BLOCK), F32), jnp.zeros((SB_ROWS, HEAD_DIM), F32))
        _, dq = lax.fori_loop(0, steps, pass2, init)
        dq_ref[...] = dq.astype(dq_ref.dtype)

        @pl.when(i == nq - 1)
        def _():
            dk_ref[...] = dk_acc[...].astype(dk_ref.dtype)
            dv_ref[...] = dv_acc[...].astype(dv_ref.dtype)

    q_spec, k_spec, v_spec = _sb_specs(n_heads, s, col0)
    blk = pl.BlockSpec((SB_ROWS, HEAD_DIM), lambda h, i: (i, h))
    full = pl.BlockSpec((s, HEAD_DIM), lambda h, i: (0, h))
    sd = jax.ShapeDtypeStruct((s, n_heads * HEAD_DIM), BF16)
    return _call(
        body, [qkv, qkv, qkv, dy], name=name, grid=(n_heads, nq),
        in_specs=[q_spec, k_spec, v_spec, blk],
        out_specs=[blk, full, full],
        out_shape=[sd, sd, sd],
        scratch_shapes=[pltpu.VMEM((s // SB_KEYS, SB_ROWS, SB_KEYS), F32),
                        pltpu.VMEM((s, HEAD_DIM), F32), pltpu.VMEM((s, HEAD_DIM), F32)],
        sem=("parallel", "arbitrary"), ride=ride)


def _band_bias(rel_bias):
    h = rel_bias.shape[0]
    width = BAND + CHUNK
    first = width - 1 - N_REL
    line = jnp.concatenate([jnp.broadcast_to(rel_bias[:, :1], (h, first)), rel_bias], axis=1)
    tiled = jnp.broadcast_to(line[:, None, :], (h, CHUNK, width - 1)).reshape(h, CHUNK * (width - 1))
    skew = jnp.pad(tiled, ((0, 0), (0, CHUNK))).reshape(h, CHUNK, width)[:, ::-1, :BAND]
    seen = jnp.arange(BAND) >= CHUNK
    return jnp.where(seen[None, None, :], skew, NEG)


def _band_bias_grad(dbias):
    h = dbias.shape[0]
    width = BAND + CHUNK
    flipped = jnp.pad(dbias[:, ::-1, :], ((0, 0), (0, 0), (0, CHUNK)))
    skew = flipped.reshape(h, CHUNK * width)[:, :CHUNK * (width - 1)].reshape(h, CHUNK, width - 1)
    diag = jnp.sum(skew, axis=1)
    first = width - 1 - N_REL
    clipped = jnp.sum(diag[:, :first + 1], axis=1, keepdims=True)
    return jnp.concatenate([clipped, diag[:, first + 1:]], axis=1)


def _group_bias(band):
    return jnp.concatenate([jnp.pad(band, ((0, 0), (0, 0), ((u + 1) * CHUNK, (CA_PER_STEP - 1 - u) * CHUNK)),
                                    constant_values=NEG) for u in range(CA_PER_STEP)], axis=1)


def _group_bias_grad(dgroup):
    return sum(dgroup[:, u * CHUNK:(u + 1) * CHUNK, (u + 1) * CHUNK:(u + 1) * CHUNK + BAND] for u in range(CA_PER_STEP))


def _ca_load_padded(k_ref, v_ref, kp, vp, s):
    kp[pl.ds(0, CA_PAD), :] = jnp.zeros((CA_PAD, HEAD_DIM), kp.dtype)
    vp[pl.ds(0, CA_PAD), :] = jnp.zeros((CA_PAD, HEAD_DIM), vp.dtype)
    kp[pl.ds(CA_PAD, s), :] = k_ref[...]
    vp[pl.ds(CA_PAD, s), :] = v_ref[...]


def _ca_weights(q, kb, bias, off, scale):
    z = _dot_nt(q, kb) * scale + bias
    pos = off + lax.broadcasted_iota(jnp.int32, (CA_ROWS, CA_BAND), 1)
    z = jnp.where(pos >= CA_PAD, z, NEG)
    p = jnp.exp(z - jnp.max(z, axis=1, keepdims=True))
    return p / jnp.sum(p, axis=1, keepdims=True)


def _ca_specs(h_count, s, col0):
    q_spec = pl.BlockSpec((CA_ROWS, HEAD_DIM), lambda h, c: (c, col0 + h))
    k_spec = pl.BlockSpec((s, HEAD_DIM), lambda h, c: (0, col0 + h_count + h))
    v_spec = pl.BlockSpec((s, HEAD_DIM), lambda h, c: (0, col0 + 2 * h_count + h))
    b_spec = pl.BlockSpec((1, CA_ROWS, CA_BAND), lambda h, c: (h, 0, 0))
    return q_spec, k_spec, v_spec, b_spec


def _ca_fwd(qkv, bias, n_heads, col0, *, name, ride=None):
    s = qkv.shape[0]
    nc = s // CA_ROWS
    scale = HEAD_DIM ** -0.5

    def body(q_ref, k_ref, v_ref, b_ref, o_ref, kp, vp):
        c = pl.program_id(1)

        @pl.when(c == 0)
        def _():
            _ca_load_padded(k_ref, v_ref, kp, vp, s)

        off = pl.multiple_of(c * CA_ROWS, CA_ROWS)
        w = _ca_weights(q_ref[...], kp[pl.ds(off, CA_BAND), :], b_ref[0], off, scale)
        o_ref[...] = _dot_nn(w.astype(BF16), vp[pl.ds(off, CA_BAND), :]).astype(o_ref.dtype)

    q_spec, k_spec, v_spec, b_spec = _ca_specs(n_heads, s, col0)
    return _call(
        body, [qkv, qkv, qkv, bias], name=name, grid=(n_heads, nc),
        in_specs=[q_spec, k_spec, v_spec, b_spec],
        out_specs=[pl.BlockSpec((CA_ROWS, HEAD_DIM), lambda h, c: (c, h))],
        out_shape=[jax.ShapeDtypeStruct((s, n_heads * HEAD_DIM), BF16)],
        scratch_shapes=[pltpu.VMEM((s + CA_PAD, HEAD_DIM), BF16), pltpu.VMEM((s + CA_PAD, HEAD_DIM), BF16)],
        sem=("parallel", "arbitrary"), ride=ride)[0]


def _ca_bwd(qkv, bias, dy, n_heads, col0, *, name, ride=None):
    s = qkv.shape[0]
    nc = s // CA_ROWS
    scale = HEAD_DIM ** -0.5

    def body(q_ref, k_ref, v_ref, b_ref, dy_ref, dq_ref, dk_ref, dv_ref, db_ref, kp, vp, dkp, dvp):
        c = pl.program_id(1)

        @pl.when(c == 0)
        def _():
            _ca_load_padded(k_ref, v_ref, kp, vp, s)
            dkp[...] = jnp.zeros_like(dkp)
            dvp[...] = jnp.zeros_like(dvp)
            db_ref[...] = jnp.zeros_like(db_ref)

        off = pl.multiple_of(c * CA_ROWS, CA_ROWS)
        band = pl.ds(off, CA_BAND)
        q = q_ref[...]
        dyv = dy_ref[...]
        kb = kp[band, :]
        w = _ca_weights(q, kb, b_ref[0], off, scale)
        dw = _dot_nt(dyv, vp[band, :])
        dvp[band, :] += _dot_tn(w.astype(BF16), dyv)
        dz = w * (dw - jnp.sum(w * dw, axis=1, keepdims=True))
        db_ref[0] += dz
        dzs = (dz * scale).astype(BF16)
        dq_ref[...] = _dot_nn(dzs, kb).astype(dq_ref.dtype)
        dkp[band, :] += _dot_tn(dzs, q)

        @pl.when(c == nc - 1)
        def _():
            dk_ref[...] = dkp[pl.ds(CA_PAD, s), :].astype(dk_ref.dtype)
            dv_ref[...] = dvp[pl.ds(CA_PAD, s), :].astype(dv_ref.dtype)

    q_spec, k_spec, v_spec, b_spec = _ca_specs(n_heads, s, col0)
    blk = pl.BlockSpec((CA_ROWS, HEAD_DIM), lambda h, c: (c, h))
    full = pl.BlockSpec((s, HEAD_DIM), lambda h, c: (0, h))
    sd = jax.ShapeDtypeStruct((s, n_heads * HEAD_DIM), BF16)
    return _call(
        body, [qkv, qkv, qkv, bias, dy], name=name, grid=(n_heads, nc),
        in_specs=[q_spec, k_spec, v_spec, b_spec, blk],
        out_specs=[blk, full, full, b_spec],
        out_shape=[sd, sd, sd, jax.ShapeDtypeStruct((n_heads, CA_ROWS, CA_BAND), F32)],
        scratch_shapes=[pltpu.VMEM((s + CA_PAD, HEAD_DIM), BF16), pltpu.VMEM((s + CA_PAD, HEAD_DIM), BF16),
                        pltpu.VMEM((s + CA_PAD, HEAD_DIM), F32), pltpu.VMEM((s + CA_PAD, HEAD_DIM), F32)],
        sem=("parallel", "arbitrary"), ride=ride)


EARLY = ("w_sb_out", "w_ca_out", "w_mix_out")


def _step(x, p, target, small, comm):
    w = comm.w
    d = x.shape[1]
    n_sb = w["w_sb_out"].shape[0] // HEAD_DIM
    n_ca = w["w_ca_out"].shape[0] // HEAD_DIM
    qkv_cols = 3 * HEAD_DIM * (n_sb + n_ca)
    ca_col0 = 3 * n_sb
    both = (F32, BF16)

    h1 = _rms_fwd(x, small["g_mix"], name="rms_mix")
    qkv = _mm(h1, w["w_in"], "nn", (BF16,), name="proj_qkv", n=qkv_cols,
              ride=comm.gather(("w_sb_out", "w_ca_out"), False))
    gates = _mm(h1, w["w_in"], "nn", (F32,), name="proj_gates", n=2 * d, b_col_off=qkv_cols,
                ride=comm.gather(("w_mix_out",), False))
    bias = _group_bias(_band_bias(small["rel_bias"]))
    y_sb = _sb_fwd(qkv, n_sb, 0, name="sb_fwd", ride=comm.gather(("w_ffn_in",), False, comm.gather(EARLY, True), (0, 2)))
    y_ca = _ca_fwd(qkv, bias, n_ca, ca_col0, name="ca_fwd", ride=comm.gather(("w_ffn_in",), False, part=(2, 4)))
    o_sb = _mm(y_sb, w["w_sb_out"], "nn", (F32,), name="sb_out")
    o_ca = _mm(y_ca, w["w_ca_out"], "nn", (F32,), name="ca_out")
    merged = _gate_merge_fwd(gates, o_sb, o_ca, name="gate_merge", ride=comm.gather(("w_ffn_in",), False, part=(6, 8)))
    x1 = _mm(merged, w["w_mix_out"], "nn", (F32,), name="mix_out", resid=x,
             ride=comm.gather(("w_ffn_in",), False, part=(7, 8)))
    h2 = _rms_fwd(x1, small["g_ffn"], name="rms_ffn", ride=comm.gather(("w_ffn_in",), True))
    gu = _mm(h2, w["w_ffn_in"], "nn", (F32,), name="ffn_in", ride=comm.gather(("w_ffn_out",), False))
    act = _swiglu_fwd(gu, name="swiglu", ride=comm.gather(("w_ple_gate", "w_ple_in"), False, comm.gather(("w_ffn_out",), True)))
    x2 = _mm(act, w["w_ffn_out"], "nn", (F32,), name="ffn_out", resid=x1,
             ride=comm.gather(("w_ple_gate", "w_ple_in"), True))
    h3 = _rms_fwd(x2, small["g_ple"], name="rms_ple")
    t = _mm(h3, w["w_ple_gate"], "nn", (F32,), name="ple_gate")
    pe = _mm(p, w["w_ple_in"], "nn", (F32,), name="ple_in")
    x3 = _ple_fwd(x2, t, pe, name="ple_add")

    gs = {}
    dx3, gs["g_final"], loss = _final_loss(x3, small["g_final"], target, name="final_loss")
    dt, dpe = _ple_bwd(dx3, t, pe, name="ple_bwd")
    comm.grad("w_ple_in", *_mm(p, dpe, "tn", both, name="dw_ple_in"))
    comm.grad("w_ple_gate", *_mm(h3, dt, "tn", both, name="dw_ple_gate"))
    ple = ("w_ple_in", "w_ple_gate")
    dh3 = _mm(dt, w["w_ple_gate"], "nt", (F32,), name="dh_ple", ride=comm.pair(ple))
    dx2, gs["g_ple"] = _rms_bwd(x2, small["g_ple"], dh3, dx3, name="rms_ple_bwd")
    comm.add(ple)
    comm.grad("w_ffn_out", *_mm(act, dx2, "tn", both, name="dw_ffn_out", ride=comm.chips(ple)))
    dact = _mm(dx2, w["w_ffn_out"], "nt", (F32,), name="dact", ride=comm.pair(("w_ffn_out",)))
    dgu = _swiglu_bwd(dact, gu, name="swiglu_bwd")
    comm.sum(ple)
    comm.add(("w_ffn_out",))
    comm.grad("w_ffn_in", *_mm(h2, dgu, "tn", both, name="dw_ffn_in",
                               ride=comm.share(ple, comm.chips(("w_ffn_out",)))))
    dh2 = _mm(dgu, w["w_ffn_in"], "nt", (F32,), name="dh_ffn", ride=comm.pair(("w_ffn_in",)))
    dx1, gs["g_ffn"] = _rms_bwd(x1, small["g_ffn"], dh2, dx2, name="rms_ffn_bwd")
    comm.add(("w_ffn_in",))
    comm.sum(("w_ffn_out",))
    comm.grad("w_mix_out", *_mm(merged, dx1, "tn", both, name="dw_mix_out", ride=comm.share(("w_ffn_out",))))
    dmerged = _mm(dx1, w["w_mix_out"], "nt", (F32,), name="dmerged", ride=comm.pair(("w_mix_out",)))
    dg_sb, dg_ca, do_sb, do_ca = _gate_merge_bwd(dmerged, gates, o_sb, o_ca, name="gate_merge_bwd")
    comm.add(("w_mix_out",))
    comm.grad("w_sb_out", *_mm(y_sb, do_sb, "tn", both, name="dw_sb_out"))
    comm.grad("w_ca_out", *_mm(y_ca, do_ca, "tn", both, name="dw_ca_out"))
    outs = ("w_sb_out", "w_ca_out")
    dy_sb = _mm(do_sb, w["w_sb_out"], "nt", (BF16,), name="dy_sb", ride=comm.pair(outs))
    dy_ca = _mm(do_ca, w["w_ca_out"], "nt", (BF16,), name="dy_ca")
    comm.add(outs)
    dq_sb, dk_sb, dv_sb = _sb_bwd(qkv, dy_sb, n_sb, 0, name="sb_bwd", ride=comm.chips(("w_ffn_in",)))
    comm.sum(("w_ffn_in",))
    late = ("w_mix_out",) + outs
    dq_ca, dk_ca, dv_ca, dbias = _ca_bwd(qkv, bias, dy_ca, n_ca, ca_col0, name="ca_bwd",
                                         ride=comm.chips(late, comm.share(("w_ffn_in",))))
    comm.sum(late)
    gs["rel_bias"] = _band_bias_grad(_group_bias_grad(dbias))
    dproj = jnp.concatenate([dq_sb, dk_sb, dv_sb, dq_ca, dk_ca, dv_ca, dg_sb, dg_ca], axis=1)
    comm.grad("w_in", *_mm(h1, dproj, "tn", both, name="dw_in", ride=comm.share(late)))
    dh1 = _mm(dproj, w["w_in"], "nt", (F32,), name="dh_mix", ride=comm.pair(("w_in",)))
    grad_x, gs["g_mix"] = _rms_bwd(x, small["g_mix"], dh1, dx1, name="rms_mix_bwd")
    comm.add(("w_in",))
    return loss, grad_x, gs


def _position():
    x, y, c = lax.axis_index("x"), lax.axis_index("y"), lax.axis_index("c")
    chips = [(1 - x, y), (x, 1 - y), (1 - x, 1 - y)]
    return x, y, c, chips


def _aligned(v, m):
    return v if isinstance(v, int) else pl.multiple_of(v, m)


def _piece_dims(shape, axis):
    k, n = shape
    return (k // 2, n // N_CHIPS) if axis == 1 else (k // N_CHIPS // 2, n)


def _piece(ref, shape, axis, j, h, part=(0, 1)):
    pr, pc = _piece_dims(shape, axis)
    nr = pr // part[1]
    r0 = part[0] * nr
    if axis == 1:
        return ref.at[pl.ds(_aligned(h * pr + r0, 16), nr), pl.ds(_aligned(j * pc, 128), pc)]
    return ref.at[pl.ds(_aligned((2 * j + h) * pr + r0, 16), nr), :]


def _shard_half(ref, h):
    rows = ref.shape[0] // 2
    return ref.at[pl.ds(_aligned(h * rows, 16), rows), :]


def _remote(src, dst, send_sems, recv_sems, k, to):
    return pltpu.make_async_remote_copy(src_ref=src, dst_ref=dst, send_sem=send_sems.at[k],
                                        recv_sem=recv_sems.at[k], device_id=to, device_id_type=MESH)


def _prefetch_call(body, scalars, ins, in_specs, out_shape, out_specs, grid, *, name):
    spec = pltpu.PrefetchScalarGridSpec(num_scalar_prefetch=1, grid=grid, in_specs=in_specs, out_specs=out_specs)
    return pl.pallas_call(body, name=name, grid_spec=spec, out_shape=out_shape,
                          compiler_params=_cparams(("parallel",) * len(grid)))(scalars, *ins)


def _slab_tiles(pr, pc):
    tc = pc if pc <= 4096 else _pick(pc, (2048, 1024, 512, 256, 128))
    tr = next(t for t in (1024, 512, 256, 128, 64, 32, 16) if pr % t == 0 and t * tc <= 512 * 1024)
    return tr, tc


def _cast_place(w, axis, pos, *, name):
    ks, ns = w.shape
    shape = (ks, ns * N_CHIPS) if axis == 1 else (ks * N_CHIPS, ns)
    tr, tc = _slab_tiles(ks, ns)
    nr, nc = ks // tr, ns // tc

    def body(pos_ref, w_ref, o_ref):
        o_ref[...] = w_ref[...].astype(o_ref.dtype)

    if axis == 1:
        out_map = lambda i, j, pos_ref: (i, pos_ref[0] * nc + j)
    else:
        out_map = lambda i, j, pos_ref: (pos_ref[0] * nr + i, j)
    return _prefetch_call(body, pos, [w], [pl.BlockSpec((tr, tc), lambda i, j, pos_ref: (i, j))],
                          jax.ShapeDtypeStruct(shape, BF16), pl.BlockSpec((tr, tc), out_map), (nr, nc), name=name)


def _run(ride, *, name):
    if ride is None:
        return

    def body(o_ref):
        o_ref[...] = jnp.zeros_like(o_ref)

    _call(body, [], name=name, grid=(1,), in_specs=[], out_specs=[pl.BlockSpec((8, 128), lambda i: (0, 0))],
          out_shape=[jax.ShapeDtypeStruct((8, 128), F32)], ride=ride)


def _ride_gather(ride, w, n, axis, to_sibling, part=(0, 1)):
    shape = w[n].shape
    piece = functools.partial(_piece, shape=shape, axis=axis, part=part)

    def copies(ins, outs, send_sems, recv_sems, arriving):
        x, y, c, chips = _position()
        me = 2 * x + y
        out = []
        for k, (px, py) in enumerate(chips):
            to = (x, y, 1 - c) if to_sibling else (px, py, c)
            if arriving:
                lands = piece(outs[0], j=2 * px + py, h=1 - c if to_sibling else c)
                out.append(_remote(lands, lands, send_sems, recv_sems, k, to))
            else:
                chip = 2 * px + py if to_sibling else me
                out.append(_remote(piece(ins[0], j=chip, h=c), piece(outs[0], j=chip, h=c), send_sems, recv_sems, k, to))
        return out

    def start(*refs):
        for cp in copies(*refs, arriving=False):
            cp.start()

    def finish(*refs):
        for cp in copies(*refs, arriving=True):
            cp.wait_recv()
        for cp in copies(*refs, arriving=False):
            cp.wait_send()

    ride.add([w[n]], [jax.ShapeDtypeStruct(shape, w[n].dtype)], {0: 0}, 3, start, finish,
             lambda outs: w.__setitem__(n, outs[0]))


def _ride_pair(ride, st, axis):
    shape = st["g16"].shape
    pr, pc = _piece_dims(shape, axis)

    def copies(ins, outs, send_sems, recv_sems):
        x, y, c, _ = _position()
        return [_remote(_piece(ins[0], shape, axis, j, 1 - c), outs[0].at[j], send_sems, recv_sems, j, (x, y, 1 - c))
                for j in range(N_CHIPS)]

    def start(*refs):
        for cp in copies(*refs):
            cp.start()

    def finish(*refs):
        for cp in copies(*refs):
            cp.wait()

    ride.add([st["g16"]], [jax.ShapeDtypeStruct((N_CHIPS, pr, pc), BF16)], {}, N_CHIPS, start, finish,
             lambda outs: st.__setitem__("sib", outs[0]))


def _ride_chips(ride, st, rows=None):
    _, pr, pc = st["s16"].shape
    r0, nr = (0, pr) if rows is None else rows

    def copies(ins, outs, send_sems, recv_sems):
        x, y, c, chips = _position()
        return [_remote(ins[0].at[2 * px + py, pl.ds(r0, nr), :], outs[0].at[k, pl.ds(r0, nr), :],
                        send_sems, recv_sems, k, (px, py, c)) for k, (px, py) in enumerate(chips)]

    def start(*refs):
        for cp in copies(*refs):
            cp.start()

    def finish(*refs):
        for cp in copies(*refs):
            cp.wait()

    ins, aliases = ([st["s16"], st["recv"]], {1: 0}) if "recv" in st else ([st["s16"]], {})
    ride.add(ins, [jax.ShapeDtypeStruct((3, pr, pc), BF16)], aliases, 3, start, finish,
             lambda outs: st.__setitem__("recv", outs[0]))


def _ride_share(ride, st):
    def sent(ins, outs, send_sems, recv_sems):
        x, y, c, _ = _position()
        return _remote(_shard_half(ins[0], c), _shard_half(outs[0], c), send_sems, recv_sems, 0, (x, y, 1 - c))

    def landed(ins, outs, send_sems, recv_sems):
        x, y, c, _ = _position()
        other = _shard_half(outs[0], 1 - c)
        return _remote(other, other, send_sems, recv_sems, 0, (x, y, 1 - c))

    def start(*refs):
        sent(*refs).start()

    def finish(*refs):
        landed(*refs).wait_recv()
        sent(*refs).wait_send()

    ride.add([st["shard"]], [jax.ShapeDtypeStruct(st["shard"].shape, F32)], {0: 0}, 1, start, finish,
             lambda outs: st.__setitem__("g", outs[0]))


def _pair_add(g32, sib, axis, pos, *, name):
    _, pr, pc = sib.shape
    tr, tc = _slab_tiles(pr, pc)
    nr, nc = pr // tr, pc // tc

    def body(pos_ref, g_ref, b_ref, o32_ref, o16_ref):
        r = g_ref[...] + b_ref[0].astype(F32)
        o32_ref[0] = r
        o16_ref[0] = r.astype(o16_ref.dtype)

    if axis == 1:
        g_map = lambda j, i, k, pos_ref: (pos_ref[1] * nr + i, j * nc + k)
    else:
        g_map = lambda j, i, k, pos_ref: ((2 * j + pos_ref[1]) * nr + i, k)
    blk = pl.BlockSpec((1, tr, tc), lambda j, i, k, pos_ref: (j, i, k))
    return _prefetch_call(body, pos, [g32, sib], [pl.BlockSpec((tr, tc), g_map), blk],
                          [jax.ShapeDtypeStruct(sib.shape, F32), jax.ShapeDtypeStruct(sib.shape, BF16)],
                          [blk, blk], (N_CHIPS, nr, nc), name=name)


def _chip_sum(s32, recv, pos, *, name):
    _, pr, pc = s32.shape
    tr, tc = _slab_tiles(pr, pc)
    nr, nc = pr // tr, pc // tc

    def body(pos_ref, m_ref, r_ref, o_ref):
        o_ref[...] = ((m_ref[0] + r_ref[0].astype(F32)) + r_ref[1].astype(F32)) + r_ref[2].astype(F32)

    return _prefetch_call(
        body, pos, [s32, recv],
        [pl.BlockSpec((1, tr, tc), lambda i, k, pos_ref: (pos_ref[0], i, k)),
         pl.BlockSpec((3, tr, tc), lambda i, k, pos_ref: (0, i, k))],
        jax.ShapeDtypeStruct((2 * pr, pc), F32),
        pl.BlockSpec((tr, tc), lambda i, k, pos_ref: (pos_ref[1] * nr + i, k)), (nr, nc), name=name)


class _Comm:
    def __init__(self, pos, w):
        self.pos, self.w, self.st = pos, w, {n: {} for n, _ in BIG}

    def gather(self, names, to_sibling, ride=None, part=(0, 1)):
        ride = _Ride() if ride is None else ride
        for n in names:
            _ride_gather(ride, self.w, n, AXIS[n], to_sibling, part)
        return ride

    def grad(self, n, g32, g16):
        self.st[n].update(g32=g32, g16=g16)

    def pair(self, names, ride=None):
        ride = _Ride() if ride is None else ride
        for n in names:
            _ride_pair(ride, self.st[n], AXIS[n])
        return ride

    def add(self, names):
        for n in names:
            st = self.st[n]
            st["s32"], st["s16"] = _pair_add(st["g32"], st["sib"], AXIS[n], self.pos, name="rs_add_" + n)

    def chips(self, names, ride=None, rows=None):
        ride = _Ride() if ride is None else ride
        for n in names:
            _ride_chips(ride, self.st[n], rows)
        return ride

    def sum(self, names):
        for n in names:
            st = self.st[n]
            st["shard"] = _chip_sum(st["s32"], st["recv"], self.pos, name="rs_sum_" + n)

    def share(self, names, ride=None):
        ride = _Ride() if ride is None else ride
        for n in names:
            _ride_share(ride, self.st[n])
        return ride

    def result(self, n):
        return self.st[n]["g"]


class _NoComm:
    def __init__(self, w):
        self.w, self.st = w, {}

    def grad(self, n, g32, g16):
        self.st[n] = (g32, g16)

    def result(self, n):
        return self.st[n]

    def add(self, names):
        pass

    sum = add

    def gather(self, names, *args, **kwargs):
        return None

    pair = chips = share = gather


def _small_all_reduce(vec, *, name):
    r = vec.shape[0]

    def body(vec_ref, out_ref, slots, send_sems, recv_sems):
        x, y, c, _ = _position()
        me = 4 * x + 2 * y + c
        slots[me] = vec_ref[...]
        sends = []
        for k in range(1, 8):
            to = (x ^ (k >> 2), y ^ ((k >> 1) & 1), c ^ (k & 1))
            cp = _remote(slots.at[me], slots.at[me], send_sems, recv_sems, k - 1, to)
            cp.start()
            sends.append(cp)
        for k in range(1, 8):
            frm = 4 * (x ^ (k >> 2)) + 2 * (y ^ ((k >> 1) & 1)) + (c ^ (k & 1))
            _remote(slots.at[frm], slots.at[frm], send_sems, recv_sems, k - 1, (x, y, c)).wait_recv()
        for cp in sends:
            cp.wait_send()
        total = slots[0]
        for d in range(1, 8):
            total = total + slots[d]
        out_ref[...] = total

    return pl.pallas_call(
        body, name=name,
        in_specs=[pl.BlockSpec(memory_space=pltpu.VMEM)], out_specs=pl.BlockSpec(memory_space=pltpu.VMEM),
        out_shape=jax.ShapeDtypeStruct((r, 128), F32),
        scratch_shapes=[pltpu.VMEM((8, r, 128), F32), pltpu.SemaphoreType.DMA((7,)), pltpu.SemaphoreType.DMA((7,))],
    )(vec)


def _adamw(w, g, m, v, *, name, ride=None):
    r, c = w.shape
    tc = c if c <= 4096 else _pick(c, (2048, 1024, 512, 256, 128))
    tr = next(t for t in (512, 256, 128, 64, 32, 16, 8) if r % t == 0 and t * tc <= 256 * 1024)

    def body(w_ref, g_ref, m_ref, v_ref, d_ref, nm_ref, nv_ref):
        gv = g_ref[...]
        nm = ADAM_B1 * m_ref[...] + (1.0 - ADAM_B1) * gv
        nv = ADAM_B2 * v_ref[...] + (1.0 - ADAM_B2) * (gv * gv)
        m_hat = nm / (1.0 - ADAM_B1 ** ADAM_STEP)
        v_hat = nv / (1.0 - ADAM_B2 ** ADAM_STEP)
        d_ref[...] = -ADAM_LR * (m_hat / (jnp.sqrt(v_hat) + ADAM_EPS) + ADAM_WD * w_ref[...])
        nm_ref[...] = nm
        nv_ref[...] = nv

    blk = ((tr, tc), lambda i, j: (i, j))
    sd = jax.ShapeDtypeStruct((r, c), F32)
    return _ew(body, [w, g, m, v], [blk] * 4, [sd, sd, sd], [blk] * 3, (r // tr, c // tc), name=name, ride=ride)


BIG = (("w_in", 1), ("w_sb_out", 1), ("w_ca_out", 1), ("w_mix_out", 0), ("w_ffn_in", 1), ("w_ffn_out", 0),
       ("w_ple_in", 1), ("w_ple_gate", 0))
AXIS = dict(BIG)
TAIL_PARTS = 16
TAIL_HOSTS = {"w_ffn_in": 4, "w_ffn_out": 2, "w_mix_out": 1, "w_ple_gate": 1}
SMALL = ("rel_bias", "g_mix", "g_ffn", "g_ple", "g_final")
ORDER = ("w_in", "w_sb_out", "w_ca_out", "w_mix_out", "rel_bias", "g_mix", "g_ffn", "g_ple", "g_final",
         "w_ffn_in", "w_ffn_out", "w_ple_in", "w_ple_gate")


def _pack(parts):
    flat = jnp.concatenate([a.reshape(-1) for a in parts])
    rows = -(-flat.shape[0] // 1024) * 8
    return jnp.pad(flat, (0, rows * 128 - flat.shape[0])).reshape(rows, 128)


def _unpack(packed, like):
    flat, out, at = packed.reshape(-1), [], 0
    for a in like:
        out.append(flat[at:at + a.size].reshape(a.shape))
        at += a.size
    return out


def kernel(x, p, w_in, w_sb_out, w_ca_out, w_mix_out, rel_bias, g_mix, g_ffn, g_ple, g_final, w_ffn_in, w_ffn_out, w_ple_in, w_ple_gate, loss_target, m_w_in, m_w_sb_out, m_w_ca_out, m_w_mix_out, m_rel_bias, m_g_mix, m_g_ffn, m_g_ple, m_g_final, m_w_ffn_in, m_w_ffn_out, m_w_ple_in, m_w_ple_gate, v_w_in, v_w_sb_out, v_w_ca_out, v_w_mix_out, v_rel_bias, v_g_mix, v_g_ffn, v_g_ple, v_g_final, v_w_ffn_in, v_w_ffn_out, v_w_ple_in, v_w_ple_gate):
    weights = dict(w_in=w_in, w_sb_out=w_sb_out, w_ca_out=w_ca_out, w_mix_out=w_mix_out, rel_bias=rel_bias,
                   g_mix=g_mix, g_ffn=g_ffn, g_ple=g_ple, g_final=g_final, w_ffn_in=w_ffn_in,
                   w_ffn_out=w_ffn_out, w_ple_in=w_ple_in, w_ple_gate=w_ple_gate)
    m_in = dict(w_in=m_w_in, w_sb_out=m_w_sb_out, w_ca_out=m_w_ca_out, w_mix_out=m_w_mix_out, rel_bias=m_rel_bias,
                g_mix=m_g_mix, g_ffn=m_g_ffn, g_ple=m_g_ple, g_final=m_g_final, w_ffn_in=m_w_ffn_in,
                w_ffn_out=m_w_ffn_out, w_ple_in=m_w_ple_in, w_ple_gate=m_w_ple_gate)
    v_in = dict(w_in=v_w_in, w_sb_out=v_w_sb_out, w_ca_out=v_w_ca_out, w_mix_out=v_w_mix_out, rel_bias=v_rel_bias,
                g_mix=v_g_mix, g_ffn=v_g_ffn, g_ple=v_g_ple, g_final=v_g_final, w_ffn_in=v_w_ffn_in,
                w_ffn_out=v_w_ffn_out, w_ple_in=v_w_ple_in, w_ple_gate=v_w_ple_gate)

    pos = jnp.stack([2 * lax.axis_index("x") + lax.axis_index("y"), lax.axis_index("c")]).astype(jnp.int32)
    placed = {n: _cast_place(weights[n][0], axis, pos, name="cast_" + n) for n, axis in BIG}
    comm = _Comm(pos, placed)
    _run(comm.gather(("w_in",), False), name="gather_w_in_chips")
    _run(comm.gather(("w_in",), True), name="gather_w_in_pair")
    small = dict(rel_bias=rel_bias[0], g_mix=g_mix, g_ffn=g_ffn, g_ple=g_ple, g_final=g_final.reshape(1, -1))
    loss, grad_x, gs = _step(x[0], p[0, 0], loss_target[0], small, comm)

    grads, delta, new_m, new_v = {}, {}, {}, {}
    rows = comm.st["w_in"]["s16"].shape[1]
    at = 0
    for n in [n for n, _ in BIG if n != "w_in"] + ["w_in"]:
        ride = None
        if n in TAIL_HOSTS:
            count = rows * TAIL_HOSTS[n] // TAIL_PARTS
            ride = comm.chips(("w_in",), rows=(at, count))
            at += count
        if n == "w_in":
            _run(comm.chips(("w_in",), rows=(at, rows - at)), name="rs_chips_w_in")
            comm.sum(("w_in",))
            _run(comm.share(("w_in",)), name="rs_share_w_in")
        g = comm.result(n)
        d, nm, nv = _adamw(weights[n][0], g, m_in[n][0], v_in[n][0], name="adamw_" + n, ride=ride)
        grads[n], delta[n], new_m[n], new_v[n] = g[None], d[None], nm[None], nv[None]

    like = [weights[n] for n in SMALL]
    reduced = _small_all_reduce(_pack([gs[n] for n in SMALL] + [loss[:, :1]]), name="small_all_reduce")
    g_small = _unpack(reduced, like + [loss[:, :1]])
    total_loss = g_small[-1].reshape(())
    g_packed = _pack(g_small[:-1])
    d_s, m_s, v_s = _adamw(_pack(like), g_packed, _pack([m_in[n] for n in SMALL]), _pack([v_in[n] for n in SMALL]),
                           name="adamw_small")
    for n, g, d, nm, nv in zip(SMALL, g_small[:-1], _unpack(d_s, like), _unpack(m_s, like), _unpack(v_s, like)):
        grads[n], delta[n], new_m[n], new_v[n] = g, d, nm, nv

    return (total_loss, grad_x[None], *[grads[n] for n in ORDER], *[delta[n] for n in ORDER],
            *[new_m[n] for n in ORDER], *[new_v[n] for n in ORDER])
```

```python
import functools
import math

import jax
import jax.numpy as jnp
import numpy as np
from jax import lax
from jax.experimental import pallas as pl
from jax.experimental.pallas import tpu as pltpu

F32 = jnp.float32
BF16 = jnp.bfloat16

HEAD_DIM = 128
CHUNK = 64
LEFT_CHUNKS = 8
REL_CLIP = 128
N_REL = REL_CLIP + CHUNK
BAND = (LEFT_CHUNKS + 2) * CHUNK
CA_PER_STEP = 4
CA_ROWS = CA_PER_STEP * CHUNK
CA_BAND = BAND + CA_PER_STEP * CHUNK
CA_PAD = BAND
SB_BLOCK = 128
SB_KEYS = 512
SB_GROUPS = SB_KEYS // SB_BLOCK
SB_ROWS = 512
EPS = 1e-6
NEG = -1e30

ADAM_LR = 0.001
ADAM_B1 = 0.9
ADAM_B2 = 0.999
ADAM_EPS = 1e-08
ADAM_WD = 0.01
ADAM_STEP = 10

VMEM_LIMIT = 48 * 1024 * 1024
MM_VMEM_BUDGET = 36 * 1024 * 1024
V7X_HBM_BYTES_PER_S = 3.7e12
GRID_STEP_S = 0.35e-6
MESH = pl.DeviceIdType.MESH
N_CHIPS = 4


def _pick(dim, prefs):
    for t in prefs:
        if dim % t == 0:
            return t
    raise ValueError(f"no tile for {dim}")


def _cparams(sem=None):
    return pltpu.CompilerParams(dimension_semantics=sem, vmem_limit_bytes=VMEM_LIMIT)


def _sigmoid(v):
    return 1.0 / (1.0 + jnp.exp(-v))


def _dot(a, b, dims):
    return lax.dot_general(a, b, (dims, ((), ())), preferred_element_type=F32)


def _dot_nn(a, b):
    return _dot(a, b, ((1,), (0,)))


def _dot_nt(a, b):
    return _dot(a, b, ((1,), (1,)))


def _dot_tn(a, b):
    return _dot(a, b, ((0,), (0,)))


HBM = pl.BlockSpec(memory_space=pltpu.HBM)


class _Ride:
    def __init__(self):
        self.items = []

    def add(self, ins, outs, aliases, n_sems, start, finish, sink):
        self.items.append((ins, outs, aliases, n_sems, start, finish, sink))


def _call(body, args, *, name, grid, in_specs, out_specs, out_shape, scratch_shapes=(), sem=None, ride=None,
          scalars=None):
    items = ride.items if ride is not None else []
    n_in, n_out, n_scr = len(args), len(out_shape), len(scratch_shapes)
    r_ins = [a for it in items for a in it[0]]
    r_outs = [o for it in items for o in it[1]]
    updated = [id(it[0][i]) for it in items for i in it[2]]
    assert len(set(updated)) == len(updated), "one call may update a buffer in place only once"
    aliases, a, b = {}, n_in, n_out
    for it in items:
        aliases.update({a + i: b + o for i, o in it[2].items()})
        a, b = a + len(it[0]), b + len(it[1])
    sems = [pltpu.SemaphoreType.DMA((it[3],)) for it in items for _ in range(2)]

    def wrapped(*refs):
        head, refs = (refs[:1], refs[1:]) if scalars is not None else ((), refs)
        ins, rin = refs[:n_in], refs[n_in:n_in + len(r_ins)]
        at = n_in + len(r_ins)
        outs, rout = refs[at:at + n_out], refs[at + n_out:at + n_out + len(r_outs)]
        at += n_out + len(r_outs)
        scr, rsem = refs[at:at + n_scr], refs[at + n_scr:]

        def each(which):
            a = b = 0
            for q, it in enumerate(items):
                it[which](rin[a:a + len(it[0])], rout[b:b + len(it[1])], rsem[2 * q], rsem[2 * q + 1])
                a, b = a + len(it[0]), b + len(it[1])

        if items:
            ids = [pl.program_id(d) for d in range(len(grid))]
            first = functools.reduce(jnp.logical_and, [i == 0 for i in ids])
            last = functools.reduce(jnp.logical_and, [i == g - 1 for i, g in zip(ids, grid)])
            pl.when(first)(lambda: each(4))
        body(*head, *ins, *outs, *scr)
        if items:
            pl.when(last)(lambda: each(5))

    specs = dict(grid=grid, in_specs=list(in_specs) + [HBM] * len(r_ins),
                 out_specs=list(out_specs) + [HBM] * len(r_outs), scratch_shapes=list(scratch_shapes) + sems)
    if scalars is not None:
        specs = dict(grid_spec=pltpu.PrefetchScalarGridSpec(num_scalar_prefetch=1, **specs))
        aliases = {i + 1: o for i, o in aliases.items()}
    res = pl.pallas_call(
        wrapped, name=name, **specs,
        out_shape=list(out_shape) + r_outs,
        input_output_aliases=aliases,
        compiler_params=_cparams(("arbitrary",) * len(grid) if items else sem),
    )(*(() if scalars is None else (scalars,)), *args, *r_ins)
    b = n_out
    for it in items:
        it[6](res[b:b + len(it[1])])
        b += len(it[1])
    return list(res[:n_out])


def _mm_tiles(m, n_align, n, k, a_bytes, b_bytes, out_bytes):
    best = None
    tks = sorted({t for t in (k, k // 2, k // 4, 2048, 1024, 512, 256, 128) if t <= k and k % t == 0 and t % 128 == 0})
    for tm in (t for t in (2048, 1024, 512, 256, 128) if m % t == 0):
        for tn in (t for t in (2048, 1024, 512, 256, 128) if n_align % t == 0):
            for tk in tks:
                nk = k // tk
                vmem = 2 * (tm * tk * a_bytes + tk * tn * b_bytes + tm * tn * out_bytes) + tm * tn * 4
                if vmem > MM_VMEM_BUDGET:
                    continue
                traffic = m * k * a_bytes * (n // tn if nk > 1 else 1) + k * n * b_bytes * (m // tm)
                traffic += tm * tk * a_bytes + tk * tn * b_bytes + tm * tn * out_bytes
                traffic += m * n * 4 * nk if nk > 1 else 0
                cost = traffic / V7X_HBM_BYTES_PER_S + (m // tm) * (n // tn) * nk * GRID_STEP_S
                if best is None or cost < best[0]:
                    best = (cost, tm, tn, tk)
    return best[1:]


def _mm(a, b, mode, out_dtypes, *, name, n=None, b_col_off=0, resid=None, ride=None):
    if mode == "nn":
        m, k = a.shape
        n = b.shape[1] if n is None else n
    elif mode == "nt":
        m, k = a.shape
        n = b.shape[0]
    else:
        k, m = a.shape
        n = b.shape[1]
    n_out = len(out_dtypes)
    has_resid = resid is not None
    out_bytes = sum(jnp.dtype(dt).itemsize for dt in out_dtypes) + (4 if has_resid else 0)
    tm, tn, tk = _mm_tiles(m, math.gcd(n, b_col_off) if b_col_off else n, n, k,
                           a.dtype.itemsize, b.dtype.itemsize, out_bytes)
    nk = k // tk
    boff = b_col_off // tn
    dot = {"nn": _dot_nn, "nt": _dot_nt, "tn": _dot_tn}[mode]

    def body(*refs):
        a_ref, b_ref = refs[0], refs[1]
        r_ref = refs[2] if has_resid else None
        o_refs = refs[2 + has_resid: 2 + has_resid + n_out]

        def finish(r):
            if has_resid:
                r = r + r_ref[...]
            for o_ref in o_refs:
                o_ref[...] = r.astype(o_ref.dtype)

        part = dot(a_ref[...].astype(BF16), b_ref[...].astype(BF16))
        if nk == 1:
            finish(part)
            return
        acc_ref = refs[-1]
        kk = pl.program_id(2)

        @pl.when(kk == 0)
        def _():
            acc_ref[...] = part

        @pl.when(kk > 0)
        def _():
            acc_ref[...] += part

        @pl.when(kk == nk - 1)
        def _():
            finish(acc_ref[...])

    if mode == "nn":
        a_spec = pl.BlockSpec((tm, tk), lambda i, j, kk: (i, kk))
        b_spec = pl.BlockSpec((tk, tn), lambda i, j, kk: (kk, j + boff))
    elif mode == "nt":
        a_spec = pl.BlockSpec((tm, tk), lambda i, j, kk: (i, kk))
        b_spec = pl.BlockSpec((tn, tk), lambda i, j, kk: (j, kk))
    else:
        a_spec = pl.BlockSpec((tk, tm), lambda i, j, kk: (kk, i))
        b_spec = pl.BlockSpec((tk, tn), lambda i, j, kk: (kk, j))
    o_spec = pl.BlockSpec((tm, tn), lambda i, j, kk: (i, j))
    in_specs = [a_spec, b_spec] + ([o_spec] if has_resid else [])
    args = [a, b] + ([resid] if has_resid else [])
    outs = _call(
        body, args, name=name,
        grid=(m // tm, n // tn, nk),
        in_specs=in_specs,
        out_specs=[o_spec] * n_out,
        out_shape=[jax.ShapeDtypeStruct((m, n), dt) for dt in out_dtypes],
        scratch_shapes=[pltpu.VMEM((tm, tn), F32)] if nk > 1 else [],
        sem=("parallel", "parallel", "arbitrary"), ride=ride)
    return outs[0] if n_out == 1 else tuple(outs)


def _row_tile(s):
    return _pick(s, (256, 128))


def _rms_fwd(x, g, *, name, ride=None):
    s, d = x.shape
    tr = _row_tile(s)

    def body(x_ref, g_ref, o_ref):
        xv = x_ref[...]
        r = lax.rsqrt(jnp.mean(xv * xv, axis=1, keepdims=True) + EPS)
        o_ref[...] = (xv * r * g_ref[...]).astype(o_ref.dtype)

    return _call(
        body, [x, g], name=name, grid=(s // tr,),
        in_specs=[pl.BlockSpec((tr, d), lambda i: (i, 0)), pl.BlockSpec((1, d), lambda i: (0, 0))],
        out_specs=[pl.BlockSpec((tr, d), lambda i: (i, 0))],
        out_shape=[jax.ShapeDtypeStruct((s, d), BF16)], sem=("parallel",), ride=ride)[0]


def _rms_bwd(x, g, dh, dres, *, name):
    s, d = x.shape
    tr = _row_tile(s)

    def body(x_ref, g_ref, dh_ref, dres_ref, dx_ref, dg_ref):
        i = pl.program_id(0)
        xv = x_ref[...]
        r = lax.rsqrt(jnp.mean(xv * xv, axis=1, keepdims=True) + EPS)
        xhat = xv * r
        dhv = dh_ref[...]
        dxhat = dhv * g_ref[...]
        proj = jnp.mean(dxhat * xhat, axis=1, keepdims=True)
        dx_ref[...] = dres_ref[...] + r * (dxhat - xhat * proj)

        @pl.when(i == 0)
        def _():
            dg_ref[...] = jnp.zeros_like(dg_ref)

        dg_ref[...] += jnp.sum(dhv * xhat, axis=0, keepdims=True)

    row = pl.BlockSpec((tr, d), lambda i: (i, 0))
    vec = pl.BlockSpec((1, d), lambda i: (0, 0))
    return pl.pallas_call(
        body, name=name, grid=(s // tr,),
        in_specs=[row, vec, row, row],
        out_specs=[row, vec],
        out_shape=[jax.ShapeDtypeStruct((s, d), F32), jax.ShapeDtypeStruct((1, d), F32)],
        compiler_params=_cparams(("arbitrary",)),
    )(x, g, dh, dres)


def _final_loss(x, g, target, *, name):
    s, d = x.shape
    tr = _row_tile(s)

    def body(x_ref, g_ref, t_ref, dx_ref, dg_ref, loss_ref):
        i = pl.program_id(0)
        xv = x_ref[...]
        gv = g_ref[...]
        r = lax.rsqrt(jnp.mean(xv * xv, axis=1, keepdims=True) + EPS)
        xhat = xv * r
        err = xhat * gv - t_ref[...]
        dy = err * (1.0 / d)
        dxhat = dy * gv
        proj = jnp.mean(dxhat * xhat, axis=1, keepdims=True)
        dx_ref[...] = r * (dxhat - xhat * proj)

        @pl.when(i == 0)
        def _():
            dg_ref[...] = jnp.zeros_like(dg_ref)
            loss_ref[...] = jnp.zeros_like(loss_ref)

        dg_ref[...] += jnp.sum(dy * xhat, axis=0, keepdims=True)
        part = 0.5 * jnp.sum(jnp.mean(err * err, axis=1, keepdims=True), axis=0, keepdims=True)
        loss_ref[...] += jnp.broadcast_to(part, loss_ref.shape)

    row = pl.BlockSpec((tr, d), lambda i: (i, 0))
    vec = pl.BlockSpec((1, d), lambda i: (0, 0))
    return pl.pallas_call(
        body, name=name, grid=(s // tr,),
        in_specs=[row, vec, row],
        out_specs=[row, vec, pl.BlockSpec((1, 128), lambda i: (0, 0))],
        out_shape=[jax.ShapeDtypeStruct((s, d), F32), jax.ShapeDtypeStruct((1, d), F32),
                   jax.ShapeDtypeStruct((1, 128), F32)],
        compiler_params=_cparams(("arbitrary",)),
    )(x, g, target)


def _ew(body, ins, in_blocks, outs, out_blocks, grid, *, name, ride=None):
    return _call(body, ins, name=name, grid=grid,
                 in_specs=[pl.BlockSpec(bs, im) for bs, im in in_blocks],
                 out_specs=[pl.BlockSpec(bs, im) for bs, im in out_blocks],
                 out_shape=outs, sem=("parallel",) * len(grid), ride=ride)


def _gate_merge_fwd(gates, o_sb, o_ca, *, name, ride=None):
    s, d = o_sb.shape
    tr, tc = _row_tile(s), _pick(d, (1024, 512, 256, 128))
    nc = d // tc

    def body(gs_ref, gc_ref, os_ref, oc_ref, m_ref):
        m = _sigmoid(gs_ref[...]) * os_ref[...] + _sigmoid(gc_ref[...]) * oc_ref[...]
        m_ref[...] = m.astype(m_ref.dtype)

    blk = ((tr, tc), lambda i, j: (i, j))
    return _ew(body, [gates, gates, o_sb, o_ca],
               [blk, ((tr, tc), lambda i, j: (i, j + nc)), blk, blk],
               [jax.ShapeDtypeStruct((s, d), BF16)], [blk], (s // tr, nc), name=name, ride=ride)[0]


def _gate_merge_bwd(dmerged, gates, o_sb, o_ca, *, name):
    s, d = o_sb.shape
    tr, tc = _row_tile(s), _pick(d, (1024, 512, 256, 128))
    nc = d // tc

    def body(dm_ref, gs_ref, gc_ref, os_ref, oc_ref, dgs_ref, dgc_ref, dos_ref, doc_ref):
        dm = dm_ref[...]
        ss = _sigmoid(gs_ref[...])
        sc = _sigmoid(gc_ref[...])
        dgs_ref[...] = (dm * os_ref[...] * ss * (1.0 - ss)).astype(dgs_ref.dtype)
        dgc_ref[...] = (dm * oc_ref[...] * sc * (1.0 - sc)).astype(dgc_ref.dtype)
        dos_ref[...] = (dm * ss).astype(dos_ref.dtype)
        doc_ref[...] = (dm * sc).astype(doc_ref.dtype)

    blk = ((tr, tc), lambda i, j: (i, j))
    sd = jax.ShapeDtypeStruct((s, d), BF16)
    return _ew(body, [dmerged, gates, gates, o_sb, o_ca],
               [blk, blk, ((tr, tc), lambda i, j: (i, j + nc)), blk, blk],
               [sd, sd, sd, sd], [blk, blk, blk, blk], (s // tr, nc), name=name)


def _swiglu_fwd(gu, *, name, ride=None):
    s, f2 = gu.shape
    f = f2 // 2
    tr, tc = _row_tile(s), _pick(f, (512, 256, 128))
    nc = f // tc

    def body(g_ref, u_ref, a_ref):
        gv = g_ref[...]
        a_ref[...] = (gv * _sigmoid(gv) * u_ref[...]).astype(a_ref.dtype)

    blk = ((tr, tc), lambda i, j: (i, j))
    return _ew(body, [gu, gu], [blk, ((tr, tc), lambda i, j: (i, j + nc))],
               [jax.ShapeDtypeStruct((s, f), BF16)], [blk], (s // tr, nc), name=name, ride=ride)[0]


def _swiglu_bwd(dact, gu, *, name):
    s, f2 = gu.shape
    f = f2 // 2
    tr, tc = 128, _pick(f, (512, 256, 128))

    def body(da_ref, gu_ref, o_ref):
        for at in range(0, f, tc):
            da = da_ref[:, at:at + tc]
            gv = gu_ref[:, at:at + tc]
            sg = _sigmoid(gv)
            o_ref[:, at:at + tc] = (da * gu_ref[:, f + at:f + at + tc] * sg * (1.0 + gv * (1.0 - sg))).astype(o_ref.dtype)
            o_ref[:, f + at:f + at + tc] = (da * gv * sg).astype(o_ref.dtype)

    row = lambda i: (i, 0)
    return _ew(body, [dact, gu], [((tr, f), row), ((tr, f2), row)], [jax.ShapeDtypeStruct((s, f2), BF16)],
               [((tr, f2), row)], (s // tr,), name=name)[0]


def _ple_fwd(x, t, pe, *, name):
    s, d = x.shape
    tr, tc = _row_tile(s), _pick(d, (1024, 512, 256, 128))

    def body(x_ref, t_ref, p_ref, o_ref):
        o_ref[...] = x_ref[...] + _sigmoid(t_ref[...]) * p_ref[...]

    blk = ((tr, tc), lambda i, j: (i, j))
    return _ew(body, [x, t, pe], [blk, blk, blk],
               [jax.ShapeDtypeStruct((s, d), F32)], [blk], (s // tr, d // tc), name=name)[0]


def _ple_bwd(dx, t, pe, *, name):
    s, d = dx.shape
    tr, tc = _row_tile(s), _pick(d, (1024, 512, 256, 128))

    def body(dx_ref, t_ref, p_ref, dt_ref, dp_ref):
        dxv = dx_ref[...]
        sg = _sigmoid(t_ref[...])
        dt_ref[...] = (dxv * p_ref[...] * sg * (1.0 - sg)).astype(dt_ref.dtype)
        dp_ref[...] = (dxv * sg).astype(dp_ref.dtype)

    blk = ((tr, tc), lambda i, j: (i, j))
    sd = jax.ShapeDtypeStruct((s, d), BF16)
    return _ew(body, [dx, t, pe], [blk, blk, blk], [sd, sd], [blk, blk], (s // tr, d // tc), name=name)


def _sb_tri(later):
    row = lax.broadcasted_iota(jnp.int32, (SB_BLOCK, SB_BLOCK), 0)
    col = lax.broadcasted_iota(jnp.int32, (SB_BLOCK, SB_BLOCK), 1)
    tri = (row > col) if later else (row < col)
    return jnp.concatenate([tri.astype(BF16), jnp.ones((SB_BLOCK, SB_BLOCK), BF16)], axis=1)


def _sb_valid(i, j):
    qi = i * SB_ROWS + lax.broadcasted_iota(jnp.int32, (SB_ROWS, SB_KEYS), 0)
    ki = j * SB_KEYS + lax.broadcasted_iota(jnp.int32, (SB_ROWS, SB_KEYS), 1)
    return ki < qi


def _sb_scan(v, tri, run, later):
    hi = v.astype(BF16)
    lo = (v - hi.astype(F32)).astype(BF16)
    outs = [None] * SB_GROUPS
    for b in (reversed(range(SB_GROUPS)) if later else range(SB_GROUPS)):
        cols = slice(b * SB_BLOCK, (b + 1) * SB_BLOCK)
        r = _dot_nn(hi[:, cols], tri) + _dot_nn(lo[:, cols], tri)
        outs[b] = r[:, :SB_BLOCK] + run
        run = run + r[:, SB_BLOCK:]
    return jnp.concatenate(outs, axis=1), run


def _sb_scores(q, kj, scale, valid):
    z = _dot_nt(q, kj) * scale
    t = jnp.log(1.0 + jnp.exp(-jnp.abs(z)))
    return jnp.minimum(z, 0.0) - t, jnp.where(valid, -jnp.maximum(z, 0.0) - t, 0.0)


def _sb_specs(h_count, s, col0):
    q_spec = pl.BlockSpec((SB_ROWS, HEAD_DIM), lambda h, i: (i, col0 + h))
    k_spec = pl.BlockSpec((s, HEAD_DIM), lambda h, i: (0, col0 + h_count + h))
    v_spec = pl.BlockSpec((s, HEAD_DIM), lambda h, i: (0, col0 + 2 * h_count + h))
    return q_spec, k_spec, v_spec


def _sb_fwd(qkv, n_heads, col0, *, name, ride=None):
    s = qkv.shape[0]
    nq = s // SB_ROWS
    scale = HEAD_DIM ** -0.5

    def body(q_ref, k_ref, v_ref, o_ref):
        i = pl.program_id(1)
        steps = ((i + 1) * SB_ROWS - 1) // SB_KEYS + 1
        q = q_ref[...]
        tri = _sb_tri(later=True)

        def step(jj, carry):
            run, acc = carry
            j = steps - 1 - jj
            off = pl.multiple_of(j * SB_KEYS, SB_KEYS)
            valid = _sb_valid(i, j)
            ls, lk = _sb_scores(q, k_ref[pl.ds(off, SB_KEYS), :], scale, valid)
            between, run = _sb_scan(lk, tri, run, later=True)
            a = jnp.where(valid, jnp.exp(ls + between), 0.0)
            return run, acc + _dot_nn(a.astype(BF16), v_ref[pl.ds(off, SB_KEYS), :])

        init = (jnp.zeros((SB_ROWS, SB_BLOCK), F32), jnp.zeros((SB_ROWS, HEAD_DIM), F32))
        _, acc = lax.fori_loop(0, steps, step, init)
        o_ref[...] = acc.astype(o_ref.dtype)

    q_spec, k_spec, v_spec = _sb_specs(n_heads, s, col0)
    return _call(
        body, [qkv, qkv, qkv], name=name, grid=(n_heads, nq),
        in_specs=[q_spec, k_spec, v_spec],
        out_specs=[pl.BlockSpec((SB_ROWS, HEAD_DIM), lambda h, i: (i, h))],
        out_shape=[jax.ShapeDtypeStruct((s, n_heads * HEAD_DIM), BF16)],
        sem=("parallel", "arbitrary"), ride=ride)[0]


def _sb_bwd(qkv, dy, n_heads, col0, *, name, ride=None):
    s = qkv.shape[0]
    nq = s // SB_ROWS
    scale = HEAD_DIM ** -0.5

    def body(q_ref, k_ref, v_ref, dy_ref, dq_ref, dk_ref, dv_ref, e_scr, dk_acc, dv_acc):
        i = pl.program_id(1)
        steps = ((i + 1) * SB_ROWS - 1) // SB_KEYS + 1
        q = q_ref[...]
        dyv = dy_ref[...]

        @pl.when(i == 0)
        def _():
            dk_acc[...] = jnp.zeros_like(dk_acc)
            dv_acc[...] = jnp.zeros_like(dv_acc)

        tri_later = _sb_tri(later=True)

        def pass1(jj, run):
            j = steps - 1 - jj
            off = pl.multiple_of(j * SB_KEYS, SB_KEYS)
            valid = _sb_valid(i, j)
            ls, lk = _sb_scores(q, k_ref[pl.ds(off, SB_KEYS), :], scale, valid)
            between, run = _sb_scan(lk, tri_later, run, later=True)
            a = jnp.where(valid, jnp.exp(ls + between), 0.0)
            e_scr[j] = a * _dot_nt(dyv, v_ref[pl.ds(off, SB_KEYS), :])
            dv_acc[pl.ds(off, SB_KEYS), :] += _dot_tn(a.astype(BF16), dyv)
            return run

        lax.fori_loop(0, steps, pass1, jnp.zeros((SB_ROWS, SB_BLOCK), F32))

        tri_earlier = _sb_tri(later=False)

        def pass2(j, carry):
            run, dq = carry
            off = pl.multiple_of(j * SB_KEYS, SB_KEYS)
            kj = k_ref[pl.ds(off, SB_KEYS), :]
            sg = _sigmoid(_dot_nt(q, kj) * scale)
            e = e_scr[j]
            before, run = _sb_scan(e, tri_earlier, run, later=False)
            dz = jnp.where(_sb_valid(i, j), e * (1.0 - sg) - sg * before, 0.0) * scale
            dzb = dz.astype(BF16)
            dk_acc[pl.ds(off, SB_KEYS), :] += _dot_tn(dzb, q)
            return run, dq + _dot_nn(dzb, kj)

        init = (jnp.zeros((SB_ROWS, SB_BLOCK), F32), jnp.zeros((SB_ROWS, HEAD_DIM), F32))
        _, dq = lax.fori_loop(0, steps, pass2, init)
        dq_ref[...] = dq.astype(dq_ref.dtype)

        @pl.when(i == nq - 1)
        def _():
            dk_ref[...] = dk_acc[...].astype(dk_ref.dtype)
            dv_ref[...] = dv_acc[...].astype(dv_ref.dtype)

    q_spec, k_spec, v_spec = _sb_specs(n_heads, s, col0)
    blk = pl.BlockSpec((SB_ROWS, HEAD_DIM), lambda h, i: (i, h))
    full = pl.BlockSpec((s, HEAD_DIM), lambda h, i: (0, h))
    sd = jax.ShapeDtypeStruct((s, n_heads * HEAD_DIM), BF16)
    return _call(
        body, [qkv, qkv, qkv, dy], name=name, grid=(n_heads, nq),
        in_specs=[q_spec, k_spec, v_spec, blk],
        out_specs=[blk, full, full],
        out_shape=[sd, sd, sd],
        scratch_shapes=[pltpu.VMEM((s // SB_KEYS, SB_ROWS, SB_KEYS), F32),
                        pltpu.VMEM((s, HEAD_DIM), F32), pltpu.VMEM((s, HEAD_DIM), F32)],
        sem=("parallel", "arbitrary"), ride=ride)


def _band_bias(rel_bias):
    h = rel_bias.shape[0]
    width = BAND + CHUNK
    first = width - 1 - N_REL
    line = jnp.concatenate([jnp.broadcast_to(rel_bias[:, :1], (h, first)), rel_bias], axis=1)
    tiled = jnp.broadcast_to(line[:, None, :], (h, CHUNK, width - 1)).reshape(h, CHUNK * (width - 1))
    skew = jnp.pad(tiled, ((0, 0), (0, CHUNK))).reshape(h, CHUNK, width)[:, ::-1, :BAND]
    seen = jnp.arange(BAND) >= CHUNK
    return jnp.where(seen[None, None, :], skew, NEG)


def _band_bias_grad(dbias):
    h = dbias.shape[0]
    width = BAND + CHUNK
    flipped = jnp.pad(dbias[:, ::-1, :], ((0, 0), (0, 0), (0, CHUNK)))
    skew = flipped.reshape(h, CHUNK * width)[:, :CHUNK * (width - 1)].reshape(h, CHUNK, width - 1)
    diag = jnp.sum(skew, axis=1)
    first = width - 1 - N_REL
    clipped = jnp.sum(diag[:, :first + 1], axis=1, keepdims=True)
    return jnp.concatenate([clipped, diag[:, first + 1:]], axis=1)


def _group_bias(band):
    return jnp.concatenate([jnp.pad(band, ((0, 0), (0, 0), ((u + 1) * CHUNK, (CA_PER_STEP - 1 - u) * CHUNK)),
                                    constant_values=NEG) for u in range(CA_PER_STEP)], axis=1)


def _group_bias_grad(dgroup):
    return sum(dgroup[:, u * CHUNK:(u + 1) * CHUNK, (u + 1) * CHUNK:(u + 1) * CHUNK + BAND] for u in range(CA_PER_STEP))


def _ca_load_padded(k_ref, v_ref, kp, vp, s):
    kp[pl.ds(0, CA_PAD), :] = jnp.zeros((CA_PAD, HEAD_DIM), kp.dtype)
    vp[pl.ds(0, CA_PAD), :] = jnp.zeros((CA_PAD, HEAD_DIM), vp.dtype)
    kp[pl.ds(CA_PAD, s), :] = k_ref[...]
    vp[pl.ds(CA_PAD, s), :] = v_ref[...]


def _ca_weights(q, kb, bias, off, scale):
    z = _dot_nt(q, kb) * scale + bias
    pos = off + lax.broadcasted_iota(jnp.int32, (CA_ROWS, CA_BAND), 1)
    z = jnp.where(pos >= CA_PAD, z, NEG)
    p = jnp.exp(z - jnp.max(z, axis=1, keepdims=True))
    return p / jnp.sum(p, axis=1, keepdims=True)


def _ca_specs(h_count, s, col0):
    q_spec = pl.BlockSpec((CA_ROWS, HEAD_DIM), lambda h, c: (c, col0 + h))
    k_spec = pl.BlockSpec((s, HEAD_DIM), lambda h, c: (0, col0 + h_count + h))
    v_spec = pl.BlockSpec((s, HEAD_DIM), lambda h, c: (0, col0 + 2 * h_count + h))
    b_spec = pl.BlockSpec((1, CA_ROWS, CA_BAND), lambda h, c: (h, 0, 0))
    return q_spec, k_spec, v_spec, b_spec


def _ca_fwd(qkv, bias, n_heads, col0, *, name, ride=None):
    s = qkv.shape[0]
    nc = s // CA_ROWS
    scale = HEAD_DIM ** -0.5

    def body(q_ref, k_ref, v_ref, b_ref, o_ref, kp, vp):
        c = pl.program_id(1)

        @pl.when(c == 0)
        def _():
            _ca_load_padded(k_ref, v_ref, kp, vp, s)

        off = pl.multiple_of(c * CA_ROWS, CA_ROWS)
        w = _ca_weights(q_ref[...], kp[pl.ds(off, CA_BAND), :], b_ref[0], off, scale)
        o_ref[...] = _dot_nn(w.astype(BF16), vp[pl.ds(off, CA_BAND), :]).astype(o_ref.dtype)

    q_spec, k_spec, v_spec, b_spec = _ca_specs(n_heads, s, col0)
    return _call(
        body, [qkv, qkv, qkv, bias], name=name, grid=(n_heads, nc),
        in_specs=[q_spec, k_spec, v_spec, b_spec],
        out_specs=[pl.BlockSpec((CA_ROWS, HEAD_DIM), lambda h, c: (c, h))],
        out_shape=[jax.ShapeDtypeStruct((s, n_heads * HEAD_DIM), BF16)],
        scratch_shapes=[pltpu.VMEM((s + CA_PAD, HEAD_DIM), BF16), pltpu.VMEM((s + CA_PAD, HEAD_DIM), BF16)],
        sem=("parallel", "arbitrary"), ride=ride)[0]


def _ca_bwd(qkv, bias, dy, n_heads, col0, *, name, ride=None):
    s = qkv.shape[0]
    nc = s // CA_ROWS
    scale = HEAD_DIM ** -0.5

    def body(q_ref, k_ref, v_ref, b_ref, dy_ref, dq_ref, dk_ref, dv_ref, db_ref, kp, vp, dkp, dvp):
        c = pl.program_id(1)

        @pl.when(c == 0)
        def _():
            _ca_load_padded(k_ref, v_ref, kp, vp, s)
            dkp[...] = jnp.zeros_like(dkp)
            dvp[...] = jnp.zeros_like(dvp)
            db_ref[...] = jnp.zeros_like(db_ref)

        off = pl.multiple_of(c * CA_ROWS, CA_ROWS)
        band = pl.ds(off, CA_BAND)
        q = q_ref[...]
        dyv = dy_ref[...]
        kb = kp[band, :]
        w = _ca_weights(q, kb, b_ref[0], off, scale)
        dw = _dot_nt(dyv, vp[band, :])
        dvp[band, :] += _dot_tn(w.astype(BF16), dyv)
        dz = w * (dw - jnp.sum(w * dw, axis=1, keepdims=True))
        db_ref[0] += dz
        dzs = (dz * scale).astype(BF16)
        dq_ref[...] = _dot_nn(dzs, kb).astype(dq_ref.dtype)
        dkp[band, :] += _dot_tn(dzs, q)

        @pl.when(c == nc - 1)
        def _():
            dk_ref[...] = dkp[pl.ds(CA_PAD, s), :].astype(dk_ref.dtype)
            dv_ref[...] = dvp[pl.ds(CA_PAD, s), :].astype(dv_ref.dtype)

    q_spec, k_spec, v_spec, b_spec = _ca_specs(n_heads, s, col0)
    blk = pl.BlockSpec((CA_ROWS, HEAD_DIM), lambda h, c: (c, h))
    full = pl.BlockSpec((s, HEAD_DIM), lambda h, c: (0, h))
    sd = jax.ShapeDtypeStruct((s, n_heads * HEAD_DIM), BF16)
    return _call(
        body, [qkv, qkv, qkv, bias, dy], name=name, grid=(n_heads, nc),
        in_specs=[q_spec, k_spec, v_spec, b_spec, blk],
        out_specs=[blk, full, full, b_spec],
        out_shape=[sd, sd, sd, jax.ShapeDtypeStruct((n_heads, CA_ROWS, CA_BAND), F32)],
        scratch_shapes=[pltpu.VMEM((s + CA_PAD, HEAD_DIM), BF16), pltpu.VMEM((s + CA_PAD, HEAD_DIM), BF16),
                        pltpu.VMEM((s + CA_PAD, HEAD_DIM), F32), pltpu.VMEM((s + CA_PAD, HEAD_DIM), F32)],
        sem=("parallel", "arbitrary"), ride=ride)


EARLY = ("w_sb_out", "w_ca_out", "w_mix_out")


def _step(x, p, target, small, comm):
    w = comm.w
    d = x.shape[1]
    n_sb = w["w_sb_out"].shape[0] // HEAD_DIM
    n_ca = w["w_ca_out"].shape[0] // HEAD_DIM
    qkv_cols = 3 * HEAD_DIM * (n_sb + n_ca)
    ca_col0 = 3 * n_sb
    both = (F32, BF16)

    h1 = _rms_fwd(x, small["g_mix"], name="rms_mix")
    ffn, ple = ("w_ffn_in",), ("w_ple_gate", "w_ple_in")
    qkv = _mm(h1, w["w_in"], "nn", (BF16,), name="proj_qkv", n=qkv_cols, ride=comm.gather(EARLY, "near"))
    gates = _mm(h1, w["w_in"], "nn", (F32,), name="proj_gates", n=2 * d, b_col_off=qkv_cols,
                ride=comm.gather(ffn, "near", comm.gather(EARLY, "far"), (0, 8)))
    bias = _group_bias(_band_bias(small["rel_bias"]))
    y_sb = _sb_fwd(qkv, n_sb, 0, name="sb_fwd", ride=comm.gather(ffn, "near", comm.gather(EARLY, "pair"), (1, 8, 7)))
    y_ca = _ca_fwd(qkv, bias, n_ca, ca_col0, name="ca_fwd", ride=comm.gather(ffn, "far"))
    out = ("w_ffn_out",)
    o_sb = _mm(y_sb, w["w_sb_out"], "nn", (F32,), name="sb_out", ride=comm.gather(out, "near", part=(0, 4)))
    o_ca = _mm(y_ca, w["w_ca_out"], "nn", (F32,), name="ca_out", ride=comm.gather(out, "near", part=(1, 4)))
    merged = _gate_merge_fwd(gates, o_sb, o_ca, name="gate_merge",
                             ride=comm.gather(out, "near", comm.gather(ffn, "pair"), (2, 4)))
    x1 = _mm(merged, w["w_mix_out"], "nn", (F32,), name="mix_out", resid=x, ride=comm.gather(out, "near", part=(3, 4)))
    h2 = _rms_fwd(x1, small["g_ffn"], name="rms_ffn")
    gu = _mm(h2, w["w_ffn_in"], "nn", (F32,), name="ffn_in", ride=comm.gather(ple, "near", comm.gather(out, "far")))
    act = _swiglu_fwd(gu, name="swiglu", ride=comm.gather(ple, "far", comm.gather(out, "pair")))
    x2 = _mm(act, w["w_ffn_out"], "nn", (F32,), name="ffn_out", resid=x1, ride=comm.gather(ple, "pair"))
    h3 = _rms_fwd(x2, small["g_ple"], name="rms_ple")
    t = _mm(h3, w["w_ple_gate"], "nn", (F32,), name="ple_gate")
    pe = _mm(p, w["w_ple_in"], "nn", (F32,), name="ple_in")
    x3 = _ple_fwd(x2, t, pe, name="ple_add")

    gs = {}
    dx3, gs["g_final"], loss = _final_loss(x3, small["g_final"], target, name="final_loss")
    dt, dpe = _ple_bwd(dx3, t, pe, name="ple_bwd")
    comm.grad("w_ple_in", *_mm(p, dpe, "tn", both, name="dw_ple_in"))
    comm.grad("w_ple_gate", *_mm(h3, dt, "tn", both, name="dw_ple_gate"))
    ple = ("w_ple_in", "w_ple_gate")
    dh3 = _mm(dt, w["w_ple_gate"], "nt", (F32,), name="dh_ple", ride=comm.pair(ple))
    dx2, gs["g_ple"] = _rms_bwd(x2, small["g_ple"], dh3, dx3, name="rms_ple_bwd")
    comm.add(ple)
    comm.grad("w_ffn_out", *_mm(act, dx2, "tn", both, name="dw_ffn_out", ride=comm.chips(ple)))
    dact = _mm(dx2, w["w_ffn_out"], "nt", (F32,), name="dact", ride=comm.pair(("w_ffn_out",)))
    dgu = _swiglu_bwd(dact, gu, name="swiglu_bwd")
    comm.sum(ple)
    comm.add(("w_ffn_out",))
    comm.grad("w_ffn_in", *_mm(h2, dgu, "tn", both, name="dw_ffn_in",
                               ride=comm.share(ple, comm.chips(("w_ffn_out",)))))
    dh2 = _mm(dgu, w["w_ffn_in"], "nt", (F32,), name="dh_ffn", ride=comm.pair(("w_ffn_in",)))
    dx1, gs["g_ffn"] = _rms_bwd(x1, small["g_ffn"], dh2, dx2, name="rms_ffn_bwd")
    comm.add(("w_ffn_in",))
    comm.sum(("w_ffn_out",))
    comm.grad("w_mix_out", *_mm(merged, dx1, "tn", both, name="dw_mix_out", ride=comm.share(("w_ffn_out",))))
    dmerged = _mm(dx1, w["w_mix_out"], "nt", (F32,), name="dmerged", ride=comm.pair(("w_mix_out",)))
    dg_sb, dg_ca, do_sb, do_ca = _gate_merge_bwd(dmerged, gates, o_sb, o_ca, name="gate_merge_bwd")
    comm.add(("w_mix_out",))
    comm.grad("w_sb_out", *_mm(y_sb, do_sb, "tn", both, name="dw_sb_out"))
    comm.grad("w_ca_out", *_mm(y_ca, do_ca, "tn", both, name="dw_ca_out"))
    outs = ("w_sb_out", "w_ca_out")
    dy_sb = _mm(do_sb, w["w_sb_out"], "nt", (BF16,), name="dy_sb", ride=comm.pair(outs))
    dy_ca = _mm(do_ca, w["w_ca_out"], "nt", (BF16,), name="dy_ca")
    comm.add(outs)
    dq_sb, dk_sb, dv_sb = _sb_bwd(qkv, dy_sb, n_sb, 0, name="sb_bwd", ride=comm.chips(("w_ffn_in",)))
    comm.sum(("w_ffn_in",))
    late = ("w_mix_out",) + outs
    dq_ca, dk_ca, dv_ca, dbias = _ca_bwd(qkv, bias, dy_ca, n_ca, ca_col0, name="ca_bwd",
                                         ride=comm.chips(late, comm.share(("w_ffn_in",))))
    comm.sum(late)
    gs["rel_bias"] = _band_bias_grad(_group_bias_grad(dbias))
    dproj = jnp.concatenate([dq_sb, dk_sb, dv_sb, dq_ca, dk_ca, dv_ca, dg_sb, dg_ca], axis=1)
    comm.grad("w_in", *_mm(h1, dproj, "tn", both, name="dw_in", ride=comm.share(late)))
    dh1 = _mm(dproj, w["w_in"], "nt", (F32,), name="dh_mix", ride=comm.pair(("w_in",)))
    grad_x, gs["g_mix"] = _rms_bwd(x, small["g_mix"], dh1, dx1, name="rms_mix_bwd")
    comm.add(("w_in",))
    return loss, grad_x, gs


def _position():
    x, y, c = lax.axis_index("x"), lax.axis_index("y"), lax.axis_index("c")
    chips = [(1 - x, y), (x, 1 - y), (1 - x, 1 - y)]
    return x, y, c, chips


def _aligned(v, m):
    return v if isinstance(v, int) else pl.multiple_of(v, m)


def _piece_dims(shape, axis):
    k, n = shape
    return (k // 2, n // N_CHIPS) if axis == 1 else (k // N_CHIPS // 2, n)


def _piece(ref, shape, axis, j, h, part=(0, 1)):
    pr, pc = _piece_dims(shape, axis)
    nr = pr // part[1] * (part[2] if len(part) > 2 else 1)
    r0 = part[0] * (pr // part[1])
    if axis == 1:
        return ref.at[pl.ds(_aligned(h * pr + r0, 16), nr), pl.ds(_aligned(j * pc, 128), pc)]
    return ref.at[pl.ds(_aligned((2 * j + h) * pr + r0, 16), nr), :]


def _shard_half(ref, h):
    rows = ref.shape[0] // 2
    return ref.at[pl.ds(_aligned(h * rows, 16), rows), :]


def _remote(src, dst, send_sems, recv_sems, k, to):
    return pltpu.make_async_remote_copy(src_ref=src, dst_ref=dst, send_sem=send_sems.at[k],
                                        recv_sem=recv_sems.at[k], device_id=to, device_id_type=MESH)


def _prefetch_call(body, scalars, ins, in_specs, out_shape, out_specs, grid, *, name, ride=None):
    single = not isinstance(out_shape, (list, tuple))
    outs = _call(body, ins, name=name, grid=grid, in_specs=in_specs,
                 out_specs=[out_specs] if single else out_specs, out_shape=[out_shape] if single else out_shape,
                 sem=("parallel",) * len(grid), ride=ride, scalars=scalars)
    return outs[0] if single else outs


def _slab_tiles(pr, pc):
    tc = pc if pc <= 4096 else _pick(pc, (2048, 1024, 512, 256, 128))
    tr = next(t for t in (1024, 512, 256, 128, 64, 32, 16) if pr % t == 0 and t * tc <= 512 * 1024)
    return tr, tc


def _cast_place(w, axis, pos, *, name, ride=None):
    ks, ns = w.shape
    shape = (ks, ns * N_CHIPS) if axis == 1 else (ks * N_CHIPS, ns)
    tr, tc = _slab_tiles(ks, ns)
    nr, nc = ks // tr, ns // tc

    def body(pos_ref, w_ref, o_ref):
        o_ref[...] = w_ref[...].astype(o_ref.dtype)

    if axis == 1:
        out_map = lambda i, j, pos_ref: (i, pos_ref[0] * nc + j)
    else:
        out_map = lambda i, j, pos_ref: (pos_ref[0] * nr + i, j)
    return _prefetch_call(body, pos, [w], [pl.BlockSpec((tr, tc), lambda i, j, pos_ref: (i, j))],
                          jax.ShapeDtypeStruct(shape, BF16), pl.BlockSpec((tr, tc), out_map), (nr, nc), name=name, ride=ride)


def _run(ride, *, name):
    if ride is None:
        return

    def body(o_ref):
        o_ref[...] = jnp.zeros_like(o_ref)

    _call(body, [], name=name, grid=(1,), in_specs=[], out_specs=[pl.BlockSpec((8, 128), lambda i: (0, 0))],
          out_shape=[jax.ShapeDtypeStruct((8, 128), F32)], ride=ride)


def _ride_gather(ride, w, n, axis, stage, part=(0, 1)):
    shape = w[n].shape
    piece = functools.partial(_piece, shape=shape, axis=axis)
    span = part[2] if len(part) > 2 else 1
    halves = [(2 * part[0] + t * span, 2 * part[1], span) for t in range(2)]

    def copies(ins, outs, send_sems, recv_sems, arriving):
        x, y, c, chips = _position()
        me, (xn, yn, dn) = 2 * x + y, [2 * px + py for px, py in chips]
        if stage == "near":
            plan = [(me, c, part, (1 - x, y, c), xn, c, part), (me, c, part, (x, 1 - y, c), yn, c, part)]
        elif stage == "far":
            plan = [(yn, c, halves[1], (1 - x, y, c), dn, c, halves[1]), (xn, c, halves[0], (x, 1 - y, c), dn, c, halves[0])]
        else:
            plan = [(j, c, part, (x, y, 1 - c), j, 1 - c, part) for j in (xn, yn, dn)]
        out = []
        for k, (chip, h, rows, to, from_chip, from_h, from_rows) in enumerate(plan):
            if arriving:
                lands = piece(outs[0], j=from_chip, h=from_h, part=from_rows)
                out.append(_remote(lands, lands, send_sems, recv_sems, k, to))
            else:
                out.append(_remote(piece(ins[0], j=chip, h=h, part=rows), piece(outs[0], j=chip, h=h, part=rows),
                                   send_sems, recv_sems, k, to))
        return out

    def start(*refs):
        for cp in copies(*refs, arriving=False):
            cp.start()

    def finish(*refs):
        for cp in copies(*refs, arriving=True):
            cp.wait_recv()
        for cp in copies(*refs, arriving=False):
            cp.wait_send()

    ride.add([w[n]], [jax.ShapeDtypeStruct(shape, w[n].dtype)], {0: 0}, 3, start, finish,
             lambda outs: w.__setitem__(n, outs[0]))


def _ride_pair(ride, st, axis):
    shape = st["g16"].shape
    pr, pc = _piece_dims(shape, axis)

    def copies(ins, outs, send_sems, recv_sems):
        x, y, c, _ = _position()
        return [_remote(_piece(ins[0], shape, axis, j, 1 - c), outs[0].at[j], send_sems, recv_sems, j, (x, y, 1 - c))
                for j in range(N_CHIPS)]

    def start(*refs):
        for cp in copies(*refs):
            cp.start()

    def finish(*refs):
        for cp in copies(*refs):
            cp.wait()

    ride.add([st["g16"]], [jax.ShapeDtypeStruct((N_CHIPS, pr, pc), BF16)], {}, N_CHIPS, start, finish,
             lambda outs: st.__setitem__("sib", outs[0]))


def _ride_chips(ride, st, rows=None):
    _, pr, pc = st["s16"].shape
    r0, nr = (0, pr) if rows is None else rows

    def copies(ins, outs, send_sems, recv_sems):
        x, y, c, chips = _position()
        return [_remote(ins[0].at[2 * px + py, pl.ds(r0, nr), :], outs[0].at[k, pl.ds(r0, nr), :],
                        send_sems, recv_sems, k, (px, py, c)) for k, (px, py) in enumerate(chips)]

    def start(*refs):
        for cp in copies(*refs):
            cp.start()

    def finish(*refs):
        for cp in copies(*refs):
            cp.wait()

    ins, aliases = ([st["s16"], st["recv"]], {1: 0}) if "recv" in st else ([st["s16"]], {})
    ride.add(ins, [jax.ShapeDtypeStruct((3, pr, pc), BF16)], aliases, 3, start, finish,
             lambda outs: st.__setitem__("recv", outs[0]))


def _ride_share(ride, st):
    def sent(ins, outs, send_sems, recv_sems):
        x, y, c, _ = _position()
        return _remote(_shard_half(ins[0], c), _shard_half(outs[0], c), send_sems, recv_sems, 0, (x, y, 1 - c))

    def landed(ins, outs, send_sems, recv_sems):
        x, y, c, _ = _position()
        other = _shard_half(outs[0], 1 - c)
        return _remote(other, other, send_sems, recv_sems, 0, (x, y, 1 - c))

    def start(*refs):
        sent(*refs).start()

    def finish(*refs):
        landed(*refs).wait_recv()
        sent(*refs).wait_send()

    ride.add([st["shard"]], [jax.ShapeDtypeStruct(st["shard"].shape, F32)], {0: 0}, 1, start, finish,
             lambda outs: st.__setitem__("g", outs[0]))


def _piece_block(axis, nr, nc, chip):
    if axis == 1:
        return lambda *a: ((a[-1][1] * nr + a[-3]), (a[0] if chip is None else chip(a[-1])) * nc + a[-2])
    return lambda *a: ((2 * (a[0] if chip is None else chip(a[-1])) + a[-1][1]) * nr + a[-3], a[-2])


def _pair_add(g32, sib, axis, pos, *, name):
    _, pr, pc = sib.shape
    tr, tc = _slab_tiles(pr, pc)
    nr, nc = pr // tr, pc // tc

    def body(pos_ref, g_ref, b_ref, o16_ref):
        o16_ref[0] = (g_ref[...] + b_ref[0].astype(F32)).astype(o16_ref.dtype)

    blk = pl.BlockSpec((1, tr, tc), lambda j, i, k, pos_ref: (j, i, k))
    return _prefetch_call(body, pos, [g32, sib], [pl.BlockSpec((tr, tc), _piece_block(axis, nr, nc, None)), blk],
                          jax.ShapeDtypeStruct(sib.shape, BF16), blk, (N_CHIPS, nr, nc), name=name)


def _chip_sum(g32, sib, recv, axis, pos, *, name):
    _, pr, pc = sib.shape
    tr, tc = _slab_tiles(pr, pc)
    nr, nc = pr // tr, pc // tc

    def body(pos_ref, g_ref, b_ref, r_ref, o_ref):
        pair = g_ref[...] + b_ref[0].astype(F32)
        o_ref[...] = ((pair + r_ref[0].astype(F32)) + r_ref[1].astype(F32)) + r_ref[2].astype(F32)

    return _prefetch_call(
        body, pos, [g32, sib, recv],
        [pl.BlockSpec((tr, tc), _piece_block(axis, nr, nc, lambda pos_ref: pos_ref[0])),
         pl.BlockSpec((1, tr, tc), lambda i, k, pos_ref: (pos_ref[0], i, k)),
         pl.BlockSpec((3, tr, tc), lambda i, k, pos_ref: (0, i, k))],
        jax.ShapeDtypeStruct((2 * pr, pc), F32),
        pl.BlockSpec((tr, tc), lambda i, k, pos_ref: (pos_ref[1] * nr + i, k)), (nr, nc), name=name)


class _Comm:
    def __init__(self, pos, w):
        self.pos, self.w, self.st = pos, w, {n: {} for n, _ in BIG}

    def gather(self, names, stage, ride=None, part=(0, 1)):
        ride = _Ride() if ride is None else ride
        for n in names:
            _ride_gather(ride, self.w, n, AXIS[n], stage, part)
        return ride

    def grad(self, n, g32, g16):
        self.st[n].update(g32=g32, g16=g16)

    def pair(self, names, ride=None):
        ride = _Ride() if ride is None else ride
        for n in names:
            _ride_pair(ride, self.st[n], AXIS[n])
        return ride

    def add(self, names):
        for n in names:
            st = self.st[n]
            st["s16"] = _pair_add(st["g32"], st["sib"], AXIS[n], self.pos, name="rs_add_" + n)

    def chips(self, names, ride=None, rows=None):
        ride = _Ride() if ride is None else ride
        for n in names:
            _ride_chips(ride, self.st[n], rows)
        return ride

    def sum(self, names):
        for n in names:
            st = self.st[n]
            st["shard"] = _chip_sum(st["g32"], st["sib"], st["recv"], AXIS[n], self.pos, name="rs_sum_" + n)

    def share(self, names, ride=None):
        ride = _Ride() if ride is None else ride
        for n in names:
            _ride_share(ride, self.st[n])
        return ride

    def result(self, n):
        return self.st[n]["g"]


class _NoComm:
    def __init__(self, w):
        self.w, self.st = w, {}

    def grad(self, n, g32, g16):
        self.st[n] = (g32, g16)

    def result(self, n):
        return self.st[n]

    def add(self, names):
        pass

    sum = add

    def gather(self, names, *args, **kwargs):
        return None

    pair = chips = share = gather


def _small_all_reduce(vec, *, name):
    r = vec.shape[0]

    def body(vec_ref, out_ref, slots, send_sems, recv_sems):
        x, y, c, _ = _position()
        me = 4 * x + 2 * y + c
        slots[me] = vec_ref[...]
        sends = []
        for k in range(1, 8):
            to = (x ^ (k >> 2), y ^ ((k >> 1) & 1), c ^ (k & 1))
            cp = _remote(slots.at[me], slots.at[me], send_sems, recv_sems, k - 1, to)
            cp.start()
            sends.append(cp)
        for k in range(1, 8):
            frm = 4 * (x ^ (k >> 2)) + 2 * (y ^ ((k >> 1) & 1)) + (c ^ (k & 1))
            _remote(slots.at[frm], slots.at[frm], send_sems, recv_sems, k - 1, (x, y, c)).wait_recv()
        for cp in sends:
            cp.wait_send()
        total = slots[0]
        for d in range(1, 8):
            total = total + slots[d]
        out_ref[...] = total

    return pl.pallas_call(
        body, name=name,
        in_specs=[pl.BlockSpec(memory_space=pltpu.VMEM)], out_specs=pl.BlockSpec(memory_space=pltpu.VMEM),
        out_shape=jax.ShapeDtypeStruct((r, 128), F32),
        scratch_shapes=[pltpu.VMEM((8, r, 128), F32), pltpu.SemaphoreType.DMA((7,)), pltpu.SemaphoreType.DMA((7,))],
    )(vec)


def _adamw(w, g, m, v, *, name, ride=None):
    r, c = w.shape
    tc = c if c <= 4096 else _pick(c, (2048, 1024, 512, 256, 128))
    tr = next(t for t in (512, 256, 128, 64, 32, 16, 8) if r % t == 0 and t * tc <= 256 * 1024)

    def body(w_ref, g_ref, m_ref, v_ref, d_ref, nm_ref, nv_ref):
        gv = g_ref[...]
        nm = ADAM_B1 * m_ref[...] + (1.0 - ADAM_B1) * gv
        nv = ADAM_B2 * v_ref[...] + (1.0 - ADAM_B2) * (gv * gv)
        m_hat = nm / (1.0 - ADAM_B1 ** ADAM_STEP)
        v_hat = nv / (1.0 - ADAM_B2 ** ADAM_STEP)
        d_ref[...] = -ADAM_LR * (m_hat / (jnp.sqrt(v_hat) + ADAM_EPS) + ADAM_WD * w_ref[...])
        nm_ref[...] = nm
        nv_ref[...] = nv

    blk = ((tr, tc), lambda i, j: (i, j))
    sd = jax.ShapeDtypeStruct((r, c), F32)
    return _ew(body, [w, g, m, v], [blk] * 4, [sd, sd, sd], [blk] * 3, (r // tr, c // tc), name=name, ride=ride)


BIG = (("w_in", 1), ("w_sb_out", 1), ("w_ca_out", 1), ("w_mix_out", 0), ("w_ffn_in", 1), ("w_ffn_out", 0),
       ("w_ple_in", 1), ("w_ple_gate", 0))
AXIS = dict(BIG)
HEAD_PARTS = 8
HEAD_HOSTS = ("w_ffn_in", "w_ffn_out")
TAIL_PARTS = 16
TAIL_HOSTS = {"w_ffn_in": 4, "w_ffn_out": 2, "w_mix_out": 1, "w_ple_gate": 1}
SMALL = ("rel_bias", "g_mix", "g_ffn", "g_ple", "g_final")
ORDER = ("w_in", "w_sb_out", "w_ca_out", "w_mix_out", "rel_bias", "g_mix", "g_ffn", "g_ple", "g_final",
         "w_ffn_in", "w_ffn_out", "w_ple_in", "w_ple_gate")


def _pack(parts):
    flat = jnp.concatenate([a.reshape(-1) for a in parts])
    rows = -(-flat.shape[0] // 1024) * 8
    return jnp.pad(flat, (0, rows * 128 - flat.shape[0])).reshape(rows, 128)


def _unpack(packed, like):
    flat, out, at = packed.reshape(-1), [], 0
    for a in like:
        out.append(flat[at:at + a.size].reshape(a.shape))
        at += a.size
    return out


def kernel(x, p, w_in, w_sb_out, w_ca_out, w_mix_out, rel_bias, g_mix, g_ffn, g_ple, g_final, w_ffn_in, w_ffn_out, w_ple_in, w_ple_gate, loss_target, m_w_in, m_w_sb_out, m_w_ca_out, m_w_mix_out, m_rel_bias, m_g_mix, m_g_ffn, m_g_ple, m_g_final, m_w_ffn_in, m_w_ffn_out, m_w_ple_in, m_w_ple_gate, v_w_in, v_w_sb_out, v_w_ca_out, v_w_mix_out, v_rel_bias, v_g_mix, v_g_ffn, v_g_ple, v_g_final, v_w_ffn_in, v_w_ffn_out, v_w_ple_in, v_w_ple_gate):
    weights = dict(w_in=w_in, w_sb_out=w_sb_out, w_ca_out=w_ca_out, w_mix_out=w_mix_out, rel_bias=rel_bias,
                   g_mix=g_mix, g_ffn=g_ffn, g_ple=g_ple, g_final=g_final, w_ffn_in=w_ffn_in,
                   w_ffn_out=w_ffn_out, w_ple_in=w_ple_in, w_ple_gate=w_ple_gate)
    m_in = dict(w_in=m_w_in, w_sb_out=m_w_sb_out, w_ca_out=m_w_ca_out, w_mix_out=m_w_mix_out, rel_bias=m_rel_bias,
                g_mix=m_g_mix, g_ffn=m_g_ffn, g_ple=m_g_ple, g_final=m_g_final, w_ffn_in=m_w_ffn_in,
                w_ffn_out=m_w_ffn_out, w_ple_in=m_w_ple_in, w_ple_gate=m_w_ple_gate)
    v_in = dict(w_in=v_w_in, w_sb_out=v_w_sb_out, w_ca_out=v_w_ca_out, w_mix_out=v_w_mix_out, rel_bias=v_rel_bias,
                g_mix=v_g_mix, g_ffn=v_g_ffn, g_ple=v_g_ple, g_final=v_g_final, w_ffn_in=v_w_ffn_in,
                w_ffn_out=v_w_ffn_out, w_ple_in=v_w_ple_in, w_ple_gate=v_w_ple_gate)

    pos = jnp.stack([2 * lax.axis_index("x") + lax.axis_index("y"), lax.axis_index("c")]).astype(jnp.int32)
    comm = _Comm(pos, {"w_in": _cast_place(w_in[0], AXIS["w_in"], pos, name="cast_w_in")})
    at = 0
    for n in HEAD_HOSTS:
        ride = comm.gather(("w_in",), "near", part=(at, HEAD_PARTS))
        comm.w[n] = _cast_place(weights[n][0], AXIS[n], pos, name="cast_" + n, ride=ride)
        at += 1
    for n, axis in BIG:
        if n not in comm.w:
            comm.w[n] = _cast_place(weights[n][0], axis, pos, name="cast_" + n)
    _run(comm.gather(("w_in",), "near", part=(at, HEAD_PARTS, HEAD_PARTS - at)), name="gather_w_in_near")
    _run(comm.gather(("w_in",), "far"), name="gather_w_in_far")
    _run(comm.gather(("w_in",), "pair"), name="gather_w_in_pair")
    small = dict(rel_bias=rel_bias[0], g_mix=g_mix, g_ffn=g_ffn, g_ple=g_ple, g_final=g_final.reshape(1, -1))
    loss, grad_x, gs = _step(x[0], p[0, 0], loss_target[0], small, comm)

    grads, delta, new_m, new_v = {}, {}, {}, {}
    rows = comm.st["w_in"]["s16"].shape[1]
    at = 0
    for n in [n for n, _ in BIG if n != "w_in"] + ["w_in"]:
        ride = None
        if n in TAIL_HOSTS:
            count = rows * TAIL_HOSTS[n] // TAIL_PARTS
            ride = comm.chips(("w_in",), rows=(at, count))
            at += count
        if n == "w_in":
            _run(comm.chips(("w_in",), rows=(at, rows - at)), name="rs_chips_w_in")
            comm.sum(("w_in",))
            _run(comm.share(("w_in",)), name="rs_share_w_in")
        g = comm.result(n)
        d, nm, nv = _adamw(weights[n][0], g, m_in[n][0], v_in[n][0], name="adamw_" + n, ride=ride)
        grads[n], delta[n], new_m[n], new_v[n] = g[None], d[None], nm[None], nv[None]

    like = [weights[n] for n in SMALL]
    reduced = _small_all_reduce(_pack([gs[n] for n in SMALL] + [loss[:, :1]]), name="small_all_reduce")
    g_small = _unpack(reduced, like + [loss[:, :1]])
    total_loss = g_small[-1].reshape(())
    g_packed = _pack(g_small[:-1])
    d_s, m_s, v_s = _adamw(_pack(like), g_packed, _pack([m_in[n] for n in SMALL]), _pack([v_in[n] for n in SMALL]),
                           name="adamw_small")
    for n, g, d, nm, nv in zip(SMALL, g_small[:-1], _unpack(d_s, like), _unpack(m_s, like), _unpack(v_s, like)):
        grads[n], delta[n], new_m[n], new_v[n] = g, d, nm, nv

    return (total_loss, grad_x[None], *[grads[n] for n in ORDER], *[delta[n] for n in ORDER],
            *[new_m[n] for n in ORDER], *[new_v[n] for n in ORDER])
```

```python
import functools
import math

import jax
import jax.numpy as jnp
import numpy as np
from jax import lax
from jax.experimental import pallas as pl
from jax.experimental.pallas import tpu as pltpu

F32 = jnp.float32
BF16 = jnp.bfloat16

HEAD_DIM = 128
CHUNK = 64
LEFT_CHUNKS = 8
REL_CLIP = 128
N_REL = REL_CLIP + CHUNK
BAND = (LEFT_CHUNKS + 2) * CHUNK
CA_PER_STEP = 4
CA_ROWS = CA_PER_STEP * CHUNK
CA_BAND = BAND + CA_PER_STEP * CHUNK
CA_PAD = BAND
SB_BLOCK = 128
SB_KEYS = 512
SB_GROUPS = SB_KEYS // SB_BLOCK
SB_ROWS = 512
EPS = 1e-6
NEG = -1e30

ADAM_LR = 0.001
ADAM_B1 = 0.9
ADAM_B2 = 0.999
ADAM_EPS = 1e-08
ADAM_WD = 0.01
ADAM_STEP = 10

VMEM_LIMIT = 48 * 1024 * 1024
MM_VMEM_BUDGET = 36 * 1024 * 1024
V7X_HBM_BYTES_PER_S = 3.7e12
GRID_STEP_S = 0.35e-6
MESH = pl.DeviceIdType.MESH
N_CHIPS = 4


def _pick(dim, prefs):
    for t in prefs:
        if dim % t == 0:
            return t
    raise ValueError(f"no tile for {dim}")


def _cparams(sem=None):
    return pltpu.CompilerParams(dimension_semantics=sem, vmem_limit_bytes=VMEM_LIMIT)


def _sigmoid(v):
    return 1.0 / (1.0 + jnp.exp(-v))


def _dot(a, b, dims):
    return lax.dot_general(a, b, (dims, ((), ())), preferred_element_type=F32)


def _dot_nn(a, b):
    return _dot(a, b, ((1,), (0,)))


def _dot_nt(a, b):
    return _dot(a, b, ((1,), (1,)))


def _dot_tn(a, b):
    return _dot(a, b, ((0,), (0,)))


HBM = pl.BlockSpec(memory_space=pltpu.HBM)


class _Ride:
    def __init__(self):
        self.items = []

    def add(self, ins, outs, aliases, n_sems, start, finish, sink):
        self.items.append((ins, outs, aliases, n_sems, start, finish, sink))


def _call(body, args, *, name, grid, in_specs, out_specs, out_shape, scratch_shapes=(), sem=None, ride=None,
          scalars=None):
    items = ride.items if ride is not None else []
    n_in, n_out, n_scr = len(args), len(out_shape), len(scratch_shapes)
    r_ins = [a for it in items for a in it[0]]
    r_outs = [o for it in items for o in it[1]]
    updated = [id(it[0][i]) for it in items for i in it[2]]
    assert len(set(updated)) == len(updated), "one call may update a buffer in place only once"
    aliases, a, b = {}, n_in, n_out
    for it in items:
        aliases.update({a + i: b + o for i, o in it[2].items()})
        a, b = a + len(it[0]), b + len(it[1])
    sems = [pltpu.SemaphoreType.DMA((it[3],)) for it in items for _ in range(2)]

    def wrapped(*refs):
        head, refs = (refs[:1], refs[1:]) if scalars is not None else ((), refs)
        ins, rin = refs[:n_in], refs[n_in:n_in + len(r_ins)]
        at = n_in + len(r_ins)
        outs, rout = refs[at:at + n_out], refs[at + n_out:at + n_out + len(r_outs)]
        at += n_out + len(r_outs)
        scr, rsem = refs[at:at + n_scr], refs[at + n_scr:]

        def each(which):
            a = b = 0
            for q, it in enumerate(items):
                it[which](rin[a:a + len(it[0])], rout[b:b + len(it[1])], rsem[2 * q], rsem[2 * q + 1])
                a, b = a + len(it[0]), b + len(it[1])

        if items:
            ids = [pl.program_id(d) for d in range(len(grid))]
            first = functools.reduce(jnp.logical_and, [i == 0 for i in ids])
            last = functools.reduce(jnp.logical_and, [i == g - 1 for i, g in zip(ids, grid)])
            pl.when(first)(lambda: each(4))
        body(*head, *ins, *outs, *scr)
        if items:
            pl.when(last)(lambda: each(5))

    specs = dict(grid=grid, in_specs=list(in_specs) + [HBM] * len(r_ins),
                 out_specs=list(out_specs) + [HBM] * len(r_outs), scratch_shapes=list(scratch_shapes) + sems)
    if scalars is not None:
        specs = dict(grid_spec=pltpu.PrefetchScalarGridSpec(num_scalar_prefetch=1, **specs))
        aliases = {i + 1: o for i, o in aliases.items()}
    res = pl.pallas_call(
        wrapped, name=name, **specs,
        out_shape=list(out_shape) + r_outs,
        input_output_aliases=aliases,
        compiler_params=_cparams(("arbitrary",) * len(grid) if items else sem),
    )(*(() if scalars is None else (scalars,)), *args, *r_ins)
    b = n_out
    for it in items:
        it[6](res[b:b + len(it[1])])
        b += len(it[1])
    return list(res[:n_out])


def _mm_tiles(m, n_align, n, k, a_bytes, b_bytes, out_bytes):
    best = None
    tks = sorted({t for t in (k, k // 2, k // 4, 2048, 1024, 512, 256, 128) if t <= k and k % t == 0 and t % 128 == 0})
    for tm in (t for t in (2048, 1024, 512, 256, 128) if m % t == 0):
        for tn in (t for t in (2048, 1024, 512, 256, 128) if n_align % t == 0):
            for tk in tks:
                nk = k // tk
                vmem = 2 * (tm * tk * a_bytes + tk * tn * b_bytes + tm * tn * out_bytes) + tm * tn * 4
                if vmem > MM_VMEM_BUDGET:
                    continue
                traffic = m * k * a_bytes * (n // tn if nk > 1 else 1) + k * n * b_bytes * (m // tm)
                traffic += tm * tk * a_bytes + tk * tn * b_bytes + tm * tn * out_bytes
                traffic += m * n * 4 * nk if nk > 1 else 0
                cost = traffic / V7X_HBM_BYTES_PER_S + (m // tm) * (n // tn) * nk * GRID_STEP_S
                if best is None or cost < best[0]:
                    best = (cost, tm, tn, tk)
    return best[1:]


def _mm(a, b, mode, out_dtypes, *, name, n=None, b_col_off=0, resid=None, ride=None):
    if mode == "nn":
        m, k = a.shape
        n = b.shape[1] if n is None else n
    elif mode == "nt":
        m, k = a.shape
        n = b.shape[0]
    else:
        k, m = a.shape
        n = b.shape[1]
    n_out = len(out_dtypes)
    has_resid = resid is not None
    out_bytes = sum(jnp.dtype(dt).itemsize for dt in out_dtypes) + (4 if has_resid else 0)
    tm, tn, tk = _mm_tiles(m, math.gcd(n, b_col_off) if b_col_off else n, n, k,
                           a.dtype.itemsize, b.dtype.itemsize, out_bytes)
    nk = k // tk
    boff = b_col_off // tn
    dot = {"nn": _dot_nn, "nt": _dot_nt, "tn": _dot_tn}[mode]

    def body(*refs):
        a_ref, b_ref = refs[0], refs[1]
        r_ref = refs[2] if has_resid else None
        o_refs = refs[2 + has_resid: 2 + has_resid + n_out]

        def finish(r):
            if has_resid:
                r = r + r_ref[...]
            for o_ref in o_refs:
                o_ref[...] = r.astype(o_ref.dtype)

        part = dot(a_ref[...].astype(BF16), b_ref[...].astype(BF16))
        if nk == 1:
            finish(part)
            return
        acc_ref = refs[-1]
        kk = pl.program_id(2)

        @pl.when(kk == 0)
        def _():
            acc_ref[...] = part

        @pl.when(kk > 0)
        def _():
            acc_ref[...] += part

        @pl.when(kk == nk - 1)
        def _():
            finish(acc_ref[...])

    if mode == "nn":
        a_spec = pl.BlockSpec((tm, tk), lambda i, j, kk: (i, kk))
        b_spec = pl.BlockSpec((tk, tn), lambda i, j, kk: (kk, j + boff))
    elif mode == "nt":
        a_spec = pl.BlockSpec((tm, tk), lambda i, j, kk: (i, kk))
        b_spec = pl.BlockSpec((tn, tk), lambda i, j, kk: (j, kk))
    else:
        a_spec = pl.BlockSpec((tk, tm), lambda i, j, kk: (kk, i))
        b_spec = pl.BlockSpec((tk, tn), lambda i, j, kk: (kk, j))
    o_spec = pl.BlockSpec((tm, tn), lambda i, j, kk: (i, j))
    in_specs = [a_spec, b_spec] + ([o_spec] if has_resid else [])
    args = [a, b] + ([resid] if has_resid else [])
    outs = _call(
        body, args, name=name,
        grid=(m // tm, n // tn, nk),
        in_specs=in_specs,
        out_specs=[o_spec] * n_out,
        out_shape=[jax.ShapeDtypeStruct((m, n), dt) for dt in out_dtypes],
        scratch_shapes=[pltpu.VMEM((tm, tn), F32)] if nk > 1 else [],
        sem=("parallel", "parallel", "arbitrary"), ride=ride)
    return outs[0] if n_out == 1 else tuple(outs)


def _row_tile(s):
    return _pick(s, (256, 128))


def _rms_fwd(x, g, *, name, ride=None):
    s, d = x.shape
    tr = _row_tile(s)

    def body(x_ref, g_ref, o_ref):
        xv = x_ref[...]
        r = lax.rsqrt(jnp.mean(xv * xv, axis=1, keepdims=True) + EPS)
        o_ref[...] = (xv * r * g_ref[...]).astype(o_ref.dtype)

    return _call(
        body, [x, g], name=name, grid=(s // tr,),
        in_specs=[pl.BlockSpec((tr, d), lambda i: (i, 0)), pl.BlockSpec((1, d), lambda i: (0, 0))],
        out_specs=[pl.BlockSpec((tr, d), lambda i: (i, 0))],
        out_shape=[jax.ShapeDtypeStruct((s, d), BF16)], sem=("parallel",), ride=ride)[0]


def _rms_bwd(x, g, dh, dres, *, name, ride=None):
    s, d = x.shape
    tr = _row_tile(s)

    def body(x_ref, g_ref, dh_ref, dres_ref, dx_ref, dx16_ref, dg_ref):
        i = pl.program_id(0)
        xv = x_ref[...]
        r = lax.rsqrt(jnp.mean(xv * xv, axis=1, keepdims=True) + EPS)
        xhat = xv * r
        dhv = dh_ref[...]
        dxhat = dhv * g_ref[...]
        proj = jnp.mean(dxhat * xhat, axis=1, keepdims=True)
        dx = dres_ref[...] + r * (dxhat - xhat * proj)
        dx_ref[...] = dx
        dx16_ref[...] = dx.astype(dx16_ref.dtype)

        @pl.when(i == 0)
        def _():
            dg_ref[...] = jnp.zeros_like(dg_ref)

        dg_ref[...] += jnp.sum(dhv * xhat, axis=0, keepdims=True)

    row = pl.BlockSpec((tr, d), lambda i: (i, 0))
    vec = pl.BlockSpec((1, d), lambda i: (0, 0))
    return _call(
        body, [x, g, dh, dres], name=name, grid=(s // tr,),
        in_specs=[row, vec, row, row],
        out_specs=[row, row, vec],
        out_shape=[jax.ShapeDtypeStruct((s, d), F32), jax.ShapeDtypeStruct((s, d), BF16),
                   jax.ShapeDtypeStruct((1, d), F32)],
        sem=("arbitrary",), ride=ride)


def _final_loss(x, g, target, *, name):
    s, d = x.shape
    tr = _row_tile(s)

    def body(x_ref, g_ref, t_ref, dx_ref, dg_ref, loss_ref):
        i = pl.program_id(0)
        xv = x_ref[...]
        gv = g_ref[...]
        r = lax.rsqrt(jnp.mean(xv * xv, axis=1, keepdims=True) + EPS)
        xhat = xv * r
        err = xhat * gv - t_ref[...]
        dy = err * (1.0 / d)
        dxhat = dy * gv
        proj = jnp.mean(dxhat * xhat, axis=1, keepdims=True)
        dx_ref[...] = r * (dxhat - xhat * proj)

        @pl.when(i == 0)
        def _():
            dg_ref[...] = jnp.zeros_like(dg_ref)
            loss_ref[...] = jnp.zeros_like(loss_ref)

        dg_ref[...] += jnp.sum(dy * xhat, axis=0, keepdims=True)
        part = 0.5 * jnp.sum(jnp.mean(err * err, axis=1, keepdims=True), axis=0, keepdims=True)
        loss_ref[...] += jnp.broadcast_to(part, loss_ref.shape)

    row = pl.BlockSpec((tr, d), lambda i: (i, 0))
    vec = pl.BlockSpec((1, d), lambda i: (0, 0))
    return pl.pallas_call(
        body, name=name, grid=(s // tr,),
        in_specs=[row, vec, row],
        out_specs=[row, vec, pl.BlockSpec((1, 128), lambda i: (0, 0))],
        out_shape=[jax.ShapeDtypeStruct((s, d), F32), jax.ShapeDtypeStruct((1, d), F32),
                   jax.ShapeDtypeStruct((1, 128), F32)],
        compiler_params=_cparams(("arbitrary",)),
    )(x, g, target)


def _ew(body, ins, in_blocks, outs, out_blocks, grid, *, name, ride=None):
    return _call(body, ins, name=name, grid=grid,
                 in_specs=[pl.BlockSpec(bs, im) for bs, im in in_blocks],
                 out_specs=[pl.BlockSpec(bs, im) for bs, im in out_blocks],
                 out_shape=outs, sem=("parallel",) * len(grid), ride=ride)


def _gate_merge_fwd(gates, o_sb, o_ca, *, name, ride=None):
    s, d = o_sb.shape
    tr, tc = _row_tile(s), _pick(d, (1024, 512, 256, 128))
    nc = d // tc

    def body(gs_ref, gc_ref, os_ref, oc_ref, m_ref):
        m = _sigmoid(gs_ref[...]) * os_ref[...] + _sigmoid(gc_ref[...]) * oc_ref[...]
        m_ref[...] = m.astype(m_ref.dtype)

    blk = ((tr, tc), lambda i, j: (i, j))
    return _ew(body, [gates, gates, o_sb, o_ca],
               [blk, ((tr, tc), lambda i, j: (i, j + nc)), blk, blk],
               [jax.ShapeDtypeStruct((s, d), BF16)], [blk], (s // tr, nc), name=name, ride=ride)[0]


def _gate_merge_bwd(dmerged, gates, o_sb, o_ca, *, name):
    s, d = o_sb.shape
    tr, tc = _row_tile(s), _pick(d, (1024, 512, 256, 128))
    nc = d // tc

    def body(dm_ref, gs_ref, gc_ref, os_ref, oc_ref, dgs_ref, dgc_ref, dos_ref, doc_ref):
        dm = dm_ref[...]
        ss = _sigmoid(gs_ref[...])
        sc = _sigmoid(gc_ref[...])
        dgs_ref[...] = (dm * os_ref[...] * ss * (1.0 - ss)).astype(dgs_ref.dtype)
        dgc_ref[...] = (dm * oc_ref[...] * sc * (1.0 - sc)).astype(dgc_ref.dtype)
        dos_ref[...] = (dm * ss).astype(dos_ref.dtype)
        doc_ref[...] = (dm * sc).astype(doc_ref.dtype)

    blk = ((tr, tc), lambda i, j: (i, j))
    sd = jax.ShapeDtypeStruct((s, d), BF16)
    return _ew(body, [dmerged, gates, gates, o_sb, o_ca],
               [blk, blk, ((tr, tc), lambda i, j: (i, j + nc)), blk, blk],
               [sd, sd, sd, sd], [blk, blk, blk, blk], (s // tr, nc), name=name)


def _swiglu_fwd(gu, *, name, ride=None):
    s, f2 = gu.shape
    f = f2 // 2
    tr, tc = _row_tile(s), _pick(f, (512, 256, 128))
    nc = f // tc

    def body(g_ref, u_ref, a_ref):
        gv = g_ref[...]
        a_ref[...] = (gv * _sigmoid(gv) * u_ref[...]).astype(a_ref.dtype)

    blk = ((tr, tc), lambda i, j: (i, j))
    return _ew(body, [gu, gu], [blk, ((tr, tc), lambda i, j: (i, j + nc))],
               [jax.ShapeDtypeStruct((s, f), BF16)], [blk], (s // tr, nc), name=name, ride=ride)[0]


def _swiglu_bwd(dact, gu, *, name):
    s, f2 = gu.shape
    f = f2 // 2
    tr, tc = 128, _pick(f, (512, 256, 128))

    def body(da_ref, gu_ref, o_ref):
        for at in range(0, f, tc):
            da = da_ref[:, at:at + tc]
            gv = gu_ref[:, at:at + tc]
            sg = _sigmoid(gv)
            o_ref[:, at:at + tc] = (da * gu_ref[:, f + at:f + at + tc] * sg * (1.0 + gv * (1.0 - sg))).astype(o_ref.dtype)
            o_ref[:, f + at:f + at + tc] = (da * gv * sg).astype(o_ref.dtype)

    row = lambda i: (i, 0)
    return _ew(body, [dact, gu], [((tr, f), row), ((tr, f2), row)], [jax.ShapeDtypeStruct((s, f2), BF16)],
               [((tr, f2), row)], (s // tr,), name=name)[0]


def _ple_fwd(x, t, pe, *, name):
    s, d = x.shape
    tr, tc = _row_tile(s), _pick(d, (1024, 512, 256, 128))

    def body(x_ref, t_ref, p_ref, o_ref):
        o_ref[...] = x_ref[...] + _sigmoid(t_ref[...]) * p_ref[...]

    blk = ((tr, tc), lambda i, j: (i, j))
    return _ew(body, [x, t, pe], [blk, blk, blk],
               [jax.ShapeDtypeStruct((s, d), F32)], [blk], (s // tr, d // tc), name=name)[0]


def _ple_bwd(dx, t, pe, *, name):
    s, d = dx.shape
    tr, tc = _row_tile(s), _pick(d, (1024, 512, 256, 128))

    def body(dx_ref, t_ref, p_ref, dt_ref, dp_ref):
        dxv = dx_ref[...]
        sg = _sigmoid(t_ref[...])
        dt_ref[...] = (dxv * p_ref[...] * sg * (1.0 - sg)).astype(dt_ref.dtype)
        dp_ref[...] = (dxv * sg).astype(dp_ref.dtype)

    blk = ((tr, tc), lambda i, j: (i, j))
    sd = jax.ShapeDtypeStruct((s, d), BF16)
    return _ew(body, [dx, t, pe], [blk, blk, blk], [sd, sd], [blk, blk], (s // tr, d // tc), name=name)


def _sb_tri(later):
    row = lax.broadcasted_iota(jnp.int32, (SB_BLOCK, SB_BLOCK), 0)
    col = lax.broadcasted_iota(jnp.int32, (SB_BLOCK, SB_BLOCK), 1)
    tri = (row > col) if later else (row < col)
    return jnp.concatenate([tri.astype(BF16), jnp.ones((SB_BLOCK, SB_BLOCK), BF16)], axis=1)


def _sb_valid(i, j):
    qi = i * SB_ROWS + lax.broadcasted_iota(jnp.int32, (SB_ROWS, SB_KEYS), 0)
    ki = j * SB_KEYS + lax.broadcasted_iota(jnp.int32, (SB_ROWS, SB_KEYS), 1)
    return ki < qi


def _sb_scan(v, tri, run, later):
    hi = v.astype(BF16)
    lo = (v - hi.astype(F32)).astype(BF16)
    outs = [None] * SB_GROUPS
    for b in (reversed(range(SB_GROUPS)) if later else range(SB_GROUPS)):
        cols = slice(b * SB_BLOCK, (b + 1) * SB_BLOCK)
        r = _dot_nn(hi[:, cols], tri) + _dot_nn(lo[:, cols], tri)
        outs[b] = r[:, :SB_BLOCK] + run
        run = run + r[:, SB_BLOCK:]
    return jnp.concatenate(outs, axis=1), run


def _sb_scores(q, kj, scale, valid):
    z = _dot_nt(q, kj) * scale
    t = jnp.log(1.0 + jnp.exp(-jnp.abs(z)))
    return jnp.minimum(z, 0.0) - t, jnp.where(valid, -jnp.maximum(z, 0.0) - t, 0.0)


def _sb_specs(h_count, s, col0):
    q_spec = pl.BlockSpec((SB_ROWS, HEAD_DIM), lambda h, i: (i, col0 + h))
    k_spec = pl.BlockSpec((s, HEAD_DIM), lambda h, i: (0, col0 + h_count + h))
    v_spec = pl.BlockSpec((s, HEAD_DIM), lambda h, i: (0, col0 + 2 * h_count + h))
    return q_spec, k_spec, v_spec


def _sb_fwd(qkv, n_heads, col0, *, name, ride=None):
    s = qkv.shape[0]
    nq = s // SB_ROWS
    scale = HEAD_DIM ** -0.5

    def body(q_ref, k_ref, v_ref, o_ref):
        i = pl.program_id(1)
        steps = ((i + 1) * SB_ROWS - 1) // SB_KEYS + 1
        q = q_ref[...]
        tri = _sb_tri(later=True)

        def step(jj, carry):
            run, acc = carry
            j = steps - 1 - jj
            off = pl.multiple_of(j * SB_KEYS, SB_KEYS)
            valid = _sb_valid(i, j)
            ls, lk = _sb_scores(q, k_ref[pl.ds(off, SB_KEYS), :], scale, valid)
            between, run = _sb_scan(lk, tri, run, later=True)
            a = jnp.where(valid, jnp.exp(ls + between), 0.0)
            return run, acc + _dot_nn(a.astype(BF16), v_ref[pl.ds(off, SB_KEYS), :])

        init = (jnp.zeros((SB_ROWS, SB_BLOCK), F32), jnp.zeros((SB_ROWS, HEAD_DIM), F32))
        _, acc = lax.fori_loop(0, steps, step, init)
        o_ref[...] = acc.astype(o_ref.dtype)

    q_spec, k_spec, v_spec = _sb_specs(n_heads, s, col0)
    return _call(
        body, [qkv, qkv, qkv], name=name, grid=(n_heads, nq),
        in_specs=[q_spec, k_spec, v_spec],
        out_specs=[pl.BlockSpec((SB_ROWS, HEAD_DIM), lambda h, i: (i, h))],
        out_shape=[jax.ShapeDtypeStruct((s, n_heads * HEAD_DIM), BF16)],
        sem=("parallel", "arbitrary"), ride=ride)[0]


def _sb_bwd(qkv, dy, n_heads, col0, *, name, ride=None):
    s = qkv.shape[0]
    nq = s // SB_ROWS
    scale = HEAD_DIM ** -0.5

    def body(q_ref, k_ref, v_ref, dy_ref, dq_ref, dk_ref, dv_ref, e_scr, sg_scr, dk_acc, dv_acc):
        i = pl.program_id(1)
        steps = ((i + 1) * SB_ROWS - 1) // SB_KEYS + 1
        q = q_ref[...]
        dyv = dy_ref[...]

        @pl.when(i == 0)
        def _():
            dk_acc[...] = jnp.zeros_like(dk_acc)
            dv_acc[...] = jnp.zeros_like(dv_acc)

        tri_later = _sb_tri(later=True)

        def pass1(jj, run):
            j = steps - 1 - jj
            off = pl.multiple_of(j * SB_KEYS, SB_KEYS)
            valid = _sb_valid(i, j)
            ls, lk = _sb_scores(q, k_ref[pl.ds(off, SB_KEYS), :], scale, valid)
            between, run = _sb_scan(lk, tri_later, run, later=True)
            a = jnp.where(valid, jnp.exp(ls + between), 0.0)
            e_scr[j] = a * _dot_nt(dyv, v_ref[pl.ds(off, SB_KEYS), :])
            sg_scr[j] = jnp.exp(ls)
            dv_acc[pl.ds(off, SB_KEYS), :] += _dot_tn(a.astype(BF16), dyv)
            return run

        lax.fori_loop(0, steps, pass1, jnp.zeros((SB_ROWS, SB_BLOCK), F32))

        tri_earlier = _sb_tri(later=False)

        def pass2(j, carry):
            run, dq = carry
            off = pl.multiple_of(j * SB_KEYS, SB_KEYS)
            kj = k_ref[pl.ds(off, SB_KEYS), :]
            sg = sg_scr[j]
            e = e_scr[j]
            before, run = _sb_scan(e, tri_earlier, run, later=False)
            dz = jnp.where(_sb_valid(i, j), e * (1.0 - sg) - sg * before, 0.0) * scale
            dzb = dz.astype(BF16)
            dk_acc[pl.ds(off, SB_KEYS), :] += _dot_tn(dzb, q)
            return run, dq + _dot_nn(dzb, kj)

        init = (jnp.zeros((SB_ROWS, SB_BLOCK), F32), jnp.zeros((SB_ROWS, HEAD_DIM), F32))
        _, dq = lax.fori_loop(0, steps, pass2, init)
        dq_ref[...] = dq.astype(dq_ref.dtype)

        @pl.when(i == nq - 1)
        def _():
            dk_ref[...] = dk_acc[...].astype(dk_ref.dtype)
            dv_ref[...] = dv_acc[...].astype(dv_ref.dtype)

    q_spec, k_spec, v_spec = _sb_specs(n_heads, s, col0)
    blk = pl.BlockSpec((SB_ROWS, HEAD_DIM), lambda h, i: (i, h))
    full = pl.BlockSpec((s, HEAD_DIM), lambda h, i: (0, h))
    sd = jax.ShapeDtypeStruct((s, n_heads * HEAD_DIM), BF16)
    return _call(
        body, [qkv, qkv, qkv, dy], name=name, grid=(n_heads, nq),
        in_specs=[q_spec, k_spec, v_spec, blk],
        out_specs=[blk, full, full],
        out_shape=[sd, sd, sd],
        scratch_shapes=[pltpu.VMEM((s // SB_KEYS, SB_ROWS, SB_KEYS), F32), pltpu.VMEM((s // SB_KEYS, SB_ROWS, SB_KEYS), F32),
                        pltpu.VMEM((s, HEAD_DIM), F32), pltpu.VMEM((s, HEAD_DIM), F32)],
        sem=("parallel", "arbitrary"), ride=ride)


def _band_bias(rel_bias):
    h = rel_bias.shape[0]
    width = BAND + CHUNK
    first = width - 1 - N_REL
    line = jnp.concatenate([jnp.broadcast_to(rel_bias[:, :1], (h, first)), rel_bias], axis=1)
    tiled = jnp.broadcast_to(line[:, None, :], (h, CHUNK, width - 1)).reshape(h, CHUNK * (width - 1))
    skew = jnp.pad(tiled, ((0, 0), (0, CHUNK))).reshape(h, CHUNK, width)[:, ::-1, :BAND]
    seen = jnp.arange(BAND) >= CHUNK
    return jnp.where(seen[None, None, :], skew, NEG)


def _band_bias_grad(dbias):
    h = dbias.shape[0]
    width = BAND + CHUNK
    flipped = jnp.pad(dbias[:, ::-1, :], ((0, 0), (0, 0), (0, CHUNK)))
    skew = flipped.reshape(h, CHUNK * width)[:, :CHUNK * (width - 1)].reshape(h, CHUNK, width - 1)
    diag = jnp.sum(skew, axis=1)
    first = width - 1 - N_REL
    clipped = jnp.sum(diag[:, :first + 1], axis=1, keepdims=True)
    return jnp.concatenate([clipped, diag[:, first + 1:]], axis=1)


def _group_bias(band):
    return jnp.concatenate([jnp.pad(band, ((0, 0), (0, 0), ((u + 1) * CHUNK, (CA_PER_STEP - 1 - u) * CHUNK)),
                                    constant_values=NEG) for u in range(CA_PER_STEP)], axis=1)


def _group_bias_grad(dgroup):
    return sum(dgroup[:, u * CHUNK:(u + 1) * CHUNK, (u + 1) * CHUNK:(u + 1) * CHUNK + BAND] for u in range(CA_PER_STEP))


def _ca_load_padded(k_ref, v_ref, kp, vp, s):
    kp[pl.ds(0, CA_PAD), :] = jnp.zeros((CA_PAD, HEAD_DIM), kp.dtype)
    vp[pl.ds(0, CA_PAD), :] = jnp.zeros((CA_PAD, HEAD_DIM), vp.dtype)
    kp[pl.ds(CA_PAD, s), :] = k_ref[...]
    vp[pl.ds(CA_PAD, s), :] = v_ref[...]


def _ca_weights(q, kb, bias, off, scale):
    z = _dot_nt(q, kb) * scale + bias
    pos = off + lax.broadcasted_iota(jnp.int32, (CA_ROWS, CA_BAND), 1)
    z = jnp.where(pos >= CA_PAD, z, NEG)
    p = jnp.exp(z - jnp.max(z, axis=1, keepdims=True))
    return p / jnp.sum(p, axis=1, keepdims=True)


def _ca_specs(h_count, s, col0):
    q_spec = pl.BlockSpec((CA_ROWS, HEAD_DIM), lambda h, c: (c, col0 + h))
    k_spec = pl.BlockSpec((s, HEAD_DIM), lambda h, c: (0, col0 + h_count + h))
    v_spec = pl.BlockSpec((s, HEAD_DIM), lambda h, c: (0, col0 + 2 * h_count + h))
    b_spec = pl.BlockSpec((1, CA_ROWS, CA_BAND), lambda h, c: (h, 0, 0))
    return q_spec, k_spec, v_spec, b_spec


def _ca_fwd(qkv, bias, n_heads, col0, *, name, ride=None):
    s = qkv.shape[0]
    nc = s // CA_ROWS
    scale = HEAD_DIM ** -0.5

    def body(q_ref, k_ref, v_ref, b_ref, o_ref, kp, vp):
        c = pl.program_id(1)

        @pl.when(c == 0)
        def _():
            _ca_load_padded(k_ref, v_ref, kp, vp, s)

        off = pl.multiple_of(c * CA_ROWS, CA_ROWS)
        w = _ca_weights(q_ref[...], kp[pl.ds(off, CA_BAND), :], b_ref[0], off, scale)
        o_ref[...] = _dot_nn(w.astype(BF16), vp[pl.ds(off, CA_BAND), :]).astype(o_ref.dtype)

    q_spec, k_spec, v_spec, b_spec = _ca_specs(n_heads, s, col0)
    return _call(
        body, [qkv, qkv, qkv, bias], name=name, grid=(n_heads, nc),
        in_specs=[q_spec, k_spec, v_spec, b_spec],
        out_specs=[pl.BlockSpec((CA_ROWS, HEAD_DIM), lambda h, c: (c, h))],
        out_shape=[jax.ShapeDtypeStruct((s, n_heads * HEAD_DIM), BF16)],
        scratch_shapes=[pltpu.VMEM((s + CA_PAD, HEAD_DIM), BF16), pltpu.VMEM((s + CA_PAD, HEAD_DIM), BF16)],
        sem=("parallel", "arbitrary"), ride=ride)[0]


def _ca_bwd(qkv, bias, dy, n_heads, col0, *, name, ride=None):
    s = qkv.shape[0]
    nc = s // CA_ROWS
    scale = HEAD_DIM ** -0.5

    def body(q_ref, k_ref, v_ref, b_ref, dy_ref, dq_ref, dk_ref, dv_ref, db_ref, kp, vp, dkp, dvp):
        c = pl.program_id(1)

        @pl.when(c == 0)
        def _():
            _ca_load_padded(k_ref, v_ref, kp, vp, s)
            dkp[...] = jnp.zeros_like(dkp)
            dvp[...] = jnp.zeros_like(dvp)
            db_ref[...] = jnp.zeros_like(db_ref)

        off = pl.multiple_of(c * CA_ROWS, CA_ROWS)
        band = pl.ds(off, CA_BAND)
        q = q_ref[...]
        dyv = dy_ref[...]
        kb = kp[band, :]
        w = _ca_weights(q, kb, b_ref[0], off, scale)
        dw = _dot_nt(dyv, vp[band, :])
        dvp[band, :] += _dot_tn(w.astype(BF16), dyv)
        dz = w * (dw - jnp.sum(w * dw, axis=1, keepdims=True))
        db_ref[0] += dz
        dzs = (dz * scale).astype(BF16)
        dq_ref[...] = _dot_nn(dzs, kb).astype(dq_ref.dtype)
        dkp[band, :] += _dot_tn(dzs, q)

        @pl.when(c == nc - 1)
        def _():
            dk_ref[...] = dkp[pl.ds(CA_PAD, s), :].astype(dk_ref.dtype)
            dv_ref[...] = dvp[pl.ds(CA_PAD, s), :].astype(dv_ref.dtype)

    q_spec, k_spec, v_spec, b_spec = _ca_specs(n_heads, s, col0)
    blk = pl.BlockSpec((CA_ROWS, HEAD_DIM), lambda h, c: (c, h))
    full = pl.BlockSpec((s, HEAD_DIM), lambda h, c: (0, h))
    sd = jax.ShapeDtypeStruct((s, n_heads * HEAD_DIM), BF16)
    return _call(
        body, [qkv, qkv, qkv, bias, dy], name=name, grid=(n_heads, nc),
        in_specs=[q_spec, k_spec, v_spec, b_spec, blk],
        out_specs=[blk, full, full, b_spec],
        out_shape=[sd, sd, sd, jax.ShapeDtypeStruct((n_heads, CA_ROWS, CA_BAND), F32)],
        scratch_shapes=[pltpu.VMEM((s + CA_PAD, HEAD_DIM), BF16), pltpu.VMEM((s + CA_PAD, HEAD_DIM), BF16),
                        pltpu.VMEM((s + CA_PAD, HEAD_DIM), F32), pltpu.VMEM((s + CA_PAD, HEAD_DIM), F32)],
        sem=("parallel", "arbitrary"), ride=ride)


EARLY = ("w_sb_out", "w_ca_out", "w_mix_out")


def _step(x, p, target, small, comm):
    w = comm.w
    d = x.shape[1]
    n_sb = w["w_sb_out"].shape[0] // HEAD_DIM
    n_ca = w["w_ca_out"].shape[0] // HEAD_DIM
    qkv_cols = 3 * HEAD_DIM * (n_sb + n_ca)
    ca_col0 = 3 * n_sb
    both = (F32, BF16)

    h1 = _rms_fwd(x, small["g_mix"], name="rms_mix")
    ffn, ple = ("w_ffn_in",), ("w_ple_gate", "w_ple_in")
    qkv = _mm(h1, w["w_in"], "nn", (BF16,), name="proj_qkv", n=qkv_cols, ride=comm.gather(EARLY, "near"))
    gates = _mm(h1, w["w_in"], "nn", (F32,), name="proj_gates", n=2 * d, b_col_off=qkv_cols,
                ride=comm.gather(ffn, "near", comm.gather(EARLY, "far"), (0, 8)))
    bias = _group_bias(_band_bias(small["rel_bias"]))
    y_sb = _sb_fwd(qkv, n_sb, 0, name="sb_fwd", ride=comm.gather(ffn, "near", comm.gather(EARLY, "pair"), (1, 8, 7)))
    y_ca = _ca_fwd(qkv, bias, n_ca, ca_col0, name="ca_fwd", ride=comm.gather(ffn, "far"))
    out = ("w_ffn_out",)
    o_sb = _mm(y_sb, w["w_sb_out"], "nn", (F32,), name="sb_out", ride=comm.gather(out, "near", part=(0, 4)))
    o_ca = _mm(y_ca, w["w_ca_out"], "nn", (F32,), name="ca_out", ride=comm.gather(out, "near", part=(1, 4)))
    merged = _gate_merge_fwd(gates, o_sb, o_ca, name="gate_merge",
                             ride=comm.gather(out, "near", comm.gather(ffn, "pair"), (2, 4)))
    x1 = _mm(merged, w["w_mix_out"], "nn", (F32,), name="mix_out", resid=x, ride=comm.gather(out, "near", part=(3, 4)))
    h2 = _rms_fwd(x1, small["g_ffn"], name="rms_ffn")
    gu = _mm(h2, w["w_ffn_in"], "nn", (F32,), name="ffn_in", ride=comm.gather(ple, "near", comm.gather(out, "far")))
    act = _swiglu_fwd(gu, name="swiglu", ride=comm.gather(ple, "far", comm.gather(out, "pair")))
    x2 = _mm(act, w["w_ffn_out"], "nn", (F32,), name="ffn_out", resid=x1, ride=comm.gather(ple, "pair"))
    h3 = _rms_fwd(x2, small["g_ple"], name="rms_ple")
    t = _mm(h3, w["w_ple_gate"], "nn", (F32,), name="ple_gate")
    pe = _mm(p, w["w_ple_in"], "nn", (F32,), name="ple_in")
    x3 = _ple_fwd(x2, t, pe, name="ple_add")

    gs = {}
    dx3, gs["g_final"], loss = _final_loss(x3, small["g_final"], target, name="final_loss")
    dt, dpe = _ple_bwd(dx3, t, pe, name="ple_bwd")
    comm.grad("w_ple_in", *_mm(p, dpe, "tn", both, name="dw_ple_in"))
    comm.grad("w_ple_gate", *_mm(h3, dt, "tn", both, name="dw_ple_gate"))
    ple = ("w_ple_in", "w_ple_gate")
    dh3 = _mm(dt, w["w_ple_gate"], "nt", (F32,), name="dh_ple", ride=comm.pair(ple))
    dx2, dx2_16, gs["g_ple"] = _rms_bwd(x2, small["g_ple"], dh3, dx3, name="rms_ple_bwd")
    comm.add(ple)
    comm.grad("w_ffn_out", *_mm(act, dx2_16, "tn", both, name="dw_ffn_out", ride=comm.chips(ple)))
    dact = _mm(dx2_16, w["w_ffn_out"], "nt", (F32,), name="dact", ride=comm.pair(("w_ffn_out",)))
    dgu = _swiglu_bwd(dact, gu, name="swiglu_bwd")
    comm.sum(ple)
    comm.add(("w_ffn_out",))
    comm.grad("w_ffn_in", *_mm(h2, dgu, "tn", both, name="dw_ffn_in",
                               ride=comm.share(ple, comm.chips(("w_ffn_out",)))))
    dh2 = _mm(dgu, w["w_ffn_in"], "nt", (F32,), name="dh_ffn", ride=comm.pair(("w_ffn_in",)))
    dx1, dx1_16, gs["g_ffn"] = _rms_bwd(x1, small["g_ffn"], dh2, dx2, name="rms_ffn_bwd")
    comm.add(("w_ffn_in",))
    comm.sum(("w_ffn_out",))
    comm.grad("w_mix_out", *_mm(merged, dx1_16, "tn", both, name="dw_mix_out", ride=comm.share(("w_ffn_out",))))
    dmerged = _mm(dx1_16, w["w_mix_out"], "nt", (F32,), name="dmerged", ride=comm.pair(("w_mix_out",)))
    dg_sb, dg_ca, do_sb, do_ca = _gate_merge_bwd(dmerged, gates, o_sb, o_ca, name="gate_merge_bwd")
    comm.add(("w_mix_out",))
    comm.grad("w_sb_out", *_mm(y_sb, do_sb, "tn", both, name="dw_sb_out"))
    comm.grad("w_ca_out", *_mm(y_ca, do_ca, "tn", both, name="dw_ca_out"))
    outs = ("w_sb_out", "w_ca_out")
    dy_sb = _mm(do_sb, w["w_sb_out"], "nt", (BF16,), name="dy_sb", ride=comm.pair(outs))
    dy_ca = _mm(do_ca, w["w_ca_out"], "nt", (BF16,), name="dy_ca")
    comm.add(outs)
    dq_sb, dk_sb, dv_sb = _sb_bwd(qkv, dy_sb, n_sb, 0, name="sb_bwd", ride=comm.chips(("w_ffn_in",)))
    comm.sum(("w_ffn_in",))
    late = ("w_mix_out",) + outs
    dq_ca, dk_ca, dv_ca, dbias = _ca_bwd(qkv, bias, dy_ca, n_ca, ca_col0, name="ca_bwd",
                                         ride=comm.chips(late, comm.share(("w_ffn_in",))))
    comm.sum(late)
    gs["rel_bias"] = _band_bias_grad(_group_bias_grad(dbias))
    dproj = jnp.concatenate([dq_sb, dk_sb, dv_sb, dq_ca, dk_ca, dv_ca, dg_sb, dg_ca], axis=1)
    comm.grad("w_in", *_mm(h1, dproj, "tn", both, name="dw_in", ride=comm.share(late)))
    dh1 = _mm(dproj, w["w_in"], "nt", (F32,), name="dh_mix", ride=comm.pair(("w_in",)))
    comm.add(("w_in",))
    grad_x, _, gs["g_mix"] = _rms_bwd(x, small["g_mix"], dh1, dx1, name="rms_mix_bwd", ride=comm.tail(TAIL_FIRST))
    return loss, grad_x, gs


def _position():
    x, y, c = lax.axis_index("x"), lax.axis_index("y"), lax.axis_index("c")
    chips = [(1 - x, y), (x, 1 - y), (1 - x, 1 - y)]
    return x, y, c, chips


def _aligned(v, m):
    return v if isinstance(v, int) else pl.multiple_of(v, m)


def _piece_dims(shape, axis):
    k, n = shape
    return (k // 2, n // N_CHIPS) if axis == 1 else (k // N_CHIPS // 2, n)


def _piece(ref, shape, axis, j, h, part=(0, 1)):
    pr, pc = _piece_dims(shape, axis)
    nr = pr // part[1] * (part[2] if len(part) > 2 else 1)
    r0 = part[0] * (pr // part[1])
    if axis == 1:
        return ref.at[pl.ds(_aligned(h * pr + r0, 16), nr), pl.ds(_aligned(j * pc, 128), pc)]
    return ref.at[pl.ds(_aligned((2 * j + h) * pr + r0, 16), nr), :]


def _shard_half(ref, h):
    rows = ref.shape[0] // 2
    return ref.at[pl.ds(_aligned(h * rows, 16), rows), :]


def _remote(src, dst, send_sems, recv_sems, k, to):
    return pltpu.make_async_remote_copy(src_ref=src, dst_ref=dst, send_sem=send_sems.at[k],
                                        recv_sem=recv_sems.at[k], device_id=to, device_id_type=MESH)


def _prefetch_call(body, scalars, ins, in_specs, out_shape, out_specs, grid, *, name, ride=None):
    single = not isinstance(out_shape, (list, tuple))
    outs = _call(body, ins, name=name, grid=grid, in_specs=in_specs,
                 out_specs=[out_specs] if single else out_specs, out_shape=[out_shape] if single else out_shape,
                 sem=("parallel",) * len(grid), ride=ride, scalars=scalars)
    return outs[0] if single else outs


def _slab_tiles(pr, pc):
    tc = pc if pc <= 4096 else _pick(pc, (2048, 1024, 512, 256, 128))
    tr = next(t for t in (1024, 512, 256, 128, 64, 32, 16) if pr % t == 0 and t * tc <= 512 * 1024)
    return tr, tc


def _cast_place(w, axis, pos, *, name, ride=None):
    ks, ns = w.shape
    shape = (ks, ns * N_CHIPS) if axis == 1 else (ks * N_CHIPS, ns)
    tr, tc = _slab_tiles(ks, ns)
    nr, nc = ks // tr, ns // tc

    def body(pos_ref, w_ref, o_ref):
        o_ref[...] = w_ref[...].astype(o_ref.dtype)

    if axis == 1:
        out_map = lambda i, j, pos_ref: (i, pos_ref[0] * nc + j)
    else:
        out_map = lambda i, j, pos_ref: (pos_ref[0] * nr + i, j)
    return _prefetch_call(body, pos, [w], [pl.BlockSpec((tr, tc), lambda i, j, pos_ref: (i, j))],
                          jax.ShapeDtypeStruct(shape, BF16), pl.BlockSpec((tr, tc), out_map), (nr, nc), name=name, ride=ride)


def _run(ride, *, name):
    if ride is None:
        return

    def body(o_ref):
        o_ref[...] = jnp.zeros_like(o_ref)

    _call(body, [], name=name, grid=(1,), in_specs=[], out_specs=[pl.BlockSpec((8, 128), lambda i: (0, 0))],
          out_shape=[jax.ShapeDtypeStruct((8, 128), F32)], ride=ride)


def _ride_gather(ride, w, n, axis, stage, part=(0, 1)):
    shape = w[n].shape
    piece = functools.partial(_piece, shape=shape, axis=axis)
    span = part[2] if len(part) > 2 else 1
    halves = [(2 * part[0] + t * span, 2 * part[1], span) for t in range(2)]

    def copies(ins, outs, send_sems, recv_sems, arriving):
        x, y, c, chips = _position()
        me, (xn, yn, dn) = 2 * x + y, [2 * px + py for px, py in chips]
        if stage == "near":
            plan = [(me, c, part, (1 - x, y, c), xn, c, part), (me, c, part, (x, 1 - y, c), yn, c, part)]
        elif stage == "far":
            plan = [(yn, c, halves[1], (1 - x, y, c), dn, c, halves[1]), (xn, c, halves[0], (x, 1 - y, c), dn, c, halves[0])]
        else:
            plan = [(j, c, part, (x, y, 1 - c), j, 1 - c, part) for j in (xn, yn, dn)]
        out = []
        for k, (chip, h, rows, to, from_chip, from_h, from_rows) in enumerate(plan):
            if arriving:
                lands = piece(outs[0], j=from_chip, h=from_h, part=from_rows)
                out.append(_remote(lands, lands, send_sems, recv_sems, k, to))
            else:
                out.append(_remote(piece(ins[0], j=chip, h=h, part=rows), piece(outs[0], j=chip, h=h, part=rows),
                                   send_sems, recv_sems, k, to))
        return out

    def start(*refs):
        for cp in copies(*refs, arriving=False):
            cp.start()

    def finish(*refs):
        for cp in copies(*refs, arriving=True):
            cp.wait_recv()
        for cp in copies(*refs, arriving=False):
            cp.wait_send()

    ride.add([w[n]], [jax.ShapeDtypeStruct(shape, w[n].dtype)], {0: 0}, 3, start, finish,
             lambda outs: w.__setitem__(n, outs[0]))


def _ride_pair(ride, st, axis):
    shape = st["g16"].shape
    pr, pc = _piece_dims(shape, axis)

    def copies(ins, outs, send_sems, recv_sems):
        x, y, c, _ = _position()
        return [_remote(_piece(ins[0], shape, axis, j, 1 - c), outs[0].at[j], send_sems, recv_sems, j, (x, y, 1 - c))
                for j in range(N_CHIPS)]

    def start(*refs):
        for cp in copies(*refs):
            cp.start()

    def finish(*refs):
        for cp in copies(*refs):
            cp.wait()

    ride.add([st["g16"]], [jax.ShapeDtypeStruct((N_CHIPS, pr, pc), BF16)], {}, N_CHIPS, start, finish,
             lambda outs: st.__setitem__("sib", outs[0]))


def _ride_chips(ride, st, rows=None):
    _, pr, pc = st["s16"].shape
    r0, nr = (0, pr) if rows is None else rows

    def copies(ins, outs, send_sems, recv_sems):
        x, y, c, chips = _position()
        return [_remote(ins[0].at[2 * px + py, pl.ds(r0, nr), :], outs[0].at[k, pl.ds(r0, nr), :],
                        send_sems, recv_sems, k, (px, py, c)) for k, (px, py) in enumerate(chips)]

    def start(*refs):
        for cp in copies(*refs):
            cp.start()

    def finish(*refs):
        for cp in copies(*refs):
            cp.wait()

    ins, aliases = ([st["s16"], st["recv"]], {1: 0}) if "recv" in st else ([st["s16"]], {})
    ride.add(ins, [jax.ShapeDtypeStruct((3, pr, pc), BF16)], aliases, 3, start, finish,
             lambda outs: st.__setitem__("recv", outs[0]))


def _ride_share(ride, st):
    def sent(ins, outs, send_sems, recv_sems):
        x, y, c, _ = _position()
        return _remote(_shard_half(ins[0], c), _shard_half(outs[0], c), send_sems, recv_sems, 0, (x, y, 1 - c))

    def landed(ins, outs, send_sems, recv_sems):
        x, y, c, _ = _position()
        other = _shard_half(outs[0], 1 - c)
        return _remote(other, other, send_sems, recv_sems, 0, (x, y, 1 - c))

    def start(*refs):
        sent(*refs).start()

    def finish(*refs):
        landed(*refs).wait_recv()
        sent(*refs).wait_send()

    ride.add([st["shard"]], [jax.ShapeDtypeStruct(st["shard"].shape, F32)], {0: 0}, 1, start, finish,
             lambda outs: st.__setitem__("g", outs[0]))


def _piece_block(axis, nr, nc, chip):
    if axis == 1:
        return lambda *a: ((a[-1][1] * nr + a[-3]), (a[0] if chip is None else chip(a[-1])) * nc + a[-2])
    return lambda *a: ((2 * (a[0] if chip is None else chip(a[-1])) + a[-1][1]) * nr + a[-3], a[-2])


def _pair_add(g32, sib, axis, pos, *, name):
    _, pr, pc = sib.shape
    tr, tc = _slab_tiles(pr, pc)
    nr, nc = pr // tr, pc // tc

    def body(pos_ref, g_ref, b_ref, o16_ref):
        o16_ref[0] = (g_ref[...] + b_ref[0].astype(F32)).astype(o16_ref.dtype)

    blk = pl.BlockSpec((1, tr, tc), lambda j, i, k, pos_ref: (j, i, k))
    return _prefetch_call(body, pos, [g32, sib], [pl.BlockSpec((tr, tc), _piece_block(axis, nr, nc, None)), blk],
                          jax.ShapeDtypeStruct(sib.shape, BF16), blk, (N_CHIPS, nr, nc), name=name)


def _chip_sum(g32, sib, recv, axis, pos, *, name):
    _, pr, pc = sib.shape
    tr, tc = _slab_tiles(pr, pc)
    nr, nc = pr // tr, pc // tc

    def body(pos_ref, g_ref, b_ref, r_ref, o_ref):
        pair = g_ref[...] + b_ref[0].astype(F32)
        o_ref[...] = ((pair + r_ref[0].astype(F32)) + r_ref[1].astype(F32)) + r_ref[2].astype(F32)

    return _prefetch_call(
        body, pos, [g32, sib, recv],
        [pl.BlockSpec((tr, tc), _piece_block(axis, nr, nc, lambda pos_ref: pos_ref[0])),
         pl.BlockSpec((1, tr, tc), lambda i, k, pos_ref: (pos_ref[0], i, k)),
         pl.BlockSpec((3, tr, tc), lambda i, k, pos_ref: (0, i, k))],
        jax.ShapeDtypeStruct((2 * pr, pc), F32),
        pl.BlockSpec((tr, tc), lambda i, k, pos_ref: (pos_ref[1] * nr + i, k)), (nr, nc), name=name)


class _Comm:
    def __init__(self, pos, w):
        self.pos, self.w, self.st = pos, w, {n: {} for n, _ in BIG}

    def gather(self, names, stage, ride=None, part=(0, 1)):
        ride = _Ride() if ride is None else ride
        for n in names:
            _ride_gather(ride, self.w, n, AXIS[n], stage, part)
        return ride

    def grad(self, n, g32, g16):
        self.st[n].update(g32=g32, g16=g16)

    def pair(self, names, ride=None):
        ride = _Ride() if ride is None else ride
        for n in names:
            _ride_pair(ride, self.st[n], AXIS[n])
        return ride

    def add(self, names):
        for n in names:
            st = self.st[n]
            st["s16"] = _pair_add(st["g32"], st["sib"], AXIS[n], self.pos, name="rs_add_" + n)

    def chips(self, names, ride=None, rows=None):
        ride = _Ride() if ride is None else ride
        for n in names:
            _ride_chips(ride, self.st[n], rows)
        return ride

    def sum(self, names):
        for n in names:
            st = self.st[n]
            st["shard"] = _chip_sum(st["g32"], st["sib"], st["recv"], AXIS[n], self.pos, name="rs_sum_" + n)

    def share(self, names, ride=None):
        ride = _Ride() if ride is None else ride
        for n in names:
            _ride_share(ride, self.st[n])
        return ride

    def tail(self, count):
        st = self.st["w_in"]
        rows, at = st["s16"].shape[1], st.get("at", 0)
        st["at"] = at + count
        return self.chips(("w_in",), rows=(at * rows // TAIL_PARTS, count * rows // TAIL_PARTS))

    def tail_rest(self):
        return self.tail(TAIL_PARTS - self.st["w_in"].get("at", 0))

    def result(self, n):
        return self.st[n]["g"]


class _NoComm:
    def __init__(self, w):
        self.w, self.st = w, {}

    def grad(self, n, g32, g16):
        self.st[n] = (g32, g16)

    def result(self, n):
        return self.st[n]

    def add(self, names):
        pass

    sum = add

    def gather(self, names, *args, **kwargs):
        return None

    pair = chips = share = tail = gather


def _small_all_reduce(vec, *, name):
    r = vec.shape[0]

    def body(vec_ref, out_ref, slots, send_sems, recv_sems):
        x, y, c, _ = _position()
        me = 4 * x + 2 * y + c
        slots[me] = vec_ref[...]
        sends = []
        for k in range(1, 8):
            to = (x ^ (k >> 2), y ^ ((k >> 1) & 1), c ^ (k & 1))
            cp = _remote(slots.at[me], slots.at[me], send_sems, recv_sems, k - 1, to)
            cp.start()
            sends.append(cp)
        for k in range(1, 8):
            frm = 4 * (x ^ (k >> 2)) + 2 * (y ^ ((k >> 1) & 1)) + (c ^ (k & 1))
            _remote(slots.at[frm], slots.at[frm], send_sems, recv_sems, k - 1, (x, y, c)).wait_recv()
        for cp in sends:
            cp.wait_send()
        total = slots[0]
        for d in range(1, 8):
            total = total + slots[d]
        out_ref[...] = total

    return pl.pallas_call(
        body, name=name,
        in_specs=[pl.BlockSpec(memory_space=pltpu.VMEM)], out_specs=pl.BlockSpec(memory_space=pltpu.VMEM),
        out_shape=jax.ShapeDtypeStruct((r, 128), F32),
        scratch_shapes=[pltpu.VMEM((8, r, 128), F32), pltpu.SemaphoreType.DMA((7,)), pltpu.SemaphoreType.DMA((7,))],
    )(vec)


def _adamw(w, g, m, v, *, name, ride=None):
    r, c = w.shape
    tc = c if c <= 4096 else _pick(c, (2048, 1024, 512, 256, 128))
    tr = next(t for t in (512, 256, 128, 64, 32, 16, 8) if r % t == 0 and t * tc <= 256 * 1024)

    def body(w_ref, g_ref, m_ref, v_ref, d_ref, nm_ref, nv_ref):
        gv = g_ref[...]
        nm = ADAM_B1 * m_ref[...] + (1.0 - ADAM_B1) * gv
        nv = ADAM_B2 * v_ref[...] + (1.0 - ADAM_B2) * (gv * gv)
        m_hat = nm / (1.0 - ADAM_B1 ** ADAM_STEP)
        v_hat = nv / (1.0 - ADAM_B2 ** ADAM_STEP)
        d_ref[...] = -ADAM_LR * (m_hat / (jnp.sqrt(v_hat) + ADAM_EPS) + ADAM_WD * w_ref[...])
        nm_ref[...] = nm
        nv_ref[...] = nv

    blk = ((tr, tc), lambda i, j: (i, j))
    sd = jax.ShapeDtypeStruct((r, c), F32)
    return _ew(body, [w, g, m, v], [blk] * 4, [sd, sd, sd], [blk] * 3, (r // tr, c // tc), name=name, ride=ride)


BIG = (("w_in", 1), ("w_sb_out", 1), ("w_ca_out", 1), ("w_mix_out", 0), ("w_ffn_in", 1), ("w_ffn_out", 0),
       ("w_ple_in", 1), ("w_ple_gate", 0))
AXIS = dict(BIG)
HEAD_PARTS = 8
HEAD_HOSTS = ("w_ffn_in", "w_ffn_out")
TAIL_PARTS = 16
TAIL_FIRST = 2
TAIL_HOSTS = {"w_ffn_in": 4, "w_ffn_out": 2, "w_mix_out": 1, "w_ple_gate": 1}
SMALL = ("rel_bias", "g_mix", "g_ffn", "g_ple", "g_final")
ORDER = ("w_in", "w_sb_out", "w_ca_out", "w_mix_out", "rel_bias", "g_mix", "g_ffn", "g_ple", "g_final",
         "w_ffn_in", "w_ffn_out", "w_ple_in", "w_ple_gate")


def _pack(parts):
    flat = jnp.concatenate([a.reshape(-1) for a in parts])
    rows = -(-flat.shape[0] // 1024) * 8
    return jnp.pad(flat, (0, rows * 128 - flat.shape[0])).reshape(rows, 128)


def _unpack(packed, like):
    flat, out, at = packed.reshape(-1), [], 0
    for a in like:
        out.append(flat[at:at + a.size].reshape(a.shape))
        at += a.size
    return out


def kernel(x, p, w_in, w_sb_out, w_ca_out, w_mix_out, rel_bias, g_mix, g_ffn, g_ple, g_final, w_ffn_in, w_ffn_out, w_ple_in, w_ple_gate, loss_target, m_w_in, m_w_sb_out, m_w_ca_out, m_w_mix_out, m_rel_bias, m_g_mix, m_g_ffn, m_g_ple, m_g_final, m_w_ffn_in, m_w_ffn_out, m_w_ple_in, m_w_ple_gate, v_w_in, v_w_sb_out, v_w_ca_out, v_w_mix_out, v_rel_bias, v_g_mix, v_g_ffn, v_g_ple, v_g_final, v_w_ffn_in, v_w_ffn_out, v_w_ple_in, v_w_ple_gate):
    weights = dict(w_in=w_in, w_sb_out=w_sb_out, w_ca_out=w_ca_out, w_mix_out=w_mix_out, rel_bias=rel_bias,
                   g_mix=g_mix, g_ffn=g_ffn, g_ple=g_ple, g_final=g_final, w_ffn_in=w_ffn_in,
                   w_ffn_out=w_ffn_out, w_ple_in=w_ple_in, w_ple_gate=w_ple_gate)
    m_in = dict(w_in=m_w_in, w_sb_out=m_w_sb_out, w_ca_out=m_w_ca_out, w_mix_out=m_w_mix_out, rel_bias=m_rel_bias,
                g_mix=m_g_mix, g_ffn=m_g_ffn, g_ple=m_g_ple, g_final=m_g_final, w_ffn_in=m_w_ffn_in,
                w_ffn_out=m_w_ffn_out, w_ple_in=m_w_ple_in, w_ple_gate=m_w_ple_gate)
    v_in = dict(w_in=v_w_in, w_sb_out=v_w_sb_out, w_ca_out=v_w_ca_out, w_mix_out=v_w_mix_out, rel_bias=v_rel_bias,
                g_mix=v_g_mix, g_ffn=v_g_ffn, g_ple=v_g_ple, g_final=v_g_final, w_ffn_in=v_w_ffn_in,
                w_ffn_out=v_w_ffn_out, w_ple_in=v_w_ple_in, w_ple_gate=v_w_ple_gate)

    pos = jnp.stack([2 * lax.axis_index("x") + lax.axis_index("y"), lax.axis_index("c")]).astype(jnp.int32)
    comm = _Comm(pos, {"w_in": _cast_place(w_in[0], AXIS["w_in"], pos, name="cast_w_in")})
    at = 0
    for n in HEAD_HOSTS:
        ride = comm.gather(("w_in",), "near", part=(at, HEAD_PARTS))
        comm.w[n] = _cast_place(weights[n][0], AXIS[n], pos, name="cast_" + n, ride=ride)
        at += 1
    for n, axis in BIG:
        if n not in comm.w:
            comm.w[n] = _cast_place(weights[n][0], axis, pos, name="cast_" + n)
    _run(comm.gather(("w_in",), "near", part=(at, HEAD_PARTS, HEAD_PARTS - at)), name="gather_w_in_near")
    _run(comm.gather(("w_in",), "far"), name="gather_w_in_far")
    _run(comm.gather(("w_in",), "pair"), name="gather_w_in_pair")
    small = dict(rel_bias=rel_bias[0], g_mix=g_mix, g_ffn=g_ffn, g_ple=g_ple, g_final=g_final.reshape(1, -1))
    loss, grad_x, gs = _step(x[0], p[0, 0], loss_target[0], small, comm)

    grads, delta, new_m, new_v = {}, {}, {}, {}
    for n in [n for n, _ in BIG if n != "w_in"] + ["w_in"]:
        ride = comm.tail(TAIL_HOSTS[n]) if n in TAIL_HOSTS else None
        if n == "w_in":
            _run(comm.tail_rest(), name="rs_chips_w_in")
            comm.sum(("w_in",))
            _run(comm.share(("w_in",)), name="rs_share_w_in")
        g = comm.result(n)
        d, nm, nv = _adamw(weights[n][0], g, m_in[n][0], v_in[n][0], name="adamw_" + n, ride=ride)
        grads[n], delta[n], new_m[n], new_v[n] = g[None], d[None], nm[None], nv[None]

    like = [weights[n] for n in SMALL]
    reduced = _small_all_reduce(_pack([gs[n] for n in SMALL] + [loss[:, :1]]), name="small_all_reduce")
    g_small = _unpack(reduced, like + [loss[:, :1]])
    total_loss = g_small[-1].reshape(())
    g_packed = _pack(g_small[:-1])
    d_s, m_s, v_s = _adamw(_pack(like), g_packed, _pack([m_in[n] for n in SMALL]), _pack([v_in[n] for n in SMALL]),
                           name="adamw_small")
    for n, g, d, nm, nv in zip(SMALL, g_small[:-1], _unpack(d_s, like), _unpack(m_s, like), _unpack(v_s, like)):
        grads[n], delta[n], new_m[n], new_v[n] = g, d, nm, nv

    return (total_loss, grad_x[None], *[grads[n] for n in ORDER], *[delta[n] for n in ORDER],
            *[new_m[n] for n in ORDER], *[new_v[n] for n in ORDER])
```

```python
import functools
import math

import jax
import jax.numpy as jnp
import numpy as np
from jax import lax
from jax.experimental import pallas as pl
from jax.experimental.pallas import tpu as pltpu

F32 = jnp.float32
BF16 = jnp.bfloat16

HEAD_DIM = 128
CHUNK = 64
LEFT_CHUNKS = 8
REL_CLIP = 128
N_REL = REL_CLIP + CHUNK
BAND = (LEFT_CHUNKS + 2) * CHUNK
CA_PER_STEP = 4
CA_ROWS = CA_PER_STEP * CHUNK
CA_BAND = BAND + CA_PER_STEP * CHUNK
CA_PAD = BAND
SB_BLOCK = 128
SB_KEYS = 512
SB_GROUPS = SB_KEYS // SB_BLOCK
SB_ROWS = 512
EPS = 1e-6
NEG = -1e30

ADAM_LR = 0.001
ADAM_B1 = 0.9
ADAM_B2 = 0.999
ADAM_EPS = 1e-08
ADAM_WD = 0.01
ADAM_STEP = 10

VMEM_LIMIT = 48 * 1024 * 1024
MM_VMEM_BUDGET = 36 * 1024 * 1024
V7X_HBM_BYTES_PER_S = 3.7e12
GRID_STEP_S = 0.35e-6
MESH = pl.DeviceIdType.MESH
N_CHIPS = 4


def _pick(dim, prefs):
    for t in prefs:
        if dim % t == 0:
            return t
    raise ValueError(f"no tile for {dim}")


def _cparams(sem=None):
    return pltpu.CompilerParams(dimension_semantics=sem, vmem_limit_bytes=VMEM_LIMIT)


def _sigmoid(v):
    return 1.0 / (1.0 + jnp.exp(-v))


def _dot(a, b, dims):
    return lax.dot_general(a, b, (dims, ((), ())), preferred_element_type=F32)


def _dot_nn(a, b):
    return _dot(a, b, ((1,), (0,)))


def _dot_nt(a, b):
    return _dot(a, b, ((1,), (1,)))


def _dot_tn(a, b):
    return _dot(a, b, ((0,), (0,)))


HBM = pl.BlockSpec(memory_space=pltpu.HBM)


class _Ride:
    def __init__(self):
        self.items = []

    def add(self, ins, outs, aliases, n_sems, start, finish, sink):
        self.items.append((ins, outs, aliases, n_sems, start, finish, sink))


def _call(body, args, *, name, grid, in_specs, out_specs, out_shape, scratch_shapes=(), sem=None, ride=None,
          scalars=None, onto=()):
    items = ride.items if ride is not None else []
    if onto:
        args, in_specs = list(args) + list(onto), list(in_specs) + [HBM] * len(onto)
        inner, body = body, lambda *refs: inner(*refs[:len(args) - len(onto)], *refs[len(args):])
    n_in, n_out, n_scr = len(args), len(out_shape), len(scratch_shapes)
    r_ins = [a for it in items for a in it[0]]
    r_outs = [o for it in items for o in it[1]]
    updated = [id(it[0][i]) for it in items for i in it[2]]
    assert len(set(updated)) == len(updated), "one call may update a buffer in place only once"
    aliases, a, b = {n_in - len(onto) + t: t for t in range(len(onto))}, n_in, n_out
    for it in items:
        aliases.update({a + i: b + o for i, o in it[2].items()})
        a, b = a + len(it[0]), b + len(it[1])
    sems = [pltpu.SemaphoreType.DMA((it[3],)) for it in items for _ in range(2)]

    def wrapped(*refs):
        head, refs = (refs[:1], refs[1:]) if scalars is not None else ((), refs)
        ins, rin = refs[:n_in], refs[n_in:n_in + len(r_ins)]
        at = n_in + len(r_ins)
        outs, rout = refs[at:at + n_out], refs[at + n_out:at + n_out + len(r_outs)]
        at += n_out + len(r_outs)
        scr, rsem = refs[at:at + n_scr], refs[at + n_scr:]

        def each(which):
            a = b = 0
            for q, it in enumerate(items):
                it[which](rin[a:a + len(it[0])], rout[b:b + len(it[1])], rsem[2 * q], rsem[2 * q + 1])
                a, b = a + len(it[0]), b + len(it[1])

        if items:
            ids = [pl.program_id(d) for d in range(len(grid))]
            first = functools.reduce(jnp.logical_and, [i == 0 for i in ids])
            last = functools.reduce(jnp.logical_and, [i == g - 1 for i, g in zip(ids, grid)])
            pl.when(first)(lambda: each(4))
        body(*head, *ins, *outs, *scr)
        if items:
            pl.when(last)(lambda: each(5))

    specs = dict(grid=grid, in_specs=list(in_specs) + [HBM] * len(r_ins),
                 out_specs=list(out_specs) + [HBM] * len(r_outs), scratch_shapes=list(scratch_shapes) + sems)
    if scalars is not None:
        specs = dict(grid_spec=pltpu.PrefetchScalarGridSpec(num_scalar_prefetch=1, **specs))
        aliases = {i + 1: o for i, o in aliases.items()}
    res = pl.pallas_call(
        wrapped, name=name, **specs,
        out_shape=list(out_shape) + r_outs,
        input_output_aliases=aliases,
        compiler_params=_cparams(("arbitrary",) * len(grid) if items else sem),
    )(*(() if scalars is None else (scalars,)), *args, *r_ins)
    b = n_out
    for it in items:
        it[6](res[b:b + len(it[1])])
        b += len(it[1])
    return list(res[:n_out])


def _mm_tiles(m, n_align, n, k, a_bytes, b_bytes, out_bytes):
    best = None
    tks = sorted({t for t in (k, k // 2, k // 4, 2048, 1024, 512, 256, 128) if t <= k and k % t == 0 and t % 128 == 0})
    for tm in (t for t in (2048, 1024, 512, 256, 128) if m % t == 0):
        for tn in (t for t in (2048, 1024, 512, 256, 128) if n_align % t == 0):
            for tk in tks:
                nk = k // tk
                vmem = 2 * (tm * tk * a_bytes + tk * tn * b_bytes + tm * tn * out_bytes) + tm * tn * 4
                if vmem > MM_VMEM_BUDGET:
                    continue
                traffic = m * k * a_bytes * (n // tn if nk > 1 else 1) + k * n * b_bytes * (m // tm)
                traffic += tm * tk * a_bytes + tk * tn * b_bytes + tm * tn * out_bytes
                traffic += m * n * 4 * nk if nk > 1 else 0
                cost = traffic / V7X_HBM_BYTES_PER_S + (m // tm) * (n // tn) * nk * GRID_STEP_S
                if best is None or cost < best[0]:
                    best = (cost, tm, tn, tk)
    return best[1:]


def _mm(a, b, mode, out_dtypes, *, name, n=None, b_col_off=0, resid=None, ride=None, rows=None, onto=()):
    if mode == "nn":
        m, k = a.shape
        n = b.shape[1] if n is None else n
    elif mode == "nt":
        m, k = a.shape
        n = b.shape[0]
    else:
        k, m = a.shape
        n = b.shape[1]
    m_all, (row0, m) = m, (0, m) if rows is None else rows
    n_out = len(out_dtypes)
    has_resid = resid is not None
    out_bytes = sum(jnp.dtype(dt).itemsize for dt in out_dtypes) + (4 if has_resid else 0)
    tm, tn, tk = _mm_tiles(math.gcd(m, row0) if row0 else m, math.gcd(n, b_col_off) if b_col_off else n, n, k,
                           a.dtype.itemsize, b.dtype.itemsize, out_bytes)
    nk = k // tk
    boff, roff = b_col_off // tn, row0 // tm
    dot = {"nn": _dot_nn, "nt": _dot_nt, "tn": _dot_tn}[mode]

    def body(*refs):
        a_ref, b_ref = refs[0], refs[1]
        r_ref = refs[2] if has_resid else None
        o_refs = refs[2 + has_resid: 2 + has_resid + n_out]

        def finish(r):
            if has_resid:
                r = r + r_ref[...]
            for o_ref in o_refs:
                o_ref[...] = r.astype(o_ref.dtype)

        part = dot(a_ref[...].astype(BF16), b_ref[...].astype(BF16))
        if nk == 1:
            finish(part)
            return
        acc_ref = refs[-1]
        kk = pl.program_id(2)

        @pl.when(kk == 0)
        def _():
            acc_ref[...] = part

        @pl.when(kk > 0)
        def _():
            acc_ref[...] += part

        @pl.when(kk == nk - 1)
        def _():
            finish(acc_ref[...])

    if mode == "nn":
        a_spec = pl.BlockSpec((tm, tk), lambda i, j, kk: (i + roff, kk))
        b_spec = pl.BlockSpec((tk, tn), lambda i, j, kk: (kk, j + boff))
    elif mode == "nt":
        a_spec = pl.BlockSpec((tm, tk), lambda i, j, kk: (i + roff, kk))
        b_spec = pl.BlockSpec((tn, tk), lambda i, j, kk: (j, kk))
    else:
        a_spec = pl.BlockSpec((tk, tm), lambda i, j, kk: (kk, i))
        b_spec = pl.BlockSpec((tk, tn), lambda i, j, kk: (kk, j))
    o_spec = pl.BlockSpec((tm, tn), lambda i, j, kk: (i + roff, j))
    in_specs = [a_spec, b_spec] + ([o_spec] if has_resid else [])
    args = [a, b] + ([resid] if has_resid else [])
    outs = _call(
        body, args, name=name,
        grid=(m // tm, n // tn, nk),
        in_specs=in_specs,
        out_specs=[o_spec] * n_out,
        out_shape=[jax.ShapeDtypeStruct((m_all, n), dt) for dt in out_dtypes],
        scratch_shapes=[pltpu.VMEM((tm, tn), F32)] if nk > 1 else [],
        sem=("parallel", "parallel", "arbitrary"), ride=ride, onto=onto)
    return outs[0] if n_out == 1 else tuple(outs)


def _row_tile(s):
    return _pick(s, (256, 128))


def _rms_fwd(x, g, *, name, ride=None):
    s, d = x.shape
    tr = _row_tile(s)

    def body(x_ref, g_ref, o_ref):
        xv = x_ref[...]
        r = lax.rsqrt(jnp.mean(xv * xv, axis=1, keepdims=True) + EPS)
        o_ref[...] = (xv * r * g_ref[...]).astype(o_ref.dtype)

    return _call(
        body, [x, g], name=name, grid=(s // tr,),
        in_specs=[pl.BlockSpec((tr, d), lambda i: (i, 0)), pl.BlockSpec((1, d), lambda i: (0, 0))],
        out_specs=[pl.BlockSpec((tr, d), lambda i: (i, 0))],
        out_shape=[jax.ShapeDtypeStruct((s, d), BF16)], sem=("parallel",), ride=ride)[0]


def _rms_bwd(x, g, dh, dres, *, name, ride=None):
    s, d = x.shape
    tr = _row_tile(s)

    def body(x_ref, g_ref, dh_ref, dres_ref, dx_ref, dx16_ref, dg_ref):
        i = pl.program_id(0)
        xv = x_ref[...]
        r = lax.rsqrt(jnp.mean(xv * xv, axis=1, keepdims=True) + EPS)
        xhat = xv * r
        dhv = dh_ref[...]
        dxhat = dhv * g_ref[...]
        proj = jnp.mean(dxhat * xhat, axis=1, keepdims=True)
        dx = dres_ref[...] + r * (dxhat - xhat * proj)
        dx_ref[...] = dx
        dx16_ref[...] = dx.astype(dx16_ref.dtype)

        @pl.when(i == 0)
        def _():
            dg_ref[...] = jnp.zeros_like(dg_ref)

        dg_ref[...] += jnp.sum(dhv * xhat, axis=0, keepdims=True)

    row = pl.BlockSpec((tr, d), lambda i: (i, 0))
    vec = pl.BlockSpec((1, d), lambda i: (0, 0))
    return _call(
        body, [x, g, dh, dres], name=name, grid=(s // tr,),
        in_specs=[row, vec, row, row],
        out_specs=[row, row, vec],
        out_shape=[jax.ShapeDtypeStruct((s, d), F32), jax.ShapeDtypeStruct((s, d), BF16),
                   jax.ShapeDtypeStruct((1, d), F32)],
        sem=("arbitrary",), ride=ride)


def _final_loss(x, g, target, *, name):
    s, d = x.shape
    tr = _row_tile(s)

    def body(x_ref, g_ref, t_ref, dx_ref, dg_ref, loss_ref):
        i = pl.program_id(0)
        xv = x_ref[...]
        gv = g_ref[...]
        r = lax.rsqrt(jnp.mean(xv * xv, axis=1, keepdims=True) + EPS)
        xhat = xv * r
        err = xhat * gv - t_ref[...]
        dy = err * (1.0 / d)
        dxhat = dy * gv
        proj = jnp.mean(dxhat * xhat, axis=1, keepdims=True)
        dx_ref[...] = r * (dxhat - xhat * proj)

        @pl.when(i == 0)
        def _():
            dg_ref[...] = jnp.zeros_like(dg_ref)
            loss_ref[...] = jnp.zeros_like(loss_ref)

        dg_ref[...] += jnp.sum(dy * xhat, axis=0, keepdims=True)
        part = 0.5 * jnp.sum(jnp.mean(err * err, axis=1, keepdims=True), axis=0, keepdims=True)
        loss_ref[...] += jnp.broadcast_to(part, loss_ref.shape)

    row = pl.BlockSpec((tr, d), lambda i: (i, 0))
    vec = pl.BlockSpec((1, d), lambda i: (0, 0))
    return pl.pallas_call(
        body, name=name, grid=(s // tr,),
        in_specs=[row, vec, row],
        out_specs=[row, vec, pl.BlockSpec((1, 128), lambda i: (0, 0))],
        out_shape=[jax.ShapeDtypeStruct((s, d), F32), jax.ShapeDtypeStruct((1, d), F32),
                   jax.ShapeDtypeStruct((1, 128), F32)],
        compiler_params=_cparams(("arbitrary",)),
    )(x, g, target)


def _ew(body, ins, in_blocks, outs, out_blocks, grid, *, name, ride=None):
    return _call(body, ins, name=name, grid=grid,
                 in_specs=[pl.BlockSpec(bs, im) for bs, im in in_blocks],
                 out_specs=[pl.BlockSpec(bs, im) for bs, im in out_blocks],
                 out_shape=outs, sem=("parallel",) * len(grid), ride=ride)


def _gate_merge_fwd(gates, o_sb, o_ca, *, name, ride=None):
    s, d = o_sb.shape
    tr, tc = _row_tile(s), _pick(d, (1024, 512, 256, 128))
    nc = d // tc

    def body(gs_ref, gc_ref, os_ref, oc_ref, m_ref):
        m = _sigmoid(gs_ref[...]) * os_ref[...] + _sigmoid(gc_ref[...]) * oc_ref[...]
        m_ref[...] = m.astype(m_ref.dtype)

    blk = ((tr, tc), lambda i, j: (i, j))
    return _ew(body, [gates, gates, o_sb, o_ca],
               [blk, ((tr, tc), lambda i, j: (i, j + nc)), blk, blk],
               [jax.ShapeDtypeStruct((s, d), BF16)], [blk], (s // tr, nc), name=name, ride=ride)[0]


def _gate_merge_bwd(dmerged, gates, o_sb, o_ca, *, name):
    s, d = o_sb.shape
    tr, tc = _row_tile(s), _pick(d, (1024, 512, 256, 128))
    nc = d // tc

    def body(dm_ref, gs_ref, gc_ref, os_ref, oc_ref, dgs_ref, dgc_ref, dos_ref, doc_ref):
        dm = dm_ref[...]
        ss = _sigmoid(gs_ref[...])
        sc = _sigmoid(gc_ref[...])
        dgs_ref[...] = (dm * os_ref[...] * ss * (1.0 - ss)).astype(dgs_ref.dtype)
        dgc_ref[...] = (dm * oc_ref[...] * sc * (1.0 - sc)).astype(dgc_ref.dtype)
        dos_ref[...] = (dm * ss).astype(dos_ref.dtype)
        doc_ref[...] = (dm * sc).astype(doc_ref.dtype)

    blk = ((tr, tc), lambda i, j: (i, j))
    sd = jax.ShapeDtypeStruct((s, d), BF16)
    return _ew(body, [dmerged, gates, gates, o_sb, o_ca],
               [blk, blk, ((tr, tc), lambda i, j: (i, j + nc)), blk, blk],
               [sd, sd, sd, sd], [blk, blk, blk, blk], (s // tr, nc), name=name)


def _swiglu_fwd(gu, *, name, ride=None):
    s, f2 = gu.shape
    f = f2 // 2
    tr, tc = 128, _pick(f, (512, 256, 128))

    def body(gu_ref, a_ref):
        for at in range(0, f, tc):
            gv = gu_ref[:, at:at + tc]
            a_ref[:, at:at + tc] = (gv * _sigmoid(gv) * gu_ref[:, f + at:f + at + tc]).astype(a_ref.dtype)

    row = lambda i: (i, 0)
    return _ew(body, [gu], [((tr, f2), row)], [jax.ShapeDtypeStruct((s, f), BF16)], [((tr, f), row)],
               (s // tr,), name=name, ride=ride)[0]


def _swiglu_bwd(dact, gu, *, name):
    s, f2 = gu.shape
    f = f2 // 2
    tr, tc = 128, _pick(f, (512, 256, 128))

    def body(da_ref, gu_ref, o_ref):
        for at in range(0, f, tc):
            da = da_ref[:, at:at + tc]
            gv = gu_ref[:, at:at + tc]
            sg = _sigmoid(gv)
            o_ref[:, at:at + tc] = (da * gu_ref[:, f + at:f + at + tc] * sg * (1.0 + gv * (1.0 - sg))).astype(o_ref.dtype)
            o_ref[:, f + at:f + at + tc] = (da * gv * sg).astype(o_ref.dtype)

    row = lambda i: (i, 0)
    return _ew(body, [dact, gu], [((tr, f), row), ((tr, f2), row)], [jax.ShapeDtypeStruct((s, f2), BF16)],
               [((tr, f2), row)], (s // tr,), name=name)[0]


def _concat_cols(parts, *, name):
    s = parts[0].shape[0]
    widths = [p.shape[1] for p in parts]
    tr = 256

    def body(*refs):
        o_ref, at = refs[-1], 0
        for p_ref, width in zip(refs, widths):
            o_ref[:, at:at + width] = p_ref[...]
            at += width

    row = lambda i: (i, 0)
    return _ew(body, list(parts), [((tr, width), row) for width in widths],
               [jax.ShapeDtypeStruct((s, sum(widths)), parts[0].dtype)], [((tr, sum(widths)), row)],
               (s // tr,), name=name)[0]


def _ple_fwd(x, t, pe, *, name):
    s, d = x.shape
    tr, tc = _row_tile(s), _pick(d, (1024, 512, 256, 128))

    def body(x_ref, t_ref, p_ref, o_ref):
        o_ref[...] = x_ref[...] + _sigmoid(t_ref[...]) * p_ref[...]

    blk = ((tr, tc), lambda i, j: (i, j))
    return _ew(body, [x, t, pe], [blk, blk, blk],
               [jax.ShapeDtypeStruct((s, d), F32)], [blk], (s // tr, d // tc), name=name)[0]


def _ple_bwd(dx, t, pe, *, name):
    s, d = dx.shape
    tr, tc = _row_tile(s), _pick(d, (1024, 512, 256, 128))

    def body(dx_ref, t_ref, p_ref, dt_ref, dp_ref):
        dxv = dx_ref[...]
        sg = _sigmoid(t_ref[...])
        dt_ref[...] = (dxv * p_ref[...] * sg * (1.0 - sg)).astype(dt_ref.dtype)
        dp_ref[...] = (dxv * sg).astype(dp_ref.dtype)

    blk = ((tr, tc), lambda i, j: (i, j))
    sd = jax.ShapeDtypeStruct((s, d), BF16)
    return _ew(body, [dx, t, pe], [blk, blk, blk], [sd, sd], [blk, blk], (s // tr, d // tc), name=name)


def _sb_tri(later):
    row = lax.broadcasted_iota(jnp.int32, (SB_BLOCK, SB_BLOCK), 0)
    col = lax.broadcasted_iota(jnp.int32, (SB_BLOCK, SB_BLOCK), 1)
    tri = (row > col) if later else (row < col)
    return jnp.concatenate([tri.astype(BF16), jnp.ones((SB_BLOCK, SB_BLOCK), BF16)], axis=1)


def _sb_valid(i, j):
    qi = i * SB_ROWS + lax.broadcasted_iota(jnp.int32, (SB_ROWS, SB_KEYS), 0)
    ki = j * SB_KEYS + lax.broadcasted_iota(jnp.int32, (SB_ROWS, SB_KEYS), 1)
    return ki < qi


def _sb_scan(v, tri, run, later):
    hi = v.astype(BF16)
    lo = (v - hi.astype(F32)).astype(BF16)
    outs = [None] * SB_GROUPS
    for b in (reversed(range(SB_GROUPS)) if later else range(SB_GROUPS)):
        cols = slice(b * SB_BLOCK, (b + 1) * SB_BLOCK)
        r = _dot_nn(hi[:, cols], tri) + _dot_nn(lo[:, cols], tri)
        outs[b] = r[:, :SB_BLOCK] + run
        run = run + r[:, SB_BLOCK:]
    return jnp.concatenate(outs, axis=1), run


def _sb_scores(q, kj, scale, valid):
    z = _dot_nt(q, kj) * scale
    t = jnp.log(1.0 + jnp.exp(-jnp.abs(z)))
    return jnp.minimum(z, 0.0) - t, jnp.where(valid, -jnp.maximum(z, 0.0) - t, 0.0)


def _sb_specs(h_count, s, col0):
    q_spec = pl.BlockSpec((SB_ROWS, HEAD_DIM), lambda h, i: (i, col0 + h))
    k_spec = pl.BlockSpec((s, HEAD_DIM), lambda h, i: (0, col0 + h_count + h))
    v_spec = pl.BlockSpec((s, HEAD_DIM), lambda h, i: (0, col0 + 2 * h_count + h))
    return q_spec, k_spec, v_spec


def _sb_fwd(qkv, n_heads, col0, *, name, ride=None):
    s = qkv.shape[0]
    nq = s // SB_ROWS
    scale = HEAD_DIM ** -0.5

    def body(q_ref, k_ref, v_ref, o_ref):
        i = pl.program_id(1)
        steps = ((i + 1) * SB_ROWS - 1) // SB_KEYS + 1
        q = q_ref[...]
        tri = _sb_tri(later=True)

        def step(jj, carry):
            run, acc = carry
            j = steps - 1 - jj
            off = pl.multiple_of(j * SB_KEYS, SB_KEYS)
            valid = _sb_valid(i, j)
            ls, lk = _sb_scores(q, k_ref[pl.ds(off, SB_KEYS), :], scale, valid)
            between, run = _sb_scan(lk, tri, run, later=True)
            a = jnp.where(valid, jnp.exp(ls + between), 0.0)
            return run, acc + _dot_nn(a.astype(BF16), v_ref[pl.ds(off, SB_KEYS), :])

        init = (jnp.zeros((SB_ROWS, SB_BLOCK), F32), jnp.zeros((SB_ROWS, HEAD_DIM), F32))
        _, acc = lax.fori_loop(0, steps, step, init)
        o_ref[...] = acc.astype(o_ref.dtype)

    q_spec, k_spec, v_spec = _sb_specs(n_heads, s, col0)
    return _call(
        body, [qkv, qkv, qkv], name=name, grid=(n_heads, nq),
        in_specs=[q_spec, k_spec, v_spec],
        out_specs=[pl.BlockSpec((SB_ROWS, HEAD_DIM), lambda h, i: (i, h))],
        out_shape=[jax.ShapeDtypeStruct((s, n_heads * HEAD_DIM), BF16)],
        sem=("parallel", "arbitrary"), ride=ride)[0]


def _sb_bwd(qkv, dy, n_heads, col0, *, name, ride=None):
    s = qkv.shape[0]
    nq = s // SB_ROWS
    scale = HEAD_DIM ** -0.5

    def body(q_ref, k_ref, v_ref, dy_ref, dq_ref, dk_ref, dv_ref, e_scr, sg_scr, dk_acc, dv_acc):
        i = pl.program_id(1)
        steps = ((i + 1) * SB_ROWS - 1) // SB_KEYS + 1
        q = q_ref[...]
        dyv = dy_ref[...]

        @pl.when(i == 0)
        def _():
            dk_acc[...] = jnp.zeros_like(dk_acc)
            dv_acc[...] = jnp.zeros_like(dv_acc)

        tri_later = _sb_tri(later=True)

        def pass1(jj, run):
            j = steps - 1 - jj
            off = pl.multiple_of(j * SB_KEYS, SB_KEYS)
            valid = _sb_valid(i, j)
            ls, lk = _sb_scores(q, k_ref[pl.ds(off, SB_KEYS), :], scale, valid)
            between, run = _sb_scan(lk, tri_later, run, later=True)
            a = jnp.where(valid, jnp.exp(ls + between), 0.0)
            e_scr[j] = a * _dot_nt(dyv, v_ref[pl.ds(off, SB_KEYS), :])
            sg_scr[j] = jnp.exp(ls)
            dv_acc[pl.ds(off, SB_KEYS), :] += _dot_tn(a.astype(BF16), dyv)
            return run

        lax.fori_loop(0, steps, pass1, jnp.zeros((SB_ROWS, SB_BLOCK), F32))

        tri_earlier = _sb_tri(later=False)

        def pass2(j, carry):
            run, dq = carry
            off = pl.multiple_of(j * SB_KEYS, SB_KEYS)
            kj = k_ref[pl.ds(off, SB_KEYS), :]
            sg = sg_scr[j]
            e = e_scr[j]
            before, run = _sb_scan(e, tri_earlier, run, later=False)
            dz = jnp.where(_sb_valid(i, j), e * (1.0 - sg) - sg * before, 0.0) * scale
            dzb = dz.astype(BF16)
            dk_acc[pl.ds(off, SB_KEYS), :] += _dot_tn(dzb, q)
            return run, dq + _dot_nn(dzb, kj)

        init = (jnp.zeros((SB_ROWS, SB_BLOCK), F32), jnp.zeros((SB_ROWS, HEAD_DIM), F32))
        _, dq = lax.fori_loop(0, steps, pass2, init)
        dq_ref[...] = dq.astype(dq_ref.dtype)

        @pl.when(i == nq - 1)
        def _():
            dk_ref[...] = dk_acc[...].astype(dk_ref.dtype)
            dv_ref[...] = dv_acc[...].astype(dv_ref.dtype)

    q_spec, k_spec, v_spec = _sb_specs(n_heads, s, col0)
    blk = pl.BlockSpec((SB_ROWS, HEAD_DIM), lambda h, i: (i, h))
    full = pl.BlockSpec((s, HEAD_DIM), lambda h, i: (0, h))
    sd = jax.ShapeDtypeStruct((s, n_heads * HEAD_DIM), BF16)
    return _call(
        body, [qkv, qkv, qkv, dy], name=name, grid=(n_heads, nq),
        in_specs=[q_spec, k_spec, v_spec, blk],
        out_specs=[blk, full, full],
        out_shape=[sd, sd, sd],
        scratch_shapes=[pltpu.VMEM((s // SB_KEYS, SB_ROWS, SB_KEYS), F32), pltpu.VMEM((s // SB_KEYS, SB_ROWS, SB_KEYS), F32),
                        pltpu.VMEM((s, HEAD_DIM), F32), pltpu.VMEM((s, HEAD_DIM), F32)],
        sem=("parallel", "arbitrary"), ride=ride)


def _band_bias(rel_bias):
    h = rel_bias.shape[0]
    width = BAND + CHUNK
    first = width - 1 - N_REL
    line = jnp.concatenate([jnp.broadcast_to(rel_bias[:, :1], (h, first)), rel_bias], axis=1)
    tiled = jnp.broadcast_to(line[:, None, :], (h, CHUNK, width - 1)).reshape(h, CHUNK * (width - 1))
    skew = jnp.pad(tiled, ((0, 0), (0, CHUNK))).reshape(h, CHUNK, width)[:, ::-1, :BAND]
    seen = jnp.arange(BAND) >= CHUNK
    return jnp.where(seen[None, None, :], skew, NEG)


def _band_bias_grad(dbias):
    h = dbias.shape[0]
    width = BAND + CHUNK
    flipped = jnp.pad(dbias[:, ::-1, :], ((0, 0), (0, 0), (0, CHUNK)))
    skew = flipped.reshape(h, CHUNK * width)[:, :CHUNK * (width - 1)].reshape(h, CHUNK, width - 1)
    diag = jnp.sum(skew, axis=1)
    first = width - 1 - N_REL
    clipped = jnp.sum(diag[:, :first + 1], axis=1, keepdims=True)
    return jnp.concatenate([clipped, diag[:, first + 1:]], axis=1)


def _group_bias(band):
    return jnp.concatenate([jnp.pad(band, ((0, 0), (0, 0), ((u + 1) * CHUNK, (CA_PER_STEP - 1 - u) * CHUNK)),
                                    constant_values=NEG) for u in range(CA_PER_STEP)], axis=1)


def _group_bias_grad(dgroup):
    return sum(dgroup[:, u * CHUNK:(u + 1) * CHUNK, (u + 1) * CHUNK:(u + 1) * CHUNK + BAND] for u in range(CA_PER_STEP))


def _ca_load_padded(k_ref, v_ref, kp, vp, s):
    kp[pl.ds(0, CA_PAD), :] = jnp.zeros((CA_PAD, HEAD_DIM), kp.dtype)
    vp[pl.ds(0, CA_PAD), :] = jnp.zeros((CA_PAD, HEAD_DIM), vp.dtype)
    kp[pl.ds(CA_PAD, s), :] = k_ref[...]
    vp[pl.ds(CA_PAD, s), :] = v_ref[...]


def _ca_weights(q, kb, bias, off, scale):
    z = _dot_nt(q, kb) * scale + bias
    pos = off + lax.broadcasted_iota(jnp.int32, (CA_ROWS, CA_BAND), 1)
    z = jnp.where(pos >= CA_PAD, z, NEG)
    p = jnp.exp(z - jnp.max(z, axis=1, keepdims=True))
    return p / jnp.sum(p, axis=1, keepdims=True)


def _ca_specs(h_count, s, col0):
    q_spec = pl.BlockSpec((CA_ROWS, HEAD_DIM), lambda h, c: (c, col0 + h))
    k_spec = pl.BlockSpec((s, HEAD_DIM), lambda h, c: (0, col0 + h_count + h))
    v_spec = pl.BlockSpec((s, HEAD_DIM), lambda h, c: (0, col0 + 2 * h_count + h))
    b_spec = pl.BlockSpec((1, CA_ROWS, CA_BAND), lambda h, c: (h, 0, 0))
    return q_spec, k_spec, v_spec, b_spec


def _ca_fwd(qkv, bias, n_heads, col0, *, name, ride=None):
    s = qkv.shape[0]
    nc = s // CA_ROWS
    scale = HEAD_DIM ** -0.5

    def body(q_ref, k_ref, v_ref, b_ref, o_ref, kp, vp):
        c = pl.program_id(1)

        @pl.when(c == 0)
        def _():
            _ca_load_padded(k_ref, v_ref, kp, vp, s)

        off = pl.multiple_of(c * CA_ROWS, CA_ROWS)
        w = _ca_weights(q_ref[...], kp[pl.ds(off, CA_BAND), :], b_ref[0], off, scale)
        o_ref[...] = _dot_nn(w.astype(BF16), vp[pl.ds(off, CA_BAND), :]).astype(o_ref.dtype)

    q_spec, k_spec, v_spec, b_spec = _ca_specs(n_heads, s, col0)
    return _call(
        body, [qkv, qkv, qkv, bias], name=name, grid=(n_heads, nc),
        in_specs=[q_spec, k_spec, v_spec, b_spec],
        out_specs=[pl.BlockSpec((CA_ROWS, HEAD_DIM), lambda h, c: (c, h))],
        out_shape=[jax.ShapeDtypeStruct((s, n_heads * HEAD_DIM), BF16)],
        scratch_shapes=[pltpu.VMEM((s + CA_PAD, HEAD_DIM), BF16), pltpu.VMEM((s + CA_PAD, HEAD_DIM), BF16)],
        sem=("parallel", "arbitrary"), ride=ride)[0]


def _ca_bwd(qkv, bias, dy, n_heads, col0, *, name, ride=None):
    s = qkv.shape[0]
    nc = s // CA_ROWS
    scale = HEAD_DIM ** -0.5

    def body(q_ref, k_ref, v_ref, b_ref, dy_ref, dq_ref, dk_ref, dv_ref, db_ref, kp, vp, dkp, dvp):
        c = pl.program_id(1)

        @pl.when(c == 0)
        def _():
            _ca_load_padded(k_ref, v_ref, kp, vp, s)
            dkp[...] = jnp.zeros_like(dkp)
            dvp[...] = jnp.zeros_like(dvp)
            db_ref[...] = jnp.zeros_like(db_ref)

        off = pl.multiple_of(c * CA_ROWS, CA_ROWS)
        band = pl.ds(off, CA_BAND)
        q = q_ref[...]
        dyv = dy_ref[...]
        kb = kp[band, :]
        w = _ca_weights(q, kb, b_ref[0], off, scale)
        dw = _dot_nt(dyv, vp[band, :])
        dvp[band, :] += _dot_tn(w.astype(BF16), dyv)
        dz = w * (dw - jnp.sum(w * dw, axis=1, keepdims=True))
        db_ref[0] += dz
        dzs = (dz * scale).astype(BF16)
        dq_ref[...] = _dot_nn(dzs, kb).astype(dq_ref.dtype)
        dkp[band, :] += _dot_tn(dzs, q)

        @pl.when(c == nc - 1)
        def _():
            dk_ref[...] = dkp[pl.ds(CA_PAD, s), :].astype(dk_ref.dtype)
            dv_ref[...] = dvp[pl.ds(CA_PAD, s), :].astype(dv_ref.dtype)

    q_spec, k_spec, v_spec, b_spec = _ca_specs(n_heads, s, col0)
    blk = pl.BlockSpec((CA_ROWS, HEAD_DIM), lambda h, c: (c, h))
    full = pl.BlockSpec((s, HEAD_DIM), lambda h, c: (0, h))
    sd = jax.ShapeDtypeStruct((s, n_heads * HEAD_DIM), BF16)
    return _call(
        body, [qkv, qkv, qkv, bias, dy], name=name, grid=(n_heads, nc),
        in_specs=[q_spec, k_spec, v_spec, b_spec, blk],
        out_specs=[blk, full, full, b_spec],
        out_shape=[sd, sd, sd, jax.ShapeDtypeStruct((n_heads, CA_ROWS, CA_BAND), F32)],
        scratch_shapes=[pltpu.VMEM((s + CA_PAD, HEAD_DIM), BF16), pltpu.VMEM((s + CA_PAD, HEAD_DIM), BF16),
                        pltpu.VMEM((s + CA_PAD, HEAD_DIM), F32), pltpu.VMEM((s + CA_PAD, HEAD_DIM), F32)],
        sem=("parallel", "arbitrary"), ride=ride)


EARLY = ("w_sb_out", "w_ca_out", "w_mix_out")


def _step(x, p, target, small, comm):
    w = comm.w
    d = x.shape[1]
    n_sb = w["w_sb_out"].shape[0] // HEAD_DIM
    n_ca = w["w_ca_out"].shape[0] // HEAD_DIM
    qkv_cols = 3 * HEAD_DIM * (n_sb + n_ca)
    ca_col0 = 3 * n_sb
    both = (F32, BF16)

    h1 = _rms_fwd(x, small["g_mix"], name="rms_mix")
    ffn, ple = ("w_ffn_in",), ("w_ple_gate", "w_ple_in")
    qkv = _mm(h1, w["w_in"], "nn", (BF16,), name="proj_qkv", n=qkv_cols, ride=comm.gather(EARLY, "near"))
    gates = _mm(h1, w["w_in"], "nn", (F32,), name="proj_gates", n=2 * d, b_col_off=qkv_cols,
                ride=comm.gather(ffn, "near", comm.gather(EARLY, "far"), (0, 8)))
    bias = _group_bias(_band_bias(small["rel_bias"]))
    y_sb = _sb_fwd(qkv, n_sb, 0, name="sb_fwd", ride=comm.gather(ffn, "near", comm.gather(EARLY, "pair"), (1, 8, 7)))
    y_ca = _ca_fwd(qkv, bias, n_ca, ca_col0, name="ca_fwd", ride=comm.gather(ffn, "far"))
    out = ("w_ffn_out",)
    o_sb = _mm(y_sb, w["w_sb_out"], "nn", (F32,), name="sb_out", ride=comm.gather(out, "near", part=(0, 4)))
    o_ca = _mm(y_ca, w["w_ca_out"], "nn", (F32,), name="ca_out", ride=comm.gather(out, "near", part=(1, 4)))
    merged = _gate_merge_fwd(gates, o_sb, o_ca, name="gate_merge",
                             ride=comm.gather(out, "near", comm.gather(ffn, "pair"), (2, 4)))
    x1 = _mm(merged, w["w_mix_out"], "nn", (F32,), name="mix_out", resid=x, ride=comm.gather(out, "near", part=(3, 4)))
    h2 = _rms_fwd(x1, small["g_ffn"], name="rms_ffn")
    gu = _mm(h2, w["w_ffn_in"], "nn", (F32,), name="ffn_in", ride=comm.gather(ple, "near", comm.gather(out, "far")))
    act = _swiglu_fwd(gu, name="swiglu", ride=comm.gather(ple, "far", comm.gather(out, "pair")))
    x2 = _mm(act, w["w_ffn_out"], "nn", (F32,), name="ffn_out", resid=x1, ride=comm.gather(ple, "pair"))
    h3 = _rms_fwd(x2, small["g_ple"], name="rms_ple")
    t = _mm(h3, w["w_ple_gate"], "nn", (F32,), name="ple_gate")
    pe = _mm(p, w["w_ple_in"], "nn", (F32,), name="ple_in")
    x3 = _ple_fwd(x2, t, pe, name="ple_add")

    gs = {}
    dx3, gs["g_final"], loss = _final_loss(x3, small["g_final"], target, name="final_loss")
    dt, dpe = _ple_bwd(dx3, t, pe, name="ple_bwd")
    comm.grad("w_ple_in", *_mm(p, dpe, "tn", both, name="dw_ple_in"))
    comm.grad("w_ple_gate", *_mm(h3, dt, "tn", both, name="dw_ple_gate"))
    ple = ("w_ple_in", "w_ple_gate")
    dh3 = _mm(dt, w["w_ple_gate"], "nt", (F32,), name="dh_ple", ride=comm.pair(ple))
    dx2, dx2_16, gs["g_ple"] = _rms_bwd(x2, small["g_ple"], dh3, dx3, name="rms_ple_bwd")
    comm.add(ple)
    comm.grad("w_ffn_out", *_mm(act, dx2_16, "tn", both, name="dw_ffn_out", ride=comm.chips(ple)))
    dact = _mm(dx2_16, w["w_ffn_out"], "nt", (F32,), name="dact", ride=comm.pair(("w_ffn_out",)))
    dgu = _swiglu_bwd(dact, gu, name="swiglu_bwd")
    comm.sum(ple)
    comm.add(("w_ffn_out",))
    comm.grad("w_ffn_in", *_mm(h2, dgu, "tn", both, name="dw_ffn_in",
                               ride=comm.share(ple, comm.chips(("w_ffn_out",)))))
    dh2 = _mm(dgu, w["w_ffn_in"], "nt", (F32,), name="dh_ffn", ride=comm.pair(("w_ffn_in",)))
    dx1, dx1_16, gs["g_ffn"] = _rms_bwd(x1, small["g_ffn"], dh2, dx2, name="rms_ffn_bwd")
    comm.add(("w_ffn_in",))
    comm.sum(("w_ffn_out",))
    comm.grad("w_mix_out", *_mm(merged, dx1_16, "tn", both, name="dw_mix_out", ride=comm.share(("w_ffn_out",))))
    dmerged = _mm(dx1_16, w["w_mix_out"], "nt", (F32,), name="dmerged", ride=comm.pair(("w_mix_out",)))
    dg_sb, dg_ca, do_sb, do_ca = _gate_merge_bwd(dmerged, gates, o_sb, o_ca, name="gate_merge_bwd")
    comm.add(("w_mix_out",))
    comm.grad("w_sb_out", *_mm(y_sb, do_sb, "tn", both, name="dw_sb_out"))
    comm.grad("w_ca_out", *_mm(y_ca, do_ca, "tn", both, name="dw_ca_out"))
    outs = ("w_sb_out", "w_ca_out")
    dy_sb = _mm(do_sb, w["w_sb_out"], "nt", (BF16,), name="dy_sb", ride=comm.pair(outs))
    dy_ca = _mm(do_ca, w["w_ca_out"], "nt", (BF16,), name="dy_ca")
    comm.add(outs)
    dq_sb, dk_sb, dv_sb = _sb_bwd(qkv, dy_sb, n_sb, 0, name="sb_bwd", ride=comm.chips(("w_ffn_in",)))
    comm.sum(("w_ffn_in",))
    late = ("w_mix_out",) + outs
    dq_ca, dk_ca, dv_ca, dbias = _ca_bwd(qkv, bias, dy_ca, n_ca, ca_col0, name="ca_bwd",
                                         ride=comm.chips(late, comm.share(("w_ffn_in",))))
    comm.sum(late)
    gs["rel_bias"] = _band_bias_grad(_group_bias_grad(dbias))
    dproj = _concat_cols([dq_sb, dk_sb, dv_sb, dq_ca, dk_ca, dv_ca, dg_sb, dg_ca], name="dproj")
    comm.grad("w_in", *_mm(h1, dproj, "tn", both, name="dw_in", ride=comm.share(late)))
    half = x.shape[0] // 2
    dh1 = _mm(dproj, w["w_in"], "nt", (F32,), name="dh_mix_top", rows=(0, half), ride=comm.pair(("w_in",)))
    comm.add(("w_in",))
    dh1 = _mm(dproj, w["w_in"], "nt", (F32,), name="dh_mix_bottom", rows=(half, half), onto=(dh1,),
              ride=comm.tail(TAIL_SECOND))
    grad_x, _, gs["g_mix"] = _rms_bwd(x, small["g_mix"], dh1, dx1, name="rms_mix_bwd", ride=comm.tail(TAIL_FIRST))
    return loss, grad_x, gs


def _position():
    x, y, c = lax.axis_index("x"), lax.axis_index("y"), lax.axis_index("c")
    chips = [(1 - x, y), (x, 1 - y), (1 - x, 1 - y)]
    return x, y, c, chips


def _aligned(v, m):
    return v if isinstance(v, int) else pl.multiple_of(v, m)


def _piece_dims(shape, axis):
    k, n = shape
    return (k // 2, n // N_CHIPS) if axis == 1 else (k // N_CHIPS // 2, n)


def _piece(ref, shape, axis, j, h, part=(0, 1)):
    pr, pc = _piece_dims(shape, axis)
    nr = pr // part[1] * (part[2] if len(part) > 2 else 1)
    r0 = part[0] * (pr // part[1])
    if axis == 1:
        return ref.at[pl.ds(_aligned(h * pr + r0, 16), nr), pl.ds(_aligned(j * pc, 128), pc)]
    return ref.at[pl.ds(_aligned((2 * j + h) * pr + r0, 16), nr), :]


def _shard_half(ref, h):
    rows = ref.shape[0] // 2
    return ref.at[pl.ds(_aligned(h * rows, 16), rows), :]


def _remote(src, dst, send_sems, recv_sems, k, to):
    return pltpu.make_async_remote_copy(src_ref=src, dst_ref=dst, send_sem=send_sems.at[k],
                                        recv_sem=recv_sems.at[k], device_id=to, device_id_type=MESH)


def _prefetch_call(body, scalars, ins, in_specs, out_shape, out_specs, grid, *, name, ride=None):
    single = not isinstance(out_shape, (list, tuple))
    outs = _call(body, ins, name=name, grid=grid, in_specs=in_specs,
                 out_specs=[out_specs] if single else out_specs, out_shape=[out_shape] if single else out_shape,
                 sem=("parallel",) * len(grid), ride=ride, scalars=scalars)
    return outs[0] if single else outs


def _slab_tiles(pr, pc):
    tc = pc if pc <= 4096 else _pick(pc, (2048, 1024, 512, 256, 128))
    tr = next(t for t in (1024, 512, 256, 128, 64, 32, 16) if pr % t == 0 and t * tc <= 512 * 1024)
    return tr, tc


def _cast_place(w, axis, pos, *, name, ride=None):
    ks, ns = w.shape
    shape = (ks, ns * N_CHIPS) if axis == 1 else (ks * N_CHIPS, ns)
    tr, tc = _slab_tiles(ks, ns)
    nr, nc = ks // tr, ns // tc

    def body(pos_ref, w_ref, o_ref):
        o_ref[...] = w_ref[...].astype(o_ref.dtype)

    if axis == 1:
        out_map = lambda i, j, pos_ref: (i, pos_ref[0] * nc + j)
    else:
        out_map = lambda i, j, pos_ref: (pos_ref[0] * nr + i, j)
    return _prefetch_call(body, pos, [w], [pl.BlockSpec((tr, tc), lambda i, j, pos_ref: (i, j))],
                          jax.ShapeDtypeStruct(shape, BF16), pl.BlockSpec((tr, tc), out_map), (nr, nc), name=name, ride=ride)


def _run(ride, *, name):
    if ride is None:
        return

    def body(o_ref):
        o_ref[...] = jnp.zeros_like(o_ref)

    _call(body, [], name=name, grid=(1,), in_specs=[], out_specs=[pl.BlockSpec((8, 128), lambda i: (0, 0))],
          out_shape=[jax.ShapeDtypeStruct((8, 128), F32)], ride=ride)


def _ride_gather(ride, w, n, axis, stage, part=(0, 1)):
    shape = w[n].shape
    piece = functools.partial(_piece, shape=shape, axis=axis)
    span = part[2] if len(part) > 2 else 1
    halves = [(2 * part[0] + t * span, 2 * part[1], span) for t in range(2)]

    def copies(ins, outs, send_sems, recv_sems, arriving):
        x, y, c, chips = _position()
        me, (xn, yn, dn) = 2 * x + y, [2 * px + py for px, py in chips]
        if stage == "near":
            plan = [(me, c, part, (1 - x, y, c), xn, c, part), (me, c, part, (x, 1 - y, c), yn, c, part)]
        elif stage == "far":
            plan = [(yn, c, halves[1], (1 - x, y, c), dn, c, halves[1]), (xn, c, halves[0], (x, 1 - y, c), dn, c, halves[0])]
        else:
            plan = [(j, c, part, (x, y, 1 - c), j, 1 - c, part) for j in (xn, yn, dn)]
        out = []
        for k, (chip, h, rows, to, from_chip, from_h, from_rows) in enumerate(plan):
            if arriving:
                lands = piece(outs[0], j=from_chip, h=from_h, part=from_rows)
                out.append(_remote(lands, lands, send_sems, recv_sems, k, to))
            else:
                out.append(_remote(piece(ins[0], j=chip, h=h, part=rows), piece(outs[0], j=chip, h=h, part=rows),
                                   send_sems, recv_sems, k, to))
        return out

    def start(*refs):
        for cp in copies(*refs, arriving=False):
            cp.start()

    def finish(*refs):
        for cp in copies(*refs, arriving=True):
            cp.wait_recv()
        for cp in copies(*refs, arriving=False):
            cp.wait_send()

    ride.add([w[n]], [jax.ShapeDtypeStruct(shape, w[n].dtype)], {0: 0}, 3, start, finish,
             lambda outs: w.__setitem__(n, outs[0]))


def _ride_pair(ride, st, axis):
    shape = st["g16"].shape
    pr, pc = _piece_dims(shape, axis)

    def copies(ins, outs, send_sems, recv_sems):
        x, y, c, _ = _position()
        return [_remote(_piece(ins[0], shape, axis, j, 1 - c), outs[0].at[j], send_sems, recv_sems, j, (x, y, 1 - c))
                for j in range(N_CHIPS)]

    def start(*refs):
        for cp in copies(*refs):
            cp.start()

    def finish(*refs):
        for cp in copies(*refs):
            cp.wait()

    ride.add([st["g16"]], [jax.ShapeDtypeStruct((N_CHIPS, pr, pc), BF16)], {}, N_CHIPS, start, finish,
             lambda outs: st.__setitem__("sib", outs[0]))


def _ride_chips(ride, st, rows=None):
    _, pr, pc = st["s16"].shape
    r0, nr = (0, pr) if rows is None else rows

    def copies(ins, outs, send_sems, recv_sems):
        x, y, c, chips = _position()
        return [_remote(ins[0].at[2 * px + py, pl.ds(r0, nr), :], outs[0].at[k, pl.ds(r0, nr), :],
                        send_sems, recv_sems, k, (px, py, c)) for k, (px, py) in enumerate(chips)]

    def start(*refs):
        for cp in copies(*refs):
            cp.start()

    def finish(*refs):
        for cp in copies(*refs):
            cp.wait()

    ins, aliases = ([st["s16"], st["recv"]], {1: 0}) if "recv" in st else ([st["s16"]], {})
    ride.add(ins, [jax.ShapeDtypeStruct((3, pr, pc), BF16)], aliases, 3, start, finish,
             lambda outs: st.__setitem__("recv", outs[0]))


def _ride_share(ride, st):
    def sent(ins, outs, send_sems, recv_sems):
        x, y, c, _ = _position()
        return _remote(_shard_half(ins[0], c), _shard_half(outs[0], c), send_sems, recv_sems, 0, (x, y, 1 - c))

    def landed(ins, outs, send_sems, recv_sems):
        x, y, c, _ = _position()
        other = _shard_half(outs[0], 1 - c)
        return _remote(other, other, send_sems, recv_sems, 0, (x, y, 1 - c))

    def start(*refs):
        sent(*refs).start()

    def finish(*refs):
        landed(*refs).wait_recv()
        sent(*refs).wait_send()

    ride.add([st["shard"]], [jax.ShapeDtypeStruct(st["shard"].shape, F32)], {0: 0}, 1, start, finish,
             lambda outs: st.__setitem__("g", outs[0]))


def _piece_block(axis, nr, nc, chip):
    if axis == 1:
        return lambda *a: ((a[-1][1] * nr + a[-3]), (a[0] if chip is None else chip(a[-1])) * nc + a[-2])
    return lambda *a: ((2 * (a[0] if chip is None else chip(a[-1])) + a[-1][1]) * nr + a[-3], a[-2])


def _pair_add(g32, sib, axis, pos, *, name):
    _, pr, pc = sib.shape
    tr, tc = _slab_tiles(pr, pc)
    nr, nc = pr // tr, pc // tc

    def body(pos_ref, g_ref, b_ref, o16_ref):
        o16_ref[0] = (g_ref[...] + b_ref[0].astype(F32)).astype(o16_ref.dtype)

    blk = pl.BlockSpec((1, tr, tc), lambda j, i, k, pos_ref: (j, i, k))
    return _prefetch_call(body, pos, [g32, sib], [pl.BlockSpec((tr, tc), _piece_block(axis, nr, nc, None)), blk],
                          jax.ShapeDtypeStruct(sib.shape, BF16), blk, (N_CHIPS, nr, nc), name=name)


def _chip_sum(g32, sib, recv, axis, pos, *, name):
    _, pr, pc = sib.shape
    tr, tc = _slab_tiles(pr, pc)
    nr, nc = pr // tr, pc // tc

    def body(pos_ref, g_ref, b_ref, r_ref, o_ref):
        pair = g_ref[...] + b_ref[0].astype(F32)
        o_ref[...] = ((pair + r_ref[0].astype(F32)) + r_ref[1].astype(F32)) + r_ref[2].astype(F32)

    return _prefetch_call(
        body, pos, [g32, sib, recv],
        [pl.BlockSpec((tr, tc), _piece_block(axis, nr, nc, lambda pos_ref: pos_ref[0])),
         pl.BlockSpec((1, tr, tc), lambda i, k, pos_ref: (pos_ref[0], i, k)),
         pl.BlockSpec((3, tr, tc), lambda i, k, pos_ref: (0, i, k))],
        jax.ShapeDtypeStruct((2 * pr, pc), F32),
        pl.BlockSpec((tr, tc), lambda i, k, pos_ref: (pos_ref[1] * nr + i, k)), (nr, nc), name=name)


class _Comm:
    def __init__(self, pos, w):
        self.pos, self.w, self.st = pos, w, {n: {} for n, _ in BIG}

    def gather(self, names, stage, ride=None, part=(0, 1)):
        ride = _Ride() if ride is None else ride
        for n in names:
            _ride_gather(ride, self.w, n, AXIS[n], stage, part)
        return ride

    def grad(self, n, g32, g16):
        self.st[n].update(g32=g32, g16=g16)

    def pair(self, names, ride=None):
        ride = _Ride() if ride is None else ride
        for n in names:
            _ride_pair(ride, self.st[n], AXIS[n])
        return ride

    def add(self, names):
        for n in names:
            st = self.st[n]
            st["s16"] = _pair_add(st["g32"], st["sib"], AXIS[n], self.pos, name="rs_add_" + n)

    def chips(self, names, ride=None, rows=None):
        ride = _Ride() if ride is None else ride
        for n in names:
            _ride_chips(ride, self.st[n], rows)
        return ride

    def sum(self, names):
        for n in names:
            st = self.st[n]
            st["shard"] = _chip_sum(st["g32"], st["sib"], st["recv"], AXIS[n], self.pos, name="rs_sum_" + n)

    def share(self, names, ride=None):
        ride = _Ride() if ride is None else ride
        for n in names:
            _ride_share(ride, self.st[n])
        return ride

    def tail(self, count):
        st = self.st["w_in"]
        rows, at = st["s16"].shape[1], st.get("at", 0)
        st["at"] = at + count
        return self.chips(("w_in",), rows=(at * rows // TAIL_PARTS, count * rows // TAIL_PARTS))

    def tail_rest(self):
        return self.tail(TAIL_PARTS - self.st["w_in"].get("at", 0))

    def result(self, n):
        return self.st[n]["g"]


class _NoComm:
    def __init__(self, w):
        self.w, self.st = w, {}

    def grad(self, n, g32, g16):
        self.st[n] = (g32, g16)

    def result(self, n):
        return self.st[n]

    def add(self, names):
        pass

    sum = add

    def gather(self, names, *args, **kwargs):
        return None

    pair = chips = share = tail = gather


def _small_all_reduce(vec, *, name):
    r = vec.shape[0]

    def body(vec_ref, out_ref, slots, send_sems, recv_sems):
        x, y, c, _ = _position()
        me = 4 * x + 2 * y + c
        slots[me] = vec_ref[...]
        sends = []
        for k in range(1, 8):
            to = (x ^ (k >> 2), y ^ ((k >> 1) & 1), c ^ (k & 1))
            cp = _remote(slots.at[me], slots.at[me], send_sems, recv_sems, k - 1, to)
            cp.start()
            sends.append(cp)
        for k in range(1, 8):
            frm = 4 * (x ^ (k >> 2)) + 2 * (y ^ ((k >> 1) & 1)) + (c ^ (k & 1))
            _remote(slots.at[frm], slots.at[frm], send_sems, recv_sems, k - 1, (x, y, c)).wait_recv()
        for cp in sends:
            cp.wait_send()
        total = slots[0]
        for d in range(1, 8):
            total = total + slots[d]
        out_ref[...] = total

    return pl.pallas_call(
        body, name=name,
        in_specs=[pl.BlockSpec(memory_space=pltpu.VMEM)], out_specs=pl.BlockSpec(memory_space=pltpu.VMEM),
        out_shape=jax.ShapeDtypeStruct((r, 128), F32),
        scratch_shapes=[pltpu.VMEM((8, r, 128), F32), pltpu.SemaphoreType.DMA((7,)), pltpu.SemaphoreType.DMA((7,))],
    )(vec)


def _adamw(w, g, m, v, *, name, ride=None):
    r, c = w.shape
    tc = c if c <= 4096 else _pick(c, (2048, 1024, 512, 256, 128))
    tr = next(t for t in (512, 256, 128, 64, 32, 16, 8) if r % t == 0 and t * tc <= 256 * 1024)

    def body(w_ref, g_ref, m_ref, v_ref, d_ref, nm_ref, nv_ref):
        gv = g_ref[...]
        nm = ADAM_B1 * m_ref[...] + (1.0 - ADAM_B1) * gv
        nv = ADAM_B2 * v_ref[...] + (1.0 - ADAM_B2) * (gv * gv)
        m_hat = nm / (1.0 - ADAM_B1 ** ADAM_STEP)
        v_hat = nv / (1.0 - ADAM_B2 ** ADAM_STEP)
        d_ref[...] = -ADAM_LR * (m_hat / (jnp.sqrt(v_hat) + ADAM_EPS) + ADAM_WD * w_ref[...])
        nm_ref[...] = nm
        nv_ref[...] = nv

    blk = ((tr, tc), lambda i, j: (i, j))
    sd = jax.ShapeDtypeStruct((r, c), F32)
    return _ew(body, [w, g, m, v], [blk] * 4, [sd, sd, sd], [blk] * 3, (r // tr, c // tc), name=name, ride=ride)


BIG = (("w_in", 1), ("w_sb_out", 1), ("w_ca_out", 1), ("w_mix_out", 0), ("w_ffn_in", 1), ("w_ffn_out", 0),
       ("w_ple_in", 1), ("w_ple_gate", 0))
AXIS = dict(BIG)
HEAD_PARTS = 8
HEAD_HOSTS = ("w_ffn_in", "w_ffn_out")
TAIL_PARTS = 16
TAIL_SECOND = 4
TAIL_FIRST = 2
TAIL_HOSTS = {"w_ffn_in": 4, "w_ffn_out": 2, "w_mix_out": 1, "w_ple_gate": 1}
SMALL = ("rel_bias", "g_mix", "g_ffn", "g_ple", "g_final")
ORDER = ("w_in", "w_sb_out", "w_ca_out", "w_mix_out", "rel_bias", "g_mix", "g_ffn", "g_ple", "g_final",
         "w_ffn_in", "w_ffn_out", "w_ple_in", "w_ple_gate")


def _pack(parts):
    flat = jnp.concatenate([a.reshape(-1) for a in parts])
    rows = -(-flat.shape[0] // 1024) * 8
    return jnp.pad(flat, (0, rows * 128 - flat.shape[0])).reshape(rows, 128)


def _unpack(packed, like):
    flat, out, at = packed.reshape(-1), [], 0
    for a in like:
        out.append(flat[at:at + a.size].reshape(a.shape))
        at += a.size
    return out


def kernel(x, p, w_in, w_sb_out, w_ca_out, w_mix_out, rel_bias, g_mix, g_ffn, g_ple, g_final, w_ffn_in, w_ffn_out, w_ple_in, w_ple_gate, loss_target, m_w_in, m_w_sb_out, m_w_ca_out, m_w_mix_out, m_rel_bias, m_g_mix, m_g_ffn, m_g_ple, m_g_final, m_w_ffn_in, m_w_ffn_out, m_w_ple_in, m_w_ple_gate, v_w_in, v_w_sb_out, v_w_ca_out, v_w_mix_out, v_rel_bias, v_g_mix, v_g_ffn, v_g_ple, v_g_final, v_w_ffn_in, v_w_ffn_out, v_w_ple_in, v_w_ple_gate):
    weights = dict(w_in=w_in, w_sb_out=w_sb_out, w_ca_out=w_ca_out, w_mix_out=w_mix_out, rel_bias=rel_bias,
                   g_mix=g_mix, g_ffn=g_ffn, g_ple=g_ple, g_final=g_final, w_ffn_in=w_ffn_in,
                   w_ffn_out=w_ffn_out, w_ple_in=w_ple_in, w_ple_gate=w_ple_gate)
    m_in = dict(w_in=m_w_in, w_sb_out=m_w_sb_out, w_ca_out=m_w_ca_out, w_mix_out=m_w_mix_out, rel_bias=m_rel_bias,
                g_mix=m_g_mix, g_ffn=m_g_ffn, g_ple=m_g_ple, g_final=m_g_final, w_ffn_in=m_w_ffn_in,
                w_ffn_out=m_w_ffn_out, w_ple_in=m_w_ple_in, w_ple_gate=m_w_ple_gate)
    v_in = dict(w_in=v_w_in, w_sb_out=v_w_sb_out, w_ca_out=v_w_ca_out, w_mix_out=v_w_mix_out, rel_bias=v_rel_bias,
                g_mix=v_g_mix, g_ffn=v_g_ffn, g_ple=v_g_ple, g_final=v_g_final, w_ffn_in=v_w_ffn_in,
                w_ffn_out=v_w_ffn_out, w_ple_in=v_w_ple_in, w_ple_gate=v_w_ple_gate)

    pos = jnp.stack([2 * lax.axis_index("x") + lax.axis_index("y"), lax.axis_index("c")]).astype(jnp.int32)
    comm = _Comm(pos, {"w_in": _cast_place(w_in[0], AXIS["w_in"], pos, name="cast_w_in")})
    at = 0
    for n in HEAD_HOSTS:
        ride = comm.gather(("w_in",), "near", part=(at, HEAD_PARTS))
        comm.w[n] = _cast_place(weights[n][0], AXIS[n], pos, name="cast_" + n, ride=ride)
        at += 1
    for n, axis in BIG:
        if n not in comm.w:
            comm.w[n] = _cast_place(weights[n][0], axis, pos, name="cast_" + n)
    _run(comm.gather(("w_in",), "near", part=(at, HEAD_PARTS, HEAD_PARTS - at)), name="gather_w_in_near")
    _run(comm.gather(("w_in",), "far"), name="gather_w_in_far")
    _run(comm.gather(("w_in",), "pair"), name="gather_w_in_pair")
    small = dict(rel_bias=rel_bias[0], g_mix=g_mix, g_ffn=g_ffn, g_ple=g_ple, g_final=g_final.reshape(1, -1))
    loss, grad_x, gs = _step(x[0], p[0, 0], loss_target[0], small, comm)

    grads, delta, new_m, new_v = {}, {}, {}, {}
    for n in [n for n, _ in BIG if n != "w_in"] + ["w_in"]:
        ride = comm.tail(TAIL_HOSTS[n]) if n in TAIL_HOSTS else None
        if n == "w_in":
            _run(comm.tail_rest(), name="rs_chips_w_in")
            comm.sum(("w_in",))
            _run(comm.share(("w_in",)), name="rs_share_w_in")
        g = comm.result(n)
        d, nm, nv = _adamw(weights[n][0], g, m_in[n][0], v_in[n][0], name="adamw_" + n, ride=ride)
        grads[n], delta[n], new_m[n], new_v[n] = g[None], d[None], nm[None], nv[None]

    like = [weights[n] for n in SMALL]
    reduced = _small_all_reduce(_pack([gs[n] for n in SMALL] + [loss[:, :1]]), name="small_all_reduce")
    g_small = _unpack(reduced, like + [loss[:, :1]])
    total_loss = g_small[-1].reshape(())
    g_packed = _pack(g_small[:-1])
    d_s, m_s, v_s = _adamw(_pack(like), g_packed, _pack([m_in[n] for n in SMALL]), _pack([v_in[n] for n in SMALL]),
                           name="adamw_small")
    for n, g, d, nm, nv in zip(SMALL, g_small[:-1], _unpack(d_s, like), _unpack(m_s, like), _unpack(v_s, like)):
        grads[n], delta[n], new_m[n], new_v[n] = g, d, nm, nv

    return (total_loss, grad_x[None], *[grads[n] for n in ORDER], *[delta[n] for n in ORDER],
            *[new_m[n] for n in ORDER], *[new_v[n] for n in ORDER])
```

```python
import functools
import math

import jax
import jax.numpy as jnp
import numpy as np
from jax import lax
from jax.experimental import pallas as pl
from jax.experimental.pallas import tpu as pltpu

F32 = jnp.float32
BF16 = jnp.bfloat16

HEAD_DIM = 128
CHUNK = 64
LEFT_CHUNKS = 8
REL_CLIP = 128
N_REL = REL_CLIP + CHUNK
BAND = (LEFT_CHUNKS + 2) * CHUNK
CA_PER_STEP = 4
CA_ROWS = CA_PER_STEP * CHUNK
CA_BAND = BAND + CA_PER_STEP * CHUNK
CA_PAD = BAND
SB_BLOCK = 128
SB_KEYS = 512
SB_GROUPS = SB_KEYS // SB_BLOCK
SB_ROWS = SB_KEYS
EPS = 1e-6
NEG = -1e30

ADAM_LR = 0.001
ADAM_B1 = 0.9
ADAM_B2 = 0.999
ADAM_EPS = 1e-08
ADAM_WD = 0.01
ADAM_STEP = 10

VMEM_LIMIT = 48 * 1024 * 1024
MM_VMEM_BUDGET = 36 * 1024 * 1024
V7X_HBM_BYTES_PER_S = 3.7e12
GRID_STEP_S = 0.35e-6
MESH = pl.DeviceIdType.MESH
N_CHIPS = 4


def _pick(dim, prefs):
    for t in prefs:
        if dim % t == 0:
            return t
    raise ValueError(f"no tile for {dim}")


def _cparams(sem=None):
    return pltpu.CompilerParams(dimension_semantics=sem, vmem_limit_bytes=VMEM_LIMIT)


def _sigmoid(v):
    return 1.0 / (1.0 + jnp.exp(-v))


def _dot(a, b, dims):
    return lax.dot_general(a, b, (dims, ((), ())), preferred_element_type=F32)


def _dot_nn(a, b):
    return _dot(a, b, ((1,), (0,)))


def _dot_nt(a, b):
    return _dot(a, b, ((1,), (1,)))


def _dot_tn(a, b):
    return _dot(a, b, ((0,), (0,)))


HBM = pl.BlockSpec(memory_space=pltpu.HBM)


class _Ride:
    def __init__(self):
        self.items = []

    def add(self, ins, outs, aliases, n_sems, start, finish, sink):
        self.items.append((ins, outs, aliases, n_sems, start, finish, sink))


def _call(body, args, *, name, grid, in_specs, out_specs, out_shape, scratch_shapes=(), sem=None, ride=None,
          scalars=None, onto=()):
    items = ride.items if ride is not None else []
    if onto:
        args, in_specs = list(args) + list(onto), list(in_specs) + [HBM] * len(onto)
        inner, body = body, lambda *refs: inner(*refs[:len(args) - len(onto)], *refs[len(args):])
    n_in, n_out, n_scr = len(args), len(out_shape), len(scratch_shapes)
    r_ins = [a for it in items for a in it[0]]
    r_outs = [o for it in items for o in it[1]]
    updated = [id(it[0][i]) for it in items for i in it[2]]
    assert len(set(updated)) == len(updated), "one call may update a buffer in place only once"
    aliases, a, b = {n_in - len(onto) + t: t for t in range(len(onto))}, n_in, n_out
    for it in items:
        aliases.update({a + i: b + o for i, o in it[2].items()})
        a, b = a + len(it[0]), b + len(it[1])
    sems = [pltpu.SemaphoreType.DMA((it[3],)) for it in items for _ in range(2)]

    def wrapped(*refs):
        head, refs = (refs[:1], refs[1:]) if scalars is not None else ((), refs)
        ins, rin = refs[:n_in], refs[n_in:n_in + len(r_ins)]
        at = n_in + len(r_ins)
        outs, rout = refs[at:at + n_out], refs[at + n_out:at + n_out + len(r_outs)]
        at += n_out + len(r_outs)
        scr, rsem = refs[at:at + n_scr], refs[at + n_scr:]

        def each(which):
            a = b = 0
            for q, it in enumerate(items):
                it[which](rin[a:a + len(it[0])], rout[b:b + len(it[1])], rsem[2 * q], rsem[2 * q + 1])
                a, b = a + len(it[0]), b + len(it[1])

        if items:
            ids = [pl.program_id(d) for d in range(len(grid))]
            first = functools.reduce(jnp.logical_and, [i == 0 for i in ids])
            last = functools.reduce(jnp.logical_and, [i == g - 1 for i, g in zip(ids, grid)])
            pl.when(first)(lambda: each(4))
        body(*head, *ins, *outs, *scr)
        if items:
            pl.when(last)(lambda: each(5))

    specs = dict(grid=grid, in_specs=list(in_specs) + [HBM] * len(r_ins),
                 out_specs=list(out_specs) + [HBM] * len(r_outs), scratch_shapes=list(scratch_shapes) + sems)
    if scalars is not None:
        specs = dict(grid_spec=pltpu.PrefetchScalarGridSpec(num_scalar_prefetch=1, **specs))
        aliases = {i + 1: o for i, o in aliases.items()}
    res = pl.pallas_call(
        wrapped, name=name, **specs,
        out_shape=list(out_shape) + r_outs,
        input_output_aliases=aliases,
        compiler_params=_cparams(("arbitrary",) * len(grid) if items else sem),
    )(*(() if scalars is None else (scalars,)), *args, *r_ins)
    b = n_out
    for it in items:
        it[6](res[b:b + len(it[1])])
        b += len(it[1])
    return list(res[:n_out])


def _mm_tiles(m, n_align, n, k, a_bytes, b_bytes, out_bytes):
    best = None
    tks = sorted({t for t in (k, k // 2, k // 4, 2048, 1024, 512, 256, 128) if t <= k and k % t == 0 and t % 128 == 0})
    for tm in (t for t in (2048, 1024, 512, 256, 128) if m % t == 0):
        for tn in (t for t in (2048, 1024, 512, 256, 128) if n_align % t == 0):
            for tk in tks:
                nk = k // tk
                vmem = 2 * (tm * tk * a_bytes + tk * tn * b_bytes + tm * tn * out_bytes) + tm * tn * 4
                if vmem > MM_VMEM_BUDGET:
                    continue
                traffic = m * k * a_bytes * (n // tn if nk > 1 else 1) + k * n * b_bytes * (m // tm)
                traffic += tm * tk * a_bytes + tk * tn * b_bytes + tm * tn * out_bytes
                traffic += m * n * 4 * nk if nk > 1 else 0
                cost = traffic / V7X_HBM_BYTES_PER_S + (m // tm) * (n // tn) * nk * GRID_STEP_S
                if best is None or cost < best[0]:
                    best = (cost, tm, tn, tk)
    return best[1:]


def _mm(a, b, mode, out_dtypes, *, name, n=None, b_col_off=0, resid=None, ride=None, rows=None, onto=()):
    if mode == "nn":
        m, k = a.shape
        n = b.shape[1] if n is None else n
    elif mode == "nt":
        m, k = a.shape
        n = b.shape[0]
    else:
        k, m = a.shape
        n = b.shape[1]
    m_all, (row0, m) = m, (0, m) if rows is None else rows
    n_out = len(out_dtypes)
    has_resid = resid is not None
    out_bytes = sum(jnp.dtype(dt).itemsize for dt in out_dtypes) + (4 if has_resid else 0)
    tm, tn, tk = _mm_tiles(math.gcd(m, row0) if row0 else m, math.gcd(n, b_col_off) if b_col_off else n, n, k,
                           a.dtype.itemsize, b.dtype.itemsize, out_bytes)
    nk = k // tk
    boff, roff = b_col_off // tn, row0 // tm
    dot = {"nn": _dot_nn, "nt": _dot_nt, "tn": _dot_tn}[mode]

    def body(*refs):
        a_ref, b_ref = refs[0], refs[1]
        r_ref = refs[2] if has_resid else None
        o_refs = refs[2 + has_resid: 2 + has_resid + n_out]

        def finish(r):
            if has_resid:
                r = r + r_ref[...]
            for o_ref in o_refs:
                o_ref[...] = r.astype(o_ref.dtype)

        part = dot(a_ref[...].astype(BF16), b_ref[...].astype(BF16))
        if nk == 1:
            finish(part)
            return
        acc_ref = refs[-1]
        kk = pl.program_id(2)

        @pl.when(kk == 0)
        def _():
            acc_ref[...] = part

        @pl.when(kk > 0)
        def _():
            acc_ref[...] += part

        @pl.when(kk == nk - 1)
        def _():
            finish(acc_ref[...])

    if mode == "nn":
        a_spec = pl.BlockSpec((tm, tk), lambda i, j, kk: (i + roff, kk))
        b_spec = pl.BlockSpec((tk, tn), lambda i, j, kk: (kk, j + boff))
    elif mode == "nt":
        a_spec = pl.BlockSpec((tm, tk), lambda i, j, kk: (i + roff, kk))
        b_spec = pl.BlockSpec((tn, tk), lambda i, j, kk: (j, kk))
    else:
        a_spec = pl.BlockSpec((tk, tm), lambda i, j, kk: (kk, i))
        b_spec = pl.BlockSpec((tk, tn), lambda i, j, kk: (kk, j))
    o_spec = pl.BlockSpec((tm, tn), lambda i, j, kk: (i + roff, j))
    in_specs = [a_spec, b_spec] + ([o_spec] if has_resid else [])
    args = [a, b] + ([resid] if has_resid else [])
    outs = _call(
        body, args, name=name,
        grid=(m // tm, n // tn, nk),
        in_specs=in_specs,
        out_specs=[o_spec] * n_out,
        out_shape=[jax.ShapeDtypeStruct((m_all, n), dt) for dt in out_dtypes],
        scratch_shapes=[pltpu.VMEM((tm, tn), F32)] if nk > 1 else [],
        sem=("parallel", "parallel", "arbitrary"), ride=ride, onto=onto)
    return outs[0] if n_out == 1 else tuple(outs)


def _row_tile(s):
    return _pick(s, (256, 128))


def _rms_fwd(x, g, *, name, ride=None):
    s, d = x.shape
    tr = _row_tile(s)

    def body(x_ref, g_ref, o_ref):
        xv = x_ref[...]
        r = lax.rsqrt(jnp.mean(xv * xv, axis=1, keepdims=True) + EPS)
        o_ref[...] = (xv * r * g_ref[...]).astype(o_ref.dtype)

    return _call(
        body, [x, g], name=name, grid=(s // tr,),
        in_specs=[pl.BlockSpec((tr, d), lambda i: (i, 0)), pl.BlockSpec((1, d), lambda i: (0, 0))],
        out_specs=[pl.BlockSpec((tr, d), lambda i: (i, 0))],
        out_shape=[jax.ShapeDtypeStruct((s, d), BF16)], sem=("parallel",), ride=ride)[0]


def _rms_bwd(x, g, dh, dres, *, name, ride=None):
    s, d = x.shape
    tr = _row_tile(s)

    def body(x_ref, g_ref, dh_ref, dres_ref, dx_ref, dx16_ref, dg_ref):
        i = pl.program_id(0)
        xv = x_ref[...]
        r = lax.rsqrt(jnp.mean(xv * xv, axis=1, keepdims=True) + EPS)
        xhat = xv * r
        dhv = dh_ref[...]
        dxhat = dhv * g_ref[...]
        proj = jnp.mean(dxhat * xhat, axis=1, keepdims=True)
        dx = dres_ref[...] + r * (dxhat - xhat * proj)
        dx_ref[...] = dx
        dx16_ref[...] = dx.astype(dx16_ref.dtype)

        @pl.when(i == 0)
        def _():
            dg_ref[...] = jnp.zeros_like(dg_ref)

        dg_ref[...] += jnp.sum(dhv * xhat, axis=0, keepdims=True)

    row = pl.BlockSpec((tr, d), lambda i: (i, 0))
    vec = pl.BlockSpec((1, d), lambda i: (0, 0))
    return _call(
        body, [x, g, dh, dres], name=name, grid=(s // tr,),
        in_specs=[row, vec, row, row],
        out_specs=[row, row, vec],
        out_shape=[jax.ShapeDtypeStruct((s, d), F32), jax.ShapeDtypeStruct((s, d), BF16),
                   jax.ShapeDtypeStruct((1, d), F32)],
        sem=("arbitrary",), ride=ride)


def _final_loss(x, g, target, *, name):
    s, d = x.shape
    tr = _row_tile(s)

    def body(x_ref, g_ref, t_ref, dx_ref, dg_ref, loss_ref):
        i = pl.program_id(0)
        xv = x_ref[...]
        gv = g_ref[...]
        r = lax.rsqrt(jnp.mean(xv * xv, axis=1, keepdims=True) + EPS)
        xhat = xv * r
        err = xhat * gv - t_ref[...]
        dy = err * (1.0 / d)
        dxhat = dy * gv
        proj = jnp.mean(dxhat * xhat, axis=1, keepdims=True)
        dx_ref[...] = r * (dxhat - xhat * proj)

        @pl.when(i == 0)
        def _():
            dg_ref[...] = jnp.zeros_like(dg_ref)
            loss_ref[...] = jnp.zeros_like(loss_ref)

        dg_ref[...] += jnp.sum(dy * xhat, axis=0, keepdims=True)
        part = 0.5 * jnp.sum(jnp.mean(err * err, axis=1, keepdims=True), axis=0, keepdims=True)
        loss_ref[...] += jnp.broadcast_to(part, loss_ref.shape)

    row = pl.BlockSpec((tr, d), lambda i: (i, 0))
    vec = pl.BlockSpec((1, d), lambda i: (0, 0))
    return pl.pallas_call(
        body, name=name, grid=(s // tr,),
        in_specs=[row, vec, row],
        out_specs=[row, vec, pl.BlockSpec((1, 128), lambda i: (0, 0))],
        out_shape=[jax.ShapeDtypeStruct((s, d), F32), jax.ShapeDtypeStruct((1, d), F32),
                   jax.ShapeDtypeStruct((1, 128), F32)],
        compiler_params=_cparams(("arbitrary",)),
    )(x, g, target)


def _ew(body, ins, in_blocks, outs, out_blocks, grid, *, name, ride=None):
    return _call(body, ins, name=name, grid=grid,
                 in_specs=[pl.BlockSpec(bs, im) for bs, im in in_blocks],
                 out_specs=[pl.BlockSpec(bs, im) for bs, im in out_blocks],
                 out_shape=outs, sem=("parallel",) * len(grid), ride=ride)


def _gate_merge_fwd(gates, o_sb, o_ca, *, name, ride=None):
    s, d = o_sb.shape
    tr, tc = _row_tile(s), _pick(d, (1024, 512, 256, 128))
    nc = d // tc

    def body(gs_ref, gc_ref, os_ref, oc_ref, m_ref):
        m = _sigmoid(gs_ref[...]) * os_ref[...] + _sigmoid(gc_ref[...]) * oc_ref[...]
        m_ref[...] = m.astype(m_ref.dtype)

    blk = ((tr, tc), lambda i, j: (i, j))
    return _ew(body, [gates, gates, o_sb, o_ca],
               [blk, ((tr, tc), lambda i, j: (i, j + nc)), blk, blk],
               [jax.ShapeDtypeStruct((s, d), BF16)], [blk], (s // tr, nc), name=name, ride=ride)[0]


def _gate_merge_bwd(dmerged, gates, o_sb, o_ca, *, name):
    s, d = o_sb.shape
    tr, tc = _row_tile(s), _pick(d, (1024, 512, 256, 128))
    nc = d // tc

    def body(dm_ref, gs_ref, gc_ref, os_ref, oc_ref, dgs_ref, dgc_ref, dos_ref, doc_ref):
        dm = dm_ref[...]
        ss = _sigmoid(gs_ref[...])
        sc = _sigmoid(gc_ref[...])
        dgs_ref[...] = (dm * os_ref[...] * ss * (1.0 - ss)).astype(dgs_ref.dtype)
        dgc_ref[...] = (dm * oc_ref[...] * sc * (1.0 - sc)).astype(dgc_ref.dtype)
        dos_ref[...] = (dm * ss).astype(dos_ref.dtype)
        doc_ref[...] = (dm * sc).astype(doc_ref.dtype)

    blk = ((tr, tc), lambda i, j: (i, j))
    sd = jax.ShapeDtypeStruct((s, d), BF16)
    return _ew(body, [dmerged, gates, gates, o_sb, o_ca],
               [blk, blk, ((tr, tc), lambda i, j: (i, j + nc)), blk, blk],
               [sd, sd, sd, sd], [blk, blk, blk, blk], (s // tr, nc), name=name)


def _swiglu_fwd(gu, *, name, ride=None):
    s, f2 = gu.shape
    f = f2 // 2
    tr, tc = 128, _pick(f, (512, 256, 128))

    def body(gu_ref, a_ref):
        for at in range(0, f, tc):
            gv = gu_ref[:, at:at + tc].astype(F32)
            a_ref[:, at:at + tc] = (gv * _sigmoid(gv) * gu_ref[:, f + at:f + at + tc].astype(F32)).astype(a_ref.dtype)

    row = lambda i: (i, 0)
    return _ew(body, [gu], [((tr, f2), row)], [jax.ShapeDtypeStruct((s, f), BF16)], [((tr, f), row)],
               (s // tr,), name=name, ride=ride)[0]


def _swiglu_bwd(dact, gu, *, name):
    s, f2 = gu.shape
    f = f2 // 2
    tr, tc = 128, _pick(f, (512, 256, 128))

    def body(da_ref, gu_ref, o_ref):
        for at in range(0, f, tc):
            da = da_ref[:, at:at + tc]
            gv = gu_ref[:, at:at + tc].astype(F32)
            sg = _sigmoid(gv)
            uv = gu_ref[:, f + at:f + at + tc].astype(F32)
            o_ref[:, at:at + tc] = (da * uv * sg * (1.0 + gv * (1.0 - sg))).astype(o_ref.dtype)
            o_ref[:, f + at:f + at + tc] = (da * gv * sg).astype(o_ref.dtype)

    row = lambda i: (i, 0)
    return _ew(body, [dact, gu], [((tr, f), row), ((tr, f2), row)], [jax.ShapeDtypeStruct((s, f2), BF16)],
               [((tr, f2), row)], (s // tr,), name=name)[0]


def _concat_cols(parts, *, name):
    s = parts[0].shape[0]
    widths = [p.shape[1] for p in parts]
    tr = 256

    def body(*refs):
        o_ref, at = refs[-1], 0
        for p_ref, width in zip(refs, widths):
            o_ref[:, at:at + width] = p_ref[...]
            at += width

    row = lambda i: (i, 0)
    return _ew(body, list(parts), [((tr, width), row) for width in widths],
               [jax.ShapeDtypeStruct((s, sum(widths)), parts[0].dtype)], [((tr, sum(widths)), row)],
               (s // tr,), name=name)[0]


def _ple_fwd(x, t, pe, *, name):
    s, d = x.shape
    tr, tc = _row_tile(s), _pick(d, (1024, 512, 256, 128))

    def body(x_ref, t_ref, p_ref, o_ref):
        o_ref[...] = x_ref[...] + _sigmoid(t_ref[...]) * p_ref[...]

    blk = ((tr, tc), lambda i, j: (i, j))
    return _ew(body, [x, t, pe], [blk, blk, blk],
               [jax.ShapeDtypeStruct((s, d), F32)], [blk], (s // tr, d // tc), name=name)[0]


def _ple_bwd(dx, t, pe, *, name):
    s, d = dx.shape
    tr, tc = _row_tile(s), _pick(d, (1024, 512, 256, 128))

    def body(dx_ref, t_ref, p_ref, dt_ref, dp_ref):
        dxv = dx_ref[...]
        sg = _sigmoid(t_ref[...])
        dt_ref[...] = (dxv * p_ref[...] * sg * (1.0 - sg)).astype(dt_ref.dtype)
        dp_ref[...] = (dxv * sg).astype(dp_ref.dtype)

    blk = ((tr, tc), lambda i, j: (i, j))
    sd = jax.ShapeDtypeStruct((s, d), BF16)
    return _ew(body, [dx, t, pe], [blk, blk, blk], [sd, sd], [blk, blk], (s // tr, d // tc), name=name)


def _sb_tri(later):
    row = lax.broadcasted_iota(jnp.int32, (SB_BLOCK, SB_BLOCK), 0)
    col = lax.broadcasted_iota(jnp.int32, (SB_BLOCK, SB_BLOCK), 1)
    tri = (row > col) if later else (row < col)
    return jnp.concatenate([tri.astype(BF16), jnp.ones((SB_BLOCK, SB_BLOCK), BF16)], axis=1)


def _sb_valid(i, j, own):
    if not own:
        return None
    qi = i * SB_ROWS + lax.broadcasted_iota(jnp.int32, (SB_ROWS, SB_KEYS), 0)
    ki = j * SB_KEYS + lax.broadcasted_iota(jnp.int32, (SB_ROWS, SB_KEYS), 1)
    return ki < qi


def _sb_scan(v, tri, run, later):
    hi = v.astype(BF16)
    lo = (v - hi.astype(F32)).astype(BF16)
    outs = [None] * SB_GROUPS
    for b in (reversed(range(SB_GROUPS)) if later else range(SB_GROUPS)):
        cols = slice(b * SB_BLOCK, (b + 1) * SB_BLOCK)
        r = _dot_nn(hi[:, cols], tri) + _dot_nn(lo[:, cols], tri)
        outs[b] = r[:, :SB_BLOCK] + run
        run = run + r[:, SB_BLOCK:]
    return jnp.concatenate(outs, axis=1), run


def _masked(valid, v):
    return v if valid is None else jnp.where(valid, v, 0.0)


def _sb_scores(q, kj, scale, valid):
    z = _dot_nt(q, kj) * scale
    t = jnp.log(1.0 + jnp.exp(-jnp.abs(z)))
    return jnp.minimum(z, 0.0) - t, _masked(valid, -jnp.maximum(z, 0.0) - t)


def _sb_specs(h_count, s, col0):
    q_spec = pl.BlockSpec((SB_ROWS, HEAD_DIM), lambda h, i: (i, col0 + h))
    k_spec = pl.BlockSpec((s, HEAD_DIM), lambda h, i: (0, col0 + h_count + h))
    v_spec = pl.BlockSpec((s, HEAD_DIM), lambda h, i: (0, col0 + 2 * h_count + h))
    return q_spec, k_spec, v_spec


def _sb_fwd(qkv, n_heads, col0, *, name, ride=None):
    s = qkv.shape[0]
    nq = s // SB_ROWS
    scale = HEAD_DIM ** -0.5

    def body(q_ref, k_ref, v_ref, o_ref):
        i = pl.program_id(1)
        q = q_ref[...]
        tri = _sb_tri(later=True)

        def step(j, carry, own):
            run, acc = carry
            off = pl.multiple_of(j * SB_KEYS, SB_KEYS)
            valid = _sb_valid(i, j, own)
            ls, lk = _sb_scores(q, k_ref[pl.ds(off, SB_KEYS), :], scale, valid)
            between, run = _sb_scan(lk, tri, run, later=True)
            a = _masked(valid, jnp.exp(ls + between))
            return run, acc + _dot_nn(a.astype(BF16), v_ref[pl.ds(off, SB_KEYS), :])

        carry = step(i, (jnp.zeros((SB_ROWS, SB_BLOCK), F32), jnp.zeros((SB_ROWS, HEAD_DIM), F32)), True)
        _, acc = lax.fori_loop(0, i, lambda jj, c: step(i - 1 - jj, c, False), carry)
        o_ref[...] = acc.astype(o_ref.dtype)

    q_spec, k_spec, v_spec = _sb_specs(n_heads, s, col0)
    return _call(
        body, [qkv, qkv, qkv], name=name, grid=(n_heads, nq),
        in_specs=[q_spec, k_spec, v_spec],
        out_specs=[pl.BlockSpec((SB_ROWS, HEAD_DIM), lambda h, i: (i, h))],
        out_shape=[jax.ShapeDtypeStruct((s, n_heads * HEAD_DIM), BF16)],
        sem=("parallel", "arbitrary"), ride=ride)[0]


def _sb_bwd(qkv, dy, n_heads, col0, *, name, ride=None):
    s = qkv.shape[0]
    nq = s // SB_ROWS
    scale = HEAD_DIM ** -0.5

    def body(q_ref, k_ref, v_ref, dy_ref, dq_ref, dk_ref, dv_ref, e_scr, sg_scr, dk_acc, dv_acc):
        i = pl.program_id(1)
        q = q_ref[...]
        dyv = dy_ref[...]

        @pl.when(i == 0)
        def _():
            dk_acc[...] = jnp.zeros_like(dk_acc)
            dv_acc[...] = jnp.zeros_like(dv_acc)

        tri_later = _sb_tri(later=True)

        def pass1(j, run, own):
            off = pl.multiple_of(j * SB_KEYS, SB_KEYS)
            valid = _sb_valid(i, j, own)
            ls, lk = _sb_scores(q, k_ref[pl.ds(off, SB_KEYS), :], scale, valid)
            between, run = _sb_scan(lk, tri_later, run, later=True)
            a = _masked(valid, jnp.exp(ls + between))
            e_scr[j] = a * _dot_nt(dyv, v_ref[pl.ds(off, SB_KEYS), :])
            sg_scr[j] = jnp.exp(ls)
            dv_acc[pl.ds(off, SB_KEYS), :] += _dot_tn(a.astype(BF16), dyv)
            return run

        lax.fori_loop(0, i, lambda jj, run: pass1(i - 1 - jj, run, False),
                      pass1(i, jnp.zeros((SB_ROWS, SB_BLOCK), F32), True))

        tri_earlier = _sb_tri(later=False)

        def pass2(j, carry, own):
            run, dq = carry
            off = pl.multiple_of(j * SB_KEYS, SB_KEYS)
            kj = k_ref[pl.ds(off, SB_KEYS), :]
            sg = sg_scr[j]
            e = e_scr[j]
            before, run = _sb_scan(e, tri_earlier, run, later=False)
            dz = _masked(_sb_valid(i, j, own), e * (1.0 - sg) - sg * before) * scale
            dzb = dz.astype(BF16)
            dk_acc[pl.ds(off, SB_KEYS), :] += _dot_tn(dzb, q)
            return run, dq + _dot_nn(dzb, kj)

        init = (jnp.zeros((SB_ROWS, SB_BLOCK), F32), jnp.zeros((SB_ROWS, HEAD_DIM), F32))
        _, dq = pass2(i, lax.fori_loop(0, i, lambda j, c: pass2(j, c, False), init), True)
        dq_ref[...] = dq.astype(dq_ref.dtype)

        @pl.when(i == nq - 1)
        def _():
            dk_ref[...] = dk_acc[...].astype(dk_ref.dtype)
            dv_ref[...] = dv_acc[...].astype(dv_ref.dtype)

    q_spec, k_spec, v_spec = _sb_specs(n_heads, s, col0)
    blk = pl.BlockSpec((SB_ROWS, HEAD_DIM), lambda h, i: (i, h))
    full = pl.BlockSpec((s, HEAD_DIM), lambda h, i: (0, h))
    sd = jax.ShapeDtypeStruct((s, n_heads * HEAD_DIM), BF16)
    return _call(
        body, [qkv, qkv, qkv, dy], name=name, grid=(n_heads, nq),
        in_specs=[q_spec, k_spec, v_spec, blk],
        out_specs=[blk, full, full],
        out_shape=[sd, sd, sd],
        scratch_shapes=[pltpu.VMEM((s // SB_KEYS, SB_ROWS, SB_KEYS), F32), pltpu.VMEM((s // SB_KEYS, SB_ROWS, SB_KEYS), F32),
                        pltpu.VMEM((s, HEAD_DIM), F32), pltpu.VMEM((s, HEAD_DIM), F32)],
        sem=("parallel", "arbitrary"), ride=ride)


def _band_bias(rel_bias):
    h = rel_bias.shape[0]
    width = BAND + CHUNK
    first = width - 1 - N_REL
    line = jnp.concatenate([jnp.broadcast_to(rel_bias[:, :1], (h, first)), rel_bias], axis=1)
    tiled = jnp.broadcast_to(line[:, None, :], (h, CHUNK, width - 1)).reshape(h, CHUNK * (width - 1))
    skew = jnp.pad(tiled, ((0, 0), (0, CHUNK))).reshape(h, CHUNK, width)[:, ::-1, :BAND]
    seen = jnp.arange(BAND) >= CHUNK
    return jnp.where(seen[None, None, :], skew, NEG)


def _band_bias_grad(dbias):
    h = dbias.shape[0]
    width = BAND + CHUNK
    flipped = jnp.pad(dbias[:, ::-1, :], ((0, 0), (0, 0), (0, CHUNK)))
    skew = flipped.reshape(h, CHUNK * width)[:, :CHUNK * (width - 1)].reshape(h, CHUNK, width - 1)
    diag = jnp.sum(skew, axis=1)
    first = width - 1 - N_REL
    clipped = jnp.sum(diag[:, :first + 1], axis=1, keepdims=True)
    return jnp.concatenate([clipped, diag[:, first + 1:]], axis=1)


def _group_bias(band):
    return jnp.concatenate([jnp.pad(band, ((0, 0), (0, 0), ((u + 1) * CHUNK, (CA_PER_STEP - 1 - u) * CHUNK)),
                                    constant_values=NEG) for u in range(CA_PER_STEP)], axis=1)


def _group_bias_grad(dgroup):
    return sum(dgroup[:, u * CHUNK:(u + 1) * CHUNK, (u + 1) * CHUNK:(u + 1) * CHUNK + BAND] for u in range(CA_PER_STEP))


def _ca_load_padded(k_ref, v_ref, kp, vp, s):
    kp[pl.ds(0, CA_PAD), :] = jnp.zeros((CA_PAD, HEAD_DIM), kp.dtype)
    vp[pl.ds(0, CA_PAD), :] = jnp.zeros((CA_PAD, HEAD_DIM), vp.dtype)
    kp[pl.ds(CA_PAD, s), :] = k_ref[...]
    vp[pl.ds(CA_PAD, s), :] = v_ref[...]


def _ca_weights(q, kb, bias, off, scale):
    z = _dot_nt(q, kb) * scale + bias
    pos = off + lax.broadcasted_iota(jnp.int32, (CA_ROWS, CA_BAND), 1)
    z = jnp.where(pos >= CA_PAD, z, NEG)
    p = jnp.exp(z - jnp.max(z, axis=1, keepdims=True))
    return p / jnp.sum(p, axis=1, keepdims=True)


def _ca_specs(h_count, s, col0):
    q_spec = pl.BlockSpec((CA_ROWS, HEAD_DIM), lambda h, c: (c, col0 + h))
    k_spec = pl.BlockSpec((s, HEAD_DIM), lambda h, c: (0, col0 + h_count + h))
    v_spec = pl.BlockSpec((s, HEAD_DIM), lambda h, c: (0, col0 + 2 * h_count + h))
    b_spec = pl.BlockSpec((1, CA_ROWS, CA_BAND), lambda h, c: (h, 0, 0))
    return q_spec, k_spec, v_spec, b_spec


def _ca_fwd(qkv, bias, n_heads, col0, *, name, ride=None):
    s = qkv.shape[0]
    nc = s // CA_ROWS
    scale = HEAD_DIM ** -0.5

    def body(q_ref, k_ref, v_ref, b_ref, o_ref, kp, vp):
        c = pl.program_id(1)

        @pl.when(c == 0)
        def _():
            _ca_load_padded(k_ref, v_ref, kp, vp, s)

        off = pl.multiple_of(c * CA_ROWS, CA_ROWS)
        w = _ca_weights(q_ref[...], kp[pl.ds(off, CA_BAND), :], b_ref[0], off, scale)
        o_ref[...] = _dot_nn(w.astype(BF16), vp[pl.ds(off, CA_BAND), :]).astype(o_ref.dtype)

    q_spec, k_spec, v_spec, b_spec = _ca_specs(n_heads, s, col0)
    return _call(
        body, [qkv, qkv, qkv, bias], name=name, grid=(n_heads, nc),
        in_specs=[q_spec, k_spec, v_spec, b_spec],
        out_specs=[pl.BlockSpec((CA_ROWS, HEAD_DIM), lambda h, c: (c, h))],
        out_shape=[jax.ShapeDtypeStruct((s, n_heads * HEAD_DIM), BF16)],
        scratch_shapes=[pltpu.VMEM((s + CA_PAD, HEAD_DIM), BF16), pltpu.VMEM((s + CA_PAD, HEAD_DIM), BF16)],
        sem=("parallel", "arbitrary"), ride=ride)[0]


def _ca_bwd(qkv, bias, dy, n_heads, col0, *, name, ride=None):
    s = qkv.shape[0]
    nc = s // CA_ROWS
    scale = HEAD_DIM ** -0.5

    def body(q_ref, k_ref, v_ref, b_ref, dy_ref, dq_ref, dk_ref, dv_ref, db_ref, kp, vp, dkp, dvp):
        c = pl.program_id(1)

        @pl.when(c == 0)
        def _():
            _ca_load_padded(k_ref, v_ref, kp, vp, s)
            dkp[...] = jnp.zeros_like(dkp)
            dvp[...] = jnp.zeros_like(dvp)
            db_ref[...] = jnp.zeros_like(db_ref)

        off = pl.multiple_of(c * CA_ROWS, CA_ROWS)
        band = pl.ds(off, CA_BAND)
        q = q_ref[...]
        dyv = dy_ref[...]
        kb = kp[band, :]
        w = _ca_weights(q, kb, b_ref[0], off, scale)
        dw = _dot_nt(dyv, vp[band, :])
        dvp[band, :] += _dot_tn(w.astype(BF16), dyv)
        dz = w * (dw - jnp.sum(w * dw, axis=1, keepdims=True))
        db_ref[0] += dz
        dzs = (dz * scale).astype(BF16)
        dq_ref[...] = _dot_nn(dzs, kb).astype(dq_ref.dtype)
        dkp[band, :] += _dot_tn(dzs, q)

        @pl.when(c == nc - 1)
        def _():
            dk_ref[...] = dkp[pl.ds(CA_PAD, s), :].astype(dk_ref.dtype)
            dv_ref[...] = dvp[pl.ds(CA_PAD, s), :].astype(dv_ref.dtype)

    q_spec, k_spec, v_spec, b_spec = _ca_specs(n_heads, s, col0)
    blk = pl.BlockSpec((CA_ROWS, HEAD_DIM), lambda h, c: (c, h))
    full = pl.BlockSpec((s, HEAD_DIM), lambda h, c: (0, h))
    sd = jax.ShapeDtypeStruct((s, n_heads * HEAD_DIM), BF16)
    return _call(
        body, [qkv, qkv, qkv, bias, dy], name=name, grid=(n_heads, nc),
        in_specs=[q_spec, k_spec, v_spec, b_spec, blk],
        out_specs=[blk, full, full, b_spec],
        out_shape=[sd, sd, sd, jax.ShapeDtypeStruct((n_heads, CA_ROWS, CA_BAND), F32)],
        scratch_shapes=[pltpu.VMEM((s + CA_PAD, HEAD_DIM), BF16), pltpu.VMEM((s + CA_PAD, HEAD_DIM), BF16),
                        pltpu.VMEM((s + CA_PAD, HEAD_DIM), F32), pltpu.VMEM((s + CA_PAD, HEAD_DIM), F32)],
        sem=("parallel", "arbitrary"), ride=ride)


EARLY = ("w_sb_out", "w_ca_out", "w_mix_out")


def _step(x, p, target, small, comm):
    w = comm.w
    d = x.shape[1]
    n_sb = w["w_sb_out"].shape[0] // HEAD_DIM
    n_ca = w["w_ca_out"].shape[0] // HEAD_DIM
    qkv_cols = 3 * HEAD_DIM * (n_sb + n_ca)
    ca_col0 = 3 * n_sb
    both = (F32, BF16)

    h1 = _rms_fwd(x, small["g_mix"], name="rms_mix")
    ffn, ple = ("w_ffn_in",), ("w_ple_gate", "w_ple_in")
    qkv = _mm(h1, w["w_in"], "nn", (BF16,), name="proj_qkv", n=qkv_cols, ride=comm.gather(EARLY, "near"))
    gates = _mm(h1, w["w_in"], "nn", (F32,), name="proj_gates", n=2 * d, b_col_off=qkv_cols,
                ride=comm.gather(ffn, "near", comm.gather(EARLY, "far"), (0, 8)))
    bias = _group_bias(_band_bias(small["rel_bias"]))
    y_sb = _sb_fwd(qkv, n_sb, 0, name="sb_fwd", ride=comm.gather(ffn, "near", comm.gather(EARLY, "pair"), (1, 8, 7)))
    y_ca = _ca_fwd(qkv, bias, n_ca, ca_col0, name="ca_fwd", ride=comm.gather(ffn, "far"))
    out = ("w_ffn_out",)
    o_sb = _mm(y_sb, w["w_sb_out"], "nn", (F32,), name="sb_out", ride=comm.gather(out, "near", part=(0, 4)))
    o_ca = _mm(y_ca, w["w_ca_out"], "nn", (F32,), name="ca_out", ride=comm.gather(out, "near", part=(1, 4)))
    merged = _gate_merge_fwd(gates, o_sb, o_ca, name="gate_merge",
                             ride=comm.gather(out, "near", comm.gather(ffn, "pair"), (2, 4)))
    x1 = _mm(merged, w["w_mix_out"], "nn", (F32,), name="mix_out", resid=x, ride=comm.gather(out, "near", part=(3, 4)))
    h2 = _rms_fwd(x1, small["g_ffn"], name="rms_ffn")
    gu = _mm(h2, w["w_ffn_in"], "nn", (BF16,), name="ffn_in", ride=comm.gather(ple, "near", comm.gather(out, "far")))
    act = _swiglu_fwd(gu, name="swiglu", ride=comm.gather(ple, "far", comm.gather(out, "pair")))
    x2 = _mm(act, w["w_ffn_out"], "nn", (F32,), name="ffn_out", resid=x1, ride=comm.gather(ple, "pair"))
    h3 = _rms_fwd(x2, small["g_ple"], name="rms_ple")
    t = _mm(h3, w["w_ple_gate"], "nn", (F32,), name="ple_gate")
    pe = _mm(p, w["w_ple_in"], "nn", (F32,), name="ple_in")
    x3 = _ple_fwd(x2, t, pe, name="ple_add")

    gs = {}
    dx3, gs["g_final"], loss = _final_loss(x3, small["g_final"], target, name="final_loss")
    dt, dpe = _ple_bwd(dx3, t, pe, name="ple_bwd")
    comm.grad("w_ple_in", *_mm(p, dpe, "tn", both, name="dw_ple_in"))
    comm.grad("w_ple_gate", *_mm(h3, dt, "tn", both, name="dw_ple_gate"))
    ple = ("w_ple_in", "w_ple_gate")
    dh3 = _mm(dt, w["w_ple_gate"], "nt", (F32,), name="dh_ple", ride=comm.pair(ple))
    dx2, dx2_16, gs["g_ple"] = _rms_bwd(x2, small["g_ple"], dh3, dx3, name="rms_ple_bwd")
    comm.add(ple)
    comm.grad("w_ffn_out", *_mm(act, dx2_16, "tn", both, name="dw_ffn_out", ride=comm.chips(ple)))
    dact = _mm(dx2_16, w["w_ffn_out"], "nt", (F32,), name="dact", ride=comm.pair(("w_ffn_out",)))
    dgu = _swiglu_bwd(dact, gu, name="swiglu_bwd")
    comm.sum(ple)
    comm.add(("w_ffn_out",))
    comm.grad("w_ffn_in", *_mm(h2, dgu, "tn", both, name="dw_ffn_in",
                               ride=comm.share(ple, comm.chips(("w_ffn_out",)))))
    dh2 = _mm(dgu, w["w_ffn_in"], "nt", (F32,), name="dh_ffn", ride=comm.pair(("w_ffn_in",)))
    dx1, dx1_16, gs["g_ffn"] = _rms_bwd(x1, small["g_ffn"], dh2, dx2, name="rms_ffn_bwd")
    comm.add(("w_ffn_in",))
    comm.sum(("w_ffn_out",))
    comm.grad("w_mix_out", *_mm(merged, dx1_16, "tn", both, name="dw_mix_out", ride=comm.share(("w_ffn_out",))))
    dmerged = _mm(dx1_16, w["w_mix_out"], "nt", (F32,), name="dmerged", ride=comm.pair(("w_mix_out",)))
    dg_sb, dg_ca, do_sb, do_ca = _gate_merge_bwd(dmerged, gates, o_sb, o_ca, name="gate_merge_bwd")
    comm.add(("w_mix_out",))
    comm.grad("w_sb_out", *_mm(y_sb, do_sb, "tn", both, name="dw_sb_out"))
    comm.grad("w_ca_out", *_mm(y_ca, do_ca, "tn", both, name="dw_ca_out"))
    outs = ("w_sb_out", "w_ca_out")
    dy_sb = _mm(do_sb, w["w_sb_out"], "nt", (BF16,), name="dy_sb", ride=comm.pair(outs))
    dy_ca = _mm(do_ca, w["w_ca_out"], "nt", (BF16,), name="dy_ca")
    comm.add(outs)
    dq_sb, dk_sb, dv_sb = _sb_bwd(qkv, dy_sb, n_sb, 0, name="sb_bwd", ride=comm.chips(("w_ffn_in",)))
    comm.sum(("w_ffn_in",))
    late = ("w_mix_out",) + outs
    dq_ca, dk_ca, dv_ca, dbias = _ca_bwd(qkv, bias, dy_ca, n_ca, ca_col0, name="ca_bwd",
                                         ride=comm.chips(late, comm.share(("w_ffn_in",))))
    comm.sum(late)
    gs["rel_bias"] = _band_bias_grad(_group_bias_grad(dbias))
    dproj = _concat_cols([dq_sb, dk_sb, dv_sb, dq_ca, dk_ca, dv_ca, dg_sb, dg_ca], name="dproj")
    comm.grad("w_in", *_mm(h1, dproj, "tn", both, name="dw_in", ride=comm.share(late)))
    half = x.shape[0] // 2
    dh1 = _mm(dproj, w["w_in"], "nt", (F32,), name="dh_mix_top", rows=(0, half), ride=comm.pair(("w_in",)))
    comm.add(("w_in",))
    dh1 = _mm(dproj, w["w_in"], "nt", (F32,), name="dh_mix_bottom", rows=(half, half), onto=(dh1,),
              ride=comm.tail(TAIL_SECOND))
    grad_x, _, gs["g_mix"] = _rms_bwd(x, small["g_mix"], dh1, dx1, name="rms_mix_bwd", ride=comm.tail(TAIL_FIRST))
    return loss, grad_x, gs


def _position():
    x, y, c = lax.axis_index("x"), lax.axis_index("y"), lax.axis_index("c")
    chips = [(1 - x, y), (x, 1 - y), (1 - x, 1 - y)]
    return x, y, c, chips


def _aligned(v, m):
    return v if isinstance(v, int) else pl.multiple_of(v, m)


def _piece_dims(shape, axis):
    k, n = shape
    return (k // 2, n // N_CHIPS) if axis == 1 else (k // N_CHIPS // 2, n)


def _piece(ref, shape, axis, j, h, part=(0, 1)):
    pr, pc = _piece_dims(shape, axis)
    nr = pr // part[1] * (part[2] if len(part) > 2 else 1)
    r0 = part[0] * (pr // part[1])
    if axis == 1:
        return ref.at[pl.ds(_aligned(h * pr + r0, 16), nr), pl.ds(_aligned(j * pc, 128), pc)]
    return ref.at[pl.ds(_aligned((2 * j + h) * pr + r0, 16), nr), :]


def _shard_half(ref, h):
    rows = ref.shape[0] // 2
    return ref.at[pl.ds(_aligned(h * rows, 16), rows), :]


def _remote(src, dst, send_sems, recv_sems, k, to):
    return pltpu.make_async_remote_copy(src_ref=src, dst_ref=dst, send_sem=send_sems.at[k],
                                        recv_sem=recv_sems.at[k], device_id=to, device_id_type=MESH)


def _prefetch_call(body, scalars, ins, in_specs, out_shape, out_specs, grid, *, name, ride=None):
    single = not isinstance(out_shape, (list, tuple))
    outs = _call(body, ins, name=name, grid=grid, in_specs=in_specs,
                 out_specs=[out_specs] if single else out_specs, out_shape=[out_shape] if single else out_shape,
                 sem=("parallel",) * len(grid), ride=ride, scalars=scalars)
    return outs[0] if single else outs


def _slab_tiles(pr, pc):
    tc = pc if pc <= 4096 else _pick(pc, (2048, 1024, 512, 256, 128))
    tr = next(t for t in (1024, 512, 256, 128, 64, 32, 16) if pr % t == 0 and t * tc <= 512 * 1024)
    return tr, tc


def _cast_place(w, axis, pos, *, name, ride=None):
    ks, ns = w.shape
    shape = (ks, ns * N_CHIPS) if axis == 1 else (ks * N_CHIPS, ns)
    tr, tc = _slab_tiles(ks, ns)
    nr, nc = ks // tr, ns // tc

    def body(pos_ref, w_ref, o_ref):
        o_ref[...] = w_ref[...].astype(o_ref.dtype)

    if axis == 1:
        out_map = lambda i, j, pos_ref: (i, pos_ref[0] * nc + j)
    else:
        out_map = lambda i, j, pos_ref: (pos_ref[0] * nr + i, j)
    return _prefetch_call(body, pos, [w], [pl.BlockSpec((tr, tc), lambda i, j, pos_ref: (i, j))],
                          jax.ShapeDtypeStruct(shape, BF16), pl.BlockSpec((tr, tc), out_map), (nr, nc), name=name, ride=ride)


def _run(ride, *, name):
    if ride is None:
        return

    def body(o_ref):
        o_ref[...] = jnp.zeros_like(o_ref)

    _call(body, [], name=name, grid=(1,), in_specs=[], out_specs=[pl.BlockSpec((8, 128), lambda i: (0, 0))],
          out_shape=[jax.ShapeDtypeStruct((8, 128), F32)], ride=ride)


def _ride_gather(ride, w, n, axis, stage, part=(0, 1)):
    shape = w[n].shape
    piece = functools.partial(_piece, shape=shape, axis=axis)
    span = part[2] if len(part) > 2 else 1
    halves = [(2 * part[0] + t * span, 2 * part[1], span) for t in range(2)]

    def copies(ins, outs, send_sems, recv_sems, arriving):
        x, y, c, chips = _position()
        me, (xn, yn, dn) = 2 * x + y, [2 * px + py for px, py in chips]
        if stage == "near":
            plan = [(me, c, part, (1 - x, y, c), xn, c, part), (me, c, part, (x, 1 - y, c), yn, c, part)]
        elif stage == "far":
            plan = [(yn, c, halves[1], (1 - x, y, c), dn, c, halves[1]), (xn, c, halves[0], (x, 1 - y, c), dn, c, halves[0])]
        else:
            plan = [(j, c, part, (x, y, 1 - c), j, 1 - c, part) for j in (xn, yn, dn)]
        out = []
        for k, (chip, h, rows, to, from_chip, from_h, from_rows) in enumerate(plan):
            if arriving:
                lands = piece(outs[0], j=from_chip, h=from_h, part=from_rows)
                out.append(_remote(lands, lands, send_sems, recv_sems, k, to))
            else:
                out.append(_remote(piece(ins[0], j=chip, h=h, part=rows), piece(outs[0], j=chip, h=h, part=rows),
                                   send_sems, recv_sems, k, to))
        return out

    def start(*refs):
        for cp in copies(*refs, arriving=False):
            cp.start()

    def finish(*refs):
        for cp in copies(*refs, arriving=True):
            cp.wait_recv()
        for cp in copies(*refs, arriving=False):
            cp.wait_send()

    ride.add([w[n]], [jax.ShapeDtypeStruct(shape, w[n].dtype)], {0: 0}, 3, start, finish,
             lambda outs: w.__setitem__(n, outs[0]))


def _ride_pair(ride, st, axis):
    shape = st["g16"].shape
    pr, pc = _piece_dims(shape, axis)

    def copies(ins, outs, send_sems, recv_sems):
        x, y, c, _ = _position()
        return [_remote(_piece(ins[0], shape, axis, j, 1 - c), outs[0].at[j], send_sems, recv_sems, j, (x, y, 1 - c))
                for j in range(N_CHIPS)]

    def start(*refs):
        for cp in copies(*refs):
            cp.start()

    def finish(*refs):
        for cp in copies(*refs):
            cp.wait()

    ride.add([st["g16"]], [jax.ShapeDtypeStruct((N_CHIPS, pr, pc), BF16)], {}, N_CHIPS, start, finish,
             lambda outs: st.__setitem__("sib", outs[0]))


def _ride_chips(ride, st, rows=None):
    _, pr, pc = st["s16"].shape
    r0, nr = (0, pr) if rows is None else rows

    def copies(ins, outs, send_sems, recv_sems):
        x, y, c, chips = _position()
        return [_remote(ins[0].at[2 * px + py, pl.ds(r0, nr), :], outs[0].at[k, pl.ds(r0, nr), :],
                        send_sems, recv_sems, k, (px, py, c)) for k, (px, py) in enumerate(chips)]

    def start(*refs):
        for cp in copies(*refs):
            cp.start()

    def finish(*refs):
        for cp in copies(*refs):
            cp.wait()

    ins, aliases = ([st["s16"], st["recv"]], {1: 0}) if "recv" in st else ([st["s16"]], {})
    ride.add(ins, [jax.ShapeDtypeStruct((3, pr, pc), BF16)], aliases, 3, start, finish,
             lambda outs: st.__setitem__("recv", outs[0]))


def _ride_share(ride, st):
    def sent(ins, outs, send_sems, recv_sems):
        x, y, c, _ = _position()
        return _remote(_shard_half(ins[0], c), _shard_half(outs[0], c), send_sems, recv_sems, 0, (x, y, 1 - c))

    def landed(ins, outs, send_sems, recv_sems):
        x, y, c, _ = _position()
        other = _shard_half(outs[0], 1 - c)
        return _remote(other, other, send_sems, recv_sems, 0, (x, y, 1 - c))

    def start(*refs):
        sent(*refs).start()

    def finish(*refs):
        landed(*refs).wait_recv()
        sent(*refs).wait_send()

    ride.add([st["shard"]], [jax.ShapeDtypeStruct(st["shard"].shape, F32)], {0: 0}, 1, start, finish,
             lambda outs: st.__setitem__("g", outs[0]))


def _piece_block(axis, nr, nc, chip):
    if axis == 1:
        return lambda *a: ((a[-1][1] * nr + a[-3]), (a[0] if chip is None else chip(a[-1])) * nc + a[-2])
    return lambda *a: ((2 * (a[0] if chip is None else chip(a[-1])) + a[-1][1]) * nr + a[-3], a[-2])


def _pair_add(g32, sib, axis, pos, *, name):
    _, pr, pc = sib.shape
    tr, tc = _slab_tiles(pr, pc)
    nr, nc = pr // tr, pc // tc

    def body(pos_ref, g_ref, b_ref, o16_ref):
        o16_ref[0] = (g_ref[...] + b_ref[0].astype(F32)).astype(o16_ref.dtype)

    blk = pl.BlockSpec((1, tr, tc), lambda j, i, k, pos_ref: (j, i, k))
    return _prefetch_call(body, pos, [g32, sib], [pl.BlockSpec((tr, tc), _piece_block(axis, nr, nc, None)), blk],
                          jax.ShapeDtypeStruct(sib.shape, BF16), blk, (N_CHIPS, nr, nc), name=name)


def _chip_sum(g32, sib, recv, axis, pos, *, name):
    _, pr, pc = sib.shape
    tr, tc = _slab_tiles(pr, pc)
    nr, nc = pr // tr, pc // tc

    def body(pos_ref, g_ref, b_ref, r_ref, o_ref):
        pair = g_ref[...] + b_ref[0].astype(F32)
        o_ref[...] = ((pair + r_ref[0].astype(F32)) + r_ref[1].astype(F32)) + r_ref[2].astype(F32)

    return _prefetch_call(
        body, pos, [g32, sib, recv],
        [pl.BlockSpec((tr, tc), _piece_block(axis, nr, nc, lambda pos_ref: pos_ref[0])),
         pl.BlockSpec((1, tr, tc), lambda i, k, pos_ref: (pos_ref[0], i, k)),
         pl.BlockSpec((3, tr, tc), lambda i, k, pos_ref: (0, i, k))],
        jax.ShapeDtypeStruct((2 * pr, pc), F32),
        pl.BlockSpec((tr, tc), lambda i, k, pos_ref: (pos_ref[1] * nr + i, k)), (nr, nc), name=name)


class _Comm:
    def __init__(self, pos, w):
        self.pos, self.w, self.st = pos, w, {n: {} for n, _ in BIG}

    def gather(self, names, stage, ride=None, part=(0, 1)):
        ride = _Ride() if ride is None else ride
        for n in names:
            _ride_gather(ride, self.w, n, AXIS[n], stage, part)
        return ride

    def grad(self, n, g32, g16):
        self.st[n].update(g32=g32, g16=g16)

    def pair(self, names, ride=None):
        ride = _Ride() if ride is None else ride
        for n in names:
            _ride_pair(ride, self.st[n], AXIS[n])
        return ride

    def add(self, names):
        for n in names:
            st = self.st[n]
            st["s16"] = _pair_add(st["g32"], st["sib"], AXIS[n], self.pos, name="rs_add_" + n)

    def chips(self, names, ride=None, rows=None):
        ride = _Ride() if ride is None else ride
        for n in names:
            _ride_chips(ride, self.st[n], rows)
        return ride

    def sum(self, names):
        for n in names:
            st = self.st[n]
            st["shard"] = _chip_sum(st["g32"], st["sib"], st["recv"], AXIS[n], self.pos, name="rs_sum_" + n)

    def share(self, names, ride=None):
        ride = _Ride() if ride is None else ride
        for n in names:
            _ride_share(ride, self.st[n])
        return ride

    def tail(self, count):
        st = self.st["w_in"]
        rows, at = st["s16"].shape[1], st.get("at", 0)
        st["at"] = at + count
        return self.chips(("w_in",), rows=(at * rows // TAIL_PARTS, count * rows // TAIL_PARTS))

    def tail_rest(self):
        return self.tail(TAIL_PARTS - self.st["w_in"].get("at", 0))

    def result(self, n):
        return self.st[n]["g"]


class _NoComm:
    def __init__(self, w):
        self.w, self.st = w, {}

    def grad(self, n, g32, g16):
        self.st[n] = (g32, g16)

    def result(self, n):
        return self.st[n]

    def add(self, names):
        pass

    sum = add

    def gather(self, names, *args, **kwargs):
        return None

    pair = chips = share = tail = gather


def _small_all_reduce(vec, *, name):
    r = vec.shape[0]

    def body(vec_ref, out_ref, slots, send_sems, recv_sems):
        x, y, c, _ = _position()
        me = 4 * x + 2 * y + c
        slots[me] = vec_ref[...]
        sends = []
        for k in range(1, 8):
            to = (x ^ (k >> 2), y ^ ((k >> 1) & 1), c ^ (k & 1))
            cp = _remote(slots.at[me], slots.at[me], send_sems, recv_sems, k - 1, to)
            cp.start()
            sends.append(cp)
        for k in range(1, 8):
            frm = 4 * (x ^ (k >> 2)) + 2 * (y ^ ((k >> 1) & 1)) + (c ^ (k & 1))
            _remote(slots.at[frm], slots.at[frm], send_sems, recv_sems, k - 1, (x, y, c)).wait_recv()
        for cp in sends:
            cp.wait_send()
        total = slots[0]
        for d in range(1, 8):
            total = total + slots[d]
        out_ref[...] = total

    return pl.pallas_call(
        body, name=name,
        in_specs=[pl.BlockSpec(memory_space=pltpu.VMEM)], out_specs=pl.BlockSpec(memory_space=pltpu.VMEM),
        out_shape=jax.ShapeDtypeStruct((r, 128), F32),
        scratch_shapes=[pltpu.VMEM((8, r, 128), F32), pltpu.SemaphoreType.DMA((7,)), pltpu.SemaphoreType.DMA((7,))],
    )(vec)


def _adamw(w, g, m, v, *, name, ride=None):
    r, c = w.shape
    tc = c if c <= 4096 else _pick(c, (2048, 1024, 512, 256, 128))
    tr = next(t for t in (512, 256, 128, 64, 32, 16, 8) if r % t == 0 and t * tc <= 256 * 1024)

    def body(w_ref, g_ref, m_ref, v_ref, d_ref, nm_ref, nv_ref):
        gv = g_ref[...]
        nm = ADAM_B1 * m_ref[...] + (1.0 - ADAM_B1) * gv
        nv = ADAM_B2 * v_ref[...] + (1.0 - ADAM_B2) * (gv * gv)
        m_hat = nm / (1.0 - ADAM_B1 ** ADAM_STEP)
        v_hat = nv / (1.0 - ADAM_B2 ** ADAM_STEP)
        d_ref[...] = -ADAM_LR * (m_hat / (jnp.sqrt(v_hat) + ADAM_EPS) + ADAM_WD * w_ref[...])
        nm_ref[...] = nm
        nv_ref[...] = nv

    blk = ((tr, tc), lambda i, j: (i, j))
    sd = jax.ShapeDtypeStruct((r, c), F32)
    return _ew(body, [w, g, m, v], [blk] * 4, [sd, sd, sd], [blk] * 3, (r // tr, c // tc), name=name, ride=ride)


BIG = (("w_in", 1), ("w_sb_out", 1), ("w_ca_out", 1), ("w_mix_out", 0), ("w_ffn_in", 1), ("w_ffn_out", 0),
       ("w_ple_in", 1), ("w_ple_gate", 0))
AXIS = dict(BIG)
HEAD_PARTS = 8
HEAD_HOSTS = ("w_ffn_in", "w_ffn_out")
TAIL_PARTS = 16
TAIL_SECOND = 4
TAIL_FIRST = 2
TAIL_HOSTS = {"w_ffn_in": 4, "w_ffn_out": 2, "w_mix_out": 1, "w_ple_gate": 1}
SMALL = ("rel_bias", "g_mix", "g_ffn", "g_ple", "g_final")
ORDER = ("w_in", "w_sb_out", "w_ca_out", "w_mix_out", "rel_bias", "g_mix", "g_ffn", "g_ple", "g_final",
         "w_ffn_in", "w_ffn_out", "w_ple_in", "w_ple_gate")


def _pack(parts):
    flat = jnp.concatenate([a.reshape(-1) for a in parts])
    rows = -(-flat.shape[0] // 1024) * 8
    return jnp.pad(flat, (0, rows * 128 - flat.shape[0])).reshape(rows, 128)


def _unpack(packed, like):
    flat, out, at = packed.reshape(-1), [], 0
    for a in like:
        out.append(flat[at:at + a.size].reshape(a.shape))
        at += a.size
    return out


def kernel(x, p, w_in, w_sb_out, w_ca_out, w_mix_out, rel_bias, g_mix, g_ffn, g_ple, g_final, w_ffn_in, w_ffn_out, w_ple_in, w_ple_gate, loss_target, m_w_in, m_w_sb_out, m_w_ca_out, m_w_mix_out, m_rel_bias, m_g_mix, m_g_ffn, m_g_ple, m_g_final, m_w_ffn_in, m_w_ffn_out, m_w_ple_in, m_w_ple_gate, v_w_in, v_w_sb_out, v_w_ca_out, v_w_mix_out, v_rel_bias, v_g_mix, v_g_ffn, v_g_ple, v_g_final, v_w_ffn_in, v_w_ffn_out, v_w_ple_in, v_w_ple_gate):
    weights = dict(w_in=w_in, w_sb_out=w_sb_out, w_ca_out=w_ca_out, w_mix_out=w_mix_out, rel_bias=rel_bias,
                   g_mix=g_mix, g_ffn=g_ffn, g_ple=g_ple, g_final=g_final, w_ffn_in=w_ffn_in,
                   w_ffn_out=w_ffn_out, w_ple_in=w_ple_in, w_ple_gate=w_ple_gate)
    m_in = dict(w_in=m_w_in, w_sb_out=m_w_sb_out, w_ca_out=m_w_ca_out, w_mix_out=m_w_mix_out, rel_bias=m_rel_bias,
                g_mix=m_g_mix, g_ffn=m_g_ffn, g_ple=m_g_ple, g_final=m_g_final, w_ffn_in=m_w_ffn_in,
                w_ffn_out=m_w_ffn_out, w_ple_in=m_w_ple_in, w_ple_gate=m_w_ple_gate)
    v_in = dict(w_in=v_w_in, w_sb_out=v_w_sb_out, w_ca_out=v_w_ca_out, w_mix_out=v_w_mix_out, rel_bias=v_rel_bias,
                g_mix=v_g_mix, g_ffn=v_g_ffn, g_ple=v_g_ple, g_final=v_g_final, w_ffn_in=v_w_ffn_in,
                w_ffn_out=v_w_ffn_out, w_ple_in=v_w_ple_in, w_ple_gate=v_w_ple_gate)

    pos = jnp.stack([2 * lax.axis_index("x") + lax.axis_index("y"), lax.axis_index("c")]).astype(jnp.int32)
    comm = _Comm(pos, {"w_in": _cast_place(w_in[0], AXIS["w_in"], pos, name="cast_w_in")})
    at = 0
    for n in HEAD_HOSTS:
        ride = comm.gather(("w_in",), "near", part=(at, HEAD_PARTS))
        comm.w[n] = _cast_place(weights[n][0], AXIS[n], pos, name="cast_" + n, ride=ride)
        at += 1
    for n, axis in BIG:
        if n not in comm.w:
            comm.w[n] = _cast_place(weights[n][0], axis, pos, name="cast_" + n)
    _run(comm.gather(("w_in",), "near", part=(at, HEAD_PARTS, HEAD_PARTS - at)), name="gather_w_in_near")
    _run(comm.gather(("w_in",), "far"), name="gather_w_in_far")
    _run(comm.gather(("w_in",), "pair"), name="gather_w_in_pair")
    small = dict(rel_bias=rel_bias[0], g_mix=g_mix, g_ffn=g_ffn, g_ple=g_ple, g_final=g_final.reshape(1, -1))
    loss, grad_x, gs = _step(x[0], p[0, 0], loss_target[0], small, comm)

    grads, delta, new_m, new_v = {}, {}, {}, {}
    for n in [n for n, _ in BIG if n != "w_in"] + ["w_in"]:
        ride = comm.tail(TAIL_HOSTS[n]) if n in TAIL_HOSTS else None
        if n == "w_in":
            _run(comm.tail_rest(), name="rs_chips_w_in")
            comm.sum(("w_in",))
            _run(comm.share(("w_in",)), name="rs_share_w_in")
        g = comm.result(n)
        d, nm, nv = _adamw(weights[n][0], g, m_in[n][0], v_in[n][0], name="adamw_" + n, ride=ride)
        grads[n], delta[n], new_m[n], new_v[n] = g[None], d[None], nm[None], nv[None]

    like = [weights[n] for n in SMALL]
    reduced = _small_all_reduce(_pack([gs[n] for n in SMALL] + [loss[:, :1]]), name="small_all_reduce")
    g_small = _unpack(reduced, like + [loss[:, :1]])
    total_loss = g_small[-1].reshape(())
    g_packed = _pack(g_small[:-1])
    d_s, m_s, v_s = _adamw(_pack(like), g_packed, _pack([m_in[n] for n in SMALL]), _pack([v_in[n] for n in SMALL]),
                           name="adamw_small")
    for n, g, d, nm, nv in zip(SMALL, g_small[:-1], _unpack(d_s, like), _unpack(m_s, like), _unpack(v_s, like)):
        grads[n], delta[n], new_m[n], new_v[n] = g, d, nm, nv

    return (total_loss, grad_x[None], *[grads[n] for n in ORDER], *[delta[n] for n in ORDER],
            *[new_m[n] for n in ORDER], *[new_v[n] for n in ORDER])
```

```python
import functools
import math

import jax
import jax.numpy as jnp
import numpy as np
from jax import lax
from jax.experimental import pallas as pl
from jax.experimental.pallas import tpu as pltpu
from jax.experimental.pallas import tpu_sc as plsc

F32 = jnp.float32
BF16 = jnp.bfloat16

HEAD_DIM = 128
CHUNK = 64
LEFT_CHUNKS = 8
REL_CLIP = 128
N_REL = REL_CLIP + CHUNK
BAND = (LEFT_CHUNKS + 2) * CHUNK
CA_PER_STEP = 4
CA_ROWS = CA_PER_STEP * CHUNK
CA_BAND = BAND + CA_PER_STEP * CHUNK
CA_PAD = BAND
SB_BLOCK = 128
SB_KEYS = 512
SB_GROUPS = SB_KEYS // SB_BLOCK
SB_ROWS = SB_KEYS
EPS = 1e-6
NEG = -1e30

ADAM_LR = 0.001
ADAM_B1 = 0.9
ADAM_B2 = 0.999
ADAM_EPS = 1e-08
ADAM_WD = 0.01
ADAM_STEP = 10

VMEM_LIMIT = 48 * 1024 * 1024
MM_VMEM_BUDGET = 36 * 1024 * 1024
V7X_HBM_BYTES_PER_S = 3.7e12
GRID_STEP_S = 0.35e-6
MESH = pl.DeviceIdType.MESH
N_CHIPS = 4


def _pick(dim, prefs):
    for t in prefs:
        if dim % t == 0:
            return t
    raise ValueError(f"no tile for {dim}")


def _cparams(sem=None):
    return pltpu.CompilerParams(dimension_semantics=sem, vmem_limit_bytes=VMEM_LIMIT)


def _sigmoid(v):
    return 1.0 / (1.0 + jnp.exp(-v))


def _dot(a, b, dims):
    return lax.dot_general(a, b, (dims, ((), ())), preferred_element_type=F32)


def _dot_nn(a, b):
    return _dot(a, b, ((1,), (0,)))


def _dot_nt(a, b):
    return _dot(a, b, ((1,), (1,)))


def _dot_tn(a, b):
    return _dot(a, b, ((0,), (0,)))


HBM = pl.BlockSpec(memory_space=pltpu.HBM)


class _Ride:
    def __init__(self):
        self.items = []

    def add(self, ins, outs, aliases, n_sems, start, finish, sink):
        self.items.append((ins, outs, aliases, n_sems, start, finish, sink))


def _call(body, args, *, name, grid, in_specs, out_specs, out_shape, scratch_shapes=(), sem=None, ride=None,
          scalars=None, onto=()):
    items = ride.items if ride is not None else []
    if onto:
        args, in_specs = list(args) + list(onto), list(in_specs) + [HBM] * len(onto)
        inner, body = body, lambda *refs: inner(*refs[:len(args) - len(onto)], *refs[len(args):])
    n_in, n_out, n_scr = len(args), len(out_shape), len(scratch_shapes)
    r_ins = [a for it in items for a in it[0]]
    r_outs = [o for it in items for o in it[1]]
    updated = [id(it[0][i]) for it in items for i in it[2]]
    assert len(set(updated)) == len(updated), "one call may update a buffer in place only once"
    aliases, a, b = {n_in - len(onto) + t: t for t in range(len(onto))}, n_in, n_out
    for it in items:
        aliases.update({a + i: b + o for i, o in it[2].items()})
        a, b = a + len(it[0]), b + len(it[1])
    sems = [pltpu.SemaphoreType.DMA((it[3],)) for it in items for _ in range(2)]

    def wrapped(*refs):
        head, refs = (refs[:1], refs[1:]) if scalars is not None else ((), refs)
        ins, rin = refs[:n_in], refs[n_in:n_in + len(r_ins)]
        at = n_in + len(r_ins)
        outs, rout = refs[at:at + n_out], refs[at + n_out:at + n_out + len(r_outs)]
        at += n_out + len(r_outs)
        scr, rsem = refs[at:at + n_scr], refs[at + n_scr:]

        def each(which):
            a = b = 0
            for q, it in enumerate(items):
                it[which](rin[a:a + len(it[0])], rout[b:b + len(it[1])], rsem[2 * q], rsem[2 * q + 1])
                a, b = a + len(it[0]), b + len(it[1])

        if items:
            ids = [pl.program_id(d) for d in range(len(grid))]
            first = functools.reduce(jnp.logical_and, [i == 0 for i in ids])
            last = functools.reduce(jnp.logical_and, [i == g - 1 for i, g in zip(ids, grid)])
            pl.when(first)(lambda: each(4))
        body(*head, *ins, *outs, *scr)
        if items:
            pl.when(last)(lambda: each(5))

    specs = dict(grid=grid, in_specs=list(in_specs) + [HBM] * len(r_ins),
                 out_specs=list(out_specs) + [HBM] * len(r_outs), scratch_shapes=list(scratch_shapes) + sems)
    if scalars is not None:
        specs = dict(grid_spec=pltpu.PrefetchScalarGridSpec(num_scalar_prefetch=1, **specs))
        aliases = {i + 1: o for i, o in aliases.items()}
    res = pl.pallas_call(
        wrapped, name=name, **specs,
        out_shape=list(out_shape) + r_outs,
        input_output_aliases=aliases,
        compiler_params=_cparams(("arbitrary",) * len(grid) if items else sem),
    )(*(() if scalars is None else (scalars,)), *args, *r_ins)
    b = n_out
    for it in items:
        it[6](res[b:b + len(it[1])])
        b += len(it[1])
    return list(res[:n_out])


def _mm_tiles(m, n_align, n, k, a_bytes, b_bytes, out_bytes):
    best = None
    tks = sorted({t for t in (k, k // 2, k // 4, 2048, 1024, 512, 256, 128) if t <= k and k % t == 0 and t % 128 == 0})
    for tm in (t for t in (2048, 1024, 512, 256, 128) if m % t == 0):
        for tn in (t for t in (2048, 1024, 512, 256, 128) if n_align % t == 0):
            for tk in tks:
                nk = k // tk
                vmem = 2 * (tm * tk * a_bytes + tk * tn * b_bytes + tm * tn * out_bytes) + tm * tn * 4
                if vmem > MM_VMEM_BUDGET:
                    continue
                traffic = m * k * a_bytes * (n // tn if nk > 1 else 1) + k * n * b_bytes * (m // tm)
                traffic += tm * tk * a_bytes + tk * tn * b_bytes + tm * tn * out_bytes
                traffic += m * n * 4 * nk if nk > 1 else 0
                cost = traffic / V7X_HBM_BYTES_PER_S + (m // tm) * (n // tn) * nk * GRID_STEP_S
                if best is None or cost < best[0]:
                    best = (cost, tm, tn, tk)
    return best[1:]


def _mm(a, b, mode, out_dtypes, *, name, n=None, b_col_off=0, resid=None, ride=None, rows=None, onto=()):
    if mode == "nn":
        m, k = a.shape
        n = b.shape[1] if n is None else n
    elif mode == "nt":
        m, k = a.shape
        n = b.shape[0]
    else:
        k, m = a.shape
        n = b.shape[1]
    m_all, (row0, m) = m, (0, m) if rows is None else rows
    n_out = len(out_dtypes)
    has_resid = resid is not None
    out_bytes = sum(jnp.dtype(dt).itemsize for dt in out_dtypes) + (4 if has_resid else 0)
    tm, tn, tk = _mm_tiles(math.gcd(m, row0) if row0 else m, math.gcd(n, b_col_off) if b_col_off else n, n, k,
                           a.dtype.itemsize, b.dtype.itemsize, out_bytes)
    nk = k // tk
    boff, roff = b_col_off // tn, row0 // tm
    dot = {"nn": _dot_nn, "nt": _dot_nt, "tn": _dot_tn}[mode]

    def body(*refs):
        a_ref, b_ref = refs[0], refs[1]
        r_ref = refs[2] if has_resid else None
        o_refs = refs[2 + has_resid: 2 + has_resid + n_out]

        def finish(r):
            if has_resid:
                r = r + r_ref[...]
            for o_ref in o_refs:
                o_ref[...] = r.astype(o_ref.dtype)

        part = dot(a_ref[...].astype(BF16), b_ref[...].astype(BF16))
        if nk == 1:
            finish(part)
            return
        acc_ref = refs[-1]
        kk = pl.program_id(2)

        @pl.when(kk == 0)
        def _():
            acc_ref[...] = part

        @pl.when(kk > 0)
        def _():
            acc_ref[...] += part

        @pl.when(kk == nk - 1)
        def _():
            finish(acc_ref[...])

    if mode == "nn":
        a_spec = pl.BlockSpec((tm, tk), lambda i, j, kk: (i + roff, kk))
        b_spec = pl.BlockSpec((tk, tn), lambda i, j, kk: (kk, j + boff))
    elif mode == "nt":
        a_spec = pl.BlockSpec((tm, tk), lambda i, j, kk: (i + roff, kk))
        b_spec = pl.BlockSpec((tn, tk), lambda i, j, kk: (j, kk))
    else:
        a_spec = pl.BlockSpec((tk, tm), lambda i, j, kk: (kk, i))
        b_spec = pl.BlockSpec((tk, tn), lambda i, j, kk: (kk, j))
    o_spec = pl.BlockSpec((tm, tn), lambda i, j, kk: (i + roff, j))
    in_specs = [a_spec, b_spec] + ([o_spec] if has_resid else [])
    args = [a, b] + ([resid] if has_resid else [])
    outs = _call(
        body, args, name=name,
        grid=(m // tm, n // tn, nk),
        in_specs=in_specs,
        out_specs=[o_spec] * n_out,
        out_shape=[jax.ShapeDtypeStruct((m_all, n), dt) for dt in out_dtypes],
        scratch_shapes=[pltpu.VMEM((tm, tn), F32)] if nk > 1 else [],
        sem=("parallel", "parallel", "arbitrary"), ride=ride, onto=onto)
    return outs[0] if n_out == 1 else tuple(outs)


def _row_tile(s):
    return _pick(s, (256, 128))


def _rms_fwd(x, g, *, name, ride=None):
    s, d = x.shape
    tr = _row_tile(s)

    def body(x_ref, g_ref, o_ref):
        xv = x_ref[...]
        r = lax.rsqrt(jnp.mean(xv * xv, axis=1, keepdims=True) + EPS)
        o_ref[...] = (xv * r * g_ref[...]).astype(o_ref.dtype)

    return _call(
        body, [x, g], name=name, grid=(s // tr,),
        in_specs=[pl.BlockSpec((tr, d), lambda i: (i, 0)), pl.BlockSpec((1, d), lambda i: (0, 0))],
        out_specs=[pl.BlockSpec((tr, d), lambda i: (i, 0))],
        out_shape=[jax.ShapeDtypeStruct((s, d), BF16)], sem=("parallel",), ride=ride)[0]


def _rms_bwd(x, g, dh, dres, *, name, ride=None):
    s, d = x.shape
    tr = _row_tile(s)

    def body(x_ref, g_ref, dh_ref, dres_ref, dx_ref, dx16_ref, dg_ref):
        i = pl.program_id(0)
        xv = x_ref[...]
        r = lax.rsqrt(jnp.mean(xv * xv, axis=1, keepdims=True) + EPS)
        xhat = xv * r
        dhv = dh_ref[...]
        dxhat = dhv * g_ref[...]
        proj = jnp.mean(dxhat * xhat, axis=1, keepdims=True)
        dx = dres_ref[...] + r * (dxhat - xhat * proj)
        dx_ref[...] = dx
        dx16_ref[...] = dx.astype(dx16_ref.dtype)

        @pl.when(i == 0)
        def _():
            dg_ref[...] = jnp.zeros_like(dg_ref)

        dg_ref[...] += jnp.sum(dhv * xhat, axis=0, keepdims=True)

    row = pl.BlockSpec((tr, d), lambda i: (i, 0))
    vec = pl.BlockSpec((1, d), lambda i: (0, 0))
    return _call(
        body, [x, g, dh, dres], name=name, grid=(s // tr,),
        in_specs=[row, vec, row, row],
        out_specs=[row, row, vec],
        out_shape=[jax.ShapeDtypeStruct((s, d), F32), jax.ShapeDtypeStruct((s, d), BF16),
                   jax.ShapeDtypeStruct((1, d), F32)],
        sem=("arbitrary",), ride=ride)


def _final_loss(x, g, target, *, name):
    s, d = x.shape
    tr = _row_tile(s)

    def body(x_ref, g_ref, t_ref, dx_ref, dg_ref, loss_ref):
        i = pl.program_id(0)
        xv = x_ref[...]
        gv = g_ref[...]
        r = lax.rsqrt(jnp.mean(xv * xv, axis=1, keepdims=True) + EPS)
        xhat = xv * r
        err = xhat * gv - t_ref[...]
        dy = err * (1.0 / d)
        dxhat = dy * gv
        proj = jnp.mean(dxhat * xhat, axis=1, keepdims=True)
        dx_ref[...] = r * (dxhat - xhat * proj)

        @pl.when(i == 0)
        def _():
            dg_ref[...] = jnp.zeros_like(dg_ref)
            loss_ref[...] = jnp.zeros_like(loss_ref)

        dg_ref[...] += jnp.sum(dy * xhat, axis=0, keepdims=True)
        part = 0.5 * jnp.sum(jnp.mean(err * err, axis=1, keepdims=True), axis=0, keepdims=True)
        loss_ref[...] += jnp.broadcast_to(part, loss_ref.shape)

    row = pl.BlockSpec((tr, d), lambda i: (i, 0))
    vec = pl.BlockSpec((1, d), lambda i: (0, 0))
    return pl.pallas_call(
        body, name=name, grid=(s // tr,),
        in_specs=[row, vec, row],
        out_specs=[row, vec, pl.BlockSpec((1, 128), lambda i: (0, 0))],
        out_shape=[jax.ShapeDtypeStruct((s, d), F32), jax.ShapeDtypeStruct((1, d), F32),
                   jax.ShapeDtypeStruct((1, 128), F32)],
        compiler_params=_cparams(("arbitrary",)),
    )(x, g, target)


def _ew(body, ins, in_blocks, outs, out_blocks, grid, *, name, ride=None):
    return _call(body, ins, name=name, grid=grid,
                 in_specs=[pl.BlockSpec(bs, im) for bs, im in in_blocks],
                 out_specs=[pl.BlockSpec(bs, im) for bs, im in out_blocks],
                 out_shape=outs, sem=("parallel",) * len(grid), ride=ride)


def _gate_merge_fwd(gates, o_sb, o_ca, *, name, ride=None):
    s, d = o_sb.shape
    tr, tc = _row_tile(s), _pick(d, (1024, 512, 256, 128))
    nc = d // tc

    def body(gs_ref, gc_ref, os_ref, oc_ref, m_ref):
        m = _sigmoid(gs_ref[...]) * os_ref[...] + _sigmoid(gc_ref[...]) * oc_ref[...]
        m_ref[...] = m.astype(m_ref.dtype)

    blk = ((tr, tc), lambda i, j: (i, j))
    return _ew(body, [gates, gates, o_sb, o_ca],
               [blk, ((tr, tc), lambda i, j: (i, j + nc)), blk, blk],
               [jax.ShapeDtypeStruct((s, d), BF16)], [blk], (s // tr, nc), name=name, ride=ride)[0]


def _gate_merge_bwd(dmerged, gates, o_sb, o_ca, *, name):
    s, d = o_sb.shape
    tr, tc = _row_tile(s), _pick(d, (1024, 512, 256, 128))
    nc = d // tc

    def body(dm_ref, gs_ref, gc_ref, os_ref, oc_ref, dgs_ref, dgc_ref, dos_ref, doc_ref):
        dm = dm_ref[...]
        ss = _sigmoid(gs_ref[...])
        sc = _sigmoid(gc_ref[...])
        dgs_ref[...] = (dm * os_ref[...] * ss * (1.0 - ss)).astype(dgs_ref.dtype)
        dgc_ref[...] = (dm * oc_ref[...] * sc * (1.0 - sc)).astype(dgc_ref.dtype)
        dos_ref[...] = (dm * ss).astype(dos_ref.dtype)
        doc_ref[...] = (dm * sc).astype(doc_ref.dtype)

    blk = ((tr, tc), lambda i, j: (i, j))
    sd = jax.ShapeDtypeStruct((s, d), BF16)
    return _ew(body, [dmerged, gates, gates, o_sb, o_ca],
               [blk, blk, ((tr, tc), lambda i, j: (i, j + nc)), blk, blk],
               [sd, sd, sd, sd], [blk, blk, blk, blk], (s // tr, nc), name=name)


def _swiglu_fwd(gu, *, name, ride=None):
    s, f2 = gu.shape
    f = f2 // 2
    tr, tc = 128, _pick(f, (512, 256, 128))

    def body(gu_ref, a_ref):
        for at in range(0, f, tc):
            gv = gu_ref[:, at:at + tc].astype(F32)
            a_ref[:, at:at + tc] = (gv * _sigmoid(gv) * gu_ref[:, f + at:f + at + tc].astype(F32)).astype(a_ref.dtype)

    row = lambda i: (i, 0)
    return _ew(body, [gu], [((tr, f2), row)], [jax.ShapeDtypeStruct((s, f), BF16)], [((tr, f), row)],
               (s // tr,), name=name, ride=ride)[0]


def _swiglu_bwd(dact, gu, *, name):
    s, f2 = gu.shape
    f = f2 // 2
    tr, tc = 128, _pick(f, (512, 256, 128))

    def body(da_ref, gu_ref, o_ref):
        for at in range(0, f, tc):
            da = da_ref[:, at:at + tc]
            gv = gu_ref[:, at:at + tc].astype(F32)
            sg = _sigmoid(gv)
            uv = gu_ref[:, f + at:f + at + tc].astype(F32)
            o_ref[:, at:at + tc] = (da * uv * sg * (1.0 + gv * (1.0 - sg))).astype(o_ref.dtype)
            o_ref[:, f + at:f + at + tc] = (da * gv * sg).astype(o_ref.dtype)

    row = lambda i: (i, 0)
    return _ew(body, [dact, gu], [((tr, f), row), ((tr, f2), row)], [jax.ShapeDtypeStruct((s, f2), BF16)],
               [((tr, f2), row)], (s // tr,), name=name)[0]


def _concat_cols(parts, *, name):
    s = parts[0].shape[0]
    widths = [p.shape[1] for p in parts]
    tr = 256

    def body(*refs):
        o_ref, at = refs[-1], 0
        for p_ref, width in zip(refs, widths):
            o_ref[:, at:at + width] = p_ref[...]
            at += width

    row = lambda i: (i, 0)
    return _ew(body, list(parts), [((tr, width), row) for width in widths],
               [jax.ShapeDtypeStruct((s, sum(widths)), parts[0].dtype)], [((tr, sum(widths)), row)],
               (s // tr,), name=name)[0]


def _ple_fwd(x, t, pe, *, name):
    s, d = x.shape
    tr, tc = _row_tile(s), _pick(d, (1024, 512, 256, 128))

    def body(x_ref, t_ref, p_ref, o_ref):
        o_ref[...] = x_ref[...] + _sigmoid(t_ref[...]) * p_ref[...]

    blk = ((tr, tc), lambda i, j: (i, j))
    return _ew(body, [x, t, pe], [blk, blk, blk],
               [jax.ShapeDtypeStruct((s, d), F32)], [blk], (s // tr, d // tc), name=name)[0]


def _ple_bwd(dx, t, pe, *, name):
    s, d = dx.shape
    tr, tc = _row_tile(s), _pick(d, (1024, 512, 256, 128))

    def body(dx_ref, t_ref, p_ref, dt_ref, dp_ref):
        dxv = dx_ref[...]
        sg = _sigmoid(t_ref[...])
        dt_ref[...] = (dxv * p_ref[...] * sg * (1.0 - sg)).astype(dt_ref.dtype)
        dp_ref[...] = (dxv * sg).astype(dp_ref.dtype)

    blk = ((tr, tc), lambda i, j: (i, j))
    sd = jax.ShapeDtypeStruct((s, d), BF16)
    return _ew(body, [dx, t, pe], [blk, blk, blk], [sd, sd], [blk, blk], (s // tr, d // tc), name=name)


def _sb_tri(later):
    row = lax.broadcasted_iota(jnp.int32, (SB_BLOCK, SB_BLOCK), 0)
    col = lax.broadcasted_iota(jnp.int32, (SB_BLOCK, SB_BLOCK), 1)
    tri = (row > col) if later else (row < col)
    return jnp.concatenate([tri.astype(BF16), jnp.ones((SB_BLOCK, SB_BLOCK), BF16)], axis=1)


def _sb_valid(i, j, own):
    if not own:
        return None
    qi = i * SB_ROWS + lax.broadcasted_iota(jnp.int32, (SB_ROWS, SB_KEYS), 0)
    ki = j * SB_KEYS + lax.broadcasted_iota(jnp.int32, (SB_ROWS, SB_KEYS), 1)
    return ki < qi


def _sb_scan(v, tri, run, later):
    hi = v.astype(BF16)
    lo = (v - hi.astype(F32)).astype(BF16)
    outs = [None] * SB_GROUPS
    for b in (reversed(range(SB_GROUPS)) if later else range(SB_GROUPS)):
        cols = slice(b * SB_BLOCK, (b + 1) * SB_BLOCK)
        r = _dot_nn(hi[:, cols], tri) + _dot_nn(lo[:, cols], tri)
        outs[b] = r[:, :SB_BLOCK] + run
        run = run + r[:, SB_BLOCK:]
    return jnp.concatenate(outs, axis=1), run


def _masked(valid, v):
    return v if valid is None else jnp.where(valid, v, 0.0)


def _sb_scores(q, kj, scale, valid):
    z = _dot_nt(q, kj) * scale
    t = jnp.log(1.0 + jnp.exp(-jnp.abs(z)))
    return jnp.minimum(z, 0.0) - t, _masked(valid, -jnp.maximum(z, 0.0) - t)


def _sb_specs(h_count, s, col0):
    q_spec = pl.BlockSpec((SB_ROWS, HEAD_DIM), lambda h, i: (i, col0 + h))
    k_spec = pl.BlockSpec((s, HEAD_DIM), lambda h, i: (0, col0 + h_count + h))
    v_spec = pl.BlockSpec((s, HEAD_DIM), lambda h, i: (0, col0 + 2 * h_count + h))
    return q_spec, k_spec, v_spec


def _sb_fwd(qkv, n_heads, col0, *, name, ride=None):
    s = qkv.shape[0]
    nq = s // SB_ROWS
    scale = HEAD_DIM ** -0.5

    def body(q_ref, k_ref, v_ref, o_ref):
        i = pl.program_id(1)
        q = q_ref[...]
        tri = _sb_tri(later=True)

        def step(j, carry, own):
            run, acc = carry
            off = pl.multiple_of(j * SB_KEYS, SB_KEYS)
            valid = _sb_valid(i, j, own)
            ls, lk = _sb_scores(q, k_ref[pl.ds(off, SB_KEYS), :], scale, valid)
            between, run = _sb_scan(lk, tri, run, later=True)
            a = _masked(valid, jnp.exp(ls + between))
            return run, acc + _dot_nn(a.astype(BF16), v_ref[pl.ds(off, SB_KEYS), :])

        carry = step(i, (jnp.zeros((SB_ROWS, SB_BLOCK), F32), jnp.zeros((SB_ROWS, HEAD_DIM), F32)), True)
        _, acc = lax.fori_loop(0, i, lambda jj, c: step(i - 1 - jj, c, False), carry)
        o_ref[...] = acc.astype(o_ref.dtype)

    q_spec, k_spec, v_spec = _sb_specs(n_heads, s, col0)
    return _call(
        body, [qkv, qkv, qkv], name=name, grid=(n_heads, nq),
        in_specs=[q_spec, k_spec, v_spec],
        out_specs=[pl.BlockSpec((SB_ROWS, HEAD_DIM), lambda h, i: (i, h))],
        out_shape=[jax.ShapeDtypeStruct((s, n_heads * HEAD_DIM), BF16)],
        sem=("parallel", "arbitrary"), ride=ride)[0]


def _sb_bwd(qkv, dy, n_heads, col0, *, name, ride=None):
    s = qkv.shape[0]
    nq = s // SB_ROWS
    scale = HEAD_DIM ** -0.5

    def body(q_ref, k_ref, v_ref, dy_ref, dq_ref, dk_ref, dv_ref, e_scr, sg_scr, dk_acc, dv_acc):
        i = pl.program_id(1)
        q = q_ref[...]
        dyv = dy_ref[...]

        @pl.when(i == 0)
        def _():
            dk_acc[...] = jnp.zeros_like(dk_acc)
            dv_acc[...] = jnp.zeros_like(dv_acc)

        tri_later = _sb_tri(later=True)

        def pass1(j, run, own):
            off = pl.multiple_of(j * SB_KEYS, SB_KEYS)
            valid = _sb_valid(i, j, own)
            ls, lk = _sb_scores(q, k_ref[pl.ds(off, SB_KEYS), :], scale, valid)
            between, run = _sb_scan(lk, tri_later, run, later=True)
            a = _masked(valid, jnp.exp(ls + between))
            e_scr[j] = a * _dot_nt(dyv, v_ref[pl.ds(off, SB_KEYS), :])
            sg_scr[j] = jnp.exp(ls)
            dv_acc[pl.ds(off, SB_KEYS), :] += _dot_tn(a.astype(BF16), dyv)
            return run

        lax.fori_loop(0, i, lambda jj, run: pass1(i - 1 - jj, run, False),
                      pass1(i, jnp.zeros((SB_ROWS, SB_BLOCK), F32), True))

        tri_earlier = _sb_tri(later=False)

        def pass2(j, carry, own):
            run, dq = carry
            off = pl.multiple_of(j * SB_KEYS, SB_KEYS)
            kj = k_ref[pl.ds(off, SB_KEYS), :]
            sg = sg_scr[j]
            e = e_scr[j]
            before, run = _sb_scan(e, tri_earlier, run, later=False)
            dz = _masked(_sb_valid(i, j, own), e * (1.0 - sg) - sg * before) * scale
            dzb = dz.astype(BF16)
            dk_acc[pl.ds(off, SB_KEYS), :] += _dot_tn(dzb, q)
            return run, dq + _dot_nn(dzb, kj)

        init = (jnp.zeros((SB_ROWS, SB_BLOCK), F32), jnp.zeros((SB_ROWS, HEAD_DIM), F32))
        _, dq = pass2(i, lax.fori_loop(0, i, lambda j, c: pass2(j, c, False), init), True)
        dq_ref[...] = dq.astype(dq_ref.dtype)

        @pl.when(i == nq - 1)
        def _():
            dk_ref[...] = dk_acc[...].astype(dk_ref.dtype)
            dv_ref[...] = dv_acc[...].astype(dv_ref.dtype)

    q_spec, k_spec, v_spec = _sb_specs(n_heads, s, col0)
    blk = pl.BlockSpec((SB_ROWS, HEAD_DIM), lambda h, i: (i, h))
    full = pl.BlockSpec((s, HEAD_DIM), lambda h, i: (0, h))
    sd = jax.ShapeDtypeStruct((s, n_heads * HEAD_DIM), BF16)
    return _call(
        body, [qkv, qkv, qkv, dy], name=name, grid=(n_heads, nq),
        in_specs=[q_spec, k_spec, v_spec, blk],
        out_specs=[blk, full, full],
        out_shape=[sd, sd, sd],
        scratch_shapes=[pltpu.VMEM((s // SB_KEYS, SB_ROWS, SB_KEYS), F32), pltpu.VMEM((s // SB_KEYS, SB_ROWS, SB_KEYS), F32),
                        pltpu.VMEM((s, HEAD_DIM), F32), pltpu.VMEM((s, HEAD_DIM), F32)],
        sem=("parallel", "arbitrary"), ride=ride)


def _band_bias(rel_bias):
    h = rel_bias.shape[0]
    width = BAND + CHUNK
    first = width - 1 - N_REL
    line = jnp.concatenate([jnp.broadcast_to(rel_bias[:, :1], (h, first)), rel_bias], axis=1)
    tiled = jnp.broadcast_to(line[:, None, :], (h, CHUNK, width - 1)).reshape(h, CHUNK * (width - 1))
    skew = jnp.pad(tiled, ((0, 0), (0, CHUNK))).reshape(h, CHUNK, width)[:, ::-1, :BAND]
    seen = jnp.arange(BAND) >= CHUNK
    return jnp.where(seen[None, None, :], skew, NEG)


def _band_bias_grad(dbias):
    h = dbias.shape[0]
    width = BAND + CHUNK
    flipped = jnp.pad(dbias[:, ::-1, :], ((0, 0), (0, 0), (0, CHUNK)))
    skew = flipped.reshape(h, CHUNK * width)[:, :CHUNK * (width - 1)].reshape(h, CHUNK, width - 1)
    diag = jnp.sum(skew, axis=1)
    first = width - 1 - N_REL
    clipped = jnp.sum(diag[:, :first + 1], axis=1, keepdims=True)
    return jnp.concatenate([clipped, diag[:, first + 1:]], axis=1)


def _group_bias(band):
    return jnp.concatenate([jnp.pad(band, ((0, 0), (0, 0), ((u + 1) * CHUNK, (CA_PER_STEP - 1 - u) * CHUNK)),
                                    constant_values=NEG) for u in range(CA_PER_STEP)], axis=1)


def _group_bias_grad(dgroup):
    return sum(dgroup[:, u * CHUNK:(u + 1) * CHUNK, (u + 1) * CHUNK:(u + 1) * CHUNK + BAND] for u in range(CA_PER_STEP))


def _ca_load_padded(k_ref, v_ref, kp, vp, s):
    kp[pl.ds(0, CA_PAD), :] = jnp.zeros((CA_PAD, HEAD_DIM), kp.dtype)
    vp[pl.ds(0, CA_PAD), :] = jnp.zeros((CA_PAD, HEAD_DIM), vp.dtype)
    kp[pl.ds(CA_PAD, s), :] = k_ref[...]
    vp[pl.ds(CA_PAD, s), :] = v_ref[...]


def _ca_weights(q, kb, bias, off, scale):
    z = _dot_nt(q, kb) * scale + bias
    pos = off + lax.broadcasted_iota(jnp.int32, (CA_ROWS, CA_BAND), 1)
    z = jnp.where(pos >= CA_PAD, z, NEG)
    p = jnp.exp(z - jnp.max(z, axis=1, keepdims=True))
    return p / jnp.sum(p, axis=1, keepdims=True)


def _ca_specs(h_count, s, col0):
    q_spec = pl.BlockSpec((CA_ROWS, HEAD_DIM), lambda h, c: (c, col0 + h))
    k_spec = pl.BlockSpec((s, HEAD_DIM), lambda h, c: (0, col0 + h_count + h))
    v_spec = pl.BlockSpec((s, HEAD_DIM), lambda h, c: (0, col0 + 2 * h_count + h))
    b_spec = pl.BlockSpec((1, CA_ROWS, CA_BAND), lambda h, c: (h, 0, 0))
    return q_spec, k_spec, v_spec, b_spec


def _ca_fwd(qkv, bias, n_heads, col0, *, name, ride=None):
    s = qkv.shape[0]
    nc = s // CA_ROWS
    scale = HEAD_DIM ** -0.5

    def body(q_ref, k_ref, v_ref, b_ref, o_ref, kp, vp):
        c = pl.program_id(1)

        @pl.when(c == 0)
        def _():
            _ca_load_padded(k_ref, v_ref, kp, vp, s)

        off = pl.multiple_of(c * CA_ROWS, CA_ROWS)
        w = _ca_weights(q_ref[...], kp[pl.ds(off, CA_BAND), :], b_ref[0], off, scale)
        o_ref[...] = _dot_nn(w.astype(BF16), vp[pl.ds(off, CA_BAND), :]).astype(o_ref.dtype)

    q_spec, k_spec, v_spec, b_spec = _ca_specs(n_heads, s, col0)
    return _call(
        body, [qkv, qkv, qkv, bias], name=name, grid=(n_heads, nc),
        in_specs=[q_spec, k_spec, v_spec, b_spec],
        out_specs=[pl.BlockSpec((CA_ROWS, HEAD_DIM), lambda h, c: (c, h))],
        out_shape=[jax.ShapeDtypeStruct((s, n_heads * HEAD_DIM), BF16)],
        scratch_shapes=[pltpu.VMEM((s + CA_PAD, HEAD_DIM), BF16), pltpu.VMEM((s + CA_PAD, HEAD_DIM), BF16)],
        sem=("parallel", "arbitrary"), ride=ride)[0]


def _ca_bwd(qkv, bias, dy, n_heads, col0, *, name, ride=None):
    s = qkv.shape[0]
    nc = s // CA_ROWS
    scale = HEAD_DIM ** -0.5

    def body(q_ref, k_ref, v_ref, b_ref, dy_ref, dq_ref, dk_ref, dv_ref, db_ref, kp, vp, dkp, dvp):
        c = pl.program_id(1)

        @pl.when(c == 0)
        def _():
            _ca_load_padded(k_ref, v_ref, kp, vp, s)
            dkp[...] = jnp.zeros_like(dkp)
            dvp[...] = jnp.zeros_like(dvp)
            db_ref[...] = jnp.zeros_like(db_ref)

        off = pl.multiple_of(c * CA_ROWS, CA_ROWS)
        band = pl.ds(off, CA_BAND)
        q = q_ref[...]
        dyv = dy_ref[...]
        kb = kp[band, :]
        w = _ca_weights(q, kb, b_ref[0], off, scale)
        dw = _dot_nt(dyv, vp[band, :])
        dvp[band, :] += _dot_tn(w.astype(BF16), dyv)
        dz = w * (dw - jnp.sum(w * dw, axis=1, keepdims=True))
        db_ref[0] += dz
        dzs = (dz * scale).astype(BF16)
        dq_ref[...] = _dot_nn(dzs, kb).astype(dq_ref.dtype)
        dkp[band, :] += _dot_tn(dzs, q)

        @pl.when(c == nc - 1)
        def _():
            dk_ref[...] = dkp[pl.ds(CA_PAD, s), :].astype(dk_ref.dtype)
            dv_ref[...] = dvp[pl.ds(CA_PAD, s), :].astype(dv_ref.dtype)

    q_spec, k_spec, v_spec, b_spec = _ca_specs(n_heads, s, col0)
    blk = pl.BlockSpec((CA_ROWS, HEAD_DIM), lambda h, c: (c, h))
    full = pl.BlockSpec((s, HEAD_DIM), lambda h, c: (0, h))
    sd = jax.ShapeDtypeStruct((s, n_heads * HEAD_DIM), BF16)
    return _call(
        body, [qkv, qkv, qkv, bias, dy], name=name, grid=(n_heads, nc),
        in_specs=[q_spec, k_spec, v_spec, b_spec, blk],
        out_specs=[blk, full, full, b_spec],
        out_shape=[sd, sd, sd, jax.ShapeDtypeStruct((n_heads, CA_ROWS, CA_BAND), F32)],
        scratch_shapes=[pltpu.VMEM((s + CA_PAD, HEAD_DIM), BF16), pltpu.VMEM((s + CA_PAD, HEAD_DIM), BF16),
                        pltpu.VMEM((s + CA_PAD, HEAD_DIM), F32), pltpu.VMEM((s + CA_PAD, HEAD_DIM), F32)],
        sem=("parallel", "arbitrary"), ride=ride)


EARLY = ("w_sb_out", "w_ca_out", "w_mix_out")


def _step(x, p, target, small, comm):
    w = comm.w
    d = x.shape[1]
    n_sb = w["w_sb_out"].shape[0] // HEAD_DIM
    n_ca = w["w_ca_out"].shape[0] // HEAD_DIM
    qkv_cols = 3 * HEAD_DIM * (n_sb + n_ca)
    ca_col0 = 3 * n_sb
    both = (F32, BF16)

    h1 = _rms_fwd(x, small["g_mix"], name="rms_mix")
    ffn, ple = ("w_ffn_in",), ("w_ple_gate", "w_ple_in")
    qkv = _mm(h1, w["w_in"], "nn", (BF16,), name="proj_qkv", n=qkv_cols, ride=comm.gather(EARLY, "near"))
    gates = _mm(h1, w["w_in"], "nn", (F32,), name="proj_gates", n=2 * d, b_col_off=qkv_cols,
                ride=comm.gather(ffn, "near", comm.gather(EARLY, "far"), (0, 8)))
    bias = _group_bias(_band_bias(small["rel_bias"]))
    y_sb = _sb_fwd(qkv, n_sb, 0, name="sb_fwd", ride=comm.gather(ffn, "near", comm.gather(EARLY, "pair"), (1, 8, 7)))
    y_ca = _ca_fwd(qkv, bias, n_ca, ca_col0, name="ca_fwd", ride=comm.gather(ffn, "far"))
    out = ("w_ffn_out",)
    o_sb = _mm(y_sb, w["w_sb_out"], "nn", (F32,), name="sb_out", ride=comm.gather(out, "near", part=(0, 4)))
    o_ca = _mm(y_ca, w["w_ca_out"], "nn", (F32,), name="ca_out", ride=comm.gather(out, "near", part=(1, 4)))
    merged = _gate_merge_fwd(gates, o_sb, o_ca, name="gate_merge",
                             ride=comm.gather(out, "near", comm.gather(ffn, "pair"), (2, 4)))
    x1 = _mm(merged, w["w_mix_out"], "nn", (F32,), name="mix_out", resid=x, ride=comm.gather(out, "near", part=(3, 4)))
    h2 = _rms_fwd(x1, small["g_ffn"], name="rms_ffn")
    gu = _mm(h2, w["w_ffn_in"], "nn", (BF16,), name="ffn_in", ride=comm.gather(ple, "near", comm.gather(out, "far")))
    act = _swiglu_fwd(gu, name="swiglu", ride=comm.gather(ple, "far", comm.gather(out, "pair")))
    x2 = _mm(act, w["w_ffn_out"], "nn", (F32,), name="ffn_out", resid=x1, ride=comm.gather(ple, "pair"))
    h3 = _rms_fwd(x2, small["g_ple"], name="rms_ple")
    t = _mm(h3, w["w_ple_gate"], "nn", (F32,), name="ple_gate")
    pe = _mm(p, w["w_ple_in"], "nn", (F32,), name="ple_in")
    x3 = _ple_fwd(x2, t, pe, name="ple_add")

    gs = {}
    dx3, gs["g_final"], loss = _final_loss(x3, small["g_final"], target, name="final_loss")
    dt, dpe = _ple_bwd(dx3, t, pe, name="ple_bwd")
    comm.grad("w_ple_in", *_mm(p, dpe, "tn", both, name="dw_ple_in"))
    comm.grad("w_ple_gate", *_mm(h3, dt, "tn", both, name="dw_ple_gate"))
    ple = ("w_ple_in", "w_ple_gate")
    dh3 = _mm(dt, w["w_ple_gate"], "nt", (F32,), name="dh_ple", ride=comm.pair(ple))
    dx2, dx2_16, gs["g_ple"] = _rms_bwd(x2, small["g_ple"], dh3, dx3, name="rms_ple_bwd")
    comm.add(ple)
    comm.grad("w_ffn_out", *_mm(act, dx2_16, "tn", both, name="dw_ffn_out", ride=comm.chips(ple)))
    dact = _mm(dx2_16, w["w_ffn_out"], "nt", (F32,), name="dact", ride=comm.pair(("w_ffn_out",)))
    dgu = _swiglu_bwd(dact, gu, name="swiglu_bwd")
    comm.sum(ple)
    comm.add(("w_ffn_out",))
    comm.grad("w_ffn_in", *_mm(h2, dgu, "tn", both, name="dw_ffn_in",
                               ride=comm.share(ple, comm.chips(("w_ffn_out",)))))
    dh2 = _mm(dgu, w["w_ffn_in"], "nt", (F32,), name="dh_ffn", ride=comm.pair(("w_ffn_in",)))
    dx1, dx1_16, gs["g_ffn"] = _rms_bwd(x1, small["g_ffn"], dh2, dx2, name="rms_ffn_bwd")
    comm.add(("w_ffn_in",))
    comm.sum(("w_ffn_out",))
    comm.grad("w_mix_out", *_mm(merged, dx1_16, "tn", both, name="dw_mix_out", ride=comm.share(("w_ffn_out",))))
    dmerged = _mm(dx1_16, w["w_mix_out"], "nt", (F32,), name="dmerged", ride=comm.pair(("w_mix_out",)))
    dg_sb, dg_ca, do_sb, do_ca = _gate_merge_bwd(dmerged, gates, o_sb, o_ca, name="gate_merge_bwd")
    comm.add(("w_mix_out",))
    comm.grad("w_sb_out", *_mm(y_sb, do_sb, "tn", both, name="dw_sb_out"))
    comm.grad("w_ca_out", *_mm(y_ca, do_ca, "tn", both, name="dw_ca_out"))
    outs = ("w_sb_out", "w_ca_out")
    dy_sb = _mm(do_sb, w["w_sb_out"], "nt", (BF16,), name="dy_sb", ride=comm.pair(outs))
    dy_ca = _mm(do_ca, w["w_ca_out"], "nt", (BF16,), name="dy_ca")
    comm.add(outs)
    dq_sb, dk_sb, dv_sb = _sb_bwd(qkv, dy_sb, n_sb, 0, name="sb_bwd", ride=comm.chips(("w_ffn_in",)))
    comm.sum(("w_ffn_in",))
    late = ("w_mix_out",) + outs
    dq_ca, dk_ca, dv_ca, dbias = _ca_bwd(qkv, bias, dy_ca, n_ca, ca_col0, name="ca_bwd",
                                         ride=comm.chips(late, comm.share(("w_ffn_in",))))
    comm.sum(late)
    gs["rel_bias"] = _band_bias_grad(_group_bias_grad(dbias))
    dproj = _concat_cols([dq_sb, dk_sb, dv_sb, dq_ca, dk_ca, dv_ca, dg_sb, dg_ca], name="dproj")
    comm.grad("w_in", *_mm(h1, dproj, "tn", both, name="dw_in", ride=comm.share(late)))
    half = x.shape[0] // 2
    dh1 = _mm(dproj, w["w_in"], "nt", (F32,), name="dh_mix_top", rows=(0, half), ride=comm.pair(("w_in",)))
    comm.add(("w_in",))
    dh1 = _mm(dproj, w["w_in"], "nt", (F32,), name="dh_mix_bottom", rows=(half, half), onto=(dh1,),
              ride=comm.tail(TAIL_SECOND))
    grad_x, _, gs["g_mix"] = _rms_bwd(x, small["g_mix"], dh1, dx1, name="rms_mix_bwd", ride=comm.tail(TAIL_FIRST))
    return loss, grad_x, gs


def _position():
    x, y, c = lax.axis_index("x"), lax.axis_index("y"), lax.axis_index("c")
    chips = [(1 - x, y), (x, 1 - y), (1 - x, 1 - y)]
    return x, y, c, chips


def _aligned(v, m):
    return v if isinstance(v, int) else pl.multiple_of(v, m)


def _piece_dims(shape, axis):
    k, n = shape
    return (k // 2, n // N_CHIPS) if axis == 1 else (k // N_CHIPS // 2, n)


def _piece(ref, shape, axis, j, h, part=(0, 1)):
    pr, pc = _piece_dims(shape, axis)
    nr = pr // part[1] * (part[2] if len(part) > 2 else 1)
    r0 = part[0] * (pr // part[1])
    if axis == 1:
        return ref.at[pl.ds(_aligned(h * pr + r0, 16), nr), pl.ds(_aligned(j * pc, 128), pc)]
    return ref.at[pl.ds(_aligned((2 * j + h) * pr + r0, 16), nr), :]


def _shard_half(ref, h):
    rows = ref.shape[0] // 2
    return ref.at[pl.ds(_aligned(h * rows, 16), rows), :]


def _remote(src, dst, send_sems, recv_sems, k, to):
    return pltpu.make_async_remote_copy(src_ref=src, dst_ref=dst, send_sem=send_sems.at[k],
                                        recv_sem=recv_sems.at[k], device_id=to, device_id_type=MESH)


def _prefetch_call(body, scalars, ins, in_specs, out_shape, out_specs, grid, *, name, ride=None):
    single = not isinstance(out_shape, (list, tuple))
    outs = _call(body, ins, name=name, grid=grid, in_specs=in_specs,
                 out_specs=[out_specs] if single else out_specs, out_shape=[out_shape] if single else out_shape,
                 sem=("parallel",) * len(grid), ride=ride, scalars=scalars)
    return outs[0] if single else outs


def _slab_tiles(pr, pc):
    tc = pc if pc <= 4096 else _pick(pc, (2048, 1024, 512, 256, 128))
    tr = next(t for t in (1024, 512, 256, 128, 64, 32, 16) if pr % t == 0 and t * tc <= 512 * 1024)
    return tr, tc


def _cast_place(w, axis, pos, *, name, ride=None):
    ks, ns = w.shape
    shape = (ks, ns * N_CHIPS) if axis == 1 else (ks * N_CHIPS, ns)
    tr, tc = _slab_tiles(ks, ns)
    nr, nc = ks // tr, ns // tc

    def body(pos_ref, w_ref, o_ref):
        o_ref[...] = w_ref[...].astype(o_ref.dtype)

    if axis == 1:
        out_map = lambda i, j, pos_ref: (i, pos_ref[0] * nc + j)
    else:
        out_map = lambda i, j, pos_ref: (pos_ref[0] * nr + i, j)
    return _prefetch_call(body, pos, [w], [pl.BlockSpec((tr, tc), lambda i, j, pos_ref: (i, j))],
                          jax.ShapeDtypeStruct(shape, BF16), pl.BlockSpec((tr, tc), out_map), (nr, nc), name=name, ride=ride)


def _run(ride, *, name):
    if ride is None:
        return

    def body(o_ref):
        o_ref[...] = jnp.zeros_like(o_ref)

    _call(body, [], name=name, grid=(1,), in_specs=[], out_specs=[pl.BlockSpec((8, 128), lambda i: (0, 0))],
          out_shape=[jax.ShapeDtypeStruct((8, 128), F32)], ride=ride)


def _ride_gather(ride, w, n, axis, stage, part=(0, 1)):
    shape = w[n].shape
    piece = functools.partial(_piece, shape=shape, axis=axis)
    span = part[2] if len(part) > 2 else 1
    halves = [(2 * part[0] + t * span, 2 * part[1], span) for t in range(2)]

    def copies(ins, outs, send_sems, recv_sems, arriving):
        x, y, c, chips = _position()
        me, (xn, yn, dn) = 2 * x + y, [2 * px + py for px, py in chips]
        if stage == "near":
            plan = [(me, c, part, (1 - x, y, c), xn, c, part), (me, c, part, (x, 1 - y, c), yn, c, part)]
        elif stage == "far":
            plan = [(yn, c, halves[1], (1 - x, y, c), dn, c, halves[1]), (xn, c, halves[0], (x, 1 - y, c), dn, c, halves[0])]
        else:
            plan = [(j, c, part, (x, y, 1 - c), j, 1 - c, part) for j in (xn, yn, dn)]
        out = []
        for k, (chip, h, rows, to, from_chip, from_h, from_rows) in enumerate(plan):
            if arriving:
                lands = piece(outs[0], j=from_chip, h=from_h, part=from_rows)
                out.append(_remote(lands, lands, send_sems, recv_sems, k, to))
            else:
                out.append(_remote(piece(ins[0], j=chip, h=h, part=rows), piece(outs[0], j=chip, h=h, part=rows),
                                   send_sems, recv_sems, k, to))
        return out

    def start(*refs):
        for cp in copies(*refs, arriving=False):
            cp.start()

    def finish(*refs):
        for cp in copies(*refs, arriving=True):
            cp.wait_recv()
        for cp in copies(*refs, arriving=False):
            cp.wait_send()

    ride.add([w[n]], [jax.ShapeDtypeStruct(shape, w[n].dtype)], {0: 0}, 3, start, finish,
             lambda outs: w.__setitem__(n, outs[0]))


def _ride_pair(ride, st, axis):
    shape = st["g16"].shape
    pr, pc = _piece_dims(shape, axis)

    def copies(ins, outs, send_sems, recv_sems):
        x, y, c, _ = _position()
        return [_remote(_piece(ins[0], shape, axis, j, 1 - c), outs[0].at[j], send_sems, recv_sems, j, (x, y, 1 - c))
                for j in range(N_CHIPS)]

    def start(*refs):
        for cp in copies(*refs):
            cp.start()

    def finish(*refs):
        for cp in copies(*refs):
            cp.wait()

    ride.add([st["g16"]], [jax.ShapeDtypeStruct((N_CHIPS, pr, pc), BF16)], {}, N_CHIPS, start, finish,
             lambda outs: st.__setitem__("sib", outs[0]))


def _ride_chips(ride, st, rows=None):
    _, pr, pc = st["s16"].shape
    r0, nr = (0, pr) if rows is None else rows

    def copies(ins, outs, send_sems, recv_sems):
        x, y, c, chips = _position()
        return [_remote(ins[0].at[2 * px + py, pl.ds(r0, nr), :], outs[0].at[k, pl.ds(r0, nr), :],
                        send_sems, recv_sems, k, (px, py, c)) for k, (px, py) in enumerate(chips)]

    def start(*refs):
        for cp in copies(*refs):
            cp.start()

    def finish(*refs):
        for cp in copies(*refs):
            cp.wait()

    ins, aliases = ([st["s16"], st["recv"]], {1: 0}) if "recv" in st else ([st["s16"]], {})
    ride.add(ins, [jax.ShapeDtypeStruct((3, pr, pc), BF16)], aliases, 3, start, finish,
             lambda outs: st.__setitem__("recv", outs[0]))


def _ride_share(ride, st):
    def sent(ins, outs, send_sems, recv_sems):
        x, y, c, _ = _position()
        return _remote(_shard_half(ins[0], c), _shard_half(outs[0], c), send_sems, recv_sems, 0, (x, y, 1 - c))

    def landed(ins, outs, send_sems, recv_sems):
        x, y, c, _ = _position()
        other = _shard_half(outs[0], 1 - c)
        return _remote(other, other, send_sems, recv_sems, 0, (x, y, 1 - c))

    def start(*refs):
        sent(*refs).start()

    def finish(*refs):
        landed(*refs).wait_recv()
        sent(*refs).wait_send()

    ride.add([st["shard"]], [jax.ShapeDtypeStruct(st["shard"].shape, F32)], {0: 0}, 1, start, finish,
             lambda outs: st.__setitem__("g", outs[0]))


def _piece_block(axis, nr, nc, chip):
    if axis == 1:
        return lambda *a: ((a[-1][1] * nr + a[-3]), (a[0] if chip is None else chip(a[-1])) * nc + a[-2])
    return lambda *a: ((2 * (a[0] if chip is None else chip(a[-1])) + a[-1][1]) * nr + a[-3], a[-2])


def _pair_add(g32, sib, axis, pos, *, name):
    _, pr, pc = sib.shape
    tr, tc = _slab_tiles(pr, pc)
    nr, nc = pr // tr, pc // tc

    def body(pos_ref, g_ref, b_ref, o16_ref):
        o16_ref[0] = (g_ref[...] + b_ref[0].astype(F32)).astype(o16_ref.dtype)

    blk = pl.BlockSpec((1, tr, tc), lambda j, i, k, pos_ref: (j, i, k))
    return _prefetch_call(body, pos, [g32, sib], [pl.BlockSpec((tr, tc), _piece_block(axis, nr, nc, None)), blk],
                          jax.ShapeDtypeStruct(sib.shape, BF16), blk, (N_CHIPS, nr, nc), name=name)


def _chip_sum(g32, sib, recv, axis, pos, *, name):
    _, pr, pc = sib.shape
    tr, tc = _slab_tiles(pr, pc)
    nr, nc = pr // tr, pc // tc

    def body(pos_ref, g_ref, b_ref, r_ref, o_ref):
        pair = g_ref[...] + b_ref[0].astype(F32)
        o_ref[...] = ((pair + r_ref[0].astype(F32)) + r_ref[1].astype(F32)) + r_ref[2].astype(F32)

    return _prefetch_call(
        body, pos, [g32, sib, recv],
        [pl.BlockSpec((tr, tc), _piece_block(axis, nr, nc, lambda pos_ref: pos_ref[0])),
         pl.BlockSpec((1, tr, tc), lambda i, k, pos_ref: (pos_ref[0], i, k)),
         pl.BlockSpec((3, tr, tc), lambda i, k, pos_ref: (0, i, k))],
        jax.ShapeDtypeStruct((2 * pr, pc), F32),
        pl.BlockSpec((tr, tc), lambda i, k, pos_ref: (pos_ref[1] * nr + i, k)), (nr, nc), name=name)


class _Comm:
    def __init__(self, pos, w):
        self.pos, self.w, self.st = pos, w, {n: {} for n, _ in BIG}

    def gather(self, names, stage, ride=None, part=(0, 1)):
        ride = _Ride() if ride is None else ride
        for n in names:
            _ride_gather(ride, self.w, n, AXIS[n], stage, part)
        return ride

    def grad(self, n, g32, g16):
        self.st[n].update(g32=g32, g16=g16)

    def pair(self, names, ride=None):
        ride = _Ride() if ride is None else ride
        for n in names:
            _ride_pair(ride, self.st[n], AXIS[n])
        return ride

    def add(self, names):
        for n in names:
            st = self.st[n]
            st["s16"] = _pair_add(st["g32"], st["sib"], AXIS[n], self.pos, name="rs_add_" + n)

    def chips(self, names, ride=None, rows=None):
        ride = _Ride() if ride is None else ride
        for n in names:
            _ride_chips(ride, self.st[n], rows)
        return ride

    def sum(self, names):
        for n in names:
            st = self.st[n]
            st["shard"] = _chip_sum(st["g32"], st["sib"], st["recv"], AXIS[n], self.pos, name="rs_sum_" + n)

    def share(self, names, ride=None):
        ride = _Ride() if ride is None else ride
        for n in names:
            _ride_share(ride, self.st[n])
        return ride

    def tail(self, count):
        st = self.st["w_in"]
        rows, at = st["s16"].shape[1], st.get("at", 0)
        st["at"] = at + count
        return self.chips(("w_in",), rows=(at * rows // TAIL_PARTS, count * rows // TAIL_PARTS))

    def tail_rest(self):
        return self.tail(TAIL_PARTS - self.st["w_in"].get("at", 0))

    def result(self, n):
        return self.st[n]["g"]


class _NoComm:
    def __init__(self, w):
        self.w, self.st = w, {}

    def grad(self, n, g32, g16):
        self.st[n] = (g32, g16)

    def result(self, n):
        return self.st[n]

    def add(self, names):
        pass

    sum = add

    def gather(self, names, *args, **kwargs):
        return None

    pair = chips = share = tail = gather


def _small_all_reduce(vec, *, name):
    r = vec.shape[0]

    def body(vec_ref, out_ref, slots, send_sems, recv_sems):
        x, y, c, _ = _position()
        me = 4 * x + 2 * y + c
        slots[me] = vec_ref[...]
        sends = []
        for k in range(1, 8):
            to = (x ^ (k >> 2), y ^ ((k >> 1) & 1), c ^ (k & 1))
            cp = _remote(slots.at[me], slots.at[me], send_sems, recv_sems, k - 1, to)
            cp.start()
            sends.append(cp)
        for k in range(1, 8):
            frm = 4 * (x ^ (k >> 2)) + 2 * (y ^ ((k >> 1) & 1)) + (c ^ (k & 1))
            _remote(slots.at[frm], slots.at[frm], send_sems, recv_sems, k - 1, (x, y, c)).wait_recv()
        for cp in sends:
            cp.wait_send()
        total = slots[0]
        for d in range(1, 8):
            total = total + slots[d]
        out_ref[...] = total

    return pl.pallas_call(
        body, name=name,
        in_specs=[pl.BlockSpec(memory_space=pltpu.VMEM)], out_specs=pl.BlockSpec(memory_space=pltpu.VMEM),
        out_shape=jax.ShapeDtypeStruct((r, 128), F32),
        scratch_shapes=[pltpu.VMEM((8, r, 128), F32), pltpu.SemaphoreType.DMA((7,)), pltpu.SemaphoreType.DMA((7,))],
    )(vec)


SC_TILES = 32
SC_LANES = 16


def _adamw_update(wv, gv, mv, vv):
    nm = ADAM_B1 * mv + (1.0 - ADAM_B1) * gv
    nv = ADAM_B2 * vv + (1.0 - ADAM_B2) * (gv * gv)
    m_hat = nm / (1.0 - ADAM_B1 ** ADAM_STEP)
    v_hat = nv / (1.0 - ADAM_B2 ** ADAM_STEP)
    return -ADAM_LR * (m_hat / (jnp.sqrt(v_hat) + ADAM_EPS) + ADAM_WD * wv), nm, nv


def _adamw_sc(w, g, m, v, *, name):
    r, c = w.shape
    rows = r // SC_TILES
    cb = _pick(c, (2048, 1024, 512, 256, 128))

    def body(w_hbm, g_hbm, m_hbm, v_hbm, d_hbm, nm_hbm, nv_hbm, wb, gb, mb, vb):
        tile = lax.axis_index("sc_tile") * 2 + lax.axis_index("sc_core")

        @pl.loop(0, rows, step=8)
        def _(r0):
            for c0 in range(0, c, cb):
                at = (pl.ds(tile * rows + r0, 8), pl.ds(c0, cb))
                for hbm, buf in ((w_hbm, wb), (g_hbm, gb), (m_hbm, mb), (v_hbm, vb)):
                    pltpu.sync_copy(hbm.at[at], buf)

                @pl.loop(0, 8)
                def _(rr):
                    @pl.loop(0, cb, step=SC_LANES)
                    def _(i):
                        lanes = (rr, pl.ds(i, SC_LANES))
                        wb[lanes], mb[lanes], vb[lanes] = _adamw_update(wb[lanes], gb[lanes], mb[lanes], vb[lanes])

                for buf, hbm in ((wb, d_hbm), (mb, nm_hbm), (vb, nv_hbm)):
                    pltpu.sync_copy(buf, hbm.at[at])

    sd = jax.ShapeDtypeStruct((r, c), F32)
    return pl.kernel(body, name=name, out_type=[sd, sd, sd],
                     mesh=plsc.VectorSubcoreMesh(core_axis_name="sc_core", subcore_axis_name="sc_tile"),
                     scratch_types=[pltpu.VMEM((8, cb), F32)] * 4)(w, g, m, v)


def _adamw(w, g, m, v, *, name, ride=None):
    r, c = w.shape
    tc = c if c <= 4096 else _pick(c, (2048, 1024, 512, 256, 128))
    tr = next(t for t in (512, 256, 128, 64, 32, 16, 8) if r % t == 0 and t * tc <= 256 * 1024)

    def body(w_ref, g_ref, m_ref, v_ref, d_ref, nm_ref, nv_ref):
        gv = g_ref[...]
        nm = ADAM_B1 * m_ref[...] + (1.0 - ADAM_B1) * gv
        nv = ADAM_B2 * v_ref[...] + (1.0 - ADAM_B2) * (gv * gv)
        m_hat = nm / (1.0 - ADAM_B1 ** ADAM_STEP)
        v_hat = nv / (1.0 - ADAM_B2 ** ADAM_STEP)
        d_ref[...] = -ADAM_LR * (m_hat / (jnp.sqrt(v_hat) + ADAM_EPS) + ADAM_WD * w_ref[...])
        nm_ref[...] = nm
        nv_ref[...] = nv

    blk = ((tr, tc), lambda i, j: (i, j))
    sd = jax.ShapeDtypeStruct((r, c), F32)
    return _ew(body, [w, g, m, v], [blk] * 4, [sd, sd, sd], [blk] * 3, (r // tr, c // tc), name=name, ride=ride)


BIG = (("w_in", 1), ("w_sb_out", 1), ("w_ca_out", 1), ("w_mix_out", 0), ("w_ffn_in", 1), ("w_ffn_out", 0),
       ("w_ple_in", 1), ("w_ple_gate", 0))
AXIS = dict(BIG)
HEAD_PARTS = 8
HEAD_HOSTS = ("w_ffn_in", "w_ffn_out")
TAIL_PARTS = 16
TAIL_SECOND = 4
TAIL_FIRST = 2
TAIL_HOSTS = {"w_ffn_in": 4, "w_ffn_out": 2, "w_ple_gate": 1}
ON_SPARSECORE = ("w_sb_out", "w_ca_out", "w_mix_out", "w_ple_in")
SMALL = ("rel_bias", "g_mix", "g_ffn", "g_ple", "g_final")
ORDER = ("w_in", "w_sb_out", "w_ca_out", "w_mix_out", "rel_bias", "g_mix", "g_ffn", "g_ple", "g_final",
         "w_ffn_in", "w_ffn_out", "w_ple_in", "w_ple_gate")


def _pack(parts):
    flat = jnp.concatenate([a.reshape(-1) for a in parts])
    rows = -(-flat.shape[0] // 1024) * 8
    return jnp.pad(flat, (0, rows * 128 - flat.shape[0])).reshape(rows, 128)


def _unpack(packed, like):
    flat, out, at = packed.reshape(-1), [], 0
    for a in like:
        out.append(flat[at:at + a.size].reshape(a.shape))
        at += a.size
    return out


def kernel(x, p, w_in, w_sb_out, w_ca_out, w_mix_out, rel_bias, g_mix, g_ffn, g_ple, g_final, w_ffn_in, w_ffn_out, w_ple_in, w_ple_gate, loss_target, m_w_in, m_w_sb_out, m_w_ca_out, m_w_mix_out, m_rel_bias, m_g_mix, m_g_ffn, m_g_ple, m_g_final, m_w_ffn_in, m_w_ffn_out, m_w_ple_in, m_w_ple_gate, v_w_in, v_w_sb_out, v_w_ca_out, v_w_mix_out, v_rel_bias, v_g_mix, v_g_ffn, v_g_ple, v_g_final, v_w_ffn_in, v_w_ffn_out, v_w_ple_in, v_w_ple_gate):
    weights = dict(w_in=w_in, w_sb_out=w_sb_out, w_ca_out=w_ca_out, w_mix_out=w_mix_out, rel_bias=rel_bias,
                   g_mix=g_mix, g_ffn=g_ffn, g_ple=g_ple, g_final=g_final, w_ffn_in=w_ffn_in,
                   w_ffn_out=w_ffn_out, w_ple_in=w_ple_in, w_ple_gate=w_ple_gate)
    m_in = dict(w_in=m_w_in, w_sb_out=m_w_sb_out, w_ca_out=m_w_ca_out, w_mix_out=m_w_mix_out, rel_bias=m_rel_bias,
                g_mix=m_g_mix, g_ffn=m_g_ffn, g_ple=m_g_ple, g_final=m_g_final, w_ffn_in=m_w_ffn_in,
                w_ffn_out=m_w_ffn_out, w_ple_in=m_w_ple_in, w_ple_gate=m_w_ple_gate)
    v_in = dict(w_in=v_w_in, w_sb_out=v_w_sb_out, w_ca_out=v_w_ca_out, w_mix_out=v_w_mix_out, rel_bias=v_rel_bias,
                g_mix=v_g_mix, g_ffn=v_g_ffn, g_ple=v_g_ple, g_final=v_g_final, w_ffn_in=v_w_ffn_in,
                w_ffn_out=v_w_ffn_out, w_ple_in=v_w_ple_in, w_ple_gate=v_w_ple_gate)

    pos = jnp.stack([2 * lax.axis_index("x") + lax.axis_index("y"), lax.axis_index("c")]).astype(jnp.int32)
    comm = _Comm(pos, {"w_in": _cast_place(w_in[0], AXIS["w_in"], pos, name="cast_w_in")})
    at = 0
    for n in HEAD_HOSTS:
        ride = comm.gather(("w_in",), "near", part=(at, HEAD_PARTS))
        comm.w[n] = _cast_place(weights[n][0], AXIS[n], pos, name="cast_" + n, ride=ride)
        at += 1
    for n, axis in BIG:
        if n not in comm.w:
            comm.w[n] = _cast_place(weights[n][0], axis, pos, name="cast_" + n)
    _run(comm.gather(("w_in",), "near", part=(at, HEAD_PARTS, HEAD_PARTS - at)), name="gather_w_in_near")
    _run(comm.gather(("w_in",), "far"), name="gather_w_in_far")
    _run(comm.gather(("w_in",), "pair"), name="gather_w_in_pair")
    small = dict(rel_bias=rel_bias[0], g_mix=g_mix, g_ffn=g_ffn, g_ple=g_ple, g_final=g_final.reshape(1, -1))
    loss, grad_x, gs = _step(x[0], p[0, 0], loss_target[0], small, comm)

    grads, delta, new_m, new_v = {}, {}, {}, {}
    for n in [n for n, _ in BIG if n != "w_in"] + ["w_in"]:
        ride = comm.tail(TAIL_HOSTS[n]) if n in TAIL_HOSTS else None
        if n == "w_in":
            _run(comm.tail_rest(), name="rs_chips_w_in")
            comm.sum(("w_in",))
            _run(comm.share(("w_in",)), name="rs_share_w_in")
        g = comm.result(n)
        if n in ON_SPARSECORE:
            d, nm, nv = _adamw_sc(weights[n][0], g, m_in[n][0], v_in[n][0], name="adamw_sc_" + n)
        else:
            d, nm, nv = _adamw(weights[n][0], g, m_in[n][0], v_in[n][0], name="adamw_" + n, ride=ride)
        grads[n], delta[n], new_m[n], new_v[n] = g[None], d[None], nm[None], nv[None]

    like = [weights[n] for n in SMALL]
    reduced = _small_all_reduce(_pack([gs[n] for n in SMALL] + [loss[:, :1]]), name="small_all_reduce")
    g_small = _unpack(reduced, like + [loss[:, :1]])
    total_loss = g_small[-1].reshape(())
    g_packed = _pack(g_small[:-1])
    d_s, m_s, v_s = _adamw(_pack(like), g_packed, _pack([m_in[n] for n in SMALL]), _pack([v_in[n] for n in SMALL]),
                           name="adamw_small")
    for n, g, d, nm, nv in zip(SMALL, g_small[:-1], _unpack(d_s, like), _unpack(m_s, like), _unpack(v_s, like)):
        grads[n], delta[n], new_m[n], new_v[n] = g, d, nm, nv

    return (total_loss, grad_x[None], *[grads[n] for n in ORDER], *[delta[n] for n in ORDER],
            *[new_m[n] for n in ORDER], *[new_v[n] for n in ORDER])
```

```python
import functools
import math

import jax
import jax.numpy as jnp
import numpy as np
from jax import lax
from jax.experimental import pallas as pl
from jax.experimental.pallas import tpu as pltpu
from jax.experimental.pallas import tpu_sc as plsc

F32 = jnp.float32
BF16 = jnp.bfloat16

HEAD_DIM = 128
CHUNK = 64
LEFT_CHUNKS = 8
REL_CLIP = 128
N_REL = REL_CLIP + CHUNK
BAND = (LEFT_CHUNKS + 2) * CHUNK
CA_PER_STEP = 4
CA_ROWS = CA_PER_STEP * CHUNK
CA_BAND = BAND + CA_PER_STEP * CHUNK
CA_PAD = BAND
SB_BLOCK = 128
SB_KEYS = 512
SB_GROUPS = SB_KEYS // SB_BLOCK
SB_ROWS = SB_KEYS
EPS = 1e-6
NEG = -1e30

ADAM_LR = 0.001
ADAM_B1 = 0.9
ADAM_B2 = 0.999
ADAM_EPS = 1e-08
ADAM_WD = 0.01
ADAM_STEP = 10

VMEM_LIMIT = 48 * 1024 * 1024
MM_VMEM_BUDGET = 36 * 1024 * 1024
V7X_HBM_BYTES_PER_S = 3.7e12
GRID_STEP_S = 0.35e-6
MESH = pl.DeviceIdType.MESH
N_CHIPS = 4


def _pick(dim, prefs):
    for t in prefs:
        if dim % t == 0:
            return t
    raise ValueError(f"no tile for {dim}")


def _cparams(sem=None):
    return pltpu.CompilerParams(dimension_semantics=sem, vmem_limit_bytes=VMEM_LIMIT)


def _sigmoid(v):
    return 1.0 / (1.0 + jnp.exp(-v))


def _dot(a, b, dims):
    return lax.dot_general(a, b, (dims, ((), ())), preferred_element_type=F32)


def _dot_nn(a, b):
    return _dot(a, b, ((1,), (0,)))


def _dot_nt(a, b):
    return _dot(a, b, ((1,), (1,)))


def _dot_tn(a, b):
    return _dot(a, b, ((0,), (0,)))


HBM = pl.BlockSpec(memory_space=pltpu.HBM)


class _Ride:
    def __init__(self):
        self.items = []

    def add(self, ins, outs, aliases, n_sems, start, finish, sink):
        self.items.append((ins, outs, aliases, n_sems, start, finish, sink))


def _call(body, args, *, name, grid, in_specs, out_specs, out_shape, scratch_shapes=(), sem=None, ride=None,
          scalars=None, onto=()):
    items = ride.items if ride is not None else []
    if onto:
        args, in_specs = list(args) + list(onto), list(in_specs) + [HBM] * len(onto)
        inner, body = body, lambda *refs: inner(*refs[:len(args) - len(onto)], *refs[len(args):])
    n_in, n_out, n_scr = len(args), len(out_shape), len(scratch_shapes)
    r_ins = [a for it in items for a in it[0]]
    r_outs = [o for it in items for o in it[1]]
    updated = [id(it[0][i]) for it in items for i in it[2]]
    assert len(set(updated)) == len(updated), "one call may update a buffer in place only once"
    aliases, a, b = {n_in - len(onto) + t: t for t in range(len(onto))}, n_in, n_out
    for it in items:
        aliases.update({a + i: b + o for i, o in it[2].items()})
        a, b = a + len(it[0]), b + len(it[1])
    sems = [pltpu.SemaphoreType.DMA((it[3],)) for it in items for _ in range(2)]

    def wrapped(*refs):
        head, refs = (refs[:1], refs[1:]) if scalars is not None else ((), refs)
        ins, rin = refs[:n_in], refs[n_in:n_in + len(r_ins)]
        at = n_in + len(r_ins)
        outs, rout = refs[at:at + n_out], refs[at + n_out:at + n_out + len(r_outs)]
        at += n_out + len(r_outs)
        scr, rsem = refs[at:at + n_scr], refs[at + n_scr:]

        def each(which):
            a = b = 0
            for q, it in enumerate(items):
                it[which](rin[a:a + len(it[0])], rout[b:b + len(it[1])], rsem[2 * q], rsem[2 * q + 1])
                a, b = a + len(it[0]), b + len(it[1])

        if items:
            ids = [pl.program_id(d) for d in range(len(grid))]
            first = functools.reduce(jnp.logical_and, [i == 0 for i in ids])
            last = functools.reduce(jnp.logical_and, [i == g - 1 for i, g in zip(ids, grid)])
            pl.when(first)(lambda: each(4))
        body(*head, *ins, *outs, *scr)
        if items:
            pl.when(last)(lambda: each(5))

    specs = dict(grid=grid, in_specs=list(in_specs) + [HBM] * len(r_ins),
                 out_specs=list(out_specs) + [HBM] * len(r_outs), scratch_shapes=list(scratch_shapes) + sems)
    if scalars is not None:
        specs = dict(grid_spec=pltpu.PrefetchScalarGridSpec(num_scalar_prefetch=1, **specs))
        aliases = {i + 1: o for i, o in aliases.items()}
    res = pl.pallas_call(
        wrapped, name=name, **specs,
        out_shape=list(out_shape) + r_outs,
        input_output_aliases=aliases,
        compiler_params=_cparams(("arbitrary",) * len(grid) if items else sem),
    )(*(() if scalars is None else (scalars,)), *args, *r_ins)
    b = n_out
    for it in items:
        it[6](res[b:b + len(it[1])])
        b += len(it[1])
    return list(res[:n_out])


def _mm_tiles(m, n_align, n, k, a_bytes, b_bytes, out_bytes):
    best = None
    tks = sorted({t for t in (k, k // 2, k // 4, 2048, 1024, 512, 256, 128) if t <= k and k % t == 0 and t % 128 == 0})
    for tm in (t for t in (2048, 1024, 512, 256, 128) if m % t == 0):
        for tn in (t for t in (2048, 1024, 512, 256, 128) if n_align % t == 0):
            for tk in tks:
                nk = k // tk
                vmem = 2 * (tm * tk * a_bytes + tk * tn * b_bytes + tm * tn * out_bytes) + tm * tn * 4
                if vmem > MM_VMEM_BUDGET:
                    continue
                traffic = m * k * a_bytes * (n // tn if nk > 1 else 1) + k * n * b_bytes * (m // tm)
                traffic += tm * tk * a_bytes + tk * tn * b_bytes + tm * tn * out_bytes
                traffic += m * n * 4 * nk if nk > 1 else 0
                cost = traffic / V7X_HBM_BYTES_PER_S + (m // tm) * (n // tn) * nk * GRID_STEP_S
                if best is None or cost < best[0]:
                    best = (cost, tm, tn, tk)
    return best[1:]


def _mm(a, b, mode, out_dtypes, *, name, n=None, b_col_off=0, resid=None, ride=None, rows=None, onto=()):
    if mode == "nn":
        m, k = a.shape
        n = b.shape[1] if n is None else n
    elif mode == "nt":
        m, k = a.shape
        n = b.shape[0]
    else:
        k, m = a.shape
        n = b.shape[1]
    m_all, (row0, m) = m, (0, m) if rows is None else rows
    n_out = len(out_dtypes)
    has_resid = resid is not None
    out_bytes = sum(jnp.dtype(dt).itemsize for dt in out_dtypes) + (4 if has_resid else 0)
    tm, tn, tk = _mm_tiles(math.gcd(m, row0) if row0 else m, math.gcd(n, b_col_off) if b_col_off else n, n, k,
                           a.dtype.itemsize, b.dtype.itemsize, out_bytes)
    nk = k // tk
    boff, roff = b_col_off // tn, row0 // tm
    dot = {"nn": _dot_nn, "nt": _dot_nt, "tn": _dot_tn}[mode]

    def body(*refs):
        a_ref, b_ref = refs[0], refs[1]
        r_ref = refs[2] if has_resid else None
        o_refs = refs[2 + has_resid: 2 + has_resid + n_out]

        def finish(r):
            if has_resid:
                r = r + r_ref[...]
            for o_ref in o_refs:
                o_ref[...] = r.astype(o_ref.dtype)

        part = dot(a_ref[...].astype(BF16), b_ref[...].astype(BF16))
        if nk == 1:
            finish(part)
            return
        acc_ref = refs[-1]
        kk = pl.program_id(2)

        @pl.when(kk == 0)
        def _():
            acc_ref[...] = part

        @pl.when(kk > 0)
        def _():
            acc_ref[...] += part

        @pl.when(kk == nk - 1)
        def _():
            finish(acc_ref[...])

    if mode == "nn":
        a_spec = pl.BlockSpec((tm, tk), lambda i, j, kk: (i + roff, kk))
        b_spec = pl.BlockSpec((tk, tn), lambda i, j, kk: (kk, j + boff))
    elif mode == "nt":
        a_spec = pl.BlockSpec((tm, tk), lambda i, j, kk: (i + roff, kk))
        b_spec = pl.BlockSpec((tn, tk), lambda i, j, kk: (j, kk))
    else:
        a_spec = pl.BlockSpec((tk, tm), lambda i, j, kk: (kk, i))
        b_spec = pl.BlockSpec((tk, tn), lambda i, j, kk: (kk, j))
    o_spec = pl.BlockSpec((tm, tn), lambda i, j, kk: (i + roff, j))
    in_specs = [a_spec, b_spec] + ([o_spec] if has_resid else [])
    args = [a, b] + ([resid] if has_resid else [])
    outs = _call(
        body, args, name=name,
        grid=(m // tm, n // tn, nk),
        in_specs=in_specs,
        out_specs=[o_spec] * n_out,
        out_shape=[jax.ShapeDtypeStruct((m_all, n), dt) for dt in out_dtypes],
        scratch_shapes=[pltpu.VMEM((tm, tn), F32)] if nk > 1 else [],
        sem=("parallel", "parallel", "arbitrary"), ride=ride, onto=onto)
    return outs[0] if n_out == 1 else tuple(outs)


def _row_tile(s):
    return _pick(s, (256, 128))


def _rms_fwd(x, g, *, name, ride=None):
    s, d = x.shape
    tr = _row_tile(s)

    def body(x_ref, g_ref, o_ref):
        xv = x_ref[...]
        r = lax.rsqrt(jnp.mean(xv * xv, axis=1, keepdims=True) + EPS)
        o_ref[...] = (xv * r * g_ref[...]).astype(o_ref.dtype)

    return _call(
        body, [x, g], name=name, grid=(s // tr,),
        in_specs=[pl.BlockSpec((tr, d), lambda i: (i, 0)), pl.BlockSpec((1, d), lambda i: (0, 0))],
        out_specs=[pl.BlockSpec((tr, d), lambda i: (i, 0))],
        out_shape=[jax.ShapeDtypeStruct((s, d), BF16)], sem=("parallel",), ride=ride)[0]


def _rms_bwd(x, g, dh, dres, *, name, ride=None):
    s, d = x.shape
    tr = _row_tile(s)

    def body(x_ref, g_ref, dh_ref, dres_ref, dx_ref, dx16_ref, dg_ref):
        i = pl.program_id(0)
        xv = x_ref[...]
        r = lax.rsqrt(jnp.mean(xv * xv, axis=1, keepdims=True) + EPS)
        xhat = xv * r
        dhv = dh_ref[...]
        dxhat = dhv * g_ref[...]
        proj = jnp.mean(dxhat * xhat, axis=1, keepdims=True)
        dx = dres_ref[...] + r * (dxhat - xhat * proj)
        dx_ref[...] = dx
        dx16_ref[...] = dx.astype(dx16_ref.dtype)

        @pl.when(i == 0)
        def _():
            dg_ref[...] = jnp.zeros_like(dg_ref)

        dg_ref[...] += jnp.sum(dhv * xhat, axis=0, keepdims=True)

    row = pl.BlockSpec((tr, d), lambda i: (i, 0))
    vec = pl.BlockSpec((1, d), lambda i: (0, 0))
    return _call(
        body, [x, g, dh, dres], name=name, grid=(s // tr,),
        in_specs=[row, vec, row, row],
        out_specs=[row, row, vec],
        out_shape=[jax.ShapeDtypeStruct((s, d), F32), jax.ShapeDtypeStruct((s, d), BF16),
                   jax.ShapeDtypeStruct((1, d), F32)],
        sem=("arbitrary",), ride=ride)


def _final_loss(x, g, target, *, name):
    s, d = x.shape
    tr = _row_tile(s)

    def body(x_ref, g_ref, t_ref, dx_ref, dg_ref, loss_ref):
        i = pl.program_id(0)
        xv = x_ref[...]
        gv = g_ref[...]
        r = lax.rsqrt(jnp.mean(xv * xv, axis=1, keepdims=True) + EPS)
        xhat = xv * r
        err = xhat * gv - t_ref[...]
        dy = err * (1.0 / d)
        dxhat = dy * gv
        proj = jnp.mean(dxhat * xhat, axis=1, keepdims=True)
        dx_ref[...] = r * (dxhat - xhat * proj)

        @pl.when(i == 0)
        def _():
            dg_ref[...] = jnp.zeros_like(dg_ref)
            loss_ref[...] = jnp.zeros_like(loss_ref)

        dg_ref[...] += jnp.sum(dy * xhat, axis=0, keepdims=True)
        part = 0.5 * jnp.sum(jnp.mean(err * err, axis=1, keepdims=True), axis=0, keepdims=True)
        loss_ref[...] += jnp.broadcast_to(part, loss_ref.shape)

    row = pl.BlockSpec((tr, d), lambda i: (i, 0))
    vec = pl.BlockSpec((1, d), lambda i: (0, 0))
    return pl.pallas_call(
        body, name=name, grid=(s // tr,),
        in_specs=[row, vec, row],
        out_specs=[row, vec, pl.BlockSpec((1, 128), lambda i: (0, 0))],
        out_shape=[jax.ShapeDtypeStruct((s, d), F32), jax.ShapeDtypeStruct((1, d), F32),
                   jax.ShapeDtypeStruct((1, 128), F32)],
        compiler_params=_cparams(("arbitrary",)),
    )(x, g, target)


def _ew(body, ins, in_blocks, outs, out_blocks, grid, *, name, ride=None):
    return _call(body, ins, name=name, grid=grid,
                 in_specs=[pl.BlockSpec(bs, im) for bs, im in in_blocks],
                 out_specs=[pl.BlockSpec(bs, im) for bs, im in out_blocks],
                 out_shape=outs, sem=("parallel",) * len(grid), ride=ride)


def _gate_merge_fwd(gates, o_sb, o_ca, *, name, ride=None):
    s, d = o_sb.shape
    tr, tc = _row_tile(s), _pick(d, (1024, 512, 256, 128))
    nc = d // tc

    def body(gs_ref, gc_ref, os_ref, oc_ref, m_ref):
        m = _sigmoid(gs_ref[...]) * os_ref[...] + _sigmoid(gc_ref[...]) * oc_ref[...]
        m_ref[...] = m.astype(m_ref.dtype)

    blk = ((tr, tc), lambda i, j: (i, j))
    return _ew(body, [gates, gates, o_sb, o_ca],
               [blk, ((tr, tc), lambda i, j: (i, j + nc)), blk, blk],
               [jax.ShapeDtypeStruct((s, d), BF16)], [blk], (s // tr, nc), name=name, ride=ride)[0]


def _gate_merge_bwd(dmerged, gates, o_sb, o_ca, *, name):
    s, d = o_sb.shape
    tr, tc = _row_tile(s), _pick(d, (1024, 512, 256, 128))
    nc = d // tc

    def body(dm_ref, gs_ref, gc_ref, os_ref, oc_ref, dgs_ref, dgc_ref, dos_ref, doc_ref):
        dm = dm_ref[...]
        ss = _sigmoid(gs_ref[...])
        sc = _sigmoid(gc_ref[...])
        dgs_ref[...] = (dm * os_ref[...] * ss * (1.0 - ss)).astype(dgs_ref.dtype)
        dgc_ref[...] = (dm * oc_ref[...] * sc * (1.0 - sc)).astype(dgc_ref.dtype)
        dos_ref[...] = (dm * ss).astype(dos_ref.dtype)
        doc_ref[...] = (dm * sc).astype(doc_ref.dtype)

    blk = ((tr, tc), lambda i, j: (i, j))
    sd = jax.ShapeDtypeStruct((s, d), BF16)
    return _ew(body, [dmerged, gates, gates, o_sb, o_ca],
               [blk, blk, ((tr, tc), lambda i, j: (i, j + nc)), blk, blk],
               [sd, sd, sd, sd], [blk, blk, blk, blk], (s // tr, nc), name=name)


def _swiglu_fwd(gu, *, name, ride=None):
    s, f2 = gu.shape
    f = f2 // 2
    tr, tc = 128, _pick(f, (512, 256, 128))

    def body(gu_ref, a_ref):
        for at in range(0, f, tc):
            gv = gu_ref[:, at:at + tc].astype(F32)
            a_ref[:, at:at + tc] = (gv * _sigmoid(gv) * gu_ref[:, f + at:f + at + tc].astype(F32)).astype(a_ref.dtype)

    row = lambda i: (i, 0)
    return _ew(body, [gu], [((tr, f2), row)], [jax.ShapeDtypeStruct((s, f), BF16)], [((tr, f), row)],
               (s // tr,), name=name, ride=ride)[0]


def _swiglu_bwd(dact, gu, *, name):
    s, f2 = gu.shape
    f = f2 // 2
    tr, tc = 128, _pick(f, (512, 256, 128))

    def body(da_ref, gu_ref, o_ref):
        for at in range(0, f, tc):
            da = da_ref[:, at:at + tc]
            gv = gu_ref[:, at:at + tc].astype(F32)
            sg = _sigmoid(gv)
            uv = gu_ref[:, f + at:f + at + tc].astype(F32)
            o_ref[:, at:at + tc] = (da * uv * sg * (1.0 + gv * (1.0 - sg))).astype(o_ref.dtype)
            o_ref[:, f + at:f + at + tc] = (da * gv * sg).astype(o_ref.dtype)

    row = lambda i: (i, 0)
    return _ew(body, [dact, gu], [((tr, f), row), ((tr, f2), row)], [jax.ShapeDtypeStruct((s, f2), BF16)],
               [((tr, f2), row)], (s // tr,), name=name)[0]


def _concat_cols(parts, *, name):
    s = parts[0].shape[0]
    widths = [p.shape[1] for p in parts]
    tr = 256

    def body(*refs):
        o_ref, at = refs[-1], 0
        for p_ref, width in zip(refs, widths):
            o_ref[:, at:at + width] = p_ref[...]
            at += width

    row = lambda i: (i, 0)
    return _ew(body, list(parts), [((tr, width), row) for width in widths],
               [jax.ShapeDtypeStruct((s, sum(widths)), parts[0].dtype)], [((tr, sum(widths)), row)],
               (s // tr,), name=name)[0]


def _ple_fwd(x, t, pe, *, name):
    s, d = x.shape
    tr, tc = _row_tile(s), _pick(d, (1024, 512, 256, 128))

    def body(x_ref, t_ref, p_ref, o_ref):
        o_ref[...] = x_ref[...] + _sigmoid(t_ref[...]) * p_ref[...]

    blk = ((tr, tc), lambda i, j: (i, j))
    return _ew(body, [x, t, pe], [blk, blk, blk],
               [jax.ShapeDtypeStruct((s, d), F32)], [blk], (s // tr, d // tc), name=name)[0]


def _ple_bwd(dx, t, pe, *, name):
    s, d = dx.shape
    tr, tc = _row_tile(s), _pick(d, (1024, 512, 256, 128))

    def body(dx_ref, t_ref, p_ref, dt_ref, dp_ref):
        dxv = dx_ref[...]
        sg = _sigmoid(t_ref[...])
        dt_ref[...] = (dxv * p_ref[...] * sg * (1.0 - sg)).astype(dt_ref.dtype)
        dp_ref[...] = (dxv * sg).astype(dp_ref.dtype)

    blk = ((tr, tc), lambda i, j: (i, j))
    sd = jax.ShapeDtypeStruct((s, d), BF16)
    return _ew(body, [dx, t, pe], [blk, blk, blk], [sd, sd], [blk, blk], (s // tr, d // tc), name=name)


def _sb_tri(later):
    row = lax.broadcasted_iota(jnp.int32, (SB_BLOCK, SB_BLOCK), 0)
    col = lax.broadcasted_iota(jnp.int32, (SB_BLOCK, SB_BLOCK), 1)
    tri = (row > col) if later else (row < col)
    return jnp.concatenate([tri.astype(BF16), jnp.ones((SB_BLOCK, SB_BLOCK), BF16)], axis=1)


def _sb_valid(i, j, own):
    if not own:
        return None
    qi = i * SB_ROWS + lax.broadcasted_iota(jnp.int32, (SB_ROWS, SB_KEYS), 0)
    ki = j * SB_KEYS + lax.broadcasted_iota(jnp.int32, (SB_ROWS, SB_KEYS), 1)
    return ki < qi


def _sb_scan(v, tri, run, later):
    hi = v.astype(BF16)
    lo = (v - hi.astype(F32)).astype(BF16)
    outs = [None] * SB_GROUPS
    for b in (reversed(range(SB_GROUPS)) if later else range(SB_GROUPS)):
        cols = slice(b * SB_BLOCK, (b + 1) * SB_BLOCK)
        r = _dot_nn(hi[:, cols], tri) + _dot_nn(lo[:, cols], tri)
        outs[b] = r[:, :SB_BLOCK] + run
        run = run + r[:, SB_BLOCK:]
    return jnp.concatenate(outs, axis=1), run


def _masked(valid, v):
    return v if valid is None else jnp.where(valid, v, 0.0)


def _sb_scores(q, kj, scale, valid):
    z = _dot_nt(q, kj) * scale
    t = jnp.log(1.0 + jnp.exp(-jnp.abs(z)))
    return jnp.minimum(z, 0.0) - t, _masked(valid, -jnp.maximum(z, 0.0) - t)


def _sb_specs(h_count, s, col0):
    q_spec = pl.BlockSpec((SB_ROWS, HEAD_DIM), lambda h, i: (i, col0 + h))
    k_spec = pl.BlockSpec((s, HEAD_DIM), lambda h, i: (0, col0 + h_count + h))
    v_spec = pl.BlockSpec((s, HEAD_DIM), lambda h, i: (0, col0 + 2 * h_count + h))
    return q_spec, k_spec, v_spec


def _sb_fwd(qkv, n_heads, col0, *, name, ride=None):
    s = qkv.shape[0]
    nq = s // SB_ROWS
    scale = HEAD_DIM ** -0.5

    def body(q_ref, k_ref, v_ref, o_ref):
        i = pl.program_id(1)
        q = q_ref[...]
        tri = _sb_tri(later=True)

        def step(j, carry, own):
            run, acc = carry
            off = pl.multiple_of(j * SB_KEYS, SB_KEYS)
            valid = _sb_valid(i, j, own)
            ls, lk = _sb_scores(q, k_ref[pl.ds(off, SB_KEYS), :], scale, valid)
            between, run = _sb_scan(lk, tri, run, later=True)
            a = _masked(valid, jnp.exp(ls + between))
            return run, acc + _dot_nn(a.astype(BF16), v_ref[pl.ds(off, SB_KEYS), :])

        carry = step(i, (jnp.zeros((SB_ROWS, SB_BLOCK), F32), jnp.zeros((SB_ROWS, HEAD_DIM), F32)), True)
        _, acc = lax.fori_loop(0, i, lambda jj, c: step(i - 1 - jj, c, False), carry)
        o_ref[...] = acc.astype(o_ref.dtype)

    q_spec, k_spec, v_spec = _sb_specs(n_heads, s, col0)
    return _call(
        body, [qkv, qkv, qkv], name=name, grid=(n_heads, nq),
        in_specs=[q_spec, k_spec, v_spec],
        out_specs=[pl.BlockSpec((SB_ROWS, HEAD_DIM), lambda h, i: (i, h))],
        out_shape=[jax.ShapeDtypeStruct((s, n_heads * HEAD_DIM), BF16)],
        sem=("parallel", "arbitrary"), ride=ride)[0]


def _sb_bwd(qkv, dy, n_heads, col0, *, name, ride=None):
    s = qkv.shape[0]
    nq = s // SB_ROWS
    scale = HEAD_DIM ** -0.5

    def body(q_ref, k_ref, v_ref, dy_ref, dq_ref, dk_ref, dv_ref, e_scr, sg_scr, dk_acc, dv_acc):
        i = pl.program_id(1)
        q = q_ref[...]
        dyv = dy_ref[...]

        @pl.when(i == 0)
        def _():
            dk_acc[...] = jnp.zeros_like(dk_acc)
            dv_acc[...] = jnp.zeros_like(dv_acc)

        tri_later = _sb_tri(later=True)

        def pass1(j, run, own):
            off = pl.multiple_of(j * SB_KEYS, SB_KEYS)
            valid = _sb_valid(i, j, own)
            ls, lk = _sb_scores(q, k_ref[pl.ds(off, SB_KEYS), :], scale, valid)
            between, run = _sb_scan(lk, tri_later, run, later=True)
            a = _masked(valid, jnp.exp(ls + between))
            e_scr[j] = a * _dot_nt(dyv, v_ref[pl.ds(off, SB_KEYS), :])
            sg_scr[j] = jnp.exp(ls)
            dv_acc[pl.ds(off, SB_KEYS), :] += _dot_tn(a.astype(BF16), dyv)
            return run

        lax.fori_loop(0, i, lambda jj, run: pass1(i - 1 - jj, run, False),
                      pass1(i, jnp.zeros((SB_ROWS, SB_BLOCK), F32), True))

        tri_earlier = _sb_tri(later=False)

        def pass2(j, carry, own):
            run, dq = carry
            off = pl.multiple_of(j * SB_KEYS, SB_KEYS)
            kj = k_ref[pl.ds(off, SB_KEYS), :]
            sg = sg_scr[j]
            e = e_scr[j]
            before, run = _sb_scan(e, tri_earlier, run, later=False)
            dz = _masked(_sb_valid(i, j, own), e * (1.0 - sg) - sg * before) * scale
            dzb = dz.astype(BF16)
            dk_acc[pl.ds(off, SB_KEYS), :] += _dot_tn(dzb, q)
            return run, dq + _dot_nn(dzb, kj)

        init = (jnp.zeros((SB_ROWS, SB_BLOCK), F32), jnp.zeros((SB_ROWS, HEAD_DIM), F32))
        _, dq = pass2(i, lax.fori_loop(0, i, lambda j, c: pass2(j, c, False), init), True)
        dq_ref[...] = dq.astype(dq_ref.dtype)

        @pl.when(i == nq - 1)
        def _():
            dk_ref[...] = dk_acc[...].astype(dk_ref.dtype)
            dv_ref[...] = dv_acc[...].astype(dv_ref.dtype)

    q_spec, k_spec, v_spec = _sb_specs(n_heads, s, col0)
    blk = pl.BlockSpec((SB_ROWS, HEAD_DIM), lambda h, i: (i, h))
    full = pl.BlockSpec((s, HEAD_DIM), lambda h, i: (0, h))
    sd = jax.ShapeDtypeStruct((s, n_heads * HEAD_DIM), BF16)
    return _call(
        body, [qkv, qkv, qkv, dy], name=name, grid=(n_heads, nq),
        in_specs=[q_spec, k_spec, v_spec, blk],
        out_specs=[blk, full, full],
        out_shape=[sd, sd, sd],
        scratch_shapes=[pltpu.VMEM((s // SB_KEYS, SB_ROWS, SB_KEYS), F32), pltpu.VMEM((s // SB_KEYS, SB_ROWS, SB_KEYS), F32),
                        pltpu.VMEM((s, HEAD_DIM), F32), pltpu.VMEM((s, HEAD_DIM), F32)],
        sem=("parallel", "arbitrary"), ride=ride)


def _band_bias(rel_bias):
    h = rel_bias.shape[0]
    width = BAND + CHUNK
    first = width - 1 - N_REL
    line = jnp.concatenate([jnp.broadcast_to(rel_bias[:, :1], (h, first)), rel_bias], axis=1)
    tiled = jnp.broadcast_to(line[:, None, :], (h, CHUNK, width - 1)).reshape(h, CHUNK * (width - 1))
    skew = jnp.pad(tiled, ((0, 0), (0, CHUNK))).reshape(h, CHUNK, width)[:, ::-1, :BAND]
    seen = jnp.arange(BAND) >= CHUNK
    return jnp.where(seen[None, None, :], skew, NEG)


def _band_bias_grad(dbias):
    h = dbias.shape[0]
    width = BAND + CHUNK
    flipped = jnp.pad(dbias[:, ::-1, :], ((0, 0), (0, 0), (0, CHUNK)))
    skew = flipped.reshape(h, CHUNK * width)[:, :CHUNK * (width - 1)].reshape(h, CHUNK, width - 1)
    diag = jnp.sum(skew, axis=1)
    first = width - 1 - N_REL
    clipped = jnp.sum(diag[:, :first + 1], axis=1, keepdims=True)
    return jnp.concatenate([clipped, diag[:, first + 1:]], axis=1)


def _group_bias(band):
    return jnp.concatenate([jnp.pad(band, ((0, 0), (0, 0), ((u + 1) * CHUNK, (CA_PER_STEP - 1 - u) * CHUNK)),
                                    constant_values=NEG) for u in range(CA_PER_STEP)], axis=1)


def _group_bias_grad(dgroup):
    return sum(dgroup[:, u * CHUNK:(u + 1) * CHUNK, (u + 1) * CHUNK:(u + 1) * CHUNK + BAND] for u in range(CA_PER_STEP))


def _ca_load_padded(k_ref, v_ref, kp, vp, s):
    kp[pl.ds(0, CA_PAD), :] = jnp.zeros((CA_PAD, HEAD_DIM), kp.dtype)
    vp[pl.ds(0, CA_PAD), :] = jnp.zeros((CA_PAD, HEAD_DIM), vp.dtype)
    kp[pl.ds(CA_PAD, s), :] = k_ref[...]
    vp[pl.ds(CA_PAD, s), :] = v_ref[...]


def _ca_weights(q, kb, bias, off, scale):
    z = _dot_nt(q, kb) * scale + bias
    pos = off + lax.broadcasted_iota(jnp.int32, (CA_ROWS, CA_BAND), 1)
    z = jnp.where(pos >= CA_PAD, z, NEG)
    p = jnp.exp(z - jnp.max(z, axis=1, keepdims=True))
    return p / jnp.sum(p, axis=1, keepdims=True)


def _ca_specs(h_count, s, col0):
    q_spec = pl.BlockSpec((CA_ROWS, HEAD_DIM), lambda h, c: (c, col0 + h))
    k_spec = pl.BlockSpec((s, HEAD_DIM), lambda h, c: (0, col0 + h_count + h))
    v_spec = pl.BlockSpec((s, HEAD_DIM), lambda h, c: (0, col0 + 2 * h_count + h))
    b_spec = pl.BlockSpec((1, CA_ROWS, CA_BAND), lambda h, c: (h, 0, 0))
    return q_spec, k_spec, v_spec, b_spec


def _ca_fwd(qkv, bias, n_heads, col0, *, name, ride=None):
    s = qkv.shape[0]
    nc = s // CA_ROWS
    scale = HEAD_DIM ** -0.5

    def body(q_ref, k_ref, v_ref, b_ref, o_ref, kp, vp):
        c = pl.program_id(1)

        @pl.when(c == 0)
        def _():
            _ca_load_padded(k_ref, v_ref, kp, vp, s)

        off = pl.multiple_of(c * CA_ROWS, CA_ROWS)
        w = _ca_weights(q_ref[...], kp[pl.ds(off, CA_BAND), :], b_ref[0], off, scale)
        o_ref[...] = _dot_nn(w.astype(BF16), vp[pl.ds(off, CA_BAND), :]).astype(o_ref.dtype)

    q_spec, k_spec, v_spec, b_spec = _ca_specs(n_heads, s, col0)
    return _call(
        body, [qkv, qkv, qkv, bias], name=name, grid=(n_heads, nc),
        in_specs=[q_spec, k_spec, v_spec, b_spec],
        out_specs=[pl.BlockSpec((CA_ROWS, HEAD_DIM), lambda h, c: (c, h))],
        out_shape=[jax.ShapeDtypeStruct((s, n_heads * HEAD_DIM), BF16)],
        scratch_shapes=[pltpu.VMEM((s + CA_PAD, HEAD_DIM), BF16), pltpu.VMEM((s + CA_PAD, HEAD_DIM), BF16)],
        sem=("parallel", "arbitrary"), ride=ride)[0]


def _ca_bwd(qkv, bias, dy, n_heads, col0, *, name, ride=None):
    s = qkv.shape[0]
    nc = s // CA_ROWS
    scale = HEAD_DIM ** -0.5

    def body(q_ref, k_ref, v_ref, b_ref, dy_ref, dq_ref, dk_ref, dv_ref, db_ref, kp, vp, dkp, dvp):
        c = pl.program_id(1)

        @pl.when(c == 0)
        def _():
            _ca_load_padded(k_ref, v_ref, kp, vp, s)
            dkp[...] = jnp.zeros_like(dkp)
            dvp[...] = jnp.zeros_like(dvp)
            db_ref[...] = jnp.zeros_like(db_ref)

        off = pl.multiple_of(c * CA_ROWS, CA_ROWS)
        band = pl.ds(off, CA_BAND)
        q = q_ref[...]
        dyv = dy_ref[...]
        kb = kp[band, :]
        w = _ca_weights(q, kb, b_ref[0], off, scale)
        dw = _dot_nt(dyv, vp[band, :])
        dvp[band, :] += _dot_tn(w.astype(BF16), dyv)
        dz = w * (dw - jnp.sum(w * dw, axis=1, keepdims=True))
        db_ref[0] += dz
        dzs = (dz * scale).astype(BF16)
        dq_ref[...] = _dot_nn(dzs, kb).astype(dq_ref.dtype)
        dkp[band, :] += _dot_tn(dzs, q)

        @pl.when(c == nc - 1)
        def _():
            dk_ref[...] = dkp[pl.ds(CA_PAD, s), :].astype(dk_ref.dtype)
            dv_ref[...] = dvp[pl.ds(CA_PAD, s), :].astype(dv_ref.dtype)

    q_spec, k_spec, v_spec, b_spec = _ca_specs(n_heads, s, col0)
    blk = pl.BlockSpec((CA_ROWS, HEAD_DIM), lambda h, c: (c, h))
    full = pl.BlockSpec((s, HEAD_DIM), lambda h, c: (0, h))
    sd = jax.ShapeDtypeStruct((s, n_heads * HEAD_DIM), BF16)
    return _call(
        body, [qkv, qkv, qkv, bias, dy], name=name, grid=(n_heads, nc),
        in_specs=[q_spec, k_spec, v_spec, b_spec, blk],
        out_specs=[blk, full, full, b_spec],
        out_shape=[sd, sd, sd, jax.ShapeDtypeStruct((n_heads, CA_ROWS, CA_BAND), F32)],
        scratch_shapes=[pltpu.VMEM((s + CA_PAD, HEAD_DIM), BF16), pltpu.VMEM((s + CA_PAD, HEAD_DIM), BF16),
                        pltpu.VMEM((s + CA_PAD, HEAD_DIM), F32), pltpu.VMEM((s + CA_PAD, HEAD_DIM), F32)],
        sem=("parallel", "arbitrary"), ride=ride)


EARLY = ("w_sb_out", "w_ca_out", "w_mix_out")


def _step(x, p, target, small, comm):
    w = comm.w
    d = x.shape[1]
    n_sb = w["w_sb_out"].shape[0] // HEAD_DIM
    n_ca = w["w_ca_out"].shape[0] // HEAD_DIM
    qkv_cols = 3 * HEAD_DIM * (n_sb + n_ca)
    ca_col0 = 3 * n_sb
    both = (F32, BF16)

    h1 = _rms_fwd(x, small["g_mix"], name="rms_mix")
    ffn, ple = ("w_ffn_in",), ("w_ple_gate", "w_ple_in")
    qkv = _mm(h1, w["w_in"], "nn", (BF16,), name="proj_qkv", n=qkv_cols, ride=comm.gather(EARLY, "near"))
    gates = _mm(h1, w["w_in"], "nn", (F32,), name="proj_gates", n=2 * d, b_col_off=qkv_cols,
                ride=comm.gather(ffn, "near", comm.gather(EARLY, "far"), (0, 8)))
    bias = _group_bias(_band_bias(small["rel_bias"]))
    y_sb = _sb_fwd(qkv, n_sb, 0, name="sb_fwd", ride=comm.gather(ffn, "near", comm.gather(EARLY, "pair"), (1, 8, 7)))
    y_ca = _ca_fwd(qkv, bias, n_ca, ca_col0, name="ca_fwd", ride=comm.gather(ffn, "far"))
    out = ("w_ffn_out",)
    o_sb = _mm(y_sb, w["w_sb_out"], "nn", (F32,), name="sb_out", ride=comm.gather(out, "near", part=(0, 4)))
    o_ca = _mm(y_ca, w["w_ca_out"], "nn", (F32,), name="ca_out", ride=comm.gather(out, "near", part=(1, 4)))
    merged = _gate_merge_fwd(gates, o_sb, o_ca, name="gate_merge",
                             ride=comm.gather(out, "near", comm.gather(ffn, "pair"), (2, 4)))
    x1 = _mm(merged, w["w_mix_out"], "nn", (F32,), name="mix_out", resid=x, ride=comm.gather(out, "near", part=(3, 4)))
    h2 = _rms_fwd(x1, small["g_ffn"], name="rms_ffn")
    gu = _mm(h2, w["w_ffn_in"], "nn", (BF16,), name="ffn_in", ride=comm.gather(ple, "near", comm.gather(out, "far")))
    act = _swiglu_fwd(gu, name="swiglu", ride=comm.gather(ple, "far", comm.gather(out, "pair")))
    x2 = _mm(act, w["w_ffn_out"], "nn", (F32,), name="ffn_out", resid=x1, ride=comm.gather(ple, "pair"))
    h3 = _rms_fwd(x2, small["g_ple"], name="rms_ple")
    t = _mm(h3, w["w_ple_gate"], "nn", (F32,), name="ple_gate")
    pe = _mm(p, w["w_ple_in"], "nn", (F32,), name="ple_in")
    x3 = _ple_fwd(x2, t, pe, name="ple_add")

    gs = {}
    dx3, gs["g_final"], loss = _final_loss(x3, small["g_final"], target, name="final_loss")
    dt, dpe = _ple_bwd(dx3, t, pe, name="ple_bwd")
    comm.grad("w_ple_in", *_mm(p, dpe, "tn", both, name="dw_ple_in"))
    comm.grad("w_ple_gate", *_mm(h3, dt, "tn", both, name="dw_ple_gate"))
    ple = ("w_ple_in", "w_ple_gate")
    dh3 = _mm(dt, w["w_ple_gate"], "nt", (F32,), name="dh_ple", ride=comm.pair(ple))
    dx2, dx2_16, gs["g_ple"] = _rms_bwd(x2, small["g_ple"], dh3, dx3, name="rms_ple_bwd")
    comm.add(ple)
    comm.grad("w_ffn_out", *_mm(act, dx2_16, "tn", both, name="dw_ffn_out", ride=comm.chips(ple)))
    dact = _mm(dx2_16, w["w_ffn_out"], "nt", (F32,), name="dact", ride=comm.pair(("w_ffn_out",)))
    dgu = _swiglu_bwd(dact, gu, name="swiglu_bwd")
    comm.sum(ple)
    comm.add(("w_ffn_out",))
    comm.grad("w_ffn_in", *_mm(h2, dgu, "tn", both, name="dw_ffn_in",
                               ride=comm.share(ple, comm.chips(("w_ffn_out",)))))
    dh2 = _mm(dgu, w["w_ffn_in"], "nt", (F32,), name="dh_ffn", ride=comm.pair(("w_ffn_in",)))
    dx1, dx1_16, gs["g_ffn"] = _rms_bwd(x1, small["g_ffn"], dh2, dx2, name="rms_ffn_bwd")
    comm.add(("w_ffn_in",))
    comm.sum(("w_ffn_out",))
    comm.grad("w_mix_out", *_mm(merged, dx1_16, "tn", both, name="dw_mix_out", ride=comm.share(("w_ffn_out",))))
    dmerged = _mm(dx1_16, w["w_mix_out"], "nt", (F32,), name="dmerged", ride=comm.pair(("w_mix_out",)))
    dg_sb, dg_ca, do_sb, do_ca = _gate_merge_bwd(dmerged, gates, o_sb, o_ca, name="gate_merge_bwd")
    comm.add(("w_mix_out",))
    comm.grad("w_sb_out", *_mm(y_sb, do_sb, "tn", both, name="dw_sb_out"))
    comm.grad("w_ca_out", *_mm(y_ca, do_ca, "tn", both, name="dw_ca_out"))
    outs = ("w_sb_out", "w_ca_out")
    dy_sb = _mm(do_sb, w["w_sb_out"], "nt", (BF16,), name="dy_sb", ride=comm.pair(outs))
    dy_ca = _mm(do_ca, w["w_ca_out"], "nt", (BF16,), name="dy_ca")
    comm.add(outs)
    dq_sb, dk_sb, dv_sb = _sb_bwd(qkv, dy_sb, n_sb, 0, name="sb_bwd", ride=comm.chips(("w_ffn_in",)))
    comm.sum(("w_ffn_in",))
    late = ("w_mix_out",) + outs
    dq_ca, dk_ca, dv_ca, dbias = _ca_bwd(qkv, bias, dy_ca, n_ca, ca_col0, name="ca_bwd",
                                         ride=comm.chips(late, comm.share(("w_ffn_in",))))
    comm.sum(late)
    gs["rel_bias"] = _band_bias_grad(_group_bias_grad(dbias))
    dproj = _concat_cols([dq_sb, dk_sb, dv_sb, dq_ca, dk_ca, dv_ca, dg_sb, dg_ca], name="dproj")
    comm.grad("w_in", *_mm(h1, dproj, "tn", both, name="dw_in", ride=comm.share(late)))
    half = x.shape[0] // 2
    dh1 = _mm(dproj, w["w_in"], "nt", (F32,), name="dh_mix_top", rows=(0, half), ride=comm.pair(("w_in",)))
    comm.add(("w_in",))
    dh1 = _mm(dproj, w["w_in"], "nt", (F32,), name="dh_mix_bottom", rows=(half, half), onto=(dh1,),
              ride=comm.tail(TAIL_SECOND))
    grad_x, _, gs["g_mix"] = _rms_bwd(x, small["g_mix"], dh1, dx1, name="rms_mix_bwd", ride=comm.tail(TAIL_FIRST))
    return loss, grad_x, gs


def _position():
    x, y, c = lax.axis_index("x"), lax.axis_index("y"), lax.axis_index("c")
    chips = [(1 - x, y), (x, 1 - y), (1 - x, 1 - y)]
    return x, y, c, chips


def _aligned(v, m):
    return v if isinstance(v, int) else pl.multiple_of(v, m)


def _piece_dims(shape, axis):
    k, n = shape
    return (k // 2, n // N_CHIPS) if axis == 1 else (k // N_CHIPS // 2, n)


def _piece(ref, shape, axis, j, h, part=(0, 1)):
    pr, pc = _piece_dims(shape, axis)
    nr = pr // part[1] * (part[2] if len(part) > 2 else 1)
    r0 = part[0] * (pr // part[1])
    if axis == 1:
        return ref.at[pl.ds(_aligned(h * pr + r0, 16), nr), pl.ds(_aligned(j * pc, 128), pc)]
    return ref.at[pl.ds(_aligned((2 * j + h) * pr + r0, 16), nr), :]


def _shard_half(ref, h):
    rows = ref.shape[0] // 2
    return ref.at[pl.ds(_aligned(h * rows, 16), rows), :]


def _remote(src, dst, send_sems, recv_sems, k, to):
    return pltpu.make_async_remote_copy(src_ref=src, dst_ref=dst, send_sem=send_sems.at[k],
                                        recv_sem=recv_sems.at[k], device_id=to, device_id_type=MESH)


def _prefetch_call(body, scalars, ins, in_specs, out_shape, out_specs, grid, *, name, ride=None):
    single = not isinstance(out_shape, (list, tuple))
    outs = _call(body, ins, name=name, grid=grid, in_specs=in_specs,
                 out_specs=[out_specs] if single else out_specs, out_shape=[out_shape] if single else out_shape,
                 sem=("parallel",) * len(grid), ride=ride, scalars=scalars)
    return outs[0] if single else outs


def _slab_tiles(pr, pc):
    tc = pc if pc <= 4096 else _pick(pc, (2048, 1024, 512, 256, 128))
    tr = next(t for t in (1024, 512, 256, 128, 64, 32, 16) if pr % t == 0 and t * tc <= 512 * 1024)
    return tr, tc


def _cast_place(w, axis, pos, *, name, ride=None):
    ks, ns = w.shape
    shape = (ks, ns * N_CHIPS) if axis == 1 else (ks * N_CHIPS, ns)
    tr, tc = _slab_tiles(ks, ns)
    nr, nc = ks // tr, ns // tc

    def body(pos_ref, w_ref, o_ref):
        o_ref[...] = w_ref[...].astype(o_ref.dtype)

    if axis == 1:
        out_map = lambda i, j, pos_ref: (i, pos_ref[0] * nc + j)
    else:
        out_map = lambda i, j, pos_ref: (pos_ref[0] * nr + i, j)
    return _prefetch_call(body, pos, [w], [pl.BlockSpec((tr, tc), lambda i, j, pos_ref: (i, j))],
                          jax.ShapeDtypeStruct(shape, BF16), pl.BlockSpec((tr, tc), out_map), (nr, nc), name=name, ride=ride)


def _run(ride, *, name):
    if ride is None:
        return

    def body(o_ref):
        o_ref[...] = jnp.zeros_like(o_ref)

    _call(body, [], name=name, grid=(1,), in_specs=[], out_specs=[pl.BlockSpec((8, 128), lambda i: (0, 0))],
          out_shape=[jax.ShapeDtypeStruct((8, 128), F32)], ride=ride)


def _ride_gather(ride, w, n, axis, stage, part=(0, 1)):
    shape = w[n].shape
    piece = functools.partial(_piece, shape=shape, axis=axis)
    span = part[2] if len(part) > 2 else 1
    halves = [(2 * part[0] + t * span, 2 * part[1], span) for t in range(2)]

    def copies(ins, outs, send_sems, recv_sems, arriving):
        x, y, c, chips = _position()
        me, (xn, yn, dn) = 2 * x + y, [2 * px + py for px, py in chips]
        if stage == "near":
            plan = [(me, c, part, (1 - x, y, c), xn, c, part), (me, c, part, (x, 1 - y, c), yn, c, part)]
        elif stage == "far":
            plan = [(yn, c, halves[1], (1 - x, y, c), dn, c, halves[1]), (xn, c, halves[0], (x, 1 - y, c), dn, c, halves[0])]
        else:
            plan = [(j, c, part, (x, y, 1 - c), j, 1 - c, part) for j in (xn, yn, dn)]
        out = []
        for k, (chip, h, rows, to, from_chip, from_h, from_rows) in enumerate(plan):
            if arriving:
                lands = piece(outs[0], j=from_chip, h=from_h, part=from_rows)
                out.append(_remote(lands, lands, send_sems, recv_sems, k, to))
            else:
                out.append(_remote(piece(ins[0], j=chip, h=h, part=rows), piece(outs[0], j=chip, h=h, part=rows),
                                   send_sems, recv_sems, k, to))
        return out

    def start(*refs):
        for cp in copies(*refs, arriving=False):
            cp.start()

    def finish(*refs):
        for cp in copies(*refs, arriving=True):
            cp.wait_recv()
        for cp in copies(*refs, arriving=False):
            cp.wait_send()

    ride.add([w[n]], [jax.ShapeDtypeStruct(shape, w[n].dtype)], {0: 0}, 3, start, finish,
             lambda outs: w.__setitem__(n, outs[0]))


def _ride_pair(ride, st, axis):
    shape = st["g16"].shape
    pr, pc = _piece_dims(shape, axis)

    def copies(ins, outs, send_sems, recv_sems):
        x, y, c, _ = _position()
        return [_remote(_piece(ins[0], shape, axis, j, 1 - c), outs[0].at[j], send_sems, recv_sems, j, (x, y, 1 - c))
                for j in range(N_CHIPS)]

    def start(*refs):
        for cp in copies(*refs):
            cp.start()

    def finish(*refs):
        for cp in copies(*refs):
            cp.wait()

    ride.add([st["g16"]], [jax.ShapeDtypeStruct((N_CHIPS, pr, pc), BF16)], {}, N_CHIPS, start, finish,
             lambda outs: st.__setitem__("sib", outs[0]))


def _ride_chips(ride, st, rows=None):
    _, pr, pc = st["s16"].shape
    r0, nr = (0, pr) if rows is None else rows

    def copies(ins, outs, send_sems, recv_sems):
        x, y, c, chips = _position()
        return [_remote(ins[0].at[2 * px + py, pl.ds(r0, nr), :], outs[0].at[k, pl.ds(r0, nr), :],
                        send_sems, recv_sems, k, (px, py, c)) for k, (px, py) in enumerate(chips)]

    def start(*refs):
        for cp in copies(*refs):
            cp.start()

    def finish(*refs):
        for cp in copies(*refs):
            cp.wait()

    ins, aliases = ([st["s16"], st["recv"]], {1: 0}) if "recv" in st else ([st["s16"]], {})
    ride.add(ins, [jax.ShapeDtypeStruct((3, pr, pc), BF16)], aliases, 3, start, finish,
             lambda outs: st.__setitem__("recv", outs[0]))


def _ride_share(ride, st):
    def sent(ins, outs, send_sems, recv_sems):
        x, y, c, _ = _position()
        return _remote(_shard_half(ins[0], c), _shard_half(outs[0], c), send_sems, recv_sems, 0, (x, y, 1 - c))

    def landed(ins, outs, send_sems, recv_sems):
        x, y, c, _ = _position()
        other = _shard_half(outs[0], 1 - c)
        return _remote(other, other, send_sems, recv_sems, 0, (x, y, 1 - c))

    def start(*refs):
        sent(*refs).start()

    def finish(*refs):
        landed(*refs).wait_recv()
        sent(*refs).wait_send()

    ride.add([st["shard"]], [jax.ShapeDtypeStruct(st["shard"].shape, F32)], {0: 0}, 1, start, finish,
             lambda outs: st.__setitem__("g", outs[0]))


def _piece_block(axis, nr, nc, chip):
    if axis == 1:
        return lambda *a: ((a[-1][1] * nr + a[-3]), (a[0] if chip is None else chip(a[-1])) * nc + a[-2])
    return lambda *a: ((2 * (a[0] if chip is None else chip(a[-1])) + a[-1][1]) * nr + a[-3], a[-2])


def _pair_add(g32, sib, axis, pos, *, name):
    _, pr, pc = sib.shape
    tr, tc = _slab_tiles(pr, pc)
    nr, nc = pr // tr, pc // tc

    def body(pos_ref, g_ref, b_ref, o16_ref):
        o16_ref[0] = (g_ref[...] + b_ref[0].astype(F32)).astype(o16_ref.dtype)

    blk = pl.BlockSpec((1, tr, tc), lambda j, i, k, pos_ref: (j, i, k))
    return _prefetch_call(body, pos, [g32, sib], [pl.BlockSpec((tr, tc), _piece_block(axis, nr, nc, None)), blk],
                          jax.ShapeDtypeStruct(sib.shape, BF16), blk, (N_CHIPS, nr, nc), name=name)


def _chip_sum(g32, sib, recv, axis, pos, *, name):
    _, pr, pc = sib.shape
    tr, tc = _slab_tiles(pr, pc)
    nr, nc = pr // tr, pc // tc

    def body(pos_ref, g_ref, b_ref, r_ref, o_ref):
        pair = g_ref[...] + b_ref[0].astype(F32)
        o_ref[...] = ((pair + r_ref[0].astype(F32)) + r_ref[1].astype(F32)) + r_ref[2].astype(F32)

    return _prefetch_call(
        body, pos, [g32, sib, recv],
        [pl.BlockSpec((tr, tc), _piece_block(axis, nr, nc, lambda pos_ref: pos_ref[0])),
         pl.BlockSpec((1, tr, tc), lambda i, k, pos_ref: (pos_ref[0], i, k)),
         pl.BlockSpec((3, tr, tc), lambda i, k, pos_ref: (0, i, k))],
        jax.ShapeDtypeStruct((2 * pr, pc), F32),
        pl.BlockSpec((tr, tc), lambda i, k, pos_ref: (pos_ref[1] * nr + i, k)), (nr, nc), name=name)


class _Comm:
    def __init__(self, pos, w):
        self.pos, self.w, self.st = pos, w, {n: {} for n, _ in BIG}

    def gather(self, names, stage, ride=None, part=(0, 1)):
        ride = _Ride() if ride is None else ride
        for n in names:
            _ride_gather(ride, self.w, n, AXIS[n], stage, part)
        return ride

    def grad(self, n, g32, g16):
        self.st[n].update(g32=g32, g16=g16)

    def pair(self, names, ride=None):
        ride = _Ride() if ride is None else ride
        for n in names:
            _ride_pair(ride, self.st[n], AXIS[n])
        return ride

    def add(self, names):
        for n in names:
            st = self.st[n]
            st["s16"] = _pair_add(st["g32"], st["sib"], AXIS[n], self.pos, name="rs_add_" + n)

    def chips(self, names, ride=None, rows=None):
        ride = _Ride() if ride is None else ride
        for n in names:
            _ride_chips(ride, self.st[n], rows)
        return ride

    def sum(self, names):
        for n in names:
            st = self.st[n]
            st["shard"] = _chip_sum(st["g32"], st["sib"], st["recv"], AXIS[n], self.pos, name="rs_sum_" + n)

    def share(self, names, ride=None):
        ride = _Ride() if ride is None else ride
        for n in names:
            _ride_share(ride, self.st[n])
        return ride

    def tail(self, count):
        st = self.st["w_in"]
        rows, at = st["s16"].shape[1], st.get("at", 0)
        st["at"] = at + count
        return self.chips(("w_in",), rows=(at * rows // TAIL_PARTS, count * rows // TAIL_PARTS))

    def tail_rest(self):
        return self.tail(TAIL_PARTS - self.st["w_in"].get("at", 0))

    def result(self, n):
        return self.st[n]["g"]


class _NoComm:
    def __init__(self, w):
        self.w, self.st = w, {}

    def grad(self, n, g32, g16):
        self.st[n] = (g32, g16)

    def result(self, n):
        return self.st[n]

    def add(self, names):
        pass

    sum = add

    def gather(self, names, *args, **kwargs):
        return None

    pair = chips = share = tail = gather


def _small_all_reduce(vec, *, name):
    r = vec.shape[0]

    def body(vec_ref, out_ref, slots, send_sems, recv_sems):
        x, y, c, _ = _position()
        me = 4 * x + 2 * y + c
        slots[me] = vec_ref[...]
        sends = []
        for k in range(1, 8):
            to = (x ^ (k >> 2), y ^ ((k >> 1) & 1), c ^ (k & 1))
            cp = _remote(slots.at[me], slots.at[me], send_sems, recv_sems, k - 1, to)
            cp.start()
            sends.append(cp)
        for k in range(1, 8):
            frm = 4 * (x ^ (k >> 2)) + 2 * (y ^ ((k >> 1) & 1)) + (c ^ (k & 1))
            _remote(slots.at[frm], slots.at[frm], send_sems, recv_sems, k - 1, (x, y, c)).wait_recv()
        for cp in sends:
            cp.wait_send()
        total = slots[0]
        for d in range(1, 8):
            total = total + slots[d]
        out_ref[...] = total

    return pl.pallas_call(
        body, name=name,
        in_specs=[pl.BlockSpec(memory_space=pltpu.VMEM)], out_specs=pl.BlockSpec(memory_space=pltpu.VMEM),
        out_shape=jax.ShapeDtypeStruct((r, 128), F32),
        scratch_shapes=[pltpu.VMEM((8, r, 128), F32), pltpu.SemaphoreType.DMA((7,)), pltpu.SemaphoreType.DMA((7,))],
    )(vec)


SC_TILES = 32
SC_LANES = 16
SC_TILE_BUDGET = 400 * 1024


def _adamw_update(wv, gv, mv, vv):
    nm = ADAM_B1 * mv + (1.0 - ADAM_B1) * gv
    nv = ADAM_B2 * vv + (1.0 - ADAM_B2) * (gv * gv)
    m_hat = nm / (1.0 - ADAM_B1 ** ADAM_STEP)
    v_hat = nv / (1.0 - ADAM_B2 ** ADAM_STEP)
    return -ADAM_LR * (m_hat / (jnp.sqrt(v_hat) + ADAM_EPS) + ADAM_WD * wv), nm, nv


def _adamw_sc(w, g, m, v, *, name):
    r, c = w.shape
    groups = r // 8
    per_tile = -(-groups // SC_TILES)
    cb = c if 4 * 8 * c * 4 <= SC_TILE_BUDGET else _pick(c, (2048, 1024, 512, 256, 128))

    def body(w_hbm, g_hbm, m_hbm, v_hbm, d_hbm, nm_hbm, nv_hbm, wb, gb, mb, vb):
        tile = lax.axis_index("sc_tile") * 2 + lax.axis_index("sc_core")

        def update(group):
            for c0 in range(0, c, cb):
                at = (pl.ds(group * 8, 8), pl.ds(c0, cb))
                for hbm, buf in ((w_hbm, wb), (g_hbm, gb), (m_hbm, mb), (v_hbm, vb)):
                    pltpu.sync_copy(hbm.at[at], buf)

                @pl.loop(0, 8)
                def _(rr):
                    @pl.loop(0, cb, step=SC_LANES)
                    def _(i):
                        lanes = (rr, pl.ds(i, SC_LANES))
                        wb[lanes], mb[lanes], vb[lanes] = _adamw_update(wb[lanes], gb[lanes], mb[lanes], vb[lanes])

                for buf, hbm in ((wb, d_hbm), (mb, nm_hbm), (vb, nv_hbm)):
                    pltpu.sync_copy(buf, hbm.at[at])

        @pl.loop(0, per_tile)
        def _(k):
            group = k * SC_TILES + tile
            if groups % SC_TILES:
                pl.when(group < groups)(lambda: update(group))
            else:
                update(group)

    sd = jax.ShapeDtypeStruct((r, c), F32)
    return pl.kernel(body, name=name, out_type=[sd, sd, sd],
                     mesh=plsc.VectorSubcoreMesh(core_axis_name="sc_core", subcore_axis_name="sc_tile"),
                     scratch_types=[pltpu.VMEM((8, cb), F32)] * 4)(w, g, m, v)


def _adamw(w, g, m, v, *, name, ride=None):
    r, c = w.shape
    tc = c if c <= 4096 else _pick(c, (2048, 1024, 512, 256, 128))
    tr = next(t for t in (512, 256, 128, 64, 32, 16, 8) if r % t == 0 and t * tc <= 256 * 1024)

    def body(w_ref, g_ref, m_ref, v_ref, d_ref, nm_ref, nv_ref):
        gv = g_ref[...]
        nm = ADAM_B1 * m_ref[...] + (1.0 - ADAM_B1) * gv
        nv = ADAM_B2 * v_ref[...] + (1.0 - ADAM_B2) * (gv * gv)
        m_hat = nm / (1.0 - ADAM_B1 ** ADAM_STEP)
        v_hat = nv / (1.0 - ADAM_B2 ** ADAM_STEP)
        d_ref[...] = -ADAM_LR * (m_hat / (jnp.sqrt(v_hat) + ADAM_EPS) + ADAM_WD * w_ref[...])
        nm_ref[...] = nm
        nv_ref[...] = nv

    blk = ((tr, tc), lambda i, j: (i, j))
    sd = jax.ShapeDtypeStruct((r, c), F32)
    return _ew(body, [w, g, m, v], [blk] * 4, [sd, sd, sd], [blk] * 3, (r // tr, c // tc), name=name, ride=ride)


BIG = (("w_in", 1), ("w_sb_out", 1), ("w_ca_out", 1), ("w_mix_out", 0), ("w_ffn_in", 1), ("w_ffn_out", 0),
       ("w_ple_in", 1), ("w_ple_gate", 0))
AXIS = dict(BIG)
HEAD_PARTS = 8
HEAD_HOSTS = ("w_ffn_in", "w_ffn_out")
TAIL_PARTS = 16
TAIL_SECOND = 4
TAIL_FIRST = 2
TAIL_HOSTS = {}
ON_SPARSECORE = tuple(n for n, _ in BIG if n != "w_in")
SMALL = ("rel_bias", "g_mix", "g_ffn", "g_ple", "g_final")
ORDER = ("w_in", "w_sb_out", "w_ca_out", "w_mix_out", "rel_bias", "g_mix", "g_ffn", "g_ple", "g_final",
         "w_ffn_in", "w_ffn_out", "w_ple_in", "w_ple_gate")


def _pack(parts):
    flat = jnp.concatenate([a.reshape(-1) for a in parts])
    rows = -(-flat.shape[0] // 1024) * 8
    return jnp.pad(flat, (0, rows * 128 - flat.shape[0])).reshape(rows, 128)


def _unpack(packed, like):
    flat, out, at = packed.reshape(-1), [], 0
    for a in like:
        out.append(flat[at:at + a.size].reshape(a.shape))
        at += a.size
    return out


def kernel(x, p, w_in, w_sb_out, w_ca_out, w_mix_out, rel_bias, g_mix, g_ffn, g_ple, g_final, w_ffn_in, w_ffn_out, w_ple_in, w_ple_gate, loss_target, m_w_in, m_w_sb_out, m_w_ca_out, m_w_mix_out, m_rel_bias, m_g_mix, m_g_ffn, m_g_ple, m_g_final, m_w_ffn_in, m_w_ffn_out, m_w_ple_in, m_w_ple_gate, v_w_in, v_w_sb_out, v_w_ca_out, v_w_mix_out, v_rel_bias, v_g_mix, v_g_ffn, v_g_ple, v_g_final, v_w_ffn_in, v_w_ffn_out, v_w_ple_in, v_w_ple_gate):
    weights = dict(w_in=w_in, w_sb_out=w_sb_out, w_ca_out=w_ca_out, w_mix_out=w_mix_out, rel_bias=rel_bias,
                   g_mix=g_mix, g_ffn=g_ffn, g_ple=g_ple, g_final=g_final, w_ffn_in=w_ffn_in,
                   w_ffn_out=w_ffn_out, w_ple_in=w_ple_in, w_ple_gate=w_ple_gate)
    m_in = dict(w_in=m_w_in, w_sb_out=m_w_sb_out, w_ca_out=m_w_ca_out, w_mix_out=m_w_mix_out, rel_bias=m_rel_bias,
                g_mix=m_g_mix, g_ffn=m_g_ffn, g_ple=m_g_ple, g_final=m_g_final, w_ffn_in=m_w_ffn_in,
                w_ffn_out=m_w_ffn_out, w_ple_in=m_w_ple_in, w_ple_gate=m_w_ple_gate)
    v_in = dict(w_in=v_w_in, w_sb_out=v_w_sb_out, w_ca_out=v_w_ca_out, w_mix_out=v_w_mix_out, rel_bias=v_rel_bias,
                g_mix=v_g_mix, g_ffn=v_g_ffn, g_ple=v_g_ple, g_final=v_g_final, w_ffn_in=v_w_ffn_in,
                w_ffn_out=v_w_ffn_out, w_ple_in=v_w_ple_in, w_ple_gate=v_w_ple_gate)

    pos = jnp.stack([2 * lax.axis_index("x") + lax.axis_index("y"), lax.axis_index("c")]).astype(jnp.int32)
    comm = _Comm(pos, {"w_in": _cast_place(w_in[0], AXIS["w_in"], pos, name="cast_w_in")})
    at = 0
    for n in HEAD_HOSTS:
        ride = comm.gather(("w_in",), "near", part=(at, HEAD_PARTS))
        comm.w[n] = _cast_place(weights[n][0], AXIS[n], pos, name="cast_" + n, ride=ride)
        at += 1
    for n, axis in BIG:
        if n not in comm.w:
            comm.w[n] = _cast_place(weights[n][0], axis, pos, name="cast_" + n)
    _run(comm.gather(("w_in",), "near", part=(at, HEAD_PARTS, HEAD_PARTS - at)), name="gather_w_in_near")
    _run(comm.gather(("w_in",), "far"), name="gather_w_in_far")
    _run(comm.gather(("w_in",), "pair"), name="gather_w_in_pair")
    small = dict(rel_bias=rel_bias[0], g_mix=g_mix, g_ffn=g_ffn, g_ple=g_ple, g_final=g_final.reshape(1, -1))
    loss, grad_x, gs = _step(x[0], p[0, 0], loss_target[0], small, comm)

    grads, delta, new_m, new_v = {}, {}, {}, {}
    for n in [n for n, _ in BIG if n != "w_in"] + ["w_in"]:
        ride = comm.tail(TAIL_HOSTS[n]) if n in TAIL_HOSTS else None
        if n == "w_in":
            _run(comm.tail_rest(), name="rs_chips_w_in")
            comm.sum(("w_in",))
            _run(comm.share(("w_in",)), name="rs_share_w_in")
        g = comm.result(n)
        if n in ON_SPARSECORE:
            d, nm, nv = _adamw_sc(weights[n][0], g, m_in[n][0], v_in[n][0], name="adamw_sc_" + n)
        else:
            d, nm, nv = _adamw(weights[n][0], g, m_in[n][0], v_in[n][0], name="adamw_" + n, ride=ride)
        grads[n], delta[n], new_m[n], new_v[n] = g[None], d[None], nm[None], nv[None]

    like = [weights[n] for n in SMALL]
    reduced = _small_all_reduce(_pack([gs[n] for n in SMALL] + [loss[:, :1]]), name="small_all_reduce")
    g_small = _unpack(reduced, like + [loss[:, :1]])
    total_loss = g_small[-1].reshape(())
    g_packed = _pack(g_small[:-1])
    d_s, m_s, v_s = _adamw(_pack(like), g_packed, _pack([m_in[n] for n in SMALL]), _pack([v_in[n] for n in SMALL]),
                           name="adamw_small")
    for n, g, d, nm, nv in zip(SMALL, g_small[:-1], _unpack(d_s, like), _unpack(m_s, like), _unpack(v_s, like)):
        grads[n], delta[n], new_m[n], new_v[n] = g, d, nm, nv

    return (total_loss, grad_x[None], *[grads[n] for n in ORDER], *[delta[n] for n in ORDER],
            *[new_m[n] for n in ORDER], *[new_v[n] for n in ORDER])
```

```python
import functools
import math

import jax
import jax.numpy as jnp
import numpy as np
from jax import lax
from jax.experimental import pallas as pl
from jax.experimental.pallas import tpu as pltpu
from jax.experimental.pallas import tpu_sc as plsc

F32 = jnp.float32
BF16 = jnp.bfloat16

HEAD_DIM = 128
CHUNK = 64
LEFT_CHUNKS = 8
REL_CLIP = 128
N_REL = REL_CLIP + CHUNK
BAND = (LEFT_CHUNKS + 2) * CHUNK
CA_PER_STEP = 4
CA_ROWS = CA_PER_STEP * CHUNK
CA_BAND = BAND + CA_PER_STEP * CHUNK
CA_PAD = BAND
SB_BLOCK = 128
SB_KEYS = 512
SB_GROUPS = SB_KEYS // SB_BLOCK
SB_ROWS = SB_KEYS
EPS = 1e-6
NEG = -1e30

ADAM_LR = 0.001
ADAM_B1 = 0.9
ADAM_B2 = 0.999
ADAM_EPS = 1e-08
ADAM_WD = 0.01
ADAM_STEP = 10

VMEM_LIMIT = 48 * 1024 * 1024
MM_VMEM_BUDGET = 36 * 1024 * 1024
V7X_HBM_BYTES_PER_S = 3.7e12
GRID_STEP_S = 0.35e-6
MESH = pl.DeviceIdType.MESH
N_CHIPS = 4


def _pick(dim, prefs):
    for t in prefs:
        if dim % t == 0:
            return t
    raise ValueError(f"no tile for {dim}")


def _cparams(sem=None):
    return pltpu.CompilerParams(dimension_semantics=sem, vmem_limit_bytes=VMEM_LIMIT)


def _sigmoid(v):
    return 1.0 / (1.0 + jnp.exp(-v))


def _dot(a, b, dims):
    return lax.dot_general(a, b, (dims, ((), ())), preferred_element_type=F32)


def _dot_nn(a, b):
    return _dot(a, b, ((1,), (0,)))


def _dot_nt(a, b):
    return _dot(a, b, ((1,), (1,)))


def _dot_tn(a, b):
    return _dot(a, b, ((0,), (0,)))


HBM = pl.BlockSpec(memory_space=pltpu.HBM)


class _Ride:
    def __init__(self):
        self.items = []

    def add(self, ins, outs, aliases, n_sems, start, finish, sink):
        self.items.append((ins, outs, aliases, n_sems, start, finish, sink))


def _call(body, args, *, name, grid, in_specs, out_specs, out_shape, scratch_shapes=(), sem=None, ride=None,
          scalars=None, onto=()):
    items = ride.items if ride is not None else []
    if onto:
        args, in_specs = list(args) + list(onto), list(in_specs) + [HBM] * len(onto)
        inner, body = body, lambda *refs: inner(*refs[:len(args) - len(onto)], *refs[len(args):])
    n_in, n_out, n_scr = len(args), len(out_shape), len(scratch_shapes)
    r_ins = [a for it in items for a in it[0]]
    r_outs = [o for it in items for o in it[1]]
    updated = [id(it[0][i]) for it in items for i in it[2]]
    assert len(set(updated)) == len(updated), "one call may update a buffer in place only once"
    aliases, a, b = {n_in - len(onto) + t: t for t in range(len(onto))}, n_in, n_out
    for it in items:
        aliases.update({a + i: b + o for i, o in it[2].items()})
        a, b = a + len(it[0]), b + len(it[1])
    sems = [pltpu.SemaphoreType.DMA((it[3],)) for it in items for _ in range(2)]

    def wrapped(*refs):
        head, refs = (refs[:1], refs[1:]) if scalars is not None else ((), refs)
        ins, rin = refs[:n_in], refs[n_in:n_in + len(r_ins)]
        at = n_in + len(r_ins)
        outs, rout = refs[at:at + n_out], refs[at + n_out:at + n_out + len(r_outs)]
        at += n_out + len(r_outs)
        scr, rsem = refs[at:at + n_scr], refs[at + n_scr:]

        def each(which):
            a = b = 0
            for q, it in enumerate(items):
                it[which](rin[a:a + len(it[0])], rout[b:b + len(it[1])], rsem[2 * q], rsem[2 * q + 1])
                a, b = a + len(it[0]), b + len(it[1])

        if items:
            ids = [pl.program_id(d) for d in range(len(grid))]
            first = functools.reduce(jnp.logical_and, [i == 0 for i in ids])
            last = functools.reduce(jnp.logical_and, [i == g - 1 for i, g in zip(ids, grid)])
            pl.when(first)(lambda: each(4))
        body(*head, *ins, *outs, *scr)
        if items:
            pl.when(last)(lambda: each(5))

    specs = dict(grid=grid, in_specs=list(in_specs) + [HBM] * len(r_ins),
                 out_specs=list(out_specs) + [HBM] * len(r_outs), scratch_shapes=list(scratch_shapes) + sems)
    if scalars is not None:
        specs = dict(grid_spec=pltpu.PrefetchScalarGridSpec(num_scalar_prefetch=1, **specs))
        aliases = {i + 1: o for i, o in aliases.items()}
    res = pl.pallas_call(
        wrapped, name=name, **specs,
        out_shape=list(out_shape) + r_outs,
        input_output_aliases=aliases,
        compiler_params=_cparams(("arbitrary",) * len(grid) if items else sem),
    )(*(() if scalars is None else (scalars,)), *args, *r_ins)
    b = n_out
    for it in items:
        it[6](res[b:b + len(it[1])])
        b += len(it[1])
    return list(res[:n_out])


def _mm_tiles(m, n_align, n, k, a_bytes, b_bytes, out_bytes):
    best = None
    tks = sorted({t for t in (k, k // 2, k // 4, 2048, 1024, 512, 256, 128) if t <= k and k % t == 0 and t % 128 == 0})
    for tm in (t for t in (2048, 1024, 512, 256, 128) if m % t == 0):
        for tn in (t for t in (2048, 1024, 512, 256, 128) if n_align % t == 0):
            for tk in tks:
                nk = k // tk
                vmem = 2 * (tm * tk * a_bytes + tk * tn * b_bytes + tm * tn * out_bytes) + tm * tn * 4
                if vmem > MM_VMEM_BUDGET:
                    continue
                traffic = m * k * a_bytes * (n // tn if nk > 1 else 1) + k * n * b_bytes * (m // tm)
                traffic += tm * tk * a_bytes + tk * tn * b_bytes + tm * tn * out_bytes
                traffic += m * n * 4 * nk if nk > 1 else 0
                cost = traffic / V7X_HBM_BYTES_PER_S + (m // tm) * (n // tn) * nk * GRID_STEP_S
                if best is None or cost < best[0]:
                    best = (cost, tm, tn, tk)
    return best[1:]


def _mm(a, b, mode, out_dtypes, *, name, n=None, b_col_off=0, resid=None, ride=None, rows=None, onto=(), m_half=None):
    if mode == "nn":
        m, k = a.shape
        n = b.shape[1] if n is None else n
    elif mode == "nt":
        m, k = a.shape
        n = b.shape[0]
    else:
        k, m = a.shape
        n = b.shape[1]
    if m_half is not None:
        m //= 2
    m_all, (row0, m) = m, (0, m) if rows is None else rows
    n_out = len(out_dtypes)
    has_resid = resid is not None
    out_bytes = sum(jnp.dtype(dt).itemsize for dt in out_dtypes) + (4 if has_resid else 0)
    tm, tn, tk = _mm_tiles(math.gcd(m, row0) if row0 else m, math.gcd(n, b_col_off) if b_col_off else n, n, k,
                           a.dtype.itemsize, b.dtype.itemsize, out_bytes)
    nk = k // tk
    boff, roff = b_col_off // tn, row0 // tm
    dot = {"nn": _dot_nn, "nt": _dot_nt, "tn": _dot_tn}[mode]
    if m_half is None:
        half = lambda: 0
    else:
        half = lambda pos_ref: (pos_ref[1] if m_half[0] else 1 - pos_ref[1]) * (m // tm)

    def body(*refs):
        refs = refs[m_half is not None:]
        a_ref, b_ref = refs[0], refs[1]
        r_ref = refs[2] if has_resid else None
        o_refs = refs[2 + has_resid: 2 + has_resid + n_out]

        def finish(r):
            if has_resid:
                r = r + r_ref[...]
            for o_ref in o_refs:
                o_ref[...] = r.astype(o_ref.dtype)

        part = dot(a_ref[...].astype(BF16), b_ref[...].astype(BF16))
        if nk == 1:
            finish(part)
            return
        acc_ref = refs[-1]
        kk = pl.program_id(2)

        @pl.when(kk == 0)
        def _():
            acc_ref[...] = part

        @pl.when(kk > 0)
        def _():
            acc_ref[...] += part

        @pl.when(kk == nk - 1)
        def _():
            finish(acc_ref[...])

    if mode == "nn":
        a_spec = pl.BlockSpec((tm, tk), lambda i, j, kk, *_: (i + roff, kk))
        b_spec = pl.BlockSpec((tk, tn), lambda i, j, kk, *_: (kk, j + boff))
    elif mode == "nt":
        a_spec = pl.BlockSpec((tm, tk), lambda i, j, kk, *_: (i + roff, kk))
        b_spec = pl.BlockSpec((tn, tk), lambda i, j, kk, *_: (j, kk))
    else:
        a_spec = pl.BlockSpec((tk, tm), lambda i, j, kk, *pos: (kk, i + half(*pos)))
        b_spec = pl.BlockSpec((tk, tn), lambda i, j, kk, *_: (kk, j))
    o_spec = pl.BlockSpec((tm, tn), lambda i, j, kk, *_: (i + roff, j))
    in_specs = [a_spec, b_spec] + ([o_spec] if has_resid else [])
    args = [a, b] + ([resid] if has_resid else [])
    outs = _call(
        body, args, name=name,
        grid=(m // tm, n // tn, nk),
        in_specs=in_specs,
        out_specs=[o_spec] * n_out,
        out_shape=[jax.ShapeDtypeStruct((m_all, n), dt) for dt in out_dtypes],
        scratch_shapes=[pltpu.VMEM((tm, tn), F32)] if nk > 1 else [],
        sem=("parallel", "parallel", "arbitrary"), ride=ride, onto=onto,
        scalars=None if m_half is None else m_half[1])
    return outs[0] if n_out == 1 else tuple(outs)


def _row_tile(s):
    return _pick(s, (256, 128))


def _rms_fwd(x, g, *, name, ride=None):
    s, d = x.shape
    tr = _row_tile(s)

    def body(x_ref, g_ref, o_ref):
        xv = x_ref[...]
        r = lax.rsqrt(jnp.mean(xv * xv, axis=1, keepdims=True) + EPS)
        o_ref[...] = (xv * r * g_ref[...]).astype(o_ref.dtype)

    return _call(
        body, [x, g], name=name, grid=(s // tr,),
        in_specs=[pl.BlockSpec((tr, d), lambda i: (i, 0)), pl.BlockSpec((1, d), lambda i: (0, 0))],
        out_specs=[pl.BlockSpec((tr, d), lambda i: (i, 0))],
        out_shape=[jax.ShapeDtypeStruct((s, d), BF16)], sem=("parallel",), ride=ride)[0]


def _rms_bwd(x, g, dh, dres, *, name, ride=None):
    s, d = x.shape
    tr = _row_tile(s)

    def body(x_ref, g_ref, dh_ref, dres_ref, dx_ref, dx16_ref, dg_ref):
        i = pl.program_id(0)
        xv = x_ref[...]
        r = lax.rsqrt(jnp.mean(xv * xv, axis=1, keepdims=True) + EPS)
        xhat = xv * r
        dhv = dh_ref[...]
        dxhat = dhv * g_ref[...]
        proj = jnp.mean(dxhat * xhat, axis=1, keepdims=True)
        dx = dres_ref[...] + r * (dxhat - xhat * proj)
        dx_ref[...] = dx
        dx16_ref[...] = dx.astype(dx16_ref.dtype)

        @pl.when(i == 0)
        def _():
            dg_ref[...] = jnp.zeros_like(dg_ref)

        dg_ref[...] += jnp.sum(dhv * xhat, axis=0, keepdims=True)

    row = pl.BlockSpec((tr, d), lambda i: (i, 0))
    vec = pl.BlockSpec((1, d), lambda i: (0, 0))
    return _call(
        body, [x, g, dh, dres], name=name, grid=(s // tr,),
        in_specs=[row, vec, row, row],
        out_specs=[row, row, vec],
        out_shape=[jax.ShapeDtypeStruct((s, d), F32), jax.ShapeDtypeStruct((s, d), BF16),
                   jax.ShapeDtypeStruct((1, d), F32)],
        sem=("arbitrary",), ride=ride)


def _final_loss(x, g, target, *, name):
    s, d = x.shape
    tr = _row_tile(s)

    def body(x_ref, g_ref, t_ref, dx_ref, dg_ref, loss_ref):
        i = pl.program_id(0)
        xv = x_ref[...]
        gv = g_ref[...]
        r = lax.rsqrt(jnp.mean(xv * xv, axis=1, keepdims=True) + EPS)
        xhat = xv * r
        err = xhat * gv - t_ref[...]
        dy = err * (1.0 / d)
        dxhat = dy * gv
        proj = jnp.mean(dxhat * xhat, axis=1, keepdims=True)
        dx_ref[...] = r * (dxhat - xhat * proj)

        @pl.when(i == 0)
        def _():
            dg_ref[...] = jnp.zeros_like(dg_ref)
            loss_ref[...] = jnp.zeros_like(loss_ref)

        dg_ref[...] += jnp.sum(dy * xhat, axis=0, keepdims=True)
        part = 0.5 * jnp.sum(jnp.mean(err * err, axis=1, keepdims=True), axis=0, keepdims=True)
        loss_ref[...] += jnp.broadcast_to(part, loss_ref.shape)

    row = pl.BlockSpec((tr, d), lambda i: (i, 0))
    vec = pl.BlockSpec((1, d), lambda i: (0, 0))
    return pl.pallas_call(
        body, name=name, grid=(s // tr,),
        in_specs=[row, vec, row],
        out_specs=[row, vec, pl.BlockSpec((1, 128), lambda i: (0, 0))],
        out_shape=[jax.ShapeDtypeStruct((s, d), F32), jax.ShapeDtypeStruct((1, d), F32),
                   jax.ShapeDtypeStruct((1, 128), F32)],
        compiler_params=_cparams(("arbitrary",)),
    )(x, g, target)


def _ew(body, ins, in_blocks, outs, out_blocks, grid, *, name, ride=None):
    return _call(body, ins, name=name, grid=grid,
                 in_specs=[pl.BlockSpec(bs, im) for bs, im in in_blocks],
                 out_specs=[pl.BlockSpec(bs, im) for bs, im in out_blocks],
                 out_shape=outs, sem=("parallel",) * len(grid), ride=ride)


def _gate_merge_fwd(gates, o_sb, o_ca, *, name, ride=None):
    s, d = o_sb.shape
    tr, tc = _row_tile(s), _pick(d, (1024, 512, 256, 128))
    nc = d // tc

    def body(gs_ref, gc_ref, os_ref, oc_ref, m_ref):
        m = _sigmoid(gs_ref[...]) * os_ref[...] + _sigmoid(gc_ref[...]) * oc_ref[...]
        m_ref[...] = m.astype(m_ref.dtype)

    blk = ((tr, tc), lambda i, j: (i, j))
    return _ew(body, [gates, gates, o_sb, o_ca],
               [blk, ((tr, tc), lambda i, j: (i, j + nc)), blk, blk],
               [jax.ShapeDtypeStruct((s, d), BF16)], [blk], (s // tr, nc), name=name, ride=ride)[0]


def _gate_merge_bwd(dmerged, gates, o_sb, o_ca, *, name):
    s, d = o_sb.shape
    tr, tc = _row_tile(s), _pick(d, (1024, 512, 256, 128))
    nc = d // tc

    def body(dm_ref, gs_ref, gc_ref, os_ref, oc_ref, dgs_ref, dgc_ref, dos_ref, doc_ref):
        dm = dm_ref[...]
        ss = _sigmoid(gs_ref[...])
        sc = _sigmoid(gc_ref[...])
        dgs_ref[...] = (dm * os_ref[...] * ss * (1.0 - ss)).astype(dgs_ref.dtype)
        dgc_ref[...] = (dm * oc_ref[...] * sc * (1.0 - sc)).astype(dgc_ref.dtype)
        dos_ref[...] = (dm * ss).astype(dos_ref.dtype)
        doc_ref[...] = (dm * sc).astype(doc_ref.dtype)

    blk = ((tr, tc), lambda i, j: (i, j))
    sd = jax.ShapeDtypeStruct((s, d), BF16)
    return _ew(body, [dmerged, gates, gates, o_sb, o_ca],
               [blk, blk, ((tr, tc), lambda i, j: (i, j + nc)), blk, blk],
               [sd, sd, sd, sd], [blk, blk, blk, blk], (s // tr, nc), name=name)


def _swiglu_fwd(gu, *, name, ride=None):
    s, f2 = gu.shape
    f = f2 // 2
    tr, tc = 128, _pick(f, (512, 256, 128))

    def body(gu_ref, a_ref):
        for at in range(0, f, tc):
            gv = gu_ref[:, at:at + tc].astype(F32)
            a_ref[:, at:at + tc] = (gv * _sigmoid(gv) * gu_ref[:, f + at:f + at + tc].astype(F32)).astype(a_ref.dtype)

    row = lambda i: (i, 0)
    return _ew(body, [gu], [((tr, f2), row)], [jax.ShapeDtypeStruct((s, f), BF16)], [((tr, f), row)],
               (s // tr,), name=name, ride=ride)[0]


def _swiglu_bwd(dact, gu, *, name):
    s, f2 = gu.shape
    f = f2 // 2
    tr, tc = 128, _pick(f, (512, 256, 128))

    def body(da_ref, gu_ref, o_ref):
        for at in range(0, f, tc):
            da = da_ref[:, at:at + tc]
            gv = gu_ref[:, at:at + tc].astype(F32)
            sg = _sigmoid(gv)
            uv = gu_ref[:, f + at:f + at + tc].astype(F32)
            o_ref[:, at:at + tc] = (da * uv * sg * (1.0 + gv * (1.0 - sg))).astype(o_ref.dtype)
            o_ref[:, f + at:f + at + tc] = (da * gv * sg).astype(o_ref.dtype)

    row = lambda i: (i, 0)
    return _ew(body, [dact, gu], [((tr, f), row), ((tr, f2), row)], [jax.ShapeDtypeStruct((s, f2), BF16)],
               [((tr, f2), row)], (s // tr,), name=name)[0]


def _concat_cols(parts, *, name):
    s = parts[0].shape[0]
    widths = [p.shape[1] for p in parts]
    tr = 256

    def body(*refs):
        o_ref, at = refs[-1], 0
        for p_ref, width in zip(refs, widths):
            o_ref[:, at:at + width] = p_ref[...]
            at += width

    row = lambda i: (i, 0)
    return _ew(body, list(parts), [((tr, width), row) for width in widths],
               [jax.ShapeDtypeStruct((s, sum(widths)), parts[0].dtype)], [((tr, sum(widths)), row)],
               (s // tr,), name=name)[0]


def _ple_fwd(x, t, pe, *, name):
    s, d = x.shape
    tr, tc = _row_tile(s), _pick(d, (1024, 512, 256, 128))

    def body(x_ref, t_ref, p_ref, o_ref):
        o_ref[...] = x_ref[...] + _sigmoid(t_ref[...]) * p_ref[...]

    blk = ((tr, tc), lambda i, j: (i, j))
    return _ew(body, [x, t, pe], [blk, blk, blk],
               [jax.ShapeDtypeStruct((s, d), F32)], [blk], (s // tr, d // tc), name=name)[0]


def _ple_bwd(dx, t, pe, *, name):
    s, d = dx.shape
    tr, tc = _row_tile(s), _pick(d, (1024, 512, 256, 128))

    def body(dx_ref, t_ref, p_ref, dt_ref, dp_ref):
        dxv = dx_ref[...]
        sg = _sigmoid(t_ref[...])
        dt_ref[...] = (dxv * p_ref[...] * sg * (1.0 - sg)).astype(dt_ref.dtype)
        dp_ref[...] = (dxv * sg).astype(dp_ref.dtype)

    blk = ((tr, tc), lambda i, j: (i, j))
    sd = jax.ShapeDtypeStruct((s, d), BF16)
    return _ew(body, [dx, t, pe], [blk, blk, blk], [sd, sd], [blk, blk], (s // tr, d // tc), name=name)


def _sb_tri(later):
    row = lax.broadcasted_iota(jnp.int32, (SB_BLOCK, SB_BLOCK), 0)
    col = lax.broadcasted_iota(jnp.int32, (SB_BLOCK, SB_BLOCK), 1)
    tri = (row > col) if later else (row < col)
    return jnp.concatenate([tri.astype(BF16), jnp.ones((SB_BLOCK, SB_BLOCK), BF16)], axis=1)


def _sb_valid(i, j, own):
    if not own:
        return None
    qi = i * SB_ROWS + lax.broadcasted_iota(jnp.int32, (SB_ROWS, SB_KEYS), 0)
    ki = j * SB_KEYS + lax.broadcasted_iota(jnp.int32, (SB_ROWS, SB_KEYS), 1)
    return ki < qi


def _sb_scan(v, tri, run, later):
    hi = v.astype(BF16)
    lo = (v - hi.astype(F32)).astype(BF16)
    outs = [None] * SB_GROUPS
    for b in (reversed(range(SB_GROUPS)) if later else range(SB_GROUPS)):
        cols = slice(b * SB_BLOCK, (b + 1) * SB_BLOCK)
        r = _dot_nn(hi[:, cols], tri) + _dot_nn(lo[:, cols], tri)
        outs[b] = r[:, :SB_BLOCK] + run
        run = run + r[:, SB_BLOCK:]
    return jnp.concatenate(outs, axis=1), run


def _masked(valid, v):
    return v if valid is None else jnp.where(valid, v, 0.0)


def _sb_scores(q, kj, scale, valid):
    z = _dot_nt(q, kj) * scale
    t = jnp.log(1.0 + jnp.exp(-jnp.abs(z)))
    return jnp.minimum(z, 0.0) - t, _masked(valid, -jnp.maximum(z, 0.0) - t)


def _sb_specs(h_count, s, col0):
    q_spec = pl.BlockSpec((SB_ROWS, HEAD_DIM), lambda h, i: (i, col0 + h))
    k_spec = pl.BlockSpec((s, HEAD_DIM), lambda h, i: (0, col0 + h_count + h))
    v_spec = pl.BlockSpec((s, HEAD_DIM), lambda h, i: (0, col0 + 2 * h_count + h))
    return q_spec, k_spec, v_spec


def _sb_fwd(qkv, n_heads, col0, *, name, ride=None):
    s = qkv.shape[0]
    nq = s // SB_ROWS
    scale = HEAD_DIM ** -0.5

    def body(q_ref, k_ref, v_ref, o_ref):
        i = pl.program_id(1)
        q = q_ref[...]
        tri = _sb_tri(later=True)

        def step(j, carry, own):
            run, acc = carry
            off = pl.multiple_of(j * SB_KEYS, SB_KEYS)
            valid = _sb_valid(i, j, own)
            ls, lk = _sb_scores(q, k_ref[pl.ds(off, SB_KEYS), :], scale, valid)
            between, run = _sb_scan(lk, tri, run, later=True)
            a = _masked(valid, jnp.exp(ls + between))
            return run, acc + _dot_nn(a.astype(BF16), v_ref[pl.ds(off, SB_KEYS), :])

        carry = step(i, (jnp.zeros((SB_ROWS, SB_BLOCK), F32), jnp.zeros((SB_ROWS, HEAD_DIM), F32)), True)
        _, acc = lax.fori_loop(0, i, lambda jj, c: step(i - 1 - jj, c, False), carry)
        o_ref[...] = acc.astype(o_ref.dtype)

    q_spec, k_spec, v_spec = _sb_specs(n_heads, s, col0)
    return _call(
        body, [qkv, qkv, qkv], name=name, grid=(n_heads, nq),
        in_specs=[q_spec, k_spec, v_spec],
        out_specs=[pl.BlockSpec((SB_ROWS, HEAD_DIM), lambda h, i: (i, h))],
        out_shape=[jax.ShapeDtypeStruct((s, n_heads * HEAD_DIM), BF16)],
        sem=("parallel", "arbitrary"), ride=ride)[0]


def _sb_bwd(qkv, dy, n_heads, col0, *, name, ride=None):
    s = qkv.shape[0]
    nq = s // SB_ROWS
    scale = HEAD_DIM ** -0.5

    def body(q_ref, k_ref, v_ref, dy_ref, dq_ref, dk_ref, dv_ref, e_scr, sg_scr, dk_acc, dv_acc):
        i = pl.program_id(1)
        q = q_ref[...]
        dyv = dy_ref[...]

        @pl.when(i == 0)
        def _():
            dk_acc[...] = jnp.zeros_like(dk_acc)
            dv_acc[...] = jnp.zeros_like(dv_acc)

        tri_later = _sb_tri(later=True)

        def pass1(j, run, own):
            off = pl.multiple_of(j * SB_KEYS, SB_KEYS)
            valid = _sb_valid(i, j, own)
            ls, lk = _sb_scores(q, k_ref[pl.ds(off, SB_KEYS), :], scale, valid)
            between, run = _sb_scan(lk, tri_later, run, later=True)
            a = _masked(valid, jnp.exp(ls + between))
            e_scr[j] = a * _dot_nt(dyv, v_ref[pl.ds(off, SB_KEYS), :])
            sg_scr[j] = jnp.exp(ls)
            dv_acc[pl.ds(off, SB_KEYS), :] += _dot_tn(a.astype(BF16), dyv)
            return run

        lax.fori_loop(0, i, lambda jj, run: pass1(i - 1 - jj, run, False),
                      pass1(i, jnp.zeros((SB_ROWS, SB_BLOCK), F32), True))

        tri_earlier = _sb_tri(later=False)

        def pass2(j, carry, own):
            run, dq = carry
            off = pl.multiple_of(j * SB_KEYS, SB_KEYS)
            kj = k_ref[pl.ds(off, SB_KEYS), :]
            sg = sg_scr[j]
            e = e_scr[j]
            before, run = _sb_scan(e, tri_earlier, run, later=False)
            dz = _masked(_sb_valid(i, j, own), e * (1.0 - sg) - sg * before) * scale
            dzb = dz.astype(BF16)
            dk_acc[pl.ds(off, SB_KEYS), :] += _dot_tn(dzb, q)
            return run, dq + _dot_nn(dzb, kj)

        init = (jnp.zeros((SB_ROWS, SB_BLOCK), F32), jnp.zeros((SB_ROWS, HEAD_DIM), F32))
        _, dq = pass2(i, lax.fori_loop(0, i, lambda j, c: pass2(j, c, False), init), True)
        dq_ref[...] = dq.astype(dq_ref.dtype)

        @pl.when(i == nq - 1)
        def _():
            dk_ref[...] = dk_acc[...].astype(dk_ref.dtype)
            dv_ref[...] = dv_acc[...].astype(dv_ref.dtype)

    q_spec, k_spec, v_spec = _sb_specs(n_heads, s, col0)
    blk = pl.BlockSpec((SB_ROWS, HEAD_DIM), lambda h, i: (i, h))
    full = pl.BlockSpec((s, HEAD_DIM), lambda h, i: (0, h))
    sd = jax.ShapeDtypeStruct((s, n_heads * HEAD_DIM), BF16)
    return _call(
        body, [qkv, qkv, qkv, dy], name=name, grid=(n_heads, nq),
        in_specs=[q_spec, k_spec, v_spec, blk],
        out_specs=[blk, full, full],
        out_shape=[sd, sd, sd],
        scratch_shapes=[pltpu.VMEM((s // SB_KEYS, SB_ROWS, SB_KEYS), F32), pltpu.VMEM((s // SB_KEYS, SB_ROWS, SB_KEYS), F32),
                        pltpu.VMEM((s, HEAD_DIM), F32), pltpu.VMEM((s, HEAD_DIM), F32)],
        sem=("parallel", "arbitrary"), ride=ride)


def _band_bias(rel_bias):
    h = rel_bias.shape[0]
    width = BAND + CHUNK
    first = width - 1 - N_REL
    line = jnp.concatenate([jnp.broadcast_to(rel_bias[:, :1], (h, first)), rel_bias], axis=1)
    tiled = jnp.broadcast_to(line[:, None, :], (h, CHUNK, width - 1)).reshape(h, CHUNK * (width - 1))
    skew = jnp.pad(tiled, ((0, 0), (0, CHUNK))).reshape(h, CHUNK, width)[:, ::-1, :BAND]
    seen = jnp.arange(BAND) >= CHUNK
    return jnp.where(seen[None, None, :], skew, NEG)


def _band_bias_grad(dbias):
    h = dbias.shape[0]
    width = BAND + CHUNK
    flipped = jnp.pad(dbias[:, ::-1, :], ((0, 0), (0, 0), (0, CHUNK)))
    skew = flipped.reshape(h, CHUNK * width)[:, :CHUNK * (width - 1)].reshape(h, CHUNK, width - 1)
    diag = jnp.sum(skew, axis=1)
    first = width - 1 - N_REL
    clipped = jnp.sum(diag[:, :first + 1], axis=1, keepdims=True)
    return jnp.concatenate([clipped, diag[:, first + 1:]], axis=1)


def _group_bias(band):
    return jnp.concatenate([jnp.pad(band, ((0, 0), (0, 0), ((u + 1) * CHUNK, (CA_PER_STEP - 1 - u) * CHUNK)),
                                    constant_values=NEG) for u in range(CA_PER_STEP)], axis=1)


def _group_bias_grad(dgroup):
    return sum(dgroup[:, u * CHUNK:(u + 1) * CHUNK, (u + 1) * CHUNK:(u + 1) * CHUNK + BAND] for u in range(CA_PER_STEP))


def _ca_load_padded(k_ref, v_ref, kp, vp, s):
    kp[pl.ds(0, CA_PAD), :] = jnp.zeros((CA_PAD, HEAD_DIM), kp.dtype)
    vp[pl.ds(0, CA_PAD), :] = jnp.zeros((CA_PAD, HEAD_DIM), vp.dtype)
    kp[pl.ds(CA_PAD, s), :] = k_ref[...]
    vp[pl.ds(CA_PAD, s), :] = v_ref[...]


def _ca_weights(q, kb, bias, off, scale):
    z = _dot_nt(q, kb) * scale + bias
    pos = off + lax.broadcasted_iota(jnp.int32, (CA_ROWS, CA_BAND), 1)
    z = jnp.where(pos >= CA_PAD, z, NEG)
    p = jnp.exp(z - jnp.max(z, axis=1, keepdims=True))
    return p / jnp.sum(p, axis=1, keepdims=True)


def _ca_specs(h_count, s, col0):
    q_spec = pl.BlockSpec((CA_ROWS, HEAD_DIM), lambda h, c: (c, col0 + h))
    k_spec = pl.BlockSpec((s, HEAD_DIM), lambda h, c: (0, col0 + h_count + h))
    v_spec = pl.BlockSpec((s, HEAD_DIM), lambda h, c: (0, col0 + 2 * h_count + h))
    b_spec = pl.BlockSpec((1, CA_ROWS, CA_BAND), lambda h, c: (h, 0, 0))
    return q_spec, k_spec, v_spec, b_spec


def _ca_fwd(qkv, bias, n_heads, col0, *, name, ride=None):
    s = qkv.shape[0]
    nc = s // CA_ROWS
    scale = HEAD_DIM ** -0.5

    def body(q_ref, k_ref, v_ref, b_ref, o_ref, kp, vp):
        c = pl.program_id(1)

        @pl.when(c == 0)
        def _():
            _ca_load_padded(k_ref, v_ref, kp, vp, s)

        off = pl.multiple_of(c * CA_ROWS, CA_ROWS)
        w = _ca_weights(q_ref[...], kp[pl.ds(off, CA_BAND), :], b_ref[0], off, scale)
        o_ref[...] = _dot_nn(w.astype(BF16), vp[pl.ds(off, CA_BAND), :]).astype(o_ref.dtype)

    q_spec, k_spec, v_spec, b_spec = _ca_specs(n_heads, s, col0)
    return _call(
        body, [qkv, qkv, qkv, bias], name=name, grid=(n_heads, nc),
        in_specs=[q_spec, k_spec, v_spec, b_spec],
        out_specs=[pl.BlockSpec((CA_ROWS, HEAD_DIM), lambda h, c: (c, h))],
        out_shape=[jax.ShapeDtypeStruct((s, n_heads * HEAD_DIM), BF16)],
        scratch_shapes=[pltpu.VMEM((s + CA_PAD, HEAD_DIM), BF16), pltpu.VMEM((s + CA_PAD, HEAD_DIM), BF16)],
        sem=("parallel", "arbitrary"), ride=ride)[0]


def _ca_bwd(qkv, bias, dy, n_heads, col0, *, name, ride=None):
    s = qkv.shape[0]
    nc = s // CA_ROWS
    scale = HEAD_DIM ** -0.5

    def body(q_ref, k_ref, v_ref, b_ref, dy_ref, dq_ref, dk_ref, dv_ref, db_ref, kp, vp, dkp, dvp):
        c = pl.program_id(1)

        @pl.when(c == 0)
        def _():
            _ca_load_padded(k_ref, v_ref, kp, vp, s)
            dkp[...] = jnp.zeros_like(dkp)
            dvp[...] = jnp.zeros_like(dvp)
            db_ref[...] = jnp.zeros_like(db_ref)

        off = pl.multiple_of(c * CA_ROWS, CA_ROWS)
        band = pl.ds(off, CA_BAND)
        q = q_ref[...]
        dyv = dy_ref[...]
        kb = kp[band, :]
        w = _ca_weights(q, kb, b_ref[0], off, scale)
        dw = _dot_nt(dyv, vp[band, :])
        dvp[band, :] += _dot_tn(w.astype(BF16), dyv)
        dz = w * (dw - jnp.sum(w * dw, axis=1, keepdims=True))
        db_ref[0] += dz
        dzs = (dz * scale).astype(BF16)
        dq_ref[...] = _dot_nn(dzs, kb).astype(dq_ref.dtype)
        dkp[band, :] += _dot_tn(dzs, q)

        @pl.when(c == nc - 1)
        def _():
            dk_ref[...] = dkp[pl.ds(CA_PAD, s), :].astype(dk_ref.dtype)
            dv_ref[...] = dvp[pl.ds(CA_PAD, s), :].astype(dv_ref.dtype)

    q_spec, k_spec, v_spec, b_spec = _ca_specs(n_heads, s, col0)
    blk = pl.BlockSpec((CA_ROWS, HEAD_DIM), lambda h, c: (c, h))
    full = pl.BlockSpec((s, HEAD_DIM), lambda h, c: (0, h))
    sd = jax.ShapeDtypeStruct((s, n_heads * HEAD_DIM), BF16)
    return _call(
        body, [qkv, qkv, qkv, bias, dy], name=name, grid=(n_heads, nc),
        in_specs=[q_spec, k_spec, v_spec, b_spec, blk],
        out_specs=[blk, full, full, b_spec],
        out_shape=[sd, sd, sd, jax.ShapeDtypeStruct((n_heads, CA_ROWS, CA_BAND), F32)],
        scratch_shapes=[pltpu.VMEM((s + CA_PAD, HEAD_DIM), BF16), pltpu.VMEM((s + CA_PAD, HEAD_DIM), BF16),
                        pltpu.VMEM((s + CA_PAD, HEAD_DIM), F32), pltpu.VMEM((s + CA_PAD, HEAD_DIM), F32)],
        sem=("parallel", "arbitrary"), ride=ride)


EARLY = ("w_sb_out", "w_ca_out", "w_mix_out")


def _step(x, p, target, small, comm):
    w = comm.w
    d = x.shape[1]
    n_sb = w["w_sb_out"].shape[0] // HEAD_DIM
    n_ca = w["w_ca_out"].shape[0] // HEAD_DIM
    qkv_cols = 3 * HEAD_DIM * (n_sb + n_ca)
    ca_col0 = 3 * n_sb
    both = (F32, BF16)

    h1 = _rms_fwd(x, small["g_mix"], name="rms_mix")
    ffn, ple = ("w_ffn_in",), ("w_ple_gate", "w_ple_in")
    qkv = _mm(h1, w["w_in"], "nn", (BF16,), name="proj_qkv", n=qkv_cols, ride=comm.gather(EARLY, "near"))
    gates = _mm(h1, w["w_in"], "nn", (F32,), name="proj_gates", n=2 * d, b_col_off=qkv_cols,
                ride=comm.gather(ffn, "near", comm.gather(EARLY, "far"), (0, 8)))
    bias = _group_bias(_band_bias(small["rel_bias"]))
    y_sb = _sb_fwd(qkv, n_sb, 0, name="sb_fwd", ride=comm.gather(ffn, "near", comm.gather(EARLY, "pair"), (1, 8, 7)))
    y_ca = _ca_fwd(qkv, bias, n_ca, ca_col0, name="ca_fwd", ride=comm.gather(ffn, "far"))
    out = ("w_ffn_out",)
    o_sb = _mm(y_sb, w["w_sb_out"], "nn", (F32,), name="sb_out", ride=comm.gather(out, "near", part=(0, 4)))
    o_ca = _mm(y_ca, w["w_ca_out"], "nn", (F32,), name="ca_out", ride=comm.gather(out, "near", part=(1, 4)))
    merged = _gate_merge_fwd(gates, o_sb, o_ca, name="gate_merge",
                             ride=comm.gather(out, "near", comm.gather(ffn, "pair"), (2, 4)))
    x1 = _mm(merged, w["w_mix_out"], "nn", (F32,), name="mix_out", resid=x, ride=comm.gather(out, "near", part=(3, 4)))
    h2 = _rms_fwd(x1, small["g_ffn"], name="rms_ffn")
    gu = _mm(h2, w["w_ffn_in"], "nn", (BF16,), name="ffn_in", ride=comm.gather(ple, "near", comm.gather(out, "far")))
    act = _swiglu_fwd(gu, name="swiglu", ride=comm.gather(ple, "far", comm.gather(out, "pair")))
    x2 = _mm(act, w["w_ffn_out"], "nn", (F32,), name="ffn_out", resid=x1, ride=comm.gather(ple, "pair"))
    h3 = _rms_fwd(x2, small["g_ple"], name="rms_ple")
    t = _mm(h3, w["w_ple_gate"], "nn", (F32,), name="ple_gate")
    pe = _mm(p, w["w_ple_in"], "nn", (F32,), name="ple_in")
    x3 = _ple_fwd(x2, t, pe, name="ple_add")

    gs = {}
    dx3, gs["g_final"], loss = _final_loss(x3, small["g_final"], target, name="final_loss")
    dt, dpe = _ple_bwd(dx3, t, pe, name="ple_bwd")
    comm.grad("w_ple_in", *_mm(p, dpe, "tn", both, name="dw_ple_in"))
    comm.grad("w_ple_gate", *_mm(h3, dt, "tn", both, name="dw_ple_gate"))
    ple = ("w_ple_in", "w_ple_gate")
    dh3 = _mm(dt, w["w_ple_gate"], "nt", (F32,), name="dh_ple", ride=comm.pair(ple))
    dx2, dx2_16, gs["g_ple"] = _rms_bwd(x2, small["g_ple"], dh3, dx3, name="rms_ple_bwd")
    comm.add(ple)
    comm.grad("w_ffn_out", *_mm(act, dx2_16, "tn", both, name="dw_ffn_out", ride=comm.chips(ple)))
    dact = _mm(dx2_16, w["w_ffn_out"], "nt", (F32,), name="dact", ride=comm.pair(("w_ffn_out",)))
    dgu = _swiglu_bwd(dact, gu, name="swiglu_bwd")
    comm.sum(ple)
    comm.add(("w_ffn_out",))
    comm.grad("w_ffn_in", *_mm(h2, dgu, "tn", both, name="dw_ffn_in",
                               ride=comm.share(ple, comm.chips(("w_ffn_out",)))))
    dh2 = _mm(dgu, w["w_ffn_in"], "nt", (F32,), name="dh_ffn", ride=comm.pair(("w_ffn_in",)))
    dx1, dx1_16, gs["g_ffn"] = _rms_bwd(x1, small["g_ffn"], dh2, dx2, name="rms_ffn_bwd")
    comm.add(("w_ffn_in",))
    comm.sum(("w_ffn_out",))
    comm.grad("w_mix_out", *_mm(merged, dx1_16, "tn", both, name="dw_mix_out", ride=comm.share(("w_ffn_out",))))
    dmerged = _mm(dx1_16, w["w_mix_out"], "nt", (F32,), name="dmerged", ride=comm.pair(("w_mix_out",)))
    dg_sb, dg_ca, do_sb, do_ca = _gate_merge_bwd(dmerged, gates, o_sb, o_ca, name="gate_merge_bwd")
    comm.add(("w_mix_out",))
    comm.grad("w_sb_out", *_mm(y_sb, do_sb, "tn", both, name="dw_sb_out"))
    comm.grad("w_ca_out", *_mm(y_ca, do_ca, "tn", both, name="dw_ca_out"))
    outs = ("w_sb_out", "w_ca_out")
    dy_sb = _mm(do_sb, w["w_sb_out"], "nt", (BF16,), name="dy_sb", ride=comm.pair(outs))
    dy_ca = _mm(do_ca, w["w_ca_out"], "nt", (BF16,), name="dy_ca")
    comm.add(outs)
    dq_sb, dk_sb, dv_sb = _sb_bwd(qkv, dy_sb, n_sb, 0, name="sb_bwd", ride=comm.chips(("w_ffn_in",)))
    comm.sum(("w_ffn_in",))
    late = ("w_mix_out",) + outs
    dq_ca, dk_ca, dv_ca, dbias = _ca_bwd(qkv, bias, dy_ca, n_ca, ca_col0, name="ca_bwd",
                                         ride=comm.chips(late, comm.share(("w_ffn_in",))))
    comm.sum(late)
    gs["rel_bias"] = _band_bias_grad(_group_bias_grad(dbias))
    dproj = _concat_cols([dq_sb, dk_sb, dv_sb, dq_ca, dk_ca, dv_ca, dg_sb, dg_ca], name="dproj")
    if comm.pos is None:
        comm.grad("w_in", *_mm(h1, dproj, "tn", both, name="dw_in"))
    else:
        g16 = _mm(h1, dproj, "tn", (BF16,), name="dw_in_other", m_half=(False, comm.pos), ride=comm.share(late))
        comm.grad("w_in", None, g16, half=True)
        g32 = _mm(h1, dproj, "tn", (F32,), name="dw_in_own", m_half=(True, comm.pos), ride=comm.pair(("w_in",)))
        comm.grad("w_in", g32, g16, half=True)
    comm.add(("w_in",))
    half = x.shape[0] // 2
    dh1 = _mm(dproj, w["w_in"], "nt", (F32,), name="dh_mix_top", rows=(0, half), ride=comm.tail(TAIL_SECOND))
    dh1 = _mm(dproj, w["w_in"], "nt", (F32,), name="dh_mix_bottom", rows=(half, half), onto=(dh1,),
              ride=comm.tail(TAIL_SECOND))
    grad_x, _, gs["g_mix"] = _rms_bwd(x, small["g_mix"], dh1, dx1, name="rms_mix_bwd", ride=comm.tail(TAIL_FIRST))
    return loss, grad_x, gs


def _position():
    x, y, c = lax.axis_index("x"), lax.axis_index("y"), lax.axis_index("c")
    chips = [(1 - x, y), (x, 1 - y), (1 - x, 1 - y)]
    return x, y, c, chips


def _aligned(v, m):
    return v if isinstance(v, int) else pl.multiple_of(v, m)


def _piece_dims(shape, axis):
    k, n = shape
    return (k // 2, n // N_CHIPS) if axis == 1 else (k // N_CHIPS // 2, n)


def _piece(ref, shape, axis, j, h, part=(0, 1)):
    pr, pc = _piece_dims(shape, axis)
    nr = pr // part[1] * (part[2] if len(part) > 2 else 1)
    r0 = part[0] * (pr // part[1])
    if axis == 1:
        return ref.at[pl.ds(_aligned(h * pr + r0, 16), nr), pl.ds(_aligned(j * pc, 128), pc)]
    return ref.at[pl.ds(_aligned((2 * j + h) * pr + r0, 16), nr), :]


def _shard_half(ref, h):
    rows = ref.shape[0] // 2
    return ref.at[pl.ds(_aligned(h * rows, 16), rows), :]


def _remote(src, dst, send_sems, recv_sems, k, to):
    return pltpu.make_async_remote_copy(src_ref=src, dst_ref=dst, send_sem=send_sems.at[k],
                                        recv_sem=recv_sems.at[k], device_id=to, device_id_type=MESH)


def _prefetch_call(body, scalars, ins, in_specs, out_shape, out_specs, grid, *, name, ride=None):
    single = not isinstance(out_shape, (list, tuple))
    outs = _call(body, ins, name=name, grid=grid, in_specs=in_specs,
                 out_specs=[out_specs] if single else out_specs, out_shape=[out_shape] if single else out_shape,
                 sem=("parallel",) * len(grid), ride=ride, scalars=scalars)
    return outs[0] if single else outs


def _slab_tiles(pr, pc):
    tc = pc if pc <= 4096 else _pick(pc, (2048, 1024, 512, 256, 128))
    tr = next(t for t in (1024, 512, 256, 128, 64, 32, 16) if pr % t == 0 and t * tc <= 512 * 1024)
    return tr, tc


def _cast_place(w, axis, pos, *, name, ride=None):
    ks, ns = w.shape
    shape = (ks, ns * N_CHIPS) if axis == 1 else (ks * N_CHIPS, ns)
    tr, tc = _slab_tiles(ks, ns)
    nr, nc = ks // tr, ns // tc

    def body(pos_ref, w_ref, o_ref):
        o_ref[...] = w_ref[...].astype(o_ref.dtype)

    if axis == 1:
        out_map = lambda i, j, pos_ref: (i, pos_ref[0] * nc + j)
    else:
        out_map = lambda i, j, pos_ref: (pos_ref[0] * nr + i, j)
    return _prefetch_call(body, pos, [w], [pl.BlockSpec((tr, tc), lambda i, j, pos_ref: (i, j))],
                          jax.ShapeDtypeStruct(shape, BF16), pl.BlockSpec((tr, tc), out_map), (nr, nc), name=name, ride=ride)


def _run(ride, *, name):
    if ride is None:
        return

    def body(o_ref):
        o_ref[...] = jnp.zeros_like(o_ref)

    _call(body, [], name=name, grid=(1,), in_specs=[], out_specs=[pl.BlockSpec((8, 128), lambda i: (0, 0))],
          out_shape=[jax.ShapeDtypeStruct((8, 128), F32)], ride=ride)


def _ride_gather(ride, w, n, axis, stage, part=(0, 1)):
    shape = w[n].shape
    piece = functools.partial(_piece, shape=shape, axis=axis)
    span = part[2] if len(part) > 2 else 1
    halves = [(2 * part[0] + t * span, 2 * part[1], span) for t in range(2)]

    def copies(ins, outs, send_sems, recv_sems, arriving):
        x, y, c, chips = _position()
        me, (xn, yn, dn) = 2 * x + y, [2 * px + py for px, py in chips]
        if stage == "near":
            plan = [(me, c, part, (1 - x, y, c), xn, c, part), (me, c, part, (x, 1 - y, c), yn, c, part)]
        elif stage == "far":
            plan = [(yn, c, halves[1], (1 - x, y, c), dn, c, halves[1]), (xn, c, halves[0], (x, 1 - y, c), dn, c, halves[0])]
        else:
            plan = [(j, c, part, (x, y, 1 - c), j, 1 - c, part) for j in (xn, yn, dn)]
        out = []
        for k, (chip, h, rows, to, from_chip, from_h, from_rows) in enumerate(plan):
            if arriving:
                lands = piece(outs[0], j=from_chip, h=from_h, part=from_rows)
                out.append(_remote(lands, lands, send_sems, recv_sems, k, to))
            else:
                out.append(_remote(piece(ins[0], j=chip, h=h, part=rows), piece(outs[0], j=chip, h=h, part=rows),
                                   send_sems, recv_sems, k, to))
        return out

    def start(*refs):
        for cp in copies(*refs, arriving=False):
            cp.start()

    def finish(*refs):
        for cp in copies(*refs, arriving=True):
            cp.wait_recv()
        for cp in copies(*refs, arriving=False):
            cp.wait_send()

    ride.add([w[n]], [jax.ShapeDtypeStruct(shape, w[n].dtype)], {0: 0}, 3, start, finish,
             lambda outs: w.__setitem__(n, outs[0]))


def _ride_pair(ride, st, axis):
    shape = st["g16"].shape
    pr, pc = (shape[0], shape[1] // N_CHIPS) if st.get("half") else _piece_dims(shape, axis)

    def copies(ins, outs, send_sems, recv_sems):
        x, y, c, _ = _position()
        if st.get("half"):
            pieces = [ins[0].at[:, pl.ds(j * pc, pc)] for j in range(N_CHIPS)]
        else:
            pieces = [_piece(ins[0], shape, axis, j, 1 - c) for j in range(N_CHIPS)]
        return [_remote(pieces[j], outs[0].at[j], send_sems, recv_sems, j, (x, y, 1 - c)) for j in range(N_CHIPS)]

    def start(*refs):
        for cp in copies(*refs):
            cp.start()

    def finish(*refs):
        for cp in copies(*refs):
            cp.wait()

    ride.add([st["g16"]], [jax.ShapeDtypeStruct((N_CHIPS, pr, pc), BF16)], {}, N_CHIPS, start, finish,
             lambda outs: st.__setitem__("sib", outs[0]))


def _ride_chips(ride, st, rows=None):
    _, pr, pc = st["s16"].shape
    r0, nr = (0, pr) if rows is None else rows

    def copies(ins, outs, send_sems, recv_sems):
        x, y, c, chips = _position()
        return [_remote(ins[0].at[2 * px + py, pl.ds(r0, nr), :], outs[0].at[k, pl.ds(r0, nr), :],
                        send_sems, recv_sems, k, (px, py, c)) for k, (px, py) in enumerate(chips)]

    def start(*refs):
        for cp in copies(*refs):
            cp.start()

    def finish(*refs):
        for cp in copies(*refs):
            cp.wait()

    ins, aliases = ([st["s16"], st["recv"]], {1: 0}) if "recv" in st else ([st["s16"]], {})
    ride.add(ins, [jax.ShapeDtypeStruct((3, pr, pc), BF16)], aliases, 3, start, finish,
             lambda outs: st.__setitem__("recv", outs[0]))


def _ride_share(ride, st):
    def sent(ins, outs, send_sems, recv_sems):
        x, y, c, _ = _position()
        return _remote(_shard_half(ins[0], c), _shard_half(outs[0], c), send_sems, recv_sems, 0, (x, y, 1 - c))

    def landed(ins, outs, send_sems, recv_sems):
        x, y, c, _ = _position()
        other = _shard_half(outs[0], 1 - c)
        return _remote(other, other, send_sems, recv_sems, 0, (x, y, 1 - c))

    def start(*refs):
        sent(*refs).start()

    def finish(*refs):
        landed(*refs).wait_recv()
        sent(*refs).wait_send()

    ride.add([st["shard"]], [jax.ShapeDtypeStruct(st["shard"].shape, F32)], {0: 0}, 1, start, finish,
             lambda outs: st.__setitem__("g", outs[0]))


def _piece_block(axis, nr, nc, chip, half=False):
    if half:
        return lambda *a: (a[-3], (a[0] if chip is None else chip(a[-1])) * nc + a[-2])
    if axis == 1:
        return lambda *a: ((a[-1][1] * nr + a[-3]), (a[0] if chip is None else chip(a[-1])) * nc + a[-2])
    return lambda *a: ((2 * (a[0] if chip is None else chip(a[-1])) + a[-1][1]) * nr + a[-3], a[-2])


def _pair_add(g32, sib, axis, pos, *, name, half=False):
    _, pr, pc = sib.shape
    tr, tc = _slab_tiles(pr, pc)
    nr, nc = pr // tr, pc // tc

    def body(pos_ref, g_ref, b_ref, o16_ref):
        o16_ref[0] = (g_ref[...] + b_ref[0].astype(F32)).astype(o16_ref.dtype)

    blk = pl.BlockSpec((1, tr, tc), lambda j, i, k, pos_ref: (j, i, k))
    return _prefetch_call(body, pos, [g32, sib], [pl.BlockSpec((tr, tc), _piece_block(axis, nr, nc, None, half)), blk],
                          jax.ShapeDtypeStruct(sib.shape, BF16), blk, (N_CHIPS, nr, nc), name=name)


def _chip_sum(g32, sib, recv, axis, pos, *, name, half=False):
    _, pr, pc = sib.shape
    tr, tc = _slab_tiles(pr, pc)
    nr, nc = pr // tr, pc // tc

    def body(pos_ref, g_ref, b_ref, r_ref, o_ref):
        pair = g_ref[...] + b_ref[0].astype(F32)
        o_ref[...] = ((pair + r_ref[0].astype(F32)) + r_ref[1].astype(F32)) + r_ref[2].astype(F32)

    return _prefetch_call(
        body, pos, [g32, sib, recv],
        [pl.BlockSpec((tr, tc), _piece_block(axis, nr, nc, lambda pos_ref: pos_ref[0], half)),
         pl.BlockSpec((1, tr, tc), lambda i, k, pos_ref: (pos_ref[0], i, k)),
         pl.BlockSpec((3, tr, tc), lambda i, k, pos_ref: (0, i, k))],
        jax.ShapeDtypeStruct((2 * pr, pc), F32),
        pl.BlockSpec((tr, tc), lambda i, k, pos_ref: (pos_ref[1] * nr + i, k)), (nr, nc), name=name)


class _Comm:
    def __init__(self, pos, w):
        self.pos, self.w, self.st = pos, w, {n: {} for n, _ in BIG}

    def gather(self, names, stage, ride=None, part=(0, 1)):
        ride = _Ride() if ride is None else ride
        for n in names:
            _ride_gather(ride, self.w, n, AXIS[n], stage, part)
        return ride

    def grad(self, n, g32, g16, half=False):
        self.st[n].update(g32=g32, g16=g16, half=half)

    def pair(self, names, ride=None):
        ride = _Ride() if ride is None else ride
        for n in names:
            _ride_pair(ride, self.st[n], AXIS[n])
        return ride

    def add(self, names):
        for n in names:
            st = self.st[n]
            st["s16"] = _pair_add(st["g32"], st["sib"], AXIS[n], self.pos, name="rs_add_" + n, half=st["half"])

    def chips(self, names, ride=None, rows=None):
        ride = _Ride() if ride is None else ride
        for n in names:
            _ride_chips(ride, self.st[n], rows)
        return ride

    def sum(self, names):
        for n in names:
            st = self.st[n]
            st["shard"] = _chip_sum(st["g32"], st["sib"], st["recv"], AXIS[n], self.pos, name="rs_sum_" + n,
                                    half=st["half"])

    def share(self, names, ride=None):
        ride = _Ride() if ride is None else ride
        for n in names:
            _ride_share(ride, self.st[n])
        return ride

    def tail(self, count):
        st = self.st["w_in"]
        rows, at = st["s16"].shape[1], st.get("at", 0)
        st["at"] = at + count
        return self.chips(("w_in",), rows=(at * rows // TAIL_PARTS, count * rows // TAIL_PARTS))

    def tail_rest(self):
        return self.tail(TAIL_PARTS - self.st["w_in"].get("at", 0))

    def result(self, n):
        return self.st[n]["g"]


class _NoComm:
    pos = None

    def __init__(self, w):
        self.w, self.st = w, {}

    def grad(self, n, g32, g16, half=False):
        self.st[n] = (g32, g16)

    def result(self, n):
        return self.st[n]

    def add(self, names):
        pass

    sum = add

    def gather(self, names, *args, **kwargs):
        return None

    pair = chips = share = tail = gather


def _small_all_reduce(vec, *, name):
    r = vec.shape[0]

    def body(vec_ref, out_ref, slots, send_sems, recv_sems):
        x, y, c, _ = _position()
        me = 4 * x + 2 * y + c
        slots[me] = vec_ref[...]
        sends = []
        for k in range(1, 8):
            to = (x ^ (k >> 2), y ^ ((k >> 1) & 1), c ^ (k & 1))
            cp = _remote(slots.at[me], slots.at[me], send_sems, recv_sems, k - 1, to)
            cp.start()
            sends.append(cp)
        for k in range(1, 8):
            frm = 4 * (x ^ (k >> 2)) + 2 * (y ^ ((k >> 1) & 1)) + (c ^ (k & 1))
            _remote(slots.at[frm], slots.at[frm], send_sems, recv_sems, k - 1, (x, y, c)).wait_recv()
        for cp in sends:
            cp.wait_send()
        total = slots[0]
        for d in range(1, 8):
            total = total + slots[d]
        out_ref[...] = total

    return pl.pallas_call(
        body, name=name,
        in_specs=[pl.BlockSpec(memory_space=pltpu.VMEM)], out_specs=pl.BlockSpec(memory_space=pltpu.VMEM),
        out_shape=jax.ShapeDtypeStruct((r, 128), F32),
        scratch_shapes=[pltpu.VMEM((8, r, 128), F32), pltpu.SemaphoreType.DMA((7,)), pltpu.SemaphoreType.DMA((7,))],
    )(vec)


SC_TILES = 32
SC_LANES = 16
SC_TILE_BUDGET = 400 * 1024


def _adamw_update(wv, gv, mv, vv):
    nm = ADAM_B1 * mv + (1.0 - ADAM_B1) * gv
    nv = ADAM_B2 * vv + (1.0 - ADAM_B2) * (gv * gv)
    m_hat = nm / (1.0 - ADAM_B1 ** ADAM_STEP)
    v_hat = nv / (1.0 - ADAM_B2 ** ADAM_STEP)
    return -ADAM_LR * (m_hat / (jnp.sqrt(v_hat) + ADAM_EPS) + ADAM_WD * wv), nm, nv


def _adamw_sc(w, g, m, v, *, name):
    r, c = w.shape
    groups = r // 8
    per_tile = -(-groups // SC_TILES)
    cb = c if 4 * 8 * c * 4 <= SC_TILE_BUDGET else _pick(c, (2048, 1024, 512, 256, 128))

    def body(w_hbm, g_hbm, m_hbm, v_hbm, d_hbm, nm_hbm, nv_hbm, wb, gb, mb, vb):
        tile = lax.axis_index("sc_tile") * 2 + lax.axis_index("sc_core")

        def update(group):
            for c0 in range(0, c, cb):
                at = (pl.ds(group * 8, 8), pl.ds(c0, cb))
                for hbm, buf in ((w_hbm, wb), (g_hbm, gb), (m_hbm, mb), (v_hbm, vb)):
                    pltpu.sync_copy(hbm.at[at], buf)

                @pl.loop(0, 8)
                def _(rr):
                    @pl.loop(0, cb, step=SC_LANES)
                    def _(i):
                        lanes = (rr, pl.ds(i, SC_LANES))
                        wb[lanes], mb[lanes], vb[lanes] = _adamw_update(wb[lanes], gb[lanes], mb[lanes], vb[lanes])

                for buf, hbm in ((wb, d_hbm), (mb, nm_hbm), (vb, nv_hbm)):
                    pltpu.sync_copy(buf, hbm.at[at])

        @pl.loop(0, per_tile)
        def _(k):
            group = k * SC_TILES + tile
            if groups % SC_TILES:
                pl.when(group < groups)(lambda: update(group))
            else:
                update(group)

    sd = jax.ShapeDtypeStruct((r, c), F32)
    return pl.kernel(body, name=name, out_type=[sd, sd, sd],
                     mesh=plsc.VectorSubcoreMesh(core_axis_name="sc_core", subcore_axis_name="sc_tile"),
                     scratch_types=[pltpu.VMEM((8, cb), F32)] * 4)(w, g, m, v)


def _adamw(w, g, m, v, *, name, ride=None):
    r, c = w.shape
    tc = c if c <= 4096 else _pick(c, (2048, 1024, 512, 256, 128))
    tr = next(t for t in (512, 256, 128, 64, 32, 16, 8) if r % t == 0 and t * tc <= 256 * 1024)

    def body(w_ref, g_ref, m_ref, v_ref, d_ref, nm_ref, nv_ref):
        gv = g_ref[...]
        nm = ADAM_B1 * m_ref[...] + (1.0 - ADAM_B1) * gv
        nv = ADAM_B2 * v_ref[...] + (1.0 - ADAM_B2) * (gv * gv)
        m_hat = nm / (1.0 - ADAM_B1 ** ADAM_STEP)
        v_hat = nv / (1.0 - ADAM_B2 ** ADAM_STEP)
        d_ref[...] = -ADAM_LR * (m_hat / (jnp.sqrt(v_hat) + ADAM_EPS) + ADAM_WD * w_ref[...])
        nm_ref[...] = nm
        nv_ref[...] = nv

    blk = ((tr, tc), lambda i, j: (i, j))
    sd = jax.ShapeDtypeStruct((r, c), F32)
    return _ew(body, [w, g, m, v], [blk] * 4, [sd, sd, sd], [blk] * 3, (r // tr, c // tc), name=name, ride=ride)


BIG = (("w_in", 1), ("w_sb_out", 1), ("w_ca_out", 1), ("w_mix_out", 0), ("w_ffn_in", 1), ("w_ffn_out", 0),
       ("w_ple_in", 1), ("w_ple_gate", 0))
AXIS = dict(BIG)
HEAD_PARTS = 8
HEAD_HOSTS = ("w_ffn_in", "w_ffn_out")
TAIL_PARTS = 16
TAIL_SECOND = 4
TAIL_FIRST = 2
TAIL_HOSTS = {}
ON_SPARSECORE = tuple(n for n, _ in BIG if n != "w_in")
SMALL = ("rel_bias", "g_mix", "g_ffn", "g_ple", "g_final")
ORDER = ("w_in", "w_sb_out", "w_ca_out", "w_mix_out", "rel_bias", "g_mix", "g_ffn", "g_ple", "g_final",
         "w_ffn_in", "w_ffn_out", "w_ple_in", "w_ple_gate")


def _pack(parts):
    flat = jnp.concatenate([a.reshape(-1) for a in parts])
    rows = -(-flat.shape[0] // 1024) * 8
    return jnp.pad(flat, (0, rows * 128 - flat.shape[0])).reshape(rows, 128)


def _unpack(packed, like):
    flat, out, at = packed.reshape(-1), [], 0
    for a in like:
        out.append(flat[at:at + a.size].reshape(a.shape))
        at += a.size
    return out


def kernel(x, p, w_in, w_sb_out, w_ca_out, w_mix_out, rel_bias, g_mix, g_ffn, g_ple, g_final, w_ffn_in, w_ffn_out, w_ple_in, w_ple_gate, loss_target, m_w_in, m_w_sb_out, m_w_ca_out, m_w_mix_out, m_rel_bias, m_g_mix, m_g_ffn, m_g_ple, m_g_final, m_w_ffn_in, m_w_ffn_out, m_w_ple_in, m_w_ple_gate, v_w_in, v_w_sb_out, v_w_ca_out, v_w_mix_out, v_rel_bias, v_g_mix, v_g_ffn, v_g_ple, v_g_final, v_w_ffn_in, v_w_ffn_out, v_w_ple_in, v_w_ple_gate):
    weights = dict(w_in=w_in, w_sb_out=w_sb_out, w_ca_out=w_ca_out, w_mix_out=w_mix_out, rel_bias=rel_bias,
                   g_mix=g_mix, g_ffn=g_ffn, g_ple=g_ple, g_final=g_final, w_ffn_in=w_ffn_in,
                   w_ffn_out=w_ffn_out, w_ple_in=w_ple_in, w_ple_gate=w_ple_gate)
    m_in = dict(w_in=m_w_in, w_sb_out=m_w_sb_out, w_ca_out=m_w_ca_out, w_mix_out=m_w_mix_out, rel_bias=m_rel_bias,
                g_mix=m_g_mix, g_ffn=m_g_ffn, g_ple=m_g_ple, g_final=m_g_final, w_ffn_in=m_w_ffn_in,
                w_ffn_out=m_w_ffn_out, w_ple_in=m_w_ple_in, w_ple_gate=m_w_ple_gate)
    v_in = dict(w_in=v_w_in, w_sb_out=v_w_sb_out, w_ca_out=v_w_ca_out, w_mix_out=v_w_mix_out, rel_bias=v_rel_bias,
                g_mix=v_g_mix, g_ffn=v_g_ffn, g_ple=v_g_ple, g_final=v_g_final, w_ffn_in=v_w_ffn_in,
                w_ffn_out=v_w_ffn_out, w_ple_in=v_w_ple_in, w_ple_gate=v_w_ple_gate)

    pos = jnp.stack([2 * lax.axis_index("x") + lax.axis_index("y"), lax.axis_index("c")]).astype(jnp.int32)
    comm = _Comm(pos, {"w_in": _cast_place(w_in[0], AXIS["w_in"], pos, name="cast_w_in")})
    at = 0
    for n in HEAD_HOSTS:
        ride = comm.gather(("w_in",), "near", part=(at, HEAD_PARTS))
        comm.w[n] = _cast_place(weights[n][0], AXIS[n], pos, name="cast_" + n, ride=ride)
        at += 1
    for n, axis in BIG:
        if n not in comm.w:
            comm.w[n] = _cast_place(weights[n][0], axis, pos, name="cast_" + n)
    _run(comm.gather(("w_in",), "near", part=(at, HEAD_PARTS, HEAD_PARTS - at)), name="gather_w_in_near")
    _run(comm.gather(("w_in",), "far"), name="gather_w_in_far")
    _run(comm.gather(("w_in",), "pair"), name="gather_w_in_pair")
    small = dict(rel_bias=rel_bias[0], g_mix=g_mix, g_ffn=g_ffn, g_ple=g_ple, g_final=g_final.reshape(1, -1))
    loss, grad_x, gs = _step(x[0], p[0, 0], loss_target[0], small, comm)

    grads, delta, new_m, new_v = {}, {}, {}, {}
    for n in [n for n, _ in BIG if n != "w_in"] + ["w_in"]:
        ride = comm.tail(TAIL_HOSTS[n]) if n in TAIL_HOSTS else None
        if n == "w_in":
            _run(comm.tail_rest(), name="rs_chips_w_in")
            comm.sum(("w_in",))
            _run(comm.share(("w_in",)), name="rs_share_w_in")
        g = comm.result(n)
        if n in ON_SPARSECORE:
            d, nm, nv = _adamw_sc(weights[n][0], g, m_in[n][0], v_in[n][0], name="adamw_sc_" + n)
        else:
            d, nm, nv = _adamw(weights[n][0], g, m_in[n][0], v_in[n][0], name="adamw_" + n, ride=ride)
        grads[n], delta[n], new_m[n], new_v[n] = g[None], d[None], nm[None], nv[None]

    like = [weights[n] for n in SMALL]
    reduced = _small_all_reduce(_pack([gs[n] for n in SMALL] + [loss[:, :1]]), name="small_all_reduce")
    g_small = _unpack(reduced, like + [loss[:, :1]])
    total_loss = g_small[-1].reshape(())
    g_packed = _pack(g_small[:-1])
    d_s, m_s, v_s = _adamw(_pack(like), g_packed, _pack([m_in[n] for n in SMALL]), _pack([v_in[n] for n in SMALL]),
                           name="adamw_small")
    for n, g, d, nm, nv in zip(SMALL, g_small[:-1], _unpack(d_s, like), _unpack(m_s, like), _unpack(v_s, like)):
        grads[n], delta[n], new_m[n], new_v[n] = g, d, nm, nv

    return (total_loss, grad_x[None], *[grads[n] for n in ORDER], *[delta[n] for n in ORDER],
            *[new_m[n] for n in ORDER], *[new_v[n] for n in ORDER])
```

```python
import functools
import math

import jax
import jax.numpy as jnp
import numpy as np
from jax import lax
from jax.experimental import pallas as pl
from jax.experimental.pallas import tpu as pltpu
from jax.experimental.pallas import tpu_sc as plsc

F32 = jnp.float32
BF16 = jnp.bfloat16

HEAD_DIM = 128
CHUNK = 64
LEFT_CHUNKS = 8
REL_CLIP = 128
N_REL = REL_CLIP + CHUNK
BAND = (LEFT_CHUNKS + 2) * CHUNK
CA_PER_STEP = 4
CA_ROWS = CA_PER_STEP * CHUNK
CA_BAND = BAND + CA_PER_STEP * CHUNK
CA_PAD = BAND
SB_BLOCK = 128
SB_KEYS = 512
SB_GROUPS = SB_KEYS // SB_BLOCK
SB_ROWS = SB_KEYS
EPS = 1e-6
NEG = -1e30

ADAM_LR = 0.001
ADAM_B1 = 0.9
ADAM_B2 = 0.999
ADAM_EPS = 1e-08
ADAM_WD = 0.01
ADAM_STEP = 10

VMEM_LIMIT = 48 * 1024 * 1024
MM_VMEM_BUDGET = 36 * 1024 * 1024
V7X_HBM_BYTES_PER_S = 3.7e12
GRID_STEP_S = 0.35e-6
MESH = pl.DeviceIdType.MESH
N_CHIPS = 4


def _pick(dim, prefs):
    for t in prefs:
        if dim % t == 0:
            return t
    raise ValueError(f"no tile for {dim}")


def _cparams(sem=None):
    return pltpu.CompilerParams(dimension_semantics=sem, vmem_limit_bytes=VMEM_LIMIT)


def _sigmoid(v):
    return 1.0 / (1.0 + jnp.exp(-v))


def _dot(a, b, dims):
    return lax.dot_general(a, b, (dims, ((), ())), preferred_element_type=F32)


def _dot_nn(a, b):
    return _dot(a, b, ((1,), (0,)))


def _dot_nt(a, b):
    return _dot(a, b, ((1,), (1,)))


def _dot_tn(a, b):
    return _dot(a, b, ((0,), (0,)))


HBM = pl.BlockSpec(memory_space=pltpu.HBM)


class _Ride:
    def __init__(self):
        self.items = []

    def add(self, ins, outs, aliases, n_sems, start, finish, sink):
        self.items.append((ins, outs, aliases, n_sems, start, finish, sink))


def _call(body, args, *, name, grid, in_specs, out_specs, out_shape, scratch_shapes=(), sem=None, ride=None,
          scalars=None, onto=()):
    items = ride.items if ride is not None else []
    if onto:
        args, in_specs = list(args) + list(onto), list(in_specs) + [HBM] * len(onto)
        inner, body = body, lambda *refs: inner(*refs[:len(args) - len(onto)], *refs[len(args):])
    n_in, n_out, n_scr = len(args), len(out_shape), len(scratch_shapes)
    r_ins = [a for it in items for a in it[0]]
    r_outs = [o for it in items for o in it[1]]
    updated = [id(it[0][i]) for it in items for i in it[2]]
    assert len(set(updated)) == len(updated), "one call may update a buffer in place only once"
    aliases, a, b = {n_in - len(onto) + t: t for t in range(len(onto))}, n_in, n_out
    for it in items:
        aliases.update({a + i: b + o for i, o in it[2].items()})
        a, b = a + len(it[0]), b + len(it[1])
    sems = [pltpu.SemaphoreType.DMA((it[3],)) for it in items for _ in range(2)]

    def wrapped(*refs):
        head, refs = (refs[:1], refs[1:]) if scalars is not None else ((), refs)
        ins, rin = refs[:n_in], refs[n_in:n_in + len(r_ins)]
        at = n_in + len(r_ins)
        outs, rout = refs[at:at + n_out], refs[at + n_out:at + n_out + len(r_outs)]
        at += n_out + len(r_outs)
        scr, rsem = refs[at:at + n_scr], refs[at + n_scr:]

        def each(which):
            a = b = 0
            for q, it in enumerate(items):
                it[which](rin[a:a + len(it[0])], rout[b:b + len(it[1])], rsem[2 * q], rsem[2 * q + 1])
                a, b = a + len(it[0]), b + len(it[1])

        if items:
            ids = [pl.program_id(d) for d in range(len(grid))]
            first = functools.reduce(jnp.logical_and, [i == 0 for i in ids])
            last = functools.reduce(jnp.logical_and, [i == g - 1 for i, g in zip(ids, grid)])
            pl.when(first)(lambda: each(4))
        body(*head, *ins, *outs, *scr)
        if items:
            pl.when(last)(lambda: each(5))

    specs = dict(grid=grid, in_specs=list(in_specs) + [HBM] * len(r_ins),
                 out_specs=list(out_specs) + [HBM] * len(r_outs), scratch_shapes=list(scratch_shapes) + sems)
    if scalars is not None:
        specs = dict(grid_spec=pltpu.PrefetchScalarGridSpec(num_scalar_prefetch=1, **specs))
        aliases = {i + 1: o for i, o in aliases.items()}
    res = pl.pallas_call(
        wrapped, name=name, **specs,
        out_shape=list(out_shape) + r_outs,
        input_output_aliases=aliases,
        compiler_params=_cparams(("arbitrary",) * len(grid) if items else sem),
    )(*(() if scalars is None else (scalars,)), *args, *r_ins)
    b = n_out
    for it in items:
        it[6](res[b:b + len(it[1])])
        b += len(it[1])
    return list(res[:n_out])


def _mm_tiles(m, n_align, n, k, a_bytes, b_bytes, out_bytes):
    best = None
    tks = sorted({t for t in (k, k // 2, k // 4, 2048, 1024, 512, 256, 128) if t <= k and k % t == 0 and t % 128 == 0})
    for tm in (t for t in (2048, 1024, 512, 256, 128) if m % t == 0):
        for tn in (t for t in (2048, 1024, 512, 256, 128) if n_align % t == 0):
            for tk in tks:
                nk = k // tk
                vmem = 2 * (tm * tk * a_bytes + tk * tn * b_bytes + tm * tn * out_bytes) + tm * tn * 4
                if vmem > MM_VMEM_BUDGET:
                    continue
                traffic = m * k * a_bytes * (n // tn if nk > 1 else 1) + k * n * b_bytes * (m // tm)
                traffic += tm * tk * a_bytes + tk * tn * b_bytes + tm * tn * out_bytes
                traffic += m * n * 4 * nk if nk > 1 else 0
                cost = traffic / V7X_HBM_BYTES_PER_S + (m // tm) * (n // tn) * nk * GRID_STEP_S
                if best is None or cost < best[0]:
                    best = (cost, tm, tn, tk)
    return best[1:]


def _mm(a, b, mode, out_dtypes, *, name, n=None, b_col_off=0, resid=None, ride=None, rows=None, onto=(), m_half=None):
    if mode == "nn":
        m, k = a.shape
        n = b.shape[1] if n is None else n
    elif mode == "nt":
        m, k = a.shape
        n = b.shape[0]
    else:
        k, m = a.shape
        n = b.shape[1]
    if m_half is not None:
        m //= 2
    m_all, (row0, m) = m, (0, m) if rows is None else rows
    n_out = len(out_dtypes)
    has_resid = resid is not None
    out_bytes = sum(jnp.dtype(dt).itemsize for dt in out_dtypes) + (4 if has_resid else 0)
    tm, tn, tk = _mm_tiles(math.gcd(m, row0) if row0 else m, math.gcd(n, b_col_off) if b_col_off else n, n, k,
                           a.dtype.itemsize, b.dtype.itemsize, out_bytes)
    nk = k // tk
    boff, roff = b_col_off // tn, row0 // tm
    dot = {"nn": _dot_nn, "nt": _dot_nt, "tn": _dot_tn}[mode]
    if m_half is None:
        half = lambda: 0
    else:
        half = lambda pos_ref: (pos_ref[1] if m_half[0] else 1 - pos_ref[1]) * (m // tm)

    def body(*refs):
        refs = refs[m_half is not None:]
        a_ref, b_ref = refs[0], refs[1]
        r_ref = refs[2] if has_resid else None
        o_refs = refs[2 + has_resid: 2 + has_resid + n_out]

        def finish(r):
            if has_resid:
                r = r + r_ref[...]
            for o_ref in o_refs:
                o_ref[...] = r.astype(o_ref.dtype)

        part = dot(a_ref[...].astype(BF16), b_ref[...].astype(BF16))
        if nk == 1:
            finish(part)
            return
        acc_ref = refs[-1]
        kk = pl.program_id(2)

        @pl.when(kk == 0)
        def _():
            acc_ref[...] = part

        @pl.when(kk > 0)
        def _():
            acc_ref[...] += part

        @pl.when(kk == nk - 1)
        def _():
            finish(acc_ref[...])

    if mode == "nn":
        a_spec = pl.BlockSpec((tm, tk), lambda i, j, kk, *_: (i + roff, kk))
        b_spec = pl.BlockSpec((tk, tn), lambda i, j, kk, *_: (kk, j + boff))
    elif mode == "nt":
        a_spec = pl.BlockSpec((tm, tk), lambda i, j, kk, *_: (i + roff, kk))
        b_spec = pl.BlockSpec((tn, tk), lambda i, j, kk, *_: (j, kk))
    else:
        a_spec = pl.BlockSpec((tk, tm), lambda i, j, kk, *pos: (kk, i + half(*pos)))
        b_spec = pl.BlockSpec((tk, tn), lambda i, j, kk, *_: (kk, j))
    o_spec = pl.BlockSpec((tm, tn), lambda i, j, kk, *_: (i + roff, j))
    in_specs = [a_spec, b_spec] + ([o_spec] if has_resid else [])
    args = [a, b] + ([resid] if has_resid else [])
    outs = _call(
        body, args, name=name,
        grid=(m // tm, n // tn, nk),
        in_specs=in_specs,
        out_specs=[o_spec] * n_out,
        out_shape=[jax.ShapeDtypeStruct((m_all, n), dt) for dt in out_dtypes],
        scratch_shapes=[pltpu.VMEM((tm, tn), F32)] if nk > 1 else [],
        sem=("parallel", "parallel", "arbitrary"), ride=ride, onto=onto,
        scalars=None if m_half is None else m_half[1])
    return outs[0] if n_out == 1 else tuple(outs)


def _row_tile(s):
    return _pick(s, (256, 128))


def _rms_fwd(x, g, *, name, ride=None):
    s, d = x.shape
    tr = _row_tile(s)

    def body(x_ref, g_ref, o_ref):
        xv = x_ref[...]
        r = lax.rsqrt(jnp.mean(xv * xv, axis=1, keepdims=True) + EPS)
        o_ref[...] = (xv * r * g_ref[...]).astype(o_ref.dtype)

    return _call(
        body, [x, g], name=name, grid=(s // tr,),
        in_specs=[pl.BlockSpec((tr, d), lambda i: (i, 0)), pl.BlockSpec((1, d), lambda i: (0, 0))],
        out_specs=[pl.BlockSpec((tr, d), lambda i: (i, 0))],
        out_shape=[jax.ShapeDtypeStruct((s, d), BF16)], sem=("parallel",), ride=ride)[0]


def _rms_bwd(x, g, dh, dres, *, name, ride=None):
    s, d = x.shape
    tr = _row_tile(s)

    def body(x_ref, g_ref, dh_ref, dres_ref, dx_ref, dx16_ref, dg_ref):
        i = pl.program_id(0)
        xv = x_ref[...]
        r = lax.rsqrt(jnp.mean(xv * xv, axis=1, keepdims=True) + EPS)
        xhat = xv * r
        dhv = dh_ref[...]
        dxhat = dhv * g_ref[...]
        proj = jnp.mean(dxhat * xhat, axis=1, keepdims=True)
        dx = dres_ref[...] + r * (dxhat - xhat * proj)
        dx_ref[...] = dx
        dx16_ref[...] = dx.astype(dx16_ref.dtype)

        @pl.when(i == 0)
        def _():
            dg_ref[...] = jnp.zeros_like(dg_ref)

        dg_ref[...] += jnp.sum(dhv * xhat, axis=0, keepdims=True)

    row = pl.BlockSpec((tr, d), lambda i: (i, 0))
    vec = pl.BlockSpec((1, d), lambda i: (0, 0))
    return _call(
        body, [x, g, dh, dres], name=name, grid=(s // tr,),
        in_specs=[row, vec, row, row],
        out_specs=[row, row, vec],
        out_shape=[jax.ShapeDtypeStruct((s, d), F32), jax.ShapeDtypeStruct((s, d), BF16),
                   jax.ShapeDtypeStruct((1, d), F32)],
        sem=("arbitrary",), ride=ride)


def _final_loss(x, g, target, *, name):
    s, d = x.shape
    tr = _row_tile(s)

    def body(x_ref, g_ref, t_ref, dx_ref, dg_ref, loss_ref):
        i = pl.program_id(0)
        xv = x_ref[...]
        gv = g_ref[...]
        r = lax.rsqrt(jnp.mean(xv * xv, axis=1, keepdims=True) + EPS)
        xhat = xv * r
        err = xhat * gv - t_ref[...]
        dy = err * (1.0 / d)
        dxhat = dy * gv
        proj = jnp.mean(dxhat * xhat, axis=1, keepdims=True)
        dx_ref[...] = r * (dxhat - xhat * proj)

        @pl.when(i == 0)
        def _():
            dg_ref[...] = jnp.zeros_like(dg_ref)
            loss_ref[...] = jnp.zeros_like(loss_ref)

        dg_ref[...] += jnp.sum(dy * xhat, axis=0, keepdims=True)
        part = 0.5 * jnp.sum(jnp.mean(err * err, axis=1, keepdims=True), axis=0, keepdims=True)
        loss_ref[...] += jnp.broadcast_to(part, loss_ref.shape)

    row = pl.BlockSpec((tr, d), lambda i: (i, 0))
    vec = pl.BlockSpec((1, d), lambda i: (0, 0))
    return pl.pallas_call(
        body, name=name, grid=(s // tr,),
        in_specs=[row, vec, row],
        out_specs=[row, vec, pl.BlockSpec((1, 128), lambda i: (0, 0))],
        out_shape=[jax.ShapeDtypeStruct((s, d), F32), jax.ShapeDtypeStruct((1, d), F32),
                   jax.ShapeDtypeStruct((1, 128), F32)],
        compiler_params=_cparams(("arbitrary",)),
    )(x, g, target)


def _ew(body, ins, in_blocks, outs, out_blocks, grid, *, name, ride=None):
    return _call(body, ins, name=name, grid=grid,
                 in_specs=[pl.BlockSpec(bs, im) for bs, im in in_blocks],
                 out_specs=[pl.BlockSpec(bs, im) for bs, im in out_blocks],
                 out_shape=outs, sem=("parallel",) * len(grid), ride=ride)


def _gate_merge_fwd(gates, o_sb, o_ca, *, name, ride=None):
    s, d = o_sb.shape
    tr, tc = _row_tile(s), _pick(d, (1024, 512, 256, 128))
    nc = d // tc

    def body(gs_ref, gc_ref, os_ref, oc_ref, m_ref):
        m = _sigmoid(gs_ref[...]) * os_ref[...] + _sigmoid(gc_ref[...]) * oc_ref[...]
        m_ref[...] = m.astype(m_ref.dtype)

    blk = ((tr, tc), lambda i, j: (i, j))
    return _ew(body, [gates, gates, o_sb, o_ca],
               [blk, ((tr, tc), lambda i, j: (i, j + nc)), blk, blk],
               [jax.ShapeDtypeStruct((s, d), BF16)], [blk], (s // tr, nc), name=name, ride=ride)[0]


def _gate_merge_bwd(dmerged, gates, o_sb, o_ca, *, name):
    s, d = o_sb.shape
    tr, tc = _row_tile(s), _pick(d, (1024, 512, 256, 128))
    nc = d // tc

    def body(dm_ref, gs_ref, gc_ref, os_ref, oc_ref, dgs_ref, dgc_ref, dos_ref, doc_ref):
        dm = dm_ref[...]
        ss = _sigmoid(gs_ref[...])
        sc = _sigmoid(gc_ref[...])
        dgs_ref[...] = (dm * os_ref[...] * ss * (1.0 - ss)).astype(dgs_ref.dtype)
        dgc_ref[...] = (dm * oc_ref[...] * sc * (1.0 - sc)).astype(dgc_ref.dtype)
        dos_ref[...] = (dm * ss).astype(dos_ref.dtype)
        doc_ref[...] = (dm * sc).astype(doc_ref.dtype)

    blk = ((tr, tc), lambda i, j: (i, j))
    sd = jax.ShapeDtypeStruct((s, d), BF16)
    return _ew(body, [dmerged, gates, gates, o_sb, o_ca],
               [blk, blk, ((tr, tc), lambda i, j: (i, j + nc)), blk, blk],
               [sd, sd, sd, sd], [blk, blk, blk, blk], (s // tr, nc), name=name)


def _swiglu_fwd(gu, *, name, ride=None):
    s, f2 = gu.shape
    f = f2 // 2
    tr, tc = 128, _pick(f, (512, 256, 128))

    def body(gu_ref, a_ref):
        for at in range(0, f, tc):
            gv = gu_ref[:, at:at + tc].astype(F32)
            a_ref[:, at:at + tc] = (gv * _sigmoid(gv) * gu_ref[:, f + at:f + at + tc].astype(F32)).astype(a_ref.dtype)

    row = lambda i: (i, 0)
    return _ew(body, [gu], [((tr, f2), row)], [jax.ShapeDtypeStruct((s, f), BF16)], [((tr, f), row)],
               (s // tr,), name=name, ride=ride)[0]


def _swiglu_bwd(dact, gu, *, name):
    s, f2 = gu.shape
    f = f2 // 2
    tr, tc = 128, _pick(f, (512, 256, 128))

    def body(da_ref, gu_ref, o_ref):
        for at in range(0, f, tc):
            da = da_ref[:, at:at + tc]
            gv = gu_ref[:, at:at + tc].astype(F32)
            sg = _sigmoid(gv)
            uv = gu_ref[:, f + at:f + at + tc].astype(F32)
            o_ref[:, at:at + tc] = (da * uv * sg * (1.0 + gv * (1.0 - sg))).astype(o_ref.dtype)
            o_ref[:, f + at:f + at + tc] = (da * gv * sg).astype(o_ref.dtype)

    row = lambda i: (i, 0)
    return _ew(body, [dact, gu], [((tr, f), row), ((tr, f2), row)], [jax.ShapeDtypeStruct((s, f2), BF16)],
               [((tr, f2), row)], (s // tr,), name=name)[0]


def _concat_cols(parts, *, name):
    s = parts[0].shape[0]
    widths = [p.shape[1] for p in parts]
    tr = 256

    def body(*refs):
        o_ref, at = refs[-1], 0
        for p_ref, width in zip(refs, widths):
            o_ref[:, at:at + width] = p_ref[...]
            at += width

    row = lambda i: (i, 0)
    return _ew(body, list(parts), [((tr, width), row) for width in widths],
               [jax.ShapeDtypeStruct((s, sum(widths)), parts[0].dtype)], [((tr, sum(widths)), row)],
               (s // tr,), name=name)[0]


def _ple_fwd(x, t, pe, *, name):
    s, d = x.shape
    tr, tc = _row_tile(s), _pick(d, (1024, 512, 256, 128))

    def body(x_ref, t_ref, p_ref, o_ref):
        o_ref[...] = x_ref[...] + _sigmoid(t_ref[...]) * p_ref[...]

    blk = ((tr, tc), lambda i, j: (i, j))
    return _ew(body, [x, t, pe], [blk, blk, blk],
               [jax.ShapeDtypeStruct((s, d), F32)], [blk], (s // tr, d // tc), name=name)[0]


def _ple_bwd(dx, t, pe, *, name):
    s, d = dx.shape
    tr, tc = _row_tile(s), _pick(d, (1024, 512, 256, 128))

    def body(dx_ref, t_ref, p_ref, dt_ref, dp_ref):
        dxv = dx_ref[...]
        sg = _sigmoid(t_ref[...])
        dt_ref[...] = (dxv * p_ref[...] * sg * (1.0 - sg)).astype(dt_ref.dtype)
        dp_ref[...] = (dxv * sg).astype(dp_ref.dtype)

    blk = ((tr, tc), lambda i, j: (i, j))
    sd = jax.ShapeDtypeStruct((s, d), BF16)
    return _ew(body, [dx, t, pe], [blk, blk, blk], [sd, sd], [blk, blk], (s // tr, d // tc), name=name)


def _sb_tri(later):
    row = lax.broadcasted_iota(jnp.int32, (SB_BLOCK, SB_BLOCK), 0)
    col = lax.broadcasted_iota(jnp.int32, (SB_BLOCK, SB_BLOCK), 1)
    tri = (row > col) if later else (row < col)
    return jnp.concatenate([tri.astype(BF16), jnp.ones((SB_BLOCK, SB_BLOCK), BF16)], axis=1)


def _sb_valid(i, j, own):
    if not own:
        return None
    qi = i * SB_ROWS + lax.broadcasted_iota(jnp.int32, (SB_ROWS, SB_KEYS), 0)
    ki = j * SB_KEYS + lax.broadcasted_iota(jnp.int32, (SB_ROWS, SB_KEYS), 1)
    return ki < qi


def _sb_scan(v, tri, run, later):
    hi = v.astype(BF16)
    lo = (v - hi.astype(F32)).astype(BF16)
    outs = [None] * SB_GROUPS
    for b in (reversed(range(SB_GROUPS)) if later else range(SB_GROUPS)):
        cols = slice(b * SB_BLOCK, (b + 1) * SB_BLOCK)
        r = _dot_nn(hi[:, cols], tri) + _dot_nn(lo[:, cols], tri)
        outs[b] = r[:, :SB_BLOCK] + run
        run = run + r[:, SB_BLOCK:]
    return jnp.concatenate(outs, axis=1), run


def _masked(valid, v):
    return v if valid is None else jnp.where(valid, v, 0.0)


def _sb_scores(q, kj, scale, valid):
    z = _dot_nt(q, kj) * scale
    t = jnp.log(1.0 + jnp.exp(-jnp.abs(z)))
    return jnp.minimum(z, 0.0) - t, _masked(valid, -jnp.maximum(z, 0.0) - t)


def _sb_specs(h_count, s, col0):
    q_spec = pl.BlockSpec((SB_ROWS, HEAD_DIM), lambda h, i: (i, col0 + h))
    k_spec = pl.BlockSpec((s, HEAD_DIM), lambda h, i: (0, col0 + h_count + h))
    v_spec = pl.BlockSpec((s, HEAD_DIM), lambda h, i: (0, col0 + 2 * h_count + h))
    return q_spec, k_spec, v_spec


def _sb_fwd(qkv, n_heads, col0, *, name, ride=None):
    s = qkv.shape[0]
    nq = s // SB_ROWS
    scale = HEAD_DIM ** -0.5

    def body(q_ref, k_ref, v_ref, o_ref):
        i = pl.program_id(1)
        q = q_ref[...]
        tri = _sb_tri(later=True)

        def step(j, carry, own):
            run, acc = carry
            off = pl.multiple_of(j * SB_KEYS, SB_KEYS)
            valid = _sb_valid(i, j, own)
            ls, lk = _sb_scores(q, k_ref[pl.ds(off, SB_KEYS), :], scale, valid)
            between, run = _sb_scan(lk, tri, run, later=True)
            a = _masked(valid, jnp.exp(ls + between))
            return run, acc + _dot_nn(a.astype(BF16), v_ref[pl.ds(off, SB_KEYS), :])

        carry = step(i, (jnp.zeros((SB_ROWS, SB_BLOCK), F32), jnp.zeros((SB_ROWS, HEAD_DIM), F32)), True)
        _, acc = lax.fori_loop(0, i, lambda jj, c: step(i - 1 - jj, c, False), carry)
        o_ref[...] = acc.astype(o_ref.dtype)

    q_spec, k_spec, v_spec = _sb_specs(n_heads, s, col0)
    return _call(
        body, [qkv, qkv, qkv], name=name, grid=(n_heads, nq),
        in_specs=[q_spec, k_spec, v_spec],
        out_specs=[pl.BlockSpec((SB_ROWS, HEAD_DIM), lambda h, i: (i, h))],
        out_shape=[jax.ShapeDtypeStruct((s, n_heads * HEAD_DIM), BF16)],
        sem=("parallel", "arbitrary"), ride=ride)[0]


def _sb_bwd(qkv, dy, n_heads, col0, *, name, ride=None):
    s = qkv.shape[0]
    nq = s // SB_ROWS
    scale = HEAD_DIM ** -0.5

    def body(q_ref, k_ref, v_ref, dy_ref, dq_ref, dk_ref, dv_ref, e_scr, sg_scr, dk_acc, dv_acc):
        i = pl.program_id(1)
        q = q_ref[...]
        dyv = dy_ref[...]

        @pl.when(i == 0)
        def _():
            dk_acc[...] = jnp.zeros_like(dk_acc)
            dv_acc[...] = jnp.zeros_like(dv_acc)

        tri_later = _sb_tri(later=True)

        def pass1(j, run, own):
            off = pl.multiple_of(j * SB_KEYS, SB_KEYS)
            valid = _sb_valid(i, j, own)
            ls, lk = _sb_scores(q, k_ref[pl.ds(off, SB_KEYS), :], scale, valid)
            between, run = _sb_scan(lk, tri_later, run, later=True)
            a = _masked(valid, jnp.exp(ls + between))
            e_scr[j] = a * _dot_nt(dyv, v_ref[pl.ds(off, SB_KEYS), :])
            sg_scr[j] = jnp.exp(ls)
            dv_acc[pl.ds(off, SB_KEYS), :] += _dot_tn(a.astype(BF16), dyv)
            return run

        lax.fori_loop(0, i, lambda jj, run: pass1(i - 1 - jj, run, False),
                      pass1(i, jnp.zeros((SB_ROWS, SB_BLOCK), F32), True))

        tri_earlier = _sb_tri(later=False)

        def pass2(j, carry, own):
            run, dq = carry
            off = pl.multiple_of(j * SB_KEYS, SB_KEYS)
            kj = k_ref[pl.ds(off, SB_KEYS), :]
            sg = sg_scr[j]
            e = e_scr[j]
            before, run = _sb_scan(e, tri_earlier, run, later=False)
            dz = _masked(_sb_valid(i, j, own), e * (1.0 - sg) - sg * before) * scale
            dzb = dz.astype(BF16)
            dk_acc[pl.ds(off, SB_KEYS), :] += _dot_tn(dzb, q)
            return run, dq + _dot_nn(dzb, kj)

        init = (jnp.zeros((SB_ROWS, SB_BLOCK), F32), jnp.zeros((SB_ROWS, HEAD_DIM), F32))
        _, dq = pass2(i, lax.fori_loop(0, i, lambda j, c: pass2(j, c, False), init), True)
        dq_ref[...] = dq.astype(dq_ref.dtype)

        @pl.when(i == nq - 1)
        def _():
            dk_ref[...] = dk_acc[...].astype(dk_ref.dtype)
            dv_ref[...] = dv_acc[...].astype(dv_ref.dtype)

    q_spec, k_spec, v_spec = _sb_specs(n_heads, s, col0)
    blk = pl.BlockSpec((SB_ROWS, HEAD_DIM), lambda h, i: (i, h))
    full = pl.BlockSpec((s, HEAD_DIM), lambda h, i: (0, h))
    sd = jax.ShapeDtypeStruct((s, n_heads * HEAD_DIM), BF16)
    return _call(
        body, [qkv, qkv, qkv, dy], name=name, grid=(n_heads, nq),
        in_specs=[q_spec, k_spec, v_spec, blk],
        out_specs=[blk, full, full],
        out_shape=[sd, sd, sd],
        scratch_shapes=[pltpu.VMEM((s // SB_KEYS, SB_ROWS, SB_KEYS), F32), pltpu.VMEM((s // SB_KEYS, SB_ROWS, SB_KEYS), F32),
                        pltpu.VMEM((s, HEAD_DIM), F32), pltpu.VMEM((s, HEAD_DIM), F32)],
        sem=("parallel", "arbitrary"), ride=ride)


def _band_bias(rel_bias):
    h = rel_bias.shape[0]
    width = BAND + CHUNK
    first = width - 1 - N_REL
    line = jnp.concatenate([jnp.broadcast_to(rel_bias[:, :1], (h, first)), rel_bias], axis=1)
    tiled = jnp.broadcast_to(line[:, None, :], (h, CHUNK, width - 1)).reshape(h, CHUNK * (width - 1))
    skew = jnp.pad(tiled, ((0, 0), (0, CHUNK))).reshape(h, CHUNK, width)[:, ::-1, :BAND]
    seen = jnp.arange(BAND) >= CHUNK
    return jnp.where(seen[None, None, :], skew, NEG)


def _band_bias_grad(dbias):
    h = dbias.shape[0]
    width = BAND + CHUNK
    flipped = jnp.pad(dbias[:, ::-1, :], ((0, 0), (0, 0), (0, CHUNK)))
    skew = flipped.reshape(h, CHUNK * width)[:, :CHUNK * (width - 1)].reshape(h, CHUNK, width - 1)
    diag = jnp.sum(skew, axis=1)
    first = width - 1 - N_REL
    clipped = jnp.sum(diag[:, :first + 1], axis=1, keepdims=True)
    return jnp.concatenate([clipped, diag[:, first + 1:]], axis=1)


def _group_bias(band):
    return jnp.concatenate([jnp.pad(band, ((0, 0), (0, 0), ((u + 1) * CHUNK, (CA_PER_STEP - 1 - u) * CHUNK)),
                                    constant_values=NEG) for u in range(CA_PER_STEP)], axis=1)


def _group_bias_grad(dgroup):
    return sum(dgroup[:, u * CHUNK:(u + 1) * CHUNK, (u + 1) * CHUNK:(u + 1) * CHUNK + BAND] for u in range(CA_PER_STEP))


def _ca_load_padded(k_ref, v_ref, kp, vp, s):
    kp[pl.ds(0, CA_PAD), :] = jnp.zeros((CA_PAD, HEAD_DIM), kp.dtype)
    vp[pl.ds(0, CA_PAD), :] = jnp.zeros((CA_PAD, HEAD_DIM), vp.dtype)
    kp[pl.ds(CA_PAD, s), :] = k_ref[...]
    vp[pl.ds(CA_PAD, s), :] = v_ref[...]


def _ca_weights(q, kb, bias, off, scale):
    z = _dot_nt(q, kb) * scale + bias
    pos = off + lax.broadcasted_iota(jnp.int32, (CA_ROWS, CA_BAND), 1)
    z = jnp.where(pos >= CA_PAD, z, NEG)
    p = jnp.exp(z - jnp.max(z, axis=1, keepdims=True))
    return p / jnp.sum(p, axis=1, keepdims=True)


def _ca_specs(h_count, s, col0):
    q_spec = pl.BlockSpec((CA_ROWS, HEAD_DIM), lambda h, c: (c, col0 + h))
    k_spec = pl.BlockSpec((s, HEAD_DIM), lambda h, c: (0, col0 + h_count + h))
    v_spec = pl.BlockSpec((s, HEAD_DIM), lambda h, c: (0, col0 + 2 * h_count + h))
    b_spec = pl.BlockSpec((1, CA_ROWS, CA_BAND), lambda h, c: (h, 0, 0))
    return q_spec, k_spec, v_spec, b_spec


def _ca_fwd(qkv, bias, n_heads, col0, *, name, ride=None):
    s = qkv.shape[0]
    nc = s // CA_ROWS
    scale = HEAD_DIM ** -0.5

    def body(q_ref, k_ref, v_ref, b_ref, o_ref, kp, vp):
        c = pl.program_id(1)

        @pl.when(c == 0)
        def _():
            _ca_load_padded(k_ref, v_ref, kp, vp, s)

        off = pl.multiple_of(c * CA_ROWS, CA_ROWS)
        w = _ca_weights(q_ref[...], kp[pl.ds(off, CA_BAND), :], b_ref[0], off, scale)
        o_ref[...] = _dot_nn(w.astype(BF16), vp[pl.ds(off, CA_BAND), :]).astype(o_ref.dtype)

    q_spec, k_spec, v_spec, b_spec = _ca_specs(n_heads, s, col0)
    return _call(
        body, [qkv, qkv, qkv, bias], name=name, grid=(n_heads, nc),
        in_specs=[q_spec, k_spec, v_spec, b_spec],
        out_specs=[pl.BlockSpec((CA_ROWS, HEAD_DIM), lambda h, c: (c, h))],
        out_shape=[jax.ShapeDtypeStruct((s, n_heads * HEAD_DIM), BF16)],
        scratch_shapes=[pltpu.VMEM((s + CA_PAD, HEAD_DIM), BF16), pltpu.VMEM((s + CA_PAD, HEAD_DIM), BF16)],
        sem=("parallel", "arbitrary"), ride=ride)[0]


def _ca_bwd(qkv, bias, dy, n_heads, col0, *, name, ride=None):
    s = qkv.shape[0]
    nc = s // CA_ROWS
    scale = HEAD_DIM ** -0.5

    def body(q_ref, k_ref, v_ref, b_ref, dy_ref, dq_ref, dk_ref, dv_ref, db_ref, kp, vp, dkp, dvp):
        c = pl.program_id(1)

        @pl.when(c == 0)
        def _():
            _ca_load_padded(k_ref, v_ref, kp, vp, s)
            dkp[...] = jnp.zeros_like(dkp)
            dvp[...] = jnp.zeros_like(dvp)
            db_ref[...] = jnp.zeros_like(db_ref)

        off = pl.multiple_of(c * CA_ROWS, CA_ROWS)
        band = pl.ds(off, CA_BAND)
        q = q_ref[...]
        dyv = dy_ref[...]
        kb = kp[band, :]
        w = _ca_weights(q, kb, b_ref[0], off, scale)
        dw = _dot_nt(dyv, vp[band, :])
        dvp[band, :] += _dot_tn(w.astype(BF16), dyv)
        dz = w * (dw - jnp.sum(w * dw, axis=1, keepdims=True))
        db_ref[0] += dz
        dzs = (dz * scale).astype(BF16)
        dq_ref[...] = _dot_nn(dzs, kb).astype(dq_ref.dtype)
        dkp[band, :] += _dot_tn(dzs, q)

        @pl.when(c == nc - 1)
        def _():
            dk_ref[...] = dkp[pl.ds(CA_PAD, s), :].astype(dk_ref.dtype)
            dv_ref[...] = dvp[pl.ds(CA_PAD, s), :].astype(dv_ref.dtype)

    q_spec, k_spec, v_spec, b_spec = _ca_specs(n_heads, s, col0)
    blk = pl.BlockSpec((CA_ROWS, HEAD_DIM), lambda h, c: (c, h))
    full = pl.BlockSpec((s, HEAD_DIM), lambda h, c: (0, h))
    sd = jax.ShapeDtypeStruct((s, n_heads * HEAD_DIM), BF16)
    return _call(
        body, [qkv, qkv, qkv, bias, dy], name=name, grid=(n_heads, nc),
        in_specs=[q_spec, k_spec, v_spec, b_spec, blk],
        out_specs=[blk, full, full, b_spec],
        out_shape=[sd, sd, sd, jax.ShapeDtypeStruct((n_heads, CA_ROWS, CA_BAND), F32)],
        scratch_shapes=[pltpu.VMEM((s + CA_PAD, HEAD_DIM), BF16), pltpu.VMEM((s + CA_PAD, HEAD_DIM), BF16),
                        pltpu.VMEM((s + CA_PAD, HEAD_DIM), F32), pltpu.VMEM((s + CA_PAD, HEAD_DIM), F32)],
        sem=("parallel", "arbitrary"), ride=ride)


EARLY = ("w_sb_out", "w_ca_out", "w_mix_out")


def _step(x, p, target, small, comm):
    w = comm.w
    d = x.shape[1]
    n_sb = w["w_sb_out"].shape[0] // HEAD_DIM
    n_ca = w["w_ca_out"].shape[0] // HEAD_DIM
    qkv_cols = 3 * HEAD_DIM * (n_sb + n_ca)
    ca_col0 = 3 * n_sb
    both = (F32, BF16)

    h1 = _rms_fwd(x, small["g_mix"], name="rms_mix")
    ffn, ple = ("w_ffn_in",), ("w_ple_gate", "w_ple_in")
    qkv = _mm(h1, w["w_in"], "nn", (BF16,), name="proj_qkv", n=qkv_cols, ride=comm.gather(EARLY, "near"))
    gates = _mm(h1, w["w_in"], "nn", (F32,), name="proj_gates", n=2 * d, b_col_off=qkv_cols,
                ride=comm.gather(ffn, "near", comm.gather(EARLY, "far"), (0, 8)))
    bias = _group_bias(_band_bias(small["rel_bias"]))
    y_sb = _sb_fwd(qkv, n_sb, 0, name="sb_fwd", ride=comm.gather(ffn, "near", comm.gather(EARLY, "pair"), (1, 8, 7)))
    y_ca = _ca_fwd(qkv, bias, n_ca, ca_col0, name="ca_fwd", ride=comm.gather(ffn, "far"))
    out = ("w_ffn_out",)
    o_sb = _mm(y_sb, w["w_sb_out"], "nn", (F32,), name="sb_out", ride=comm.gather(out, "near", part=(0, 4)))
    o_ca = _mm(y_ca, w["w_ca_out"], "nn", (F32,), name="ca_out", ride=comm.gather(out, "near", part=(1, 4)))
    merged = _gate_merge_fwd(gates, o_sb, o_ca, name="gate_merge",
                             ride=comm.gather(out, "near", comm.gather(ffn, "pair"), (2, 4)))
    x1 = _mm(merged, w["w_mix_out"], "nn", (F32,), name="mix_out", resid=x, ride=comm.gather(out, "near", part=(3, 4)))
    h2 = _rms_fwd(x1, small["g_ffn"], name="rms_ffn")
    gu = _mm(h2, w["w_ffn_in"], "nn", (BF16,), name="ffn_in", ride=comm.gather(ple, "near", comm.gather(out, "far")))
    act = _swiglu_fwd(gu, name="swiglu", ride=comm.gather(ple, "far", comm.gather(out, "pair")))
    x2 = _mm(act, w["w_ffn_out"], "nn", (F32,), name="ffn_out", resid=x1, ride=comm.gather(ple, "pair"))
    h3 = _rms_fwd(x2, small["g_ple"], name="rms_ple")
    t = _mm(h3, w["w_ple_gate"], "nn", (F32,), name="ple_gate")
    pe = _mm(p, w["w_ple_in"], "nn", (F32,), name="ple_in")
    x3 = _ple_fwd(x2, t, pe, name="ple_add")

    def halves(n, acts, dout, ride, name):
        if comm.pos is None:
            return comm.grad(n, *_mm(acts, dout, "tn", both, name=name))
        g16 = _mm(acts, dout, "tn", (BF16,), name=name + "_other", m_half=(False, comm.pos), ride=ride)
        comm.grad(n, None, g16, half=True)
        g32 = _mm(acts, dout, "tn", (F32,), name=name + "_own", m_half=(True, comm.pos), ride=comm.pair((n,)))
        comm.grad(n, g32, g16, half=True)

    gs = {}
    dx3, gs["g_final"], loss = _final_loss(x3, small["g_final"], target, name="final_loss")
    dt, dpe = _ple_bwd(dx3, t, pe, name="ple_bwd")
    comm.grad("w_ple_in", *_mm(p, dpe, "tn", both, name="dw_ple_in"))
    comm.grad("w_ple_gate", *_mm(h3, dt, "tn", both, name="dw_ple_gate"))
    ple = ("w_ple_in", "w_ple_gate")
    dh3 = _mm(dt, w["w_ple_gate"], "nt", (F32,), name="dh_ple", ride=comm.pair(ple))
    dx2, dx2_16, gs["g_ple"] = _rms_bwd(x2, small["g_ple"], dh3, dx3, name="rms_ple_bwd")
    comm.add(ple)
    comm.grad("w_ffn_out", *_mm(act, dx2_16, "tn", both, name="dw_ffn_out", ride=comm.chips(ple)))
    dact = _mm(dx2_16, w["w_ffn_out"], "nt", (F32,), name="dact", ride=comm.pair(("w_ffn_out",)))
    dgu = _swiglu_bwd(dact, gu, name="swiglu_bwd")
    comm.sum(ple)
    comm.add(("w_ffn_out",))
    halves("w_ffn_in", h2, dgu, comm.share(ple, comm.chips(("w_ffn_out",))), "dw_ffn_in")
    dh2 = _mm(dgu, w["w_ffn_in"], "nt", (F32,), name="dh_ffn")
    dx1, dx1_16, gs["g_ffn"] = _rms_bwd(x1, small["g_ffn"], dh2, dx2, name="rms_ffn_bwd")
    comm.add(("w_ffn_in",))
    comm.sum(("w_ffn_out",))
    comm.grad("w_mix_out", *_mm(merged, dx1_16, "tn", both, name="dw_mix_out", ride=comm.share(("w_ffn_out",))))
    dmerged = _mm(dx1_16, w["w_mix_out"], "nt", (F32,), name="dmerged", ride=comm.pair(("w_mix_out",)))
    dg_sb, dg_ca, do_sb, do_ca = _gate_merge_bwd(dmerged, gates, o_sb, o_ca, name="gate_merge_bwd")
    comm.add(("w_mix_out",))
    comm.grad("w_sb_out", *_mm(y_sb, do_sb, "tn", both, name="dw_sb_out"))
    comm.grad("w_ca_out", *_mm(y_ca, do_ca, "tn", both, name="dw_ca_out"))
    outs = ("w_sb_out", "w_ca_out")
    dy_sb = _mm(do_sb, w["w_sb_out"], "nt", (BF16,), name="dy_sb", ride=comm.pair(outs))
    dy_ca = _mm(do_ca, w["w_ca_out"], "nt", (BF16,), name="dy_ca")
    comm.add(outs)
    dq_sb, dk_sb, dv_sb = _sb_bwd(qkv, dy_sb, n_sb, 0, name="sb_bwd", ride=comm.chips(("w_ffn_in",)))
    comm.sum(("w_ffn_in",))
    late = ("w_mix_out",) + outs
    dq_ca, dk_ca, dv_ca, dbias = _ca_bwd(qkv, bias, dy_ca, n_ca, ca_col0, name="ca_bwd",
                                         ride=comm.chips(late, comm.share(("w_ffn_in",))))
    comm.sum(late)
    gs["rel_bias"] = _band_bias_grad(_group_bias_grad(dbias))
    dproj = _concat_cols([dq_sb, dk_sb, dv_sb, dq_ca, dk_ca, dv_ca, dg_sb, dg_ca], name="dproj")
    halves("w_in", h1, dproj, comm.share(late), "dw_in")
    comm.add(("w_in",))
    half = x.shape[0] // 2
    dh1 = _mm(dproj, w["w_in"], "nt", (F32,), name="dh_mix_top", rows=(0, half), ride=comm.tail(TAIL_SECOND))
    dh1 = _mm(dproj, w["w_in"], "nt", (F32,), name="dh_mix_bottom", rows=(half, half), onto=(dh1,),
              ride=comm.tail(TAIL_SECOND))
    grad_x, _, gs["g_mix"] = _rms_bwd(x, small["g_mix"], dh1, dx1, name="rms_mix_bwd", ride=comm.tail(TAIL_FIRST))
    return loss, grad_x, gs


def _position():
    x, y, c = lax.axis_index("x"), lax.axis_index("y"), lax.axis_index("c")
    chips = [(1 - x, y), (x, 1 - y), (1 - x, 1 - y)]
    return x, y, c, chips


def _aligned(v, m):
    return v if isinstance(v, int) else pl.multiple_of(v, m)


def _piece_dims(shape, axis):
    k, n = shape
    return (k // 2, n // N_CHIPS) if axis == 1 else (k // N_CHIPS // 2, n)


def _piece(ref, shape, axis, j, h, part=(0, 1)):
    pr, pc = _piece_dims(shape, axis)
    nr = pr // part[1] * (part[2] if len(part) > 2 else 1)
    r0 = part[0] * (pr // part[1])
    if axis == 1:
        return ref.at[pl.ds(_aligned(h * pr + r0, 16), nr), pl.ds(_aligned(j * pc, 128), pc)]
    return ref.at[pl.ds(_aligned((2 * j + h) * pr + r0, 16), nr), :]


def _shard_half(ref, h):
    rows = ref.shape[0] // 2
    return ref.at[pl.ds(_aligned(h * rows, 16), rows), :]


def _remote(src, dst, send_sems, recv_sems, k, to):
    return pltpu.make_async_remote_copy(src_ref=src, dst_ref=dst, send_sem=send_sems.at[k],
                                        recv_sem=recv_sems.at[k], device_id=to, device_id_type=MESH)


def _prefetch_call(body, scalars, ins, in_specs, out_shape, out_specs, grid, *, name, ride=None):
    single = not isinstance(out_shape, (list, tuple))
    outs = _call(body, ins, name=name, grid=grid, in_specs=in_specs,
                 out_specs=[out_specs] if single else out_specs, out_shape=[out_shape] if single else out_shape,
                 sem=("parallel",) * len(grid), ride=ride, scalars=scalars)
    return outs[0] if single else outs


def _slab_tiles(pr, pc):
    tc = pc if pc <= 4096 else _pick(pc, (2048, 1024, 512, 256, 128))
    tr = next(t for t in (1024, 512, 256, 128, 64, 32, 16) if pr % t == 0 and t * tc <= 512 * 1024)
    return tr, tc


def _cast_place(w, axis, pos, *, name, ride=None):
    ks, ns = w.shape
    shape = (ks, ns * N_CHIPS) if axis == 1 else (ks * N_CHIPS, ns)
    tr, tc = _slab_tiles(ks, ns)
    nr, nc = ks // tr, ns // tc

    def body(pos_ref, w_ref, o_ref):
        o_ref[...] = w_ref[...].astype(o_ref.dtype)

    if axis == 1:
        out_map = lambda i, j, pos_ref: (i, pos_ref[0] * nc + j)
    else:
        out_map = lambda i, j, pos_ref: (pos_ref[0] * nr + i, j)
    return _prefetch_call(body, pos, [w], [pl.BlockSpec((tr, tc), lambda i, j, pos_ref: (i, j))],
                          jax.ShapeDtypeStruct(shape, BF16), pl.BlockSpec((tr, tc), out_map), (nr, nc), name=name, ride=ride)


def _run(ride, *, name):
    if ride is None:
        return

    def body(o_ref):
        o_ref[...] = jnp.zeros_like(o_ref)

    _call(body, [], name=name, grid=(1,), in_specs=[], out_specs=[pl.BlockSpec((8, 128), lambda i: (0, 0))],
          out_shape=[jax.ShapeDtypeStruct((8, 128), F32)], ride=ride)


def _ride_gather(ride, w, n, axis, stage, part=(0, 1)):
    shape = w[n].shape
    piece = functools.partial(_piece, shape=shape, axis=axis)
    span = part[2] if len(part) > 2 else 1
    halves = [(2 * part[0] + t * span, 2 * part[1], span) for t in range(2)]

    def copies(ins, outs, send_sems, recv_sems, arriving):
        x, y, c, chips = _position()
        me, (xn, yn, dn) = 2 * x + y, [2 * px + py for px, py in chips]
        if stage == "near":
            plan = [(me, c, part, (1 - x, y, c), xn, c, part), (me, c, part, (x, 1 - y, c), yn, c, part)]
        elif stage == "far":
            plan = [(yn, c, halves[1], (1 - x, y, c), dn, c, halves[1]), (xn, c, halves[0], (x, 1 - y, c), dn, c, halves[0])]
        else:
            plan = [(j, c, part, (x, y, 1 - c), j, 1 - c, part) for j in (xn, yn, dn)]
        out = []
        for k, (chip, h, rows, to, from_chip, from_h, from_rows) in enumerate(plan):
            if arriving:
                lands = piece(outs[0], j=from_chip, h=from_h, part=from_rows)
                out.append(_remote(lands, lands, send_sems, recv_sems, k, to))
            else:
                out.append(_remote(piece(ins[0], j=chip, h=h, part=rows), piece(outs[0], j=chip, h=h, part=rows),
                                   send_sems, recv_sems, k, to))
        return out

    def start(*refs):
        for cp in copies(*refs, arriving=False):
            cp.start()

    def finish(*refs):
        for cp in copies(*refs, arriving=True):
            cp.wait_recv()
        for cp in copies(*refs, arriving=False):
            cp.wait_send()

    ride.add([w[n]], [jax.ShapeDtypeStruct(shape, w[n].dtype)], {0: 0}, 3, start, finish,
             lambda outs: w.__setitem__(n, outs[0]))


def _ride_pair(ride, st, axis):
    shape = st["g16"].shape
    pr, pc = (shape[0], shape[1] // N_CHIPS) if st.get("half") else _piece_dims(shape, axis)

    def copies(ins, outs, send_sems, recv_sems):
        x, y, c, _ = _position()
        if st.get("half"):
            pieces = [ins[0].at[:, pl.ds(j * pc, pc)] for j in range(N_CHIPS)]
        else:
            pieces = [_piece(ins[0], shape, axis, j, 1 - c) for j in range(N_CHIPS)]
        return [_remote(pieces[j], outs[0].at[j], send_sems, recv_sems, j, (x, y, 1 - c)) for j in range(N_CHIPS)]

    def start(*refs):
        for cp in copies(*refs):
            cp.start()

    def finish(*refs):
        for cp in copies(*refs):
            cp.wait()

    ride.add([st["g16"]], [jax.ShapeDtypeStruct((N_CHIPS, pr, pc), BF16)], {}, N_CHIPS, start, finish,
             lambda outs: st.__setitem__("sib", outs[0]))


def _ride_chips(ride, st, rows=None):
    _, pr, pc = st["s16"].shape
    r0, nr = (0, pr) if rows is None else rows

    def copies(ins, outs, send_sems, recv_sems):
        x, y, c, chips = _position()
        return [_remote(ins[0].at[2 * px + py, pl.ds(r0, nr), :], outs[0].at[k, pl.ds(r0, nr), :],
                        send_sems, recv_sems, k, (px, py, c)) for k, (px, py) in enumerate(chips)]

    def start(*refs):
        for cp in copies(*refs):
            cp.start()

    def finish(*refs):
        for cp in copies(*refs):
            cp.wait()

    ins, aliases = ([st["s16"], st["recv"]], {1: 0}) if "recv" in st else ([st["s16"]], {})
    ride.add(ins, [jax.ShapeDtypeStruct((3, pr, pc), BF16)], aliases, 3, start, finish,
             lambda outs: st.__setitem__("recv", outs[0]))


def _ride_share(ride, st):
    def sent(ins, outs, send_sems, recv_sems):
        x, y, c, _ = _position()
        return _remote(_shard_half(ins[0], c), _shard_half(outs[0], c), send_sems, recv_sems, 0, (x, y, 1 - c))

    def landed(ins, outs, send_sems, recv_sems):
        x, y, c, _ = _position()
        other = _shard_half(outs[0], 1 - c)
        return _remote(other, other, send_sems, recv_sems, 0, (x, y, 1 - c))

    def start(*refs):
        sent(*refs).start()

    def finish(*refs):
        landed(*refs).wait_recv()
        sent(*refs).wait_send()

    ride.add([st["shard"]], [jax.ShapeDtypeStruct(st["shard"].shape, F32)], {0: 0}, 1, start, finish,
             lambda outs: st.__setitem__("g", outs[0]))


def _piece_block(axis, nr, nc, chip, half=False):
    if half:
        return lambda *a: (a[-3], (a[0] if chip is None else chip(a[-1])) * nc + a[-2])
    if axis == 1:
        return lambda *a: ((a[-1][1] * nr + a[-3]), (a[0] if chip is None else chip(a[-1])) * nc + a[-2])
    return lambda *a: ((2 * (a[0] if chip is None else chip(a[-1])) + a[-1][1]) * nr + a[-3], a[-2])


def _pair_add(g32, sib, axis, pos, *, name, half=False):
    _, pr, pc = sib.shape
    tr, tc = _slab_tiles(pr, pc)
    nr, nc = pr // tr, pc // tc

    def body(pos_ref, g_ref, b_ref, o16_ref):
        o16_ref[0] = (g_ref[...] + b_ref[0].astype(F32)).astype(o16_ref.dtype)

    blk = pl.BlockSpec((1, tr, tc), lambda j, i, k, pos_ref: (j, i, k))
    return _prefetch_call(body, pos, [g32, sib], [pl.BlockSpec((tr, tc), _piece_block(axis, nr, nc, None, half)), blk],
                          jax.ShapeDtypeStruct(sib.shape, BF16), blk, (N_CHIPS, nr, nc), name=name)


def _chip_sum(g32, sib, recv, axis, pos, *, name, half=False):
    _, pr, pc = sib.shape
    tr, tc = _slab_tiles(pr, pc)
    nr, nc = pr // tr, pc // tc

    def body(pos_ref, g_ref, b_ref, r_ref, o_ref):
        pair = g_ref[...] + b_ref[0].astype(F32)
        o_ref[...] = ((pair + r_ref[0].astype(F32)) + r_ref[1].astype(F32)) + r_ref[2].astype(F32)

    return _prefetch_call(
        body, pos, [g32, sib, recv],
        [pl.BlockSpec((tr, tc), _piece_block(axis, nr, nc, lambda pos_ref: pos_ref[0], half)),
         pl.BlockSpec((1, tr, tc), lambda i, k, pos_ref: (pos_ref[0], i, k)),
         pl.BlockSpec((3, tr, tc), lambda i, k, pos_ref: (0, i, k))],
        jax.ShapeDtypeStruct((2 * pr, pc), F32),
        pl.BlockSpec((tr, tc), lambda i, k, pos_ref: (pos_ref[1] * nr + i, k)), (nr, nc), name=name)


class _Comm:
    def __init__(self, pos, w):
        self.pos, self.w, self.st = pos, w, {n: {} for n, _ in BIG}

    def gather(self, names, stage, ride=None, part=(0, 1)):
        ride = _Ride() if ride is None else ride
        for n in names:
            _ride_gather(ride, self.w, n, AXIS[n], stage, part)
        return ride

    def grad(self, n, g32, g16, half=False):
        self.st[n].update(g32=g32, g16=g16, half=half)

    def pair(self, names, ride=None):
        ride = _Ride() if ride is None else ride
        for n in names:
            _ride_pair(ride, self.st[n], AXIS[n])
        return ride

    def add(self, names):
        for n in names:
            st = self.st[n]
            st["s16"] = _pair_add(st["g32"], st["sib"], AXIS[n], self.pos, name="rs_add_" + n, half=st["half"])

    def chips(self, names, ride=None, rows=None):
        ride = _Ride() if ride is None else ride
        for n in names:
            _ride_chips(ride, self.st[n], rows)
        return ride

    def sum(self, names):
        for n in names:
            st = self.st[n]
            st["shard"] = _chip_sum(st["g32"], st["sib"], st["recv"], AXIS[n], self.pos, name="rs_sum_" + n,
                                    half=st["half"])

    def share(self, names, ride=None):
        ride = _Ride() if ride is None else ride
        for n in names:
            _ride_share(ride, self.st[n])
        return ride

    def tail(self, count):
        st = self.st["w_in"]
        rows, at = st["s16"].shape[1], st.get("at", 0)
        st["at"] = at + count
        return self.chips(("w_in",), rows=(at * rows // TAIL_PARTS, count * rows // TAIL_PARTS))

    def tail_rest(self):
        return self.tail(TAIL_PARTS - self.st["w_in"].get("at", 0))

    def result(self, n):
        return self.st[n]["g"]


class _NoComm:
    pos = None

    def __init__(self, w):
        self.w, self.st = w, {}

    def grad(self, n, g32, g16, half=False):
        self.st[n] = (g32, g16)

    def result(self, n):
        return self.st[n]

    def add(self, names):
        pass

    sum = add

    def gather(self, names, *args, **kwargs):
        return None

    pair = chips = share = tail = gather


def _small_all_reduce(vec, *, name):
    r = vec.shape[0]

    def body(vec_ref, out_ref, slots, send_sems, recv_sems):
        x, y, c, _ = _position()
        me = 4 * x + 2 * y + c
        slots[me] = vec_ref[...]
        sends = []
        for k in range(1, 8):
            to = (x ^ (k >> 2), y ^ ((k >> 1) & 1), c ^ (k & 1))
            cp = _remote(slots.at[me], slots.at[me], send_sems, recv_sems, k - 1, to)
            cp.start()
            sends.append(cp)
        for k in range(1, 8):
            frm = 4 * (x ^ (k >> 2)) + 2 * (y ^ ((k >> 1) & 1)) + (c ^ (k & 1))
            _remote(slots.at[frm], slots.at[frm], send_sems, recv_sems, k - 1, (x, y, c)).wait_recv()
        for cp in sends:
            cp.wait_send()
        total = slots[0]
        for d in range(1, 8):
            total = total + slots[d]
        out_ref[...] = total

    return pl.pallas_call(
        body, name=name,
        in_specs=[pl.BlockSpec(memory_space=pltpu.VMEM)], out_specs=pl.BlockSpec(memory_space=pltpu.VMEM),
        out_shape=jax.ShapeDtypeStruct((r, 128), F32),
        scratch_shapes=[pltpu.VMEM((8, r, 128), F32), pltpu.SemaphoreType.DMA((7,)), pltpu.SemaphoreType.DMA((7,))],
    )(vec)


SC_TILES = 32
SC_LANES = 16
SC_TILE_BUDGET = 400 * 1024


def _adamw_update(wv, gv, mv, vv):
    nm = ADAM_B1 * mv + (1.0 - ADAM_B1) * gv
    nv = ADAM_B2 * vv + (1.0 - ADAM_B2) * (gv * gv)
    m_hat = nm / (1.0 - ADAM_B1 ** ADAM_STEP)
    v_hat = nv / (1.0 - ADAM_B2 ** ADAM_STEP)
    return -ADAM_LR * (m_hat / (jnp.sqrt(v_hat) + ADAM_EPS) + ADAM_WD * wv), nm, nv


def _adamw_sc(w, g, m, v, *, name):
    r, c = w.shape
    groups = r // 8
    per_tile = -(-groups // SC_TILES)
    cb = c if 4 * 8 * c * 4 <= SC_TILE_BUDGET else _pick(c, (2048, 1024, 512, 256, 128))

    def body(w_hbm, g_hbm, m_hbm, v_hbm, go_hbm, d_hbm, nm_hbm, nv_hbm, wb, gb, mb, vb):
        tile = lax.axis_index("sc_tile") * 2 + lax.axis_index("sc_core")

        def update(group):
            for c0 in range(0, c, cb):
                at = (pl.ds(group * 8, 8), pl.ds(c0, cb))
                for hbm, buf in ((w_hbm, wb), (g_hbm, gb), (m_hbm, mb), (v_hbm, vb)):
                    pltpu.sync_copy(hbm.at[at], buf)
                pltpu.sync_copy(gb, go_hbm.at[at])

                @pl.loop(0, 8)
                def _(rr):
                    @pl.loop(0, cb, step=SC_LANES)
                    def _(i):
                        lanes = (rr, pl.ds(i, SC_LANES))
                        wb[lanes], mb[lanes], vb[lanes] = _adamw_update(wb[lanes], gb[lanes], mb[lanes], vb[lanes])

                for buf, hbm in ((wb, d_hbm), (mb, nm_hbm), (vb, nv_hbm)):
                    pltpu.sync_copy(buf, hbm.at[at])

        @pl.loop(0, per_tile)
        def _(k):
            group = k * SC_TILES + tile
            if groups % SC_TILES:
                pl.when(group < groups)(lambda: update(group))
            else:
                update(group)

    sd = jax.ShapeDtypeStruct((r, c), F32)
    return pl.kernel(body, name=name, out_type=[sd, sd, sd, sd],
                     mesh=plsc.VectorSubcoreMesh(core_axis_name="sc_core", subcore_axis_name="sc_tile"),
                     scratch_types=[pltpu.VMEM((8, cb), F32)] * 4)(w, g, m, v)


def _adamw(w, g, m, v, *, name, ride=None):
    r, c = w.shape
    tc = c if c <= 4096 else _pick(c, (2048, 1024, 512, 256, 128))
    tr = next(t for t in (512, 256, 128, 64, 32, 16, 8) if r % t == 0 and t * tc <= 256 * 1024)

    def body(w_ref, g_ref, m_ref, v_ref, go_ref, d_ref, nm_ref, nv_ref):
        go_ref[...] = g_ref[...]
        d_ref[...], nm_ref[...], nv_ref[...] = _adamw_update(w_ref[...], g_ref[...], m_ref[...], v_ref[...])

    blk = ((tr, tc), lambda i, j: (i, j))
    sd = jax.ShapeDtypeStruct((r, c), F32)
    return _ew(body, [w, g, m, v], [blk] * 4, [sd] * 4, [blk] * 4, (r // tr, c // tc), name=name, ride=ride)


BIG = (("w_in", 1), ("w_sb_out", 1), ("w_ca_out", 1), ("w_mix_out", 0), ("w_ffn_in", 1), ("w_ffn_out", 0),
       ("w_ple_in", 1), ("w_ple_gate", 0))
AXIS = dict(BIG)
HEAD_PARTS = 8
HEAD_HOSTS = ("w_ffn_in", "w_ffn_out")
TAIL_PARTS = 16
TAIL_SECOND = 4
TAIL_FIRST = 2
ON_SPARSECORE = tuple(n for n, _ in BIG if n != "w_in")
SMALL = ("rel_bias", "g_mix", "g_ffn", "g_ple", "g_final")
ORDER = ("w_in", "w_sb_out", "w_ca_out", "w_mix_out", "rel_bias", "g_mix", "g_ffn", "g_ple", "g_final",
         "w_ffn_in", "w_ffn_out", "w_ple_in", "w_ple_gate")


def _pack(parts):
    flat = jnp.concatenate([a.reshape(-1) for a in parts])
    rows = -(-flat.shape[0] // 1024) * 8
    return jnp.pad(flat, (0, rows * 128 - flat.shape[0])).reshape(rows, 128)


def _unpack(packed, like):
    flat, out, at = packed.reshape(-1), [], 0
    for a in like:
        out.append(flat[at:at + a.size].reshape(a.shape))
        at += a.size
    return out


def kernel(x, p, w_in, w_sb_out, w_ca_out, w_mix_out, rel_bias, g_mix, g_ffn, g_ple, g_final, w_ffn_in, w_ffn_out, w_ple_in, w_ple_gate, loss_target, m_w_in, m_w_sb_out, m_w_ca_out, m_w_mix_out, m_rel_bias, m_g_mix, m_g_ffn, m_g_ple, m_g_final, m_w_ffn_in, m_w_ffn_out, m_w_ple_in, m_w_ple_gate, v_w_in, v_w_sb_out, v_w_ca_out, v_w_mix_out, v_rel_bias, v_g_mix, v_g_ffn, v_g_ple, v_g_final, v_w_ffn_in, v_w_ffn_out, v_w_ple_in, v_w_ple_gate):
    weights = dict(w_in=w_in, w_sb_out=w_sb_out, w_ca_out=w_ca_out, w_mix_out=w_mix_out, rel_bias=rel_bias,
                   g_mix=g_mix, g_ffn=g_ffn, g_ple=g_ple, g_final=g_final, w_ffn_in=w_ffn_in,
                   w_ffn_out=w_ffn_out, w_ple_in=w_ple_in, w_ple_gate=w_ple_gate)
    m_in = dict(w_in=m_w_in, w_sb_out=m_w_sb_out, w_ca_out=m_w_ca_out, w_mix_out=m_w_mix_out, rel_bias=m_rel_bias,
                g_mix=m_g_mix, g_ffn=m_g_ffn, g_ple=m_g_ple, g_final=m_g_final, w_ffn_in=m_w_ffn_in,
                w_ffn_out=m_w_ffn_out, w_ple_in=m_w_ple_in, w_ple_gate=m_w_ple_gate)
    v_in = dict(w_in=v_w_in, w_sb_out=v_w_sb_out, w_ca_out=v_w_ca_out, w_mix_out=v_w_mix_out, rel_bias=v_rel_bias,
                g_mix=v_g_mix, g_ffn=v_g_ffn, g_ple=v_g_ple, g_final=v_g_final, w_ffn_in=v_w_ffn_in,
                w_ffn_out=v_w_ffn_out, w_ple_in=v_w_ple_in, w_ple_gate=v_w_ple_gate)

    pos = jnp.stack([2 * lax.axis_index("x") + lax.axis_index("y"), lax.axis_index("c")]).astype(jnp.int32)
    comm = _Comm(pos, {"w_in": _cast_place(w_in[0], AXIS["w_in"], pos, name="cast_w_in")})
    at = 0
    for n in HEAD_HOSTS:
        ride = comm.gather(("w_in",), "near", part=(at, HEAD_PARTS))
        comm.w[n] = _cast_place(weights[n][0], AXIS[n], pos, name="cast_" + n, ride=ride)
        at += 1
    for n, axis in BIG:
        if n not in comm.w:
            comm.w[n] = _cast_place(weights[n][0], axis, pos, name="cast_" + n)
    _run(comm.gather(("w_in",), "near", part=(at, HEAD_PARTS, HEAD_PARTS - at)), name="gather_w_in_near")
    _run(comm.gather(("w_in",), "far"), name="gather_w_in_far")
    _run(comm.gather(("w_in",), "pair"), name="gather_w_in_pair")
    small = dict(rel_bias=rel_bias[0], g_mix=g_mix, g_ffn=g_ffn, g_ple=g_ple, g_final=g_final.reshape(1, -1))
    loss, grad_x, gs = _step(x[0], p[0, 0], loss_target[0], small, comm)

    grads, delta, new_m, new_v = {}, {}, {}, {}
    for n in [n for n, _ in BIG if n != "w_in"] + ["w_in"]:
        if n == "w_in":
            _run(comm.tail_rest(), name="rs_chips_w_in")
            comm.sum(("w_in",))
            _run(comm.share(("w_in",)), name="rs_share_w_in")
        update = _adamw_sc if n in ON_SPARSECORE else _adamw
        g, d, nm, nv = update(weights[n][0], comm.result(n), m_in[n][0], v_in[n][0], name="adamw_" + n)
        grads[n], delta[n], new_m[n], new_v[n] = g[None], d[None], nm[None], nv[None]

    like = [weights[n] for n in SMALL]
    reduced = _small_all_reduce(_pack([gs[n] for n in SMALL] + [loss[:, :1]]), name="small_all_reduce")
    g_small = _unpack(reduced, like + [loss[:, :1]])
    total_loss = g_small[-1].reshape(())
    g_packed = _pack(g_small[:-1])
    _, d_s, m_s, v_s = _adamw(_pack(like), g_packed, _pack([m_in[n] for n in SMALL]), _pack([v_in[n] for n in SMALL]),
                           name="adamw_small")
    for n, g, d, nm, nv in zip(SMALL, g_small[:-1], _unpack(d_s, like), _unpack(m_s, like), _unpack(v_s, like)):
        grads[n], delta[n], new_m[n], new_v[n] = g, d, nm, nv

    return (total_loss, grad_x[None], *[grads[n] for n in ORDER], *[delta[n] for n in ORDER],
            *[new_m[n] for n in ORDER], *[new_v[n] for n in ORDER])
```

```python
import functools
import math

import jax
import jax.numpy as jnp
import numpy as np
from jax import lax
from jax.experimental import pallas as pl
from jax.experimental.pallas import tpu as pltpu
from jax.experimental.pallas import tpu_sc as plsc

F32 = jnp.float32
BF16 = jnp.bfloat16

HEAD_DIM = 128
CHUNK = 64
LEFT_CHUNKS = 8
REL_CLIP = 128
N_REL = REL_CLIP + CHUNK
BAND = (LEFT_CHUNKS + 2) * CHUNK
CA_PER_STEP = 4
CA_ROWS = CA_PER_STEP * CHUNK
CA_BAND = BAND + CA_PER_STEP * CHUNK
CA_PAD = BAND
SB_BLOCK = 128
SB_KEYS = 512
SB_GROUPS = SB_KEYS // SB_BLOCK
SB_ROWS = SB_KEYS
EPS = 1e-6
NEG = -1e30

ADAM_LR = 0.001
ADAM_B1 = 0.9
ADAM_B2 = 0.999
ADAM_EPS = 1e-08
ADAM_WD = 0.01
ADAM_STEP = 10

VMEM_LIMIT = 48 * 1024 * 1024
MM_VMEM_BUDGET = 36 * 1024 * 1024
V7X_HBM_BYTES_PER_S = 3.7e12
GRID_STEP_S = 0.35e-6
MESH = pl.DeviceIdType.MESH
N_CHIPS = 4


def _pick(dim, prefs):
    for t in prefs:
        if dim % t == 0:
            return t
    raise ValueError(f"no tile for {dim}")


def _cparams(sem=None):
    return pltpu.CompilerParams(dimension_semantics=sem, vmem_limit_bytes=VMEM_LIMIT)


def _sigmoid(v):
    return 1.0 / (1.0 + jnp.exp(-v))


def _dot(a, b, dims):
    return lax.dot_general(a, b, (dims, ((), ())), preferred_element_type=F32)


def _dot_nn(a, b):
    return _dot(a, b, ((1,), (0,)))


def _dot_nt(a, b):
    return _dot(a, b, ((1,), (1,)))


def _dot_tn(a, b):
    return _dot(a, b, ((0,), (0,)))


HBM = pl.BlockSpec(memory_space=pltpu.HBM)


class _Ride:
    def __init__(self):
        self.items = []

    def add(self, ins, outs, aliases, n_sems, start, finish, sink):
        self.items.append((ins, outs, aliases, n_sems, start, finish, sink))


def _call(body, args, *, name, grid, in_specs, out_specs, out_shape, scratch_shapes=(), sem=None, ride=None,
          scalars=None, onto=()):
    items = ride.items if ride is not None else []
    if onto:
        args, in_specs = list(args) + list(onto), list(in_specs) + [HBM] * len(onto)
        inner, body = body, lambda *refs: inner(*refs[:len(args) - len(onto)], *refs[len(args):])
    n_in, n_out, n_scr = len(args), len(out_shape), len(scratch_shapes)
    r_ins = [a for it in items for a in it[0]]
    r_outs = [o for it in items for o in it[1]]
    updated = [id(it[0][i]) for it in items for i in it[2]]
    assert len(set(updated)) == len(updated), "one call may update a buffer in place only once"
    aliases, a, b = {n_in - len(onto) + t: t for t in range(len(onto))}, n_in, n_out
    for it in items:
        aliases.update({a + i: b + o for i, o in it[2].items()})
        a, b = a + len(it[0]), b + len(it[1])
    sems = [pltpu.SemaphoreType.DMA((it[3],)) for it in items for _ in range(2)]

    def wrapped(*refs):
        head, refs = (refs[:1], refs[1:]) if scalars is not None else ((), refs)
        ins, rin = refs[:n_in], refs[n_in:n_in + len(r_ins)]
        at = n_in + len(r_ins)
        outs, rout = refs[at:at + n_out], refs[at + n_out:at + n_out + len(r_outs)]
        at += n_out + len(r_outs)
        scr, rsem = refs[at:at + n_scr], refs[at + n_scr:]

        def each(which):
            a = b = 0
            for q, it in enumerate(items):
                it[which](rin[a:a + len(it[0])], rout[b:b + len(it[1])], rsem[2 * q], rsem[2 * q + 1])
                a, b = a + len(it[0]), b + len(it[1])

        if items:
            ids = [pl.program_id(d) for d in range(len(grid))]
            first = functools.reduce(jnp.logical_and, [i == 0 for i in ids])
            last = functools.reduce(jnp.logical_and, [i == g - 1 for i, g in zip(ids, grid)])
            pl.when(first)(lambda: each(4))
        body(*head, *ins, *outs, *scr)
        if items:
            pl.when(last)(lambda: each(5))

    specs = dict(grid=grid, in_specs=list(in_specs) + [HBM] * len(r_ins),
                 out_specs=list(out_specs) + [HBM] * len(r_outs), scratch_shapes=list(scratch_shapes) + sems)
    if scalars is not None:
        specs = dict(grid_spec=pltpu.PrefetchScalarGridSpec(num_scalar_prefetch=1, **specs))
        aliases = {i + 1: o for i, o in aliases.items()}
    res = pl.pallas_call(
        wrapped, name=name, **specs,
        out_shape=list(out_shape) + r_outs,
        input_output_aliases=aliases,
        compiler_params=_cparams(("arbitrary",) * len(grid) if items else sem),
    )(*(() if scalars is None else (scalars,)), *args, *r_ins)
    b = n_out
    for it in items:
        it[6](res[b:b + len(it[1])])
        b += len(it[1])
    return list(res[:n_out])


def _mm_tiles(m, n_align, n, k, a_bytes, b_bytes, out_bytes):
    best = None
    tks = sorted({t for t in (k, k // 2, k // 4, 2048, 1024, 512, 256, 128) if t <= k and k % t == 0 and t % 128 == 0})
    for tm in (t for t in (2048, 1024, 512, 256, 128) if m % t == 0):
        for tn in (t for t in (2048, 1024, 512, 256, 128) if n_align % t == 0):
            for tk in tks:
                nk = k // tk
                vmem = 2 * (tm * tk * a_bytes + tk * tn * b_bytes + tm * tn * out_bytes) + tm * tn * 4
                if vmem > MM_VMEM_BUDGET:
                    continue
                traffic = m * k * a_bytes * (n // tn if nk > 1 else 1) + k * n * b_bytes * (m // tm)
                traffic += tm * tk * a_bytes + tk * tn * b_bytes + tm * tn * out_bytes
                traffic += m * n * 4 * nk if nk > 1 else 0
                cost = traffic / V7X_HBM_BYTES_PER_S + (m // tm) * (n // tn) * nk * GRID_STEP_S
                if best is None or cost < best[0]:
                    best = (cost, tm, tn, tk)
    return best[1:]


def _mm(a, b, mode, out_dtypes, *, name, n=None, b_col_off=0, resid=None, ride=None, rows=None, onto=(), m_half=None):
    if mode == "nn":
        m, k = a.shape
        n = b.shape[1] if n is None else n
    elif mode == "nt":
        m, k = a.shape
        n = b.shape[0]
    else:
        k, m = a.shape
        n = b.shape[1]
    if m_half is not None:
        m //= 2
    m_all, (row0, m) = m, (0, m) if rows is None else rows
    n_out = len(out_dtypes)
    has_resid = resid is not None
    out_bytes = sum(jnp.dtype(dt).itemsize for dt in out_dtypes) + (4 if has_resid else 0)
    tm, tn, tk = _mm_tiles(math.gcd(m, row0) if row0 else m, math.gcd(n, b_col_off) if b_col_off else n, n, k,
                           a.dtype.itemsize, b.dtype.itemsize, out_bytes)
    nk = k // tk
    boff, roff = b_col_off // tn, row0 // tm
    dot = {"nn": _dot_nn, "nt": _dot_nt, "tn": _dot_tn}[mode]
    if m_half is None:
        half = lambda: 0
    else:
        half = lambda pos_ref: (pos_ref[1] if m_half[0] else 1 - pos_ref[1]) * (m // tm)

    def body(*refs):
        refs = refs[m_half is not None:]
        a_ref, b_ref = refs[0], refs[1]
        r_ref = refs[2] if has_resid else None
        o_refs = refs[2 + has_resid: 2 + has_resid + n_out]

        def finish(r):
            if has_resid:
                r = r + r_ref[...]
            for o_ref in o_refs:
                o_ref[...] = r.astype(o_ref.dtype)

        part = dot(a_ref[...].astype(BF16), b_ref[...].astype(BF16))
        if nk == 1:
            finish(part)
            return
        acc_ref = refs[-1]
        kk = pl.program_id(2)

        @pl.when(kk == 0)
        def _():
            acc_ref[...] = part

        @pl.when(kk > 0)
        def _():
            acc_ref[...] += part

        @pl.when(kk == nk - 1)
        def _():
            finish(acc_ref[...])

    if mode == "nn":
        a_spec = pl.BlockSpec((tm, tk), lambda i, j, kk, *_: (i + roff, kk))
        b_spec = pl.BlockSpec((tk, tn), lambda i, j, kk, *_: (kk, j + boff))
    elif mode == "nt":
        a_spec = pl.BlockSpec((tm, tk), lambda i, j, kk, *_: (i + roff, kk))
        b_spec = pl.BlockSpec((tn, tk), lambda i, j, kk, *_: (j, kk))
    else:
        a_spec = pl.BlockSpec((tk, tm), lambda i, j, kk, *pos: (kk, i + half(*pos)))
        b_spec = pl.BlockSpec((tk, tn), lambda i, j, kk, *_: (kk, j))
    o_spec = pl.BlockSpec((tm, tn), lambda i, j, kk, *_: (i + roff, j))
    in_specs = [a_spec, b_spec] + ([o_spec] if has_resid else [])
    args = [a, b] + ([resid] if has_resid else [])
    outs = _call(
        body, args, name=name,
        grid=(m // tm, n // tn, nk),
        in_specs=in_specs,
        out_specs=[o_spec] * n_out,
        out_shape=[jax.ShapeDtypeStruct((m_all, n), dt) for dt in out_dtypes],
        scratch_shapes=[pltpu.VMEM((tm, tn), F32)] if nk > 1 else [],
        sem=("parallel", "parallel", "arbitrary"), ride=ride, onto=onto,
        scalars=None if m_half is None else m_half[1])
    return outs[0] if n_out == 1 else tuple(outs)


def _row_tile(s):
    return _pick(s, (256, 128))


def _rms_fwd(x, g, *, name, ride=None):
    s, d = x.shape
    tr = _row_tile(s)

    def body(x_ref, g_ref, o_ref):
        xv = x_ref[...]
        r = lax.rsqrt(jnp.mean(xv * xv, axis=1, keepdims=True) + EPS)
        o_ref[...] = (xv * r * g_ref[...]).astype(o_ref.dtype)

    return _call(
        body, [x, g], name=name, grid=(s // tr,),
        in_specs=[pl.BlockSpec((tr, d), lambda i: (i, 0)), pl.BlockSpec((1, d), lambda i: (0, 0))],
        out_specs=[pl.BlockSpec((tr, d), lambda i: (i, 0))],
        out_shape=[jax.ShapeDtypeStruct((s, d), BF16)], sem=("parallel",), ride=ride)[0]


def _rms_bwd(x, g, dh, dres, *, name, ride=None):
    s, d = x.shape
    tr = _row_tile(s)

    def body(x_ref, g_ref, dh_ref, dres_ref, dx_ref, dx16_ref, dg_ref):
        i = pl.program_id(0)
        xv = x_ref[...]
        r = lax.rsqrt(jnp.mean(xv * xv, axis=1, keepdims=True) + EPS)
        xhat = xv * r
        dhv = dh_ref[...]
        dxhat = dhv * g_ref[...]
        proj = jnp.mean(dxhat * xhat, axis=1, keepdims=True)
        dx = dres_ref[...] + r * (dxhat - xhat * proj)
        dx_ref[...] = dx
        dx16_ref[...] = dx.astype(dx16_ref.dtype)

        @pl.when(i == 0)
        def _():
            dg_ref[...] = jnp.zeros_like(dg_ref)

        dg_ref[...] += jnp.sum(dhv * xhat, axis=0, keepdims=True)

    row = pl.BlockSpec((tr, d), lambda i: (i, 0))
    vec = pl.BlockSpec((1, d), lambda i: (0, 0))
    return _call(
        body, [x, g, dh, dres], name=name, grid=(s // tr,),
        in_specs=[row, vec, row, row],
        out_specs=[row, row, vec],
        out_shape=[jax.ShapeDtypeStruct((s, d), F32), jax.ShapeDtypeStruct((s, d), BF16),
                   jax.ShapeDtypeStruct((1, d), F32)],
        sem=("arbitrary",), ride=ride)


def _final_loss(x, g, target, *, name):
    s, d = x.shape
    tr = _row_tile(s)

    def body(x_ref, g_ref, t_ref, dx_ref, dg_ref, loss_ref):
        i = pl.program_id(0)
        xv = x_ref[...]
        gv = g_ref[...]
        r = lax.rsqrt(jnp.mean(xv * xv, axis=1, keepdims=True) + EPS)
        xhat = xv * r
        err = xhat * gv - t_ref[...]
        dy = err * (1.0 / d)
        dxhat = dy * gv
        proj = jnp.mean(dxhat * xhat, axis=1, keepdims=True)
        dx_ref[...] = r * (dxhat - xhat * proj)

        @pl.when(i == 0)
        def _():
            dg_ref[...] = jnp.zeros_like(dg_ref)
            loss_ref[...] = jnp.zeros_like(loss_ref)

        dg_ref[...] += jnp.sum(dy * xhat, axis=0, keepdims=True)
        part = 0.5 * jnp.sum(jnp.mean(err * err, axis=1, keepdims=True), axis=0, keepdims=True)
        loss_ref[...] += jnp.broadcast_to(part, loss_ref.shape)

    row = pl.BlockSpec((tr, d), lambda i: (i, 0))
    vec = pl.BlockSpec((1, d), lambda i: (0, 0))
    return pl.pallas_call(
        body, name=name, grid=(s // tr,),
        in_specs=[row, vec, row],
        out_specs=[row, vec, pl.BlockSpec((1, 128), lambda i: (0, 0))],
        out_shape=[jax.ShapeDtypeStruct((s, d), F32), jax.ShapeDtypeStruct((1, d), F32),
                   jax.ShapeDtypeStruct((1, 128), F32)],
        compiler_params=_cparams(("arbitrary",)),
    )(x, g, target)


def _ew(body, ins, in_blocks, outs, out_blocks, grid, *, name, ride=None):
    return _call(body, ins, name=name, grid=grid,
                 in_specs=[pl.BlockSpec(bs, im) for bs, im in in_blocks],
                 out_specs=[pl.BlockSpec(bs, im) for bs, im in out_blocks],
                 out_shape=outs, sem=("parallel",) * len(grid), ride=ride)


def _gate_merge_fwd(gates, o_sb, o_ca, *, name, ride=None):
    s, d = o_sb.shape
    tr, tc = _row_tile(s), _pick(d, (1024, 512, 256, 128))
    nc = d // tc

    def body(gs_ref, gc_ref, os_ref, oc_ref, m_ref):
        m = _sigmoid(gs_ref[...]) * os_ref[...] + _sigmoid(gc_ref[...]) * oc_ref[...]
        m_ref[...] = m.astype(m_ref.dtype)

    blk = ((tr, tc), lambda i, j: (i, j))
    return _ew(body, [gates, gates, o_sb, o_ca],
               [blk, ((tr, tc), lambda i, j: (i, j + nc)), blk, blk],
               [jax.ShapeDtypeStruct((s, d), BF16)], [blk], (s // tr, nc), name=name, ride=ride)[0]


def _gate_merge_bwd(dmerged, gates, o_sb, o_ca, *, name):
    s, d = o_sb.shape
    tr, tc = _row_tile(s), _pick(d, (1024, 512, 256, 128))
    nc = d // tc

    def body(dm_ref, gs_ref, gc_ref, os_ref, oc_ref, dgs_ref, dgc_ref, dos_ref, doc_ref):
        dm = dm_ref[...]
        ss = _sigmoid(gs_ref[...])
        sc = _sigmoid(gc_ref[...])
        dgs_ref[...] = (dm * os_ref[...] * ss * (1.0 - ss)).astype(dgs_ref.dtype)
        dgc_ref[...] = (dm * oc_ref[...] * sc * (1.0 - sc)).astype(dgc_ref.dtype)
        dos_ref[...] = (dm * ss).astype(dos_ref.dtype)
        doc_ref[...] = (dm * sc).astype(doc_ref.dtype)

    blk = ((tr, tc), lambda i, j: (i, j))
    sd = jax.ShapeDtypeStruct((s, d), BF16)
    return _ew(body, [dmerged, gates, gates, o_sb, o_ca],
               [blk, blk, ((tr, tc), lambda i, j: (i, j + nc)), blk, blk],
               [sd, sd, sd, sd], [blk, blk, blk, blk], (s // tr, nc), name=name)


def _swiglu_fwd(gu, *, name, ride=None):
    s, f2 = gu.shape
    f = f2 // 2
    tr, tc = 128, _pick(f, (512, 256, 128))

    def body(gu_ref, a_ref):
        for at in range(0, f, tc):
            gv = gu_ref[:, at:at + tc].astype(F32)
            a_ref[:, at:at + tc] = (gv * _sigmoid(gv) * gu_ref[:, f + at:f + at + tc].astype(F32)).astype(a_ref.dtype)

    row = lambda i: (i, 0)
    return _ew(body, [gu], [((tr, f2), row)], [jax.ShapeDtypeStruct((s, f), BF16)], [((tr, f), row)],
               (s // tr,), name=name, ride=ride)[0]


def _swiglu_bwd(dact, gu, *, name):
    s, f2 = gu.shape
    f = f2 // 2
    tr, tc = 128, _pick(f, (512, 256, 128))

    def body(da_ref, gu_ref, o_ref):
        for at in range(0, f, tc):
            da = da_ref[:, at:at + tc]
            gv = gu_ref[:, at:at + tc].astype(F32)
            sg = _sigmoid(gv)
            uv = gu_ref[:, f + at:f + at + tc].astype(F32)
            o_ref[:, at:at + tc] = (da * uv * sg * (1.0 + gv * (1.0 - sg))).astype(o_ref.dtype)
            o_ref[:, f + at:f + at + tc] = (da * gv * sg).astype(o_ref.dtype)

    row = lambda i: (i, 0)
    return _ew(body, [dact, gu], [((tr, f), row), ((tr, f2), row)], [jax.ShapeDtypeStruct((s, f2), BF16)],
               [((tr, f2), row)], (s // tr,), name=name)[0]


def _concat_cols(parts, *, name):
    s = parts[0].shape[0]
    widths = [p.shape[1] for p in parts]
    tr = 256

    def body(*refs):
        o_ref, at = refs[-1], 0
        for p_ref, width in zip(refs, widths):
            o_ref[:, at:at + width] = p_ref[...]
            at += width

    row = lambda i: (i, 0)
    return _ew(body, list(parts), [((tr, width), row) for width in widths],
               [jax.ShapeDtypeStruct((s, sum(widths)), parts[0].dtype)], [((tr, sum(widths)), row)],
               (s // tr,), name=name)[0]


def _ple_fwd(x, t, pe, *, name):
    s, d = x.shape
    tr, tc = _row_tile(s), _pick(d, (1024, 512, 256, 128))

    def body(x_ref, t_ref, p_ref, o_ref):
        o_ref[...] = x_ref[...] + _sigmoid(t_ref[...]) * p_ref[...]

    blk = ((tr, tc), lambda i, j: (i, j))
    return _ew(body, [x, t, pe], [blk, blk, blk],
               [jax.ShapeDtypeStruct((s, d), F32)], [blk], (s // tr, d // tc), name=name)[0]


def _ple_bwd(dx, t, pe, *, name):
    s, d = dx.shape
    tr, tc = _row_tile(s), _pick(d, (1024, 512, 256, 128))

    def body(dx_ref, t_ref, p_ref, dt_ref, dp_ref):
        dxv = dx_ref[...]
        sg = _sigmoid(t_ref[...])
        dt_ref[...] = (dxv * p_ref[...] * sg * (1.0 - sg)).astype(dt_ref.dtype)
        dp_ref[...] = (dxv * sg).astype(dp_ref.dtype)

    blk = ((tr, tc), lambda i, j: (i, j))
    sd = jax.ShapeDtypeStruct((s, d), BF16)
    return _ew(body, [dx, t, pe], [blk, blk, blk], [sd, sd], [blk, blk], (s // tr, d // tc), name=name)


def _sb_tri(later):
    row = lax.broadcasted_iota(jnp.int32, (SB_BLOCK, SB_BLOCK), 0)
    col = lax.broadcasted_iota(jnp.int32, (SB_BLOCK, SB_BLOCK), 1)
    tri = (row > col) if later else (row < col)
    return jnp.concatenate([tri.astype(BF16), jnp.ones((SB_BLOCK, SB_BLOCK), BF16)], axis=1)


def _sb_valid(i, j, own):
    if not own:
        return None
    qi = i * SB_ROWS + lax.broadcasted_iota(jnp.int32, (SB_ROWS, SB_KEYS), 0)
    ki = j * SB_KEYS + lax.broadcasted_iota(jnp.int32, (SB_ROWS, SB_KEYS), 1)
    return ki < qi


def _sb_scan(v, tri, run, later):
    hi = v.astype(BF16)
    lo = (v - hi.astype(F32)).astype(BF16)
    outs = [None] * SB_GROUPS
    for b in (reversed(range(SB_GROUPS)) if later else range(SB_GROUPS)):
        cols = slice(b * SB_BLOCK, (b + 1) * SB_BLOCK)
        r = _dot_nn(hi[:, cols], tri) + _dot_nn(lo[:, cols], tri)
        outs[b] = r[:, :SB_BLOCK] + run
        run = run + r[:, SB_BLOCK:]
    return jnp.concatenate(outs, axis=1), run


def _masked(valid, v):
    return v if valid is None else jnp.where(valid, v, 0.0)


def _sb_scores(q, kj, scale, valid):
    z = _dot_nt(q, kj) * scale
    t = jnp.log(1.0 + jnp.exp(-jnp.abs(z)))
    return jnp.minimum(z, 0.0) - t, _masked(valid, -jnp.maximum(z, 0.0) - t)


def _sb_specs(h_count, s, col0):
    q_spec = pl.BlockSpec((SB_ROWS, HEAD_DIM), lambda h, i: (i, col0 + h))
    k_spec = pl.BlockSpec((s, HEAD_DIM), lambda h, i: (0, col0 + h_count + h))
    v_spec = pl.BlockSpec((s, HEAD_DIM), lambda h, i: (0, col0 + 2 * h_count + h))
    return q_spec, k_spec, v_spec


def _sb_fwd(qkv, n_heads, col0, *, name, ride=None):
    s = qkv.shape[0]
    nq = s // SB_ROWS
    scale = HEAD_DIM ** -0.5

    def body(q_ref, k_ref, v_ref, o_ref):
        i = pl.program_id(1)
        q = q_ref[...]
        tri = _sb_tri(later=True)

        def step(j, carry, own):
            run, acc = carry
            off = pl.multiple_of(j * SB_KEYS, SB_KEYS)
            valid = _sb_valid(i, j, own)
            ls, lk = _sb_scores(q, k_ref[pl.ds(off, SB_KEYS), :], scale, valid)
            between, run = _sb_scan(lk, tri, run, later=True)
            a = _masked(valid, jnp.exp(ls + between))
            return run, acc + _dot_nn(a.astype(BF16), v_ref[pl.ds(off, SB_KEYS), :])

        carry = step(i, (jnp.zeros((SB_ROWS, SB_BLOCK), F32), jnp.zeros((SB_ROWS, HEAD_DIM), F32)), True)
        _, acc = lax.fori_loop(0, i, lambda jj, c: step(i - 1 - jj, c, False), carry)
        o_ref[...] = acc.astype(o_ref.dtype)

    q_spec, k_spec, v_spec = _sb_specs(n_heads, s, col0)
    return _call(
        body, [qkv, qkv, qkv], name=name, grid=(n_heads, nq),
        in_specs=[q_spec, k_spec, v_spec],
        out_specs=[pl.BlockSpec((SB_ROWS, HEAD_DIM), lambda h, i: (i, h))],
        out_shape=[jax.ShapeDtypeStruct((s, n_heads * HEAD_DIM), BF16)],
        sem=("parallel", "arbitrary"), ride=ride)[0]


def _sb_bwd(qkv, dy, n_heads, col0, *, name, ride=None):
    s = qkv.shape[0]
    nq = s // SB_ROWS
    scale = HEAD_DIM ** -0.5

    def body(q_ref, k_ref, v_ref, dy_ref, dq_ref, dk_ref, dv_ref, e_scr, sg_scr, dk_acc, dv_acc):
        i = pl.program_id(1)
        q = q_ref[...]
        dyv = dy_ref[...]

        @pl.when(i == 0)
        def _():
            dk_acc[...] = jnp.zeros_like(dk_acc)
            dv_acc[...] = jnp.zeros_like(dv_acc)

        tri_later = _sb_tri(later=True)

        def pass1(j, run, own):
            off = pl.multiple_of(j * SB_KEYS, SB_KEYS)
            valid = _sb_valid(i, j, own)
            ls, lk = _sb_scores(q, k_ref[pl.ds(off, SB_KEYS), :], scale, valid)
            between, run = _sb_scan(lk, tri_later, run, later=True)
            a = _masked(valid, jnp.exp(ls + between))
            e_scr[j] = a * _dot_nt(dyv, v_ref[pl.ds(off, SB_KEYS), :])
            sg_scr[j] = jnp.exp(ls)
            dv_acc[pl.ds(off, SB_KEYS), :] += _dot_tn(a.astype(BF16), dyv)
            return run

        lax.fori_loop(0, i, lambda jj, run: pass1(i - 1 - jj, run, False),
                      pass1(i, jnp.zeros((SB_ROWS, SB_BLOCK), F32), True))

        tri_earlier = _sb_tri(later=False)

        def pass2(j, carry, own):
            run, dq = carry
            off = pl.multiple_of(j * SB_KEYS, SB_KEYS)
            kj = k_ref[pl.ds(off, SB_KEYS), :]
            sg = sg_scr[j]
            e = e_scr[j]
            before, run = _sb_scan(e, tri_earlier, run, later=False)
            dz = _masked(_sb_valid(i, j, own), e * (1.0 - sg) - sg * before) * scale
            dzb = dz.astype(BF16)
            dk_acc[pl.ds(off, SB_KEYS), :] += _dot_tn(dzb, q)
            return run, dq + _dot_nn(dzb, kj)

        init = (jnp.zeros((SB_ROWS, SB_BLOCK), F32), jnp.zeros((SB_ROWS, HEAD_DIM), F32))
        _, dq = pass2(i, lax.fori_loop(0, i, lambda j, c: pass2(j, c, False), init), True)
        dq_ref[...] = dq.astype(dq_ref.dtype)

        @pl.when(i == nq - 1)
        def _():
            dk_ref[...] = dk_acc[...].astype(dk_ref.dtype)
            dv_ref[...] = dv_acc[...].astype(dv_ref.dtype)

    q_spec, k_spec, v_spec = _sb_specs(n_heads, s, col0)
    blk = pl.BlockSpec((SB_ROWS, HEAD_DIM), lambda h, i: (i, h))
    full = pl.BlockSpec((s, HEAD_DIM), lambda h, i: (0, h))
    sd = jax.ShapeDtypeStruct((s, n_heads * HEAD_DIM), BF16)
    return _call(
        body, [qkv, qkv, qkv, dy], name=name, grid=(n_heads, nq),
        in_specs=[q_spec, k_spec, v_spec, blk],
        out_specs=[blk, full, full],
        out_shape=[sd, sd, sd],
        scratch_shapes=[pltpu.VMEM((s // SB_KEYS, SB_ROWS, SB_KEYS), F32), pltpu.VMEM((s // SB_KEYS, SB_ROWS, SB_KEYS), F32),
                        pltpu.VMEM((s, HEAD_DIM), F32), pltpu.VMEM((s, HEAD_DIM), F32)],
        sem=("parallel", "arbitrary"), ride=ride)


def _band_bias(rel_bias):
    h = rel_bias.shape[0]
    width = BAND + CHUNK
    first = width - 1 - N_REL
    line = jnp.concatenate([jnp.broadcast_to(rel_bias[:, :1], (h, first)), rel_bias], axis=1)
    tiled = jnp.broadcast_to(line[:, None, :], (h, CHUNK, width - 1)).reshape(h, CHUNK * (width - 1))
    skew = jnp.pad(tiled, ((0, 0), (0, CHUNK))).reshape(h, CHUNK, width)[:, ::-1, :BAND]
    seen = jnp.arange(BAND) >= CHUNK
    return jnp.where(seen[None, None, :], skew, NEG)


def _band_bias_grad(dbias):
    h = dbias.shape[0]
    width = BAND + CHUNK
    flipped = jnp.pad(dbias[:, ::-1, :], ((0, 0), (0, 0), (0, CHUNK)))
    skew = flipped.reshape(h, CHUNK * width)[:, :CHUNK * (width - 1)].reshape(h, CHUNK, width - 1)
    diag = jnp.sum(skew, axis=1)
    first = width - 1 - N_REL
    clipped = jnp.sum(diag[:, :first + 1], axis=1, keepdims=True)
    return jnp.concatenate([clipped, diag[:, first + 1:]], axis=1)


def _group_bias(band):
    return jnp.concatenate([jnp.pad(band, ((0, 0), (0, 0), ((u + 1) * CHUNK, (CA_PER_STEP - 1 - u) * CHUNK)),
                                    constant_values=NEG) for u in range(CA_PER_STEP)], axis=1)


def _group_bias_grad(dgroup):
    return sum(dgroup[:, u * CHUNK:(u + 1) * CHUNK, (u + 1) * CHUNK:(u + 1) * CHUNK + BAND] for u in range(CA_PER_STEP))


def _ca_load_padded(k_ref, v_ref, kp, vp, s):
    kp[pl.ds(0, CA_PAD), :] = jnp.zeros((CA_PAD, HEAD_DIM), kp.dtype)
    vp[pl.ds(0, CA_PAD), :] = jnp.zeros((CA_PAD, HEAD_DIM), vp.dtype)
    kp[pl.ds(CA_PAD, s), :] = k_ref[...]
    vp[pl.ds(CA_PAD, s), :] = v_ref[...]


def _ca_weights(q, kb, bias, off, scale):
    z = _dot_nt(q, kb) * scale + bias
    pos = off + lax.broadcasted_iota(jnp.int32, (CA_ROWS, CA_BAND), 1)
    z = jnp.where(pos >= CA_PAD, z, NEG)
    p = jnp.exp(z - jnp.max(z, axis=1, keepdims=True))
    return p / jnp.sum(p, axis=1, keepdims=True)


def _ca_specs(h_count, s, col0):
    q_spec = pl.BlockSpec((CA_ROWS, HEAD_DIM), lambda h, c: (c, col0 + h))
    k_spec = pl.BlockSpec((s, HEAD_DIM), lambda h, c: (0, col0 + h_count + h))
    v_spec = pl.BlockSpec((s, HEAD_DIM), lambda h, c: (0, col0 + 2 * h_count + h))
    b_spec = pl.BlockSpec((1, CA_ROWS, CA_BAND), lambda h, c: (h, 0, 0))
    return q_spec, k_spec, v_spec, b_spec


def _ca_fwd(qkv, bias, n_heads, col0, *, name, ride=None):
    s = qkv.shape[0]
    nc = s // CA_ROWS
    scale = HEAD_DIM ** -0.5

    def body(q_ref, k_ref, v_ref, b_ref, o_ref, kp, vp):
        c = pl.program_id(1)

        @pl.when(c == 0)
        def _():
            _ca_load_padded(k_ref, v_ref, kp, vp, s)

        off = pl.multiple_of(c * CA_ROWS, CA_ROWS)
        w = _ca_weights(q_ref[...], kp[pl.ds(off, CA_BAND), :], b_ref[0], off, scale)
        o_ref[...] = _dot_nn(w.astype(BF16), vp[pl.ds(off, CA_BAND), :]).astype(o_ref.dtype)

    q_spec, k_spec, v_spec, b_spec = _ca_specs(n_heads, s, col0)
    return _call(
        body, [qkv, qkv, qkv, bias], name=name, grid=(n_heads, nc),
        in_specs=[q_spec, k_spec, v_spec, b_spec],
        out_specs=[pl.BlockSpec((CA_ROWS, HEAD_DIM), lambda h, c: (c, h))],
        out_shape=[jax.ShapeDtypeStruct((s, n_heads * HEAD_DIM), BF16)],
        scratch_shapes=[pltpu.VMEM((s + CA_PAD, HEAD_DIM), BF16), pltpu.VMEM((s + CA_PAD, HEAD_DIM), BF16)],
        sem=("parallel", "arbitrary"), ride=ride)[0]


def _ca_bwd(qkv, bias, dy, n_heads, col0, *, name, ride=None):
    s = qkv.shape[0]
    nc = s // CA_ROWS
    scale = HEAD_DIM ** -0.5

    def body(q_ref, k_ref, v_ref, b_ref, dy_ref, dq_ref, dk_ref, dv_ref, db_ref, kp, vp, dkp, dvp):
        c = pl.program_id(1)

        @pl.when(c == 0)
        def _():
            _ca_load_padded(k_ref, v_ref, kp, vp, s)
            dkp[...] = jnp.zeros_like(dkp)
            dvp[...] = jnp.zeros_like(dvp)
            db_ref[...] = jnp.zeros_like(db_ref)

        off = pl.multiple_of(c * CA_ROWS, CA_ROWS)
        band = pl.ds(off, CA_BAND)
        q = q_ref[...]
        dyv = dy_ref[...]
        kb = kp[band, :]
        w = _ca_weights(q, kb, b_ref[0], off, scale)
        dw = _dot_nt(dyv, vp[band, :])
        dvp[band, :] += _dot_tn(w.astype(BF16), dyv)
        dz = w * (dw - jnp.sum(w * dw, axis=1, keepdims=True))
        db_ref[0] += dz
        dzs = (dz * scale).astype(BF16)
        dq_ref[...] = _dot_nn(dzs, kb).astype(dq_ref.dtype)
        dkp[band, :] += _dot_tn(dzs, q)

        @pl.when(c == nc - 1)
        def _():
            dk_ref[...] = dkp[pl.ds(CA_PAD, s), :].astype(dk_ref.dtype)
            dv_ref[...] = dvp[pl.ds(CA_PAD, s), :].astype(dv_ref.dtype)

    q_spec, k_spec, v_spec, b_spec = _ca_specs(n_heads, s, col0)
    blk = pl.BlockSpec((CA_ROWS, HEAD_DIM), lambda h, c: (c, h))
    full = pl.BlockSpec((s, HEAD_DIM), lambda h, c: (0, h))
    sd = jax.ShapeDtypeStruct((s, n_heads * HEAD_DIM), BF16)
    return _call(
        body, [qkv, qkv, qkv, bias, dy], name=name, grid=(n_heads, nc),
        in_specs=[q_spec, k_spec, v_spec, b_spec, blk],
        out_specs=[blk, full, full, b_spec],
        out_shape=[sd, sd, sd, jax.ShapeDtypeStruct((n_heads, CA_ROWS, CA_BAND), F32)],
        scratch_shapes=[pltpu.VMEM((s + CA_PAD, HEAD_DIM), BF16), pltpu.VMEM((s + CA_PAD, HEAD_DIM), BF16),
                        pltpu.VMEM((s + CA_PAD, HEAD_DIM), F32), pltpu.VMEM((s + CA_PAD, HEAD_DIM), F32)],
        sem=("parallel", "arbitrary"), ride=ride)


EARLY = ("w_sb_out", "w_ca_out", "w_mix_out")


def _step(x, p, target, small, comm):
    w = comm.w
    d = x.shape[1]
    n_sb = w["w_sb_out"].shape[0] // HEAD_DIM
    n_ca = w["w_ca_out"].shape[0] // HEAD_DIM
    qkv_cols = 3 * HEAD_DIM * (n_sb + n_ca)
    ca_col0 = 3 * n_sb
    both = (F32, BF16)

    h1 = _rms_fwd(x, small["g_mix"], name="rms_mix")
    ffn, ple = ("w_ffn_in",), ("w_ple_gate", "w_ple_in")
    qkv = _mm(h1, w["w_in"], "nn", (BF16,), name="proj_qkv", n=qkv_cols, ride=comm.gather(EARLY, "near"))
    gates = _mm(h1, w["w_in"], "nn", (F32,), name="proj_gates", n=2 * d, b_col_off=qkv_cols,
                ride=comm.gather(ffn, "near", comm.gather(EARLY, "far"), (0, 8)))
    bias = _group_bias(_band_bias(small["rel_bias"]))
    y_sb = _sb_fwd(qkv, n_sb, 0, name="sb_fwd", ride=comm.gather(ffn, "near", comm.gather(EARLY, "pair"), (1, 8, 7)))
    y_ca = _ca_fwd(qkv, bias, n_ca, ca_col0, name="ca_fwd", ride=comm.gather(ffn, "far"))
    out = ("w_ffn_out",)
    o_sb = _mm(y_sb, w["w_sb_out"], "nn", (F32,), name="sb_out", ride=comm.gather(out, "near", part=(0, 4)))
    o_ca = _mm(y_ca, w["w_ca_out"], "nn", (F32,), name="ca_out", ride=comm.gather(out, "near", part=(1, 4)))
    merged = _gate_merge_fwd(gates, o_sb, o_ca, name="gate_merge",
                             ride=comm.gather(out, "near", comm.gather(ffn, "pair"), (2, 4)))
    x1 = _mm(merged, w["w_mix_out"], "nn", (F32,), name="mix_out", resid=x, ride=comm.gather(out, "near", part=(3, 4)))
    h2 = _rms_fwd(x1, small["g_ffn"], name="rms_ffn")
    gu = _mm(h2, w["w_ffn_in"], "nn", (BF16,), name="ffn_in", ride=comm.gather(ple, "near", comm.gather(out, "far")))
    act = _swiglu_fwd(gu, name="swiglu", ride=comm.gather(ple, "far", comm.gather(out, "pair")))
    x2 = _mm(act, w["w_ffn_out"], "nn", (F32,), name="ffn_out", resid=x1, ride=comm.gather(ple, "pair"))
    h3 = _rms_fwd(x2, small["g_ple"], name="rms_ple")
    t = _mm(h3, w["w_ple_gate"], "nn", (F32,), name="ple_gate")
    pe = _mm(p, w["w_ple_in"], "nn", (F32,), name="ple_in")
    x3 = _ple_fwd(x2, t, pe, name="ple_add")

    def halves(n, acts, dout, ride, name):
        if comm.pos is None:
            return comm.grad(n, *_mm(acts, dout, "tn", both, name=name))
        g16 = _mm(acts, dout, "tn", (BF16,), name=name + "_other", m_half=(False, comm.pos), ride=ride)
        comm.grad(n, None, g16, half=True)
        g32 = _mm(acts, dout, "tn", (F32,), name=name + "_own", m_half=(True, comm.pos), ride=comm.pair((n,)))
        comm.grad(n, g32, g16, half=True)

    gs = {}
    dx3, gs["g_final"], loss = _final_loss(x3, small["g_final"], target, name="final_loss")
    dt, dpe = _ple_bwd(dx3, t, pe, name="ple_bwd")
    comm.grad("w_ple_in", *_mm(p, dpe, "tn", both, name="dw_ple_in"))
    comm.grad("w_ple_gate", *_mm(h3, dt, "tn", both, name="dw_ple_gate"))
    ple = ("w_ple_in", "w_ple_gate")
    dh3 = _mm(dt, w["w_ple_gate"], "nt", (F32,), name="dh_ple", ride=comm.pair(ple))
    dx2, dx2_16, gs["g_ple"] = _rms_bwd(x2, small["g_ple"], dh3, dx3, name="rms_ple_bwd")
    comm.add(ple)
    comm.grad("w_ffn_out", *_mm(act, dx2_16, "tn", both, name="dw_ffn_out", ride=comm.chips(ple)))
    dact = _mm(dx2_16, w["w_ffn_out"], "nt", (F32,), name="dact", ride=comm.pair(("w_ffn_out",)))
    dgu = _swiglu_bwd(dact, gu, name="swiglu_bwd")
    comm.sum(ple)
    comm.add(("w_ffn_out",))
    comm.grad("w_ffn_in", *_mm(h2, dgu, "tn", both, name="dw_ffn_in",
                               ride=comm.share(ple, comm.chips(("w_ffn_out",)))))
    dh2 = _mm(dgu, w["w_ffn_in"], "nt", (F32,), name="dh_ffn", ride=comm.pair(("w_ffn_in",)))
    dx1, dx1_16, gs["g_ffn"] = _rms_bwd(x1, small["g_ffn"], dh2, dx2, name="rms_ffn_bwd")
    comm.add(("w_ffn_in",))
    comm.sum(("w_ffn_out",))
    comm.grad("w_mix_out", *_mm(merged, dx1_16, "tn", both, name="dw_mix_out", ride=comm.share(("w_ffn_out",))))
    dmerged = _mm(dx1_16, w["w_mix_out"], "nt", (F32,), name="dmerged", ride=comm.pair(("w_mix_out",)))
    dg_sb, dg_ca, do_sb, do_ca = _gate_merge_bwd(dmerged, gates, o_sb, o_ca, name="gate_merge_bwd")
    comm.add(("w_mix_out",))
    comm.grad("w_sb_out", *_mm(y_sb, do_sb, "tn", both, name="dw_sb_out"))
    comm.grad("w_ca_out", *_mm(y_ca, do_ca, "tn", both, name="dw_ca_out"))
    outs = ("w_sb_out", "w_ca_out")
    dy_sb = _mm(do_sb, w["w_sb_out"], "nt", (BF16,), name="dy_sb", ride=comm.pair(outs))
    dy_ca = _mm(do_ca, w["w_ca_out"], "nt", (BF16,), name="dy_ca")
    comm.add(outs)
    dq_sb, dk_sb, dv_sb = _sb_bwd(qkv, dy_sb, n_sb, 0, name="sb_bwd", ride=comm.chips(("w_ffn_in",)))
    comm.sum(("w_ffn_in",))
    late = ("w_mix_out",) + outs
    dq_ca, dk_ca, dv_ca, dbias = _ca_bwd(qkv, bias, dy_ca, n_ca, ca_col0, name="ca_bwd",
                                         ride=comm.chips(late, comm.share(("w_ffn_in",))))
    comm.sum(late)
    gs["rel_bias"] = _band_bias_grad(_group_bias_grad(dbias))
    dproj = _concat_cols([dq_sb, dk_sb, dv_sb, dq_ca, dk_ca, dv_ca, dg_sb, dg_ca], name="dproj")
    halves("w_in", h1, dproj, comm.share(late), "dw_in")
    comm.add(("w_in",))
    half = x.shape[0] // 2
    dh1 = _mm(dproj, w["w_in"], "nt", (F32,), name="dh_mix_top", rows=(0, half), ride=comm.tail(TAIL_SECOND))
    dh1 = _mm(dproj, w["w_in"], "nt", (F32,), name="dh_mix_bottom", rows=(half, half), onto=(dh1,),
              ride=comm.tail(TAIL_SECOND))
    grad_x, _, gs["g_mix"] = _rms_bwd(x, small["g_mix"], dh1, dx1, name="rms_mix_bwd", ride=comm.tail(TAIL_FIRST))
    return loss, grad_x, gs


def _position():
    x, y, c = lax.axis_index("x"), lax.axis_index("y"), lax.axis_index("c")
    chips = [(1 - x, y), (x, 1 - y), (1 - x, 1 - y)]
    return x, y, c, chips


def _aligned(v, m):
    return v if isinstance(v, int) else pl.multiple_of(v, m)


def _piece_dims(shape, axis):
    k, n = shape
    return (k // 2, n // N_CHIPS) if axis == 1 else (k // N_CHIPS // 2, n)


def _piece(ref, shape, axis, j, h, part=(0, 1)):
    pr, pc = _piece_dims(shape, axis)
    nr = pr // part[1] * (part[2] if len(part) > 2 else 1)
    r0 = part[0] * (pr // part[1])
    if axis == 1:
        return ref.at[pl.ds(_aligned(h * pr + r0, 16), nr), pl.ds(_aligned(j * pc, 128), pc)]
    return ref.at[pl.ds(_aligned((2 * j + h) * pr + r0, 16), nr), :]


def _shard_half(ref, h):
    rows = ref.shape[0] // 2
    return ref.at[pl.ds(_aligned(h * rows, 16), rows), :]


def _remote(src, dst, send_sems, recv_sems, k, to):
    return pltpu.make_async_remote_copy(src_ref=src, dst_ref=dst, send_sem=send_sems.at[k],
                                        recv_sem=recv_sems.at[k], device_id=to, device_id_type=MESH)


def _prefetch_call(body, scalars, ins, in_specs, out_shape, out_specs, grid, *, name, ride=None):
    single = not isinstance(out_shape, (list, tuple))
    outs = _call(body, ins, name=name, grid=grid, in_specs=in_specs,
                 out_specs=[out_specs] if single else out_specs, out_shape=[out_shape] if single else out_shape,
                 sem=("parallel",) * len(grid), ride=ride, scalars=scalars)
    return outs[0] if single else outs


def _slab_tiles(pr, pc):
    tc = pc if pc <= 4096 else _pick(pc, (2048, 1024, 512, 256, 128))
    tr = next(t for t in (1024, 512, 256, 128, 64, 32, 16) if pr % t == 0 and t * tc <= 512 * 1024)
    return tr, tc


def _cast_place(w, axis, pos, *, name, ride=None):
    ks, ns = w.shape
    shape = (ks, ns * N_CHIPS) if axis == 1 else (ks * N_CHIPS, ns)
    tr, tc = _slab_tiles(ks, ns)
    nr, nc = ks // tr, ns // tc

    def body(pos_ref, w_ref, o_ref):
        o_ref[...] = w_ref[...].astype(o_ref.dtype)

    if axis == 1:
        out_map = lambda i, j, pos_ref: (i, pos_ref[0] * nc + j)
    else:
        out_map = lambda i, j, pos_ref: (pos_ref[0] * nr + i, j)
    return _prefetch_call(body, pos, [w], [pl.BlockSpec((tr, tc), lambda i, j, pos_ref: (i, j))],
                          jax.ShapeDtypeStruct(shape, BF16), pl.BlockSpec((tr, tc), out_map), (nr, nc), name=name, ride=ride)


def _run(ride, *, name):
    if ride is None:
        return

    def body(o_ref):
        o_ref[...] = jnp.zeros_like(o_ref)

    _call(body, [], name=name, grid=(1,), in_specs=[], out_specs=[pl.BlockSpec((8, 128), lambda i: (0, 0))],
          out_shape=[jax.ShapeDtypeStruct((8, 128), F32)], ride=ride)


def _ride_gather(ride, w, n, axis, stage, part=(0, 1)):
    shape = w[n].shape
    piece = functools.partial(_piece, shape=shape, axis=axis)
    span = part[2] if len(part) > 2 else 1
    halves = [(2 * part[0] + t * span, 2 * part[1], span) for t in range(2)]

    def copies(ins, outs, send_sems, recv_sems, arriving):
        x, y, c, chips = _position()
        me, (xn, yn, dn) = 2 * x + y, [2 * px + py for px, py in chips]
        if stage == "near":
            plan = [(me, c, part, (1 - x, y, c), xn, c, part), (me, c, part, (x, 1 - y, c), yn, c, part)]
        elif stage == "far":
            plan = [(yn, c, halves[1], (1 - x, y, c), dn, c, halves[1]), (xn, c, halves[0], (x, 1 - y, c), dn, c, halves[0])]
        else:
            plan = [(j, c, part, (x, y, 1 - c), j, 1 - c, part) for j in (xn, yn, dn)]
        out = []
        for k, (chip, h, rows, to, from_chip, from_h, from_rows) in enumerate(plan):
            if arriving:
                lands = piece(outs[0], j=from_chip, h=from_h, part=from_rows)
                out.append(_remote(lands, lands, send_sems, recv_sems, k, to))
            else:
                out.append(_remote(piece(ins[0], j=chip, h=h, part=rows), piece(outs[0], j=chip, h=h, part=rows),
                                   send_sems, recv_sems, k, to))
        return out

    def start(*refs):
        for cp in copies(*refs, arriving=False):
            cp.start()

    def finish(*refs):
        for cp in copies(*refs, arriving=True):
            cp.wait_recv()
        for cp in copies(*refs, arriving=False):
            cp.wait_send()

    ride.add([w[n]], [jax.ShapeDtypeStruct(shape, w[n].dtype)], {0: 0}, 3, start, finish,
             lambda outs: w.__setitem__(n, outs[0]))


def _ride_pair(ride, st, axis):
    shape = st["g16"].shape
    pr, pc = (shape[0], shape[1] // N_CHIPS) if st.get("half") else _piece_dims(shape, axis)

    def copies(ins, outs, send_sems, recv_sems):
        x, y, c, _ = _position()
        if st.get("half"):
            pieces = [ins[0].at[:, pl.ds(j * pc, pc)] for j in range(N_CHIPS)]
        else:
            pieces = [_piece(ins[0], shape, axis, j, 1 - c) for j in range(N_CHIPS)]
        return [_remote(pieces[j], outs[0].at[j], send_sems, recv_sems, j, (x, y, 1 - c)) for j in range(N_CHIPS)]

    def start(*refs):
        for cp in copies(*refs):
            cp.start()

    def finish(*refs):
        for cp in copies(*refs):
            cp.wait()

    ride.add([st["g16"]], [jax.ShapeDtypeStruct((N_CHIPS, pr, pc), BF16)], {}, N_CHIPS, start, finish,
             lambda outs: st.__setitem__("sib", outs[0]))


def _ride_chips(ride, st, rows=None):
    _, pr, pc = st["s16"].shape
    r0, nr = (0, pr) if rows is None else rows

    def copies(ins, outs, send_sems, recv_sems):
        x, y, c, chips = _position()
        return [_remote(ins[0].at[2 * px + py, pl.ds(r0, nr), :], outs[0].at[k, pl.ds(r0, nr), :],
                        send_sems, recv_sems, k, (px, py, c)) for k, (px, py) in enumerate(chips)]

    def start(*refs):
        for cp in copies(*refs):
            cp.start()

    def finish(*refs):
        for cp in copies(*refs):
            cp.wait()

    ins, aliases = ([st["s16"], st["recv"]], {1: 0}) if "recv" in st else ([st["s16"]], {})
    ride.add(ins, [jax.ShapeDtypeStruct((3, pr, pc), BF16)], aliases, 3, start, finish,
             lambda outs: st.__setitem__("recv", outs[0]))


def _ride_share(ride, st):
    def sent(ins, outs, send_sems, recv_sems):
        x, y, c, _ = _position()
        return _remote(_shard_half(ins[0], c), _shard_half(outs[0], c), send_sems, recv_sems, 0, (x, y, 1 - c))

    def landed(ins, outs, send_sems, recv_sems):
        x, y, c, _ = _position()
        other = _shard_half(outs[0], 1 - c)
        return _remote(other, other, send_sems, recv_sems, 0, (x, y, 1 - c))

    def start(*refs):
        sent(*refs).start()

    def finish(*refs):
        landed(*refs).wait_recv()
        sent(*refs).wait_send()

    ride.add([st["shard"]], [jax.ShapeDtypeStruct(st["shard"].shape, F32)], {0: 0}, 1, start, finish,
             lambda outs: st.__setitem__("g", outs[0]))


def _piece_block(axis, nr, nc, chip, half=False):
    if half:
        return lambda *a: (a[-3], (a[0] if chip is None else chip(a[-1])) * nc + a[-2])
    if axis == 1:
        return lambda *a: ((a[-1][1] * nr + a[-3]), (a[0] if chip is None else chip(a[-1])) * nc + a[-2])
    return lambda *a: ((2 * (a[0] if chip is None else chip(a[-1])) + a[-1][1]) * nr + a[-3], a[-2])


def _pair_add(g32, sib, axis, pos, *, name, half=False):
    _, pr, pc = sib.shape
    tr, tc = _slab_tiles(pr, pc)
    nr, nc = pr // tr, pc // tc

    def body(pos_ref, g_ref, b_ref, o16_ref):
        o16_ref[0] = (g_ref[...] + b_ref[0].astype(F32)).astype(o16_ref.dtype)

    blk = pl.BlockSpec((1, tr, tc), lambda j, i, k, pos_ref: (j, i, k))
    return _prefetch_call(body, pos, [g32, sib], [pl.BlockSpec((tr, tc), _piece_block(axis, nr, nc, None, half)), blk],
                          jax.ShapeDtypeStruct(sib.shape, BF16), blk, (N_CHIPS, nr, nc), name=name)


def _chip_sum(g32, sib, recv, axis, pos, *, name, half=False):
    _, pr, pc = sib.shape
    tr, tc = _slab_tiles(pr, pc)
    nr, nc = pr // tr, pc // tc

    def body(pos_ref, g_ref, b_ref, r_ref, o_ref):
        pair = g_ref[...] + b_ref[0].astype(F32)
        o_ref[...] = ((pair + r_ref[0].astype(F32)) + r_ref[1].astype(F32)) + r_ref[2].astype(F32)

    return _prefetch_call(
        body, pos, [g32, sib, recv],
        [pl.BlockSpec((tr, tc), _piece_block(axis, nr, nc, lambda pos_ref: pos_ref[0], half)),
         pl.BlockSpec((1, tr, tc), lambda i, k, pos_ref: (pos_ref[0], i, k)),
         pl.BlockSpec((3, tr, tc), lambda i, k, pos_ref: (0, i, k))],
        jax.ShapeDtypeStruct((2 * pr, pc), F32),
        pl.BlockSpec((tr, tc), lambda i, k, pos_ref: (pos_ref[1] * nr + i, k)), (nr, nc), name=name)


class _Comm:
    def __init__(self, pos, w):
        self.pos, self.w, self.st = pos, w, {n: {} for n, _ in BIG}

    def gather(self, names, stage, ride=None, part=(0, 1)):
        ride = _Ride() if ride is None else ride
        for n in names:
            _ride_gather(ride, self.w, n, AXIS[n], stage, part)
        return ride

    def grad(self, n, g32, g16, half=False):
        self.st[n].update(g32=g32, g16=g16, half=half)

    def pair(self, names, ride=None):
        ride = _Ride() if ride is None else ride
        for n in names:
            _ride_pair(ride, self.st[n], AXIS[n])
        return ride

    def add(self, names):
        for n in names:
            st = self.st[n]
            st["s16"] = _pair_add(st["g32"], st["sib"], AXIS[n], self.pos, name="rs_add_" + n, half=st["half"])

    def chips(self, names, ride=None, rows=None):
        ride = _Ride() if ride is None else ride
        for n in names:
            _ride_chips(ride, self.st[n], rows)
        return ride

    def sum(self, names):
        for n in names:
            st = self.st[n]
            st["shard"] = _chip_sum(st["g32"], st["sib"], st["recv"], AXIS[n], self.pos, name="rs_sum_" + n,
                                    half=st["half"])

    def share(self, names, ride=None):
        ride = _Ride() if ride is None else ride
        for n in names:
            _ride_share(ride, self.st[n])
        return ride

    def tail(self, count):
        st = self.st["w_in"]
        rows, at = st["s16"].shape[1], st.get("at", 0)
        st["at"] = at + count
        return self.chips(("w_in",), rows=(at * rows // TAIL_PARTS, count * rows // TAIL_PARTS))

    def tail_rest(self):
        return self.tail(TAIL_PARTS - self.st["w_in"].get("at", 0))

    def result(self, n):
        return self.st[n]["g"]


class _NoComm:
    pos = None

    def __init__(self, w):
        self.w, self.st = w, {}

    def grad(self, n, g32, g16, half=False):
        self.st[n] = (g32, g16)

    def result(self, n):
        return self.st[n]

    def add(self, names):
        pass

    sum = add

    def gather(self, names, *args, **kwargs):
        return None

    pair = chips = share = tail = gather


def _small_all_reduce(vec, *, name):
    r = vec.shape[0]

    def body(vec_ref, out_ref, slots, send_sems, recv_sems):
        x, y, c, _ = _position()
        me = 4 * x + 2 * y + c
        slots[me] = vec_ref[...]
        sends = []
        for k in range(1, 8):
            to = (x ^ (k >> 2), y ^ ((k >> 1) & 1), c ^ (k & 1))
            cp = _remote(slots.at[me], slots.at[me], send_sems, recv_sems, k - 1, to)
            cp.start()
            sends.append(cp)
        for k in range(1, 8):
            frm = 4 * (x ^ (k >> 2)) + 2 * (y ^ ((k >> 1) & 1)) + (c ^ (k & 1))
            _remote(slots.at[frm], slots.at[frm], send_sems, recv_sems, k - 1, (x, y, c)).wait_recv()
        for cp in sends:
            cp.wait_send()
        total = slots[0]
        for d in range(1, 8):
            total = total + slots[d]
        out_ref[...] = total

    return pl.pallas_call(
        body, name=name,
        in_specs=[pl.BlockSpec(memory_space=pltpu.VMEM)], out_specs=pl.BlockSpec(memory_space=pltpu.VMEM),
        out_shape=jax.ShapeDtypeStruct((r, 128), F32),
        scratch_shapes=[pltpu.VMEM((8, r, 128), F32), pltpu.SemaphoreType.DMA((7,)), pltpu.SemaphoreType.DMA((7,))],
    )(vec)


SC_TILES = 32
SC_LANES = 16
SC_TILE_BUDGET = 400 * 1024


def _adamw_update(wv, gv, mv, vv):
    nm = ADAM_B1 * mv + (1.0 - ADAM_B1) * gv
    nv = ADAM_B2 * vv + (1.0 - ADAM_B2) * (gv * gv)
    m_hat = nm / (1.0 - ADAM_B1 ** ADAM_STEP)
    v_hat = nv / (1.0 - ADAM_B2 ** ADAM_STEP)
    return -ADAM_LR * (m_hat / (jnp.sqrt(v_hat) + ADAM_EPS) + ADAM_WD * wv), nm, nv


def _adamw_sc(w, g, m, v, *, name):
    r, c = w.shape
    groups = r // 8
    per_tile = -(-groups // SC_TILES)
    cb = c if 4 * 8 * c * 4 <= SC_TILE_BUDGET else _pick(c, (2048, 1024, 512, 256, 128))

    def body(w_hbm, g_hbm, m_hbm, v_hbm, go_hbm, d_hbm, nm_hbm, nv_hbm, wb, gb, mb, vb):
        tile = lax.axis_index("sc_tile") * 2 + lax.axis_index("sc_core")

        def update(group):
            for c0 in range(0, c, cb):
                at = (pl.ds(group * 8, 8), pl.ds(c0, cb))
                for hbm, buf in ((w_hbm, wb), (g_hbm, gb), (m_hbm, mb), (v_hbm, vb)):
                    pltpu.sync_copy(hbm.at[at], buf)
                pltpu.sync_copy(gb, go_hbm.at[at])

                @pl.loop(0, 8)
                def _(rr):
                    @pl.loop(0, cb, step=SC_LANES)
                    def _(i):
                        lanes = (rr, pl.ds(i, SC_LANES))
                        wb[lanes], mb[lanes], vb[lanes] = _adamw_update(wb[lanes], gb[lanes], mb[lanes], vb[lanes])

                for buf, hbm in ((wb, d_hbm), (mb, nm_hbm), (vb, nv_hbm)):
                    pltpu.sync_copy(buf, hbm.at[at])

        @pl.loop(0, per_tile)
        def _(k):
            group = k * SC_TILES + tile
            if groups % SC_TILES:
                pl.when(group < groups)(lambda: update(group))
            else:
                update(group)

    sd = jax.ShapeDtypeStruct((r, c), F32)
    return pl.kernel(body, name=name, out_type=[sd, sd, sd, sd],
                     mesh=plsc.VectorSubcoreMesh(core_axis_name="sc_core", subcore_axis_name="sc_tile"),
                     scratch_types=[pltpu.VMEM((8, cb), F32)] * 4)(w, g, m, v)


def _adamw(w, g, m, v, *, name, ride=None):
    r, c = w.shape
    tc = c if c <= 4096 else _pick(c, (2048, 1024, 512, 256, 128))
    tr = next(t for t in (512, 256, 128, 64, 32, 16, 8) if r % t == 0 and t * tc <= 256 * 1024)

    def body(w_ref, g_ref, m_ref, v_ref, go_ref, d_ref, nm_ref, nv_ref):
        go_ref[...] = g_ref[...]
        d_ref[...], nm_ref[...], nv_ref[...] = _adamw_update(w_ref[...], g_ref[...], m_ref[...], v_ref[...])

    blk = ((tr, tc), lambda i, j: (i, j))
    sd = jax.ShapeDtypeStruct((r, c), F32)
    return _ew(body, [w, g, m, v], [blk] * 4, [sd] * 4, [blk] * 4, (r // tr, c // tc), name=name, ride=ride)


BIG = (("w_in", 1), ("w_sb_out", 1), ("w_ca_out", 1), ("w_mix_out", 0), ("w_ffn_in", 1), ("w_ffn_out", 0),
       ("w_ple_in", 1), ("w_ple_gate", 0))
AXIS = dict(BIG)
HEAD_PARTS = 8
HEAD_HOSTS = ("w_ffn_in", "w_ffn_out")
TAIL_PARTS = 16
TAIL_SECOND = 4
TAIL_FIRST = 2
ON_SPARSECORE = tuple(n for n, _ in BIG if n != "w_in")
SMALL = ("rel_bias", "g_mix", "g_ffn", "g_ple", "g_final")
ORDER = ("w_in", "w_sb_out", "w_ca_out", "w_mix_out", "rel_bias", "g_mix", "g_ffn", "g_ple", "g_final",
         "w_ffn_in", "w_ffn_out", "w_ple_in", "w_ple_gate")


def _pack(parts):
    flat = jnp.concatenate([a.reshape(-1) for a in parts])
    rows = -(-flat.shape[0] // 1024) * 8
    return jnp.pad(flat, (0, rows * 128 - flat.shape[0])).reshape(rows, 128)


def _unpack(packed, like):
    flat, out, at = packed.reshape(-1), [], 0
    for a in like:
        out.append(flat[at:at + a.size].reshape(a.shape))
        at += a.size
    return out


def kernel(x, p, w_in, w_sb_out, w_ca_out, w_mix_out, rel_bias, g_mix, g_ffn, g_ple, g_final, w_ffn_in, w_ffn_out, w_ple_in, w_ple_gate, loss_target, m_w_in, m_w_sb_out, m_w_ca_out, m_w_mix_out, m_rel_bias, m_g_mix, m_g_ffn, m_g_ple, m_g_final, m_w_ffn_in, m_w_ffn_out, m_w_ple_in, m_w_ple_gate, v_w_in, v_w_sb_out, v_w_ca_out, v_w_mix_out, v_rel_bias, v_g_mix, v_g_ffn, v_g_ple, v_g_final, v_w_ffn_in, v_w_ffn_out, v_w_ple_in, v_w_ple_gate):
    weights = dict(w_in=w_in, w_sb_out=w_sb_out, w_ca_out=w_ca_out, w_mix_out=w_mix_out, rel_bias=rel_bias,
                   g_mix=g_mix, g_ffn=g_ffn, g_ple=g_ple, g_final=g_final, w_ffn_in=w_ffn_in,
                   w_ffn_out=w_ffn_out, w_ple_in=w_ple_in, w_ple_gate=w_ple_gate)
    m_in = dict(w_in=m_w_in, w_sb_out=m_w_sb_out, w_ca_out=m_w_ca_out, w_mix_out=m_w_mix_out, rel_bias=m_rel_bias,
                g_mix=m_g_mix, g_ffn=m_g_ffn, g_ple=m_g_ple, g_final=m_g_final, w_ffn_in=m_w_ffn_in,
                w_ffn_out=m_w_ffn_out, w_ple_in=m_w_ple_in, w_ple_gate=m_w_ple_gate)
    v_in = dict(w_in=v_w_in, w_sb_out=v_w_sb_out, w_ca_out=v_w_ca_out, w_mix_out=v_w_mix_out, rel_bias=v_rel_bias,
                g_mix=v_g_mix, g_ffn=v_g_ffn, g_ple=v_g_ple, g_final=v_g_final, w_ffn_in=v_w_ffn_in,
                w_ffn_out=v_w_ffn_out, w_ple_in=v_w_ple_in, w_ple_gate=v_w_ple_gate)

    pos = jnp.stack([2 * lax.axis_index("x") + lax.axis_index("y"), lax.axis_index("c")]).astype(jnp.int32)
    comm = _Comm(pos, {"w_in": _cast_place(w_in[0], AXIS["w_in"], pos, name="cast_w_in")})
    at = 0
    for n in HEAD_HOSTS:
        ride = comm.gather(("w_in",), "near", part=(at, HEAD_PARTS))
        comm.w[n] = _cast_place(weights[n][0], AXIS[n], pos, name="cast_" + n, ride=ride)
        at += 1
    for n, axis in BIG:
        if n not in comm.w:
            comm.w[n] = _cast_place(weights[n][0], axis, pos, name="cast_" + n)
    _run(comm.gather(("w_in",), "near", part=(at, HEAD_PARTS, HEAD_PARTS - at)), name="gather_w_in_near")
    _run(comm.gather(("w_in",), "far"), name="gather_w_in_far")
    _run(comm.gather(("w_in",), "pair"), name="gather_w_in_pair")
    small = dict(rel_bias=rel_bias[0], g_mix=g_mix, g_ffn=g_ffn, g_ple=g_ple, g_final=g_final.reshape(1, -1))
    loss, grad_x, gs = _step(x[0], p[0, 0], loss_target[0], small, comm)

    grads, delta, new_m, new_v = {}, {}, {}, {}
    for n in [n for n, _ in BIG if n != "w_in"] + ["w_in"]:
        if n == "w_in":
            _run(comm.tail_rest(), name="rs_chips_w_in")
            comm.sum(("w_in",))
            _run(comm.share(("w_in",)), name="rs_share_w_in")
        update = _adamw_sc if n in ON_SPARSECORE else _adamw
        g, d, nm, nv = update(weights[n][0], comm.result(n), m_in[n][0], v_in[n][0], name="adamw_" + n)
        grads[n], delta[n], new_m[n], new_v[n] = g[None], d[None], nm[None], nv[None]

    like = [weights[n] for n in SMALL]
    reduced = _small_all_reduce(_pack([gs[n] for n in SMALL] + [loss[:, :1]]), name="small_all_reduce")
    g_small = _unpack(reduced, like + [loss[:, :1]])
    total_loss = g_small[-1].reshape(())
    g_packed = _pack(g_small[:-1])
    _, d_s, m_s, v_s = _adamw(_pack(like), g_packed, _pack([m_in[n] for n in SMALL]), _pack([v_in[n] for n in SMALL]),
                           name="adamw_small")
    for n, g, d, nm, nv in zip(SMALL, g_small[:-1], _unpack(d_s, like), _unpack(m_s, like), _unpack(v_s, like)):
        grads[n], delta[n], new_m[n], new_v[n] = g, d, nm, nv

    return (total_loss, grad_x[None], *[grads[n] for n in ORDER], *[delta[n] for n in ORDER],
            *[new_m[n] for n in ORDER], *[new_v[n] for n in ORDER])
```

```python
import functools
import math

import jax
import jax.numpy as jnp
import numpy as np
from jax import lax
from jax.experimental import pallas as pl
from jax.experimental.pallas import tpu as pltpu
from jax.experimental.pallas import tpu_sc as plsc

F32 = jnp.float32
BF16 = jnp.bfloat16

HEAD_DIM = 128
CHUNK = 64
LEFT_CHUNKS = 8
REL_CLIP = 128
N_REL = REL_CLIP + CHUNK
BAND = (LEFT_CHUNKS + 2) * CHUNK
CA_PER_STEP = 4
CA_ROWS = CA_PER_STEP * CHUNK
CA_BAND = BAND + CA_PER_STEP * CHUNK
CA_PAD = BAND
SB_BLOCK = 128
SB_KEYS = 512
SB_GROUPS = SB_KEYS // SB_BLOCK
SB_ROWS = SB_KEYS
EPS = 1e-6
NEG = -1e30

ADAM_LR = 0.001
ADAM_B1 = 0.9
ADAM_B2 = 0.999
ADAM_EPS = 1e-08
ADAM_WD = 0.01
ADAM_STEP = 10

VMEM_LIMIT = 48 * 1024 * 1024
MM_VMEM_BUDGET = 36 * 1024 * 1024
V7X_HBM_BYTES_PER_S = 3.7e12
GRID_STEP_S = 0.35e-6
MESH = pl.DeviceIdType.MESH
N_CHIPS = 4


def _pick(dim, prefs):
    for t in prefs:
        if dim % t == 0:
            return t
    raise ValueError(f"no tile for {dim}")


def _cparams(sem=None):
    return pltpu.CompilerParams(dimension_semantics=sem, vmem_limit_bytes=VMEM_LIMIT)


def _sigmoid(v):
    return 1.0 / (1.0 + jnp.exp(-v))


def _dot(a, b, dims):
    return lax.dot_general(a, b, (dims, ((), ())), preferred_element_type=F32)


def _dot_nn(a, b):
    return _dot(a, b, ((1,), (0,)))


def _dot_nt(a, b):
    return _dot(a, b, ((1,), (1,)))


def _dot_tn(a, b):
    return _dot(a, b, ((0,), (0,)))


HBM = pl.BlockSpec(memory_space=pltpu.HBM)


class _Ride:
    def __init__(self):
        self.items = []

    def add(self, ins, outs, aliases, n_sems, start, finish, sink):
        self.items.append((ins, outs, aliases, n_sems, start, finish, sink))


def _call(body, args, *, name, grid, in_specs, out_specs, out_shape, scratch_shapes=(), sem=None, ride=None,
          scalars=None, onto=()):
    items = ride.items if ride is not None else []
    if onto:
        args, in_specs = list(args) + list(onto), list(in_specs) + [HBM] * len(onto)
        inner, body = body, lambda *refs: inner(*refs[:len(args) - len(onto)], *refs[len(args):])
    n_in, n_out, n_scr = len(args), len(out_shape), len(scratch_shapes)
    r_ins = [a for it in items for a in it[0]]
    r_outs = [o for it in items for o in it[1]]
    updated = [id(it[0][i]) for it in items for i in it[2]]
    assert len(set(updated)) == len(updated), "one call may update a buffer in place only once"
    aliases, a, b = {n_in - len(onto) + t: t for t in range(len(onto))}, n_in, n_out
    for it in items:
        aliases.update({a + i: b + o for i, o in it[2].items()})
        a, b = a + len(it[0]), b + len(it[1])
    sems = [pltpu.SemaphoreType.DMA((it[3],)) for it in items for _ in range(2)]

    def wrapped(*refs):
        head, refs = (refs[:1], refs[1:]) if scalars is not None else ((), refs)
        ins, rin = refs[:n_in], refs[n_in:n_in + len(r_ins)]
        at = n_in + len(r_ins)
        outs, rout = refs[at:at + n_out], refs[at + n_out:at + n_out + len(r_outs)]
        at += n_out + len(r_outs)
        scr, rsem = refs[at:at + n_scr], refs[at + n_scr:]

        def each(which):
            a = b = 0
            for q, it in enumerate(items):
                it[which](rin[a:a + len(it[0])], rout[b:b + len(it[1])], rsem[2 * q], rsem[2 * q + 1])
                a, b = a + len(it[0]), b + len(it[1])

        if items:
            ids = [pl.program_id(d) for d in range(len(grid))]
            first = functools.reduce(jnp.logical_and, [i == 0 for i in ids])
            last = functools.reduce(jnp.logical_and, [i == g - 1 for i, g in zip(ids, grid)])
            pl.when(first)(lambda: each(4))
        body(*head, *ins, *outs, *scr)
        if items:
            pl.when(last)(lambda: each(5))

    specs = dict(grid=grid, in_specs=list(in_specs) + [HBM] * len(r_ins),
                 out_specs=list(out_specs) + [HBM] * len(r_outs), scratch_shapes=list(scratch_shapes) + sems)
    if scalars is not None:
        specs = dict(grid_spec=pltpu.PrefetchScalarGridSpec(num_scalar_prefetch=1, **specs))
        aliases = {i + 1: o for i, o in aliases.items()}
    res = pl.pallas_call(
        wrapped, name=name, **specs,
        out_shape=list(out_shape) + r_outs,
        input_output_aliases=aliases,
        compiler_params=_cparams(("arbitrary",) * len(grid) if items else sem),
    )(*(() if scalars is None else (scalars,)), *args, *r_ins)
    b = n_out
    for it in items:
        it[6](res[b:b + len(it[1])])
        b += len(it[1])
    return list(res[:n_out])


def _mm_tiles(m, n_align, n, k, a_bytes, b_bytes, out_bytes):
    best = None
    tks = sorted({t for t in (k, k // 2, k // 4, 2048, 1024, 512, 256, 128) if t <= k and k % t == 0 and t % 128 == 0})
    for tm in (t for t in (2048, 1024, 512, 256, 128) if m % t == 0):
        for tn in (t for t in (2048, 1024, 512, 256, 128) if n_align % t == 0):
            for tk in tks:
                nk = k // tk
                vmem = 2 * (tm * tk * a_bytes + tk * tn * b_bytes + tm * tn * out_bytes) + tm * tn * 4
                if vmem > MM_VMEM_BUDGET:
                    continue
                traffic = m * k * a_bytes * (n // tn if nk > 1 else 1) + k * n * b_bytes * (m // tm)
                traffic += tm * tk * a_bytes + tk * tn * b_bytes + tm * tn * out_bytes
                traffic += m * n * 4 * nk if nk > 1 else 0
                cost = traffic / V7X_HBM_BYTES_PER_S + (m // tm) * (n // tn) * nk * GRID_STEP_S
                if best is None or cost < best[0]:
                    best = (cost, tm, tn, tk)
    return best[1:]


def _mm(a, b, mode, out_dtypes, *, name, n=None, b_col_off=0, resid=None, ride=None, rows=None, onto=(), m_half=None):
    if mode == "nn":
        m, k = a.shape
        n = b.shape[1] if n is None else n
    elif mode == "nt":
        m, k = a.shape
        n = b.shape[0]
    else:
        k, m = a.shape
        n = b.shape[1]
    if m_half is not None:
        m //= 2
    m_all, (row0, m) = m, (0, m) if rows is None else rows
    n_out = len(out_dtypes)
    has_resid = resid is not None
    out_bytes = sum(jnp.dtype(dt).itemsize for dt in out_dtypes) + (4 if has_resid else 0)
    tm, tn, tk = _mm_tiles(math.gcd(m, row0) if row0 else m, math.gcd(n, b_col_off) if b_col_off else n, n, k,
                           a.dtype.itemsize, b.dtype.itemsize, out_bytes)
    nk = k // tk
    boff, roff = b_col_off // tn, row0 // tm
    dot = {"nn": _dot_nn, "nt": _dot_nt, "tn": _dot_tn}[mode]
    if m_half is None:
        half = lambda: 0
    else:
        half = lambda pos_ref: (pos_ref[1] if m_half[0] else 1 - pos_ref[1]) * (m // tm)

    def body(*refs):
        refs = refs[m_half is not None:]
        a_ref, b_ref = refs[0], refs[1]
        r_ref = refs[2] if has_resid else None
        o_refs = refs[2 + has_resid: 2 + has_resid + n_out]

        def finish(r):
            if has_resid:
                r = r + r_ref[...]
            for o_ref in o_refs:
                o_ref[...] = r.astype(o_ref.dtype)

        part = dot(a_ref[...].astype(BF16), b_ref[...].astype(BF16))
        if nk == 1:
            finish(part)
            return
        acc_ref = refs[-1]
        kk = pl.program_id(2)

        @pl.when(kk == 0)
        def _():
            acc_ref[...] = part

        @pl.when(kk > 0)
        def _():
            acc_ref[...] += part

        @pl.when(kk == nk - 1)
        def _():
            finish(acc_ref[...])

    if mode == "nn":
        a_spec = pl.BlockSpec((tm, tk), lambda i, j, kk, *_: (i + roff, kk))
        b_spec = pl.BlockSpec((tk, tn), lambda i, j, kk, *_: (kk, j + boff))
    elif mode == "nt":
        a_spec = pl.BlockSpec((tm, tk), lambda i, j, kk, *_: (i + roff, kk))
        b_spec = pl.BlockSpec((tn, tk), lambda i, j, kk, *_: (j, kk))
    else:
        a_spec = pl.BlockSpec((tk, tm), lambda i, j, kk, *pos: (kk, i + half(*pos)))
        b_spec = pl.BlockSpec((tk, tn), lambda i, j, kk, *_: (kk, j))
    o_spec = pl.BlockSpec((tm, tn), lambda i, j, kk, *_: (i + roff, j))
    in_specs = [a_spec, b_spec] + ([o_spec] if has_resid else [])
    args = [a, b] + ([resid] if has_resid else [])
    outs = _call(
        body, args, name=name,
        grid=(m // tm, n // tn, nk),
        in_specs=in_specs,
        out_specs=[o_spec] * n_out,
        out_shape=[jax.ShapeDtypeStruct((m_all, n), dt) for dt in out_dtypes],
        scratch_shapes=[pltpu.VMEM((tm, tn), F32)] if nk > 1 else [],
        sem=("parallel", "parallel", "arbitrary"), ride=ride, onto=onto,
        scalars=None if m_half is None else m_half[1])
    return outs[0] if n_out == 1 else tuple(outs)


def _row_tile(s):
    return _pick(s, (256, 128))


def _rms_fwd(x, g, *, name, ride=None):
    s, d = x.shape
    tr = _row_tile(s)

    def body(x_ref, g_ref, o_ref):
        xv = x_ref[...]
        r = lax.rsqrt(jnp.mean(xv * xv, axis=1, keepdims=True) + EPS)
        o_ref[...] = (xv * r * g_ref[...]).astype(o_ref.dtype)

    return _call(
        body, [x, g], name=name, grid=(s // tr,),
        in_specs=[pl.BlockSpec((tr, d), lambda i: (i, 0)), pl.BlockSpec((1, d), lambda i: (0, 0))],
        out_specs=[pl.BlockSpec((tr, d), lambda i: (i, 0))],
        out_shape=[jax.ShapeDtypeStruct((s, d), BF16)], sem=("parallel",), ride=ride)[0]


def _rms_bwd(x, g, dh, dres, *, name, ride=None):
    s, d = x.shape
    tr = _row_tile(s)

    def body(x_ref, g_ref, dh_ref, dres_ref, dx_ref, dx16_ref, dg_ref):
        i = pl.program_id(0)
        xv = x_ref[...]
        r = lax.rsqrt(jnp.mean(xv * xv, axis=1, keepdims=True) + EPS)
        xhat = xv * r
        dhv = dh_ref[...]
        dxhat = dhv * g_ref[...]
        proj = jnp.mean(dxhat * xhat, axis=1, keepdims=True)
        dx = dres_ref[...] + r * (dxhat - xhat * proj)
        dx_ref[...] = dx
        dx16_ref[...] = dx.astype(dx16_ref.dtype)

        @pl.when(i == 0)
        def _():
            dg_ref[...] = jnp.zeros_like(dg_ref)

        dg_ref[...] += jnp.sum(dhv * xhat, axis=0, keepdims=True)

    row = pl.BlockSpec((tr, d), lambda i: (i, 0))
    vec = pl.BlockSpec((1, d), lambda i: (0, 0))
    return _call(
        body, [x, g, dh, dres], name=name, grid=(s // tr,),
        in_specs=[row, vec, row, row],
        out_specs=[row, row, vec],
        out_shape=[jax.ShapeDtypeStruct((s, d), F32), jax.ShapeDtypeStruct((s, d), BF16),
                   jax.ShapeDtypeStruct((1, d), F32)],
        sem=("arbitrary",), ride=ride)


def _final_loss(x, g, target, *, name):
    s, d = x.shape
    tr = _row_tile(s)

    def body(x_ref, g_ref, t_ref, dx_ref, dg_ref, loss_ref):
        i = pl.program_id(0)
        xv = x_ref[...]
        gv = g_ref[...]
        r = lax.rsqrt(jnp.mean(xv * xv, axis=1, keepdims=True) + EPS)
        xhat = xv * r
        err = xhat * gv - t_ref[...]
        dy = err * (1.0 / d)
        dxhat = dy * gv
        proj = jnp.mean(dxhat * xhat, axis=1, keepdims=True)
        dx_ref[...] = r * (dxhat - xhat * proj)

        @pl.when(i == 0)
        def _():
            dg_ref[...] = jnp.zeros_like(dg_ref)
            loss_ref[...] = jnp.zeros_like(loss_ref)

        dg_ref[...] += jnp.sum(dy * xhat, axis=0, keepdims=True)
        part = 0.5 * jnp.sum(jnp.mean(err * err, axis=1, keepdims=True), axis=0, keepdims=True)
        loss_ref[...] += jnp.broadcast_to(part, loss_ref.shape)

    row = pl.BlockSpec((tr, d), lambda i: (i, 0))
    vec = pl.BlockSpec((1, d), lambda i: (0, 0))
    return pl.pallas_call(
        body, name=name, grid=(s // tr,),
        in_specs=[row, vec, row],
        out_specs=[row, vec, pl.BlockSpec((1, 128), lambda i: (0, 0))],
        out_shape=[jax.ShapeDtypeStruct((s, d), F32), jax.ShapeDtypeStruct((1, d), F32),
                   jax.ShapeDtypeStruct((1, 128), F32)],
        compiler_params=_cparams(("arbitrary",)),
    )(x, g, target)


def _ew(body, ins, in_blocks, outs, out_blocks, grid, *, name, ride=None):
    return _call(body, ins, name=name, grid=grid,
                 in_specs=[pl.BlockSpec(bs, im) for bs, im in in_blocks],
                 out_specs=[pl.BlockSpec(bs, im) for bs, im in out_blocks],
                 out_shape=outs, sem=("parallel",) * len(grid), ride=ride)


def _gate_merge_fwd(gates, o_sb, o_ca, *, name, ride=None):
    s, d = o_sb.shape
    tr, tc = _row_tile(s), _pick(d, (1024, 512, 256, 128))
    nc = d // tc

    def body(gs_ref, gc_ref, os_ref, oc_ref, m_ref):
        m = _sigmoid(gs_ref[...]) * os_ref[...] + _sigmoid(gc_ref[...]) * oc_ref[...]
        m_ref[...] = m.astype(m_ref.dtype)

    blk = ((tr, tc), lambda i, j: (i, j))
    return _ew(body, [gates, gates, o_sb, o_ca],
               [blk, ((tr, tc), lambda i, j: (i, j + nc)), blk, blk],
               [jax.ShapeDtypeStruct((s, d), BF16)], [blk], (s // tr, nc), name=name, ride=ride)[0]


def _gate_merge_bwd(dmerged, gates, o_sb, o_ca, *, name):
    s, d = o_sb.shape
    tr, tc = _row_tile(s), _pick(d, (1024, 512, 256, 128))
    nc = d // tc

    def body(dm_ref, gs_ref, gc_ref, os_ref, oc_ref, dgs_ref, dgc_ref, dos_ref, doc_ref):
        dm = dm_ref[...]
        ss = _sigmoid(gs_ref[...])
        sc = _sigmoid(gc_ref[...])
        dgs_ref[...] = (dm * os_ref[...] * ss * (1.0 - ss)).astype(dgs_ref.dtype)
        dgc_ref[...] = (dm * oc_ref[...] * sc * (1.0 - sc)).astype(dgc_ref.dtype)
        dos_ref[...] = (dm * ss).astype(dos_ref.dtype)
        doc_ref[...] = (dm * sc).astype(doc_ref.dtype)

    blk = ((tr, tc), lambda i, j: (i, j))
    sd = jax.ShapeDtypeStruct((s, d), BF16)
    return _ew(body, [dmerged, gates, gates, o_sb, o_ca],
               [blk, blk, ((tr, tc), lambda i, j: (i, j + nc)), blk, blk],
               [sd, sd, sd, sd], [blk, blk, blk, blk], (s // tr, nc), name=name)


def _swiglu_fwd(gu, *, name, ride=None):
    s, f2 = gu.shape
    f = f2 // 2
    tr, tc = 128, _pick(f, (512, 256, 128))

    def body(gu_ref, a_ref):
        for at in range(0, f, tc):
            gv = gu_ref[:, at:at + tc].astype(F32)
            a_ref[:, at:at + tc] = (gv * _sigmoid(gv) * gu_ref[:, f + at:f + at + tc].astype(F32)).astype(a_ref.dtype)

    row = lambda i: (i, 0)
    return _ew(body, [gu], [((tr, f2), row)], [jax.ShapeDtypeStruct((s, f), BF16)], [((tr, f), row)],
               (s // tr,), name=name, ride=ride)[0]


def _swiglu_bwd(dact, gu, *, name):
    s, f2 = gu.shape
    f = f2 // 2
    tr, tc = 128, _pick(f, (512, 256, 128))

    def body(da_ref, gu_ref, o_ref):
        for at in range(0, f, tc):
            da = da_ref[:, at:at + tc]
            gv = gu_ref[:, at:at + tc].astype(F32)
            sg = _sigmoid(gv)
            uv = gu_ref[:, f + at:f + at + tc].astype(F32)
            o_ref[:, at:at + tc] = (da * uv * sg * (1.0 + gv * (1.0 - sg))).astype(o_ref.dtype)
            o_ref[:, f + at:f + at + tc] = (da * gv * sg).astype(o_ref.dtype)

    row = lambda i: (i, 0)
    return _ew(body, [dact, gu], [((tr, f), row), ((tr, f2), row)], [jax.ShapeDtypeStruct((s, f2), BF16)],
               [((tr, f2), row)], (s // tr,), name=name)[0]


def _concat_cols(parts, *, name):
    s = parts[0].shape[0]
    widths = [p.shape[1] for p in parts]
    tr = 256

    def body(*refs):
        o_ref, at = refs[-1], 0
        for p_ref, width in zip(refs, widths):
            o_ref[:, at:at + width] = p_ref[...]
            at += width

    row = lambda i: (i, 0)
    return _ew(body, list(parts), [((tr, width), row) for width in widths],
               [jax.ShapeDtypeStruct((s, sum(widths)), parts[0].dtype)], [((tr, sum(widths)), row)],
               (s // tr,), name=name)[0]


def _ple_fwd(x, t, pe, *, name):
    s, d = x.shape
    tr, tc = _row_tile(s), _pick(d, (1024, 512, 256, 128))

    def body(x_ref, t_ref, p_ref, o_ref):
        o_ref[...] = x_ref[...] + _sigmoid(t_ref[...]) * p_ref[...]

    blk = ((tr, tc), lambda i, j: (i, j))
    return _ew(body, [x, t, pe], [blk, blk, blk],
               [jax.ShapeDtypeStruct((s, d), F32)], [blk], (s // tr, d // tc), name=name)[0]


def _ple_bwd(dx, t, pe, *, name):
    s, d = dx.shape
    tr, tc = _row_tile(s), _pick(d, (1024, 512, 256, 128))

    def body(dx_ref, t_ref, p_ref, dt_ref, dp_ref):
        dxv = dx_ref[...]
        sg = _sigmoid(t_ref[...])
        dt_ref[...] = (dxv * p_ref[...] * sg * (1.0 - sg)).astype(dt_ref.dtype)
        dp_ref[...] = (dxv * sg).astype(dp_ref.dtype)

    blk = ((tr, tc), lambda i, j: (i, j))
    sd = jax.ShapeDtypeStruct((s, d), BF16)
    return _ew(body, [dx, t, pe], [blk, blk, blk], [sd, sd], [blk, blk], (s // tr, d // tc), name=name)


def _sb_tri(later):
    row = lax.broadcasted_iota(jnp.int32, (SB_BLOCK, SB_BLOCK), 0)
    col = lax.broadcasted_iota(jnp.int32, (SB_BLOCK, SB_BLOCK), 1)
    tri = (row > col) if later else (row < col)
    return jnp.concatenate([tri.astype(BF16), jnp.ones((SB_BLOCK, SB_BLOCK), BF16)], axis=1)


def _sb_valid(i, j, own):
    if not own:
        return None
    qi = i * SB_ROWS + lax.broadcasted_iota(jnp.int32, (SB_ROWS, SB_KEYS), 0)
    ki = j * SB_KEYS + lax.broadcasted_iota(jnp.int32, (SB_ROWS, SB_KEYS), 1)
    return ki < qi


def _sb_scan(v, tri, run, later, exact=True):
    hi = v.astype(BF16)
    lo = (v - hi.astype(F32)).astype(BF16) if exact else None
    outs = [None] * SB_GROUPS
    for b in (reversed(range(SB_GROUPS)) if later else range(SB_GROUPS)):
        cols = slice(b * SB_BLOCK, (b + 1) * SB_BLOCK)
        r = _dot_nn(hi[:, cols], tri)
        if exact:
            r = r + _dot_nn(lo[:, cols], tri)
        outs[b] = r[:, :SB_BLOCK] + run
        run = run + r[:, SB_BLOCK:]
    return jnp.concatenate(outs, axis=1), run


def _masked(valid, v):
    return v if valid is None else jnp.where(valid, v, 0.0)


def _sb_scores(q, kj, scale, valid):
    z = _dot_nt(q, kj) * scale
    t = jnp.log(1.0 + jnp.exp(-jnp.abs(z)))
    return jnp.minimum(z, 0.0) - t, _masked(valid, -jnp.maximum(z, 0.0) - t)


def _sb_specs(h_count, s, col0):
    q_spec = pl.BlockSpec((SB_ROWS, HEAD_DIM), lambda h, i: (i, col0 + h))
    k_spec = pl.BlockSpec((s, HEAD_DIM), lambda h, i: (0, col0 + h_count + h))
    v_spec = pl.BlockSpec((s, HEAD_DIM), lambda h, i: (0, col0 + 2 * h_count + h))
    return q_spec, k_spec, v_spec


def _sb_fwd(qkv, n_heads, col0, *, name, ride=None):
    s = qkv.shape[0]
    nq = s // SB_ROWS
    scale = HEAD_DIM ** -0.5

    def body(q_ref, k_ref, v_ref, o_ref):
        i = pl.program_id(1)
        q = q_ref[...]
        tri = _sb_tri(later=True)

        def step(j, carry, own):
            run, acc = carry
            off = pl.multiple_of(j * SB_KEYS, SB_KEYS)
            valid = _sb_valid(i, j, own)
            ls, lk = _sb_scores(q, k_ref[pl.ds(off, SB_KEYS), :], scale, valid)
            between, run = _sb_scan(lk, tri, run, later=True)
            a = _masked(valid, jnp.exp(ls + between))
            return run, acc + _dot_nn(a.astype(BF16), v_ref[pl.ds(off, SB_KEYS), :])

        carry = step(i, (jnp.zeros((SB_ROWS, SB_BLOCK), F32), jnp.zeros((SB_ROWS, HEAD_DIM), F32)), True)
        _, acc = lax.fori_loop(0, i, lambda jj, c: step(i - 1 - jj, c, False), carry)
        o_ref[...] = acc.astype(o_ref.dtype)

    q_spec, k_spec, v_spec = _sb_specs(n_heads, s, col0)
    return _call(
        body, [qkv, qkv, qkv], name=name, grid=(n_heads, nq),
        in_specs=[q_spec, k_spec, v_spec],
        out_specs=[pl.BlockSpec((SB_ROWS, HEAD_DIM), lambda h, i: (i, h))],
        out_shape=[jax.ShapeDtypeStruct((s, n_heads * HEAD_DIM), BF16)],
        sem=("parallel", "arbitrary"), ride=ride)[0]


def _sb_bwd(qkv, dy, n_heads, col0, *, name, ride=None):
    s = qkv.shape[0]
    nq = s // SB_ROWS
    scale = HEAD_DIM ** -0.5

    def body(q_ref, k_ref, v_ref, dy_ref, dq_ref, dk_ref, dv_ref, e_scr, sg_scr, dk_acc, dv_acc):
        i = pl.program_id(1)
        q = q_ref[...]
        dyv = dy_ref[...]

        @pl.when(i == 0)
        def _():
            dk_acc[...] = jnp.zeros_like(dk_acc)
            dv_acc[...] = jnp.zeros_like(dv_acc)

        tri_later = _sb_tri(later=True)

        def pass1(j, run, own):
            off = pl.multiple_of(j * SB_KEYS, SB_KEYS)
            valid = _sb_valid(i, j, own)
            ls, lk = _sb_scores(q, k_ref[pl.ds(off, SB_KEYS), :], scale, valid)
            between, run = _sb_scan(lk, tri_later, run, later=True)
            a = _masked(valid, jnp.exp(ls + between))
            e_scr[j] = a * _dot_nt(dyv, v_ref[pl.ds(off, SB_KEYS), :])
            sg_scr[j] = jnp.exp(ls)
            dv_acc[pl.ds(off, SB_KEYS), :] += _dot_tn(a.astype(BF16), dyv)
            return run

        lax.fori_loop(0, i, lambda jj, run: pass1(i - 1 - jj, run, False),
                      pass1(i, jnp.zeros((SB_ROWS, SB_BLOCK), F32), True))

        tri_earlier = _sb_tri(later=False)

        def pass2(j, carry, own):
            run, dq = carry
            off = pl.multiple_of(j * SB_KEYS, SB_KEYS)
            kj = k_ref[pl.ds(off, SB_KEYS), :]
            sg = sg_scr[j]
            e = e_scr[j]
            before, run = _sb_scan(e, tri_earlier, run, later=False, exact=False)
            dz = _masked(_sb_valid(i, j, own), e * (1.0 - sg) - sg * before) * scale
            dzb = dz.astype(BF16)
            dk_acc[pl.ds(off, SB_KEYS), :] += _dot_tn(dzb, q)
            return run, dq + _dot_nn(dzb, kj)

        init = (jnp.zeros((SB_ROWS, SB_BLOCK), F32), jnp.zeros((SB_ROWS, HEAD_DIM), F32))
        _, dq = pass2(i, lax.fori_loop(0, i, lambda j, c: pass2(j, c, False), init), True)
        dq_ref[...] = dq.astype(dq_ref.dtype)

        @pl.when(i == nq - 1)
        def _():
            dk_ref[...] = dk_acc[...].astype(dk_ref.dtype)
            dv_ref[...] = dv_acc[...].astype(dv_ref.dtype)

    q_spec, k_spec, v_spec = _sb_specs(n_heads, s, col0)
    blk = pl.BlockSpec((SB_ROWS, HEAD_DIM), lambda h, i: (i, h))
    full = pl.BlockSpec((s, HEAD_DIM), lambda h, i: (0, h))
    sd = jax.ShapeDtypeStruct((s, n_heads * HEAD_DIM), BF16)
    return _call(
        body, [qkv, qkv, qkv, dy], name=name, grid=(n_heads, nq),
        in_specs=[q_spec, k_spec, v_spec, blk],
        out_specs=[blk, full, full],
        out_shape=[sd, sd, sd],
        scratch_shapes=[pltpu.VMEM((s // SB_KEYS, SB_ROWS, SB_KEYS), F32), pltpu.VMEM((s // SB_KEYS, SB_ROWS, SB_KEYS), F32),
                        pltpu.VMEM((s, HEAD_DIM), F32), pltpu.VMEM((s, HEAD_DIM), F32)],
        sem=("parallel", "arbitrary"), ride=ride)


def _band_bias(rel_bias):
    h = rel_bias.shape[0]
    width = BAND + CHUNK
    first = width - 1 - N_REL
    line = jnp.concatenate([jnp.broadcast_to(rel_bias[:, :1], (h, first)), rel_bias], axis=1)
    tiled = jnp.broadcast_to(line[:, None, :], (h, CHUNK, width - 1)).reshape(h, CHUNK * (width - 1))
    skew = jnp.pad(tiled, ((0, 0), (0, CHUNK))).reshape(h, CHUNK, width)[:, ::-1, :BAND]
    seen = jnp.arange(BAND) >= CHUNK
    return jnp.where(seen[None, None, :], skew, NEG)


def _band_bias_grad(dbias):
    h = dbias.shape[0]
    width = BAND + CHUNK
    flipped = jnp.pad(dbias[:, ::-1, :], ((0, 0), (0, 0), (0, CHUNK)))
    skew = flipped.reshape(h, CHUNK * width)[:, :CHUNK * (width - 1)].reshape(h, CHUNK, width - 1)
    diag = jnp.sum(skew, axis=1)
    first = width - 1 - N_REL
    clipped = jnp.sum(diag[:, :first + 1], axis=1, keepdims=True)
    return jnp.concatenate([clipped, diag[:, first + 1:]], axis=1)


def _group_bias(band):
    return jnp.concatenate([jnp.pad(band, ((0, 0), (0, 0), ((u + 1) * CHUNK, (CA_PER_STEP - 1 - u) * CHUNK)),
                                    constant_values=NEG) for u in range(CA_PER_STEP)], axis=1)


def _group_bias_grad(dgroup):
    return sum(dgroup[:, u * CHUNK:(u + 1) * CHUNK, (u + 1) * CHUNK:(u + 1) * CHUNK + BAND] for u in range(CA_PER_STEP))


def _ca_load_padded(k_ref, v_ref, kp, vp, s):
    kp[pl.ds(0, CA_PAD), :] = jnp.zeros((CA_PAD, HEAD_DIM), kp.dtype)
    vp[pl.ds(0, CA_PAD), :] = jnp.zeros((CA_PAD, HEAD_DIM), vp.dtype)
    kp[pl.ds(CA_PAD, s), :] = k_ref[...]
    vp[pl.ds(CA_PAD, s), :] = v_ref[...]


def _ca_weights(q, kb, bias, off, scale):
    z = _dot_nt(q, kb) * scale + bias
    pos = off + lax.broadcasted_iota(jnp.int32, (CA_ROWS, CA_BAND), 1)
    z = jnp.where(pos >= CA_PAD, z, NEG)
    p = jnp.exp(z - jnp.max(z, axis=1, keepdims=True))
    return p / jnp.sum(p, axis=1, keepdims=True)


def _ca_specs(h_count, s, col0):
    q_spec = pl.BlockSpec((CA_ROWS, HEAD_DIM), lambda h, c: (c, col0 + h))
    k_spec = pl.BlockSpec((s, HEAD_DIM), lambda h, c: (0, col0 + h_count + h))
    v_spec = pl.BlockSpec((s, HEAD_DIM), lambda h, c: (0, col0 + 2 * h_count + h))
    b_spec = pl.BlockSpec((1, CA_ROWS, CA_BAND), lambda h, c: (h, 0, 0))
    return q_spec, k_spec, v_spec, b_spec


def _ca_fwd(qkv, bias, n_heads, col0, *, name, ride=None):
    s = qkv.shape[0]
    nc = s // CA_ROWS
    scale = HEAD_DIM ** -0.5

    def body(q_ref, k_ref, v_ref, b_ref, o_ref, kp, vp):
        c = pl.program_id(1)

        @pl.when(c == 0)
        def _():
            _ca_load_padded(k_ref, v_ref, kp, vp, s)

        off = pl.multiple_of(c * CA_ROWS, CA_ROWS)
        w = _ca_weights(q_ref[...], kp[pl.ds(off, CA_BAND), :], b_ref[0], off, scale)
        o_ref[...] = _dot_nn(w.astype(BF16), vp[pl.ds(off, CA_BAND), :]).astype(o_ref.dtype)

    q_spec, k_spec, v_spec, b_spec = _ca_specs(n_heads, s, col0)
    return _call(
        body, [qkv, qkv, qkv, bias], name=name, grid=(n_heads, nc),
        in_specs=[q_spec, k_spec, v_spec, b_spec],
        out_specs=[pl.BlockSpec((CA_ROWS, HEAD_DIM), lambda h, c: (c, h))],
        out_shape=[jax.ShapeDtypeStruct((s, n_heads * HEAD_DIM), BF16)],
        scratch_shapes=[pltpu.VMEM((s + CA_PAD, HEAD_DIM), BF16), pltpu.VMEM((s + CA_PAD, HEAD_DIM), BF16)],
        sem=("parallel", "arbitrary"), ride=ride)[0]


def _ca_bwd(qkv, bias, dy, n_heads, col0, *, name, ride=None):
    s = qkv.shape[0]
    nc = s // CA_ROWS
    scale = HEAD_DIM ** -0.5

    def body(q_ref, k_ref, v_ref, b_ref, dy_ref, dq_ref, dk_ref, dv_ref, db_ref, kp, vp, dkp, dvp):
        c = pl.program_id(1)

        @pl.when(c == 0)
        def _():
            _ca_load_padded(k_ref, v_ref, kp, vp, s)
            dkp[...] = jnp.zeros_like(dkp)
            dvp[...] = jnp.zeros_like(dvp)
            db_ref[...] = jnp.zeros_like(db_ref)

        off = pl.multiple_of(c * CA_ROWS, CA_ROWS)
        band = pl.ds(off, CA_BAND)
        q = q_ref[...]
        dyv = dy_ref[...]
        kb = kp[band, :]
        w = _ca_weights(q, kb, b_ref[0], off, scale)
        dw = _dot_nt(dyv, vp[band, :])
        dvp[band, :] += _dot_tn(w.astype(BF16), dyv)
        dz = w * (dw - jnp.sum(w * dw, axis=1, keepdims=True))
        db_ref[0] += dz
        dzs = (dz * scale).astype(BF16)
        dq_ref[...] = _dot_nn(dzs, kb).astype(dq_ref.dtype)
        dkp[band, :] += _dot_tn(dzs, q)

        @pl.when(c == nc - 1)
        def _():
            dk_ref[...] = dkp[pl.ds(CA_PAD, s), :].astype(dk_ref.dtype)
            dv_ref[...] = dvp[pl.ds(CA_PAD, s), :].astype(dv_ref.dtype)

    q_spec, k_spec, v_spec, b_spec = _ca_specs(n_heads, s, col0)
    blk = pl.BlockSpec((CA_ROWS, HEAD_DIM), lambda h, c: (c, h))
    full = pl.BlockSpec((s, HEAD_DIM), lambda h, c: (0, h))
    sd = jax.ShapeDtypeStruct((s, n_heads * HEAD_DIM), BF16)
    return _call(
        body, [qkv, qkv, qkv, bias, dy], name=name, grid=(n_heads, nc),
        in_specs=[q_spec, k_spec, v_spec, b_spec, blk],
        out_specs=[blk, full, full, b_spec],
        out_shape=[sd, sd, sd, jax.ShapeDtypeStruct((n_heads, CA_ROWS, CA_BAND), F32)],
        scratch_shapes=[pltpu.VMEM((s + CA_PAD, HEAD_DIM), BF16), pltpu.VMEM((s + CA_PAD, HEAD_DIM), BF16),
                        pltpu.VMEM((s + CA_PAD, HEAD_DIM), F32), pltpu.VMEM((s + CA_PAD, HEAD_DIM), F32)],
        sem=("parallel", "arbitrary"), ride=ride)


EARLY = ("w_sb_out", "w_ca_out", "w_mix_out")


def _step(x, p, target, small, comm):
    w = comm.w
    d = x.shape[1]
    n_sb = w["w_sb_out"].shape[0] // HEAD_DIM
    n_ca = w["w_ca_out"].shape[0] // HEAD_DIM
    qkv_cols = 3 * HEAD_DIM * (n_sb + n_ca)
    ca_col0 = 3 * n_sb
    both = (F32, BF16)

    h1 = _rms_fwd(x, small["g_mix"], name="rms_mix")
    ffn, ple = ("w_ffn_in",), ("w_ple_gate", "w_ple_in")
    qkv = _mm(h1, w["w_in"], "nn", (BF16,), name="proj_qkv", n=qkv_cols, ride=comm.gather(EARLY, "near"))
    gates = _mm(h1, w["w_in"], "nn", (F32,), name="proj_gates", n=2 * d, b_col_off=qkv_cols,
                ride=comm.gather(ffn, "near", comm.gather(EARLY, "far"), (0, 8)))
    bias = _group_bias(_band_bias(small["rel_bias"]))
    y_sb = _sb_fwd(qkv, n_sb, 0, name="sb_fwd", ride=comm.gather(ffn, "near", comm.gather(EARLY, "pair"), (1, 8, 7)))
    y_ca = _ca_fwd(qkv, bias, n_ca, ca_col0, name="ca_fwd", ride=comm.gather(ffn, "far"))
    out = ("w_ffn_out",)
    o_sb = _mm(y_sb, w["w_sb_out"], "nn", (F32,), name="sb_out", ride=comm.gather(out, "near", part=(0, 4)))
    o_ca = _mm(y_ca, w["w_ca_out"], "nn", (F32,), name="ca_out", ride=comm.gather(out, "near", part=(1, 4)))
    merged = _gate_merge_fwd(gates, o_sb, o_ca, name="gate_merge",
                             ride=comm.gather(out, "near", comm.gather(ffn, "pair"), (2, 4)))
    x1 = _mm(merged, w["w_mix_out"], "nn", (F32,), name="mix_out", resid=x, ride=comm.gather(out, "near", part=(3, 4)))
    h2 = _rms_fwd(x1, small["g_ffn"], name="rms_ffn")
    gu = _mm(h2, w["w_ffn_in"], "nn", (BF16,), name="ffn_in", ride=comm.gather(ple, "near", comm.gather(out, "far")))
    act = _swiglu_fwd(gu, name="swiglu", ride=comm.gather(ple, "far", comm.gather(out, "pair")))
    x2 = _mm(act, w["w_ffn_out"], "nn", (F32,), name="ffn_out", resid=x1, ride=comm.gather(ple, "pair"))
    h3 = _rms_fwd(x2, small["g_ple"], name="rms_ple")
    t = _mm(h3, w["w_ple_gate"], "nn", (F32,), name="ple_gate")
    pe = _mm(p, w["w_ple_in"], "nn", (F32,), name="ple_in")
    x3 = _ple_fwd(x2, t, pe, name="ple_add")

    def halves(n, acts, dout, ride, name):
        if comm.pos is None:
            return comm.grad(n, *_mm(acts, dout, "tn", both, name=name))
        g16 = _mm(acts, dout, "tn", (BF16,), name=name + "_other", m_half=(False, comm.pos), ride=ride)
        comm.grad(n, None, g16, half=True)
        g32 = _mm(acts, dout, "tn", (F32,), name=name + "_own", m_half=(True, comm.pos), ride=comm.pair((n,)))
        comm.grad(n, g32, g16, half=True)

    gs = {}
    dx3, gs["g_final"], loss = _final_loss(x3, small["g_final"], target, name="final_loss")
    dt, dpe = _ple_bwd(dx3, t, pe, name="ple_bwd")
    comm.grad("w_ple_in", *_mm(p, dpe, "tn", both, name="dw_ple_in"))
    comm.grad("w_ple_gate", *_mm(h3, dt, "tn", both, name="dw_ple_gate"))
    ple = ("w_ple_in", "w_ple_gate")
    dh3 = _mm(dt, w["w_ple_gate"], "nt", (F32,), name="dh_ple", ride=comm.pair(ple))
    dx2, dx2_16, gs["g_ple"] = _rms_bwd(x2, small["g_ple"], dh3, dx3, name="rms_ple_bwd")
    comm.add(ple)
    comm.grad("w_ffn_out", *_mm(act, dx2_16, "tn", both, name="dw_ffn_out", ride=comm.chips(ple)))
    dact = _mm(dx2_16, w["w_ffn_out"], "nt", (F32,), name="dact", ride=comm.pair(("w_ffn_out",)))
    dgu = _swiglu_bwd(dact, gu, name="swiglu_bwd")
    comm.sum(ple)
    comm.add(("w_ffn_out",))
    comm.grad("w_ffn_in", *_mm(h2, dgu, "tn", both, name="dw_ffn_in",
                               ride=comm.share(ple, comm.chips(("w_ffn_out",)))))
    dh2 = _mm(dgu, w["w_ffn_in"], "nt", (F32,), name="dh_ffn", ride=comm.pair(("w_ffn_in",)))
    dx1, dx1_16, gs["g_ffn"] = _rms_bwd(x1, small["g_ffn"], dh2, dx2, name="rms_ffn_bwd")
    comm.add(("w_ffn_in",))
    comm.sum(("w_ffn_out",))
    comm.grad("w_mix_out", *_mm(merged, dx1_16, "tn", both, name="dw_mix_out", ride=comm.share(("w_ffn_out",))))
    dmerged = _mm(dx1_16, w["w_mix_out"], "nt", (F32,), name="dmerged", ride=comm.pair(("w_mix_out",)))
    dg_sb, dg_ca, do_sb, do_ca = _gate_merge_bwd(dmerged, gates, o_sb, o_ca, name="gate_merge_bwd")
    comm.add(("w_mix_out",))
    comm.grad("w_sb_out", *_mm(y_sb, do_sb, "tn", both, name="dw_sb_out"))
    comm.grad("w_ca_out", *_mm(y_ca, do_ca, "tn", both, name="dw_ca_out"))
    outs = ("w_sb_out", "w_ca_out")
    dy_sb = _mm(do_sb, w["w_sb_out"], "nt", (BF16,), name="dy_sb", ride=comm.pair(outs))
    dy_ca = _mm(do_ca, w["w_ca_out"], "nt", (BF16,), name="dy_ca")
    comm.add(outs)
    dq_sb, dk_sb, dv_sb = _sb_bwd(qkv, dy_sb, n_sb, 0, name="sb_bwd", ride=comm.chips(("w_ffn_in",)))
    comm.sum(("w_ffn_in",))
    late = ("w_mix_out",) + outs
    dq_ca, dk_ca, dv_ca, dbias = _ca_bwd(qkv, bias, dy_ca, n_ca, ca_col0, name="ca_bwd",
                                         ride=comm.chips(late, comm.share(("w_ffn_in",))))
    comm.sum(late)
    gs["rel_bias"] = _band_bias_grad(_group_bias_grad(dbias))
    dproj = _concat_cols([dq_sb, dk_sb, dv_sb, dq_ca, dk_ca, dv_ca, dg_sb, dg_ca], name="dproj")
    halves("w_in", h1, dproj, comm.share(late), "dw_in")
    comm.add(("w_in",))
    half = x.shape[0] // 2
    dh1 = _mm(dproj, w["w_in"], "nt", (F32,), name="dh_mix_top", rows=(0, half), ride=comm.tail(TAIL_SECOND))
    dh1 = _mm(dproj, w["w_in"], "nt", (F32,), name="dh_mix_bottom", rows=(half, half), onto=(dh1,),
              ride=comm.tail(TAIL_SECOND))
    grad_x, _, gs["g_mix"] = _rms_bwd(x, small["g_mix"], dh1, dx1, name="rms_mix_bwd", ride=comm.tail(TAIL_FIRST))
    return loss, grad_x, gs


def _position():
    x, y, c = lax.axis_index("x"), lax.axis_index("y"), lax.axis_index("c")
    chips = [(1 - x, y), (x, 1 - y), (1 - x, 1 - y)]
    return x, y, c, chips


def _aligned(v, m):
    return v if isinstance(v, int) else pl.multiple_of(v, m)


def _piece_dims(shape, axis):
    k, n = shape
    return (k // 2, n // N_CHIPS) if axis == 1 else (k // N_CHIPS // 2, n)


def _piece(ref, shape, axis, j, h, part=(0, 1)):
    pr, pc = _piece_dims(shape, axis)
    nr = pr // part[1] * (part[2] if len(part) > 2 else 1)
    r0 = part[0] * (pr // part[1])
    if axis == 1:
        return ref.at[pl.ds(_aligned(h * pr + r0, 16), nr), pl.ds(_aligned(j * pc, 128), pc)]
    return ref.at[pl.ds(_aligned((2 * j + h) * pr + r0, 16), nr), :]


def _shard_half(ref, h):
    rows = ref.shape[0] // 2
    return ref.at[pl.ds(_aligned(h * rows, 16), rows), :]


def _remote(src, dst, send_sems, recv_sems, k, to):
    return pltpu.make_async_remote_copy(src_ref=src, dst_ref=dst, send_sem=send_sems.at[k],
                                        recv_sem=recv_sems.at[k], device_id=to, device_id_type=MESH)


def _prefetch_call(body, scalars, ins, in_specs, out_shape, out_specs, grid, *, name, ride=None):
    single = not isinstance(out_shape, (list, tuple))
    outs = _call(body, ins, name=name, grid=grid, in_specs=in_specs,
                 out_specs=[out_specs] if single else out_specs, out_shape=[out_shape] if single else out_shape,
                 sem=("parallel",) * len(grid), ride=ride, scalars=scalars)
    return outs[0] if single else outs


def _slab_tiles(pr, pc):
    tc = pc if pc <= 4096 else _pick(pc, (2048, 1024, 512, 256, 128))
    tr = next(t for t in (1024, 512, 256, 128, 64, 32, 16) if pr % t == 0 and t * tc <= 512 * 1024)
    return tr, tc


def _cast_place(w, axis, pos, *, name, ride=None):
    ks, ns = w.shape
    shape = (ks, ns * N_CHIPS) if axis == 1 else (ks * N_CHIPS, ns)
    tr, tc = _slab_tiles(ks, ns)
    nr, nc = ks // tr, ns // tc

    def body(pos_ref, w_ref, o_ref):
        o_ref[...] = w_ref[...].astype(o_ref.dtype)

    if axis == 1:
        out_map = lambda i, j, pos_ref: (i, pos_ref[0] * nc + j)
    else:
        out_map = lambda i, j, pos_ref: (pos_ref[0] * nr + i, j)
    return _prefetch_call(body, pos, [w], [pl.BlockSpec((tr, tc), lambda i, j, pos_ref: (i, j))],
                          jax.ShapeDtypeStruct(shape, BF16), pl.BlockSpec((tr, tc), out_map), (nr, nc), name=name, ride=ride)


def _run(ride, *, name):
    if ride is None:
        return

    def body(o_ref):
        o_ref[...] = jnp.zeros_like(o_ref)

    _call(body, [], name=name, grid=(1,), in_specs=[], out_specs=[pl.BlockSpec((8, 128), lambda i: (0, 0))],
          out_shape=[jax.ShapeDtypeStruct((8, 128), F32)], ride=ride)


def _ride_gather(ride, w, n, axis, stage, part=(0, 1)):
    shape = w[n].shape
    piece = functools.partial(_piece, shape=shape, axis=axis)
    span = part[2] if len(part) > 2 else 1
    halves = [(2 * part[0] + t * span, 2 * part[1], span) for t in range(2)]

    def copies(ins, outs, send_sems, recv_sems, arriving):
        x, y, c, chips = _position()
        me, (xn, yn, dn) = 2 * x + y, [2 * px + py for px, py in chips]
        if stage == "near":
            plan = [(me, c, part, (1 - x, y, c), xn, c, part), (me, c, part, (x, 1 - y, c), yn, c, part)]
        elif stage == "far":
            plan = [(yn, c, halves[1], (1 - x, y, c), dn, c, halves[1]), (xn, c, halves[0], (x, 1 - y, c), dn, c, halves[0])]
        else:
            plan = [(j, c, part, (x, y, 1 - c), j, 1 - c, part) for j in (xn, yn, dn)]
        out = []
        for k, (chip, h, rows, to, from_chip, from_h, from_rows) in enumerate(plan):
            if arriving:
                lands = piece(outs[0], j=from_chip, h=from_h, part=from_rows)
                out.append(_remote(lands, lands, send_sems, recv_sems, k, to))
            else:
                out.append(_remote(piece(ins[0], j=chip, h=h, part=rows), piece(outs[0], j=chip, h=h, part=rows),
                                   send_sems, recv_sems, k, to))
        return out

    def start(*refs):
        for cp in copies(*refs, arriving=False):
            cp.start()

    def finish(*refs):
        for cp in copies(*refs, arriving=True):
            cp.wait_recv()
        for cp in copies(*refs, arriving=False):
            cp.wait_send()

    ride.add([w[n]], [jax.ShapeDtypeStruct(shape, w[n].dtype)], {0: 0}, 3, start, finish,
             lambda outs: w.__setitem__(n, outs[0]))


def _ride_pair(ride, st, axis):
    shape = st["g16"].shape
    pr, pc = (shape[0], shape[1] // N_CHIPS) if st.get("half") else _piece_dims(shape, axis)

    def copies(ins, outs, send_sems, recv_sems):
        x, y, c, _ = _position()
        if st.get("half"):
            pieces = [ins[0].at[:, pl.ds(j * pc, pc)] for j in range(N_CHIPS)]
        else:
            pieces = [_piece(ins[0], shape, axis, j, 1 - c) for j in range(N_CHIPS)]
        return [_remote(pieces[j], outs[0].at[j], send_sems, recv_sems, j, (x, y, 1 - c)) for j in range(N_CHIPS)]

    def start(*refs):
        for cp in copies(*refs):
            cp.start()

    def finish(*refs):
        for cp in copies(*refs):
            cp.wait()

    ride.add([st["g16"]], [jax.ShapeDtypeStruct((N_CHIPS, pr, pc), BF16)], {}, N_CHIPS, start, finish,
             lambda outs: st.__setitem__("sib", outs[0]))


def _ride_chips(ride, st, rows=None):
    _, pr, pc = st["s16"].shape
    r0, nr = (0, pr) if rows is None else rows

    def copies(ins, outs, send_sems, recv_sems):
        x, y, c, chips = _position()
        return [_remote(ins[0].at[2 * px + py, pl.ds(r0, nr), :], outs[0].at[k, pl.ds(r0, nr), :],
                        send_sems, recv_sems, k, (px, py, c)) for k, (px, py) in enumerate(chips)]

    def start(*refs):
        for cp in copies(*refs):
            cp.start()

    def finish(*refs):
        for cp in copies(*refs):
            cp.wait()

    ins, aliases = ([st["s16"], st["recv"]], {1: 0}) if "recv" in st else ([st["s16"]], {})
    ride.add(ins, [jax.ShapeDtypeStruct((3, pr, pc), BF16)], aliases, 3, start, finish,
             lambda outs: st.__setitem__("recv", outs[0]))


def _ride_share(ride, st):
    def sent(ins, outs, send_sems, recv_sems):
        x, y, c, _ = _position()
        return _remote(_shard_half(ins[0], c), _shard_half(outs[0], c), send_sems, recv_sems, 0, (x, y, 1 - c))

    def landed(ins, outs, send_sems, recv_sems):
        x, y, c, _ = _position()
        other = _shard_half(outs[0], 1 - c)
        return _remote(other, other, send_sems, recv_sems, 0, (x, y, 1 - c))

    def start(*refs):
        sent(*refs).start()

    def finish(*refs):
        landed(*refs).wait_recv()
        sent(*refs).wait_send()

    ride.add([st["shard"]], [jax.ShapeDtypeStruct(st["shard"].shape, F32)], {0: 0}, 1, start, finish,
             lambda outs: st.__setitem__("g", outs[0]))


def _piece_block(axis, nr, nc, chip, half=False):
    if half:
        return lambda *a: (a[-3], (a[0] if chip is None else chip(a[-1])) * nc + a[-2])
    if axis == 1:
        return lambda *a: ((a[-1][1] * nr + a[-3]), (a[0] if chip is None else chip(a[-1])) * nc + a[-2])
    return lambda *a: ((2 * (a[0] if chip is None else chip(a[-1])) + a[-1][1]) * nr + a[-3], a[-2])


def _pair_add(g32, sib, axis, pos, *, name, half=False):
    _, pr, pc = sib.shape
    tr, tc = _slab_tiles(pr, pc)
    nr, nc = pr // tr, pc // tc

    def body(pos_ref, g_ref, b_ref, o16_ref):
        o16_ref[0] = (g_ref[...] + b_ref[0].astype(F32)).astype(o16_ref.dtype)

    blk = pl.BlockSpec((1, tr, tc), lambda j, i, k, pos_ref: (j, i, k))
    return _prefetch_call(body, pos, [g32, sib], [pl.BlockSpec((tr, tc), _piece_block(axis, nr, nc, None, half)), blk],
                          jax.ShapeDtypeStruct(sib.shape, BF16), blk, (N_CHIPS, nr, nc), name=name)


def _chip_sum(g32, sib, recv, axis, pos, *, name, half=False):
    _, pr, pc = sib.shape
    tr, tc = _slab_tiles(pr, pc)
    nr, nc = pr // tr, pc // tc

    def body(pos_ref, g_ref, b_ref, r_ref, o_ref):
        pair = g_ref[...] + b_ref[0].astype(F32)
        o_ref[...] = ((pair + r_ref[0].astype(F32)) + r_ref[1].astype(F32)) + r_ref[2].astype(F32)

    return _prefetch_call(
        body, pos, [g32, sib, recv],
        [pl.BlockSpec((tr, tc), _piece_block(axis, nr, nc, lambda pos_ref: pos_ref[0], half)),
         pl.BlockSpec((1, tr, tc), lambda i, k, pos_ref: (pos_ref[0], i, k)),
         pl.BlockSpec((3, tr, tc), lambda i, k, pos_ref: (0, i, k))],
        jax.ShapeDtypeStruct((2 * pr, pc), F32),
        pl.BlockSpec((tr, tc), lambda i, k, pos_ref: (pos_ref[1] * nr + i, k)), (nr, nc), name=name)


class _Comm:
    def __init__(self, pos, w):
        self.pos, self.w, self.st = pos, w, {n: {} for n, _ in BIG}

    def gather(self, names, stage, ride=None, part=(0, 1)):
        ride = _Ride() if ride is None else ride
        for n in names:
            _ride_gather(ride, self.w, n, AXIS[n], stage, part)
        return ride

    def grad(self, n, g32, g16, half=False):
        self.st[n].update(g32=g32, g16=g16, half=half)

    def pair(self, names, ride=None):
        ride = _Ride() if ride is None else ride
        for n in names:
            _ride_pair(ride, self.st[n], AXIS[n])
        return ride

    def add(self, names):
        for n in names:
            st = self.st[n]
            st["s16"] = _pair_add(st["g32"], st["sib"], AXIS[n], self.pos, name="rs_add_" + n, half=st["half"])

    def chips(self, names, ride=None, rows=None):
        ride = _Ride() if ride is None else ride
        for n in names:
            _ride_chips(ride, self.st[n], rows)
        return ride

    def sum(self, names):
        for n in names:
            st = self.st[n]
            st["shard"] = _chip_sum(st["g32"], st["sib"], st["recv"], AXIS[n], self.pos, name="rs_sum_" + n,
                                    half=st["half"])

    def share(self, names, ride=None):
        ride = _Ride() if ride is None else ride
        for n in names:
            _ride_share(ride, self.st[n])
        return ride

    def tail(self, count):
        st = self.st["w_in"]
        rows, at = st["s16"].shape[1], st.get("at", 0)
        st["at"] = at + count
        return self.chips(("w_in",), rows=(at * rows // TAIL_PARTS, count * rows // TAIL_PARTS))

    def tail_rest(self):
        return self.tail(TAIL_PARTS - self.st["w_in"].get("at", 0))

    def result(self, n):
        return self.st[n]["g"]


class _NoComm:
    pos = None

    def __init__(self, w):
        self.w, self.st = w, {}

    def grad(self, n, g32, g16, half=False):
        self.st[n] = (g32, g16)

    def result(self, n):
        return self.st[n]

    def add(self, names):
        pass

    sum = add

    def gather(self, names, *args, **kwargs):
        return None

    pair = chips = share = tail = gather


def _small_all_reduce(vec, *, name):
    r = vec.shape[0]

    def body(vec_ref, out_ref, slots, send_sems, recv_sems):
        x, y, c, _ = _position()
        me = 4 * x + 2 * y + c
        slots[me] = vec_ref[...]
        sends = []
        for k in range(1, 8):
            to = (x ^ (k >> 2), y ^ ((k >> 1) & 1), c ^ (k & 1))
            cp = _remote(slots.at[me], slots.at[me], send_sems, recv_sems, k - 1, to)
            cp.start()
            sends.append(cp)
        for k in range(1, 8):
            frm = 4 * (x ^ (k >> 2)) + 2 * (y ^ ((k >> 1) & 1)) + (c ^ (k & 1))
            _remote(slots.at[frm], slots.at[frm], send_sems, recv_sems, k - 1, (x, y, c)).wait_recv()
        for cp in sends:
            cp.wait_send()
        total = slots[0]
        for d in range(1, 8):
            total = total + slots[d]
        out_ref[...] = total

    return pl.pallas_call(
        body, name=name,
        in_specs=[pl.BlockSpec(memory_space=pltpu.VMEM)], out_specs=pl.BlockSpec(memory_space=pltpu.VMEM),
        out_shape=jax.ShapeDtypeStruct((r, 128), F32),
        scratch_shapes=[pltpu.VMEM((8, r, 128), F32), pltpu.SemaphoreType.DMA((7,)), pltpu.SemaphoreType.DMA((7,))],
    )(vec)


SC_TILES = 32
SC_LANES = 16
SC_TILE_BUDGET = 400 * 1024


def _adamw_update(wv, gv, mv, vv):
    nm = ADAM_B1 * mv + (1.0 - ADAM_B1) * gv
    nv = ADAM_B2 * vv + (1.0 - ADAM_B2) * (gv * gv)
    m_hat = nm / (1.0 - ADAM_B1 ** ADAM_STEP)
    v_hat = nv / (1.0 - ADAM_B2 ** ADAM_STEP)
    return -ADAM_LR * (m_hat / (jnp.sqrt(v_hat) + ADAM_EPS) + ADAM_WD * wv), nm, nv


def _adamw_sc(w, g, m, v, *, name):
    r, c = w.shape
    groups = r // 8
    per_tile = -(-groups // SC_TILES)
    cb = c if 4 * 8 * c * 4 <= SC_TILE_BUDGET else _pick(c, (2048, 1024, 512, 256, 128))

    def body(w_hbm, g_hbm, m_hbm, v_hbm, go_hbm, d_hbm, nm_hbm, nv_hbm, wb, gb, mb, vb):
        tile = lax.axis_index("sc_tile") * 2 + lax.axis_index("sc_core")

        def update(group):
            for c0 in range(0, c, cb):
                at = (pl.ds(group * 8, 8), pl.ds(c0, cb))
                for hbm, buf in ((w_hbm, wb), (g_hbm, gb), (m_hbm, mb), (v_hbm, vb)):
                    pltpu.sync_copy(hbm.at[at], buf)
                pltpu.sync_copy(gb, go_hbm.at[at])

                @pl.loop(0, 8)
                def _(rr):
                    @pl.loop(0, cb, step=SC_LANES)
                    def _(i):
                        lanes = (rr, pl.ds(i, SC_LANES))
                        wb[lanes], mb[lanes], vb[lanes] = _adamw_update(wb[lanes], gb[lanes], mb[lanes], vb[lanes])

                for buf, hbm in ((wb, d_hbm), (mb, nm_hbm), (vb, nv_hbm)):
                    pltpu.sync_copy(buf, hbm.at[at])

        @pl.loop(0, per_tile)
        def _(k):
            group = k * SC_TILES + tile
            if groups % SC_TILES:
                pl.when(group < groups)(lambda: update(group))
            else:
                update(group)

    sd = jax.ShapeDtypeStruct((r, c), F32)
    return pl.kernel(body, name=name, out_type=[sd, sd, sd, sd],
                     mesh=plsc.VectorSubcoreMesh(core_axis_name="sc_core", subcore_axis_name="sc_tile"),
                     scratch_types=[pltpu.VMEM((8, cb), F32)] * 4)(w, g, m, v)


def _adamw(w, g, m, v, *, name, ride=None):
    r, c = w.shape
    tc = c if c <= 4096 else _pick(c, (2048, 1024, 512, 256, 128))
    tr = next(t for t in (512, 256, 128, 64, 32, 16, 8) if r % t == 0 and t * tc <= 256 * 1024)

    def body(w_ref, g_ref, m_ref, v_ref, go_ref, d_ref, nm_ref, nv_ref):
        go_ref[...] = g_ref[...]
        d_ref[...], nm_ref[...], nv_ref[...] = _adamw_update(w_ref[...], g_ref[...], m_ref[...], v_ref[...])

    blk = ((tr, tc), lambda i, j: (i, j))
    sd = jax.ShapeDtypeStruct((r, c), F32)
    return _ew(body, [w, g, m, v], [blk] * 4, [sd] * 4, [blk] * 4, (r // tr, c // tc), name=name, ride=ride)


BIG = (("w_in", 1), ("w_sb_out", 1), ("w_ca_out", 1), ("w_mix_out", 0), ("w_ffn_in", 1), ("w_ffn_out", 0),
       ("w_ple_in", 1), ("w_ple_gate", 0))
AXIS = dict(BIG)
HEAD_PARTS = 8
HEAD_HOSTS = ("w_ffn_in", "w_ffn_out")
TAIL_PARTS = 16
TAIL_SECOND = 5
TAIL_FIRST = 2
ON_SPARSECORE = tuple(n for n, _ in BIG if n != "w_in")
SMALL = ("rel_bias", "g_mix", "g_ffn", "g_ple", "g_final")
ORDER = ("w_in", "w_sb_out", "w_ca_out", "w_mix_out", "rel_bias", "g_mix", "g_ffn", "g_ple", "g_final",
         "w_ffn_in", "w_ffn_out", "w_ple_in", "w_ple_gate")


def _pack(parts):
    flat = jnp.concatenate([a.reshape(-1) for a in parts])
    rows = -(-flat.shape[0] // 1024) * 8
    return jnp.pad(flat, (0, rows * 128 - flat.shape[0])).reshape(rows, 128)


def _unpack(packed, like):
    flat, out, at = packed.reshape(-1), [], 0
    for a in like:
        out.append(flat[at:at + a.size].reshape(a.shape))
        at += a.size
    return out


def kernel(x, p, w_in, w_sb_out, w_ca_out, w_mix_out, rel_bias, g_mix, g_ffn, g_ple, g_final, w_ffn_in, w_ffn_out, w_ple_in, w_ple_gate, loss_target, m_w_in, m_w_sb_out, m_w_ca_out, m_w_mix_out, m_rel_bias, m_g_mix, m_g_ffn, m_g_ple, m_g_final, m_w_ffn_in, m_w_ffn_out, m_w_ple_in, m_w_ple_gate, v_w_in, v_w_sb_out, v_w_ca_out, v_w_mix_out, v_rel_bias, v_g_mix, v_g_ffn, v_g_ple, v_g_final, v_w_ffn_in, v_w_ffn_out, v_w_ple_in, v_w_ple_gate):
    weights = dict(w_in=w_in, w_sb_out=w_sb_out, w_ca_out=w_ca_out, w_mix_out=w_mix_out, rel_bias=rel_bias,
                   g_mix=g_mix, g_ffn=g_ffn, g_ple=g_ple, g_final=g_final, w_ffn_in=w_ffn_in,
                   w_ffn_out=w_ffn_out, w_ple_in=w_ple_in, w_ple_gate=w_ple_gate)
    m_in = dict(w_in=m_w_in, w_sb_out=m_w_sb_out, w_ca_out=m_w_ca_out, w_mix_out=m_w_mix_out, rel_bias=m_rel_bias,
                g_mix=m_g_mix, g_ffn=m_g_ffn, g_ple=m_g_ple, g_final=m_g_final, w_ffn_in=m_w_ffn_in,
                w_ffn_out=m_w_ffn_out, w_ple_in=m_w_ple_in, w_ple_gate=m_w_ple_gate)
    v_in = dict(w_in=v_w_in, w_sb_out=v_w_sb_out, w_ca_out=v_w_ca_out, w_mix_out=v_w_mix_out, rel_bias=v_rel_bias,
                g_mix=v_g_mix, g_ffn=v_g_ffn, g_ple=v_g_ple, g_final=v_g_final, w_ffn_in=v_w_ffn_in,
                w_ffn_out=v_w_ffn_out, w_ple_in=v_w_ple_in, w_ple_gate=v_w_ple_gate)

    pos = jnp.stack([2 * lax.axis_index("x") + lax.axis_index("y"), lax.axis_index("c")]).astype(jnp.int32)
    comm = _Comm(pos, {"w_in": _cast_place(w_in[0], AXIS["w_in"], pos, name="cast_w_in")})
    at = 0
    for n in HEAD_HOSTS:
        ride = comm.gather(("w_in",), "near", part=(at, HEAD_PARTS))
        comm.w[n] = _cast_place(weights[n][0], AXIS[n], pos, name="cast_" + n, ride=ride)
        at += 1
    for n, axis in BIG:
        if n not in comm.w:
            comm.w[n] = _cast_place(weights[n][0], axis, pos, name="cast_" + n)
    _run(comm.gather(("w_in",), "near", part=(at, HEAD_PARTS, HEAD_PARTS - at)), name="gather_w_in_near")
    _run(comm.gather(("w_in",), "far"), name="gather_w_in_far")
    _run(comm.gather(("w_in",), "pair"), name="gather_w_in_pair")
    small = dict(rel_bias=rel_bias[0], g_mix=g_mix, g_ffn=g_ffn, g_ple=g_ple, g_final=g_final.reshape(1, -1))
    loss, grad_x, gs = _step(x[0], p[0, 0], loss_target[0], small, comm)

    grads, delta, new_m, new_v = {}, {}, {}, {}
    for n in [n for n, _ in BIG if n != "w_in"] + ["w_in"]:
        if n == "w_in":
            _run(comm.tail_rest(), name="rs_chips_w_in")
            comm.sum(("w_in",))
            _run(comm.share(("w_in",)), name="rs_share_w_in")
        update = _adamw_sc if n in ON_SPARSECORE else _adamw
        g, d, nm, nv = update(weights[n][0], comm.result(n), m_in[n][0], v_in[n][0], name="adamw_" + n)
        grads[n], delta[n], new_m[n], new_v[n] = g[None], d[None], nm[None], nv[None]

    like = [weights[n] for n in SMALL]
    reduced = _small_all_reduce(_pack([gs[n] for n in SMALL] + [loss[:, :1]]), name="small_all_reduce")
    g_small = _unpack(reduced, like + [loss[:, :1]])
    total_loss = g_small[-1].reshape(())
    g_packed = _pack(g_small[:-1])
    _, d_s, m_s, v_s = _adamw(_pack(like), g_packed, _pack([m_in[n] for n in SMALL]), _pack([v_in[n] for n in SMALL]),
                           name="adamw_small")
    for n, g, d, nm, nv in zip(SMALL, g_small[:-1], _unpack(d_s, like), _unpack(m_s, like), _unpack(v_s, like)):
        grads[n], delta[n], new_m[n], new_v[n] = g, d, nm, nv

    return (total_loss, grad_x[None], *[grads[n] for n in ORDER], *[delta[n] for n in ORDER],
            *[new_m[n] for n in ORDER], *[new_v[n] for n in ORDER])
```

```python
import functools
import math

import jax
import jax.numpy as jnp
from jax import lax
from jax.experimental import pallas as pl
from jax.experimental.pallas import tpu as pltpu
from jax.experimental.pallas import tpu_sc as plsc

F32 = jnp.float32
BF16 = jnp.bfloat16

HEAD_DIM = 128
CHUNK = 64
LEFT_CHUNKS = 8
REL_CLIP = 128
N_REL = REL_CLIP + CHUNK
BAND = (LEFT_CHUNKS + 2) * CHUNK
CA_PER_STEP = 4
CA_ROWS = CA_PER_STEP * CHUNK
CA_BAND = BAND + CA_PER_STEP * CHUNK
CA_PAD = BAND
SB_BLOCK = 128
SB_KEYS = 512
SB_GROUPS = SB_KEYS // SB_BLOCK
SB_ROWS = SB_KEYS
EPS = 1e-6
NEG = -1e30

ADAM_LR = 0.001
ADAM_B1 = 0.9
ADAM_B2 = 0.999
ADAM_EPS = 1e-08
ADAM_WD = 0.01
ADAM_STEP = 10

VMEM_LIMIT = 48 * 1024 * 1024
MM_VMEM_BUDGET = 36 * 1024 * 1024
V7X_HBM_BYTES_PER_S = 3.7e12
GRID_STEP_S = 0.35e-6
MESH = pl.DeviceIdType.MESH
N_CHIPS = 4


def _pick(dim, prefs):
    for t in prefs:
        if dim % t == 0:
            return t
    raise ValueError(f"no tile for {dim}")


def _cparams(sem=None):
    return pltpu.CompilerParams(dimension_semantics=sem, vmem_limit_bytes=VMEM_LIMIT)


def _sigmoid(v):
    return 1.0 / (1.0 + jnp.exp(-v))


def _dot(a, b, dims):
    return lax.dot_general(a, b, (dims, ((), ())), preferred_element_type=F32)


def _dot_nn(a, b):
    return _dot(a, b, ((1,), (0,)))


def _dot_nt(a, b):
    return _dot(a, b, ((1,), (1,)))


def _dot_tn(a, b):
    return _dot(a, b, ((0,), (0,)))


HBM = pl.BlockSpec(memory_space=pltpu.HBM)


class _Ride:
    def __init__(self):
        self.items = []

    def add(self, ins, outs, aliases, n_sems, start, finish, sink):
        self.items.append((ins, outs, aliases, n_sems, start, finish, sink))


def _call(body, args, *, name, grid, in_specs, out_specs, out_shape, scratch_shapes=(), sem=None, ride=None,
          scalars=None, onto=()):
    items = ride.items if ride is not None else []
    if onto:
        args, in_specs = list(args) + list(onto), list(in_specs) + [HBM] * len(onto)
        inner, body = body, lambda *refs: inner(*refs[:len(args) - len(onto)], *refs[len(args):])
    n_in, n_out, n_scr = len(args), len(out_shape), len(scratch_shapes)
    r_ins = [a for it in items for a in it[0]]
    r_outs = [o for it in items for o in it[1]]
    updated = [id(it[0][i]) for it in items for i in it[2]]
    assert len(set(updated)) == len(updated), "one call may update a buffer in place only once"
    aliases, a, b = {n_in - len(onto) + t: t for t in range(len(onto))}, n_in, n_out
    for it in items:
        aliases.update({a + i: b + o for i, o in it[2].items()})
        a, b = a + len(it[0]), b + len(it[1])
    sems = [pltpu.SemaphoreType.DMA((it[3],)) for it in items for _ in range(2)]

    def wrapped(*refs):
        head, refs = (refs[:1], refs[1:]) if scalars is not None else ((), refs)
        ins, rin = refs[:n_in], refs[n_in:n_in + len(r_ins)]
        at = n_in + len(r_ins)
        outs, rout = refs[at:at + n_out], refs[at + n_out:at + n_out + len(r_outs)]
        at += n_out + len(r_outs)
        scr, rsem = refs[at:at + n_scr], refs[at + n_scr:]

        def each(which):
            a = b = 0
            for q, it in enumerate(items):
                it[which](rin[a:a + len(it[0])], rout[b:b + len(it[1])], rsem[2 * q], rsem[2 * q + 1])
                a, b = a + len(it[0]), b + len(it[1])

        if items:
            ids = [pl.program_id(d) for d in range(len(grid))]
            first = functools.reduce(jnp.logical_and, [i == 0 for i in ids])
            last = functools.reduce(jnp.logical_and, [i == g - 1 for i, g in zip(ids, grid)])
            pl.when(first)(lambda: each(4))
        body(*head, *ins, *outs, *scr)
        if items:
            pl.when(last)(lambda: each(5))

    specs = dict(grid=grid, in_specs=list(in_specs) + [HBM] * len(r_ins),
                 out_specs=list(out_specs) + [HBM] * len(r_outs), scratch_shapes=list(scratch_shapes) + sems)
    if scalars is not None:
        specs = dict(grid_spec=pltpu.PrefetchScalarGridSpec(num_scalar_prefetch=1, **specs))
        aliases = {i + 1: o for i, o in aliases.items()}
    res = pl.pallas_call(
        wrapped, name=name, **specs,
        out_shape=list(out_shape) + r_outs,
        input_output_aliases=aliases,
        compiler_params=_cparams(("arbitrary",) * len(grid) if items else sem),
    )(*(() if scalars is None else (scalars,)), *args, *r_ins)
    b = n_out
    for it in items:
        it[6](res[b:b + len(it[1])])
        b += len(it[1])
    return list(res[:n_out])


def _mm_tiles(m, n_align, n, k, a_bytes, b_bytes, out_bytes):
    best = None
    tks = sorted({t for t in (k, k // 2, k // 4, 2048, 1024, 512, 256, 128) if t <= k and k % t == 0 and t % 128 == 0})
    for tm in (t for t in (2048, 1024, 512, 256, 128) if m % t == 0):
        for tn in (t for t in (2048, 1024, 512, 256, 128) if n_align % t == 0):
            for tk in tks:
                nk = k // tk
                vmem = 2 * (tm * tk * a_bytes + tk * tn * b_bytes + tm * tn * out_bytes) + tm * tn * 4
                if vmem > MM_VMEM_BUDGET:
                    continue
                traffic = m * k * a_bytes * (n // tn if nk > 1 else 1) + k * n * b_bytes * (m // tm)
                traffic += tm * tk * a_bytes + tk * tn * b_bytes + tm * tn * out_bytes
                traffic += m * n * 4 * nk if nk > 1 else 0
                cost = traffic / V7X_HBM_BYTES_PER_S + (m // tm) * (n // tn) * nk * GRID_STEP_S
                if best is None or cost < best[0]:
                    best = (cost, tm, tn, tk)
    return best[1:]


def _mm(a, b, mode, out_dtypes, *, name, n=None, b_col_off=0, resid=None, ride=None, rows=None, onto=(), m_half=None):
    if mode == "nn":
        m, k = a.shape
        n = b.shape[1] if n is None else n
    elif mode == "nt":
        m, k = a.shape
        n = b.shape[0]
    else:
        k, m = a.shape
        n = b.shape[1]
    if m_half is not None:
        m //= 2
    m_all, (row0, m) = m, (0, m) if rows is None else rows
    n_out = len(out_dtypes)
    has_resid = resid is not None
    out_bytes = sum(jnp.dtype(dt).itemsize for dt in out_dtypes) + (4 if has_resid else 0)
    tm, tn, tk = _mm_tiles(math.gcd(m, row0) if row0 else m, math.gcd(n, b_col_off) if b_col_off else n, n, k,
                           a.dtype.itemsize, b.dtype.itemsize, out_bytes)
    nk = k // tk
    boff, roff = b_col_off // tn, row0 // tm
    dot = {"nn": _dot_nn, "nt": _dot_nt, "tn": _dot_tn}[mode]
    if m_half is None:
        half = lambda: 0
    else:
        half = lambda pos_ref: (pos_ref[1] if m_half[0] else 1 - pos_ref[1]) * (m // tm)

    def body(*refs):
        refs = refs[m_half is not None:]
        a_ref, b_ref = refs[0], refs[1]
        r_ref = refs[2] if has_resid else None
        o_refs = refs[2 + has_resid: 2 + has_resid + n_out]

        def finish(r):
            if has_resid:
                r = r + r_ref[...]
            for o_ref in o_refs:
                o_ref[...] = r.astype(o_ref.dtype)

        part = dot(a_ref[...].astype(BF16), b_ref[...].astype(BF16))
        if nk == 1:
            finish(part)
            return
        acc_ref = refs[-1]
        kk = pl.program_id(2)

        @pl.when(kk == 0)
        def _():
            acc_ref[...] = part

        @pl.when(kk > 0)
        def _():
            acc_ref[...] += part

        @pl.when(kk == nk - 1)
        def _():
            finish(acc_ref[...])

    if mode == "nn":
        a_spec = pl.BlockSpec((tm, tk), lambda i, j, kk, *_: (i + roff, kk))
        b_spec = pl.BlockSpec((tk, tn), lambda i, j, kk, *_: (kk, j + boff))
    elif mode == "nt":
        a_spec = pl.BlockSpec((tm, tk), lambda i, j, kk, *_: (i + roff, kk))
        b_spec = pl.BlockSpec((tn, tk), lambda i, j, kk, *_: (j, kk))
    else:
        a_spec = pl.BlockSpec((tk, tm), lambda i, j, kk, *pos: (kk, i + half(*pos)))
        b_spec = pl.BlockSpec((tk, tn), lambda i, j, kk, *_: (kk, j))
    o_spec = pl.BlockSpec((tm, tn), lambda i, j, kk, *_: (i + roff, j))
    in_specs = [a_spec, b_spec] + ([o_spec] if has_resid else [])
    args = [a, b] + ([resid] if has_resid else [])
    outs = _call(
        body, args, name=name,
        grid=(m // tm, n // tn, nk),
        in_specs=in_specs,
        out_specs=[o_spec] * n_out,
        out_shape=[jax.ShapeDtypeStruct((m_all, n), dt) for dt in out_dtypes],
        scratch_shapes=[pltpu.VMEM((tm, tn), F32)] if nk > 1 else [],
        sem=("parallel", "parallel", "arbitrary"), ride=ride, onto=onto,
        scalars=None if m_half is None else m_half[1])
    return outs[0] if n_out == 1 else tuple(outs)


def _row_tile(s):
    return _pick(s, (256, 128))


def _rms_fwd(x, g, *, name, ride=None):
    s, d = x.shape
    tr = _row_tile(s)

    def body(x_ref, g_ref, o_ref):
        xv = x_ref[...]
        r = lax.rsqrt(jnp.mean(xv * xv, axis=1, keepdims=True) + EPS)
        o_ref[...] = (xv * r * g_ref[...]).astype(o_ref.dtype)

    return _call(
        body, [x, g], name=name, grid=(s // tr,),
        in_specs=[pl.BlockSpec((tr, d), lambda i: (i, 0)), pl.BlockSpec((1, d), lambda i: (0, 0))],
        out_specs=[pl.BlockSpec((tr, d), lambda i: (i, 0))],
        out_shape=[jax.ShapeDtypeStruct((s, d), BF16)], sem=("parallel",), ride=ride)[0]


def _rms_bwd(x, g, dh, dres, *, name, ride=None):
    s, d = x.shape
    tr = _row_tile(s)

    def body(x_ref, g_ref, dh_ref, dres_ref, dx_ref, dx16_ref, dg_ref):
        i = pl.program_id(0)
        xv = x_ref[...]
        r = lax.rsqrt(jnp.mean(xv * xv, axis=1, keepdims=True) + EPS)
        xhat = xv * r
        dhv = dh_ref[...]
        dxhat = dhv * g_ref[...]
        proj = jnp.mean(dxhat * xhat, axis=1, keepdims=True)
        dx = dres_ref[...] + r * (dxhat - xhat * proj)
        dx_ref[...] = dx
        dx16_ref[...] = dx.astype(dx16_ref.dtype)

        @pl.when(i == 0)
        def _():
            dg_ref[...] = jnp.zeros_like(dg_ref)

        dg_ref[...] += jnp.sum(dhv * xhat, axis=0, keepdims=True)

    row = pl.BlockSpec((tr, d), lambda i: (i, 0))
    vec = pl.BlockSpec((1, d), lambda i: (0, 0))
    return _call(
        body, [x, g, dh, dres], name=name, grid=(s // tr,),
        in_specs=[row, vec, row, row],
        out_specs=[row, row, vec],
        out_shape=[jax.ShapeDtypeStruct((s, d), F32), jax.ShapeDtypeStruct((s, d), BF16),
                   jax.ShapeDtypeStruct((1, d), F32)],
        sem=("arbitrary",), ride=ride)


def _final_loss(x, g, target, *, name):
    s, d = x.shape
    tr = _row_tile(s)

    def body(x_ref, g_ref, t_ref, dx_ref, dg_ref, loss_ref):
        i = pl.program_id(0)
        xv = x_ref[...]
        gv = g_ref[...]
        r = lax.rsqrt(jnp.mean(xv * xv, axis=1, keepdims=True) + EPS)
        xhat = xv * r
        err = xhat * gv - t_ref[...]
        dy = err * (1.0 / d)
        dxhat = dy * gv
        proj = jnp.mean(dxhat * xhat, axis=1, keepdims=True)
        dx_ref[...] = r * (dxhat - xhat * proj)

        @pl.when(i == 0)
        def _():
            dg_ref[...] = jnp.zeros_like(dg_ref)
            loss_ref[...] = jnp.zeros_like(loss_ref)

        dg_ref[...] += jnp.sum(dy * xhat, axis=0, keepdims=True)
        part = 0.5 * jnp.sum(jnp.mean(err * err, axis=1, keepdims=True), axis=0, keepdims=True)
        loss_ref[...] += jnp.broadcast_to(part, loss_ref.shape)

    row = pl.BlockSpec((tr, d), lambda i: (i, 0))
    vec = pl.BlockSpec((1, d), lambda i: (0, 0))
    return pl.pallas_call(
        body, name=name, grid=(s // tr,),
        in_specs=[row, vec, row],
        out_specs=[row, vec, pl.BlockSpec((1, 128), lambda i: (0, 0))],
        out_shape=[jax.ShapeDtypeStruct((s, d), F32), jax.ShapeDtypeStruct((1, d), F32),
                   jax.ShapeDtypeStruct((1, 128), F32)],
        compiler_params=_cparams(("arbitrary",)),
    )(x, g, target)


def _ew(body, ins, in_blocks, outs, out_blocks, grid, *, name, ride=None):
    return _call(body, ins, name=name, grid=grid,
                 in_specs=[pl.BlockSpec(bs, im) for bs, im in in_blocks],
                 out_specs=[pl.BlockSpec(bs, im) for bs, im in out_blocks],
                 out_shape=outs, sem=("parallel",) * len(grid), ride=ride)


def _gate_merge_fwd(gates, o_sb, o_ca, *, name, ride=None):
    s, d = o_sb.shape
    tr, tc = _row_tile(s), _pick(d, (1024, 512, 256, 128))
    nc = d // tc

    def body(gs_ref, gc_ref, os_ref, oc_ref, m_ref):
        m = _sigmoid(gs_ref[...]) * os_ref[...] + _sigmoid(gc_ref[...]) * oc_ref[...]
        m_ref[...] = m.astype(m_ref.dtype)

    blk = ((tr, tc), lambda i, j: (i, j))
    return _ew(body, [gates, gates, o_sb, o_ca],
               [blk, ((tr, tc), lambda i, j: (i, j + nc)), blk, blk],
               [jax.ShapeDtypeStruct((s, d), BF16)], [blk], (s // tr, nc), name=name, ride=ride)[0]


def _gate_merge_bwd(dmerged, gates, o_sb, o_ca, *, name):
    s, d = o_sb.shape
    tr, tc = _row_tile(s), _pick(d, (1024, 512, 256, 128))
    nc = d // tc

    def body(dm_ref, gs_ref, gc_ref, os_ref, oc_ref, dgs_ref, dgc_ref, dos_ref, doc_ref):
        dm = dm_ref[...]
        ss = _sigmoid(gs_ref[...])
        sc = _sigmoid(gc_ref[...])
        dgs_ref[...] = (dm * os_ref[...] * ss * (1.0 - ss)).astype(dgs_ref.dtype)
        dgc_ref[...] = (dm * oc_ref[...] * sc * (1.0 - sc)).astype(dgc_ref.dtype)
        dos_ref[...] = (dm * ss).astype(dos_ref.dtype)
        doc_ref[...] = (dm * sc).astype(doc_ref.dtype)

    blk = ((tr, tc), lambda i, j: (i, j))
    sd = jax.ShapeDtypeStruct((s, d), BF16)
    return _ew(body, [dmerged, gates, gates, o_sb, o_ca],
               [blk, blk, ((tr, tc), lambda i, j: (i, j + nc)), blk, blk],
               [sd, sd, sd, sd], [blk, blk, blk, blk], (s // tr, nc), name=name)


def _swiglu_fwd(gu, *, name, ride=None):
    s, f2 = gu.shape
    f = f2 // 2
    tr, tc = 128, _pick(f, (512, 256, 128))

    def body(gu_ref, a_ref):
        for at in range(0, f, tc):
            gv = gu_ref[:, at:at + tc].astype(F32)
            a_ref[:, at:at + tc] = (gv * _sigmoid(gv) * gu_ref[:, f + at:f + at + tc].astype(F32)).astype(a_ref.dtype)

    row = lambda i: (i, 0)
    return _ew(body, [gu], [((tr, f2), row)], [jax.ShapeDtypeStruct((s, f), BF16)], [((tr, f), row)],
               (s // tr,), name=name, ride=ride)[0]


def _swiglu_bwd(dact, gu, *, name):
    s, f2 = gu.shape
    f = f2 // 2
    tr, tc = 128, _pick(f, (512, 256, 128))

    def body(da_ref, gu_ref, o_ref):
        for at in range(0, f, tc):
            da = da_ref[:, at:at + tc]
            gv = gu_ref[:, at:at + tc].astype(F32)
            sg = _sigmoid(gv)
            uv = gu_ref[:, f + at:f + at + tc].astype(F32)
            o_ref[:, at:at + tc] = (da * uv * sg * (1.0 + gv * (1.0 - sg))).astype(o_ref.dtype)
            o_ref[:, f + at:f + at + tc] = (da * gv * sg).astype(o_ref.dtype)

    row = lambda i: (i, 0)
    return _ew(body, [dact, gu], [((tr, f), row), ((tr, f2), row)], [jax.ShapeDtypeStruct((s, f2), BF16)],
               [((tr, f2), row)], (s // tr,), name=name)[0]


def _concat_cols(parts, *, name):
    s = parts[0].shape[0]
    widths = [p.shape[1] for p in parts]
    tr = 256

    def body(*refs):
        o_ref, at = refs[-1], 0
        for p_ref, width in zip(refs, widths):
            o_ref[:, at:at + width] = p_ref[...]
            at += width

    row = lambda i: (i, 0)
    return _ew(body, list(parts), [((tr, width), row) for width in widths],
               [jax.ShapeDtypeStruct((s, sum(widths)), parts[0].dtype)], [((tr, sum(widths)), row)],
               (s // tr,), name=name)[0]


def _ple_fwd(x, t, pe, *, name):
    s, d = x.shape
    tr, tc = _row_tile(s), _pick(d, (1024, 512, 256, 128))

    def body(x_ref, t_ref, p_ref, o_ref):
        o_ref[...] = x_ref[...] + _sigmoid(t_ref[...]) * p_ref[...]

    blk = ((tr, tc), lambda i, j: (i, j))
    return _ew(body, [x, t, pe], [blk, blk, blk],
               [jax.ShapeDtypeStruct((s, d), F32)], [blk], (s // tr, d // tc), name=name)[0]


def _ple_bwd(dx, t, pe, *, name):
    s, d = dx.shape
    tr, tc = _row_tile(s), _pick(d, (1024, 512, 256, 128))

    def body(dx_ref, t_ref, p_ref, dt_ref, dp_ref):
        dxv = dx_ref[...]
        sg = _sigmoid(t_ref[...])
        dt_ref[...] = (dxv * p_ref[...] * sg * (1.0 - sg)).astype(dt_ref.dtype)
        dp_ref[...] = (dxv * sg).astype(dp_ref.dtype)

    blk = ((tr, tc), lambda i, j: (i, j))
    sd = jax.ShapeDtypeStruct((s, d), BF16)
    return _ew(body, [dx, t, pe], [blk, blk, blk], [sd, sd], [blk, blk], (s // tr, d // tc), name=name)


def _sb_tri(later):
    row = lax.broadcasted_iota(jnp.int32, (SB_BLOCK, SB_BLOCK), 0)
    col = lax.broadcasted_iota(jnp.int32, (SB_BLOCK, SB_BLOCK), 1)
    tri = (row > col) if later else (row < col)
    return jnp.concatenate([tri.astype(BF16), jnp.ones((SB_BLOCK, SB_BLOCK), BF16)], axis=1)


def _sb_valid(i, j, own):
    if not own:
        return None
    qi = i * SB_ROWS + lax.broadcasted_iota(jnp.int32, (SB_ROWS, SB_KEYS), 0)
    ki = j * SB_KEYS + lax.broadcasted_iota(jnp.int32, (SB_ROWS, SB_KEYS), 1)
    return ki < qi


def _sb_scan(v, tri, run, later, exact=True):
    hi = v.astype(BF16)
    lo = (v - hi.astype(F32)).astype(BF16) if exact else None
    outs = [None] * SB_GROUPS
    for b in (reversed(range(SB_GROUPS)) if later else range(SB_GROUPS)):
        cols = slice(b * SB_BLOCK, (b + 1) * SB_BLOCK)
        r = _dot_nn(hi[:, cols], tri)
        if exact:
            r = r + _dot_nn(lo[:, cols], tri)
        outs[b] = r[:, :SB_BLOCK] + run
        run = run + r[:, SB_BLOCK:]
    return jnp.concatenate(outs, axis=1), run


def _masked(valid, v):
    return v if valid is None else jnp.where(valid, v, 0.0)


def _sb_scores(q, kj, scale, valid):
    z = _dot_nt(q, kj) * scale
    t = jnp.log(1.0 + jnp.exp(-jnp.abs(z)))
    return jnp.minimum(z, 0.0) - t, _masked(valid, -jnp.maximum(z, 0.0) - t)


def _sb_specs(h_count, s, col0):
    q_spec = pl.BlockSpec((SB_ROWS, HEAD_DIM), lambda h, i: (i, col0 + h))
    k_spec = pl.BlockSpec((s, HEAD_DIM), lambda h, i: (0, col0 + h_count + h))
    v_spec = pl.BlockSpec((s, HEAD_DIM), lambda h, i: (0, col0 + 2 * h_count + h))
    return q_spec, k_spec, v_spec


def _sb_fwd(qkv, n_heads, col0, *, name, ride=None):
    s = qkv.shape[0]
    nq = s // SB_ROWS
    scale = HEAD_DIM ** -0.5

    def body(q_ref, k_ref, v_ref, o_ref):
        i = pl.program_id(1)
        q = q_ref[...]
        tri = _sb_tri(later=True)

        def step(j, carry, own):
            run, acc = carry
            off = pl.multiple_of(j * SB_KEYS, SB_KEYS)
            valid = _sb_valid(i, j, own)
            ls, lk = _sb_scores(q, k_ref[pl.ds(off, SB_KEYS), :], scale, valid)
            between, run = _sb_scan(lk, tri, run, later=True)
            a = _masked(valid, jnp.exp(ls + between))
            return run, acc + _dot_nn(a.astype(BF16), v_ref[pl.ds(off, SB_KEYS), :])

        carry = step(i, (jnp.zeros((SB_ROWS, SB_BLOCK), F32), jnp.zeros((SB_ROWS, HEAD_DIM), F32)), True)
        _, acc = lax.fori_loop(0, i, lambda jj, c: step(i - 1 - jj, c, False), carry)
        o_ref[...] = acc.astype(o_ref.dtype)

    q_spec, k_spec, v_spec = _sb_specs(n_heads, s, col0)
    return _call(
        body, [qkv, qkv, qkv], name=name, grid=(n_heads, nq),
        in_specs=[q_spec, k_spec, v_spec],
        out_specs=[pl.BlockSpec((SB_ROWS, HEAD_DIM), lambda h, i: (i, h))],
        out_shape=[jax.ShapeDtypeStruct((s, n_heads * HEAD_DIM), BF16)],
        sem=("parallel", "arbitrary"), ride=ride)[0]


def _sb_bwd(qkv, dy, n_heads, col0, *, name, ride=None):
    s = qkv.shape[0]
    nq = s // SB_ROWS
    scale = HEAD_DIM ** -0.5

    def body(q_ref, k_ref, v_ref, dy_ref, dq_ref, dk_ref, dv_ref, e_scr, sg_scr, dk_acc, dv_acc):
        i = pl.program_id(1)
        q = q_ref[...]
        dyv = dy_ref[...]

        @pl.when(i == 0)
        def _():
            dk_acc[...] = jnp.zeros_like(dk_acc)
            dv_acc[...] = jnp.zeros_like(dv_acc)

        tri_later = _sb_tri(later=True)

        def pass1(j, run, own):
            off = pl.multiple_of(j * SB_KEYS, SB_KEYS)
            valid = _sb_valid(i, j, own)
            ls, lk = _sb_scores(q, k_ref[pl.ds(off, SB_KEYS), :], scale, valid)
            between, run = _sb_scan(lk, tri_later, run, later=True)
            a = _masked(valid, jnp.exp(ls + between))
            e_scr[j] = a * _dot_nt(dyv, v_ref[pl.ds(off, SB_KEYS), :])
            sg_scr[j] = jnp.exp(ls)
            dv_acc[pl.ds(off, SB_KEYS), :] += _dot_tn(a.astype(BF16), dyv)
            return run

        lax.fori_loop(0, i, lambda jj, run: pass1(i - 1 - jj, run, False),
                      pass1(i, jnp.zeros((SB_ROWS, SB_BLOCK), F32), True))

        tri_earlier = _sb_tri(later=False)

        def pass2(j, carry, own):
            run, dq = carry
            off = pl.multiple_of(j * SB_KEYS, SB_KEYS)
            kj = k_ref[pl.ds(off, SB_KEYS), :]
            sg = sg_scr[j]
            e = e_scr[j]
            before, run = _sb_scan(e, tri_earlier, run, later=False, exact=False)
            dz = _masked(_sb_valid(i, j, own), e * (1.0 - sg) - sg * before) * scale
            dzb = dz.astype(BF16)
            dk_acc[pl.ds(off, SB_KEYS), :] += _dot_tn(dzb, q)
            return run, dq + _dot_nn(dzb, kj)

        init = (jnp.zeros((SB_ROWS, SB_BLOCK), F32), jnp.zeros((SB_ROWS, HEAD_DIM), F32))
        _, dq = pass2(i, lax.fori_loop(0, i, lambda j, c: pass2(j, c, False), init), True)
        dq_ref[...] = dq.astype(dq_ref.dtype)

        @pl.when(i == nq - 1)
        def _():
            dk_ref[...] = dk_acc[...].astype(dk_ref.dtype)
            dv_ref[...] = dv_acc[...].astype(dv_ref.dtype)

    q_spec, k_spec, v_spec = _sb_specs(n_heads, s, col0)
    blk = pl.BlockSpec((SB_ROWS, HEAD_DIM), lambda h, i: (i, h))
    full = pl.BlockSpec((s, HEAD_DIM), lambda h, i: (0, h))
    sd = jax.ShapeDtypeStruct((s, n_heads * HEAD_DIM), BF16)
    return _call(
        body, [qkv, qkv, qkv, dy], name=name, grid=(n_heads, nq),
        in_specs=[q_spec, k_spec, v_spec, blk],
        out_specs=[blk, full, full],
        out_shape=[sd, sd, sd],
        scratch_shapes=[pltpu.VMEM((s // SB_KEYS, SB_ROWS, SB_KEYS), F32), pltpu.VMEM((s // SB_KEYS, SB_ROWS, SB_KEYS), F32),
                        pltpu.VMEM((s, HEAD_DIM), F32), pltpu.VMEM((s, HEAD_DIM), F32)],
        sem=("parallel", "arbitrary"), ride=ride)


def _band_bias(rel_bias):
    h = rel_bias.shape[0]
    width = BAND + CHUNK
    first = width - 1 - N_REL
    line = jnp.concatenate([jnp.broadcast_to(rel_bias[:, :1], (h, first)), rel_bias], axis=1)
    tiled = jnp.broadcast_to(line[:, None, :], (h, CHUNK, width - 1)).reshape(h, CHUNK * (width - 1))
    skew = jnp.pad(tiled, ((0, 0), (0, CHUNK))).reshape(h, CHUNK, width)[:, ::-1, :BAND]
    seen = jnp.arange(BAND) >= CHUNK
    return jnp.where(seen[None, None, :], skew, NEG)


def _band_bias_grad(dbias):
    h = dbias.shape[0]
    width = BAND + CHUNK
    flipped = jnp.pad(dbias[:, ::-1, :], ((0, 0), (0, 0), (0, CHUNK)))
    skew = flipped.reshape(h, CHUNK * width)[:, :CHUNK * (width - 1)].reshape(h, CHUNK, width - 1)
    diag = jnp.sum(skew, axis=1)
    first = width - 1 - N_REL
    clipped = jnp.sum(diag[:, :first + 1], axis=1, keepdims=True)
    return jnp.concatenate([clipped, diag[:, first + 1:]], axis=1)


def _group_bias(band):
    return jnp.concatenate([jnp.pad(band, ((0, 0), (0, 0), ((u + 1) * CHUNK, (CA_PER_STEP - 1 - u) * CHUNK)),
                                    constant_values=NEG) for u in range(CA_PER_STEP)], axis=1)


def _group_bias_grad(dgroup):
    return sum(dgroup[:, u * CHUNK:(u + 1) * CHUNK, (u + 1) * CHUNK:(u + 1) * CHUNK + BAND] for u in range(CA_PER_STEP))


def _ca_load_padded(k_ref, v_ref, kp, vp, s):
    kp[pl.ds(0, CA_PAD), :] = jnp.zeros((CA_PAD, HEAD_DIM), kp.dtype)
    vp[pl.ds(0, CA_PAD), :] = jnp.zeros((CA_PAD, HEAD_DIM), vp.dtype)
    kp[pl.ds(CA_PAD, s), :] = k_ref[...]
    vp[pl.ds(CA_PAD, s), :] = v_ref[...]


def _ca_weights(q, kb, bias, off, scale):
    z = _dot_nt(q, kb) * scale + bias
    pos = off + lax.broadcasted_iota(jnp.int32, (CA_ROWS, CA_BAND), 1)
    z = jnp.where(pos >= CA_PAD, z, NEG)
    p = jnp.exp(z - jnp.max(z, axis=1, keepdims=True))
    return p / jnp.sum(p, axis=1, keepdims=True)


def _ca_specs(h_count, s, col0):
    q_spec = pl.BlockSpec((CA_ROWS, HEAD_DIM), lambda h, c: (c, col0 + h))
    k_spec = pl.BlockSpec((s, HEAD_DIM), lambda h, c: (0, col0 + h_count + h))
    v_spec = pl.BlockSpec((s, HEAD_DIM), lambda h, c: (0, col0 + 2 * h_count + h))
    b_spec = pl.BlockSpec((1, CA_ROWS, CA_BAND), lambda h, c: (h, 0, 0))
    return q_spec, k_spec, v_spec, b_spec


def _ca_fwd(qkv, bias, n_heads, col0, *, name, ride=None):
    s = qkv.shape[0]
    nc = s // CA_ROWS
    scale = HEAD_DIM ** -0.5

    def body(q_ref, k_ref, v_ref, b_ref, o_ref, kp, vp):
        c = pl.program_id(1)

        @pl.when(c == 0)
        def _():
            _ca_load_padded(k_ref, v_ref, kp, vp, s)

        off = pl.multiple_of(c * CA_ROWS, CA_ROWS)
        w = _ca_weights(q_ref[...], kp[pl.ds(off, CA_BAND), :], b_ref[0], off, scale)
        o_ref[...] = _dot_nn(w.astype(BF16), vp[pl.ds(off, CA_BAND), :]).astype(o_ref.dtype)

    q_spec, k_spec, v_spec, b_spec = _ca_specs(n_heads, s, col0)
    return _call(
        body, [qkv, qkv, qkv, bias], name=name, grid=(n_heads, nc),
        in_specs=[q_spec, k_spec, v_spec, b_spec],
        out_specs=[pl.BlockSpec((CA_ROWS, HEAD_DIM), lambda h, c: (c, h))],
        out_shape=[jax.ShapeDtypeStruct((s, n_heads * HEAD_DIM), BF16)],
        scratch_shapes=[pltpu.VMEM((s + CA_PAD, HEAD_DIM), BF16), pltpu.VMEM((s + CA_PAD, HEAD_DIM), BF16)],
        sem=("parallel", "arbitrary"), ride=ride)[0]


def _ca_bwd(qkv, bias, dy, n_heads, col0, *, name, ride=None):
    s = qkv.shape[0]
    nc = s // CA_ROWS
    scale = HEAD_DIM ** -0.5

    def body(q_ref, k_ref, v_ref, b_ref, dy_ref, dq_ref, dk_ref, dv_ref, db_ref, kp, vp, dkp, dvp):
        c = pl.program_id(1)

        @pl.when(c == 0)
        def _():
            _ca_load_padded(k_ref, v_ref, kp, vp, s)
            dkp[...] = jnp.zeros_like(dkp)
            dvp[...] = jnp.zeros_like(dvp)
            db_ref[...] = jnp.zeros_like(db_ref)

        off = pl.multiple_of(c * CA_ROWS, CA_ROWS)
        band = pl.ds(off, CA_BAND)
        q = q_ref[...]
        dyv = dy_ref[...]
        kb = kp[band, :]
        w = _ca_weights(q, kb, b_ref[0], off, scale)
        dw = _dot_nt(dyv, vp[band, :])
        dvp[band, :] += _dot_tn(w.astype(BF16), dyv)
        dz = w * (dw - jnp.sum(w * dw, axis=1, keepdims=True))
        db_ref[0] += dz
        dzs = (dz * scale).astype(BF16)
        dq_ref[...] = _dot_nn(dzs, kb).astype(dq_ref.dtype)
        dkp[band, :] += _dot_tn(dzs, q)

        @pl.when(c == nc - 1)
        def _():
            dk_ref[...] = dkp[pl.ds(CA_PAD, s), :].astype(dk_ref.dtype)
            dv_ref[...] = dvp[pl.ds(CA_PAD, s), :].astype(dv_ref.dtype)

    q_spec, k_spec, v_spec, b_spec = _ca_specs(n_heads, s, col0)
    blk = pl.BlockSpec((CA_ROWS, HEAD_DIM), lambda h, c: (c, h))
    full = pl.BlockSpec((s, HEAD_DIM), lambda h, c: (0, h))
    sd = jax.ShapeDtypeStruct((s, n_heads * HEAD_DIM), BF16)
    return _call(
        body, [qkv, qkv, qkv, bias, dy], name=name, grid=(n_heads, nc),
        in_specs=[q_spec, k_spec, v_spec, b_spec, blk],
        out_specs=[blk, full, full, b_spec],
        out_shape=[sd, sd, sd, jax.ShapeDtypeStruct((n_heads, CA_ROWS, CA_BAND), F32)],
        scratch_shapes=[pltpu.VMEM((s + CA_PAD, HEAD_DIM), BF16), pltpu.VMEM((s + CA_PAD, HEAD_DIM), BF16),
                        pltpu.VMEM((s + CA_PAD, HEAD_DIM), F32), pltpu.VMEM((s + CA_PAD, HEAD_DIM), F32)],
        sem=("parallel", "arbitrary"), ride=ride)


EARLY = ("w_sb_out", "w_ca_out", "w_mix_out")


def _step(x, p, target, small, comm):
    w = comm.w
    d = x.shape[1]
    n_sb = w["w_sb_out"].shape[0] // HEAD_DIM
    n_ca = w["w_ca_out"].shape[0] // HEAD_DIM
    qkv_cols = 3 * HEAD_DIM * (n_sb + n_ca)
    ca_col0 = 3 * n_sb
    both = (F32, BF16)

    h1 = _rms_fwd(x, small["g_mix"], name="rms_mix")
    ffn, ple = ("w_ffn_in",), ("w_ple_gate", "w_ple_in")
    qkv = _mm(h1, w["w_in"], "nn", (BF16,), name="proj_qkv", n=qkv_cols, ride=comm.gather(EARLY, "near"))
    gates = _mm(h1, w["w_in"], "nn", (F32,), name="proj_gates", n=2 * d, b_col_off=qkv_cols,
                ride=comm.gather(ffn, "near", comm.gather(EARLY, "far"), (0, 8)))
    bias = _group_bias(_band_bias(small["rel_bias"]))
    y_sb = _sb_fwd(qkv, n_sb, 0, name="sb_fwd", ride=comm.gather(ffn, "near", comm.gather(EARLY, "pair"), (1, 8, 7)))
    y_ca = _ca_fwd(qkv, bias, n_ca, ca_col0, name="ca_fwd", ride=comm.gather(ffn, "far"))
    out = ("w_ffn_out",)
    o_sb = _mm(y_sb, w["w_sb_out"], "nn", (F32,), name="sb_out", ride=comm.gather(out, "near", part=(0, 4)))
    o_ca = _mm(y_ca, w["w_ca_out"], "nn", (F32,), name="ca_out", ride=comm.gather(out, "near", part=(1, 4)))
    merged = _gate_merge_fwd(gates, o_sb, o_ca, name="gate_merge",
                             ride=comm.gather(out, "near", comm.gather(ffn, "pair"), (2, 4)))
    x1 = _mm(merged, w["w_mix_out"], "nn", (F32,), name="mix_out", resid=x, ride=comm.gather(out, "near", part=(3, 4)))
    h2 = _rms_fwd(x1, small["g_ffn"], name="rms_ffn")
    gu = _mm(h2, w["w_ffn_in"], "nn", (BF16,), name="ffn_in", ride=comm.gather(ple, "near", comm.gather(out, "far")))
    act = _swiglu_fwd(gu, name="swiglu", ride=comm.gather(ple, "far", comm.gather(out, "pair")))
    x2 = _mm(act, w["w_ffn_out"], "nn", (F32,), name="ffn_out", resid=x1, ride=comm.gather(ple, "pair"))
    h3 = _rms_fwd(x2, small["g_ple"], name="rms_ple")
    t = _mm(h3, w["w_ple_gate"], "nn", (F32,), name="ple_gate")
    pe = _mm(p, w["w_ple_in"], "nn", (F32,), name="ple_in")
    x3 = _ple_fwd(x2, t, pe, name="ple_add")

    def halves(n, acts, dout, ride, name):
        if comm.pos is None:
            return comm.grad(n, *_mm(acts, dout, "tn", both, name=name))
        g16 = _mm(acts, dout, "tn", (BF16,), name=name + "_other", m_half=(False, comm.pos), ride=ride)
        comm.grad(n, None, g16, half=True)
        g32 = _mm(acts, dout, "tn", (F32,), name=name + "_own", m_half=(True, comm.pos), ride=comm.pair((n,)))
        comm.grad(n, g32, g16, half=True)

    gs = {}
    dx3, gs["g_final"], loss = _final_loss(x3, small["g_final"], target, name="final_loss")
    dt, dpe = _ple_bwd(dx3, t, pe, name="ple_bwd")
    comm.grad("w_ple_in", *_mm(p, dpe, "tn", both, name="dw_ple_in"))
    comm.grad("w_ple_gate", *_mm(h3, dt, "tn", both, name="dw_ple_gate"))
    ple = ("w_ple_in", "w_ple_gate")
    dh3 = _mm(dt, w["w_ple_gate"], "nt", (F32,), name="dh_ple", ride=comm.pair(ple))
    dx2, dx2_16, gs["g_ple"] = _rms_bwd(x2, small["g_ple"], dh3, dx3, name="rms_ple_bwd")
    comm.add(ple)
    comm.grad("w_ffn_out", *_mm(act, dx2_16, "tn", both, name="dw_ffn_out", ride=comm.chips(ple)))
    dact = _mm(dx2_16, w["w_ffn_out"], "nt", (F32,), name="dact", ride=comm.pair(("w_ffn_out",)))
    dgu = _swiglu_bwd(dact, gu, name="swiglu_bwd")
    comm.sum(ple)
    comm.add(("w_ffn_out",))
    comm.grad("w_ffn_in", *_mm(h2, dgu, "tn", both, name="dw_ffn_in",
                               ride=comm.share(ple, comm.chips(("w_ffn_out",)))))
    dh2 = _mm(dgu, w["w_ffn_in"], "nt", (F32,), name="dh_ffn", ride=comm.pair(("w_ffn_in",)))
    dx1, dx1_16, gs["g_ffn"] = _rms_bwd(x1, small["g_ffn"], dh2, dx2, name="rms_ffn_bwd")
    comm.add(("w_ffn_in",))
    comm.sum(("w_ffn_out",))
    comm.grad("w_mix_out", *_mm(merged, dx1_16, "tn", both, name="dw_mix_out", ride=comm.share(("w_ffn_out",))))
    dmerged = _mm(dx1_16, w["w_mix_out"], "nt", (F32,), name="dmerged", ride=comm.pair(("w_mix_out",)))
    dg_sb, dg_ca, do_sb, do_ca = _gate_merge_bwd(dmerged, gates, o_sb, o_ca, name="gate_merge_bwd")
    comm.add(("w_mix_out",))
    comm.grad("w_sb_out", *_mm(y_sb, do_sb, "tn", both, name="dw_sb_out"))
    comm.grad("w_ca_out", *_mm(y_ca, do_ca, "tn", both, name="dw_ca_out"))
    outs = ("w_sb_out", "w_ca_out")
    dy_sb = _mm(do_sb, w["w_sb_out"], "nt", (BF16,), name="dy_sb", ride=comm.pair(outs))
    dy_ca = _mm(do_ca, w["w_ca_out"], "nt", (BF16,), name="dy_ca")
    comm.add(outs)
    dq_sb, dk_sb, dv_sb = _sb_bwd(qkv, dy_sb, n_sb, 0, name="sb_bwd", ride=comm.chips(("w_ffn_in",)))
    comm.sum(("w_ffn_in",))
    late = ("w_mix_out",) + outs
    dq_ca, dk_ca, dv_ca, dbias = _ca_bwd(qkv, bias, dy_ca, n_ca, ca_col0, name="ca_bwd",
                                         ride=comm.chips(late, comm.share(("w_ffn_in",))))
    comm.sum(late)
    gs["rel_bias"] = _band_bias_grad(_group_bias_grad(dbias))
    dproj = _concat_cols([dq_sb, dk_sb, dv_sb, dq_ca, dk_ca, dv_ca, dg_sb, dg_ca], name="dproj")
    halves("w_in", h1, dproj, comm.share(late), "dw_in")
    comm.add(("w_in",))
    half = x.shape[0] // 2
    dh1 = _mm(dproj, w["w_in"], "nt", (F32,), name="dh_mix_top", rows=(0, half), ride=comm.tail(TAIL_SECOND))
    dh1 = _mm(dproj, w["w_in"], "nt", (F32,), name="dh_mix_bottom", rows=(half, half), onto=(dh1,),
              ride=comm.tail(TAIL_SECOND))
    grad_x, _, gs["g_mix"] = _rms_bwd(x, small["g_mix"], dh1, dx1, name="rms_mix_bwd", ride=comm.tail(TAIL_FIRST))
    return loss, grad_x, gs


def _position():
    x, y, c = lax.axis_index("x"), lax.axis_index("y"), lax.axis_index("c")
    chips = [(1 - x, y), (x, 1 - y), (1 - x, 1 - y)]
    return x, y, c, chips


def _aligned(v, m):
    return v if isinstance(v, int) else pl.multiple_of(v, m)


def _piece_dims(shape, axis):
    k, n = shape
    return (k // 2, n // N_CHIPS) if axis == 1 else (k // N_CHIPS // 2, n)


def _piece(ref, shape, axis, j, h, part=(0, 1)):
    pr, pc = _piece_dims(shape, axis)
    nr = pr // part[1] * (part[2] if len(part) > 2 else 1)
    r0 = part[0] * (pr // part[1])
    if axis == 1:
        return ref.at[pl.ds(_aligned(h * pr + r0, 16), nr), pl.ds(_aligned(j * pc, 128), pc)]
    return ref.at[pl.ds(_aligned((2 * j + h) * pr + r0, 16), nr), :]


def _shard_half(ref, h):
    rows = ref.shape[0] // 2
    return ref.at[pl.ds(_aligned(h * rows, 16), rows), :]


def _remote(src, dst, send_sems, recv_sems, k, to):
    return pltpu.make_async_remote_copy(src_ref=src, dst_ref=dst, send_sem=send_sems.at[k],
                                        recv_sem=recv_sems.at[k], device_id=to, device_id_type=MESH)


def _prefetch_call(body, scalars, ins, in_specs, out_shape, out_specs, grid, *, name, ride=None):
    single = not isinstance(out_shape, (list, tuple))
    outs = _call(body, ins, name=name, grid=grid, in_specs=in_specs,
                 out_specs=[out_specs] if single else out_specs, out_shape=[out_shape] if single else out_shape,
                 sem=("parallel",) * len(grid), ride=ride, scalars=scalars)
    return outs[0] if single else outs


def _slab_tiles(pr, pc):
    tc = pc if pc <= 4096 else _pick(pc, (2048, 1024, 512, 256, 128))
    tr = next(t for t in (1024, 512, 256, 128, 64, 32, 16) if pr % t == 0 and t * tc <= 512 * 1024)
    return tr, tc


def _cast_place(w, axis, pos, *, name, ride=None):
    ks, ns = w.shape
    shape = (ks, ns * N_CHIPS) if axis == 1 else (ks * N_CHIPS, ns)
    tr, tc = _slab_tiles(ks, ns)
    nr, nc = ks // tr, ns // tc

    def body(pos_ref, w_ref, o_ref):
        o_ref[...] = w_ref[...].astype(o_ref.dtype)

    if axis == 1:
        out_map = lambda i, j, pos_ref: (i, pos_ref[0] * nc + j)
    else:
        out_map = lambda i, j, pos_ref: (pos_ref[0] * nr + i, j)
    return _prefetch_call(body, pos, [w], [pl.BlockSpec((tr, tc), lambda i, j, pos_ref: (i, j))],
                          jax.ShapeDtypeStruct(shape, BF16), pl.BlockSpec((tr, tc), out_map), (nr, nc), name=name, ride=ride)


def _run(ride, *, name):
    if ride is None:
        return

    def body(o_ref):
        o_ref[...] = jnp.zeros_like(o_ref)

    _call(body, [], name=name, grid=(1,), in_specs=[], out_specs=[pl.BlockSpec((8, 128), lambda i: (0, 0))],
          out_shape=[jax.ShapeDtypeStruct((8, 128), F32)], ride=ride)


def _ride_gather(ride, w, n, axis, stage, part=(0, 1)):
    shape = w[n].shape
    piece = functools.partial(_piece, shape=shape, axis=axis)
    span = part[2] if len(part) > 2 else 1
    halves = [(2 * part[0] + t * span, 2 * part[1], span) for t in range(2)]

    def copies(ins, outs, send_sems, recv_sems, arriving):
        x, y, c, chips = _position()
        me, (xn, yn, dn) = 2 * x + y, [2 * px + py for px, py in chips]
        if stage == "near":
            plan = [(me, c, part, (1 - x, y, c), xn, c, part), (me, c, part, (x, 1 - y, c), yn, c, part)]
        else:
            plan = []
        if stage in ("far", "far+"):
            plan = [(yn, c, halves[1], (1 - x, y, c), dn, c, halves[1]), (xn, c, halves[0], (x, 1 - y, c), dn, c, halves[0])]
        to_sibling = {"pair": (xn, yn, dn), "far+": (xn, yn), "pair-": (dn,)}.get(stage, ())
        plan += [(j, c, part, (x, y, 1 - c), j, 1 - c, part) for j in to_sibling]
        out = []
        for k, (chip, h, rows, to, from_chip, from_h, from_rows) in enumerate(plan):
            if arriving:
                lands = piece(outs[0], j=from_chip, h=from_h, part=from_rows)
                out.append(_remote(lands, lands, send_sems, recv_sems, k, to))
            else:
                out.append(_remote(piece(ins[0], j=chip, h=h, part=rows), piece(outs[0], j=chip, h=h, part=rows),
                                   send_sems, recv_sems, k, to))
        return out

    def start(*refs):
        for cp in copies(*refs, arriving=False):
            cp.start()

    def finish(*refs):
        for cp in copies(*refs, arriving=True):
            cp.wait_recv()
        for cp in copies(*refs, arriving=False):
            cp.wait_send()

    ride.add([w[n]], [jax.ShapeDtypeStruct(shape, w[n].dtype)], {0: 0}, 4, start, finish,
             lambda outs: w.__setitem__(n, outs[0]))


def _ride_pair(ride, st, axis):
    shape = st["g16"].shape
    pr, pc = (shape[0], shape[1] // N_CHIPS) if st.get("half") else _piece_dims(shape, axis)

    def copies(ins, outs, send_sems, recv_sems):
        x, y, c, _ = _position()
        if st.get("half"):
            pieces = [ins[0].at[:, pl.ds(j * pc, pc)] for j in range(N_CHIPS)]
        else:
            pieces = [_piece(ins[0], shape, axis, j, 1 - c) for j in range(N_CHIPS)]
        return [_remote(pieces[j], outs[0].at[j], send_sems, recv_sems, j, (x, y, 1 - c)) for j in range(N_CHIPS)]

    def start(*refs):
        for cp in copies(*refs):
            cp.start()

    def finish(*refs):
        for cp in copies(*refs):
            cp.wait()

    ride.add([st["g16"]], [jax.ShapeDtypeStruct((N_CHIPS, pr, pc), BF16)], {}, N_CHIPS, start, finish,
             lambda outs: st.__setitem__("sib", outs[0]))


def _ride_chips(ride, st, rows=None):
    _, pr, pc = st["s16"].shape
    r0, nr = (0, pr) if rows is None else rows

    def copies(ins, outs, send_sems, recv_sems):
        x, y, c, chips = _position()
        return [_remote(ins[0].at[2 * px + py, pl.ds(r0, nr), :], outs[0].at[k, pl.ds(r0, nr), :],
                        send_sems, recv_sems, k, (px, py, c)) for k, (px, py) in enumerate(chips)]

    def start(*refs):
        for cp in copies(*refs):
            cp.start()

    def finish(*refs):
        for cp in copies(*refs):
            cp.wait()

    ins, aliases = ([st["s16"], st["recv"]], {1: 0}) if "recv" in st else ([st["s16"]], {})
    ride.add(ins, [jax.ShapeDtypeStruct((3, pr, pc), BF16)], aliases, 3, start, finish,
             lambda outs: st.__setitem__("recv", outs[0]))


def _ride_share(ride, st):
    def sent(ins, outs, send_sems, recv_sems):
        x, y, c, _ = _position()
        return _remote(_shard_half(ins[0], c), _shard_half(outs[0], c), send_sems, recv_sems, 0, (x, y, 1 - c))

    def landed(ins, outs, send_sems, recv_sems):
        x, y, c, _ = _position()
        other = _shard_half(outs[0], 1 - c)
        return _remote(other, other, send_sems, recv_sems, 0, (x, y, 1 - c))

    def start(*refs):
        sent(*refs).start()

    def finish(*refs):
        landed(*refs).wait_recv()
        sent(*refs).wait_send()

    ride.add([st["shard"]], [jax.ShapeDtypeStruct(st["shard"].shape, F32)], {0: 0}, 1, start, finish,
             lambda outs: st.__setitem__("g", outs[0]))


def _piece_block(axis, nr, nc, chip, half=False):
    if half:
        return lambda *a: (a[-3], (a[0] if chip is None else chip(a[-1])) * nc + a[-2])
    if axis == 1:
        return lambda *a: ((a[-1][1] * nr + a[-3]), (a[0] if chip is None else chip(a[-1])) * nc + a[-2])
    return lambda *a: ((2 * (a[0] if chip is None else chip(a[-1])) + a[-1][1]) * nr + a[-3], a[-2])


def _pair_add(g32, sib, axis, pos, *, name, half=False):
    _, pr, pc = sib.shape
    tr, tc = _slab_tiles(pr, pc)
    nr, nc = pr // tr, pc // tc

    def body(pos_ref, g_ref, b_ref, o16_ref):
        o16_ref[0] = (g_ref[...] + b_ref[0].astype(F32)).astype(o16_ref.dtype)

    blk = pl.BlockSpec((1, tr, tc), lambda j, i, k, pos_ref: (j, i, k))
    return _prefetch_call(body, pos, [g32, sib], [pl.BlockSpec((tr, tc), _piece_block(axis, nr, nc, None, half)), blk],
                          jax.ShapeDtypeStruct(sib.shape, BF16), blk, (N_CHIPS, nr, nc), name=name)


def _chip_sum(g32, sib, recv, axis, pos, *, name, half=False):
    _, pr, pc = sib.shape
    tr, tc = _slab_tiles(pr, pc)
    nr, nc = pr // tr, pc // tc

    def body(pos_ref, g_ref, b_ref, r_ref, o_ref):
        pair = g_ref[...] + b_ref[0].astype(F32)
        o_ref[...] = ((pair + r_ref[0].astype(F32)) + r_ref[1].astype(F32)) + r_ref[2].astype(F32)

    return _prefetch_call(
        body, pos, [g32, sib, recv],
        [pl.BlockSpec((tr, tc), _piece_block(axis, nr, nc, lambda pos_ref: pos_ref[0], half)),
         pl.BlockSpec((1, tr, tc), lambda i, k, pos_ref: (pos_ref[0], i, k)),
         pl.BlockSpec((3, tr, tc), lambda i, k, pos_ref: (0, i, k))],
        jax.ShapeDtypeStruct((2 * pr, pc), F32),
        pl.BlockSpec((tr, tc), lambda i, k, pos_ref: (pos_ref[1] * nr + i, k)), (nr, nc), name=name)


class _Comm:
    def __init__(self, pos, w):
        self.pos, self.w, self.st = pos, w, {n: {} for n, _ in BIG}

    def gather(self, names, stage, ride=None, part=(0, 1)):
        ride = _Ride() if ride is None else ride
        for n in names:
            _ride_gather(ride, self.w, n, AXIS[n], stage, part)
        return ride

    def grad(self, n, g32, g16, half=False):
        self.st[n].update(g32=g32, g16=g16, half=half)

    def pair(self, names, ride=None):
        ride = _Ride() if ride is None else ride
        for n in names:
            _ride_pair(ride, self.st[n], AXIS[n])
        return ride

    def add(self, names):
        for n in names:
            st = self.st[n]
            st["s16"] = _pair_add(st["g32"], st["sib"], AXIS[n], self.pos, name="rs_add_" + n, half=st["half"])

    def chips(self, names, ride=None, rows=None):
        ride = _Ride() if ride is None else ride
        for n in names:
            _ride_chips(ride, self.st[n], rows)
        return ride

    def sum(self, names):
        for n in names:
            st = self.st[n]
            st["shard"] = _chip_sum(st["g32"], st["sib"], st["recv"], AXIS[n], self.pos, name="rs_sum_" + n,
                                    half=st["half"])

    def share(self, names, ride=None):
        ride = _Ride() if ride is None else ride
        for n in names:
            _ride_share(ride, self.st[n])
        return ride

    def tail(self, count):
        st = self.st["w_in"]
        rows, at = st["s16"].shape[1], st.get("at", 0)
        st["at"] = at + count
        return self.chips(("w_in",), rows=(at * rows // TAIL_PARTS, count * rows // TAIL_PARTS))

    def tail_rest(self):
        return self.tail(TAIL_PARTS - self.st["w_in"].get("at", 0))

    def result(self, n):
        return self.st[n]["g"]


class _NoComm:
    pos = None

    def __init__(self, w):
        self.w, self.st = w, {}

    def grad(self, n, g32, g16, half=False):
        self.st[n] = (g32, g16)

    def result(self, n):
        return self.st[n]

    def add(self, names):
        pass

    sum = add

    def gather(self, names, *args, **kwargs):
        return None

    pair = chips = share = tail = gather


def _small_all_reduce(vec, *, name):
    r = vec.shape[0]

    def body(vec_ref, out_ref, slots, send_sems, recv_sems):
        x, y, c, _ = _position()
        me = 4 * x + 2 * y + c
        slots[me] = vec_ref[...]
        sends = []
        for k in range(1, 8):
            to = (x ^ (k >> 2), y ^ ((k >> 1) & 1), c ^ (k & 1))
            cp = _remote(slots.at[me], slots.at[me], send_sems, recv_sems, k - 1, to)
            cp.start()
            sends.append(cp)
        for k in range(1, 8):
            frm = 4 * (x ^ (k >> 2)) + 2 * (y ^ ((k >> 1) & 1)) + (c ^ (k & 1))
            _remote(slots.at[frm], slots.at[frm], send_sems, recv_sems, k - 1, (x, y, c)).wait_recv()
        for cp in sends:
            cp.wait_send()
        total = slots[0]
        for d in range(1, 8):
            total = total + slots[d]
        out_ref[...] = total

    return pl.pallas_call(
        body, name=name,
        in_specs=[pl.BlockSpec(memory_space=pltpu.VMEM)], out_specs=pl.BlockSpec(memory_space=pltpu.VMEM),
        out_shape=jax.ShapeDtypeStruct((r, 128), F32),
        scratch_shapes=[pltpu.VMEM((8, r, 128), F32), pltpu.SemaphoreType.DMA((7,)), pltpu.SemaphoreType.DMA((7,))],
    )(vec)


SC_TILES = 32
SC_LANES = 16
SC_TILE_BUDGET = 400 * 1024


def _adamw_update(wv, gv, mv, vv):
    nm = ADAM_B1 * mv + (1.0 - ADAM_B1) * gv
    nv = ADAM_B2 * vv + (1.0 - ADAM_B2) * (gv * gv)
    m_hat = nm / (1.0 - ADAM_B1 ** ADAM_STEP)
    v_hat = nv / (1.0 - ADAM_B2 ** ADAM_STEP)
    return -ADAM_LR * (m_hat / (jnp.sqrt(v_hat) + ADAM_EPS) + ADAM_WD * wv), nm, nv


def _adamw_sc(w, g, m, v, *, name):
    r, c = w.shape
    groups = r // 8
    per_tile = -(-groups // SC_TILES)
    cb = c if 4 * 8 * c * 4 <= SC_TILE_BUDGET else _pick(c, (2048, 1024, 512, 256, 128))

    def body(w_hbm, g_hbm, m_hbm, v_hbm, go_hbm, d_hbm, nm_hbm, nv_hbm, wb, gb, mb, vb):
        tile = lax.axis_index("sc_tile") * 2 + lax.axis_index("sc_core")

        def update(group):
            for c0 in range(0, c, cb):
                at = (pl.ds(group * 8, 8), pl.ds(c0, cb))
                for hbm, buf in ((w_hbm, wb), (g_hbm, gb), (m_hbm, mb), (v_hbm, vb)):
                    pltpu.sync_copy(hbm.at[at], buf)
                pltpu.sync_copy(gb, go_hbm.at[at])

                @pl.loop(0, 8)
                def _(rr):
                    @pl.loop(0, cb, step=SC_LANES)
                    def _(i):
                        lanes = (rr, pl.ds(i, SC_LANES))
                        wb[lanes], mb[lanes], vb[lanes] = _adamw_update(wb[lanes], gb[lanes], mb[lanes], vb[lanes])

                for buf, hbm in ((wb, d_hbm), (mb, nm_hbm), (vb, nv_hbm)):
                    pltpu.sync_copy(buf, hbm.at[at])

        @pl.loop(0, per_tile)
        def _(k):
            group = k * SC_TILES + tile
            if groups % SC_TILES:
                pl.when(group < groups)(lambda: update(group))
            else:
                update(group)

    sd = jax.ShapeDtypeStruct((r, c), F32)
    return pl.kernel(body, name=name, out_type=[sd, sd, sd, sd],
                     mesh=plsc.VectorSubcoreMesh(core_axis_name="sc_core", subcore_axis_name="sc_tile"),
                     scratch_types=[pltpu.VMEM((8, cb), F32)] * 4)(w, g, m, v)


def _adamw(w, g, m, v, *, name, ride=None):
    r, c = w.shape
    tc = c if c <= 4096 else _pick(c, (2048, 1024, 512, 256, 128))
    tr = next(t for t in (512, 256, 128, 64, 32, 16, 8) if r % t == 0 and t * tc <= 256 * 1024)

    def body(w_ref, g_ref, m_ref, v_ref, go_ref, d_ref, nm_ref, nv_ref):
        go_ref[...] = g_ref[...]
        d_ref[...], nm_ref[...], nv_ref[...] = _adamw_update(w_ref[...], g_ref[...], m_ref[...], v_ref[...])

    blk = ((tr, tc), lambda i, j: (i, j))
    sd = jax.ShapeDtypeStruct((r, c), F32)
    return _ew(body, [w, g, m, v], [blk] * 4, [sd] * 4, [blk] * 4, (r // tr, c // tc), name=name, ride=ride)


BIG = (("w_in", 1), ("w_sb_out", 1), ("w_ca_out", 1), ("w_mix_out", 0), ("w_ffn_in", 1), ("w_ffn_out", 0),
       ("w_ple_in", 1), ("w_ple_gate", 0))
AXIS = dict(BIG)
HEAD_PARTS = 8
HEAD_HOSTS = ("w_ffn_in", "w_ffn_out")
TAIL_PARTS = 16
TAIL_SECOND = 5
TAIL_FIRST = 2
ON_SPARSECORE = tuple(n for n, _ in BIG if n != "w_in")
SMALL = ("rel_bias", "g_mix", "g_ffn", "g_ple", "g_final")
ORDER = ("w_in", "w_sb_out", "w_ca_out", "w_mix_out", "rel_bias", "g_mix", "g_ffn", "g_ple", "g_final",
         "w_ffn_in", "w_ffn_out", "w_ple_in", "w_ple_gate")


def _pack(parts):
    flat = jnp.concatenate([a.reshape(-1) for a in parts])
    rows = -(-flat.shape[0] // 1024) * 8
    return jnp.pad(flat, (0, rows * 128 - flat.shape[0])).reshape(rows, 128)


def _unpack(packed, like):
    flat, out, at = packed.reshape(-1), [], 0
    for a in like:
        out.append(flat[at:at + a.size].reshape(a.shape))
        at += a.size
    return out


def kernel(x, p, w_in, w_sb_out, w_ca_out, w_mix_out, rel_bias, g_mix, g_ffn, g_ple, g_final, w_ffn_in, w_ffn_out, w_ple_in, w_ple_gate, loss_target, m_w_in, m_w_sb_out, m_w_ca_out, m_w_mix_out, m_rel_bias, m_g_mix, m_g_ffn, m_g_ple, m_g_final, m_w_ffn_in, m_w_ffn_out, m_w_ple_in, m_w_ple_gate, v_w_in, v_w_sb_out, v_w_ca_out, v_w_mix_out, v_rel_bias, v_g_mix, v_g_ffn, v_g_ple, v_g_final, v_w_ffn_in, v_w_ffn_out, v_w_ple_in, v_w_ple_gate):
    weights = dict(w_in=w_in, w_sb_out=w_sb_out, w_ca_out=w_ca_out, w_mix_out=w_mix_out, rel_bias=rel_bias,
                   g_mix=g_mix, g_ffn=g_ffn, g_ple=g_ple, g_final=g_final, w_ffn_in=w_ffn_in,
                   w_ffn_out=w_ffn_out, w_ple_in=w_ple_in, w_ple_gate=w_ple_gate)
    m_in = dict(w_in=m_w_in, w_sb_out=m_w_sb_out, w_ca_out=m_w_ca_out, w_mix_out=m_w_mix_out, rel_bias=m_rel_bias,
                g_mix=m_g_mix, g_ffn=m_g_ffn, g_ple=m_g_ple, g_final=m_g_final, w_ffn_in=m_w_ffn_in,
                w_ffn_out=m_w_ffn_out, w_ple_in=m_w_ple_in, w_ple_gate=m_w_ple_gate)
    v_in = dict(w_in=v_w_in, w_sb_out=v_w_sb_out, w_ca_out=v_w_ca_out, w_mix_out=v_w_mix_out, rel_bias=v_rel_bias,
                g_mix=v_g_mix, g_ffn=v_g_ffn, g_ple=v_g_ple, g_final=v_g_final, w_ffn_in=v_w_ffn_in,
                w_ffn_out=v_w_ffn_out, w_ple_in=v_w_ple_in, w_ple_gate=v_w_ple_gate)

    pos = jnp.stack([2 * lax.axis_index("x") + lax.axis_index("y"), lax.axis_index("c")]).astype(jnp.int32)
    comm = _Comm(pos, {"w_in": _cast_place(w_in[0], AXIS["w_in"], pos, name="cast_w_in")})
    at = 0
    for n in HEAD_HOSTS:
        ride = comm.gather(("w_in",), "near", part=(at, HEAD_PARTS))
        comm.w[n] = _cast_place(weights[n][0], AXIS[n], pos, name="cast_" + n, ride=ride)
        at += 1
    for n, axis in BIG:
        if n not in comm.w:
            comm.w[n] = _cast_place(weights[n][0], axis, pos, name="cast_" + n)
    _run(comm.gather(("w_in",), "near", part=(at, HEAD_PARTS, HEAD_PARTS - at)), name="gather_w_in_near")
    _run(comm.gather(("w_in",), "far+"), name="gather_w_in_far")
    _run(comm.gather(("w_in",), "pair-"), name="gather_w_in_pair")
    small = dict(rel_bias=rel_bias[0], g_mix=g_mix, g_ffn=g_ffn, g_ple=g_ple, g_final=g_final.reshape(1, -1))
    loss, grad_x, gs = _step(x[0], p[0, 0], loss_target[0], small, comm)

    grads, delta, new_m, new_v = {}, {}, {}, {}
    for n in [n for n, _ in BIG if n != "w_in"] + ["w_in"]:
        if n == "w_in":
            _run(comm.tail_rest(), name="rs_chips_w_in")
            comm.sum(("w_in",))
            _run(comm.share(("w_in",)), name="rs_share_w_in")
        update = _adamw_sc if n in ON_SPARSECORE else _adamw
        g, d, nm, nv = update(weights[n][0], comm.result(n), m_in[n][0], v_in[n][0], name="adamw_" + n)
        grads[n], delta[n], new_m[n], new_v[n] = g[None], d[None], nm[None], nv[None]

    like = [weights[n] for n in SMALL]
    reduced = _small_all_reduce(_pack([gs[n] for n in SMALL] + [loss[:, :1]]), name="small_all_reduce")
    g_small = _unpack(reduced, like + [loss[:, :1]])
    total_loss = g_small[-1].reshape(())
    g_packed = _pack(g_small[:-1])
    _, d_s, m_s, v_s = _adamw(_pack(like), g_packed, _pack([m_in[n] for n in SMALL]), _pack([v_in[n] for n in SMALL]),
                           name="adamw_small")
    for n, g, d, nm, nv in zip(SMALL, g_small[:-1], _unpack(d_s, like), _unpack(m_s, like), _unpack(v_s, like)):
        grads[n], delta[n], new_m[n], new_v[n] = g, d, nm, nv

    return (total_loss, grad_x[None], *[grads[n] for n in ORDER], *[delta[n] for n in ORDER],
            *[new_m[n] for n in ORDER], *[new_v[n] for n in ORDER])
```

```python
import functools
import math

import jax
import jax.numpy as jnp
from jax import lax
from jax.experimental import pallas as pl
from jax.experimental.pallas import tpu as pltpu
from jax.experimental.pallas import tpu_sc as plsc

F32 = jnp.float32
BF16 = jnp.bfloat16

HEAD_DIM = 128
CHUNK = 64
LEFT_CHUNKS = 8
REL_CLIP = 128
N_REL = REL_CLIP + CHUNK
BAND = (LEFT_CHUNKS + 2) * CHUNK
CA_PER_STEP = 4
CA_ROWS = CA_PER_STEP * CHUNK
CA_BAND = BAND + CA_PER_STEP * CHUNK
CA_PAD = BAND
SB_BLOCK = 128
SB_KEYS = 512
SB_GROUPS = SB_KEYS // SB_BLOCK
SB_ROWS = SB_KEYS
EPS = 1e-6
NEG = -1e30

ADAM_LR = 0.001
ADAM_B1 = 0.9
ADAM_B2 = 0.999
ADAM_EPS = 1e-08
ADAM_WD = 0.01
ADAM_STEP = 10

VMEM_LIMIT = 48 * 1024 * 1024
MM_VMEM_BUDGET = 36 * 1024 * 1024
V7X_HBM_BYTES_PER_S = 3.7e12
GRID_STEP_S = 0.35e-6
MESH = pl.DeviceIdType.MESH
N_CHIPS = 4


def _pick(dim, prefs):
    for t in prefs:
        if dim % t == 0:
            return t
    raise ValueError(f"no tile for {dim}")


def _cparams(sem=None):
    return pltpu.CompilerParams(dimension_semantics=sem, vmem_limit_bytes=VMEM_LIMIT)


def _sigmoid(v):
    return 1.0 / (1.0 + jnp.exp(-v))


def _dot(a, b, dims):
    return lax.dot_general(a, b, (dims, ((), ())), preferred_element_type=F32)


def _dot_nn(a, b):
    return _dot(a, b, ((1,), (0,)))


def _dot_nt(a, b):
    return _dot(a, b, ((1,), (1,)))


def _dot_tn(a, b):
    return _dot(a, b, ((0,), (0,)))


HBM = pl.BlockSpec(memory_space=pltpu.HBM)


class _Ride:
    def __init__(self):
        self.items = []

    def add(self, ins, outs, aliases, n_sems, start, finish, sink):
        self.items.append((ins, outs, aliases, n_sems, start, finish, sink))


def _call(body, args, *, name, grid, in_specs, out_specs, out_shape, scratch_shapes=(), sem=None, ride=None,
          scalars=None, onto=()):
    items = ride.items if ride is not None else []
    if onto:
        args, in_specs = list(args) + list(onto), list(in_specs) + [HBM] * len(onto)
        inner, body = body, lambda *refs: inner(*refs[:len(args) - len(onto)], *refs[len(args):])
    n_in, n_out, n_scr = len(args), len(out_shape), len(scratch_shapes)
    r_ins = [a for it in items for a in it[0]]
    r_outs = [o for it in items for o in it[1]]
    updated = [id(it[0][i]) for it in items for i in it[2]]
    assert len(set(updated)) == len(updated), "one call may update a buffer in place only once"
    aliases, a, b = {n_in - len(onto) + t: t for t in range(len(onto))}, n_in, n_out
    for it in items:
        aliases.update({a + i: b + o for i, o in it[2].items()})
        a, b = a + len(it[0]), b + len(it[1])
    sems = [pltpu.SemaphoreType.DMA((it[3],)) for it in items for _ in range(2)]

    def wrapped(*refs):
        head, refs = (refs[:1], refs[1:]) if scalars is not None else ((), refs)
        ins, rin = refs[:n_in], refs[n_in:n_in + len(r_ins)]
        at = n_in + len(r_ins)
        outs, rout = refs[at:at + n_out], refs[at + n_out:at + n_out + len(r_outs)]
        at += n_out + len(r_outs)
        scr, rsem = refs[at:at + n_scr], refs[at + n_scr:]

        def each(which):
            a = b = 0
            for q, it in enumerate(items):
                it[which](rin[a:a + len(it[0])], rout[b:b + len(it[1])], rsem[2 * q], rsem[2 * q + 1])
                a, b = a + len(it[0]), b + len(it[1])

        if items:
            ids = [pl.program_id(d) for d in range(len(grid))]
            first = functools.reduce(jnp.logical_and, [i == 0 for i in ids])
            last = functools.reduce(jnp.logical_and, [i == g - 1 for i, g in zip(ids, grid)])
            pl.when(first)(lambda: each(4))
        body(*head, *ins, *outs, *scr)
        if items:
            pl.when(last)(lambda: each(5))

    specs = dict(grid=grid, in_specs=list(in_specs) + [HBM] * len(r_ins),
                 out_specs=list(out_specs) + [HBM] * len(r_outs), scratch_shapes=list(scratch_shapes) + sems)
    if scalars is not None:
        specs = dict(grid_spec=pltpu.PrefetchScalarGridSpec(num_scalar_prefetch=1, **specs))
        aliases = {i + 1: o for i, o in aliases.items()}
    res = pl.pallas_call(
        wrapped, name=name, **specs,
        out_shape=list(out_shape) + r_outs,
        input_output_aliases=aliases,
        compiler_params=_cparams(("arbitrary",) * len(grid) if items else sem),
    )(*(() if scalars is None else (scalars,)), *args, *r_ins)
    b = n_out
    for it in items:
        it[6](res[b:b + len(it[1])])
        b += len(it[1])
    return list(res[:n_out])


def _mm_tiles(m, n_align, n, k, a_bytes, b_bytes, out_bytes):
    best = None
    tks = sorted({t for t in (k, k // 2, k // 4, 2048, 1024, 512, 256, 128) if t <= k and k % t == 0 and t % 128 == 0})
    for tm in (t for t in (2048, 1024, 512, 256, 128) if m % t == 0):
        for tn in (t for t in (2048, 1024, 512, 256, 128) if n_align % t == 0):
            for tk in tks:
                nk = k // tk
                vmem = 2 * (tm * tk * a_bytes + tk * tn * b_bytes + tm * tn * out_bytes) + tm * tn * 4
                if vmem > MM_VMEM_BUDGET:
                    continue
                traffic = m * k * a_bytes * (n // tn if nk > 1 else 1) + k * n * b_bytes * (m // tm)
                traffic += tm * tk * a_bytes + tk * tn * b_bytes + tm * tn * out_bytes
                traffic += m * n * 4 * nk if nk > 1 else 0
                cost = traffic / V7X_HBM_BYTES_PER_S + (m // tm) * (n // tn) * nk * GRID_STEP_S
                if best is None or cost < best[0]:
                    best = (cost, tm, tn, tk)
    return best[1:]


def _mm(a, b, mode, out_dtypes, *, name, n=None, b_col_off=0, resid=None, ride=None, rows=None, onto=(), m_half=None):
    if mode == "nn":
        m, k = a.shape
        n = b.shape[1] if n is None else n
    elif mode == "nt":
        m, k = a.shape
        n = b.shape[0]
    else:
        k, m = a.shape
        n = b.shape[1]
    if m_half is not None:
        m //= 2
    m_all, (row0, m) = m, (0, m) if rows is None else rows
    n_out = len(out_dtypes)
    has_resid = resid is not None
    out_bytes = sum(jnp.dtype(dt).itemsize for dt in out_dtypes) + (4 if has_resid else 0)
    tm, tn, tk = _mm_tiles(math.gcd(m, row0) if row0 else m, math.gcd(n, b_col_off) if b_col_off else n, n, k,
                           a.dtype.itemsize, b.dtype.itemsize, out_bytes)
    nk = k // tk
    boff, roff = b_col_off // tn, row0 // tm
    dot = {"nn": _dot_nn, "nt": _dot_nt, "tn": _dot_tn}[mode]
    if m_half is None:
        half = lambda: 0
    else:
        half = lambda pos_ref: (pos_ref[1] if m_half[0] else 1 - pos_ref[1]) * (m // tm)

    def body(*refs):
        refs = refs[m_half is not None:]
        a_ref, b_ref = refs[0], refs[1]
        r_ref = refs[2] if has_resid else None
        o_refs = refs[2 + has_resid: 2 + has_resid + n_out]

        def finish(r):
            if has_resid:
                r = r + r_ref[...]
            for o_ref in o_refs:
                o_ref[...] = r.astype(o_ref.dtype)

        part = dot(a_ref[...].astype(BF16), b_ref[...].astype(BF16))
        if nk == 1:
            finish(part)
            return
        acc_ref = refs[-1]
        kk = pl.program_id(2)

        @pl.when(kk == 0)
        def _():
            acc_ref[...] = part

        @pl.when(kk > 0)
        def _():
            acc_ref[...] += part

        @pl.when(kk == nk - 1)
        def _():
            finish(acc_ref[...])

    if mode == "nn":
        a_spec = pl.BlockSpec((tm, tk), lambda i, j, kk, *_: (i + roff, kk))
        b_spec = pl.BlockSpec((tk, tn), lambda i, j, kk, *_: (kk, j + boff))
    elif mode == "nt":
        a_spec = pl.BlockSpec((tm, tk), lambda i, j, kk, *_: (i + roff, kk))
        b_spec = pl.BlockSpec((tn, tk), lambda i, j, kk, *_: (j, kk))
    else:
        a_spec = pl.BlockSpec((tk, tm), lambda i, j, kk, *pos: (kk, i + half(*pos)))
        b_spec = pl.BlockSpec((tk, tn), lambda i, j, kk, *_: (kk, j))
    o_spec = pl.BlockSpec((tm, tn), lambda i, j, kk, *_: (i + roff, j))
    in_specs = [a_spec, b_spec] + ([o_spec] if has_resid else [])
    args = [a, b] + ([resid] if has_resid else [])
    outs = _call(
        body, args, name=name,
        grid=(m // tm, n // tn, nk),
        in_specs=in_specs,
        out_specs=[o_spec] * n_out,
        out_shape=[jax.ShapeDtypeStruct((m_all, n), dt) for dt in out_dtypes],
        scratch_shapes=[pltpu.VMEM((tm, tn), F32)] if nk > 1 else [],
        sem=("parallel", "parallel", "arbitrary"), ride=ride, onto=onto,
        scalars=None if m_half is None else m_half[1])
    return outs[0] if n_out == 1 else tuple(outs)


def _row_tile(s):
    return _pick(s, (256, 128))


def _rms_fwd(x, g, *, name, ride=None):
    s, d = x.shape
    tr = _row_tile(s)

    def body(x_ref, g_ref, o_ref):
        xv = x_ref[...]
        r = lax.rsqrt(jnp.mean(xv * xv, axis=1, keepdims=True) + EPS)
        o_ref[...] = (xv * r * g_ref[...]).astype(o_ref.dtype)

    return _call(
        body, [x, g], name=name, grid=(s // tr,),
        in_specs=[pl.BlockSpec((tr, d), lambda i: (i, 0)), pl.BlockSpec((1, d), lambda i: (0, 0))],
        out_specs=[pl.BlockSpec((tr, d), lambda i: (i, 0))],
        out_shape=[jax.ShapeDtypeStruct((s, d), BF16)], sem=("parallel",), ride=ride)[0]


def _rms_bwd(x, g, dh, dres, *, name, ride=None):
    s, d = x.shape
    tr = _row_tile(s)

    def body(x_ref, g_ref, dh_ref, dres_ref, dx_ref, dx16_ref, dg_ref):
        i = pl.program_id(0)
        xv = x_ref[...]
        r = lax.rsqrt(jnp.mean(xv * xv, axis=1, keepdims=True) + EPS)
        xhat = xv * r
        dhv = dh_ref[...]
        dxhat = dhv * g_ref[...]
        proj = jnp.mean(dxhat * xhat, axis=1, keepdims=True)
        dx = dres_ref[...] + r * (dxhat - xhat * proj)
        dx_ref[...] = dx
        dx16_ref[...] = dx.astype(dx16_ref.dtype)

        @pl.when(i == 0)
        def _():
            dg_ref[...] = jnp.zeros_like(dg_ref)

        dg_ref[...] += jnp.sum(dhv * xhat, axis=0, keepdims=True)

    row = pl.BlockSpec((tr, d), lambda i: (i, 0))
    vec = pl.BlockSpec((1, d), lambda i: (0, 0))
    return _call(
        body, [x, g, dh, dres], name=name, grid=(s // tr,),
        in_specs=[row, vec, row, row],
        out_specs=[row, row, vec],
        out_shape=[jax.ShapeDtypeStruct((s, d), F32), jax.ShapeDtypeStruct((s, d), BF16),
                   jax.ShapeDtypeStruct((1, d), F32)],
        sem=("arbitrary",), ride=ride)


def _final_loss(x, g, target, *, name):
    s, d = x.shape
    tr = _row_tile(s)

    def body(x_ref, g_ref, t_ref, dx_ref, dg_ref, loss_ref):
        i = pl.program_id(0)
        xv = x_ref[...]
        gv = g_ref[...]
        r = lax.rsqrt(jnp.mean(xv * xv, axis=1, keepdims=True) + EPS)
        xhat = xv * r
        err = xhat * gv - t_ref[...]
        dy = err * (1.0 / d)
        dxhat = dy * gv
        proj = jnp.mean(dxhat * xhat, axis=1, keepdims=True)
        dx_ref[...] = r * (dxhat - xhat * proj)

        @pl.when(i == 0)
        def _():
            dg_ref[...] = jnp.zeros_like(dg_ref)
            loss_ref[...] = jnp.zeros_like(loss_ref)

        dg_ref[...] += jnp.sum(dy * xhat, axis=0, keepdims=True)
        part = 0.5 * jnp.sum(jnp.mean(err * err, axis=1, keepdims=True), axis=0, keepdims=True)
        loss_ref[...] += jnp.broadcast_to(part, loss_ref.shape)

    row = pl.BlockSpec((tr, d), lambda i: (i, 0))
    vec = pl.BlockSpec((1, d), lambda i: (0, 0))
    return pl.pallas_call(
        body, name=name, grid=(s // tr,),
        in_specs=[row, vec, row],
        out_specs=[row, vec, pl.BlockSpec((1, 128), lambda i: (0, 0))],
        out_shape=[jax.ShapeDtypeStruct((s, d), F32), jax.ShapeDtypeStruct((1, d), F32),
                   jax.ShapeDtypeStruct((1, 128), F32)],
        compiler_params=_cparams(("arbitrary",)),
    )(x, g, target)


def _ew(body, ins, in_blocks, outs, out_blocks, grid, *, name, ride=None):
    return _call(body, ins, name=name, grid=grid,
                 in_specs=[pl.BlockSpec(bs, im) for bs, im in in_blocks],
                 out_specs=[pl.BlockSpec(bs, im) for bs, im in out_blocks],
                 out_shape=outs, sem=("parallel",) * len(grid), ride=ride)


def _gate_merge_fwd(gates, o_sb, o_ca, *, name, ride=None):
    s, d = o_sb.shape
    tr, tc = _row_tile(s), _pick(d, (1024, 512, 256, 128))
    nc = d // tc

    def body(gs_ref, gc_ref, os_ref, oc_ref, m_ref):
        f32 = lambda ref: ref[...].astype(F32)
        m = _sigmoid(f32(gs_ref)) * f32(os_ref) + _sigmoid(f32(gc_ref)) * f32(oc_ref)
        m_ref[...] = m.astype(m_ref.dtype)

    blk = ((tr, tc), lambda i, j: (i, j))
    return _ew(body, [gates, gates, o_sb, o_ca],
               [blk, ((tr, tc), lambda i, j: (i, j + nc)), blk, blk],
               [jax.ShapeDtypeStruct((s, d), BF16)], [blk], (s // tr, nc), name=name, ride=ride)[0]


def _gate_merge_bwd(dmerged, gates, o_sb, o_ca, *, name):
    s, d = o_sb.shape
    tr, tc = _row_tile(s), _pick(d, (1024, 512, 256, 128))
    nc = d // tc

    def body(dm_ref, gs_ref, gc_ref, os_ref, oc_ref, dgs_ref, dgc_ref, dos_ref, doc_ref):
        f32 = lambda ref: ref[...].astype(F32)
        dm = dm_ref[...]
        ss = _sigmoid(f32(gs_ref))
        sc = _sigmoid(f32(gc_ref))
        dgs_ref[...] = (dm * f32(os_ref) * ss * (1.0 - ss)).astype(dgs_ref.dtype)
        dgc_ref[...] = (dm * f32(oc_ref) * sc * (1.0 - sc)).astype(dgc_ref.dtype)
        dos_ref[...] = (dm * ss).astype(dos_ref.dtype)
        doc_ref[...] = (dm * sc).astype(doc_ref.dtype)

    blk = ((tr, tc), lambda i, j: (i, j))
    sd = jax.ShapeDtypeStruct((s, d), BF16)
    return _ew(body, [dmerged, gates, gates, o_sb, o_ca],
               [blk, blk, ((tr, tc), lambda i, j: (i, j + nc)), blk, blk],
               [sd, sd, sd, sd], [blk, blk, blk, blk], (s // tr, nc), name=name)


def _swiglu_fwd(gu, *, name, ride=None):
    s, f2 = gu.shape
    f = f2 // 2
    tr, tc = 128, _pick(f, (512, 256, 128))

    def body(gu_ref, a_ref):
        for at in range(0, f, tc):
            gv = gu_ref[:, at:at + tc].astype(F32)
            a_ref[:, at:at + tc] = (gv * _sigmoid(gv) * gu_ref[:, f + at:f + at + tc].astype(F32)).astype(a_ref.dtype)

    row = lambda i: (i, 0)
    return _ew(body, [gu], [((tr, f2), row)], [jax.ShapeDtypeStruct((s, f), BF16)], [((tr, f), row)],
               (s // tr,), name=name, ride=ride)[0]


def _swiglu_bwd(dact, gu, *, name):
    s, f2 = gu.shape
    f = f2 // 2
    tr, tc = 128, _pick(f, (512, 256, 128))

    def body(da_ref, gu_ref, o_ref):
        for at in range(0, f, tc):
            da = da_ref[:, at:at + tc]
            gv = gu_ref[:, at:at + tc].astype(F32)
            sg = _sigmoid(gv)
            uv = gu_ref[:, f + at:f + at + tc].astype(F32)
            o_ref[:, at:at + tc] = (da * uv * sg * (1.0 + gv * (1.0 - sg))).astype(o_ref.dtype)
            o_ref[:, f + at:f + at + tc] = (da * gv * sg).astype(o_ref.dtype)

    row = lambda i: (i, 0)
    return _ew(body, [dact, gu], [((tr, f), row), ((tr, f2), row)], [jax.ShapeDtypeStruct((s, f2), BF16)],
               [((tr, f2), row)], (s // tr,), name=name)[0]


def _concat_cols(parts, *, name):
    s = parts[0].shape[0]
    widths = [p.shape[1] for p in parts]
    tr = 256

    def body(*refs):
        o_ref, at = refs[-1], 0
        for p_ref, width in zip(refs, widths):
            o_ref[:, at:at + width] = p_ref[...]
            at += width

    row = lambda i: (i, 0)
    return _ew(body, list(parts), [((tr, width), row) for width in widths],
               [jax.ShapeDtypeStruct((s, sum(widths)), parts[0].dtype)], [((tr, sum(widths)), row)],
               (s // tr,), name=name)[0]


def _ple_fwd(x, t, pe, *, name):
    s, d = x.shape
    tr, tc = _row_tile(s), _pick(d, (1024, 512, 256, 128))

    def body(x_ref, t_ref, p_ref, o_ref):
        o_ref[...] = x_ref[...] + _sigmoid(t_ref[...]) * p_ref[...]

    blk = ((tr, tc), lambda i, j: (i, j))
    return _ew(body, [x, t, pe], [blk, blk, blk],
               [jax.ShapeDtypeStruct((s, d), F32)], [blk], (s // tr, d // tc), name=name)[0]


def _ple_bwd(dx, t, pe, *, name):
    s, d = dx.shape
    tr, tc = _row_tile(s), _pick(d, (1024, 512, 256, 128))

    def body(dx_ref, t_ref, p_ref, dt_ref, dp_ref):
        dxv = dx_ref[...]
        sg = _sigmoid(t_ref[...])
        dt_ref[...] = (dxv * p_ref[...] * sg * (1.0 - sg)).astype(dt_ref.dtype)
        dp_ref[...] = (dxv * sg).astype(dp_ref.dtype)

    blk = ((tr, tc), lambda i, j: (i, j))
    sd = jax.ShapeDtypeStruct((s, d), BF16)
    return _ew(body, [dx, t, pe], [blk, blk, blk], [sd, sd], [blk, blk], (s // tr, d // tc), name=name)


def _sb_tri(later):
    row = lax.broadcasted_iota(jnp.int32, (SB_BLOCK, SB_BLOCK), 0)
    col = lax.broadcasted_iota(jnp.int32, (SB_BLOCK, SB_BLOCK), 1)
    tri = (row > col) if later else (row < col)
    return jnp.concatenate([tri.astype(BF16), jnp.ones((SB_BLOCK, SB_BLOCK), BF16)], axis=1)


def _sb_valid(i, j, own):
    if not own:
        return None
    qi = i * SB_ROWS + lax.broadcasted_iota(jnp.int32, (SB_ROWS, SB_KEYS), 0)
    ki = j * SB_KEYS + lax.broadcasted_iota(jnp.int32, (SB_ROWS, SB_KEYS), 1)
    return ki < qi


def _sb_scan(v, tri, run, later):
    hi = v.astype(BF16)
    lo = (v - hi.astype(F32)).astype(BF16)
    outs = [None] * SB_GROUPS
    for b in (reversed(range(SB_GROUPS)) if later else range(SB_GROUPS)):
        cols = slice(b * SB_BLOCK, (b + 1) * SB_BLOCK)
        r = _dot_nn(hi[:, cols], tri) + _dot_nn(lo[:, cols], tri)
        outs[b] = r[:, :SB_BLOCK] + run
        run = run + r[:, SB_BLOCK:]
    return jnp.concatenate(outs, axis=1), run


def _masked(valid, v):
    return v if valid is None else jnp.where(valid, v, 0.0)


def _sb_scores(q, kj, scale, valid):
    z = _dot_nt(q, kj) * scale
    t = jnp.log(1.0 + jnp.exp(-jnp.abs(z)))
    return jnp.minimum(z, 0.0) - t, _masked(valid, -jnp.maximum(z, 0.0) - t)


def _sb_specs(h_count, s, col0):
    q_spec = pl.BlockSpec((SB_ROWS, HEAD_DIM), lambda h, i: (i, col0 + h))
    k_spec = pl.BlockSpec((s, HEAD_DIM), lambda h, i: (0, col0 + h_count + h))
    v_spec = pl.BlockSpec((s, HEAD_DIM), lambda h, i: (0, col0 + 2 * h_count + h))
    return q_spec, k_spec, v_spec


def _sb_fwd(qkv, n_heads, col0, *, name, ride=None):
    s = qkv.shape[0]
    nq = s // SB_ROWS
    scale = HEAD_DIM ** -0.5

    def body(q_ref, k_ref, v_ref, o_ref):
        i = pl.program_id(1)
        q = q_ref[...]
        tri = _sb_tri(later=True)

        def step(j, carry, own):
            run, acc = carry
            off = pl.multiple_of(j * SB_KEYS, SB_KEYS)
            valid = _sb_valid(i, j, own)
            ls, lk = _sb_scores(q, k_ref[pl.ds(off, SB_KEYS), :], scale, valid)
            between, run = _sb_scan(lk, tri, run, later=True)
            a = _masked(valid, jnp.exp(ls + between))
            return run, acc + _dot_nn(a.astype(BF16), v_ref[pl.ds(off, SB_KEYS), :])

        carry = step(i, (jnp.zeros((SB_ROWS, SB_BLOCK), F32), jnp.zeros((SB_ROWS, HEAD_DIM), F32)), True)
        _, acc = lax.fori_loop(0, i, lambda jj, c: step(i - 1 - jj, c, False), carry)
        o_ref[...] = acc.astype(o_ref.dtype)

    q_spec, k_spec, v_spec = _sb_specs(n_heads, s, col0)
    return _call(
        body, [qkv, qkv, qkv], name=name, grid=(n_heads, nq),
        in_specs=[q_spec, k_spec, v_spec],
        out_specs=[pl.BlockSpec((SB_ROWS, HEAD_DIM), lambda h, i: (i, h))],
        out_shape=[jax.ShapeDtypeStruct((s, n_heads * HEAD_DIM), BF16)],
        sem=("parallel", "arbitrary"), ride=ride)[0]


def _sb_bwd(qkv, dy, n_heads, col0, *, name, ride=None):
    s = qkv.shape[0]
    nq = s // SB_ROWS
    scale = HEAD_DIM ** -0.5

    def body(q_ref, k_ref, v_ref, dy_ref, dq_ref, dk_ref, dv_ref, e_scr, sg_scr, dk_acc, dv_acc):
        i = pl.program_id(1)
        q = q_ref[...]
        dyv = dy_ref[...]

        @pl.when(i == 0)
        def _():
            dk_acc[...] = jnp.zeros_like(dk_acc)
            dv_acc[...] = jnp.zeros_like(dv_acc)

        tri_later = _sb_tri(later=True)

        def pass1(j, run, own):
            off = pl.multiple_of(j * SB_KEYS, SB_KEYS)
            valid = _sb_valid(i, j, own)
            ls, lk = _sb_scores(q, k_ref[pl.ds(off, SB_KEYS), :], scale, valid)
            between, run = _sb_scan(lk, tri_later, run, later=True)
            a = _masked(valid, jnp.exp(ls + between))
            e_scr[j] = a * _dot_nt(dyv, v_ref[pl.ds(off, SB_KEYS), :])
            sg_scr[j] = jnp.exp(ls)
            dv_acc[pl.ds(off, SB_KEYS), :] += _dot_tn(a.astype(BF16), dyv)
            return run

        lax.fori_loop(0, i, lambda jj, run: pass1(i - 1 - jj, run, False),
                      pass1(i, jnp.zeros((SB_ROWS, SB_BLOCK), F32), True))

        tri_earlier = _sb_tri(later=False)

        def pass2(j, carry, own):
            run, dq = carry
            off = pl.multiple_of(j * SB_KEYS, SB_KEYS)
            kj = k_ref[pl.ds(off, SB_KEYS), :]
            sg = sg_scr[j]
            e = e_scr[j]
            before, run = _sb_scan(e, tri_earlier, run, later=False)
            dz = _masked(_sb_valid(i, j, own), e * (1.0 - sg) - sg * before) * scale
            dzb = dz.astype(BF16)
            dk_acc[pl.ds(off, SB_KEYS), :] += _dot_tn(dzb, q)
            return run, dq + _dot_nn(dzb, kj)

        init = (jnp.zeros((SB_ROWS, SB_BLOCK), F32), jnp.zeros((SB_ROWS, HEAD_DIM), F32))
        _, dq = pass2(i, lax.fori_loop(0, i, lambda j, c: pass2(j, c, False), init), True)
        dq_ref[...] = dq.astype(dq_ref.dtype)

        @pl.when(i == nq - 1)
        def _():
            dk_ref[...] = dk_acc[...].astype(dk_ref.dtype)
            dv_ref[...] = dv_acc[...].astype(dv_ref.dtype)

    q_spec, k_spec, v_spec = _sb_specs(n_heads, s, col0)
    blk = pl.BlockSpec((SB_ROWS, HEAD_DIM), lambda h, i: (i, h))
    full = pl.BlockSpec((s, HEAD_DIM), lambda h, i: (0, h))
    sd = jax.ShapeDtypeStruct((s, n_heads * HEAD_DIM), BF16)
    return _call(
        body, [qkv, qkv, qkv, dy], name=name, grid=(n_heads, nq),
        in_specs=[q_spec, k_spec, v_spec, blk],
        out_specs=[blk, full, full],
        out_shape=[sd, sd, sd],
        scratch_shapes=[pltpu.VMEM((s // SB_KEYS, SB_ROWS, SB_KEYS), F32), pltpu.VMEM((s // SB_KEYS, SB_ROWS, SB_KEYS), F32),
                        pltpu.VMEM((s, HEAD_DIM), F32), pltpu.VMEM((s, HEAD_DIM), F32)],
        sem=("parallel", "arbitrary"), ride=ride)


def _band_bias(rel_bias):
    h = rel_bias.shape[0]
    width = BAND + CHUNK
    first = width - 1 - N_REL
    line = jnp.concatenate([jnp.broadcast_to(rel_bias[:, :1], (h, first)), rel_bias], axis=1)
    tiled = jnp.broadcast_to(line[:, None, :], (h, CHUNK, width - 1)).reshape(h, CHUNK * (width - 1))
    skew = jnp.pad(tiled, ((0, 0), (0, CHUNK))).reshape(h, CHUNK, width)[:, ::-1, :BAND]
    seen = jnp.arange(BAND) >= CHUNK
    return jnp.where(seen[None, None, :], skew, NEG)


def _band_bias_grad(dbias):
    h = dbias.shape[0]
    width = BAND + CHUNK
    flipped = jnp.pad(dbias[:, ::-1, :], ((0, 0), (0, 0), (0, CHUNK)))
    skew = flipped.reshape(h, CHUNK * width)[:, :CHUNK * (width - 1)].reshape(h, CHUNK, width - 1)
    diag = jnp.sum(skew, axis=1)
    first = width - 1 - N_REL
    clipped = jnp.sum(diag[:, :first + 1], axis=1, keepdims=True)
    return jnp.concatenate([clipped, diag[:, first + 1:]], axis=1)


def _group_bias(band):
    return jnp.concatenate([jnp.pad(band, ((0, 0), (0, 0), ((u + 1) * CHUNK, (CA_PER_STEP - 1 - u) * CHUNK)),
                                    constant_values=NEG) for u in range(CA_PER_STEP)], axis=1)


def _group_bias_grad(dgroup):
    return sum(dgroup[:, u * CHUNK:(u + 1) * CHUNK, (u + 1) * CHUNK:(u + 1) * CHUNK + BAND] for u in range(CA_PER_STEP))


def _ca_load_padded(k_ref, v_ref, kp, vp, s):
    kp[pl.ds(0, CA_PAD), :] = jnp.zeros((CA_PAD, HEAD_DIM), kp.dtype)
    vp[pl.ds(0, CA_PAD), :] = jnp.zeros((CA_PAD, HEAD_DIM), vp.dtype)
    kp[pl.ds(CA_PAD, s), :] = k_ref[...]
    vp[pl.ds(CA_PAD, s), :] = v_ref[...]


def _ca_weights(q, kb, bias, off, scale):
    z = _dot_nt(q, kb) * scale + bias
    pos = off + lax.broadcasted_iota(jnp.int32, (CA_ROWS, CA_BAND), 1)
    z = jnp.where(pos >= CA_PAD, z, NEG)
    p = jnp.exp(z - jnp.max(z, axis=1, keepdims=True))
    return p / jnp.sum(p, axis=1, keepdims=True)


def _ca_specs(h_count, s, col0):
    q_spec = pl.BlockSpec((CA_ROWS, HEAD_DIM), lambda h, c: (c, col0 + h))
    k_spec = pl.BlockSpec((s, HEAD_DIM), lambda h, c: (0, col0 + h_count + h))
    v_spec = pl.BlockSpec((s, HEAD_DIM), lambda h, c: (0, col0 + 2 * h_count + h))
    b_spec = pl.BlockSpec((1, CA_ROWS, CA_BAND), lambda h, c: (h, 0, 0))
    return q_spec, k_spec, v_spec, b_spec


def _ca_fwd(qkv, bias, n_heads, col0, *, name, ride=None):
    s = qkv.shape[0]
    nc = s // CA_ROWS
    scale = HEAD_DIM ** -0.5

    def body(q_ref, k_ref, v_ref, b_ref, o_ref, kp, vp):
        c = pl.program_id(1)

        @pl.when(c == 0)
        def _():
            _ca_load_padded(k_ref, v_ref, kp, vp, s)

        off = pl.multiple_of(c * CA_ROWS, CA_ROWS)
        w = _ca_weights(q_ref[...], kp[pl.ds(off, CA_BAND), :], b_ref[0], off, scale)
        o_ref[...] = _dot_nn(w.astype(BF16), vp[pl.ds(off, CA_BAND), :]).astype(o_ref.dtype)

    q_spec, k_spec, v_spec, b_spec = _ca_specs(n_heads, s, col0)
    return _call(
        body, [qkv, qkv, qkv, bias], name=name, grid=(n_heads, nc),
        in_specs=[q_spec, k_spec, v_spec, b_spec],
        out_specs=[pl.BlockSpec((CA_ROWS, HEAD_DIM), lambda h, c: (c, h))],
        out_shape=[jax.ShapeDtypeStruct((s, n_heads * HEAD_DIM), BF16)],
        scratch_shapes=[pltpu.VMEM((s + CA_PAD, HEAD_DIM), BF16), pltpu.VMEM((s + CA_PAD, HEAD_DIM), BF16)],
        sem=("parallel", "arbitrary"), ride=ride)[0]


def _ca_bwd(qkv, bias, dy, n_heads, col0, *, name, ride=None):
    s = qkv.shape[0]
    nc = s // CA_ROWS
    scale = HEAD_DIM ** -0.5

    def body(q_ref, k_ref, v_ref, b_ref, dy_ref, dq_ref, dk_ref, dv_ref, db_ref, kp, vp, dkp, dvp):
        c = pl.program_id(1)

        @pl.when(c == 0)
        def _():
            _ca_load_padded(k_ref, v_ref, kp, vp, s)
            dkp[...] = jnp.zeros_like(dkp)
            dvp[...] = jnp.zeros_like(dvp)
            db_ref[...] = jnp.zeros_like(db_ref)

        off = pl.multiple_of(c * CA_ROWS, CA_ROWS)
        band = pl.ds(off, CA_BAND)
        q = q_ref[...]
        dyv = dy_ref[...]
        kb = kp[band, :]
        w = _ca_weights(q, kb, b_ref[0], off, scale)
        dw = _dot_nt(dyv, vp[band, :])
        dvp[band, :] += _dot_tn(w.astype(BF16), dyv)
        dz = w * (dw - jnp.sum(w * dw, axis=1, keepdims=True))
        db_ref[0] += dz
        dzs = (dz * scale).astype(BF16)
        dq_ref[...] = _dot_nn(dzs, kb).astype(dq_ref.dtype)
        dkp[band, :] += _dot_tn(dzs, q)

        @pl.when(c == nc - 1)
        def _():
            dk_ref[...] = dkp[pl.ds(CA_PAD, s), :].astype(dk_ref.dtype)
            dv_ref[...] = dvp[pl.ds(CA_PAD, s), :].astype(dv_ref.dtype)

    q_spec, k_spec, v_spec, b_spec = _ca_specs(n_heads, s, col0)
    blk = pl.BlockSpec((CA_ROWS, HEAD_DIM), lambda h, c: (c, h))
    full = pl.BlockSpec((s, HEAD_DIM), lambda h, c: (0, h))
    sd = jax.ShapeDtypeStruct((s, n_heads * HEAD_DIM), BF16)
    return _call(
        body, [qkv, qkv, qkv, bias, dy], name=name, grid=(n_heads, nc),
        in_specs=[q_spec, k_spec, v_spec, b_spec, blk],
        out_specs=[blk, full, full, b_spec],
        out_shape=[sd, sd, sd, jax.ShapeDtypeStruct((n_heads, CA_ROWS, CA_BAND), F32)],
        scratch_shapes=[pltpu.VMEM((s + CA_PAD, HEAD_DIM), BF16), pltpu.VMEM((s + CA_PAD, HEAD_DIM), BF16),
                        pltpu.VMEM((s + CA_PAD, HEAD_DIM), F32), pltpu.VMEM((s + CA_PAD, HEAD_DIM), F32)],
        sem=("parallel", "arbitrary"), ride=ride)


EARLY = ("w_sb_out", "w_ca_out", "w_mix_out")


def _step(x, p, target, small, comm):
    w = comm.w
    d = x.shape[1]
    n_sb = w["w_sb_out"].shape[0] // HEAD_DIM
    n_ca = w["w_ca_out"].shape[0] // HEAD_DIM
    qkv_cols = 3 * HEAD_DIM * (n_sb + n_ca)
    ca_col0 = 3 * n_sb
    both = (F32, BF16)

    h1 = _rms_fwd(x, small["g_mix"], name="rms_mix")
    ffn, ple = ("w_ffn_in",), ("w_ple_gate", "w_ple_in")
    qkv = _mm(h1, w["w_in"], "nn", (BF16,), name="proj_qkv", n=qkv_cols, ride=comm.gather(EARLY, "near"))
    gates = _mm(h1, w["w_in"], "nn", (BF16,), name="proj_gates", n=2 * d, b_col_off=qkv_cols,
                ride=comm.gather(ffn, "near", comm.gather(EARLY, "far"), (0, 8)))
    bias = _group_bias(_band_bias(small["rel_bias"]))
    y_sb = _sb_fwd(qkv, n_sb, 0, name="sb_fwd", ride=comm.gather(ffn, "near", comm.gather(EARLY, "pair"), (1, 8, 7)))
    y_ca = _ca_fwd(qkv, bias, n_ca, ca_col0, name="ca_fwd", ride=comm.gather(ffn, "far"))
    out = ("w_ffn_out",)
    o_sb = _mm(y_sb, w["w_sb_out"], "nn", (BF16,), name="sb_out", ride=comm.gather(out, "near", part=(0, 4)))
    o_ca = _mm(y_ca, w["w_ca_out"], "nn", (BF16,), name="ca_out", ride=comm.gather(out, "near", part=(1, 4)))
    merged = _gate_merge_fwd(gates, o_sb, o_ca, name="gate_merge",
                             ride=comm.gather(out, "near", comm.gather(ffn, "pair"), (2, 4)))
    x1 = _mm(merged, w["w_mix_out"], "nn", (F32,), name="mix_out", resid=x, ride=comm.gather(out, "near", part=(3, 4)))
    h2 = _rms_fwd(x1, small["g_ffn"], name="rms_ffn")
    gu = _mm(h2, w["w_ffn_in"], "nn", (BF16,), name="ffn_in", ride=comm.gather(ple, "near", comm.gather(out, "far")))
    act = _swiglu_fwd(gu, name="swiglu", ride=comm.gather(ple, "far", comm.gather(out, "pair")))
    x2 = _mm(act, w["w_ffn_out"], "nn", (F32,), name="ffn_out", resid=x1, ride=comm.gather(ple, "pair"))
    h3 = _rms_fwd(x2, small["g_ple"], name="rms_ple")
    t = _mm(h3, w["w_ple_gate"], "nn", (F32,), name="ple_gate")
    pe = _mm(p, w["w_ple_in"], "nn", (F32,), name="ple_in")
    x3 = _ple_fwd(x2, t, pe, name="ple_add")

    def halves(n, acts, dout, ride, name):
        if comm.pos is None:
            return comm.grad(n, *_mm(acts, dout, "tn", both, name=name))
        g16 = _mm(acts, dout, "tn", (BF16,), name=name + "_other", m_half=(False, comm.pos), ride=ride)
        comm.grad(n, None, g16, half=True)
        g32 = _mm(acts, dout, "tn", (F32,), name=name + "_own", m_half=(True, comm.pos), ride=comm.pair((n,)))
        comm.grad(n, g32, g16, half=True)

    gs = {}
    dx3, gs["g_final"], loss = _final_loss(x3, small["g_final"], target, name="final_loss")
    dt, dpe = _ple_bwd(dx3, t, pe, name="ple_bwd")
    comm.grad("w_ple_in", *_mm(p, dpe, "tn", both, name="dw_ple_in"))
    comm.grad("w_ple_gate", *_mm(h3, dt, "tn", both, name="dw_ple_gate"))
    ple = ("w_ple_in", "w_ple_gate")
    dh3 = _mm(dt, w["w_ple_gate"], "nt", (F32,), name="dh_ple", ride=comm.pair(ple))
    dx2, dx2_16, gs["g_ple"] = _rms_bwd(x2, small["g_ple"], dh3, dx3, name="rms_ple_bwd")
    comm.add(ple)
    comm.grad("w_ffn_out", *_mm(act, dx2_16, "tn", both, name="dw_ffn_out", ride=comm.chips(ple)))
    dact = _mm(dx2_16, w["w_ffn_out"], "nt", (F32,), name="dact", ride=comm.pair(("w_ffn_out",)))
    dgu = _swiglu_bwd(dact, gu, name="swiglu_bwd")
    comm.sum(ple)
    comm.add(("w_ffn_out",))
    comm.grad("w_ffn_in", *_mm(h2, dgu, "tn", both, name="dw_ffn_in",
                               ride=comm.share(ple, comm.chips(("w_ffn_out",)))))
    dh2 = _mm(dgu, w["w_ffn_in"], "nt", (F32,), name="dh_ffn", ride=comm.pair(("w_ffn_in",)))
    dx1, dx1_16, gs["g_ffn"] = _rms_bwd(x1, small["g_ffn"], dh2, dx2, name="rms_ffn_bwd")
    comm.add(("w_ffn_in",))
    comm.sum(("w_ffn_out",))
    comm.grad("w_mix_out", *_mm(merged, dx1_16, "tn", both, name="dw_mix_out", ride=comm.share(("w_ffn_out",))))
    dmerged = _mm(dx1_16, w["w_mix_out"], "nt", (F32,), name="dmerged", ride=comm.pair(("w_mix_out",)))
    dg_sb, dg_ca, do_sb, do_ca = _gate_merge_bwd(dmerged, gates, o_sb, o_ca, name="gate_merge_bwd")
    comm.add(("w_mix_out",))
    comm.grad("w_sb_out", *_mm(y_sb, do_sb, "tn", both, name="dw_sb_out"))
    comm.grad("w_ca_out", *_mm(y_ca, do_ca, "tn", both, name="dw_ca_out"))
    outs = ("w_sb_out", "w_ca_out")
    dy_sb = _mm(do_sb, w["w_sb_out"], "nt", (BF16,), name="dy_sb", ride=comm.pair(outs))
    dy_ca = _mm(do_ca, w["w_ca_out"], "nt", (BF16,), name="dy_ca")
    comm.add(outs)
    dq_sb, dk_sb, dv_sb = _sb_bwd(qkv, dy_sb, n_sb, 0, name="sb_bwd", ride=comm.chips(("w_ffn_in",)))
    comm.sum(("w_ffn_in",))
    late = ("w_mix_out",) + outs
    dq_ca, dk_ca, dv_ca, dbias = _ca_bwd(qkv, bias, dy_ca, n_ca, ca_col0, name="ca_bwd",
                                         ride=comm.chips(late, comm.share(("w_ffn_in",))))
    comm.sum(late)
    gs["rel_bias"] = _band_bias_grad(_group_bias_grad(dbias))
    dproj = _concat_cols([dq_sb, dk_sb, dv_sb, dq_ca, dk_ca, dv_ca, dg_sb, dg_ca], name="dproj")
    halves("w_in", h1, dproj, comm.share(late), "dw_in")
    comm.add(("w_in",))
    half = x.shape[0] // 2
    dh1 = _mm(dproj, w["w_in"], "nt", (F32,), name="dh_mix_top", rows=(0, half), ride=comm.tail(TAIL_SECOND))
    dh1 = _mm(dproj, w["w_in"], "nt", (F32,), name="dh_mix_bottom", rows=(half, half), onto=(dh1,),
              ride=comm.tail(TAIL_SECOND))
    grad_x, _, gs["g_mix"] = _rms_bwd(x, small["g_mix"], dh1, dx1, name="rms_mix_bwd", ride=comm.tail(TAIL_FIRST))
    return loss, grad_x, gs


def _position():
    x, y, c = lax.axis_index("x"), lax.axis_index("y"), lax.axis_index("c")
    chips = [(1 - x, y), (x, 1 - y), (1 - x, 1 - y)]
    return x, y, c, chips


def _aligned(v, m):
    return v if isinstance(v, int) else pl.multiple_of(v, m)


def _piece_dims(shape, axis):
    k, n = shape
    return (k // 2, n // N_CHIPS) if axis == 1 else (k // N_CHIPS // 2, n)


def _piece(ref, shape, axis, j, h, part=(0, 1)):
    pr, pc = _piece_dims(shape, axis)
    nr = pr // part[1] * (part[2] if len(part) > 2 else 1)
    r0 = part[0] * (pr // part[1])
    if axis == 1:
        return ref.at[pl.ds(_aligned(h * pr + r0, 16), nr), pl.ds(_aligned(j * pc, 128), pc)]
    return ref.at[pl.ds(_aligned((2 * j + h) * pr + r0, 16), nr), :]


def _shard_half(ref, h):
    rows = ref.shape[0] // 2
    return ref.at[pl.ds(_aligned(h * rows, 16), rows), :]


def _remote(src, dst, send_sems, recv_sems, k, to):
    return pltpu.make_async_remote_copy(src_ref=src, dst_ref=dst, send_sem=send_sems.at[k],
                                        recv_sem=recv_sems.at[k], device_id=to, device_id_type=MESH)


def _prefetch_call(body, scalars, ins, in_specs, out_shape, out_specs, grid, *, name, ride=None):
    single = not isinstance(out_shape, (list, tuple))
    outs = _call(body, ins, name=name, grid=grid, in_specs=in_specs,
                 out_specs=[out_specs] if single else out_specs, out_shape=[out_shape] if single else out_shape,
                 sem=("parallel",) * len(grid), ride=ride, scalars=scalars)
    return outs[0] if single else outs


def _slab_tiles(pr, pc):
    tc = pc if pc <= 4096 else _pick(pc, (2048, 1024, 512, 256, 128))
    tr = next(t for t in (1024, 512, 256, 128, 64, 32, 16) if pr % t == 0 and t * tc <= 512 * 1024)
    return tr, tc


def _cast_place(w, axis, pos, *, name, ride=None):
    ks, ns = w.shape
    shape = (ks, ns * N_CHIPS) if axis == 1 else (ks * N_CHIPS, ns)
    tr, tc = _slab_tiles(ks, ns)
    nr, nc = ks // tr, ns // tc

    def body(pos_ref, w_ref, o_ref):
        o_ref[...] = w_ref[...].astype(o_ref.dtype)

    if axis == 1:
        out_map = lambda i, j, pos_ref: (i, pos_ref[0] * nc + j)
    else:
        out_map = lambda i, j, pos_ref: (pos_ref[0] * nr + i, j)
    return _prefetch_call(body, pos, [w], [pl.BlockSpec((tr, tc), lambda i, j, pos_ref: (i, j))],
                          jax.ShapeDtypeStruct(shape, BF16), pl.BlockSpec((tr, tc), out_map), (nr, nc), name=name, ride=ride)


def _run(ride, *, name):
    if ride is None:
        return

    def body(o_ref):
        o_ref[...] = jnp.zeros_like(o_ref)

    _call(body, [], name=name, grid=(1,), in_specs=[], out_specs=[pl.BlockSpec((8, 128), lambda i: (0, 0))],
          out_shape=[jax.ShapeDtypeStruct((8, 128), F32)], ride=ride)


def _ride_gather(ride, w, n, axis, stage, part=(0, 1)):
    shape = w[n].shape
    piece = functools.partial(_piece, shape=shape, axis=axis)
    span = part[2] if len(part) > 2 else 1
    halves = [(2 * part[0] + t * span, 2 * part[1], span) for t in range(2)]

    def copies(ins, outs, send_sems, recv_sems, arriving):
        x, y, c, chips = _position()
        me, (xn, yn, dn) = 2 * x + y, [2 * px + py for px, py in chips]
        if stage == "near":
            plan = [(me, c, part, (1 - x, y, c), xn, c, part), (me, c, part, (x, 1 - y, c), yn, c, part)]
        else:
            plan = []
        if stage in ("far", "far+"):
            plan = [(yn, c, halves[1], (1 - x, y, c), dn, c, halves[1]), (xn, c, halves[0], (x, 1 - y, c), dn, c, halves[0])]
        to_sibling = {"pair": (xn, yn, dn), "far+": (xn, yn), "pair-": (dn,)}.get(stage, ())
        plan += [(j, c, part, (x, y, 1 - c), j, 1 - c, part) for j in to_sibling]
        out = []
        for k, (chip, h, rows, to, from_chip, from_h, from_rows) in enumerate(plan):
            if arriving:
                lands = piece(outs[0], j=from_chip, h=from_h, part=from_rows)
                out.append(_remote(lands, lands, send_sems, recv_sems, k, to))
            else:
                out.append(_remote(piece(ins[0], j=chip, h=h, part=rows), piece(outs[0], j=chip, h=h, part=rows),
                                   send_sems, recv_sems, k, to))
        return out

    def start(*refs):
        for cp in copies(*refs, arriving=False):
            cp.start()

    def finish(*refs):
        for cp in copies(*refs, arriving=True):
            cp.wait_recv()
        for cp in copies(*refs, arriving=False):
            cp.wait_send()

    ride.add([w[n]], [jax.ShapeDtypeStruct(shape, w[n].dtype)], {0: 0}, 4, start, finish,
             lambda outs: w.__setitem__(n, outs[0]))


def _ride_pair(ride, st, axis):
    shape = st["g16"].shape
    pr, pc = (shape[0], shape[1] // N_CHIPS) if st.get("half") else _piece_dims(shape, axis)

    def copies(ins, outs, send_sems, recv_sems):
        x, y, c, _ = _position()
        if st.get("half"):
            pieces = [ins[0].at[:, pl.ds(j * pc, pc)] for j in range(N_CHIPS)]
        else:
            pieces = [_piece(ins[0], shape, axis, j, 1 - c) for j in range(N_CHIPS)]
        return [_remote(pieces[j], outs[0].at[j], send_sems, recv_sems, j, (x, y, 1 - c)) for j in range(N_CHIPS)]

    def start(*refs):
        for cp in copies(*refs):
            cp.start()

    def finish(*refs):
        for cp in copies(*refs):
            cp.wait()

    ride.add([st["g16"]], [jax.ShapeDtypeStruct((N_CHIPS, pr, pc), BF16)], {}, N_CHIPS, start, finish,
             lambda outs: st.__setitem__("sib", outs[0]))


def _ride_chips(ride, st, rows=None):
    _, pr, pc = st["s16"].shape
    r0, nr = (0, pr) if rows is None else rows

    def copies(ins, outs, send_sems, recv_sems):
        x, y, c, chips = _position()
        return [_remote(ins[0].at[2 * px + py, pl.ds(r0, nr), :], outs[0].at[k, pl.ds(r0, nr), :],
                        send_sems, recv_sems, k, (px, py, c)) for k, (px, py) in enumerate(chips)]

    def start(*refs):
        for cp in copies(*refs):
            cp.start()

    def finish(*refs):
        for cp in copies(*refs):
            cp.wait()

    ins, aliases = ([st["s16"], st["recv"]], {1: 0}) if "recv" in st else ([st["s16"]], {})
    ride.add(ins, [jax.ShapeDtypeStruct((3, pr, pc), BF16)], aliases, 3, start, finish,
             lambda outs: st.__setitem__("recv", outs[0]))


def _ride_share(ride, st):
    def sent(ins, outs, send_sems, recv_sems):
        x, y, c, _ = _position()
        return _remote(_shard_half(ins[0], c), _shard_half(outs[0], c), send_sems, recv_sems, 0, (x, y, 1 - c))

    def landed(ins, outs, send_sems, recv_sems):
        x, y, c, _ = _position()
        other = _shard_half(outs[0], 1 - c)
        return _remote(other, other, send_sems, recv_sems, 0, (x, y, 1 - c))

    def start(*refs):
        sent(*refs).start()

    def finish(*refs):
        landed(*refs).wait_recv()
        sent(*refs).wait_send()

    ride.add([st["shard"]], [jax.ShapeDtypeStruct(st["shard"].shape, F32)], {0: 0}, 1, start, finish,
             lambda outs: st.__setitem__("g", outs[0]))


def _piece_block(axis, nr, nc, chip, half=False):
    if half:
        return lambda *a: (a[-3], (a[0] if chip is None else chip(a[-1])) * nc + a[-2])
    if axis == 1:
        return lambda *a: ((a[-1][1] * nr + a[-3]), (a[0] if chip is None else chip(a[-1])) * nc + a[-2])
    return lambda *a: ((2 * (a[0] if chip is None else chip(a[-1])) + a[-1][1]) * nr + a[-3], a[-2])


def _pair_add(g32, sib, axis, pos, *, name, half=False):
    _, pr, pc = sib.shape
    tr, tc = _slab_tiles(pr, pc)
    nr, nc = pr // tr, pc // tc

    def body(pos_ref, g_ref, b_ref, o16_ref):
        o16_ref[0] = (g_ref[...] + b_ref[0].astype(F32)).astype(o16_ref.dtype)

    blk = pl.BlockSpec((1, tr, tc), lambda j, i, k, pos_ref: (j, i, k))
    return _prefetch_call(body, pos, [g32, sib], [pl.BlockSpec((tr, tc), _piece_block(axis, nr, nc, None, half)), blk],
                          jax.ShapeDtypeStruct(sib.shape, BF16), blk, (N_CHIPS, nr, nc), name=name)


def _chip_sum(g32, sib, recv, axis, pos, *, name, half=False):
    _, pr, pc = sib.shape
    tr, tc = _slab_tiles(pr, pc)
    nr, nc = pr // tr, pc // tc

    def body(pos_ref, g_ref, b_ref, r_ref, o_ref):
        pair = g_ref[...] + b_ref[0].astype(F32)
        o_ref[...] = ((pair + r_ref[0].astype(F32)) + r_ref[1].astype(F32)) + r_ref[2].astype(F32)

    return _prefetch_call(
        body, pos, [g32, sib, recv],
        [pl.BlockSpec((tr, tc), _piece_block(axis, nr, nc, lambda pos_ref: pos_ref[0], half)),
         pl.BlockSpec((1, tr, tc), lambda i, k, pos_ref: (pos_ref[0], i, k)),
         pl.BlockSpec((3, tr, tc), lambda i, k, pos_ref: (0, i, k))],
        jax.ShapeDtypeStruct((2 * pr, pc), F32),
        pl.BlockSpec((tr, tc), lambda i, k, pos_ref: (pos_ref[1] * nr + i, k)), (nr, nc), name=name)


class _Comm:
    def __init__(self, pos, w):
        self.pos, self.w, self.st = pos, w, {n: {} for n, _ in BIG}

    def gather(self, names, stage, ride=None, part=(0, 1)):
        ride = _Ride() if ride is None else ride
        for n in names:
            _ride_gather(ride, self.w, n, AXIS[n], stage, part)
        return ride

    def grad(self, n, g32, g16, half=False):
        self.st[n].update(g32=g32, g16=g16, half=half)

    def pair(self, names, ride=None):
        ride = _Ride() if ride is None else ride
        for n in names:
            _ride_pair(ride, self.st[n], AXIS[n])
        return ride

    def add(self, names):
        for n in names:
            st = self.st[n]
            st["s16"] = _pair_add(st["g32"], st["sib"], AXIS[n], self.pos, name="rs_add_" + n, half=st["half"])

    def chips(self, names, ride=None, rows=None):
        ride = _Ride() if ride is None else ride
        for n in names:
            _ride_chips(ride, self.st[n], rows)
        return ride

    def sum(self, names):
        for n in names:
            st = self.st[n]
            st["shard"] = _chip_sum(st["g32"], st["sib"], st["recv"], AXIS[n], self.pos, name="rs_sum_" + n,
                                    half=st["half"])

    def share(self, names, ride=None):
        ride = _Ride() if ride is None else ride
        for n in names:
            _ride_share(ride, self.st[n])
        return ride

    def tail(self, count):
        st = self.st["w_in"]
        rows, at = st["s16"].shape[1], st.get("at", 0)
        st["at"] = at + count
        return self.chips(("w_in",), rows=(at * rows // TAIL_PARTS, count * rows // TAIL_PARTS))

    def tail_rest(self):
        return self.tail(TAIL_PARTS - self.st["w_in"].get("at", 0))

    def result(self, n):
        return self.st[n]["g"]


class _NoComm:
    pos = None

    def __init__(self, w):
        self.w, self.st = w, {}

    def grad(self, n, g32, g16, half=False):
        self.st[n] = (g32, g16)

    def result(self, n):
        return self.st[n]

    def add(self, names):
        pass

    sum = add

    def gather(self, names, *args, **kwargs):
        return None

    pair = chips = share = tail = gather


def _small_all_reduce(vec, *, name):
    r = vec.shape[0]

    def body(vec_ref, out_ref, slots, send_sems, recv_sems):
        x, y, c, _ = _position()
        me = 4 * x + 2 * y + c
        slots[me] = vec_ref[...]
        sends = []
        for k in range(1, 8):
            to = (x ^ (k >> 2), y ^ ((k >> 1) & 1), c ^ (k & 1))
            cp = _remote(slots.at[me], slots.at[me], send_sems, recv_sems, k - 1, to)
            cp.start()
            sends.append(cp)
        for k in range(1, 8):
            frm = 4 * (x ^ (k >> 2)) + 2 * (y ^ ((k >> 1) & 1)) + (c ^ (k & 1))
            _remote(slots.at[frm], slots.at[frm], send_sems, recv_sems, k - 1, (x, y, c)).wait_recv()
        for cp in sends:
            cp.wait_send()
        total = slots[0]
        for d in range(1, 8):
            total = total + slots[d]
        out_ref[...] = total

    return pl.pallas_call(
        body, name=name,
        in_specs=[pl.BlockSpec(memory_space=pltpu.VMEM)], out_specs=pl.BlockSpec(memory_space=pltpu.VMEM),
        out_shape=jax.ShapeDtypeStruct((r, 128), F32),
        scratch_shapes=[pltpu.VMEM((8, r, 128), F32), pltpu.SemaphoreType.DMA((7,)), pltpu.SemaphoreType.DMA((7,))],
    )(vec)


SC_TILES = 32
SC_LANES = 16
SC_TILE_BUDGET = 400 * 1024


def _adamw_update(wv, gv, mv, vv):
    nm = ADAM_B1 * mv + (1.0 - ADAM_B1) * gv
    nv = ADAM_B2 * vv + (1.0 - ADAM_B2) * (gv * gv)
    m_hat = nm / (1.0 - ADAM_B1 ** ADAM_STEP)
    v_hat = nv / (1.0 - ADAM_B2 ** ADAM_STEP)
    return -ADAM_LR * (m_hat / (jnp.sqrt(v_hat) + ADAM_EPS) + ADAM_WD * wv), nm, nv


def _adamw_sc(w, g, m, v, *, name):
    r, c = w.shape
    groups = r // 8
    per_tile = -(-groups // SC_TILES)
    cb = c if 4 * 8 * c * 4 <= SC_TILE_BUDGET else _pick(c, (2048, 1024, 512, 256, 128))

    def body(w_hbm, g_hbm, m_hbm, v_hbm, go_hbm, d_hbm, nm_hbm, nv_hbm, wb, gb, mb, vb):
        tile = lax.axis_index("sc_tile") * 2 + lax.axis_index("sc_core")

        def update(group):
            for c0 in range(0, c, cb):
                at = (pl.ds(group * 8, 8), pl.ds(c0, cb))
                for hbm, buf in ((w_hbm, wb), (g_hbm, gb), (m_hbm, mb), (v_hbm, vb)):
                    pltpu.sync_copy(hbm.at[at], buf)
                pltpu.sync_copy(gb, go_hbm.at[at])

                @pl.loop(0, 8)
                def _(rr):
                    @pl.loop(0, cb, step=SC_LANES)
                    def _(i):
                        lanes = (rr, pl.ds(i, SC_LANES))
                        wb[lanes], mb[lanes], vb[lanes] = _adamw_update(wb[lanes], gb[lanes], mb[lanes], vb[lanes])

                for buf, hbm in ((wb, d_hbm), (mb, nm_hbm), (vb, nv_hbm)):
                    pltpu.sync_copy(buf, hbm.at[at])

        @pl.loop(0, per_tile)
        def _(k):
            group = k * SC_TILES + tile
            if groups % SC_TILES:
                pl.when(group < groups)(lambda: update(group))
            else:
                update(group)

    sd = jax.ShapeDtypeStruct((r, c), F32)
    return pl.kernel(body, name=name, out_type=[sd, sd, sd, sd],
                     mesh=plsc.VectorSubcoreMesh(core_axis_name="sc_core", subcore_axis_name="sc_tile"),
                     scratch_types=[pltpu.VMEM((8, cb), F32)] * 4)(w, g, m, v)


def _adamw(w, g, m, v, *, name, ride=None):
    r, c = w.shape
    tc = c if c <= 4096 else _pick(c, (2048, 1024, 512, 256, 128))
    tr = next(t for t in (512, 256, 128, 64, 32, 16, 8) if r % t == 0 and t * tc <= 256 * 1024)

    def body(w_ref, g_ref, m_ref, v_ref, go_ref, d_ref, nm_ref, nv_ref):
        go_ref[...] = g_ref[...]
        d_ref[...], nm_ref[...], nv_ref[...] = _adamw_update(w_ref[...], g_ref[...], m_ref[...], v_ref[...])

    blk = ((tr, tc), lambda i, j: (i, j))
    sd = jax.ShapeDtypeStruct((r, c), F32)
    return _ew(body, [w, g, m, v], [blk] * 4, [sd] * 4, [blk] * 4, (r // tr, c // tc), name=name, ride=ride)


BIG = (("w_in", 1), ("w_sb_out", 1), ("w_ca_out", 1), ("w_mix_out", 0), ("w_ffn_in", 1), ("w_ffn_out", 0),
       ("w_ple_in", 1), ("w_ple_gate", 0))
AXIS = dict(BIG)
HEAD_PARTS = 8
HEAD_HOSTS = ("w_ffn_in", "w_ffn_out")
TAIL_PARTS = 16
TAIL_SECOND = 5
TAIL_FIRST = 2
ON_SPARSECORE = tuple(n for n, _ in BIG if n != "w_in")
SMALL = ("rel_bias", "g_mix", "g_ffn", "g_ple", "g_final")
ORDER = ("w_in", "w_sb_out", "w_ca_out", "w_mix_out", "rel_bias", "g_mix", "g_ffn", "g_ple", "g_final",
         "w_ffn_in", "w_ffn_out", "w_ple_in", "w_ple_gate")


def _pack(parts):
    flat = jnp.concatenate([a.reshape(-1) for a in parts])
    rows = -(-flat.shape[0] // 1024) * 8
    return jnp.pad(flat, (0, rows * 128 - flat.shape[0])).reshape(rows, 128)


def _unpack(packed, like):
    flat, out, at = packed.reshape(-1), [], 0
    for a in like:
        out.append(flat[at:at + a.size].reshape(a.shape))
        at += a.size
    return out


def kernel(x, p, w_in, w_sb_out, w_ca_out, w_mix_out, rel_bias, g_mix, g_ffn, g_ple, g_final, w_ffn_in, w_ffn_out, w_ple_in, w_ple_gate, loss_target, m_w_in, m_w_sb_out, m_w_ca_out, m_w_mix_out, m_rel_bias, m_g_mix, m_g_ffn, m_g_ple, m_g_final, m_w_ffn_in, m_w_ffn_out, m_w_ple_in, m_w_ple_gate, v_w_in, v_w_sb_out, v_w_ca_out, v_w_mix_out, v_rel_bias, v_g_mix, v_g_ffn, v_g_ple, v_g_final, v_w_ffn_in, v_w_ffn_out, v_w_ple_in, v_w_ple_gate):
    weights = dict(w_in=w_in, w_sb_out=w_sb_out, w_ca_out=w_ca_out, w_mix_out=w_mix_out, rel_bias=rel_bias,
                   g_mix=g_mix, g_ffn=g_ffn, g_ple=g_ple, g_final=g_final, w_ffn_in=w_ffn_in,
                   w_ffn_out=w_ffn_out, w_ple_in=w_ple_in, w_ple_gate=w_ple_gate)
    m_in = dict(w_in=m_w_in, w_sb_out=m_w_sb_out, w_ca_out=m_w_ca_out, w_mix_out=m_w_mix_out, rel_bias=m_rel_bias,
                g_mix=m_g_mix, g_ffn=m_g_ffn, g_ple=m_g_ple, g_final=m_g_final, w_ffn_in=m_w_ffn_in,
                w_ffn_out=m_w_ffn_out, w_ple_in=m_w_ple_in, w_ple_gate=m_w_ple_gate)
    v_in = dict(w_in=v_w_in, w_sb_out=v_w_sb_out, w_ca_out=v_w_ca_out, w_mix_out=v_w_mix_out, rel_bias=v_rel_bias,
                g_mix=v_g_mix, g_ffn=v_g_ffn, g_ple=v_g_ple, g_final=v_g_final, w_ffn_in=v_w_ffn_in,
                w_ffn_out=v_w_ffn_out, w_ple_in=v_w_ple_in, w_ple_gate=v_w_ple_gate)

    pos = jnp.stack([2 * lax.axis_index("x") + lax.axis_index("y"), lax.axis_index("c")]).astype(jnp.int32)
    comm = _Comm(pos, {"w_in": _cast_place(w_in[0], AXIS["w_in"], pos, name="cast_w_in")})
    at = 0
    for n in HEAD_HOSTS:
        ride = comm.gather(("w_in",), "near", part=(at, HEAD_PARTS))
        comm.w[n] = _cast_place(weights[n][0], AXIS[n], pos, name="cast_" + n, ride=ride)
        at += 1
    for n, axis in BIG:
        if n not in comm.w:
            comm.w[n] = _cast_place(weights[n][0], axis, pos, name="cast_" + n)
    _run(comm.gather(("w_in",), "near", part=(at, HEAD_PARTS, HEAD_PARTS - at)), name="gather_w_in_near")
    _run(comm.gather(("w_in",), "far+"), name="gather_w_in_far")
    _run(comm.gather(("w_in",), "pair-"), name="gather_w_in_pair")
    small = dict(rel_bias=rel_bias[0], g_mix=g_mix, g_ffn=g_ffn, g_ple=g_ple, g_final=g_final.reshape(1, -1))
    loss, grad_x, gs = _step(x[0], p[0, 0], loss_target[0], small, comm)

    grads, delta, new_m, new_v = {}, {}, {}, {}
    for n in [n for n, _ in BIG if n != "w_in"] + ["w_in"]:
        if n == "w_in":
            _run(comm.tail_rest(), name="rs_chips_w_in")
            comm.sum(("w_in",))
            _run(comm.share(("w_in",)), name="rs_share_w_in")
        update = _adamw_sc if n in ON_SPARSECORE else _adamw
        g, d, nm, nv = update(weights[n][0], comm.result(n), m_in[n][0], v_in[n][0], name="adamw_" + n)
        grads[n], delta[n], new_m[n], new_v[n] = g[None], d[None], nm[None], nv[None]

    like = [weights[n] for n in SMALL]
    reduced = _small_all_reduce(_pack([gs[n] for n in SMALL] + [loss[:, :1]]), name="small_all_reduce")
    g_small = _unpack(reduced, like + [loss[:, :1]])
    total_loss = g_small[-1].reshape(())
    g_packed = _pack(g_small[:-1])
    _, d_s, m_s, v_s = _adamw(_pack(like), g_packed, _pack([m_in[n] for n in SMALL]), _pack([v_in[n] for n in SMALL]),
                           name="adamw_small")
    for n, g, d, nm, nv in zip(SMALL, g_small[:-1], _unpack(d_s, like), _unpack(m_s, like), _unpack(v_s, like)):
        grads[n], delta[n], new_m[n], new_v[n] = g, d, nm, nv

    return (total_loss, grad_x[None], *[grads[n] for n in ORDER], *[delta[n] for n in ORDER],
            *[new_m[n] for n in ORDER], *[new_v[n] for n in ORDER])
```

```python
import functools
import math

import jax
import jax.numpy as jnp
from jax import lax
from jax.experimental import pallas as pl
from jax.experimental.pallas import tpu as pltpu
from jax.experimental.pallas import tpu_sc as plsc

F32 = jnp.float32
BF16 = jnp.bfloat16

HEAD_DIM = 128
CHUNK = 64
LEFT_CHUNKS = 8
REL_CLIP = 128
N_REL = REL_CLIP + CHUNK
BAND = (LEFT_CHUNKS + 2) * CHUNK
CA_PER_STEP = 4
CA_ROWS = CA_PER_STEP * CHUNK
CA_BAND = BAND + CA_PER_STEP * CHUNK
CA_PAD = BAND
SB_BLOCK = 128
SB_KEYS = 512
SB_GROUPS = SB_KEYS // SB_BLOCK
SB_ROWS = SB_KEYS
EPS = 1e-6
NEG = -1e30

ADAM_LR = 0.001
ADAM_B1 = 0.9
ADAM_B2 = 0.999
ADAM_EPS = 1e-08
ADAM_WD = 0.01
ADAM_STEP = 10

VMEM_LIMIT = 48 * 1024 * 1024
MM_VMEM_BUDGET = 36 * 1024 * 1024
V7X_HBM_BYTES_PER_S = 3.7e12
GRID_STEP_S = 0.35e-6
MESH = pl.DeviceIdType.MESH
N_CHIPS = 4


def _pick(dim, prefs):
    for t in prefs:
        if dim % t == 0:
            return t
    raise ValueError(f"no tile for {dim}")


def _cparams(sem=None):
    return pltpu.CompilerParams(dimension_semantics=sem, vmem_limit_bytes=VMEM_LIMIT)


def _sigmoid(v):
    return 1.0 / (1.0 + jnp.exp(-v))


def _dot(a, b, dims):
    return lax.dot_general(a, b, (dims, ((), ())), preferred_element_type=F32)


def _dot_nn(a, b):
    return _dot(a, b, ((1,), (0,)))


def _dot_nt(a, b):
    return _dot(a, b, ((1,), (1,)))


def _dot_tn(a, b):
    return _dot(a, b, ((0,), (0,)))


HBM = pl.BlockSpec(memory_space=pltpu.HBM)


class _Ride:
    def __init__(self):
        self.items = []

    def add(self, ins, outs, aliases, n_sems, start, finish, sink):
        self.items.append((ins, outs, aliases, n_sems, start, finish, sink))


def _call(body, args, *, name, grid, in_specs, out_specs, out_shape, scratch_shapes=(), sem=None, ride=None,
          scalars=None, onto=()):
    items = ride.items if ride is not None else []
    if onto:
        args, in_specs = list(args) + list(onto), list(in_specs) + [HBM] * len(onto)
        inner, body = body, lambda *refs: inner(*refs[:len(args) - len(onto)], *refs[len(args):])
    n_in, n_out, n_scr = len(args), len(out_shape), len(scratch_shapes)
    r_ins = [a for it in items for a in it[0]]
    r_outs = [o for it in items for o in it[1]]
    updated = [id(it[0][i]) for it in items for i in it[2]]
    assert len(set(updated)) == len(updated), "one call may update a buffer in place only once"
    aliases, a, b = {n_in - len(onto) + t: t for t in range(len(onto))}, n_in, n_out
    for it in items:
        aliases.update({a + i: b + o for i, o in it[2].items()})
        a, b = a + len(it[0]), b + len(it[1])
    sems = [pltpu.SemaphoreType.DMA((it[3],)) for it in items for _ in range(2)]

    def wrapped(*refs):
        head, refs = (refs[:1], refs[1:]) if scalars is not None else ((), refs)
        ins, rin = refs[:n_in], refs[n_in:n_in + len(r_ins)]
        at = n_in + len(r_ins)
        outs, rout = refs[at:at + n_out], refs[at + n_out:at + n_out + len(r_outs)]
        at += n_out + len(r_outs)
        scr, rsem = refs[at:at + n_scr], refs[at + n_scr:]

        def each(which):
            a = b = 0
            for q, it in enumerate(items):
                it[which](rin[a:a + len(it[0])], rout[b:b + len(it[1])], rsem[2 * q], rsem[2 * q + 1])
                a, b = a + len(it[0]), b + len(it[1])

        if items:
            ids = [pl.program_id(d) for d in range(len(grid))]
            first = functools.reduce(jnp.logical_and, [i == 0 for i in ids])
            last = functools.reduce(jnp.logical_and, [i == g - 1 for i, g in zip(ids, grid)])
            pl.when(first)(lambda: each(4))
        body(*head, *ins, *outs, *scr)
        if items:
            pl.when(last)(lambda: each(5))

    specs = dict(grid=grid, in_specs=list(in_specs) + [HBM] * len(r_ins),
                 out_specs=list(out_specs) + [HBM] * len(r_outs), scratch_shapes=list(scratch_shapes) + sems)
    if scalars is not None:
        specs = dict(grid_spec=pltpu.PrefetchScalarGridSpec(num_scalar_prefetch=1, **specs))
        aliases = {i + 1: o for i, o in aliases.items()}
    res = pl.pallas_call(
        wrapped, name=name, **specs,
        out_shape=list(out_shape) + r_outs,
        input_output_aliases=aliases,
        compiler_params=_cparams(("arbitrary",) * len(grid) if items else sem),
    )(*(() if scalars is None else (scalars,)), *args, *r_ins)
    b = n_out
    for it in items:
        it[6](res[b:b + len(it[1])])
        b += len(it[1])
    return list(res[:n_out])


def _mm_tiles(m, n_align, n, k, a_bytes, b_bytes, out_bytes):
    best = None
    tks = sorted({t for t in (k, k // 2, k // 4, 2048, 1024, 512, 256, 128) if t <= k and k % t == 0 and t % 128 == 0})
    for tm in (t for t in (2048, 1024, 512, 256, 128) if m % t == 0):
        for tn in (t for t in (2048, 1024, 512, 256, 128) if n_align % t == 0):
            for tk in tks:
                nk = k // tk
                vmem = 2 * (tm * tk * a_bytes + tk * tn * b_bytes + tm * tn * out_bytes) + tm * tn * 4
                if vmem > MM_VMEM_BUDGET:
                    continue
                traffic = m * k * a_bytes * (n // tn if nk > 1 else 1) + k * n * b_bytes * (m // tm)
                traffic += tm * tk * a_bytes + tk * tn * b_bytes + tm * tn * out_bytes
                traffic += m * n * 4 * nk if nk > 1 else 0
                cost = traffic / V7X_HBM_BYTES_PER_S + (m // tm) * (n // tn) * nk * GRID_STEP_S
                if best is None or cost < best[0]:
                    best = (cost, tm, tn, tk)
    return best[1:]


def _mm(a, b, mode, out_dtypes, *, name, n=None, b_col_off=0, resid=None, ride=None, rows=None, onto=(), m_half=None):
    if mode == "nn":
        m, k = a.shape
        n = b.shape[1] if n is None else n
    elif mode == "nt":
        m, k = a.shape
        n = b.shape[0]
    else:
        k, m = a.shape
        n = b.shape[1]
    if m_half is not None:
        m //= 2
    m_all, (row0, m) = m, (0, m) if rows is None else rows
    n_out = len(out_dtypes)
    has_resid = resid is not None
    out_bytes = sum(jnp.dtype(dt).itemsize for dt in out_dtypes) + (4 if has_resid else 0)
    tm, tn, tk = _mm_tiles(math.gcd(m, row0) if row0 else m, math.gcd(n, b_col_off) if b_col_off else n, n, k,
                           a.dtype.itemsize, b.dtype.itemsize, out_bytes)
    nk = k // tk
    boff, roff = b_col_off // tn, row0 // tm
    dot = {"nn": _dot_nn, "nt": _dot_nt, "tn": _dot_tn}[mode]
    if m_half is None:
        half = lambda: 0
    else:
        half = lambda pos_ref: (pos_ref[1] if m_half[0] else 1 - pos_ref[1]) * (m // tm)

    def body(*refs):
        refs = refs[m_half is not None:]
        a_ref, b_ref = refs[0], refs[1]
        r_ref = refs[2] if has_resid else None
        o_refs = refs[2 + has_resid: 2 + has_resid + n_out]

        def finish(r):
            if has_resid:
                r = r + r_ref[...]
            for o_ref in o_refs:
                o_ref[...] = r.astype(o_ref.dtype)

        part = dot(a_ref[...].astype(BF16), b_ref[...].astype(BF16))
        if nk == 1:
            finish(part)
            return
        acc_ref = refs[-1]
        kk = pl.program_id(2)

        @pl.when(kk == 0)
        def _():
            acc_ref[...] = part

        @pl.when(kk > 0)
        def _():
            acc_ref[...] += part

        @pl.when(kk == nk - 1)
        def _():
            finish(acc_ref[...])

    if mode == "nn":
        a_spec = pl.BlockSpec((tm, tk), lambda i, j, kk, *_: (i + roff, kk))
        b_spec = pl.BlockSpec((tk, tn), lambda i, j, kk, *_: (kk, j + boff))
    elif mode == "nt":
        a_spec = pl.BlockSpec((tm, tk), lambda i, j, kk, *_: (i + roff, kk))
        b_spec = pl.BlockSpec((tn, tk), lambda i, j, kk, *_: (j, kk))
    else:
        a_spec = pl.BlockSpec((tk, tm), lambda i, j, kk, *pos: (kk, i + half(*pos)))
        b_spec = pl.BlockSpec((tk, tn), lambda i, j, kk, *_: (kk, j))
    o_spec = pl.BlockSpec((tm, tn), lambda i, j, kk, *_: (i + roff, j))
    in_specs = [a_spec, b_spec] + ([o_spec] if has_resid else [])
    args = [a, b] + ([resid] if has_resid else [])
    outs = _call(
        body, args, name=name,
        grid=(m // tm, n // tn, nk),
        in_specs=in_specs,
        out_specs=[o_spec] * n_out,
        out_shape=[jax.ShapeDtypeStruct((m_all, n), dt) for dt in out_dtypes],
        scratch_shapes=[pltpu.VMEM((tm, tn), F32)] if nk > 1 else [],
        sem=("parallel", "parallel", "arbitrary"), ride=ride, onto=onto,
        scalars=None if m_half is None else m_half[1])
    return outs[0] if n_out == 1 else tuple(outs)


def _row_tile(s):
    return _pick(s, (256, 128))


def _rms_fwd(x, g, *, name, ride=None):
    s, d = x.shape
    tr = _row_tile(s)

    def body(x_ref, g_ref, o_ref):
        xv = x_ref[...]
        r = lax.rsqrt(jnp.mean(xv * xv, axis=1, keepdims=True) + EPS)
        o_ref[...] = (xv * r * g_ref[...]).astype(o_ref.dtype)

    return _call(
        body, [x, g], name=name, grid=(s // tr,),
        in_specs=[pl.BlockSpec((tr, d), lambda i: (i, 0)), pl.BlockSpec((1, d), lambda i: (0, 0))],
        out_specs=[pl.BlockSpec((tr, d), lambda i: (i, 0))],
        out_shape=[jax.ShapeDtypeStruct((s, d), BF16)], sem=("parallel",), ride=ride)[0]


def _rms_bwd(x, g, dh, dres, *, name, ride=None):
    s, d = x.shape
    tr = _row_tile(s)

    def body(x_ref, g_ref, dh_ref, dres_ref, dx_ref, dx16_ref, dg_ref):
        i = pl.program_id(0)
        xv = x_ref[...]
        r = lax.rsqrt(jnp.mean(xv * xv, axis=1, keepdims=True) + EPS)
        xhat = xv * r
        dhv = dh_ref[...]
        dxhat = dhv * g_ref[...]
        proj = jnp.mean(dxhat * xhat, axis=1, keepdims=True)
        dx = dres_ref[...] + r * (dxhat - xhat * proj)
        dx_ref[...] = dx
        dx16_ref[...] = dx.astype(dx16_ref.dtype)

        @pl.when(i == 0)
        def _():
            dg_ref[...] = jnp.zeros_like(dg_ref)

        dg_ref[...] += jnp.sum(dhv * xhat, axis=0, keepdims=True)

    row = pl.BlockSpec((tr, d), lambda i: (i, 0))
    vec = pl.BlockSpec((1, d), lambda i: (0, 0))
    return _call(
        body, [x, g, dh, dres], name=name, grid=(s // tr,),
        in_specs=[row, vec, row, row],
        out_specs=[row, row, vec],
        out_shape=[jax.ShapeDtypeStruct((s, d), F32), jax.ShapeDtypeStruct((s, d), BF16),
                   jax.ShapeDtypeStruct((1, d), F32)],
        sem=("arbitrary",), ride=ride)


def _final_loss(x, g, target, *, name):
    s, d = x.shape
    tr = _row_tile(s)

    def body(x_ref, g_ref, t_ref, dx_ref, dg_ref, loss_ref):
        i = pl.program_id(0)
        xv = x_ref[...]
        gv = g_ref[...]
        r = lax.rsqrt(jnp.mean(xv * xv, axis=1, keepdims=True) + EPS)
        xhat = xv * r
        err = xhat * gv - t_ref[...]
        dy = err * (1.0 / d)
        dxhat = dy * gv
        proj = jnp.mean(dxhat * xhat, axis=1, keepdims=True)
        dx_ref[...] = r * (dxhat - xhat * proj)

        @pl.when(i == 0)
        def _():
            dg_ref[...] = jnp.zeros_like(dg_ref)
            loss_ref[...] = jnp.zeros_like(loss_ref)

        dg_ref[...] += jnp.sum(dy * xhat, axis=0, keepdims=True)
        part = 0.5 * jnp.sum(jnp.mean(err * err, axis=1, keepdims=True), axis=0, keepdims=True)
        loss_ref[...] += jnp.broadcast_to(part, loss_ref.shape)

    row = pl.BlockSpec((tr, d), lambda i: (i, 0))
    vec = pl.BlockSpec((1, d), lambda i: (0, 0))
    return pl.pallas_call(
        body, name=name, grid=(s // tr,),
        in_specs=[row, vec, row],
        out_specs=[row, vec, pl.BlockSpec((1, 128), lambda i: (0, 0))],
        out_shape=[jax.ShapeDtypeStruct((s, d), F32), jax.ShapeDtypeStruct((1, d), F32),
                   jax.ShapeDtypeStruct((1, 128), F32)],
        compiler_params=_cparams(("arbitrary",)),
    )(x, g, target)


def _ew(body, ins, in_blocks, outs, out_blocks, grid, *, name, ride=None):
    return _call(body, ins, name=name, grid=grid,
                 in_specs=[pl.BlockSpec(bs, im) for bs, im in in_blocks],
                 out_specs=[pl.BlockSpec(bs, im) for bs, im in out_blocks],
                 out_shape=outs, sem=("parallel",) * len(grid), ride=ride)


def _gate_merge_fwd(gates, o_sb, o_ca, *, name, ride=None):
    s, d = o_sb.shape
    tr, tc = _row_tile(s), _pick(d, (1024, 512, 256, 128))
    nc = d // tc

    def body(gs_ref, gc_ref, os_ref, oc_ref, m_ref):
        f32 = lambda ref: ref[...].astype(F32)
        m = _sigmoid(f32(gs_ref)) * f32(os_ref) + _sigmoid(f32(gc_ref)) * f32(oc_ref)
        m_ref[...] = m.astype(m_ref.dtype)

    blk = ((tr, tc), lambda i, j: (i, j))
    return _ew(body, [gates, gates, o_sb, o_ca],
               [blk, ((tr, tc), lambda i, j: (i, j + nc)), blk, blk],
               [jax.ShapeDtypeStruct((s, d), BF16)], [blk], (s // tr, nc), name=name, ride=ride)[0]


def _gate_merge_bwd(dmerged, gates, o_sb, o_ca, *, name):
    s, d = o_sb.shape
    tr, tc = _row_tile(s), _pick(d, (1024, 512, 256, 128))
    nc = d // tc

    def body(dm_ref, gs_ref, gc_ref, os_ref, oc_ref, dgs_ref, dgc_ref, dos_ref, doc_ref):
        f32 = lambda ref: ref[...].astype(F32)
        dm = dm_ref[...]
        ss = _sigmoid(f32(gs_ref))
        sc = _sigmoid(f32(gc_ref))
        dgs_ref[...] = (dm * f32(os_ref) * ss * (1.0 - ss)).astype(dgs_ref.dtype)
        dgc_ref[...] = (dm * f32(oc_ref) * sc * (1.0 - sc)).astype(dgc_ref.dtype)
        dos_ref[...] = (dm * ss).astype(dos_ref.dtype)
        doc_ref[...] = (dm * sc).astype(doc_ref.dtype)

    blk = ((tr, tc), lambda i, j: (i, j))
    sd = jax.ShapeDtypeStruct((s, d), BF16)
    return _ew(body, [dmerged, gates, gates, o_sb, o_ca],
               [blk, blk, ((tr, tc), lambda i, j: (i, j + nc)), blk, blk],
               [sd, sd, sd, sd], [blk, blk, blk, blk], (s // tr, nc), name=name)


def _swiglu_fwd(gu, *, name, ride=None):
    s, f2 = gu.shape
    f = f2 // 2
    tr, tc = 128, _pick(f, (512, 256, 128))

    def body(gu_ref, a_ref):
        for at in range(0, f, tc):
            gv = gu_ref[:, at:at + tc].astype(F32)
            a_ref[:, at:at + tc] = (gv * _sigmoid(gv) * gu_ref[:, f + at:f + at + tc].astype(F32)).astype(a_ref.dtype)

    row = lambda i: (i, 0)
    return _ew(body, [gu], [((tr, f2), row)], [jax.ShapeDtypeStruct((s, f), BF16)], [((tr, f), row)],
               (s // tr,), name=name, ride=ride)[0]


def _swiglu_bwd(dact, gu, *, name):
    s, f2 = gu.shape
    f = f2 // 2
    tr, tc = 128, _pick(f, (512, 256, 128))

    def body(da_ref, gu_ref, o_ref):
        for at in range(0, f, tc):
            da = da_ref[:, at:at + tc].astype(F32)
            gv = gu_ref[:, at:at + tc].astype(F32)
            sg = _sigmoid(gv)
            uv = gu_ref[:, f + at:f + at + tc].astype(F32)
            o_ref[:, at:at + tc] = (da * uv * sg * (1.0 + gv * (1.0 - sg))).astype(o_ref.dtype)
            o_ref[:, f + at:f + at + tc] = (da * gv * sg).astype(o_ref.dtype)

    row = lambda i: (i, 0)
    return _ew(body, [dact, gu], [((tr, f), row), ((tr, f2), row)], [jax.ShapeDtypeStruct((s, f2), BF16)],
               [((tr, f2), row)], (s // tr,), name=name)[0]


def _concat_cols(parts, *, name):
    s = parts[0].shape[0]
    widths = [p.shape[1] for p in parts]
    tr = 256

    def body(*refs):
        o_ref, at = refs[-1], 0
        for p_ref, width in zip(refs, widths):
            o_ref[:, at:at + width] = p_ref[...]
            at += width

    row = lambda i: (i, 0)
    return _ew(body, list(parts), [((tr, width), row) for width in widths],
               [jax.ShapeDtypeStruct((s, sum(widths)), parts[0].dtype)], [((tr, sum(widths)), row)],
               (s // tr,), name=name)[0]


def _ple_fwd(x, t, pe, *, name):
    s, d = x.shape
    tr, tc = _row_tile(s), _pick(d, (1024, 512, 256, 128))

    def body(x_ref, t_ref, p_ref, o_ref):
        o_ref[...] = x_ref[...] + _sigmoid(t_ref[...].astype(F32)) * p_ref[...].astype(F32)

    blk = ((tr, tc), lambda i, j: (i, j))
    return _ew(body, [x, t, pe], [blk, blk, blk],
               [jax.ShapeDtypeStruct((s, d), F32)], [blk], (s // tr, d // tc), name=name)[0]


def _ple_bwd(dx, t, pe, *, name):
    s, d = dx.shape
    tr, tc = _row_tile(s), _pick(d, (1024, 512, 256, 128))

    def body(dx_ref, t_ref, p_ref, dt_ref, dp_ref):
        dxv = dx_ref[...]
        sg = _sigmoid(t_ref[...].astype(F32))
        dt_ref[...] = (dxv * p_ref[...].astype(F32) * sg * (1.0 - sg)).astype(dt_ref.dtype)
        dp_ref[...] = (dxv * sg).astype(dp_ref.dtype)

    blk = ((tr, tc), lambda i, j: (i, j))
    sd = jax.ShapeDtypeStruct((s, d), BF16)
    return _ew(body, [dx, t, pe], [blk, blk, blk], [sd, sd], [blk, blk], (s // tr, d // tc), name=name)


def _sb_tri(later):
    row = lax.broadcasted_iota(jnp.int32, (SB_BLOCK, SB_BLOCK), 0)
    col = lax.broadcasted_iota(jnp.int32, (SB_BLOCK, SB_BLOCK), 1)
    tri = (row > col) if later else (row < col)
    return jnp.concatenate([tri.astype(BF16), jnp.ones((SB_BLOCK, SB_BLOCK), BF16)], axis=1)


def _sb_valid(i, j, own):
    if not own:
        return None
    qi = i * SB_ROWS + lax.broadcasted_iota(jnp.int32, (SB_ROWS, SB_KEYS), 0)
    ki = j * SB_KEYS + lax.broadcasted_iota(jnp.int32, (SB_ROWS, SB_KEYS), 1)
    return ki < qi


def _sb_scan(v, tri, run, later):
    hi = v.astype(BF16)
    lo = (v - hi.astype(F32)).astype(BF16)
    outs = [None] * SB_GROUPS
    for b in (reversed(range(SB_GROUPS)) if later else range(SB_GROUPS)):
        cols = slice(b * SB_BLOCK, (b + 1) * SB_BLOCK)
        r = _dot_nn(hi[:, cols], tri) + _dot_nn(lo[:, cols], tri)
        outs[b] = r[:, :SB_BLOCK] + run
        run = run + r[:, SB_BLOCK:]
    return jnp.concatenate(outs, axis=1), run


def _masked(valid, v):
    return v if valid is None else jnp.where(valid, v, 0.0)


def _sb_scores(q, kj, scale, valid):
    z = _dot_nt(q, kj) * scale
    t = jnp.log(1.0 + jnp.exp(-jnp.abs(z)))
    return jnp.minimum(z, 0.0) - t, _masked(valid, -jnp.maximum(z, 0.0) - t)


def _sb_specs(h_count, s, col0):
    q_spec = pl.BlockSpec((SB_ROWS, HEAD_DIM), lambda h, i: (i, col0 + h))
    k_spec = pl.BlockSpec((s, HEAD_DIM), lambda h, i: (0, col0 + h_count + h))
    v_spec = pl.BlockSpec((s, HEAD_DIM), lambda h, i: (0, col0 + 2 * h_count + h))
    return q_spec, k_spec, v_spec


def _sb_fwd(qkv, n_heads, col0, *, name, ride=None):
    s = qkv.shape[0]
    nq = s // SB_ROWS
    scale = HEAD_DIM ** -0.5

    def body(q_ref, k_ref, v_ref, o_ref):
        i = pl.program_id(1)
        q = q_ref[...]
        tri = _sb_tri(later=True)

        def step(j, carry, own):
            run, acc = carry
            off = pl.multiple_of(j * SB_KEYS, SB_KEYS)
            valid = _sb_valid(i, j, own)
            ls, lk = _sb_scores(q, k_ref[pl.ds(off, SB_KEYS), :], scale, valid)
            between, run = _sb_scan(lk, tri, run, later=True)
            a = _masked(valid, jnp.exp(ls + between))
            return run, acc + _dot_nn(a.astype(BF16), v_ref[pl.ds(off, SB_KEYS), :])

        carry = step(i, (jnp.zeros((SB_ROWS, SB_BLOCK), F32), jnp.zeros((SB_ROWS, HEAD_DIM), F32)), True)
        _, acc = lax.fori_loop(0, i, lambda jj, c: step(i - 1 - jj, c, False), carry)
        o_ref[...] = acc.astype(o_ref.dtype)

    q_spec, k_spec, v_spec = _sb_specs(n_heads, s, col0)
    return _call(
        body, [qkv, qkv, qkv], name=name, grid=(n_heads, nq),
        in_specs=[q_spec, k_spec, v_spec],
        out_specs=[pl.BlockSpec((SB_ROWS, HEAD_DIM), lambda h, i: (i, h))],
        out_shape=[jax.ShapeDtypeStruct((s, n_heads * HEAD_DIM), BF16)],
        sem=("parallel", "arbitrary"), ride=ride)[0]


def _sb_bwd(qkv, dy, n_heads, col0, *, name, ride=None):
    s = qkv.shape[0]
    nq = s // SB_ROWS
    scale = HEAD_DIM ** -0.5

    def body(q_ref, k_ref, v_ref, dy_ref, dq_ref, dk_ref, dv_ref, e_scr, sg_scr, dk_acc, dv_acc):
        i = pl.program_id(1)
        q = q_ref[...]
        dyv = dy_ref[...]

        @pl.when(i == 0)
        def _():
            dk_acc[...] = jnp.zeros_like(dk_acc)
            dv_acc[...] = jnp.zeros_like(dv_acc)

        tri_later = _sb_tri(later=True)

        def pass1(j, run, own):
            off = pl.multiple_of(j * SB_KEYS, SB_KEYS)
            valid = _sb_valid(i, j, own)
            ls, lk = _sb_scores(q, k_ref[pl.ds(off, SB_KEYS), :], scale, valid)
            between, run = _sb_scan(lk, tri_later, run, later=True)
            a = _masked(valid, jnp.exp(ls + between))
            e_scr[j] = a * _dot_nt(dyv, v_ref[pl.ds(off, SB_KEYS), :])
            sg_scr[j] = jnp.exp(ls)
            dv_acc[pl.ds(off, SB_KEYS), :] += _dot_tn(a.astype(BF16), dyv)
            return run

        lax.fori_loop(0, i, lambda jj, run: pass1(i - 1 - jj, run, False),
                      pass1(i, jnp.zeros((SB_ROWS, SB_BLOCK), F32), True))

        tri_earlier = _sb_tri(later=False)

        def pass2(j, carry, own):
            run, dq = carry
            off = pl.multiple_of(j * SB_KEYS, SB_KEYS)
            kj = k_ref[pl.ds(off, SB_KEYS), :]
            sg = sg_scr[j]
            e = e_scr[j]
            before, run = _sb_scan(e, tri_earlier, run, later=False)
            dz = _masked(_sb_valid(i, j, own), e * (1.0 - sg) - sg * before) * scale
            dzb = dz.astype(BF16)
            dk_acc[pl.ds(off, SB_KEYS), :] += _dot_tn(dzb, q)
            return run, dq + _dot_nn(dzb, kj)

        init = (jnp.zeros((SB_ROWS, SB_BLOCK), F32), jnp.zeros((SB_ROWS, HEAD_DIM), F32))
        _, dq = pass2(i, lax.fori_loop(0, i, lambda j, c: pass2(j, c, False), init), True)
        dq_ref[...] = dq.astype(dq_ref.dtype)

        @pl.when(i == nq - 1)
        def _():
            dk_ref[...] = dk_acc[...].astype(dk_ref.dtype)
            dv_ref[...] = dv_acc[...].astype(dv_ref.dtype)

    q_spec, k_spec, v_spec = _sb_specs(n_heads, s, col0)
    blk = pl.BlockSpec((SB_ROWS, HEAD_DIM), lambda h, i: (i, h))
    full = pl.BlockSpec((s, HEAD_DIM), lambda h, i: (0, h))
    sd = jax.ShapeDtypeStruct((s, n_heads * HEAD_DIM), BF16)
    return _call(
        body, [qkv, qkv, qkv, dy], name=name, grid=(n_heads, nq),
        in_specs=[q_spec, k_spec, v_spec, blk],
        out_specs=[blk, full, full],
        out_shape=[sd, sd, sd],
        scratch_shapes=[pltpu.VMEM((s // SB_KEYS, SB_ROWS, SB_KEYS), F32), pltpu.VMEM((s // SB_KEYS, SB_ROWS, SB_KEYS), F32),
                        pltpu.VMEM((s, HEAD_DIM), F32), pltpu.VMEM((s, HEAD_DIM), F32)],
        sem=("parallel", "arbitrary"), ride=ride)


def _band_bias(rel_bias):
    h = rel_bias.shape[0]
    width = BAND + CHUNK
    first = width - 1 - N_REL
    line = jnp.concatenate([jnp.broadcast_to(rel_bias[:, :1], (h, first)), rel_bias], axis=1)
    tiled = jnp.broadcast_to(line[:, None, :], (h, CHUNK, width - 1)).reshape(h, CHUNK * (width - 1))
    skew = jnp.pad(tiled, ((0, 0), (0, CHUNK))).reshape(h, CHUNK, width)[:, ::-1, :BAND]
    seen = jnp.arange(BAND) >= CHUNK
    return jnp.where(seen[None, None, :], skew, NEG)


def _band_bias_grad(dbias):
    h = dbias.shape[0]
    width = BAND + CHUNK
    flipped = jnp.pad(dbias[:, ::-1, :], ((0, 0), (0, 0), (0, CHUNK)))
    skew = flipped.reshape(h, CHUNK * width)[:, :CHUNK * (width - 1)].reshape(h, CHUNK, width - 1)
    diag = jnp.sum(skew, axis=1)
    first = width - 1 - N_REL
    clipped = jnp.sum(diag[:, :first + 1], axis=1, keepdims=True)
    return jnp.concatenate([clipped, diag[:, first + 1:]], axis=1)


def _group_bias(band):
    return jnp.concatenate([jnp.pad(band, ((0, 0), (0, 0), ((u + 1) * CHUNK, (CA_PER_STEP - 1 - u) * CHUNK)),
                                    constant_values=NEG) for u in range(CA_PER_STEP)], axis=1)


def _group_bias_grad(dgroup):
    return sum(dgroup[:, u * CHUNK:(u + 1) * CHUNK, (u + 1) * CHUNK:(u + 1) * CHUNK + BAND] for u in range(CA_PER_STEP))


def _ca_load_padded(k_ref, v_ref, kp, vp, s):
    kp[pl.ds(0, CA_PAD), :] = jnp.zeros((CA_PAD, HEAD_DIM), kp.dtype)
    vp[pl.ds(0, CA_PAD), :] = jnp.zeros((CA_PAD, HEAD_DIM), vp.dtype)
    kp[pl.ds(CA_PAD, s), :] = k_ref[...]
    vp[pl.ds(CA_PAD, s), :] = v_ref[...]


def _ca_weights(q, kb, bias, off, scale):
    z = _dot_nt(q, kb) * scale + bias
    pos = off + lax.broadcasted_iota(jnp.int32, (CA_ROWS, CA_BAND), 1)
    z = jnp.where(pos >= CA_PAD, z, NEG)
    p = jnp.exp(z - jnp.max(z, axis=1, keepdims=True))
    return p / jnp.sum(p, axis=1, keepdims=True)


def _ca_specs(h_count, s, col0):
    q_spec = pl.BlockSpec((CA_ROWS, HEAD_DIM), lambda h, c: (c, col0 + h))
    k_spec = pl.BlockSpec((s, HEAD_DIM), lambda h, c: (0, col0 + h_count + h))
    v_spec = pl.BlockSpec((s, HEAD_DIM), lambda h, c: (0, col0 + 2 * h_count + h))
    b_spec = pl.BlockSpec((1, CA_ROWS, CA_BAND), lambda h, c: (h, 0, 0))
    return q_spec, k_spec, v_spec, b_spec


def _ca_fwd(qkv, bias, n_heads, col0, *, name, ride=None):
    s = qkv.shape[0]
    nc = s // CA_ROWS
    scale = HEAD_DIM ** -0.5

    def body(q_ref, k_ref, v_ref, b_ref, o_ref, kp, vp):
        c = pl.program_id(1)

        @pl.when(c == 0)
        def _():
            _ca_load_padded(k_ref, v_ref, kp, vp, s)

        off = pl.multiple_of(c * CA_ROWS, CA_ROWS)
        w = _ca_weights(q_ref[...], kp[pl.ds(off, CA_BAND), :], b_ref[0], off, scale)
        o_ref[...] = _dot_nn(w.astype(BF16), vp[pl.ds(off, CA_BAND), :]).astype(o_ref.dtype)

    q_spec, k_spec, v_spec, b_spec = _ca_specs(n_heads, s, col0)
    return _call(
        body, [qkv, qkv, qkv, bias], name=name, grid=(n_heads, nc),
        in_specs=[q_spec, k_spec, v_spec, b_spec],
        out_specs=[pl.BlockSpec((CA_ROWS, HEAD_DIM), lambda h, c: (c, h))],
        out_shape=[jax.ShapeDtypeStruct((s, n_heads * HEAD_DIM), BF16)],
        scratch_shapes=[pltpu.VMEM((s + CA_PAD, HEAD_DIM), BF16), pltpu.VMEM((s + CA_PAD, HEAD_DIM), BF16)],
        sem=("parallel", "arbitrary"), ride=ride)[0]


def _ca_bwd(qkv, bias, dy, n_heads, col0, *, name, ride=None):
    s = qkv.shape[0]
    nc = s // CA_ROWS
    scale = HEAD_DIM ** -0.5

    def body(q_ref, k_ref, v_ref, b_ref, dy_ref, dq_ref, dk_ref, dv_ref, db_ref, kp, vp, dkp, dvp):
        c = pl.program_id(1)

        @pl.when(c == 0)
        def _():
            _ca_load_padded(k_ref, v_ref, kp, vp, s)
            dkp[...] = jnp.zeros_like(dkp)
            dvp[...] = jnp.zeros_like(dvp)
            db_ref[...] = jnp.zeros_like(db_ref)

        off = pl.multiple_of(c * CA_ROWS, CA_ROWS)
        band = pl.ds(off, CA_BAND)
        q = q_ref[...]
        dyv = dy_ref[...]
        kb = kp[band, :]
        w = _ca_weights(q, kb, b_ref[0], off, scale)
        dw = _dot_nt(dyv, vp[band, :])
        dvp[band, :] += _dot_tn(w.astype(BF16), dyv)
        dz = w * (dw - jnp.sum(w * dw, axis=1, keepdims=True))
        db_ref[0] += dz
        dzs = (dz * scale).astype(BF16)
        dq_ref[...] = _dot_nn(dzs, kb).astype(dq_ref.dtype)
        dkp[band, :] += _dot_tn(dzs, q)

        @pl.when(c == nc - 1)
        def _():
            dk_ref[...] = dkp[pl.ds(CA_PAD, s), :].astype(dk_ref.dtype)
            dv_ref[...] = dvp[pl.ds(CA_PAD, s), :].astype(dv_ref.dtype)

    q_spec, k_spec, v_spec, b_spec = _ca_specs(n_heads, s, col0)
    blk = pl.BlockSpec((CA_ROWS, HEAD_DIM), lambda h, c: (c, h))
    full = pl.BlockSpec((s, HEAD_DIM), lambda h, c: (0, h))
    sd = jax.ShapeDtypeStruct((s, n_heads * HEAD_DIM), BF16)
    return _call(
        body, [qkv, qkv, qkv, bias, dy], name=name, grid=(n_heads, nc),
        in_specs=[q_spec, k_spec, v_spec, b_spec, blk],
        out_specs=[blk, full, full, b_spec],
        out_shape=[sd, sd, sd, jax.ShapeDtypeStruct((n_heads, CA_ROWS, CA_BAND), F32)],
        scratch_shapes=[pltpu.VMEM((s + CA_PAD, HEAD_DIM), BF16), pltpu.VMEM((s + CA_PAD, HEAD_DIM), BF16),
                        pltpu.VMEM((s + CA_PAD, HEAD_DIM), F32), pltpu.VMEM((s + CA_PAD, HEAD_DIM), F32)],
        sem=("parallel", "arbitrary"), ride=ride)


EARLY = ("w_sb_out", "w_ca_out", "w_mix_out")


def _step(x, p, target, small, comm):
    w = comm.w
    d = x.shape[1]
    n_sb = w["w_sb_out"].shape[0] // HEAD_DIM
    n_ca = w["w_ca_out"].shape[0] // HEAD_DIM
    qkv_cols = 3 * HEAD_DIM * (n_sb + n_ca)
    ca_col0 = 3 * n_sb
    both = (F32, BF16)

    h1 = _rms_fwd(x, small["g_mix"], name="rms_mix")
    ffn, ple = ("w_ffn_in",), ("w_ple_gate", "w_ple_in")
    qkv = _mm(h1, w["w_in"], "nn", (BF16,), name="proj_qkv", n=qkv_cols, ride=comm.gather(EARLY, "near"))
    gates = _mm(h1, w["w_in"], "nn", (BF16,), name="proj_gates", n=2 * d, b_col_off=qkv_cols,
                ride=comm.gather(ffn, "near", comm.gather(EARLY, "far"), (0, 8)))
    bias = _group_bias(_band_bias(small["rel_bias"]))
    y_sb = _sb_fwd(qkv, n_sb, 0, name="sb_fwd", ride=comm.gather(ffn, "near", comm.gather(EARLY, "pair"), (1, 8, 7)))
    y_ca = _ca_fwd(qkv, bias, n_ca, ca_col0, name="ca_fwd", ride=comm.gather(ffn, "far"))
    out = ("w_ffn_out",)
    o_sb = _mm(y_sb, w["w_sb_out"], "nn", (BF16,), name="sb_out", ride=comm.gather(out, "near", part=(0, 4)))
    o_ca = _mm(y_ca, w["w_ca_out"], "nn", (BF16,), name="ca_out", ride=comm.gather(out, "near", part=(1, 4)))
    merged = _gate_merge_fwd(gates, o_sb, o_ca, name="gate_merge", ride=comm.gather(out, "near", part=(2, 4)))
    x1 = _mm(merged, w["w_mix_out"], "nn", (F32,), name="mix_out", resid=x,
             ride=comm.gather(out, "near", comm.gather(ffn, "pair"), (3, 4)))
    h2 = _rms_fwd(x1, small["g_ffn"], name="rms_ffn")
    gu = _mm(h2, w["w_ffn_in"], "nn", (BF16,), name="ffn_in", ride=comm.gather(ple, "near", comm.gather(out, "far")))
    act = _swiglu_fwd(gu, name="swiglu", ride=comm.gather(ple, "far", comm.gather(out, "pair")))
    x2 = _mm(act, w["w_ffn_out"], "nn", (F32,), name="ffn_out", resid=x1, ride=comm.gather(ple, "pair"))
    h3 = _rms_fwd(x2, small["g_ple"], name="rms_ple")
    t = _mm(h3, w["w_ple_gate"], "nn", (BF16,), name="ple_gate")
    pe = _mm(p, w["w_ple_in"], "nn", (BF16,), name="ple_in")
    x3 = _ple_fwd(x2, t, pe, name="ple_add")

    def halves(n, acts, dout, ride, name):
        if comm.pos is None:
            return comm.grad(n, *_mm(acts, dout, "tn", both, name=name))
        g16 = _mm(acts, dout, "tn", (BF16,), name=name + "_other", m_half=(False, comm.pos), ride=ride)
        comm.grad(n, None, g16, half=True)
        g32 = _mm(acts, dout, "tn", (F32,), name=name + "_own", m_half=(True, comm.pos), ride=comm.pair((n,)))
        comm.grad(n, g32, g16, half=True)

    gs = {}
    dx3, gs["g_final"], loss = _final_loss(x3, small["g_final"], target, name="final_loss")
    dt, dpe = _ple_bwd(dx3, t, pe, name="ple_bwd")
    comm.grad("w_ple_in", *_mm(p, dpe, "tn", both, name="dw_ple_in"))
    comm.grad("w_ple_gate", *_mm(h3, dt, "tn", both, name="dw_ple_gate"))
    ple = ("w_ple_in", "w_ple_gate")
    dh3 = _mm(dt, w["w_ple_gate"], "nt", (F32,), name="dh_ple", ride=comm.pair(ple))
    dx2, dx2_16, gs["g_ple"] = _rms_bwd(x2, small["g_ple"], dh3, dx3, name="rms_ple_bwd")
    comm.add(ple)
    comm.grad("w_ffn_out", *_mm(act, dx2_16, "tn", both, name="dw_ffn_out", ride=comm.chips(ple)))
    dact = _mm(dx2_16, w["w_ffn_out"], "nt", (BF16,), name="dact", ride=comm.pair(("w_ffn_out",)))
    dgu = _swiglu_bwd(dact, gu, name="swiglu_bwd")
    comm.sum(ple)
    comm.add(("w_ffn_out",))
    comm.grad("w_ffn_in", *_mm(h2, dgu, "tn", both, name="dw_ffn_in",
                               ride=comm.share(ple, comm.chips(("w_ffn_out",)))))
    dh2 = _mm(dgu, w["w_ffn_in"], "nt", (F32,), name="dh_ffn", ride=comm.pair(("w_ffn_in",)))
    dx1, dx1_16, gs["g_ffn"] = _rms_bwd(x1, small["g_ffn"], dh2, dx2, name="rms_ffn_bwd")
    comm.add(("w_ffn_in",))
    comm.sum(("w_ffn_out",))
    comm.grad("w_mix_out", *_mm(merged, dx1_16, "tn", both, name="dw_mix_out", ride=comm.share(("w_ffn_out",))))
    dmerged = _mm(dx1_16, w["w_mix_out"], "nt", (F32,), name="dmerged", ride=comm.pair(("w_mix_out",)))
    dg_sb, dg_ca, do_sb, do_ca = _gate_merge_bwd(dmerged, gates, o_sb, o_ca, name="gate_merge_bwd")
    comm.add(("w_mix_out",))
    comm.grad("w_sb_out", *_mm(y_sb, do_sb, "tn", both, name="dw_sb_out"))
    comm.grad("w_ca_out", *_mm(y_ca, do_ca, "tn", both, name="dw_ca_out"))
    outs = ("w_sb_out", "w_ca_out")
    dy_sb = _mm(do_sb, w["w_sb_out"], "nt", (BF16,), name="dy_sb", ride=comm.pair(outs))
    dy_ca = _mm(do_ca, w["w_ca_out"], "nt", (BF16,), name="dy_ca")
    comm.add(outs)
    dq_sb, dk_sb, dv_sb = _sb_bwd(qkv, dy_sb, n_sb, 0, name="sb_bwd", ride=comm.chips(("w_ffn_in",)))
    comm.sum(("w_ffn_in",))
    late = ("w_mix_out",) + outs
    dq_ca, dk_ca, dv_ca, dbias = _ca_bwd(qkv, bias, dy_ca, n_ca, ca_col0, name="ca_bwd",
                                         ride=comm.chips(late, comm.share(("w_ffn_in",))))
    comm.sum(late)
    gs["rel_bias"] = _band_bias_grad(_group_bias_grad(dbias))
    dproj = _concat_cols([dq_sb, dk_sb, dv_sb, dq_ca, dk_ca, dv_ca, dg_sb, dg_ca], name="dproj")
    halves("w_in", h1, dproj, comm.share(late), "dw_in")
    comm.add(("w_in",))
    half = x.shape[0] // 2
    dh1 = _mm(dproj, w["w_in"], "nt", (F32,), name="dh_mix_top", rows=(0, half), ride=comm.tail(TAIL_SECOND))
    dh1 = _mm(dproj, w["w_in"], "nt", (F32,), name="dh_mix_bottom", rows=(half, half), onto=(dh1,),
              ride=comm.tail(TAIL_SECOND))
    grad_x, _, gs["g_mix"] = _rms_bwd(x, small["g_mix"], dh1, dx1, name="rms_mix_bwd", ride=comm.tail(TAIL_FIRST))
    return loss, grad_x, gs


def _position():
    x, y, c = lax.axis_index("x"), lax.axis_index("y"), lax.axis_index("c")
    chips = [(1 - x, y), (x, 1 - y), (1 - x, 1 - y)]
    return x, y, c, chips


def _aligned(v, m):
    return v if isinstance(v, int) else pl.multiple_of(v, m)


def _piece_dims(shape, axis):
    k, n = shape
    return (k // 2, n // N_CHIPS) if axis == 1 else (k // N_CHIPS // 2, n)


def _piece(ref, shape, axis, j, h, part=(0, 1)):
    pr, pc = _piece_dims(shape, axis)
    nr = pr // part[1] * (part[2] if len(part) > 2 else 1)
    r0 = part[0] * (pr // part[1])
    if axis == 1:
        return ref.at[pl.ds(_aligned(h * pr + r0, 16), nr), pl.ds(_aligned(j * pc, 128), pc)]
    return ref.at[pl.ds(_aligned((2 * j + h) * pr + r0, 16), nr), :]


def _shard_half(ref, h):
    rows = ref.shape[0] // 2
    return ref.at[pl.ds(_aligned(h * rows, 16), rows), :]


def _remote(src, dst, send_sems, recv_sems, k, to):
    return pltpu.make_async_remote_copy(src_ref=src, dst_ref=dst, send_sem=send_sems.at[k],
                                        recv_sem=recv_sems.at[k], device_id=to, device_id_type=MESH)


def _prefetch_call(body, scalars, ins, in_specs, out_shape, out_specs, grid, *, name, ride=None):
    single = not isinstance(out_shape, (list, tuple))
    outs = _call(body, ins, name=name, grid=grid, in_specs=in_specs,
                 out_specs=[out_specs] if single else out_specs, out_shape=[out_shape] if single else out_shape,
                 sem=("parallel",) * len(grid), ride=ride, scalars=scalars)
    return outs[0] if single else outs


def _slab_tiles(pr, pc):
    tc = pc if pc <= 4096 else _pick(pc, (2048, 1024, 512, 256, 128))
    tr = next(t for t in (1024, 512, 256, 128, 64, 32, 16) if pr % t == 0 and t * tc <= 512 * 1024)
    return tr, tc


def _cast_place(w, axis, pos, *, name, ride=None):
    ks, ns = w.shape
    shape = (ks, ns * N_CHIPS) if axis == 1 else (ks * N_CHIPS, ns)
    tr, tc = _slab_tiles(ks, ns)
    nr, nc = ks // tr, ns // tc

    def body(pos_ref, w_ref, o_ref):
        o_ref[...] = w_ref[...].astype(o_ref.dtype)

    if axis == 1:
        out_map = lambda i, j, pos_ref: (i, pos_ref[0] * nc + j)
    else:
        out_map = lambda i, j, pos_ref: (pos_ref[0] * nr + i, j)
    return _prefetch_call(body, pos, [w], [pl.BlockSpec((tr, tc), lambda i, j, pos_ref: (i, j))],
                          jax.ShapeDtypeStruct(shape, BF16), pl.BlockSpec((tr, tc), out_map), (nr, nc), name=name, ride=ride)


def _run(ride, *, name):
    if ride is None:
        return

    def body(o_ref):
        o_ref[...] = jnp.zeros_like(o_ref)

    _call(body, [], name=name, grid=(1,), in_specs=[], out_specs=[pl.BlockSpec((8, 128), lambda i: (0, 0))],
          out_shape=[jax.ShapeDtypeStruct((8, 128), F32)], ride=ride)


def _ride_gather(ride, w, n, axis, stage, part=(0, 1)):
    shape = w[n].shape
    piece = functools.partial(_piece, shape=shape, axis=axis)
    span = part[2] if len(part) > 2 else 1
    halves = [(2 * part[0] + t * span, 2 * part[1], span) for t in range(2)]

    def copies(ins, outs, send_sems, recv_sems, arriving):
        x, y, c, chips = _position()
        me, (xn, yn, dn) = 2 * x + y, [2 * px + py for px, py in chips]
        if stage == "near":
            plan = [(me, c, part, (1 - x, y, c), xn, c, part), (me, c, part, (x, 1 - y, c), yn, c, part)]
        else:
            plan = []
        if stage in ("far", "far+"):
            plan = [(yn, c, halves[1], (1 - x, y, c), dn, c, halves[1]), (xn, c, halves[0], (x, 1 - y, c), dn, c, halves[0])]
        to_sibling = {"pair": (xn, yn, dn), "far+": (xn, yn), "pair-": (dn,)}.get(stage, ())
        plan += [(j, c, part, (x, y, 1 - c), j, 1 - c, part) for j in to_sibling]
        out = []
        for k, (chip, h, rows, to, from_chip, from_h, from_rows) in enumerate(plan):
            if arriving:
                lands = piece(outs[0], j=from_chip, h=from_h, part=from_rows)
                out.append(_remote(lands, lands, send_sems, recv_sems, k, to))
            else:
                out.append(_remote(piece(ins[0], j=chip, h=h, part=rows), piece(outs[0], j=chip, h=h, part=rows),
                                   send_sems, recv_sems, k, to))
        return out

    def start(*refs):
        for cp in copies(*refs, arriving=False):
            cp.start()

    def finish(*refs):
        for cp in copies(*refs, arriving=True):
            cp.wait_recv()
        for cp in copies(*refs, arriving=False):
            cp.wait_send()

    ride.add([w[n]], [jax.ShapeDtypeStruct(shape, w[n].dtype)], {0: 0}, 4, start, finish,
             lambda outs: w.__setitem__(n, outs[0]))


def _ride_pair(ride, st, axis):
    shape = st["g16"].shape
    pr, pc = (shape[0], shape[1] // N_CHIPS) if st.get("half") else _piece_dims(shape, axis)

    def copies(ins, outs, send_sems, recv_sems):
        x, y, c, _ = _position()
        if st.get("half"):
            pieces = [ins[0].at[:, pl.ds(j * pc, pc)] for j in range(N_CHIPS)]
        else:
            pieces = [_piece(ins[0], shape, axis, j, 1 - c) for j in range(N_CHIPS)]
        return [_remote(pieces[j], outs[0].at[j], send_sems, recv_sems, j, (x, y, 1 - c)) for j in range(N_CHIPS)]

    def start(*refs):
        for cp in copies(*refs):
            cp.start()

    def finish(*refs):
        for cp in copies(*refs):
            cp.wait()

    ride.add([st["g16"]], [jax.ShapeDtypeStruct((N_CHIPS, pr, pc), BF16)], {}, N_CHIPS, start, finish,
             lambda outs: st.__setitem__("sib", outs[0]))


def _ride_chips(ride, st, rows=None):
    _, pr, pc = st["s16"].shape
    r0, nr = (0, pr) if rows is None else rows

    def copies(ins, outs, send_sems, recv_sems):
        x, y, c, chips = _position()
        return [_remote(ins[0].at[2 * px + py, pl.ds(r0, nr), :], outs[0].at[k, pl.ds(r0, nr), :],
                        send_sems, recv_sems, k, (px, py, c)) for k, (px, py) in enumerate(chips)]

    def start(*refs):
        for cp in copies(*refs):
            cp.start()

    def finish(*refs):
        for cp in copies(*refs):
            cp.wait()

    ins, aliases = ([st["s16"], st["recv"]], {1: 0}) if "recv" in st else ([st["s16"]], {})
    ride.add(ins, [jax.ShapeDtypeStruct((3, pr, pc), BF16)], aliases, 3, start, finish,
             lambda outs: st.__setitem__("recv", outs[0]))


def _ride_share(ride, st):
    def sent(ins, outs, send_sems, recv_sems):
        x, y, c, _ = _position()
        return _remote(_shard_half(ins[0], c), _shard_half(outs[0], c), send_sems, recv_sems, 0, (x, y, 1 - c))

    def landed(ins, outs, send_sems, recv_sems):
        x, y, c, _ = _position()
        other = _shard_half(outs[0], 1 - c)
        return _remote(other, other, send_sems, recv_sems, 0, (x, y, 1 - c))

    def start(*refs):
        sent(*refs).start()

    def finish(*refs):
        landed(*refs).wait_recv()
        sent(*refs).wait_send()

    ride.add([st["shard"]], [jax.ShapeDtypeStruct(st["shard"].shape, F32)], {0: 0}, 1, start, finish,
             lambda outs: st.__setitem__("g", outs[0]))


def _piece_block(axis, nr, nc, chip, half=False):
    if half:
        return lambda *a: (a[-3], (a[0] if chip is None else chip(a[-1])) * nc + a[-2])
    if axis == 1:
        return lambda *a: ((a[-1][1] * nr + a[-3]), (a[0] if chip is None else chip(a[-1])) * nc + a[-2])
    return lambda *a: ((2 * (a[0] if chip is None else chip(a[-1])) + a[-1][1]) * nr + a[-3], a[-2])


def _pair_add(g32, sib, axis, pos, *, name, half=False):
    _, pr, pc = sib.shape
    tr, tc = _slab_tiles(pr, pc)
    nr, nc = pr // tr, pc // tc

    def body(pos_ref, g_ref, b_ref, o16_ref):
        o16_ref[0] = (g_ref[...] + b_ref[0].astype(F32)).astype(o16_ref.dtype)

    blk = pl.BlockSpec((1, tr, tc), lambda j, i, k, pos_ref: (j, i, k))
    return _prefetch_call(body, pos, [g32, sib], [pl.BlockSpec((tr, tc), _piece_block(axis, nr, nc, None, half)), blk],
                          jax.ShapeDtypeStruct(sib.shape, BF16), blk, (N_CHIPS, nr, nc), name=name)


def _chip_sum(g32, sib, recv, axis, pos, *, name, half=False):
    _, pr, pc = sib.shape
    tr, tc = _slab_tiles(pr, pc)
    nr, nc = pr // tr, pc // tc

    def body(pos_ref, g_ref, b_ref, r_ref, o_ref):
        pair = g_ref[...] + b_ref[0].astype(F32)
        o_ref[...] = ((pair + r_ref[0].astype(F32)) + r_ref[1].astype(F32)) + r_ref[2].astype(F32)

    return _prefetch_call(
        body, pos, [g32, sib, recv],
        [pl.BlockSpec((tr, tc), _piece_block(axis, nr, nc, lambda pos_ref: pos_ref[0], half)),
         pl.BlockSpec((1, tr, tc), lambda i, k, pos_ref: (pos_ref[0], i, k)),
         pl.BlockSpec((3, tr, tc), lambda i, k, pos_ref: (0, i, k))],
        jax.ShapeDtypeStruct((2 * pr, pc), F32),
        pl.BlockSpec((tr, tc), lambda i, k, pos_ref: (pos_ref[1] * nr + i, k)), (nr, nc), name=name)


class _Comm:
    def __init__(self, pos, w):
        self.pos, self.w, self.st = pos, w, {n: {} for n, _ in BIG}

    def gather(self, names, stage, ride=None, part=(0, 1)):
        ride = _Ride() if ride is None else ride
        for n in names:
            _ride_gather(ride, self.w, n, AXIS[n], stage, part)
        return ride

    def grad(self, n, g32, g16, half=False):
        self.st[n].update(g32=g32, g16=g16, half=half)

    def pair(self, names, ride=None):
        ride = _Ride() if ride is None else ride
        for n in names:
            _ride_pair(ride, self.st[n], AXIS[n])
        return ride

    def add(self, names):
        for n in names:
            st = self.st[n]
            st["s16"] = _pair_add(st["g32"], st["sib"], AXIS[n], self.pos, name="rs_add_" + n, half=st["half"])

    def chips(self, names, ride=None, rows=None):
        ride = _Ride() if ride is None else ride
        for n in names:
            _ride_chips(ride, self.st[n], rows)
        return ride

    def sum(self, names):
        for n in names:
            st = self.st[n]
            st["shard"] = _chip_sum(st["g32"], st["sib"], st["recv"], AXIS[n], self.pos, name="rs_sum_" + n,
                                    half=st["half"])

    def share(self, names, ride=None):
        ride = _Ride() if ride is None else ride
        for n in names:
            _ride_share(ride, self.st[n])
        return ride

    def tail(self, count):
        st = self.st["w_in"]
        rows, at = st["s16"].shape[1], st.get("at", 0)
        st["at"] = at + count
        return self.chips(("w_in",), rows=(at * rows // TAIL_PARTS, count * rows // TAIL_PARTS))

    def tail_rest(self):
        return self.tail(TAIL_PARTS - self.st["w_in"].get("at", 0))

    def result(self, n):
        return self.st[n]["g"]


class _NoComm:
    pos = None

    def __init__(self, w):
        self.w, self.st = w, {}

    def grad(self, n, g32, g16, half=False):
        self.st[n] = (g32, g16)

    def result(self, n):
        return self.st[n]

    def add(self, names):
        pass

    sum = add

    def gather(self, names, *args, **kwargs):
        return None

    pair = chips = share = tail = gather


def _small_all_reduce(vec, *, name):
    r = vec.shape[0]

    def body(vec_ref, out_ref, slots, send_sems, recv_sems):
        x, y, c, _ = _position()
        me = 4 * x + 2 * y + c
        slots[me] = vec_ref[...]
        sends = []
        for k in range(1, 8):
            to = (x ^ (k >> 2), y ^ ((k >> 1) & 1), c ^ (k & 1))
            cp = _remote(slots.at[me], slots.at[me], send_sems, recv_sems, k - 1, to)
            cp.start()
            sends.append(cp)
        for k in range(1, 8):
            frm = 4 * (x ^ (k >> 2)) + 2 * (y ^ ((k >> 1) & 1)) + (c ^ (k & 1))
            _remote(slots.at[frm], slots.at[frm], send_sems, recv_sems, k - 1, (x, y, c)).wait_recv()
        for cp in sends:
            cp.wait_send()
        total = slots[0]
        for d in range(1, 8):
            total = total + slots[d]
        out_ref[...] = total

    return pl.pallas_call(
        body, name=name,
        in_specs=[pl.BlockSpec(memory_space=pltpu.VMEM)], out_specs=pl.BlockSpec(memory_space=pltpu.VMEM),
        out_shape=jax.ShapeDtypeStruct((r, 128), F32),
        scratch_shapes=[pltpu.VMEM((8, r, 128), F32), pltpu.SemaphoreType.DMA((7,)), pltpu.SemaphoreType.DMA((7,))],
    )(vec)


SC_TILES = 32
SC_LANES = 16
SC_TILE_BUDGET = 400 * 1024


def _adamw_update(wv, gv, mv, vv):
    nm = ADAM_B1 * mv + (1.0 - ADAM_B1) * gv
    nv = ADAM_B2 * vv + (1.0 - ADAM_B2) * (gv * gv)
    m_hat = nm / (1.0 - ADAM_B1 ** ADAM_STEP)
    v_hat = nv / (1.0 - ADAM_B2 ** ADAM_STEP)
    return -ADAM_LR * (m_hat / (jnp.sqrt(v_hat) + ADAM_EPS) + ADAM_WD * wv), nm, nv


def _adamw_sc(w, g, m, v, *, name):
    r, c = w.shape
    groups = r // 8
    per_tile = -(-groups // SC_TILES)
    cb = c if 4 * 8 * c * 4 <= SC_TILE_BUDGET else _pick(c, (2048, 1024, 512, 256, 128))

    def body(w_hbm, g_hbm, m_hbm, v_hbm, go_hbm, d_hbm, nm_hbm, nv_hbm, wb, gb, mb, vb):
        tile = lax.axis_index("sc_tile") * 2 + lax.axis_index("sc_core")

        def update(group):
            for c0 in range(0, c, cb):
                at = (pl.ds(group * 8, 8), pl.ds(c0, cb))
                for hbm, buf in ((w_hbm, wb), (g_hbm, gb), (m_hbm, mb), (v_hbm, vb)):
                    pltpu.sync_copy(hbm.at[at], buf)
                pltpu.sync_copy(gb, go_hbm.at[at])

                @pl.loop(0, 8)
                def _(rr):
                    @pl.loop(0, cb, step=SC_LANES)
                    def _(i):
                        lanes = (rr, pl.ds(i, SC_LANES))
                        wb[lanes], mb[lanes], vb[lanes] = _adamw_update(wb[lanes], gb[lanes], mb[lanes], vb[lanes])

                for buf, hbm in ((wb, d_hbm), (mb, nm_hbm), (vb, nv_hbm)):
                    pltpu.sync_copy(buf, hbm.at[at])

        @pl.loop(0, per_tile)
        def _(k):
            group = k * SC_TILES + tile
            if groups % SC_TILES:
                pl.when(group < groups)(lambda: update(group))
            else:
                update(group)

    sd = jax.ShapeDtypeStruct((r, c), F32)
    return pl.kernel(body, name=name, out_type=[sd, sd, sd, sd],
                     mesh=plsc.VectorSubcoreMesh(core_axis_name="sc_core", subcore_axis_name="sc_tile"),
                     scratch_types=[pltpu.VMEM((8, cb), F32)] * 4)(w, g, m, v)


def _adamw(w, g, m, v, *, name, ride=None):
    r, c = w.shape
    tc = c if c <= 4096 else _pick(c, (2048, 1024, 512, 256, 128))
    tr = next(t for t in (512, 256, 128, 64, 32, 16, 8) if r % t == 0 and t * tc <= 256 * 1024)

    def body(w_ref, g_ref, m_ref, v_ref, go_ref, d_ref, nm_ref, nv_ref):
        go_ref[...] = g_ref[...]
        d_ref[...], nm_ref[...], nv_ref[...] = _adamw_update(w_ref[...], g_ref[...], m_ref[...], v_ref[...])

    blk = ((tr, tc), lambda i, j: (i, j))
    sd = jax.ShapeDtypeStruct((r, c), F32)
    return _ew(body, [w, g, m, v], [blk] * 4, [sd] * 4, [blk] * 4, (r // tr, c // tc), name=name, ride=ride)


BIG = (("w_in", 1), ("w_sb_out", 1), ("w_ca_out", 1), ("w_mix_out", 0), ("w_ffn_in", 1), ("w_ffn_out", 0),
       ("w_ple_in", 1), ("w_ple_gate", 0))
AXIS = dict(BIG)
HEAD_PARTS = 8
HEAD_HOSTS = ("w_ffn_in", "w_ffn_out")
TAIL_PARTS = 16
TAIL_SECOND = 5
TAIL_FIRST = 2
ON_SPARSECORE = tuple(n for n, _ in BIG if n != "w_in")
SMALL = ("rel_bias", "g_mix", "g_ffn", "g_ple", "g_final")
ORDER = ("w_in", "w_sb_out", "w_ca_out", "w_mix_out", "rel_bias", "g_mix", "g_ffn", "g_ple", "g_final",
         "w_ffn_in", "w_ffn_out", "w_ple_in", "w_ple_gate")


def _pack(parts):
    flat = jnp.concatenate([a.reshape(-1) for a in parts])
    rows = -(-flat.shape[0] // 1024) * 8
    return jnp.pad(flat, (0, rows * 128 - flat.shape[0])).reshape(rows, 128)


def _unpack(packed, like):
    flat, out, at = packed.reshape(-1), [], 0
    for a in like:
        out.append(flat[at:at + a.size].reshape(a.shape))
        at += a.size
    return out


def kernel(x, p, w_in, w_sb_out, w_ca_out, w_mix_out, rel_bias, g_mix, g_ffn, g_ple, g_final, w_ffn_in, w_ffn_out, w_ple_in, w_ple_gate, loss_target, m_w_in, m_w_sb_out, m_w_ca_out, m_w_mix_out, m_rel_bias, m_g_mix, m_g_ffn, m_g_ple, m_g_final, m_w_ffn_in, m_w_ffn_out, m_w_ple_in, m_w_ple_gate, v_w_in, v_w_sb_out, v_w_ca_out, v_w_mix_out, v_rel_bias, v_g_mix, v_g_ffn, v_g_ple, v_g_final, v_w_ffn_in, v_w_ffn_out, v_w_ple_in, v_w_ple_gate):
    weights = dict(w_in=w_in, w_sb_out=w_sb_out, w_ca_out=w_ca_out, w_mix_out=w_mix_out, rel_bias=rel_bias,
                   g_mix=g_mix, g_ffn=g_ffn, g_ple=g_ple, g_final=g_final, w_ffn_in=w_ffn_in,
                   w_ffn_out=w_ffn_out, w_ple_in=w_ple_in, w_ple_gate=w_ple_gate)
    m_in = dict(w_in=m_w_in, w_sb_out=m_w_sb_out, w_ca_out=m_w_ca_out, w_mix_out=m_w_mix_out, rel_bias=m_rel_bias,
                g_mix=m_g_mix, g_ffn=m_g_ffn, g_ple=m_g_ple, g_final=m_g_final, w_ffn_in=m_w_ffn_in,
                w_ffn_out=m_w_ffn_out, w_ple_in=m_w_ple_in, w_ple_gate=m_w_ple_gate)
    v_in = dict(w_in=v_w_in, w_sb_out=v_w_sb_out, w_ca_out=v_w_ca_out, w_mix_out=v_w_mix_out, rel_bias=v_rel_bias,
                g_mix=v_g_mix, g_ffn=v_g_ffn, g_ple=v_g_ple, g_final=v_g_final, w_ffn_in=v_w_ffn_in,
                w_ffn_out=v_w_ffn_out, w_ple_in=v_w_ple_in, w_ple_gate=v_w_ple_gate)

    pos = jnp.stack([2 * lax.axis_index("x") + lax.axis_index("y"), lax.axis_index("c")]).astype(jnp.int32)
    comm = _Comm(pos, {"w_in": _cast_place(w_in[0], AXIS["w_in"], pos, name="cast_w_in")})
    at = 0
    for n in HEAD_HOSTS:
        ride = comm.gather(("w_in",), "near", part=(at, HEAD_PARTS))
        comm.w[n] = _cast_place(weights[n][0], AXIS[n], pos, name="cast_" + n, ride=ride)
        at += 1
    for n, axis in BIG:
        if n not in comm.w:
            comm.w[n] = _cast_place(weights[n][0], axis, pos, name="cast_" + n)
    _run(comm.gather(("w_in",), "near", part=(at, HEAD_PARTS, HEAD_PARTS - at)), name="gather_w_in_near")
    _run(comm.gather(("w_in",), "far+"), name="gather_w_in_far")
    _run(comm.gather(("w_in",), "pair-"), name="gather_w_in_pair")
    small = dict(rel_bias=rel_bias[0], g_mix=g_mix, g_ffn=g_ffn, g_ple=g_ple, g_final=g_final.reshape(1, -1))
    loss, grad_x, gs = _step(x[0], p[0, 0], loss_target[0], small, comm)

    grads, delta, new_m, new_v = {}, {}, {}, {}
    for n in [n for n, _ in BIG if n != "w_in"] + ["w_in"]:
        if n == "w_in":
            _run(comm.tail_rest(), name="rs_chips_w_in")
            comm.sum(("w_in",))
            _run(comm.share(("w_in",)), name="rs_share_w_in")
        update = _adamw_sc if n in ON_SPARSECORE else _adamw
        g, d, nm, nv = update(weights[n][0], comm.result(n), m_in[n][0], v_in[n][0], name="adamw_" + n)
        grads[n], delta[n], new_m[n], new_v[n] = g[None], d[None], nm[None], nv[None]

    like = [weights[n] for n in SMALL]
    reduced = _small_all_reduce(_pack([gs[n] for n in SMALL] + [loss[:, :1]]), name="small_all_reduce")
    g_small = _unpack(reduced, like + [loss[:, :1]])
    total_loss = g_small[-1].reshape(())
    g_packed = _pack(g_small[:-1])
    _, d_s, m_s, v_s = _adamw(_pack(like), g_packed, _pack([m_in[n] for n in SMALL]), _pack([v_in[n] for n in SMALL]),
                           name="adamw_small")
    for n, g, d, nm, nv in zip(SMALL, g_small[:-1], _unpack(d_s, like), _unpack(m_s, like), _unpack(v_s, like)):
        grads[n], delta[n], new_m[n], new_v[n] = g, d, nm, nv

    return (total_loss, grad_x[None], *[grads[n] for n in ORDER], *[delta[n] for n in ORDER],
            *[new_m[n] for n in ORDER], *[new_v[n] for n in ORDER])
```

```python
import functools
import math

import jax
import jax.numpy as jnp
from jax import lax
from jax.experimental import pallas as pl
from jax.experimental.pallas import tpu as pltpu
from jax.experimental.pallas import tpu_sc as plsc

F32 = jnp.float32
BF16 = jnp.bfloat16

HEAD_DIM = 128
CHUNK = 64
LEFT_CHUNKS = 8
REL_CLIP = 128
N_REL = REL_CLIP + CHUNK
BAND = (LEFT_CHUNKS + 2) * CHUNK
CA_PER_STEP = 4
CA_ROWS = CA_PER_STEP * CHUNK
CA_BAND = BAND + CA_PER_STEP * CHUNK
CA_PAD = BAND
SB_BLOCK = 128
SB_KEYS = 512
SB_GROUPS = SB_KEYS // SB_BLOCK
SB_ROWS = SB_KEYS
EPS = 1e-6
NEG = -1e30

ADAM_LR = 0.001
ADAM_B1 = 0.9
ADAM_B2 = 0.999
ADAM_EPS = 1e-08
ADAM_WD = 0.01
ADAM_STEP = 10

VMEM_LIMIT = 48 * 1024 * 1024
MM_VMEM_BUDGET = 36 * 1024 * 1024
V7X_HBM_BYTES_PER_S = 3.7e12
GRID_STEP_S = 0.35e-6
MESH = pl.DeviceIdType.MESH
N_CHIPS = 4


def _pick(dim, prefs):
    for t in prefs:
        if dim % t == 0:
            return t
    raise ValueError(f"no tile for {dim}")


def _cparams(sem=None):
    return pltpu.CompilerParams(dimension_semantics=sem, vmem_limit_bytes=VMEM_LIMIT)


def _sigmoid(v):
    return 1.0 / (1.0 + jnp.exp(-v))


def _dot(a, b, dims):
    return lax.dot_general(a, b, (dims, ((), ())), preferred_element_type=F32)


def _dot_nn(a, b):
    return _dot(a, b, ((1,), (0,)))


def _dot_nt(a, b):
    return _dot(a, b, ((1,), (1,)))


def _dot_tn(a, b):
    return _dot(a, b, ((0,), (0,)))


HBM = pl.BlockSpec(memory_space=pltpu.HBM)


class _Ride:
    def __init__(self):
        self.items = []

    def add(self, ins, outs, aliases, n_sems, start, finish, sink):
        self.items.append((ins, outs, aliases, n_sems, start, finish, sink))


def _call(body, args, *, name, grid, in_specs, out_specs, out_shape, scratch_shapes=(), sem=None, ride=None,
          scalars=None, onto=()):
    items = ride.items if ride is not None else []
    if onto:
        args, in_specs = list(args) + list(onto), list(in_specs) + [HBM] * len(onto)
        inner, body = body, lambda *refs: inner(*refs[:len(args) - len(onto)], *refs[len(args):])
    n_in, n_out, n_scr = len(args), len(out_shape), len(scratch_shapes)
    r_ins = [a for it in items for a in it[0]]
    r_outs = [o for it in items for o in it[1]]
    updated = [id(it[0][i]) for it in items for i in it[2]]
    assert len(set(updated)) == len(updated), "one call may update a buffer in place only once"
    aliases, a, b = {n_in - len(onto) + t: t for t in range(len(onto))}, n_in, n_out
    for it in items:
        aliases.update({a + i: b + o for i, o in it[2].items()})
        a, b = a + len(it[0]), b + len(it[1])
    sems = [pltpu.SemaphoreType.DMA((it[3],)) for it in items for _ in range(2)]

    def wrapped(*refs):
        head, refs = (refs[:1], refs[1:]) if scalars is not None else ((), refs)
        ins, rin = refs[:n_in], refs[n_in:n_in + len(r_ins)]
        at = n_in + len(r_ins)
        outs, rout = refs[at:at + n_out], refs[at + n_out:at + n_out + len(r_outs)]
        at += n_out + len(r_outs)
        scr, rsem = refs[at:at + n_scr], refs[at + n_scr:]

        def each(which):
            a = b = 0
            for q, it in enumerate(items):
                it[which](rin[a:a + len(it[0])], rout[b:b + len(it[1])], rsem[2 * q], rsem[2 * q + 1])
                a, b = a + len(it[0]), b + len(it[1])

        if items:
            ids = [pl.program_id(d) for d in range(len(grid))]
            first = functools.reduce(jnp.logical_and, [i == 0 for i in ids])
            last = functools.reduce(jnp.logical_and, [i == g - 1 for i, g in zip(ids, grid)])
            pl.when(first)(lambda: each(4))
        body(*head, *ins, *outs, *scr)
        if items:
            pl.when(last)(lambda: each(5))

    specs = dict(grid=grid, in_specs=list(in_specs) + [HBM] * len(r_ins),
                 out_specs=list(out_specs) + [HBM] * len(r_outs), scratch_shapes=list(scratch_shapes) + sems)
    if scalars is not None:
        specs = dict(grid_spec=pltpu.PrefetchScalarGridSpec(num_scalar_prefetch=1, **specs))
        aliases = {i + 1: o for i, o in aliases.items()}
    res = pl.pallas_call(
        wrapped, name=name, **specs,
        out_shape=list(out_shape) + r_outs,
        input_output_aliases=aliases,
        compiler_params=_cparams(("arbitrary",) * len(grid) if items else sem),
    )(*(() if scalars is None else (scalars,)), *args, *r_ins)
    b = n_out
    for it in items:
        it[6](res[b:b + len(it[1])])
        b += len(it[1])
    return list(res[:n_out])


def _mm_tiles(m, n_align, n, k, a_bytes, b_bytes, out_bytes):
    best = None
    tks = sorted({t for t in (k, k // 2, k // 4, 2048, 1024, 512, 256, 128) if t <= k and k % t == 0 and t % 128 == 0})
    for tm in (t for t in (2048, 1024, 512, 256, 128) if m % t == 0):
        for tn in (t for t in (2048, 1024, 512, 256, 128) if n_align % t == 0):
            for tk in tks:
                nk = k // tk
                vmem = 2 * (tm * tk * a_bytes + tk * tn * b_bytes + tm * tn * out_bytes) + tm * tn * 4
                if vmem > MM_VMEM_BUDGET:
                    continue
                traffic = m * k * a_bytes * (n // tn if nk > 1 else 1) + k * n * b_bytes * (m // tm)
                traffic += tm * tk * a_bytes + tk * tn * b_bytes + tm * tn * out_bytes
                traffic += m * n * 4 * nk if nk > 1 else 0
                cost = traffic / V7X_HBM_BYTES_PER_S + (m // tm) * (n // tn) * nk * GRID_STEP_S
                if best is None or cost < best[0]:
                    best = (cost, tm, tn, tk)
    return best[1:]


def _mm(a, b, mode, out_dtypes, *, name, n=None, b_col_off=0, resid=None, ride=None, rows=None, onto=(), m_half=None):
    if mode == "nn":
        m, k = a.shape
        n = b.shape[1] if n is None else n
    elif mode == "nt":
        m, k = a.shape
        n = b.shape[0]
    else:
        k, m = a.shape
        n = b.shape[1]
    if m_half is not None:
        m //= 2
    m_all, (row0, m) = m, (0, m) if rows is None else rows
    n_out = len(out_dtypes)
    has_resid = resid is not None
    out_bytes = sum(jnp.dtype(dt).itemsize for dt in out_dtypes) + (4 if has_resid else 0)
    tm, tn, tk = _mm_tiles(math.gcd(m, row0) if row0 else m, math.gcd(n, b_col_off) if b_col_off else n, n, k,
                           a.dtype.itemsize, b.dtype.itemsize, out_bytes)
    nk = k // tk
    boff, roff = b_col_off // tn, row0 // tm
    dot = {"nn": _dot_nn, "nt": _dot_nt, "tn": _dot_tn}[mode]
    if m_half is None:
        half = lambda: 0
    else:
        half = lambda pos_ref: (pos_ref[1] if m_half[0] else 1 - pos_ref[1]) * (m // tm)

    def body(*refs):
        refs = refs[m_half is not None:]
        a_ref, b_ref = refs[0], refs[1]
        r_ref = refs[2] if has_resid else None
        o_refs = refs[2 + has_resid: 2 + has_resid + n_out]

        def finish(r):
            if has_resid:
                r = r + r_ref[...]
            for o_ref in o_refs:
                o_ref[...] = r.astype(o_ref.dtype)

        part = dot(a_ref[...].astype(BF16), b_ref[...].astype(BF16))
        if nk == 1:
            finish(part)
            return
        acc_ref = refs[-1]
        kk = pl.program_id(2)

        @pl.when(kk == 0)
        def _():
            acc_ref[...] = part

        @pl.when(kk > 0)
        def _():
            acc_ref[...] += part

        @pl.when(kk == nk - 1)
        def _():
            finish(acc_ref[...])

    if mode == "nn":
        a_spec = pl.BlockSpec((tm, tk), lambda i, j, kk, *_: (i + roff, kk))
        b_spec = pl.BlockSpec((tk, tn), lambda i, j, kk, *_: (kk, j + boff))
    elif mode == "nt":
        a_spec = pl.BlockSpec((tm, tk), lambda i, j, kk, *_: (i + roff, kk))
        b_spec = pl.BlockSpec((tn, tk), lambda i, j, kk, *_: (j, kk))
    else:
        a_spec = pl.BlockSpec((tk, tm), lambda i, j, kk, *pos: (kk, i + half(*pos)))
        b_spec = pl.BlockSpec((tk, tn), lambda i, j, kk, *_: (kk, j))
    o_spec = pl.BlockSpec((tm, tn), lambda i, j, kk, *_: (i + roff, j))
    in_specs = [a_spec, b_spec] + ([o_spec] if has_resid else [])
    args = [a, b] + ([resid] if has_resid else [])
    outs = _call(
        body, args, name=name,
        grid=(m // tm, n // tn, nk),
        in_specs=in_specs,
        out_specs=[o_spec] * n_out,
        out_shape=[jax.ShapeDtypeStruct((m_all, n), dt) for dt in out_dtypes],
        scratch_shapes=[pltpu.VMEM((tm, tn), F32)] if nk > 1 else [],
        sem=("parallel", "parallel", "arbitrary"), ride=ride, onto=onto,
        scalars=None if m_half is None else m_half[1])
    return outs[0] if n_out == 1 else tuple(outs)


def _row_tile(s):
    return _pick(s, (256, 128))


def _rms_fwd(x, g, *, name, ride=None):
    s, d = x.shape
    tr = _row_tile(s)

    def body(x_ref, g_ref, o_ref):
        xv = x_ref[...]
        r = lax.rsqrt(jnp.mean(xv * xv, axis=1, keepdims=True) + EPS)
        o_ref[...] = (xv * r * g_ref[...]).astype(o_ref.dtype)

    return _call(
        body, [x, g], name=name, grid=(s // tr,),
        in_specs=[pl.BlockSpec((tr, d), lambda i: (i, 0)), pl.BlockSpec((1, d), lambda i: (0, 0))],
        out_specs=[pl.BlockSpec((tr, d), lambda i: (i, 0))],
        out_shape=[jax.ShapeDtypeStruct((s, d), BF16)], sem=("parallel",), ride=ride)[0]


def _rms_bwd(x, g, dh, dres, *, name, ride=None):
    s, d = x.shape
    tr = _row_tile(s)

    def body(x_ref, g_ref, dh_ref, dres_ref, dx_ref, dx16_ref, dg_ref):
        i = pl.program_id(0)
        xv = x_ref[...]
        r = lax.rsqrt(jnp.mean(xv * xv, axis=1, keepdims=True) + EPS)
        xhat = xv * r
        dhv = dh_ref[...]
        dxhat = dhv * g_ref[...]
        proj = jnp.mean(dxhat * xhat, axis=1, keepdims=True)
        dx = dres_ref[...] + r * (dxhat - xhat * proj)
        dx_ref[...] = dx
        dx16_ref[...] = dx.astype(dx16_ref.dtype)

        @pl.when(i == 0)
        def _():
            dg_ref[...] = jnp.zeros_like(dg_ref)

        dg_ref[...] += jnp.sum(dhv * xhat, axis=0, keepdims=True)

    row = pl.BlockSpec((tr, d), lambda i: (i, 0))
    vec = pl.BlockSpec((1, d), lambda i: (0, 0))
    return _call(
        body, [x, g, dh, dres], name=name, grid=(s // tr,),
        in_specs=[row, vec, row, row],
        out_specs=[row, row, vec],
        out_shape=[jax.ShapeDtypeStruct((s, d), F32), jax.ShapeDtypeStruct((s, d), BF16),
                   jax.ShapeDtypeStruct((1, d), F32)],
        sem=("arbitrary",), ride=ride)


def _final_loss(x, g, target, *, name):
    s, d = x.shape
    tr = _row_tile(s)

    def body(x_ref, g_ref, t_ref, dx_ref, dg_ref, loss_ref):
        i = pl.program_id(0)
        xv = x_ref[...]
        gv = g_ref[...]
        r = lax.rsqrt(jnp.mean(xv * xv, axis=1, keepdims=True) + EPS)
        xhat = xv * r
        err = xhat * gv - t_ref[...]
        dy = err * (1.0 / d)
        dxhat = dy * gv
        proj = jnp.mean(dxhat * xhat, axis=1, keepdims=True)
        dx_ref[...] = r * (dxhat - xhat * proj)

        @pl.when(i == 0)
        def _():
            dg_ref[...] = jnp.zeros_like(dg_ref)
            loss_ref[...] = jnp.zeros_like(loss_ref)

        dg_ref[...] += jnp.sum(dy * xhat, axis=0, keepdims=True)
        part = 0.5 * jnp.sum(jnp.mean(err * err, axis=1, keepdims=True), axis=0, keepdims=True)
        loss_ref[...] += jnp.broadcast_to(part, loss_ref.shape)

    row = pl.BlockSpec((tr, d), lambda i: (i, 0))
    vec = pl.BlockSpec((1, d), lambda i: (0, 0))
    return pl.pallas_call(
        body, name=name, grid=(s // tr,),
        in_specs=[row, vec, row],
        out_specs=[row, vec, pl.BlockSpec((1, 128), lambda i: (0, 0))],
        out_shape=[jax.ShapeDtypeStruct((s, d), F32), jax.ShapeDtypeStruct((1, d), F32),
                   jax.ShapeDtypeStruct((1, 128), F32)],
        compiler_params=_cparams(("arbitrary",)),
    )(x, g, target)


def _ew(body, ins, in_blocks, outs, out_blocks, grid, *, name, ride=None):
    return _call(body, ins, name=name, grid=grid,
                 in_specs=[pl.BlockSpec(bs, im) for bs, im in in_blocks],
                 out_specs=[pl.BlockSpec(bs, im) for bs, im in out_blocks],
                 out_shape=outs, sem=("parallel",) * len(grid), ride=ride)


def _gate_merge_fwd(gates, o_sb, o_ca, *, name, ride=None):
    s, d = o_sb.shape
    tr, tc = _row_tile(s), _pick(d, (1024, 512, 256, 128))
    nc = d // tc

    def body(gs_ref, gc_ref, os_ref, oc_ref, m_ref):
        f32 = lambda ref: ref[...].astype(F32)
        m = _sigmoid(f32(gs_ref)) * f32(os_ref) + _sigmoid(f32(gc_ref)) * f32(oc_ref)
        m_ref[...] = m.astype(m_ref.dtype)

    blk = ((tr, tc), lambda i, j: (i, j))
    return _ew(body, [gates, gates, o_sb, o_ca],
               [blk, ((tr, tc), lambda i, j: (i, j + nc)), blk, blk],
               [jax.ShapeDtypeStruct((s, d), BF16)], [blk], (s // tr, nc), name=name, ride=ride)[0]


def _gate_merge_bwd(dmerged, gates, o_sb, o_ca, *, name):
    s, d = o_sb.shape
    tr, tc = _row_tile(s), _pick(d, (1024, 512, 256, 128))
    nc = d // tc

    def body(dm_ref, gs_ref, gc_ref, os_ref, oc_ref, dgs_ref, dgc_ref, dos_ref, doc_ref):
        f32 = lambda ref: ref[...].astype(F32)
        dm = dm_ref[...]
        ss = _sigmoid(f32(gs_ref))
        sc = _sigmoid(f32(gc_ref))
        dgs_ref[...] = (dm * f32(os_ref) * ss * (1.0 - ss)).astype(dgs_ref.dtype)
        dgc_ref[...] = (dm * f32(oc_ref) * sc * (1.0 - sc)).astype(dgc_ref.dtype)
        dos_ref[...] = (dm * ss).astype(dos_ref.dtype)
        doc_ref[...] = (dm * sc).astype(doc_ref.dtype)

    blk = ((tr, tc), lambda i, j: (i, j))
    sd = jax.ShapeDtypeStruct((s, d), BF16)
    return _ew(body, [dmerged, gates, gates, o_sb, o_ca],
               [blk, blk, ((tr, tc), lambda i, j: (i, j + nc)), blk, blk],
               [sd, sd, sd, sd], [blk, blk, blk, blk], (s // tr, nc), name=name)


def _swiglu_fwd(gu, *, name, ride=None):
    s, f2 = gu.shape
    f = f2 // 2
    tr, tc = 128, _pick(f, (512, 256, 128))

    def body(gu_ref, a_ref):
        for at in range(0, f, tc):
            gv = gu_ref[:, at:at + tc].astype(F32)
            a_ref[:, at:at + tc] = (gv * _sigmoid(gv) * gu_ref[:, f + at:f + at + tc].astype(F32)).astype(a_ref.dtype)

    row = lambda i: (i, 0)
    return _ew(body, [gu], [((tr, f2), row)], [jax.ShapeDtypeStruct((s, f), BF16)], [((tr, f), row)],
               (s // tr,), name=name, ride=ride)[0]


def _swiglu_bwd(dact, gu, *, name):
    s, f2 = gu.shape
    f = f2 // 2
    tr, tc = 128, _pick(f, (512, 256, 128))

    def body(da_ref, gu_ref, o_ref):
        for at in range(0, f, tc):
            da = da_ref[:, at:at + tc].astype(F32)
            gv = gu_ref[:, at:at + tc].astype(F32)
            sg = _sigmoid(gv)
            uv = gu_ref[:, f + at:f + at + tc].astype(F32)
            o_ref[:, at:at + tc] = (da * uv * sg * (1.0 + gv * (1.0 - sg))).astype(o_ref.dtype)
            o_ref[:, f + at:f + at + tc] = (da * gv * sg).astype(o_ref.dtype)

    row = lambda i: (i, 0)
    return _ew(body, [dact, gu], [((tr, f), row), ((tr, f2), row)], [jax.ShapeDtypeStruct((s, f2), BF16)],
               [((tr, f2), row)], (s // tr,), name=name)[0]


def _concat_cols(parts, *, name):
    s = parts[0].shape[0]
    widths = [p.shape[1] for p in parts]
    tr = 256

    def body(*refs):
        o_ref, at = refs[-1], 0
        for p_ref, width in zip(refs, widths):
            o_ref[:, at:at + width] = p_ref[...]
            at += width

    row = lambda i: (i, 0)
    return _ew(body, list(parts), [((tr, width), row) for width in widths],
               [jax.ShapeDtypeStruct((s, sum(widths)), parts[0].dtype)], [((tr, sum(widths)), row)],
               (s // tr,), name=name)[0]


def _ple_fwd(x, t, pe, *, name):
    s, d = x.shape
    tr, tc = _row_tile(s), _pick(d, (1024, 512, 256, 128))

    def body(x_ref, t_ref, p_ref, o_ref):
        o_ref[...] = x_ref[...] + _sigmoid(t_ref[...].astype(F32)) * p_ref[...].astype(F32)

    blk = ((tr, tc), lambda i, j: (i, j))
    return _ew(body, [x, t, pe], [blk, blk, blk],
               [jax.ShapeDtypeStruct((s, d), F32)], [blk], (s // tr, d // tc), name=name)[0]


def _ple_bwd(dx, t, pe, *, name):
    s, d = dx.shape
    tr, tc = _row_tile(s), _pick(d, (1024, 512, 256, 128))

    def body(dx_ref, t_ref, p_ref, dt_ref, dp_ref):
        dxv = dx_ref[...]
        sg = _sigmoid(t_ref[...].astype(F32))
        dt_ref[...] = (dxv * p_ref[...].astype(F32) * sg * (1.0 - sg)).astype(dt_ref.dtype)
        dp_ref[...] = (dxv * sg).astype(dp_ref.dtype)

    blk = ((tr, tc), lambda i, j: (i, j))
    sd = jax.ShapeDtypeStruct((s, d), BF16)
    return _ew(body, [dx, t, pe], [blk, blk, blk], [sd, sd], [blk, blk], (s // tr, d // tc), name=name)


def _sb_tri(later):
    row = lax.broadcasted_iota(jnp.int32, (SB_BLOCK, SB_BLOCK), 0)
    col = lax.broadcasted_iota(jnp.int32, (SB_BLOCK, SB_BLOCK), 1)
    tri = (row > col) if later else (row < col)
    return jnp.concatenate([tri.astype(BF16), jnp.ones((SB_BLOCK, SB_BLOCK), BF16)], axis=1)


def _sb_valid(i, j, own):
    if not own:
        return None
    qi = i * SB_ROWS + lax.broadcasted_iota(jnp.int32, (SB_ROWS, SB_KEYS), 0)
    ki = j * SB_KEYS + lax.broadcasted_iota(jnp.int32, (SB_ROWS, SB_KEYS), 1)
    return ki < qi


def _sb_scan(v, tri, run, later):
    hi = v.astype(BF16)
    lo = (v - hi.astype(F32)).astype(BF16)
    outs = [None] * SB_GROUPS
    for b in (reversed(range(SB_GROUPS)) if later else range(SB_GROUPS)):
        cols = slice(b * SB_BLOCK, (b + 1) * SB_BLOCK)
        r = _dot_nn(hi[:, cols], tri) + _dot_nn(lo[:, cols], tri)
        outs[b] = r[:, :SB_BLOCK] + run
        run = run + r[:, SB_BLOCK:]
    return jnp.concatenate(outs, axis=1), run


def _masked(valid, v):
    return v if valid is None else jnp.where(valid, v, 0.0)


def _sb_scores(q, kj, scale, valid):
    z = _dot_nt(q, kj) * scale
    t = jnp.log(1.0 + jnp.exp(-jnp.abs(z)))
    return jnp.minimum(z, 0.0) - t, _masked(valid, -jnp.maximum(z, 0.0) - t)


def _sb_specs(h_count, s, col0):
    q_spec = pl.BlockSpec((SB_ROWS, HEAD_DIM), lambda h, i: (i, col0 + h))
    k_spec = pl.BlockSpec((s, HEAD_DIM), lambda h, i: (0, col0 + h_count + h))
    v_spec = pl.BlockSpec((s, HEAD_DIM), lambda h, i: (0, col0 + 2 * h_count + h))
    return q_spec, k_spec, v_spec


def _sb_fwd(qkv, n_heads, col0, *, name, ride=None):
    s = qkv.shape[0]
    nq = s // SB_ROWS
    scale = HEAD_DIM ** -0.5

    def body(q_ref, k_ref, v_ref, o_ref):
        i = pl.program_id(1)
        q = q_ref[...]
        tri = _sb_tri(later=True)

        def step(j, carry, own):
            run, acc = carry
            off = pl.multiple_of(j * SB_KEYS, SB_KEYS)
            valid = _sb_valid(i, j, own)
            ls, lk = _sb_scores(q, k_ref[pl.ds(off, SB_KEYS), :], scale, valid)
            between, run = _sb_scan(lk, tri, run, later=True)
            a = _masked(valid, jnp.exp(ls + between))
            return run, acc + _dot_nn(a.astype(BF16), v_ref[pl.ds(off, SB_KEYS), :])

        carry = step(i, (jnp.zeros((SB_ROWS, SB_BLOCK), F32), jnp.zeros((SB_ROWS, HEAD_DIM), F32)), True)
        _, acc = lax.fori_loop(0, i, lambda jj, c: step(i - 1 - jj, c, False), carry)
        o_ref[...] = acc.astype(o_ref.dtype)

    q_spec, k_spec, v_spec = _sb_specs(n_heads, s, col0)
    return _call(
        body, [qkv, qkv, qkv], name=name, grid=(n_heads, nq),
        in_specs=[q_spec, k_spec, v_spec],
        out_specs=[pl.BlockSpec((SB_ROWS, HEAD_DIM), lambda h, i: (i, h))],
        out_shape=[jax.ShapeDtypeStruct((s, n_heads * HEAD_DIM), BF16)],
        sem=("parallel", "arbitrary"), ride=ride)[0]


def _sb_bwd(qkv, dy, n_heads, col0, *, name, ride=None):
    s = qkv.shape[0]
    nq = s // SB_ROWS
    scale = HEAD_DIM ** -0.5

    def body(q_ref, k_ref, v_ref, dy_ref, dq_ref, dk_ref, dv_ref, e_scr, sg_scr, dk_acc, dv_acc):
        i = pl.program_id(1)
        q = q_ref[...]
        dyv = dy_ref[...]

        @pl.when(i == 0)
        def _():
            dk_acc[...] = jnp.zeros_like(dk_acc)
            dv_acc[...] = jnp.zeros_like(dv_acc)

        tri_later = _sb_tri(later=True)

        def pass1(j, run, own):
            off = pl.multiple_of(j * SB_KEYS, SB_KEYS)
            valid = _sb_valid(i, j, own)
            ls, lk = _sb_scores(q, k_ref[pl.ds(off, SB_KEYS), :], scale, valid)
            between, run = _sb_scan(lk, tri_later, run, later=True)
            a = _masked(valid, jnp.exp(ls + between))
            e_scr[j] = a * _dot_nt(dyv, v_ref[pl.ds(off, SB_KEYS), :])
            sg_scr[j] = jnp.exp(ls)
            dv_acc[pl.ds(off, SB_KEYS), :] += _dot_tn(a.astype(BF16), dyv)
            return run

        lax.fori_loop(0, i, lambda jj, run: pass1(i - 1 - jj, run, False),
                      pass1(i, jnp.zeros((SB_ROWS, SB_BLOCK), F32), True))

        tri_earlier = _sb_tri(later=False)

        def pass2(j, carry, own):
            run, dq = carry
            off = pl.multiple_of(j * SB_KEYS, SB_KEYS)
            kj = k_ref[pl.ds(off, SB_KEYS), :]
            sg = sg_scr[j]
            e = e_scr[j]
            before, run = _sb_scan(e, tri_earlier, run, later=False)
            dz = _masked(_sb_valid(i, j, own), e * (1.0 - sg) - sg * before) * scale
            dzb = dz.astype(BF16)
            dk_acc[pl.ds(off, SB_KEYS), :] += _dot_tn(dzb, q)
            return run, dq + _dot_nn(dzb, kj)

        init = (jnp.zeros((SB_ROWS, SB_BLOCK), F32), jnp.zeros((SB_ROWS, HEAD_DIM), F32))
        _, dq = pass2(i, lax.fori_loop(0, i, lambda j, c: pass2(j, c, False), init), True)
        dq_ref[...] = dq.astype(dq_ref.dtype)

        @pl.when(i == nq - 1)
        def _():
            dk_ref[...] = dk_acc[...].astype(dk_ref.dtype)
            dv_ref[...] = dv_acc[...].astype(dv_ref.dtype)

    q_spec, k_spec, v_spec = _sb_specs(n_heads, s, col0)
    blk = pl.BlockSpec((SB_ROWS, HEAD_DIM), lambda h, i: (i, h))
    full = pl.BlockSpec((s, HEAD_DIM), lambda h, i: (0, h))
    sd = jax.ShapeDtypeStruct((s, n_heads * HEAD_DIM), BF16)
    return _call(
        body, [qkv, qkv, qkv, dy], name=name, grid=(n_heads, nq),
        in_specs=[q_spec, k_spec, v_spec, blk],
        out_specs=[blk, full, full],
        out_shape=[sd, sd, sd],
        scratch_shapes=[pltpu.VMEM((s // SB_KEYS, SB_ROWS, SB_KEYS), F32), pltpu.VMEM((s // SB_KEYS, SB_ROWS, SB_KEYS), F32),
                        pltpu.VMEM((s, HEAD_DIM), F32), pltpu.VMEM((s, HEAD_DIM), F32)],
        sem=("parallel", "arbitrary"), ride=ride)


def _band_bias(rel_bias):
    h = rel_bias.shape[0]
    width = BAND + CHUNK
    first = width - 1 - N_REL
    line = jnp.concatenate([jnp.broadcast_to(rel_bias[:, :1], (h, first)), rel_bias], axis=1)
    tiled = jnp.broadcast_to(line[:, None, :], (h, CHUNK, width - 1)).reshape(h, CHUNK * (width - 1))
    skew = jnp.pad(tiled, ((0, 0), (0, CHUNK))).reshape(h, CHUNK, width)[:, ::-1, :BAND]
    seen = jnp.arange(BAND) >= CHUNK
    return jnp.where(seen[None, None, :], skew, NEG)


def _band_bias_grad(dbias):
    h = dbias.shape[0]
    width = BAND + CHUNK
    flipped = jnp.pad(dbias[:, ::-1, :], ((0, 0), (0, 0), (0, CHUNK)))
    skew = flipped.reshape(h, CHUNK * width)[:, :CHUNK * (width - 1)].reshape(h, CHUNK, width - 1)
    diag = jnp.sum(skew, axis=1)
    first = width - 1 - N_REL
    clipped = jnp.sum(diag[:, :first + 1], axis=1, keepdims=True)
    return jnp.concatenate([clipped, diag[:, first + 1:]], axis=1)


def _group_bias(band):
    return jnp.concatenate([jnp.pad(band, ((0, 0), (0, 0), ((u + 1) * CHUNK, (CA_PER_STEP - 1 - u) * CHUNK)),
                                    constant_values=NEG) for u in range(CA_PER_STEP)], axis=1)


def _group_bias_grad(dgroup):
    return sum(dgroup[:, u * CHUNK:(u + 1) * CHUNK, (u + 1) * CHUNK:(u + 1) * CHUNK + BAND] for u in range(CA_PER_STEP))


def _ca_load_padded(k_ref, v_ref, kp, vp, s):
    kp[pl.ds(0, CA_PAD), :] = jnp.zeros((CA_PAD, HEAD_DIM), kp.dtype)
    vp[pl.ds(0, CA_PAD), :] = jnp.zeros((CA_PAD, HEAD_DIM), vp.dtype)
    kp[pl.ds(CA_PAD, s), :] = k_ref[...]
    vp[pl.ds(CA_PAD, s), :] = v_ref[...]


def _ca_weights(q, kb, bias, off, scale):
    z = _dot_nt(q, kb) * scale + bias
    pos = off + lax.broadcasted_iota(jnp.int32, (CA_ROWS, CA_BAND), 1)
    z = jnp.where(pos >= CA_PAD, z, NEG)
    p = jnp.exp(z - jnp.max(z, axis=1, keepdims=True))
    return p / jnp.sum(p, axis=1, keepdims=True)


def _ca_specs(h_count, s, col0):
    q_spec = pl.BlockSpec((CA_ROWS, HEAD_DIM), lambda h, c: (c, col0 + h))
    k_spec = pl.BlockSpec((s, HEAD_DIM), lambda h, c: (0, col0 + h_count + h))
    v_spec = pl.BlockSpec((s, HEAD_DIM), lambda h, c: (0, col0 + 2 * h_count + h))
    b_spec = pl.BlockSpec((1, CA_ROWS, CA_BAND), lambda h, c: (h, 0, 0))
    return q_spec, k_spec, v_spec, b_spec


def _ca_fwd(qkv, bias, n_heads, col0, *, name, ride=None):
    s = qkv.shape[0]
    nc = s // CA_ROWS
    scale = HEAD_DIM ** -0.5

    def body(q_ref, k_ref, v_ref, b_ref, o_ref, kp, vp):
        c = pl.program_id(1)

        @pl.when(c == 0)
        def _():
            _ca_load_padded(k_ref, v_ref, kp, vp, s)

        off = pl.multiple_of(c * CA_ROWS, CA_ROWS)
        w = _ca_weights(q_ref[...], kp[pl.ds(off, CA_BAND), :], b_ref[0], off, scale)
        o_ref[...] = _dot_nn(w.astype(BF16), vp[pl.ds(off, CA_BAND), :]).astype(o_ref.dtype)

    q_spec, k_spec, v_spec, b_spec = _ca_specs(n_heads, s, col0)
    return _call(
        body, [qkv, qkv, qkv, bias], name=name, grid=(n_heads, nc),
        in_specs=[q_spec, k_spec, v_spec, b_spec],
        out_specs=[pl.BlockSpec((CA_ROWS, HEAD_DIM), lambda h, c: (c, h))],
        out_shape=[jax.ShapeDtypeStruct((s, n_heads * HEAD_DIM), BF16)],
        scratch_shapes=[pltpu.VMEM((s + CA_PAD, HEAD_DIM), BF16), pltpu.VMEM((s + CA_PAD, HEAD_DIM), BF16)],
        sem=("parallel", "arbitrary"), ride=ride)[0]


def _ca_bwd(qkv, bias, dy, n_heads, col0, *, name, ride=None):
    s = qkv.shape[0]
    nc = s // CA_ROWS
    scale = HEAD_DIM ** -0.5

    def body(q_ref, k_ref, v_ref, b_ref, dy_ref, dq_ref, dk_ref, dv_ref, db_ref, kp, vp, dkp, dvp):
        c = pl.program_id(1)

        @pl.when(c == 0)
        def _():
            _ca_load_padded(k_ref, v_ref, kp, vp, s)
            dkp[...] = jnp.zeros_like(dkp)
            dvp[...] = jnp.zeros_like(dvp)
            db_ref[...] = jnp.zeros_like(db_ref)

        off = pl.multiple_of(c * CA_ROWS, CA_ROWS)
        band = pl.ds(off, CA_BAND)
        q = q_ref[...]
        dyv = dy_ref[...]
        kb = kp[band, :]
        w = _ca_weights(q, kb, b_ref[0], off, scale)
        dw = _dot_nt(dyv, vp[band, :])
        dvp[band, :] += _dot_tn(w.astype(BF16), dyv)
        dz = w * (dw - jnp.sum(w * dw, axis=1, keepdims=True))
        db_ref[0] += dz
        dzs = (dz * scale).astype(BF16)
        dq_ref[...] = _dot_nn(dzs, kb).astype(dq_ref.dtype)
        dkp[band, :] += _dot_tn(dzs, q)

        @pl.when(c == nc - 1)
        def _():
            dk_ref[...] = dkp[pl.ds(CA_PAD, s), :].astype(dk_ref.dtype)
            dv_ref[...] = dvp[pl.ds(CA_PAD, s), :].astype(dv_ref.dtype)

    q_spec, k_spec, v_spec, b_spec = _ca_specs(n_heads, s, col0)
    blk = pl.BlockSpec((CA_ROWS, HEAD_DIM), lambda h, c: (c, h))
    full = pl.BlockSpec((s, HEAD_DIM), lambda h, c: (0, h))
    sd = jax.ShapeDtypeStruct((s, n_heads * HEAD_DIM), BF16)
    return _call(
        body, [qkv, qkv, qkv, bias, dy], name=name, grid=(n_heads, nc),
        in_specs=[q_spec, k_spec, v_spec, b_spec, blk],
        out_specs=[blk, full, full, b_spec],
        out_shape=[sd, sd, sd, jax.ShapeDtypeStruct((n_heads, CA_ROWS, CA_BAND), F32)],
        scratch_shapes=[pltpu.VMEM((s + CA_PAD, HEAD_DIM), BF16), pltpu.VMEM((s + CA_PAD, HEAD_DIM), BF16),
                        pltpu.VMEM((s + CA_PAD, HEAD_DIM), F32), pltpu.VMEM((s + CA_PAD, HEAD_DIM), F32)],
        sem=("parallel", "arbitrary"), ride=ride)


EARLY = ("w_sb_out", "w_ca_out", "w_mix_out")


def _step(x, p, target, small, comm):
    w = comm.w
    d = x.shape[1]
    n_sb = w["w_sb_out"].shape[0] // HEAD_DIM
    n_ca = w["w_ca_out"].shape[0] // HEAD_DIM
    qkv_cols = 3 * HEAD_DIM * (n_sb + n_ca)
    ca_col0 = 3 * n_sb
    both = (F32, BF16)

    h1 = _rms_fwd(x, small["g_mix"], name="rms_mix")
    ffn, ple = ("w_ffn_in",), ("w_ple_gate", "w_ple_in")
    qkv = _mm(h1, w["w_in"], "nn", (BF16,), name="proj_qkv", n=qkv_cols, ride=comm.gather(EARLY, "near"))
    gates = _mm(h1, w["w_in"], "nn", (BF16,), name="proj_gates", n=2 * d, b_col_off=qkv_cols,
                ride=comm.gather(ffn, "near", comm.gather(EARLY, "far"), (0, 8)))
    bias = _group_bias(_band_bias(small["rel_bias"]))
    y_sb = _sb_fwd(qkv, n_sb, 0, name="sb_fwd", ride=comm.gather(ffn, "near", comm.gather(EARLY, "pair"), (1, 8, 7)))
    y_ca = _ca_fwd(qkv, bias, n_ca, ca_col0, name="ca_fwd", ride=comm.gather(ffn, "far"))
    out = ("w_ffn_out",)
    o_sb = _mm(y_sb, w["w_sb_out"], "nn", (BF16,), name="sb_out", ride=comm.gather(out, "near", part=(0, 4)))
    o_ca = _mm(y_ca, w["w_ca_out"], "nn", (BF16,), name="ca_out", ride=comm.gather(out, "near", part=(1, 4)))
    merged = _gate_merge_fwd(gates, o_sb, o_ca, name="gate_merge", ride=comm.gather(out, "near", part=(2, 4)))
    x1 = _mm(merged, w["w_mix_out"], "nn", (F32,), name="mix_out", resid=x,
             ride=comm.gather(out, "near", comm.gather(ffn, "pair"), (3, 4)))
    h2 = _rms_fwd(x1, small["g_ffn"], name="rms_ffn")
    gu = _mm(h2, w["w_ffn_in"], "nn", (BF16,), name="ffn_in", ride=comm.gather(ple, "near", comm.gather(out, "far")))
    act = _swiglu_fwd(gu, name="swiglu", ride=comm.gather(ple, "far", comm.gather(out, "pair")))
    x2 = _mm(act, w["w_ffn_out"], "nn", (F32,), name="ffn_out", resid=x1, ride=comm.gather(ple, "pair"))
    h3 = _rms_fwd(x2, small["g_ple"], name="rms_ple")
    t = _mm(h3, w["w_ple_gate"], "nn", (BF16,), name="ple_gate")
    pe = _mm(p, w["w_ple_in"], "nn", (BF16,), name="ple_in")
    x3 = _ple_fwd(x2, t, pe, name="ple_add")

    def halves(n, acts, dout, ride, name):
        if comm.pos is None:
            return comm.grad(n, *_mm(acts, dout, "tn", both, name=name))
        g16 = _mm(acts, dout, "tn", (BF16,), name=name + "_other", m_half=(False, comm.pos), ride=ride)
        comm.grad(n, None, g16, half=True)
        g32 = _mm(acts, dout, "tn", (F32,), name=name + "_own", m_half=(True, comm.pos), ride=comm.pair((n,)))
        comm.grad(n, g32, g16, half=True)

    gs = {}
    dx3, gs["g_final"], loss = _final_loss(x3, small["g_final"], target, name="final_loss")
    dt, dpe = _ple_bwd(dx3, t, pe, name="ple_bwd")
    comm.grad("w_ple_in", *_mm(p, dpe, "tn", both, name="dw_ple_in"))
    comm.grad("w_ple_gate", *_mm(h3, dt, "tn", both, name="dw_ple_gate"))
    ple = ("w_ple_in", "w_ple_gate")
    dh3 = _mm(dt, w["w_ple_gate"], "nt", (F32,), name="dh_ple", ride=comm.pair(ple))
    dx2, dx2_16, gs["g_ple"] = _rms_bwd(x2, small["g_ple"], dh3, dx3, name="rms_ple_bwd")
    comm.add(ple)
    comm.grad("w_ffn_out", *_mm(act, dx2_16, "tn", both, name="dw_ffn_out", ride=comm.chips(ple)))
    dact = _mm(dx2_16, w["w_ffn_out"], "nt", (BF16,), name="dact", ride=comm.pair(("w_ffn_out",)))
    dgu = _swiglu_bwd(dact, gu, name="swiglu_bwd")
    comm.sum(ple)
    comm.add(("w_ffn_out",))
    comm.grad("w_ffn_in", *_mm(h2, dgu, "tn", both, name="dw_ffn_in",
                               ride=comm.share(ple, comm.chips(("w_ffn_out",)))))
    dh2 = _mm(dgu, w["w_ffn_in"], "nt", (F32,), name="dh_ffn", ride=comm.pair(("w_ffn_in",)))
    dx1, dx1_16, gs["g_ffn"] = _rms_bwd(x1, small["g_ffn"], dh2, dx2, name="rms_ffn_bwd")
    comm.add(("w_ffn_in",))
    comm.sum(("w_ffn_out",))
    comm.grad("w_mix_out", *_mm(merged, dx1_16, "tn", both, name="dw_mix_out", ride=comm.share(("w_ffn_out",))))
    dmerged = _mm(dx1_16, w["w_mix_out"], "nt", (F32,), name="dmerged", ride=comm.pair(("w_mix_out",)))
    dg_sb, dg_ca, do_sb, do_ca = _gate_merge_bwd(dmerged, gates, o_sb, o_ca, name="gate_merge_bwd")
    comm.add(("w_mix_out",))
    comm.grad("w_sb_out", *_mm(y_sb, do_sb, "tn", both, name="dw_sb_out"))
    comm.grad("w_ca_out", *_mm(y_ca, do_ca, "tn", both, name="dw_ca_out"))
    outs = ("w_sb_out", "w_ca_out")
    dy_sb = _mm(do_sb, w["w_sb_out"], "nt", (BF16,), name="dy_sb", ride=comm.pair(outs))
    dy_ca = _mm(do_ca, w["w_ca_out"], "nt", (BF16,), name="dy_ca")
    comm.add(outs)
    dq_sb, dk_sb, dv_sb = _sb_bwd(qkv, dy_sb, n_sb, 0, name="sb_bwd", ride=comm.chips(("w_ffn_in",)))
    comm.sum(("w_ffn_in",))
    late = ("w_mix_out",) + outs
    dq_ca, dk_ca, dv_ca, dbias = _ca_bwd(qkv, bias, dy_ca, n_ca, ca_col0, name="ca_bwd",
                                         ride=comm.chips(late, comm.share(("w_ffn_in",))))
    comm.sum(late)
    gs["rel_bias"] = _band_bias_grad(_group_bias_grad(dbias))
    dproj = _concat_cols([dq_sb, dk_sb, dv_sb, dq_ca, dk_ca, dv_ca, dg_sb, dg_ca], name="dproj")
    halves("w_in", h1, dproj, comm.share(late), "dw_in")
    comm.add(("w_in",))
    half = x.shape[0] // 2
    dh1 = _mm(dproj, w["w_in"], "nt", (F32,), name="dh_mix_top", rows=(0, half), ride=comm.tail(TAIL_SECOND))
    dh1 = _mm(dproj, w["w_in"], "nt", (F32,), name="dh_mix_bottom", rows=(half, half), onto=(dh1,),
              ride=comm.tail(TAIL_SECOND))
    grad_x, _, gs["g_mix"] = _rms_bwd(x, small["g_mix"], dh1, dx1, name="rms_mix_bwd", ride=comm.tail(TAIL_FIRST))
    return loss, grad_x, gs


def _position():
    x, y, c = lax.axis_index("x"), lax.axis_index("y"), lax.axis_index("c")
    chips = [(1 - x, y), (x, 1 - y), (1 - x, 1 - y)]
    return x, y, c, chips


def _aligned(v, m):
    return v if isinstance(v, int) else pl.multiple_of(v, m)


def _piece_dims(shape, axis):
    k, n = shape
    return (k // 2, n // N_CHIPS) if axis == 1 else (k // N_CHIPS // 2, n)


def _piece(ref, shape, axis, j, h, part=(0, 1)):
    pr, pc = _piece_dims(shape, axis)
    nr = pr // part[1] * (part[2] if len(part) > 2 else 1)
    r0 = part[0] * (pr // part[1])
    if axis == 1:
        return ref.at[pl.ds(_aligned(h * pr + r0, 16), nr), pl.ds(_aligned(j * pc, 128), pc)]
    return ref.at[pl.ds(_aligned((2 * j + h) * pr + r0, 16), nr), :]


def _shard_half(ref, h):
    rows = ref.shape[0] // 2
    return ref.at[pl.ds(_aligned(h * rows, 16), rows), :]


def _remote(src, dst, send_sems, recv_sems, k, to):
    return pltpu.make_async_remote_copy(src_ref=src, dst_ref=dst, send_sem=send_sems.at[k],
                                        recv_sem=recv_sems.at[k], device_id=to, device_id_type=MESH)


def _prefetch_call(body, scalars, ins, in_specs, out_shape, out_specs, grid, *, name, ride=None):
    single = not isinstance(out_shape, (list, tuple))
    outs = _call(body, ins, name=name, grid=grid, in_specs=in_specs,
                 out_specs=[out_specs] if single else out_specs, out_shape=[out_shape] if single else out_shape,
                 sem=("parallel",) * len(grid), ride=ride, scalars=scalars)
    return outs[0] if single else outs


def _slab_tiles(pr, pc):
    tc = pc if pc <= 4096 else _pick(pc, (2048, 1024, 512, 256, 128))
    tr = next(t for t in (1024, 512, 256, 128, 64, 32, 16) if pr % t == 0 and t * tc <= 512 * 1024)
    return tr, tc


def _cast_place(w, axis, pos, *, name, ride=None):
    ks, ns = w.shape
    shape = (ks, ns * N_CHIPS) if axis == 1 else (ks * N_CHIPS, ns)
    tr, tc = _slab_tiles(ks, ns)
    nr, nc = ks // tr, ns // tc

    def body(pos_ref, w_ref, o_ref):
        o_ref[...] = w_ref[...].astype(o_ref.dtype)

    if axis == 1:
        out_map = lambda i, j, pos_ref: (i, pos_ref[0] * nc + j)
    else:
        out_map = lambda i, j, pos_ref: (pos_ref[0] * nr + i, j)
    return _prefetch_call(body, pos, [w], [pl.BlockSpec((tr, tc), lambda i, j, pos_ref: (i, j))],
                          jax.ShapeDtypeStruct(shape, BF16), pl.BlockSpec((tr, tc), out_map), (nr, nc), name=name, ride=ride)


def _run(ride, *, name):
    if ride is None:
        return

    def body(o_ref):
        o_ref[...] = jnp.zeros_like(o_ref)

    _call(body, [], name=name, grid=(1,), in_specs=[], out_specs=[pl.BlockSpec((8, 128), lambda i: (0, 0))],
          out_shape=[jax.ShapeDtypeStruct((8, 128), F32)], ride=ride)


def _ride_gather(ride, w, n, axis, stage, part=(0, 1)):
    shape = w[n].shape
    piece = functools.partial(_piece, shape=shape, axis=axis)
    span = part[2] if len(part) > 2 else 1
    halves = [(2 * part[0] + t * span, 2 * part[1], span) for t in range(2)]

    def copies(ins, outs, send_sems, recv_sems, arriving):
        x, y, c, chips = _position()
        me, (xn, yn, dn) = 2 * x + y, [2 * px + py for px, py in chips]
        if stage == "near":
            plan = [(me, c, part, (1 - x, y, c), xn, c, part), (me, c, part, (x, 1 - y, c), yn, c, part)]
        else:
            plan = []
        if stage == "far":
            plan = [(yn, c, halves[1], (1 - x, y, c), dn, c, halves[1]), (xn, c, halves[0], (x, 1 - y, c), dn, c, halves[0])]
        to_sibling = {"far": (xn, yn), "pair": (dn,)}.get(stage, ())
        plan += [(j, c, part, (x, y, 1 - c), j, 1 - c, part) for j in to_sibling]
        out = []
        for k, (chip, h, rows, to, from_chip, from_h, from_rows) in enumerate(plan):
            if arriving:
                lands = piece(outs[0], j=from_chip, h=from_h, part=from_rows)
                out.append(_remote(lands, lands, send_sems, recv_sems, k, to))
            else:
                out.append(_remote(piece(ins[0], j=chip, h=h, part=rows), piece(outs[0], j=chip, h=h, part=rows),
                                   send_sems, recv_sems, k, to))
        return out

    def start(*refs):
        for cp in copies(*refs, arriving=False):
            cp.start()

    def finish(*refs):
        for cp in copies(*refs, arriving=True):
            cp.wait_recv()
        for cp in copies(*refs, arriving=False):
            cp.wait_send()

    ride.add([w[n]], [jax.ShapeDtypeStruct(shape, w[n].dtype)], {0: 0}, 4, start, finish,
             lambda outs: w.__setitem__(n, outs[0]))


def _ride_pair(ride, st, axis):
    shape = st["g16"].shape
    pr, pc = (shape[0], shape[1] // N_CHIPS) if st.get("half") else _piece_dims(shape, axis)

    def copies(ins, outs, send_sems, recv_sems):
        x, y, c, _ = _position()
        if st.get("half"):
            pieces = [ins[0].at[:, pl.ds(j * pc, pc)] for j in range(N_CHIPS)]
        else:
            pieces = [_piece(ins[0], shape, axis, j, 1 - c) for j in range(N_CHIPS)]
        return [_remote(pieces[j], outs[0].at[j], send_sems, recv_sems, j, (x, y, 1 - c)) for j in range(N_CHIPS)]

    def start(*refs):
        for cp in copies(*refs):
            cp.start()

    def finish(*refs):
        for cp in copies(*refs):
            cp.wait()

    ride.add([st["g16"]], [jax.ShapeDtypeStruct((N_CHIPS, pr, pc), BF16)], {}, N_CHIPS, start, finish,
             lambda outs: st.__setitem__("sib", outs[0]))


def _ride_chips(ride, st, rows=None):
    _, pr, pc = st["s16"].shape
    r0, nr = (0, pr) if rows is None else rows

    def copies(ins, outs, send_sems, recv_sems):
        x, y, c, chips = _position()
        return [_remote(ins[0].at[2 * px + py, pl.ds(r0, nr), :], outs[0].at[k, pl.ds(r0, nr), :],
                        send_sems, recv_sems, k, (px, py, c)) for k, (px, py) in enumerate(chips)]

    def start(*refs):
        for cp in copies(*refs):
            cp.start()

    def finish(*refs):
        for cp in copies(*refs):
            cp.wait()

    ins, aliases = ([st["s16"], st["recv"]], {1: 0}) if "recv" in st else ([st["s16"]], {})
    ride.add(ins, [jax.ShapeDtypeStruct((3, pr, pc), BF16)], aliases, 3, start, finish,
             lambda outs: st.__setitem__("recv", outs[0]))


def _ride_share(ride, st):
    def sent(ins, outs, send_sems, recv_sems):
        x, y, c, _ = _position()
        return _remote(_shard_half(ins[0], c), _shard_half(outs[0], c), send_sems, recv_sems, 0, (x, y, 1 - c))

    def landed(ins, outs, send_sems, recv_sems):
        x, y, c, _ = _position()
        other = _shard_half(outs[0], 1 - c)
        return _remote(other, other, send_sems, recv_sems, 0, (x, y, 1 - c))

    def start(*refs):
        sent(*refs).start()

    def finish(*refs):
        landed(*refs).wait_recv()
        sent(*refs).wait_send()

    ride.add([st["shard"]], [jax.ShapeDtypeStruct(st["shard"].shape, F32)], {0: 0}, 1, start, finish,
             lambda outs: st.__setitem__("g", outs[0]))


def _piece_block(axis, nr, nc, chip, half=False):
    if half:
        return lambda *a: (a[-3], (a[0] if chip is None else chip(a[-1])) * nc + a[-2])
    if axis == 1:
        return lambda *a: ((a[-1][1] * nr + a[-3]), (a[0] if chip is None else chip(a[-1])) * nc + a[-2])
    return lambda *a: ((2 * (a[0] if chip is None else chip(a[-1])) + a[-1][1]) * nr + a[-3], a[-2])


def _pair_add(g32, sib, axis, pos, *, name, half=False):
    _, pr, pc = sib.shape
    tr, tc = _slab_tiles(pr, pc)
    nr, nc = pr // tr, pc // tc

    def body(pos_ref, g_ref, b_ref, o16_ref):
        o16_ref[0] = (g_ref[...] + b_ref[0].astype(F32)).astype(o16_ref.dtype)

    blk = pl.BlockSpec((1, tr, tc), lambda j, i, k, pos_ref: (j, i, k))
    return _prefetch_call(body, pos, [g32, sib], [pl.BlockSpec((tr, tc), _piece_block(axis, nr, nc, None, half)), blk],
                          jax.ShapeDtypeStruct(sib.shape, BF16), blk, (N_CHIPS, nr, nc), name=name)


def _chip_sum(g32, sib, recv, axis, pos, *, name, half=False):
    _, pr, pc = sib.shape
    tr, tc = _slab_tiles(pr, pc)
    nr, nc = pr // tr, pc // tc

    def body(pos_ref, g_ref, b_ref, r_ref, o_ref):
        pair = g_ref[...] + b_ref[0].astype(F32)
        o_ref[...] = ((pair + r_ref[0].astype(F32)) + r_ref[1].astype(F32)) + r_ref[2].astype(F32)

    return _prefetch_call(
        body, pos, [g32, sib, recv],
        [pl.BlockSpec((tr, tc), _piece_block(axis, nr, nc, lambda pos_ref: pos_ref[0], half)),
         pl.BlockSpec((1, tr, tc), lambda i, k, pos_ref: (pos_ref[0], i, k)),
         pl.BlockSpec((3, tr, tc), lambda i, k, pos_ref: (0, i, k))],
        jax.ShapeDtypeStruct((2 * pr, pc), F32),
        pl.BlockSpec((tr, tc), lambda i, k, pos_ref: (pos_ref[1] * nr + i, k)), (nr, nc), name=name)


class _Comm:
    def __init__(self, pos, w):
        self.pos, self.w, self.st = pos, w, {n: {} for n, _ in BIG}

    def gather(self, names, stage, ride=None, part=(0, 1)):
        ride = _Ride() if ride is None else ride
        for n in names:
            _ride_gather(ride, self.w, n, AXIS[n], stage, part)
        return ride

    def grad(self, n, g32, g16, half=False):
        self.st[n].update(g32=g32, g16=g16, half=half)

    def pair(self, names, ride=None):
        ride = _Ride() if ride is None else ride
        for n in names:
            _ride_pair(ride, self.st[n], AXIS[n])
        return ride

    def add(self, names):
        for n in names:
            st = self.st[n]
            st["s16"] = _pair_add(st["g32"], st["sib"], AXIS[n], self.pos, name="rs_add_" + n, half=st["half"])

    def chips(self, names, ride=None, rows=None):
        ride = _Ride() if ride is None else ride
        for n in names:
            _ride_chips(ride, self.st[n], rows)
        return ride

    def sum(self, names):
        for n in names:
            st = self.st[n]
            st["shard"] = _chip_sum(st["g32"], st["sib"], st["recv"], AXIS[n], self.pos, name="rs_sum_" + n,
                                    half=st["half"])

    def share(self, names, ride=None):
        ride = _Ride() if ride is None else ride
        for n in names:
            _ride_share(ride, self.st[n])
        return ride

    def tail(self, count):
        st = self.st["w_in"]
        rows, at = st["s16"].shape[1], st.get("at", 0)
        st["at"] = at + count
        return self.chips(("w_in",), rows=(at * rows // TAIL_PARTS, count * rows // TAIL_PARTS))

    def tail_rest(self):
        return self.tail(TAIL_PARTS - self.st["w_in"].get("at", 0))

    def result(self, n):
        return self.st[n]["g"]


class _NoComm:
    pos = None

    def __init__(self, w):
        self.w, self.st = w, {}

    def grad(self, n, g32, g16, half=False):
        self.st[n] = (g32, g16)

    def result(self, n):
        return self.st[n]

    def add(self, names):
        pass

    sum = add

    def gather(self, names, *args, **kwargs):
        return None

    pair = chips = share = tail = gather


def _small_all_reduce(vec, *, name):
    r = vec.shape[0]

    def body(vec_ref, out_ref, slots, send_sems, recv_sems):
        x, y, c, _ = _position()
        me = 4 * x + 2 * y + c
        slots[me] = vec_ref[...]
        sends = []
        for k in range(1, 8):
            to = (x ^ (k >> 2), y ^ ((k >> 1) & 1), c ^ (k & 1))
            cp = _remote(slots.at[me], slots.at[me], send_sems, recv_sems, k - 1, to)
            cp.start()
            sends.append(cp)
        for k in range(1, 8):
            frm = 4 * (x ^ (k >> 2)) + 2 * (y ^ ((k >> 1) & 1)) + (c ^ (k & 1))
            _remote(slots.at[frm], slots.at[frm], send_sems, recv_sems, k - 1, (x, y, c)).wait_recv()
        for cp in sends:
            cp.wait_send()
        total = slots[0]
        for d in range(1, 8):
            total = total + slots[d]
        out_ref[...] = total

    return pl.pallas_call(
        body, name=name,
        in_specs=[pl.BlockSpec(memory_space=pltpu.VMEM)], out_specs=pl.BlockSpec(memory_space=pltpu.VMEM),
        out_shape=jax.ShapeDtypeStruct((r, 128), F32),
        scratch_shapes=[pltpu.VMEM((8, r, 128), F32), pltpu.SemaphoreType.DMA((7,)), pltpu.SemaphoreType.DMA((7,))],
    )(vec)


SC_TILES = 32
SC_LANES = 16
SC_TILE_BUDGET = 400 * 1024


def _adamw_update(wv, gv, mv, vv):
    nm = ADAM_B1 * mv + (1.0 - ADAM_B1) * gv
    nv = ADAM_B2 * vv + (1.0 - ADAM_B2) * (gv * gv)
    m_hat = nm / (1.0 - ADAM_B1 ** ADAM_STEP)
    v_hat = nv / (1.0 - ADAM_B2 ** ADAM_STEP)
    return -ADAM_LR * (m_hat / (jnp.sqrt(v_hat) + ADAM_EPS) + ADAM_WD * wv), nm, nv


def _adamw_sc(w, g, m, v, *, name):
    r, c = w.shape
    groups = r // 8
    per_tile = -(-groups // SC_TILES)
    cb = c if 4 * 8 * c * 4 <= SC_TILE_BUDGET else _pick(c, (2048, 1024, 512, 256, 128))

    def body(w_hbm, g_hbm, m_hbm, v_hbm, go_hbm, d_hbm, nm_hbm, nv_hbm, wb, gb, mb, vb):
        tile = lax.axis_index("sc_tile") * 2 + lax.axis_index("sc_core")

        def update(group):
            for c0 in range(0, c, cb):
                at = (pl.ds(group * 8, 8), pl.ds(c0, cb))
                for hbm, buf in ((w_hbm, wb), (g_hbm, gb), (m_hbm, mb), (v_hbm, vb)):
                    pltpu.sync_copy(hbm.at[at], buf)
                pltpu.sync_copy(gb, go_hbm.at[at])

                @pl.loop(0, 8)
                def _(rr):
                    @pl.loop(0, cb, step=SC_LANES)
                    def _(i):
                        lanes = (rr, pl.ds(i, SC_LANES))
                        wb[lanes], mb[lanes], vb[lanes] = _adamw_update(wb[lanes], gb[lanes], mb[lanes], vb[lanes])

                for buf, hbm in ((wb, d_hbm), (mb, nm_hbm), (vb, nv_hbm)):
                    pltpu.sync_copy(buf, hbm.at[at])

        @pl.loop(0, per_tile)
        def _(k):
            group = k * SC_TILES + tile
            if groups % SC_TILES:
                pl.when(group < groups)(lambda: update(group))
            else:
                update(group)

    sd = jax.ShapeDtypeStruct((r, c), F32)
    return pl.kernel(body, name=name, out_type=[sd, sd, sd, sd],
                     mesh=plsc.VectorSubcoreMesh(core_axis_name="sc_core", subcore_axis_name="sc_tile"),
                     scratch_types=[pltpu.VMEM((8, cb), F32)] * 4)(w, g, m, v)


def _adamw(w, g, m, v, *, name, ride=None):
    r, c = w.shape
    tc = c if c <= 4096 else _pick(c, (2048, 1024, 512, 256, 128))
    tr = next(t for t in (512, 256, 128, 64, 32, 16, 8) if r % t == 0 and t * tc <= 256 * 1024)

    def body(w_ref, g_ref, m_ref, v_ref, go_ref, d_ref, nm_ref, nv_ref):
        go_ref[...] = g_ref[...]
        d_ref[...], nm_ref[...], nv_ref[...] = _adamw_update(w_ref[...], g_ref[...], m_ref[...], v_ref[...])

    blk = ((tr, tc), lambda i, j: (i, j))
    sd = jax.ShapeDtypeStruct((r, c), F32)
    return _ew(body, [w, g, m, v], [blk] * 4, [sd] * 4, [blk] * 4, (r // tr, c // tc), name=name, ride=ride)


BIG = (("w_in", 1), ("w_sb_out", 1), ("w_ca_out", 1), ("w_mix_out", 0), ("w_ffn_in", 1), ("w_ffn_out", 0),
       ("w_ple_in", 1), ("w_ple_gate", 0))
AXIS = dict(BIG)
HEAD_PARTS = 8
HEAD_HOSTS = ("w_ffn_in", "w_ffn_out")
TAIL_PARTS = 16
TAIL_SECOND = 5
TAIL_FIRST = 2
ON_SPARSECORE = tuple(n for n, _ in BIG if n != "w_in")
SMALL = ("rel_bias", "g_mix", "g_ffn", "g_ple", "g_final")
ORDER = ("w_in", "w_sb_out", "w_ca_out", "w_mix_out", "rel_bias", "g_mix", "g_ffn", "g_ple", "g_final",
         "w_ffn_in", "w_ffn_out", "w_ple_in", "w_ple_gate")


def _pack(parts):
    flat = jnp.concatenate([a.reshape(-1) for a in parts])
    rows = -(-flat.shape[0] // 1024) * 8
    return jnp.pad(flat, (0, rows * 128 - flat.shape[0])).reshape(rows, 128)


def _unpack(packed, like):
    flat, out, at = packed.reshape(-1), [], 0
    for a in like:
        out.append(flat[at:at + a.size].reshape(a.shape))
        at += a.size
    return out


def kernel(x, p, w_in, w_sb_out, w_ca_out, w_mix_out, rel_bias, g_mix, g_ffn, g_ple, g_final, w_ffn_in, w_ffn_out, w_ple_in, w_ple_gate, loss_target, m_w_in, m_w_sb_out, m_w_ca_out, m_w_mix_out, m_rel_bias, m_g_mix, m_g_ffn, m_g_ple, m_g_final, m_w_ffn_in, m_w_ffn_out, m_w_ple_in, m_w_ple_gate, v_w_in, v_w_sb_out, v_w_ca_out, v_w_mix_out, v_rel_bias, v_g_mix, v_g_ffn, v_g_ple, v_g_final, v_w_ffn_in, v_w_ffn_out, v_w_ple_in, v_w_ple_gate):
    weights = dict(w_in=w_in, w_sb_out=w_sb_out, w_ca_out=w_ca_out, w_mix_out=w_mix_out, rel_bias=rel_bias,
                   g_mix=g_mix, g_ffn=g_ffn, g_ple=g_ple, g_final=g_final, w_ffn_in=w_ffn_in,
                   w_ffn_out=w_ffn_out, w_ple_in=w_ple_in, w_ple_gate=w_ple_gate)
    m_in = dict(w_in=m_w_in, w_sb_out=m_w_sb_out, w_ca_out=m_w_ca_out, w_mix_out=m_w_mix_out, rel_bias=m_rel_bias,
                g_mix=m_g_mix, g_ffn=m_g_ffn, g_ple=m_g_ple, g_final=m_g_final, w_ffn_in=m_w_ffn_in,
                w_ffn_out=m_w_ffn_out, w_ple_in=m_w_ple_in, w_ple_gate=m_w_ple_gate)
    v_in = dict(w_in=v_w_in, w_sb_out=v_w_sb_out, w_ca_out=v_w_ca_out, w_mix_out=v_w_mix_out, rel_bias=v_rel_bias,
                g_mix=v_g_mix, g_ffn=v_g_ffn, g_ple=v_g_ple, g_final=v_g_final, w_ffn_in=v_w_ffn_in,
                w_ffn_out=v_w_ffn_out, w_ple_in=v_w_ple_in, w_ple_gate=v_w_ple_gate)

    pos = jnp.stack([2 * lax.axis_index("x") + lax.axis_index("y"), lax.axis_index("c")]).astype(jnp.int32)
    comm = _Comm(pos, {"w_in": _cast_place(w_in[0], AXIS["w_in"], pos, name="cast_w_in")})
    at = 0
    for n in HEAD_HOSTS:
        ride = comm.gather(("w_in",), "near", part=(at, HEAD_PARTS))
        comm.w[n] = _cast_place(weights[n][0], AXIS[n], pos, name="cast_" + n, ride=ride)
        at += 1
    for n, axis in BIG:
        if n not in comm.w:
            comm.w[n] = _cast_place(weights[n][0], axis, pos, name="cast_" + n)
    _run(comm.gather(("w_in",), "near", part=(at, HEAD_PARTS, HEAD_PARTS - at)), name="gather_w_in_near")
    _run(comm.gather(("w_in",), "far"), name="gather_w_in_far")
    _run(comm.gather(("w_in",), "pair"), name="gather_w_in_pair")
    small = dict(rel_bias=rel_bias[0], g_mix=g_mix, g_ffn=g_ffn, g_ple=g_ple, g_final=g_final.reshape(1, -1))
    loss, grad_x, gs = _step(x[0], p[0, 0], loss_target[0], small, comm)

    grads, delta, new_m, new_v = {}, {}, {}, {}
    for n in [n for n, _ in BIG if n != "w_in"] + ["w_in"]:
        if n == "w_in":
            _run(comm.tail_rest(), name="rs_chips_w_in")
            comm.sum(("w_in",))
            _run(comm.share(("w_in",)), name="rs_share_w_in")
        update = _adamw_sc if n in ON_SPARSECORE else _adamw
        g, d, nm, nv = update(weights[n][0], comm.result(n), m_in[n][0], v_in[n][0], name="adamw_" + n)
        grads[n], delta[n], new_m[n], new_v[n] = g[None], d[None], nm[None], nv[None]

    like = [weights[n] for n in SMALL]
    reduced = _small_all_reduce(_pack([gs[n] for n in SMALL] + [loss[:, :1]]), name="small_all_reduce")
    g_small = _unpack(reduced, like + [loss[:, :1]])
    total_loss = g_small[-1].reshape(())
    g_packed = _pack(g_small[:-1])
    _, d_s, m_s, v_s = _adamw(_pack(like), g_packed, _pack([m_in[n] for n in SMALL]), _pack([v_in[n] for n in SMALL]),
                           name="adamw_small")
    for n, g, d, nm, nv in zip(SMALL, g_small[:-1], _unpack(d_s, like), _unpack(m_s, like), _unpack(v_s, like)):
        grads[n], delta[n], new_m[n], new_v[n] = g, d, nm, nv

    return (total_loss, grad_x[None], *[grads[n] for n in ORDER], *[delta[n] for n in ORDER],
            *[new_m[n] for n in ORDER], *[new_v[n] for n in ORDER])
```

```python
import functools
import math

import jax
import jax.numpy as jnp
from jax import lax
from jax.experimental import pallas as pl
from jax.experimental.pallas import tpu as pltpu
from jax.experimental.pallas import tpu_sc as plsc

F32 = jnp.float32
BF16 = jnp.bfloat16

HEAD_DIM = 128
CHUNK = 64
LEFT_CHUNKS = 8
REL_CLIP = 128
N_REL = REL_CLIP + CHUNK
BAND = (LEFT_CHUNKS + 2) * CHUNK
CA_PER_STEP = 4
CA_ROWS = CA_PER_STEP * CHUNK
CA_BAND = BAND + CA_PER_STEP * CHUNK
CA_PAD = BAND
SB_BLOCK = 128
SB_KEYS = 512
SB_GROUPS = SB_KEYS // SB_BLOCK
SB_ROWS = SB_KEYS
EPS = 1e-6
NEG = -1e30

ADAM_LR = 0.001
ADAM_B1 = 0.9
ADAM_B2 = 0.999
ADAM_EPS = 1e-08
ADAM_WD = 0.01
ADAM_STEP = 10

VMEM_LIMIT = 48 * 1024 * 1024
MM_VMEM_BUDGET = 36 * 1024 * 1024
V7X_HBM_BYTES_PER_S = 3.7e12
GRID_STEP_S = 0.35e-6
MESH = pl.DeviceIdType.MESH
N_CHIPS = 4


def _pick(dim, prefs):
    for t in prefs:
        if dim % t == 0:
            return t
    raise ValueError(f"no tile for {dim}")


def _cparams(sem=None):
    return pltpu.CompilerParams(dimension_semantics=sem, vmem_limit_bytes=VMEM_LIMIT)


def _sigmoid(v):
    return 1.0 / (1.0 + jnp.exp(-v))


def _dot(a, b, dims):
    return lax.dot_general(a, b, (dims, ((), ())), preferred_element_type=F32)


def _dot_nn(a, b):
    return _dot(a, b, ((1,), (0,)))


def _dot_nt(a, b):
    return _dot(a, b, ((1,), (1,)))


def _dot_tn(a, b):
    return _dot(a, b, ((0,), (0,)))


HBM = pl.BlockSpec(memory_space=pltpu.HBM)


class _Ride:
    def __init__(self):
        self.items = []

    def add(self, ins, outs, aliases, n_sems, start, finish, sink):
        self.items.append((ins, outs, aliases, n_sems, start, finish, sink))


def _call(body, args, *, name, grid, in_specs, out_specs, out_shape, scratch_shapes=(), sem=None, ride=None,
          scalars=None, onto=()):
    items = ride.items if ride is not None else []
    if onto:
        args, in_specs = list(args) + list(onto), list(in_specs) + [HBM] * len(onto)
        inner, body = body, lambda *refs: inner(*refs[:len(args) - len(onto)], *refs[len(args):])
    n_in, n_out, n_scr = len(args), len(out_shape), len(scratch_shapes)
    r_ins = [a for it in items for a in it[0]]
    r_outs = [o for it in items for o in it[1]]
    updated = [id(it[0][i]) for it in items for i in it[2]]
    assert len(set(updated)) == len(updated), "one call may update a buffer in place only once"
    aliases, a, b = {n_in - len(onto) + t: t for t in range(len(onto))}, n_in, n_out
    for it in items:
        aliases.update({a + i: b + o for i, o in it[2].items()})
        a, b = a + len(it[0]), b + len(it[1])
    sems = [pltpu.SemaphoreType.DMA((it[3],)) for it in items for _ in range(2)]

    def wrapped(*refs):
        head, refs = (refs[:1], refs[1:]) if scalars is not None else ((), refs)
        ins, rin = refs[:n_in], refs[n_in:n_in + len(r_ins)]
        at = n_in + len(r_ins)
        outs, rout = refs[at:at + n_out], refs[at + n_out:at + n_out + len(r_outs)]
        at += n_out + len(r_outs)
        scr, rsem = refs[at:at + n_scr], refs[at + n_scr:]

        def each(which):
            a = b = 0
            for q, it in enumerate(items):
                it[which](rin[a:a + len(it[0])], rout[b:b + len(it[1])], rsem[2 * q], rsem[2 * q + 1])
                a, b = a + len(it[0]), b + len(it[1])

        if items:
            ids = [pl.program_id(d) for d in range(len(grid))]
            first = functools.reduce(jnp.logical_and, [i == 0 for i in ids])
            last = functools.reduce(jnp.logical_and, [i == g - 1 for i, g in zip(ids, grid)])
            pl.when(first)(lambda: each(4))
        body(*head, *ins, *outs, *scr)
        if items:
            pl.when(last)(lambda: each(5))

    specs = dict(grid=grid, in_specs=list(in_specs) + [HBM] * len(r_ins),
                 out_specs=list(out_specs) + [HBM] * len(r_outs), scratch_shapes=list(scratch_shapes) + sems)
    if scalars is not None:
        specs = dict(grid_spec=pltpu.PrefetchScalarGridSpec(num_scalar_prefetch=1, **specs))
        aliases = {i + 1: o for i, o in aliases.items()}
    res = pl.pallas_call(
        wrapped, name=name, **specs,
        out_shape=list(out_shape) + r_outs,
        input_output_aliases=aliases,
        compiler_params=_cparams(("arbitrary",) * len(grid) if items else sem),
    )(*(() if scalars is None else (scalars,)), *args, *r_ins)
    b = n_out
    for it in items:
        it[6](res[b:b + len(it[1])])
        b += len(it[1])
    return list(res[:n_out])


def _mm_tiles(m, n_align, n, k, a_bytes, b_bytes, out_bytes):
    best = None
    tks = sorted({t for t in (k, k // 2, k // 4, 2048, 1024, 512, 256, 128) if t <= k and k % t == 0 and t % 128 == 0})
    for tm in (t for t in (2048, 1024, 512, 256, 128) if m % t == 0):
        for tn in (t for t in (2048, 1024, 512, 256, 128) if n_align % t == 0):
            for tk in tks:
                nk = k // tk
                vmem = 2 * (tm * tk * a_bytes + tk * tn * b_bytes + tm * tn * out_bytes) + tm * tn * 4
                if vmem > MM_VMEM_BUDGET:
                    continue
                traffic = m * k * a_bytes * (n // tn if nk > 1 else 1) + k * n * b_bytes * (m // tm)
                traffic += tm * tk * a_bytes + tk * tn * b_bytes + tm * tn * out_bytes
                traffic += m * n * 4 * nk if nk > 1 else 0
                cost = traffic / V7X_HBM_BYTES_PER_S + (m // tm) * (n // tn) * nk * GRID_STEP_S
                if best is None or cost < best[0]:
                    best = (cost, tm, tn, tk)
    return best[1:]


def _mm(a, b, mode, out_dtypes, *, name, n=None, b_col_off=0, resid=None, ride=None, rows=None, onto=(), m_half=None):
    if mode == "nn":
        m, k = a.shape
        n = b.shape[1] if n is None else n
    elif mode == "nt":
        m, k = a.shape
        n = b.shape[0]
    else:
        k, m = a.shape
        n = b.shape[1]
    if m_half is not None:
        m //= 2
    m_all, (row0, m) = m, (0, m) if rows is None else rows
    n_out = len(out_dtypes)
    has_resid = resid is not None
    out_bytes = sum(jnp.dtype(dt).itemsize for dt in out_dtypes) + (4 if has_resid else 0)
    tm, tn, tk = _mm_tiles(math.gcd(m, row0) if row0 else m, math.gcd(n, b_col_off) if b_col_off else n, n, k,
                           a.dtype.itemsize, b.dtype.itemsize, out_bytes)
    nk = k // tk
    boff, roff = b_col_off // tn, row0 // tm
    dot = {"nn": _dot_nn, "nt": _dot_nt, "tn": _dot_tn}[mode]
    if m_half is None:
        half = lambda: 0
    else:
        half = lambda pos_ref: (pos_ref[1] if m_half[0] else 1 - pos_ref[1]) * (m // tm)

    def body(*refs):
        refs = refs[m_half is not None:]
        a_ref, b_ref = refs[0], refs[1]
        r_ref = refs[2] if has_resid else None
        o_refs = refs[2 + has_resid: 2 + has_resid + n_out]

        def finish(r):
            if has_resid:
                r = r + r_ref[...]
            for o_ref in o_refs:
                o_ref[...] = r.astype(o_ref.dtype)

        part = dot(a_ref[...].astype(BF16), b_ref[...].astype(BF16))
        if nk == 1:
            finish(part)
            return
        acc_ref = refs[-1]
        kk = pl.program_id(2)

        @pl.when(kk == 0)
        def _():
            acc_ref[...] = part

        @pl.when(kk > 0)
        def _():
            acc_ref[...] += part

        @pl.when(kk == nk - 1)
        def _():
            finish(acc_ref[...])

    if mode == "nn":
        a_spec = pl.BlockSpec((tm, tk), lambda i, j, kk, *_: (i + roff, kk))
        b_spec = pl.BlockSpec((tk, tn), lambda i, j, kk, *_: (kk, j + boff))
    elif mode == "nt":
        a_spec = pl.BlockSpec((tm, tk), lambda i, j, kk, *_: (i + roff, kk))
        b_spec = pl.BlockSpec((tn, tk), lambda i, j, kk, *_: (j, kk))
    else:
        a_spec = pl.BlockSpec((tk, tm), lambda i, j, kk, *pos: (kk, i + half(*pos)))
        b_spec = pl.BlockSpec((tk, tn), lambda i, j, kk, *_: (kk, j))
    o_spec = pl.BlockSpec((tm, tn), lambda i, j, kk, *_: (i + roff, j))
    in_specs = [a_spec, b_spec] + ([o_spec] if has_resid else [])
    args = [a, b] + ([resid] if has_resid else [])
    outs = _call(
        body, args, name=name,
        grid=(m // tm, n // tn, nk),
        in_specs=in_specs,
        out_specs=[o_spec] * n_out,
        out_shape=[jax.ShapeDtypeStruct((m_all, n), dt) for dt in out_dtypes],
        scratch_shapes=[pltpu.VMEM((tm, tn), F32)] if nk > 1 else [],
        sem=("parallel", "parallel", "arbitrary"), ride=ride, onto=onto,
        scalars=None if m_half is None else m_half[1])
    return outs[0] if n_out == 1 else tuple(outs)


def _row_tile(s):
    return _pick(s, (256, 128))


def _rms_fwd(x, g, *, name, ride=None):
    s, d = x.shape
    tr = _row_tile(s)

    def body(x_ref, g_ref, o_ref):
        xv = x_ref[...]
        r = lax.rsqrt(jnp.mean(xv * xv, axis=1, keepdims=True) + EPS)
        o_ref[...] = (xv * r * g_ref[...]).astype(o_ref.dtype)

    return _call(
        body, [x, g], name=name, grid=(s // tr,),
        in_specs=[pl.BlockSpec((tr, d), lambda i: (i, 0)), pl.BlockSpec((1, d), lambda i: (0, 0))],
        out_specs=[pl.BlockSpec((tr, d), lambda i: (i, 0))],
        out_shape=[jax.ShapeDtypeStruct((s, d), BF16)], sem=("parallel",), ride=ride)[0]


def _rms_bwd(x, g, dh, dres, *, name, ride=None):
    s, d = x.shape
    tr = _row_tile(s)

    def body(x_ref, g_ref, dh_ref, dres_ref, dx_ref, dx16_ref, dg_ref):
        i = pl.program_id(0)
        xv = x_ref[...]
        r = lax.rsqrt(jnp.mean(xv * xv, axis=1, keepdims=True) + EPS)
        xhat = xv * r
        dhv = dh_ref[...]
        dxhat = dhv * g_ref[...]
        proj = jnp.mean(dxhat * xhat, axis=1, keepdims=True)
        dx = dres_ref[...] + r * (dxhat - xhat * proj)
        dx_ref[...] = dx
        dx16_ref[...] = dx.astype(dx16_ref.dtype)

        @pl.when(i == 0)
        def _():
            dg_ref[...] = jnp.zeros_like(dg_ref)

        dg_ref[...] += jnp.sum(dhv * xhat, axis=0, keepdims=True)

    row = pl.BlockSpec((tr, d), lambda i: (i, 0))
    vec = pl.BlockSpec((1, d), lambda i: (0, 0))
    return _call(
        body, [x, g, dh, dres], name=name, grid=(s // tr,),
        in_specs=[row, vec, row, row],
        out_specs=[row, row, vec],
        out_shape=[jax.ShapeDtypeStruct((s, d), F32), jax.ShapeDtypeStruct((s, d), BF16),
                   jax.ShapeDtypeStruct((1, d), F32)],
        sem=("arbitrary",), ride=ride)


def _final_loss(x, g, target, *, name):
    s, d = x.shape
    tr = _row_tile(s)

    def body(x_ref, g_ref, t_ref, dx_ref, dg_ref, loss_ref):
        i = pl.program_id(0)
        xv = x_ref[...]
        gv = g_ref[...]
        r = lax.rsqrt(jnp.mean(xv * xv, axis=1, keepdims=True) + EPS)
        xhat = xv * r
        err = xhat * gv - t_ref[...]
        dy = err * (1.0 / d)
        dxhat = dy * gv
        proj = jnp.mean(dxhat * xhat, axis=1, keepdims=True)
        dx_ref[...] = r * (dxhat - xhat * proj)

        @pl.when(i == 0)
        def _():
            dg_ref[...] = jnp.zeros_like(dg_ref)
            loss_ref[...] = jnp.zeros_like(loss_ref)

        dg_ref[...] += jnp.sum(dy * xhat, axis=0, keepdims=True)
        part = 0.5 * jnp.sum(jnp.mean(err * err, axis=1, keepdims=True), axis=0, keepdims=True)
        loss_ref[...] += jnp.broadcast_to(part, loss_ref.shape)

    row = pl.BlockSpec((tr, d), lambda i: (i, 0))
    vec = pl.BlockSpec((1, d), lambda i: (0, 0))
    return pl.pallas_call(
        body, name=name, grid=(s // tr,),
        in_specs=[row, vec, row],
        out_specs=[row, vec, pl.BlockSpec((1, 128), lambda i: (0, 0))],
        out_shape=[jax.ShapeDtypeStruct((s, d), F32), jax.ShapeDtypeStruct((1, d), F32),
                   jax.ShapeDtypeStruct((1, 128), F32)],
        compiler_params=_cparams(("arbitrary",)),
    )(x, g, target)


def _ew(body, ins, in_blocks, outs, out_blocks, grid, *, name, ride=None):
    return _call(body, ins, name=name, grid=grid,
                 in_specs=[pl.BlockSpec(bs, im) for bs, im in in_blocks],
                 out_specs=[pl.BlockSpec(bs, im) for bs, im in out_blocks],
                 out_shape=outs, sem=("parallel",) * len(grid), ride=ride)


def _gate_merge_fwd(gates, o_sb, o_ca, *, name, ride=None):
    s, d = o_sb.shape
    tr, tc = _row_tile(s), _pick(d, (1024, 512, 256, 128))
    nc = d // tc

    def body(gs_ref, gc_ref, os_ref, oc_ref, m_ref):
        f32 = lambda ref: ref[...].astype(F32)
        m = _sigmoid(f32(gs_ref)) * f32(os_ref) + _sigmoid(f32(gc_ref)) * f32(oc_ref)
        m_ref[...] = m.astype(m_ref.dtype)

    blk = ((tr, tc), lambda i, j: (i, j))
    return _ew(body, [gates, gates, o_sb, o_ca],
               [blk, ((tr, tc), lambda i, j: (i, j + nc)), blk, blk],
               [jax.ShapeDtypeStruct((s, d), BF16)], [blk], (s // tr, nc), name=name, ride=ride)[0]


def _gate_merge_bwd(dmerged, gates, o_sb, o_ca, *, name):
    s, d = o_sb.shape
    tr, tc = _row_tile(s), _pick(d, (1024, 512, 256, 128))
    nc = d // tc

    def body(dm_ref, gs_ref, gc_ref, os_ref, oc_ref, dgs_ref, dgc_ref, dos_ref, doc_ref):
        f32 = lambda ref: ref[...].astype(F32)
        dm = dm_ref[...]
        ss = _sigmoid(f32(gs_ref))
        sc = _sigmoid(f32(gc_ref))
        dgs_ref[...] = (dm * f32(os_ref) * ss * (1.0 - ss)).astype(dgs_ref.dtype)
        dgc_ref[...] = (dm * f32(oc_ref) * sc * (1.0 - sc)).astype(dgc_ref.dtype)
        dos_ref[...] = (dm * ss).astype(dos_ref.dtype)
        doc_ref[...] = (dm * sc).astype(doc_ref.dtype)

    blk = ((tr, tc), lambda i, j: (i, j))
    sd = jax.ShapeDtypeStruct((s, d), BF16)
    return _ew(body, [dmerged, gates, gates, o_sb, o_ca],
               [blk, blk, ((tr, tc), lambda i, j: (i, j + nc)), blk, blk],
               [sd, sd, sd, sd], [blk, blk, blk, blk], (s // tr, nc), name=name)


def _swiglu_fwd(gu, *, name, ride=None):
    s, f2 = gu.shape
    f = f2 // 2
    tr, tc = 128, _pick(f, (512, 256, 128))

    def body(gu_ref, a_ref):
        for at in range(0, f, tc):
            gv = gu_ref[:, at:at + tc].astype(F32)
            a_ref[:, at:at + tc] = (gv * _sigmoid(gv) * gu_ref[:, f + at:f + at + tc].astype(F32)).astype(a_ref.dtype)

    row = lambda i: (i, 0)
    return _ew(body, [gu], [((tr, f2), row)], [jax.ShapeDtypeStruct((s, f), BF16)], [((tr, f), row)],
               (s // tr,), name=name, ride=ride)[0]


def _swiglu_bwd(dact, gu, *, name):
    s, f2 = gu.shape
    f = f2 // 2
    tr, tc = 128, _pick(f, (512, 256, 128))

    def body(da_ref, gu_ref, o_ref):
        for at in range(0, f, tc):
            da = da_ref[:, at:at + tc].astype(F32)
            gv = gu_ref[:, at:at + tc].astype(F32)
            sg = _sigmoid(gv)
            uv = gu_ref[:, f + at:f + at + tc].astype(F32)
            o_ref[:, at:at + tc] = (da * uv * sg * (1.0 + gv * (1.0 - sg))).astype(o_ref.dtype)
            o_ref[:, f + at:f + at + tc] = (da * gv * sg).astype(o_ref.dtype)

    row = lambda i: (i, 0)
    return _ew(body, [dact, gu], [((tr, f), row), ((tr, f2), row)], [jax.ShapeDtypeStruct((s, f2), BF16)],
               [((tr, f2), row)], (s // tr,), name=name)[0]


def _concat_cols(parts, *, name):
    s = parts[0].shape[0]
    widths = [p.shape[1] for p in parts]
    tr = 256

    def body(*refs):
        o_ref, at = refs[-1], 0
        for p_ref, width in zip(refs, widths):
            o_ref[:, at:at + width] = p_ref[...]
            at += width

    row = lambda i: (i, 0)
    return _ew(body, list(parts), [((tr, width), row) for width in widths],
               [jax.ShapeDtypeStruct((s, sum(widths)), parts[0].dtype)], [((tr, sum(widths)), row)],
               (s // tr,), name=name)[0]


def _ple_fwd(x, t, pe, *, name):
    s, d = x.shape
    tr, tc = _row_tile(s), _pick(d, (1024, 512, 256, 128))

    def body(x_ref, t_ref, p_ref, o_ref):
        o_ref[...] = x_ref[...] + _sigmoid(t_ref[...].astype(F32)) * p_ref[...].astype(F32)

    blk = ((tr, tc), lambda i, j: (i, j))
    return _ew(body, [x, t, pe], [blk, blk, blk],
               [jax.ShapeDtypeStruct((s, d), F32)], [blk], (s // tr, d // tc), name=name)[0]


def _ple_bwd(dx, t, pe, *, name):
    s, d = dx.shape
    tr, tc = _row_tile(s), _pick(d, (1024, 512, 256, 128))

    def body(dx_ref, t_ref, p_ref, dt_ref, dp_ref):
        dxv = dx_ref[...]
        sg = _sigmoid(t_ref[...].astype(F32))
        dt_ref[...] = (dxv * p_ref[...].astype(F32) * sg * (1.0 - sg)).astype(dt_ref.dtype)
        dp_ref[...] = (dxv * sg).astype(dp_ref.dtype)

    blk = ((tr, tc), lambda i, j: (i, j))
    sd = jax.ShapeDtypeStruct((s, d), BF16)
    return _ew(body, [dx, t, pe], [blk, blk, blk], [sd, sd], [blk, blk], (s // tr, d // tc), name=name)


def _sb_tri(later):
    row = lax.broadcasted_iota(jnp.int32, (SB_BLOCK, SB_BLOCK), 0)
    col = lax.broadcasted_iota(jnp.int32, (SB_BLOCK, SB_BLOCK), 1)
    tri = (row > col) if later else (row < col)
    return jnp.concatenate([tri.astype(BF16), jnp.ones((SB_BLOCK, SB_BLOCK), BF16)], axis=1)


def _sb_valid(i, j, own):
    if not own:
        return None
    qi = i * SB_ROWS + lax.broadcasted_iota(jnp.int32, (SB_ROWS, SB_KEYS), 0)
    ki = j * SB_KEYS + lax.broadcasted_iota(jnp.int32, (SB_ROWS, SB_KEYS), 1)
    return ki < qi


def _sb_scan(v, tri, run, later):
    hi = v.astype(BF16)
    lo = (v - hi.astype(F32)).astype(BF16)
    outs = [None] * SB_GROUPS
    for b in (reversed(range(SB_GROUPS)) if later else range(SB_GROUPS)):
        cols = slice(b * SB_BLOCK, (b + 1) * SB_BLOCK)
        r = _dot_nn(hi[:, cols], tri) + _dot_nn(lo[:, cols], tri)
        outs[b] = r[:, :SB_BLOCK] + run
        run = run + r[:, SB_BLOCK:]
    return jnp.concatenate(outs, axis=1), run


def _masked(valid, v):
    return v if valid is None else jnp.where(valid, v, 0.0)


def _sb_scores(q, kj, scale, valid):
    z = _dot_nt(q, kj) * scale
    t = jnp.log(1.0 + jnp.exp(-jnp.abs(z)))
    return jnp.minimum(z, 0.0) - t, _masked(valid, -jnp.maximum(z, 0.0) - t)


def _sb_specs(h_count, s, col0):
    q_spec = pl.BlockSpec((SB_ROWS, HEAD_DIM), lambda h, i: (i, col0 + h))
    k_spec = pl.BlockSpec((s, HEAD_DIM), lambda h, i: (0, col0 + h_count + h))
    v_spec = pl.BlockSpec((s, HEAD_DIM), lambda h, i: (0, col0 + 2 * h_count + h))
    return q_spec, k_spec, v_spec


def _sb_fwd(qkv, n_heads, col0, *, name, ride=None):
    s = qkv.shape[0]
    nq = s // SB_ROWS
    scale = HEAD_DIM ** -0.5

    def body(q_ref, k_ref, v_ref, o_ref):
        i = pl.program_id(1)
        q = q_ref[...]
        tri = _sb_tri(later=True)

        def step(j, carry, own):
            run, acc = carry
            off = pl.multiple_of(j * SB_KEYS, SB_KEYS)
            valid = _sb_valid(i, j, own)
            ls, lk = _sb_scores(q, k_ref[pl.ds(off, SB_KEYS), :], scale, valid)
            between, run = _sb_scan(lk, tri, run, later=True)
            a = _masked(valid, jnp.exp(ls + between))
            return run, acc + _dot_nn(a.astype(BF16), v_ref[pl.ds(off, SB_KEYS), :])

        carry = step(i, (jnp.zeros((SB_ROWS, SB_BLOCK), F32), jnp.zeros((SB_ROWS, HEAD_DIM), F32)), True)
        _, acc = lax.fori_loop(0, i, lambda jj, c: step(i - 1 - jj, c, False), carry)
        o_ref[...] = acc.astype(o_ref.dtype)

    q_spec, k_spec, v_spec = _sb_specs(n_heads, s, col0)
    return _call(
        body, [qkv, qkv, qkv], name=name, grid=(n_heads, nq),
        in_specs=[q_spec, k_spec, v_spec],
        out_specs=[pl.BlockSpec((SB_ROWS, HEAD_DIM), lambda h, i: (i, h))],
        out_shape=[jax.ShapeDtypeStruct((s, n_heads * HEAD_DIM), BF16)],
        sem=("parallel", "arbitrary"), ride=ride)[0]


def _sb_bwd(qkv, dy, n_heads, col0, *, name, ride=None):
    s = qkv.shape[0]
    nq = s // SB_ROWS
    scale = HEAD_DIM ** -0.5

    def body(q_ref, k_ref, v_ref, dy_ref, dq_ref, dk_ref, dv_ref, e_scr, sg_scr, dk_acc, dv_acc):
        i = pl.program_id(1)
        q = q_ref[...]
        dyv = dy_ref[...]

        @pl.when(i == 0)
        def _():
            dk_acc[...] = jnp.zeros_like(dk_acc)
            dv_acc[...] = jnp.zeros_like(dv_acc)

        tri_later = _sb_tri(later=True)

        def pass1(j, run, own):
            off = pl.multiple_of(j * SB_KEYS, SB_KEYS)
            valid = _sb_valid(i, j, own)
            ls, lk = _sb_scores(q, k_ref[pl.ds(off, SB_KEYS), :], scale, valid)
            between, run = _sb_scan(lk, tri_later, run, later=True)
            a = _masked(valid, jnp.exp(ls + between))
            e_scr[j] = a * _dot_nt(dyv, v_ref[pl.ds(off, SB_KEYS), :])
            sg_scr[j] = jnp.exp(ls)
            dv_acc[pl.ds(off, SB_KEYS), :] += _dot_tn(a.astype(BF16), dyv)
            return run

        lax.fori_loop(0, i, lambda jj, run: pass1(i - 1 - jj, run, False),
                      pass1(i, jnp.zeros((SB_ROWS, SB_BLOCK), F32), True))

        tri_earlier = _sb_tri(later=False)

        def pass2(j, carry, own):
            run, dq = carry
            off = pl.multiple_of(j * SB_KEYS, SB_KEYS)
            kj = k_ref[pl.ds(off, SB_KEYS), :]
            sg = sg_scr[j]
            e = e_scr[j]
            before, run = _sb_scan(e, tri_earlier, run, later=False)
            dz = _masked(_sb_valid(i, j, own), e * (1.0 - sg) - sg * before) * scale
            dzb = dz.astype(BF16)
            dk_acc[pl.ds(off, SB_KEYS), :] += _dot_tn(dzb, q)
            return run, dq + _dot_nn(dzb, kj)

        init = (jnp.zeros((SB_ROWS, SB_BLOCK), F32), jnp.zeros((SB_ROWS, HEAD_DIM), F32))
        _, dq = pass2(i, lax.fori_loop(0, i, lambda j, c: pass2(j, c, False), init), True)
        dq_ref[...] = dq.astype(dq_ref.dtype)

        @pl.when(i == nq - 1)
        def _():
            dk_ref[...] = dk_acc[...].astype(dk_ref.dtype)
            dv_ref[...] = dv_acc[...].astype(dv_ref.dtype)

    q_spec, k_spec, v_spec = _sb_specs(n_heads, s, col0)
    blk = pl.BlockSpec((SB_ROWS, HEAD_DIM), lambda h, i: (i, h))
    full = pl.BlockSpec((s, HEAD_DIM), lambda h, i: (0, h))
    sd = jax.ShapeDtypeStruct((s, n_heads * HEAD_DIM), BF16)
    return _call(
        body, [qkv, qkv, qkv, dy], name=name, grid=(n_heads, nq),
        in_specs=[q_spec, k_spec, v_spec, blk],
        out_specs=[blk, full, full],
        out_shape=[sd, sd, sd],
        scratch_shapes=[pltpu.VMEM((s // SB_KEYS, SB_ROWS, SB_KEYS), F32), pltpu.VMEM((s // SB_KEYS, SB_ROWS, SB_KEYS), F32),
                        pltpu.VMEM((s, HEAD_DIM), F32), pltpu.VMEM((s, HEAD_DIM), F32)],
        sem=("parallel", "arbitrary"), ride=ride)


def _band_bias(rel_bias):
    h = rel_bias.shape[0]
    width = BAND + CHUNK
    first = width - 1 - N_REL
    line = jnp.concatenate([jnp.broadcast_to(rel_bias[:, :1], (h, first)), rel_bias], axis=1)
    tiled = jnp.broadcast_to(line[:, None, :], (h, CHUNK, width - 1)).reshape(h, CHUNK * (width - 1))
    skew = jnp.pad(tiled, ((0, 0), (0, CHUNK))).reshape(h, CHUNK, width)[:, ::-1, :BAND]
    seen = jnp.arange(BAND) >= CHUNK
    return jnp.where(seen[None, None, :], skew, NEG)


def _band_bias_grad(dbias):
    h = dbias.shape[0]
    width = BAND + CHUNK
    flipped = jnp.pad(dbias[:, ::-1, :], ((0, 0), (0, 0), (0, CHUNK)))
    skew = flipped.reshape(h, CHUNK * width)[:, :CHUNK * (width - 1)].reshape(h, CHUNK, width - 1)
    diag = jnp.sum(skew, axis=1)
    first = width - 1 - N_REL
    clipped = jnp.sum(diag[:, :first + 1], axis=1, keepdims=True)
    return jnp.concatenate([clipped, diag[:, first + 1:]], axis=1)


def _group_bias(band):
    return jnp.concatenate([jnp.pad(band, ((0, 0), (0, 0), ((u + 1) * CHUNK, (CA_PER_STEP - 1 - u) * CHUNK)),
                                    constant_values=NEG) for u in range(CA_PER_STEP)], axis=1)


def _group_bias_grad(dgroup):
    return sum(dgroup[:, u * CHUNK:(u + 1) * CHUNK, (u + 1) * CHUNK:(u + 1) * CHUNK + BAND] for u in range(CA_PER_STEP))


def _ca_load_padded(k_ref, v_ref, kp, vp, s):
    kp[pl.ds(0, CA_PAD), :] = jnp.zeros((CA_PAD, HEAD_DIM), kp.dtype)
    vp[pl.ds(0, CA_PAD), :] = jnp.zeros((CA_PAD, HEAD_DIM), vp.dtype)
    kp[pl.ds(CA_PAD, s), :] = k_ref[...]
    vp[pl.ds(CA_PAD, s), :] = v_ref[...]


def _ca_weights(q, kb, bias, off, scale):
    z = _dot_nt(q, kb) * scale + bias
    pos = off + lax.broadcasted_iota(jnp.int32, (CA_ROWS, CA_BAND), 1)
    z = jnp.where(pos >= CA_PAD, z, NEG)
    p = jnp.exp(z - jnp.max(z, axis=1, keepdims=True))
    return p / jnp.sum(p, axis=1, keepdims=True)


def _ca_specs(h_count, s, col0):
    q_spec = pl.BlockSpec((CA_ROWS, HEAD_DIM), lambda h, c: (c, col0 + h))
    k_spec = pl.BlockSpec((s, HEAD_DIM), lambda h, c: (0, col0 + h_count + h))
    v_spec = pl.BlockSpec((s, HEAD_DIM), lambda h, c: (0, col0 + 2 * h_count + h))
    b_spec = pl.BlockSpec((1, CA_ROWS, CA_BAND), lambda h, c: (h, 0, 0))
    return q_spec, k_spec, v_spec, b_spec


def _ca_fwd(qkv, bias, n_heads, col0, *, name, ride=None):
    s = qkv.shape[0]
    nc = s // CA_ROWS
    scale = HEAD_DIM ** -0.5

    def body(q_ref, k_ref, v_ref, b_ref, o_ref, kp, vp):
        c = pl.program_id(1)

        @pl.when(c == 0)
        def _():
            _ca_load_padded(k_ref, v_ref, kp, vp, s)

        off = pl.multiple_of(c * CA_ROWS, CA_ROWS)
        w = _ca_weights(q_ref[...], kp[pl.ds(off, CA_BAND), :], b_ref[0], off, scale)
        o_ref[...] = _dot_nn(w.astype(BF16), vp[pl.ds(off, CA_BAND), :]).astype(o_ref.dtype)

    q_spec, k_spec, v_spec, b_spec = _ca_specs(n_heads, s, col0)
    return _call(
        body, [qkv, qkv, qkv, bias], name=name, grid=(n_heads, nc),
        in_specs=[q_spec, k_spec, v_spec, b_spec],
        out_specs=[pl.BlockSpec((CA_ROWS, HEAD_DIM), lambda h, c: (c, h))],
        out_shape=[jax.ShapeDtypeStruct((s, n_heads * HEAD_DIM), BF16)],
        scratch_shapes=[pltpu.VMEM((s + CA_PAD, HEAD_DIM), BF16), pltpu.VMEM((s + CA_PAD, HEAD_DIM), BF16)],
        sem=("parallel", "arbitrary"), ride=ride)[0]


def _ca_bwd(qkv, bias, dy, n_heads, col0, *, name, ride=None):
    s = qkv.shape[0]
    nc = s // CA_ROWS
    scale = HEAD_DIM ** -0.5

    def body(q_ref, k_ref, v_ref, b_ref, dy_ref, dq_ref, dk_ref, dv_ref, db_ref, kp, vp, dkp, dvp):
        c = pl.program_id(1)

        @pl.when(c == 0)
        def _():
            _ca_load_padded(k_ref, v_ref, kp, vp, s)
            dkp[...] = jnp.zeros_like(dkp)
            dvp[...] = jnp.zeros_like(dvp)
            db_ref[...] = jnp.zeros_like(db_ref)

        off = pl.multiple_of(c * CA_ROWS, CA_ROWS)
        band = pl.ds(off, CA_BAND)
        q = q_ref[...]
        dyv = dy_ref[...]
        kb = kp[band, :]
        w = _ca_weights(q, kb, b_ref[0], off, scale)
        dw = _dot_nt(dyv, vp[band, :])
        dvp[band, :] += _dot_tn(w.astype(BF16), dyv)
        dz = w * (dw - jnp.sum(w * dw, axis=1, keepdims=True))
        db_ref[0] += dz
        dzs = (dz * scale).astype(BF16)
        dq_ref[...] = _dot_nn(dzs, kb).astype(dq_ref.dtype)
        dkp[band, :] += _dot_tn(dzs, q)

        @pl.when(c == nc - 1)
        def _():
            dk_ref[...] = dkp[pl.ds(CA_PAD, s), :].astype(dk_ref.dtype)
            dv_ref[...] = dvp[pl.ds(CA_PAD, s), :].astype(dv_ref.dtype)

    q_spec, k_spec, v_spec, b_spec = _ca_specs(n_heads, s, col0)
    blk = pl.BlockSpec((CA_ROWS, HEAD_DIM), lambda h, c: (c, h))
    full = pl.BlockSpec((s, HEAD_DIM), lambda h, c: (0, h))
    sd = jax.ShapeDtypeStruct((s, n_heads * HEAD_DIM), BF16)
    return _call(
        body, [qkv, qkv, qkv, bias, dy], name=name, grid=(n_heads, nc),
        in_specs=[q_spec, k_spec, v_spec, b_spec, blk],
        out_specs=[blk, full, full, b_spec],
        out_shape=[sd, sd, sd, jax.ShapeDtypeStruct((n_heads, CA_ROWS, CA_BAND), F32)],
        scratch_shapes=[pltpu.VMEM((s + CA_PAD, HEAD_DIM), BF16), pltpu.VMEM((s + CA_PAD, HEAD_DIM), BF16),
                        pltpu.VMEM((s + CA_PAD, HEAD_DIM), F32), pltpu.VMEM((s + CA_PAD, HEAD_DIM), F32)],
        sem=("parallel", "arbitrary"), ride=ride)


EARLY = ("w_sb_out", "w_ca_out", "w_mix_out")


def _step(x, p, target, small, comm):
    w = comm.w
    d = x.shape[1]
    n_sb = w["w_sb_out"].shape[0] // HEAD_DIM
    n_ca = w["w_ca_out"].shape[0] // HEAD_DIM
    qkv_cols = 3 * HEAD_DIM * (n_sb + n_ca)
    ca_col0 = 3 * n_sb
    both = (F32, BF16)

    h1 = _rms_fwd(x, small["g_mix"], name="rms_mix")
    ffn, ple = ("w_ffn_in",), ("w_ple_gate", "w_ple_in")
    qkv = _mm(h1, w["w_in"], "nn", (BF16,), name="proj_qkv", n=qkv_cols, ride=comm.gather(EARLY, "near"))
    gates = _mm(h1, w["w_in"], "nn", (BF16,), name="proj_gates", n=2 * d, b_col_off=qkv_cols,
                ride=comm.gather(ffn, "near", comm.gather(EARLY, "far"), (0, 8)))
    bias = _group_bias(_band_bias(small["rel_bias"]))
    y_sb = _sb_fwd(qkv, n_sb, 0, name="sb_fwd", ride=comm.gather(ffn, "near", comm.gather(EARLY, "pair"), (1, 8, 7)))
    y_ca = _ca_fwd(qkv, bias, n_ca, ca_col0, name="ca_fwd", ride=comm.gather(ffn, "far"))
    out = ("w_ffn_out",)
    o_sb = _mm(y_sb, w["w_sb_out"], "nn", (BF16,), name="sb_out", ride=comm.gather(out, "near", part=(0, 4)))
    o_ca = _mm(y_ca, w["w_ca_out"], "nn", (BF16,), name="ca_out", ride=comm.gather(out, "near", part=(1, 4)))
    merged = _gate_merge_fwd(gates, o_sb, o_ca, name="gate_merge", ride=comm.gather(out, "near", part=(2, 4)))
    x1 = _mm(merged, w["w_mix_out"], "nn", (F32,), name="mix_out", resid=x,
             ride=comm.gather(out, "near", comm.gather(ffn, "pair"), (3, 4)))
    h2 = _rms_fwd(x1, small["g_ffn"], name="rms_ffn")
    gu = _mm(h2, w["w_ffn_in"], "nn", (BF16,), name="ffn_in", ride=comm.gather(ple, "near", comm.gather(out, "far")))
    act = _swiglu_fwd(gu, name="swiglu", ride=comm.gather(ple, "far", comm.gather(out, "pair")))
    x2 = _mm(act, w["w_ffn_out"], "nn", (F32,), name="ffn_out", resid=x1, ride=comm.gather(ple, "pair"))
    h3 = _rms_fwd(x2, small["g_ple"], name="rms_ple")
    t = _mm(h3, w["w_ple_gate"], "nn", (BF16,), name="ple_gate")
    pe = _mm(p, w["w_ple_in"], "nn", (BF16,), name="ple_in")
    x3 = _ple_fwd(x2, t, pe, name="ple_add")

    def halves(n, acts, dout, ride, name):
        if comm.pos is None:
            return comm.grad(n, *_mm(acts, dout, "tn", both, name=name))
        g16 = _mm(acts, dout, "tn", (BF16,), name=name + "_other", m_half=(False, comm.pos), ride=ride)
        comm.grad(n, None, g16, half=True)
        g32 = _mm(acts, dout, "tn", (F32,), name=name + "_own", m_half=(True, comm.pos), ride=comm.pair((n,)))
        comm.grad(n, g32, g16, half=True)

    gs = {}
    dx3, gs["g_final"], loss = _final_loss(x3, small["g_final"], target, name="final_loss")
    dt, dpe = _ple_bwd(dx3, t, pe, name="ple_bwd")
    comm.grad("w_ple_in", *_mm(p, dpe, "tn", both, name="dw_ple_in"))
    comm.grad("w_ple_gate", *_mm(h3, dt, "tn", both, name="dw_ple_gate"))
    ple = ("w_ple_in", "w_ple_gate")
    dh3 = _mm(dt, w["w_ple_gate"], "nt", (F32,), name="dh_ple", ride=comm.pair(ple))
    dx2, dx2_16, gs["g_ple"] = _rms_bwd(x2, small["g_ple"], dh3, dx3, name="rms_ple_bwd")
    comm.add(ple)
    comm.grad("w_ffn_out", *_mm(act, dx2_16, "tn", both, name="dw_ffn_out", ride=comm.chips(ple)))
    dact = _mm(dx2_16, w["w_ffn_out"], "nt", (BF16,), name="dact", ride=comm.pair(("w_ffn_out",)))
    dgu = _swiglu_bwd(dact, gu, name="swiglu_bwd")
    comm.sum(ple)
    comm.add(("w_ffn_out",))
    comm.grad("w_ffn_in", *_mm(h2, dgu, "tn", both, name="dw_ffn_in",
                               ride=comm.share(ple, comm.chips(("w_ffn_out",)))))
    dh2 = _mm(dgu, w["w_ffn_in"], "nt", (F32,), name="dh_ffn", ride=comm.pair(("w_ffn_in",)))
    dx1, dx1_16, gs["g_ffn"] = _rms_bwd(x1, small["g_ffn"], dh2, dx2, name="rms_ffn_bwd")
    comm.add(("w_ffn_in",))
    comm.sum(("w_ffn_out",))
    comm.grad("w_mix_out", *_mm(merged, dx1_16, "tn", both, name="dw_mix_out", ride=comm.share(("w_ffn_out",))))
    dmerged = _mm(dx1_16, w["w_mix_out"], "nt", (F32,), name="dmerged", ride=comm.pair(("w_mix_out",)))
    dg_sb, dg_ca, do_sb, do_ca = _gate_merge_bwd(dmerged, gates, o_sb, o_ca, name="gate_merge_bwd")
    comm.add(("w_mix_out",))
    comm.grad("w_sb_out", *_mm(y_sb, do_sb, "tn", both, name="dw_sb_out"))
    comm.grad("w_ca_out", *_mm(y_ca, do_ca, "tn", both, name="dw_ca_out"))
    outs = ("w_sb_out", "w_ca_out")
    dy_sb = _mm(do_sb, w["w_sb_out"], "nt", (BF16,), name="dy_sb", ride=comm.pair(outs))
    dy_ca = _mm(do_ca, w["w_ca_out"], "nt", (BF16,), name="dy_ca")
    comm.add(outs)
    dq_sb, dk_sb, dv_sb = _sb_bwd(qkv, dy_sb, n_sb, 0, name="sb_bwd", ride=comm.chips(("w_ffn_in",)))
    comm.sum(("w_ffn_in",))
    late = ("w_mix_out",) + outs
    dq_ca, dk_ca, dv_ca, dbias = _ca_bwd(qkv, bias, dy_ca, n_ca, ca_col0, name="ca_bwd",
                                         ride=comm.chips(late, comm.share(("w_ffn_in",))))
    comm.sum(late)
    gs["rel_bias"] = _band_bias_grad(_group_bias_grad(dbias))
    dproj = _concat_cols([dq_sb, dk_sb, dv_sb, dq_ca, dk_ca, dv_ca, dg_sb, dg_ca], name="dproj")
    halves("w_in", h1, dproj, comm.share(late), "dw_in")
    comm.add(("w_in",))
    half = x.shape[0] // 2
    dh1 = _mm(dproj, w["w_in"], "nt", (F32,), name="dh_mix_top", rows=(0, half), ride=comm.tail(TAIL_SECOND))
    dh1 = _mm(dproj, w["w_in"], "nt", (F32,), name="dh_mix_bottom", rows=(half, half), onto=(dh1,),
              ride=comm.tail(TAIL_SECOND))
    grad_x, _, gs["g_mix"] = _rms_bwd(x, small["g_mix"], dh1, dx1, name="rms_mix_bwd", ride=comm.tail(TAIL_FIRST))
    return loss, grad_x, gs


def _position():
    x, y, c = lax.axis_index("x"), lax.axis_index("y"), lax.axis_index("c")
    chips = [(1 - x, y), (x, 1 - y), (1 - x, 1 - y)]
    return x, y, c, chips


def _aligned(v, m):
    return v if isinstance(v, int) else pl.multiple_of(v, m)


def _piece_dims(shape, axis):
    k, n = shape
    return (k // 2, n // N_CHIPS) if axis == 1 else (k // N_CHIPS // 2, n)


def _piece(ref, shape, axis, j, h, part=(0, 1)):
    pr, pc = _piece_dims(shape, axis)
    nr = pr // part[1] * (part[2] if len(part) > 2 else 1)
    r0 = part[0] * (pr // part[1])
    if axis == 1:
        return ref.at[pl.ds(_aligned(h * pr + r0, 16), nr), pl.ds(_aligned(j * pc, 128), pc)]
    return ref.at[pl.ds(_aligned((2 * j + h) * pr + r0, 16), nr), :]


def _shard_half(ref, h):
    rows = ref.shape[0] // 2
    return ref.at[pl.ds(_aligned(h * rows, 16), rows), :]


def _remote(src, dst, send_sems, recv_sems, k, to):
    return pltpu.make_async_remote_copy(src_ref=src, dst_ref=dst, send_sem=send_sems.at[k],
                                        recv_sem=recv_sems.at[k], device_id=to, device_id_type=MESH)


def _prefetch_call(body, scalars, ins, in_specs, out_shape, out_specs, grid, *, name, ride=None):
    single = not isinstance(out_shape, (list, tuple))
    outs = _call(body, ins, name=name, grid=grid, in_specs=in_specs,
                 out_specs=[out_specs] if single else out_specs, out_shape=[out_shape] if single else out_shape,
                 sem=("parallel",) * len(grid), ride=ride, scalars=scalars)
    return outs[0] if single else outs


def _slab_tiles(pr, pc):
    tc = pc if pc <= 4096 else _pick(pc, (2048, 1024, 512, 256, 128))
    tr = next(t for t in (1024, 512, 256, 128, 64, 32, 16) if pr % t == 0 and t * tc <= 512 * 1024)
    return tr, tc


def _cast_place(w, axis, pos, *, name, ride=None):
    ks, ns = w.shape
    shape = (ks, ns * N_CHIPS) if axis == 1 else (ks * N_CHIPS, ns)
    tr, tc = _slab_tiles(ks, ns)
    nr, nc = ks // tr, ns // tc

    def body(pos_ref, w_ref, o_ref):
        o_ref[...] = w_ref[...].astype(o_ref.dtype)

    if axis == 1:
        out_map = lambda i, j, pos_ref: (i, pos_ref[0] * nc + j)
    else:
        out_map = lambda i, j, pos_ref: (pos_ref[0] * nr + i, j)
    return _prefetch_call(body, pos, [w], [pl.BlockSpec((tr, tc), lambda i, j, pos_ref: (i, j))],
                          jax.ShapeDtypeStruct(shape, BF16), pl.BlockSpec((tr, tc), out_map), (nr, nc), name=name, ride=ride)


def _run(ride, *, name):
    if ride is None:
        return

    def body(o_ref):
        o_ref[...] = jnp.zeros_like(o_ref)

    _call(body, [], name=name, grid=(1,), in_specs=[], out_specs=[pl.BlockSpec((8, 128), lambda i: (0, 0))],
          out_shape=[jax.ShapeDtypeStruct((8, 128), F32)], ride=ride)


def _ride_gather(ride, w, n, axis, stage, part=(0, 1)):
    shape = w[n].shape
    piece = functools.partial(_piece, shape=shape, axis=axis)
    span = part[2] if len(part) > 2 else 1
    halves = [(2 * part[0] + t * span, 2 * part[1], span) for t in range(2)]

    def copies(ins, outs, send_sems, recv_sems, arriving):
        x, y, c, chips = _position()
        me, (xn, yn, dn) = 2 * x + y, [2 * px + py for px, py in chips]
        if stage == "near":
            plan = [(me, c, part, (1 - x, y, c), xn, c, part), (me, c, part, (x, 1 - y, c), yn, c, part)]
        else:
            plan = []
        if stage == "far":
            plan = [(yn, c, halves[1], (1 - x, y, c), dn, c, halves[1]), (xn, c, halves[0], (x, 1 - y, c), dn, c, halves[0])]
        to_sibling = {"far": (xn, yn), "pair": (dn,)}.get(stage, ())
        plan += [(j, c, part, (x, y, 1 - c), j, 1 - c, part) for j in to_sibling]
        out = []
        for k, (chip, h, rows, to, from_chip, from_h, from_rows) in enumerate(plan):
            if arriving:
                lands = piece(outs[0], j=from_chip, h=from_h, part=from_rows)
                out.append(_remote(lands, lands, send_sems, recv_sems, k, to))
            else:
                out.append(_remote(piece(ins[0], j=chip, h=h, part=rows), piece(outs[0], j=chip, h=h, part=rows),
                                   send_sems, recv_sems, k, to))
        return out

    def start(*refs):
        for cp in copies(*refs, arriving=False):
            cp.start()

    def finish(*refs):
        for cp in copies(*refs, arriving=True):
            cp.wait_recv()
        for cp in copies(*refs, arriving=False):
            cp.wait_send()

    ride.add([w[n]], [jax.ShapeDtypeStruct(shape, w[n].dtype)], {0: 0}, 4, start, finish,
             lambda outs: w.__setitem__(n, outs[0]))


def _ride_pair(ride, st, axis):
    shape = st["g16"].shape
    pr, pc = (shape[0], shape[1] // N_CHIPS) if st.get("half") else _piece_dims(shape, axis)

    def copies(ins, outs, send_sems, recv_sems):
        x, y, c, _ = _position()
        if st.get("half"):
            pieces = [ins[0].at[:, pl.ds(j * pc, pc)] for j in range(N_CHIPS)]
        else:
            pieces = [_piece(ins[0], shape, axis, j, 1 - c) for j in range(N_CHIPS)]
        return [_remote(pieces[j], outs[0].at[j], send_sems, recv_sems, j, (x, y, 1 - c)) for j in range(N_CHIPS)]

    def start(*refs):
        for cp in copies(*refs):
            cp.start()

    def finish(*refs):
        for cp in copies(*refs):
            cp.wait()

    ride.add([st["g16"]], [jax.ShapeDtypeStruct((N_CHIPS, pr, pc), BF16)], {}, N_CHIPS, start, finish,
             lambda outs: st.__setitem__("sib", outs[0]))


def _ride_chips(ride, st, rows=None):
    _, pr, pc = st["s16"].shape
    r0, nr = (0, pr) if rows is None else rows

    def copies(ins, outs, send_sems, recv_sems):
        x, y, c, chips = _position()
        return [_remote(ins[0].at[2 * px + py, pl.ds(r0, nr), :], outs[0].at[k, pl.ds(r0, nr), :],
                        send_sems, recv_sems, k, (px, py, c)) for k, (px, py) in enumerate(chips)]

    def start(*refs):
        for cp in copies(*refs):
            cp.start()

    def finish(*refs):
        for cp in copies(*refs):
            cp.wait()

    ins, aliases = ([st["s16"], st["recv"]], {1: 0}) if "recv" in st else ([st["s16"]], {})
    ride.add(ins, [jax.ShapeDtypeStruct((3, pr, pc), BF16)], aliases, 3, start, finish,
             lambda outs: st.__setitem__("recv", outs[0]))


def _ride_share(ride, st):
    def sent(ins, outs, send_sems, recv_sems):
        x, y, c, _ = _position()
        return _remote(_shard_half(ins[0], c), _shard_half(outs[0], c), send_sems, recv_sems, 0, (x, y, 1 - c))

    def landed(ins, outs, send_sems, recv_sems):
        x, y, c, _ = _position()
        other = _shard_half(outs[0], 1 - c)
        return _remote(other, other, send_sems, recv_sems, 0, (x, y, 1 - c))

    def start(*refs):
        sent(*refs).start()

    def finish(*refs):
        landed(*refs).wait_recv()
        sent(*refs).wait_send()

    ride.add([st["shard"]], [jax.ShapeDtypeStruct(st["shard"].shape, F32)], {0: 0}, 1, start, finish,
             lambda outs: st.__setitem__("g", outs[0]))


def _piece_block(axis, nr, nc, chip, half=False):
    if half:
        return lambda *a: (a[-3], chip(a) * nc + a[-2])
    if axis == 1:
        return lambda *a: ((a[-1][1] * nr + a[-3]), chip(a) * nc + a[-2])
    return lambda *a: ((2 * chip(a) + a[-1][1]) * nr + a[-3], a[-2])


def _pair_add(g32, sib, axis, pos, *, name, half=False):
    _, pr, pc = sib.shape
    tr, tc = _slab_tiles(pr, pc)
    nr, nc = pr // tr, pc // tc

    def body(pos_ref, g_ref, b_ref, o16_ref):
        o16_ref[0] = (g_ref[...] + b_ref[0].astype(F32)).astype(o16_ref.dtype)

    other = lambda a: (a[-1][0] + 1 + a[0]) % N_CHIPS
    blk = pl.BlockSpec((1, tr, tc), lambda *a: (other(a), a[1], a[2]))
    return _prefetch_call(body, pos, [g32, sib], [pl.BlockSpec((tr, tc), _piece_block(axis, nr, nc, other, half)), blk],
                          jax.ShapeDtypeStruct(sib.shape, BF16), blk, (N_CHIPS - 1, nr, nc), name=name)


def _chip_sum(g32, sib, recv, axis, pos, *, name, half=False):
    _, pr, pc = sib.shape
    tr, tc = _slab_tiles(pr, pc)
    nr, nc = pr // tr, pc // tc

    def body(pos_ref, g_ref, b_ref, r_ref, o_ref):
        pair = g_ref[...] + b_ref[0].astype(F32)
        o_ref[...] = ((pair + r_ref[0].astype(F32)) + r_ref[1].astype(F32)) + r_ref[2].astype(F32)

    return _prefetch_call(
        body, pos, [g32, sib, recv],
        [pl.BlockSpec((tr, tc), _piece_block(axis, nr, nc, lambda a: a[-1][0], half)),
         pl.BlockSpec((1, tr, tc), lambda i, k, pos_ref: (pos_ref[0], i, k)),
         pl.BlockSpec((3, tr, tc), lambda i, k, pos_ref: (0, i, k))],
        jax.ShapeDtypeStruct((2 * pr, pc), F32),
        pl.BlockSpec((tr, tc), lambda i, k, pos_ref: (pos_ref[1] * nr + i, k)), (nr, nc), name=name)


class _Comm:
    def __init__(self, pos, w):
        self.pos, self.w, self.st = pos, w, {n: {} for n, _ in BIG}

    def gather(self, names, stage, ride=None, part=(0, 1)):
        ride = _Ride() if ride is None else ride
        for n in names:
            _ride_gather(ride, self.w, n, AXIS[n], stage, part)
        return ride

    def grad(self, n, g32, g16, half=False):
        self.st[n].update(g32=g32, g16=g16, half=half)

    def pair(self, names, ride=None):
        ride = _Ride() if ride is None else ride
        for n in names:
            _ride_pair(ride, self.st[n], AXIS[n])
        return ride

    def add(self, names):
        for n in names:
            st = self.st[n]
            st["s16"] = _pair_add(st["g32"], st["sib"], AXIS[n], self.pos, name="rs_add_" + n, half=st["half"])

    def chips(self, names, ride=None, rows=None):
        ride = _Ride() if ride is None else ride
        for n in names:
            _ride_chips(ride, self.st[n], rows)
        return ride

    def sum(self, names):
        for n in names:
            st = self.st[n]
            st["shard"] = _chip_sum(st["g32"], st["sib"], st["recv"], AXIS[n], self.pos, name="rs_sum_" + n,
                                    half=st["half"])

    def share(self, names, ride=None):
        ride = _Ride() if ride is None else ride
        for n in names:
            _ride_share(ride, self.st[n])
        return ride

    def tail(self, count):
        st = self.st["w_in"]
        rows, at = st["s16"].shape[1], st.get("at", 0)
        st["at"] = at + count
        return self.chips(("w_in",), rows=(at * rows // TAIL_PARTS, count * rows // TAIL_PARTS))

    def tail_rest(self):
        return self.tail(TAIL_PARTS - self.st["w_in"].get("at", 0))

    def result(self, n):
        return self.st[n]["g"]


class _NoComm:
    pos = None

    def __init__(self, w):
        self.w, self.st = w, {}

    def grad(self, n, g32, g16, half=False):
        self.st[n] = (g32, g16)

    def result(self, n):
        return self.st[n]

    def add(self, names):
        pass

    sum = add

    def gather(self, names, *args, **kwargs):
        return None

    pair = chips = share = tail = gather


def _small_all_reduce(vec, *, name):
    r = vec.shape[0]

    def body(vec_ref, out_ref, slots, send_sems, recv_sems):
        x, y, c, _ = _position()
        me = 4 * x + 2 * y + c
        slots[me] = vec_ref[...]
        sends = []
        for k in range(1, 8):
            to = (x ^ (k >> 2), y ^ ((k >> 1) & 1), c ^ (k & 1))
            cp = _remote(slots.at[me], slots.at[me], send_sems, recv_sems, k - 1, to)
            cp.start()
            sends.append(cp)
        for k in range(1, 8):
            frm = 4 * (x ^ (k >> 2)) + 2 * (y ^ ((k >> 1) & 1)) + (c ^ (k & 1))
            _remote(slots.at[frm], slots.at[frm], send_sems, recv_sems, k - 1, (x, y, c)).wait_recv()
        for cp in sends:
            cp.wait_send()
        total = slots[0]
        for d in range(1, 8):
            total = total + slots[d]
        out_ref[...] = total

    return pl.pallas_call(
        body, name=name,
        in_specs=[pl.BlockSpec(memory_space=pltpu.VMEM)], out_specs=pl.BlockSpec(memory_space=pltpu.VMEM),
        out_shape=jax.ShapeDtypeStruct((r, 128), F32),
        scratch_shapes=[pltpu.VMEM((8, r, 128), F32), pltpu.SemaphoreType.DMA((7,)), pltpu.SemaphoreType.DMA((7,))],
    )(vec)


SC_TILES = 32
SC_LANES = 16
SC_TILE_BUDGET = 400 * 1024


def _adamw_update(wv, gv, mv, vv):
    nm = ADAM_B1 * mv + (1.0 - ADAM_B1) * gv
    nv = ADAM_B2 * vv + (1.0 - ADAM_B2) * (gv * gv)
    m_hat = nm / (1.0 - ADAM_B1 ** ADAM_STEP)
    v_hat = nv / (1.0 - ADAM_B2 ** ADAM_STEP)
    return -ADAM_LR * (m_hat / (jnp.sqrt(v_hat) + ADAM_EPS) + ADAM_WD * wv), nm, nv


def _adamw_sc(w, g, m, v, *, name):
    r, c = w.shape
    groups = r // 8
    per_tile = -(-groups // SC_TILES)
    cb = c if 4 * 8 * c * 4 <= SC_TILE_BUDGET else _pick(c, (2048, 1024, 512, 256, 128))

    def body(w_hbm, g_hbm, m_hbm, v_hbm, go_hbm, d_hbm, nm_hbm, nv_hbm, wb, gb, mb, vb):
        tile = lax.axis_index("sc_tile") * 2 + lax.axis_index("sc_core")

        def update(group):
            for c0 in range(0, c, cb):
                at = (pl.ds(group * 8, 8), pl.ds(c0, cb))
                for hbm, buf in ((w_hbm, wb), (g_hbm, gb), (m_hbm, mb), (v_hbm, vb)):
                    pltpu.sync_copy(hbm.at[at], buf)
                pltpu.sync_copy(gb, go_hbm.at[at])

                @pl.loop(0, 8)
                def _(rr):
                    @pl.loop(0, cb, step=SC_LANES)
                    def _(i):
                        lanes = (rr, pl.ds(i, SC_LANES))
                        wb[lanes], mb[lanes], vb[lanes] = _adamw_update(wb[lanes], gb[lanes], mb[lanes], vb[lanes])

                for buf, hbm in ((wb, d_hbm), (mb, nm_hbm), (vb, nv_hbm)):
                    pltpu.sync_copy(buf, hbm.at[at])

        @pl.loop(0, per_tile)
        def _(k):
            group = k * SC_TILES + tile
            if groups % SC_TILES:
                pl.when(group < groups)(lambda: update(group))
            else:
                update(group)

    sd = jax.ShapeDtypeStruct((r, c), F32)
    return pl.kernel(body, name=name, out_type=[sd, sd, sd, sd],
                     mesh=plsc.VectorSubcoreMesh(core_axis_name="sc_core", subcore_axis_name="sc_tile"),
                     scratch_types=[pltpu.VMEM((8, cb), F32)] * 4)(w, g, m, v)


def _adamw(w, g, m, v, *, name, ride=None):
    r, c = w.shape
    tc = c if c <= 4096 else _pick(c, (2048, 1024, 512, 256, 128))
    tr = next(t for t in (512, 256, 128, 64, 32, 16, 8) if r % t == 0 and t * tc <= 256 * 1024)

    def body(w_ref, g_ref, m_ref, v_ref, go_ref, d_ref, nm_ref, nv_ref):
        go_ref[...] = g_ref[...]
        d_ref[...], nm_ref[...], nv_ref[...] = _adamw_update(w_ref[...], g_ref[...], m_ref[...], v_ref[...])

    blk = ((tr, tc), lambda i, j: (i, j))
    sd = jax.ShapeDtypeStruct((r, c), F32)
    return _ew(body, [w, g, m, v], [blk] * 4, [sd] * 4, [blk] * 4, (r // tr, c // tc), name=name, ride=ride)


BIG = (("w_in", 1), ("w_sb_out", 1), ("w_ca_out", 1), ("w_mix_out", 0), ("w_ffn_in", 1), ("w_ffn_out", 0),
       ("w_ple_in", 1), ("w_ple_gate", 0))
AXIS = dict(BIG)
HEAD_PARTS = 8
HEAD_HOSTS = ("w_ffn_in", "w_ffn_out")
TAIL_PARTS = 16
TAIL_SECOND = 5
TAIL_FIRST = 2
ON_SPARSECORE = tuple(n for n, _ in BIG if n != "w_in")
SMALL = ("rel_bias", "g_mix", "g_ffn", "g_ple", "g_final")
ORDER = ("w_in", "w_sb_out", "w_ca_out", "w_mix_out", "rel_bias", "g_mix", "g_ffn", "g_ple", "g_final",
         "w_ffn_in", "w_ffn_out", "w_ple_in", "w_ple_gate")


def _pack(parts):
    flat = jnp.concatenate([a.reshape(-1) for a in parts])
    rows = -(-flat.shape[0] // 1024) * 8
    return jnp.pad(flat, (0, rows * 128 - flat.shape[0])).reshape(rows, 128)


def _unpack(packed, like):
    flat, out, at = packed.reshape(-1), [], 0
    for a in like:
        out.append(flat[at:at + a.size].reshape(a.shape))
        at += a.size
    return out


def kernel(x, p, w_in, w_sb_out, w_ca_out, w_mix_out, rel_bias, g_mix, g_ffn, g_ple, g_final, w_ffn_in, w_ffn_out, w_ple_in, w_ple_gate, loss_target, m_w_in, m_w_sb_out, m_w_ca_out, m_w_mix_out, m_rel_bias, m_g_mix, m_g_ffn, m_g_ple, m_g_final, m_w_ffn_in, m_w_ffn_out, m_w_ple_in, m_w_ple_gate, v_w_in, v_w_sb_out, v_w_ca_out, v_w_mix_out, v_rel_bias, v_g_mix, v_g_ffn, v_g_ple, v_g_final, v_w_ffn_in, v_w_ffn_out, v_w_ple_in, v_w_ple_gate):
    weights = dict(w_in=w_in, w_sb_out=w_sb_out, w_ca_out=w_ca_out, w_mix_out=w_mix_out, rel_bias=rel_bias,
                   g_mix=g_mix, g_ffn=g_ffn, g_ple=g_ple, g_final=g_final, w_ffn_in=w_ffn_in,
                   w_ffn_out=w_ffn_out, w_ple_in=w_ple_in, w_ple_gate=w_ple_gate)
    m_in = dict(w_in=m_w_in, w_sb_out=m_w_sb_out, w_ca_out=m_w_ca_out, w_mix_out=m_w_mix_out, rel_bias=m_rel_bias,
                g_mix=m_g_mix, g_ffn=m_g_ffn, g_ple=m_g_ple, g_final=m_g_final, w_ffn_in=m_w_ffn_in,
                w_ffn_out=m_w_ffn_out, w_ple_in=m_w_ple_in, w_ple_gate=m_w_ple_gate)
    v_in = dict(w_in=v_w_in, w_sb_out=v_w_sb_out, w_ca_out=v_w_ca_out, w_mix_out=v_w_mix_out, rel_bias=v_rel_bias,
                g_mix=v_g_mix, g_ffn=v_g_ffn, g_ple=v_g_ple, g_final=v_g_final, w_ffn_in=v_w_ffn_in,
                w_ffn_out=v_w_ffn_out, w_ple_in=v_w_ple_in, w_ple_gate=v_w_ple_gate)

    pos = jnp.stack([2 * lax.axis_index("x") + lax.axis_index("y"), lax.axis_index("c")]).astype(jnp.int32)
    comm = _Comm(pos, {"w_in": _cast_place(w_in[0], AXIS["w_in"], pos, name="cast_w_in")})
    at = 0
    for n in HEAD_HOSTS:
        ride = comm.gather(("w_in",), "near", part=(at, HEAD_PARTS))
        comm.w[n] = _cast_place(weights[n][0], AXIS[n], pos, name="cast_" + n, ride=ride)
        at += 1
    for n, axis in BIG:
        if n not in comm.w:
            comm.w[n] = _cast_place(weights[n][0], axis, pos, name="cast_" + n)
    _run(comm.gather(("w_in",), "near", part=(at, HEAD_PARTS, HEAD_PARTS - at)), name="gather_w_in_near")
    _run(comm.gather(("w_in",), "far"), name="gather_w_in_far")
    _run(comm.gather(("w_in",), "pair"), name="gather_w_in_pair")
    small = dict(rel_bias=rel_bias[0], g_mix=g_mix, g_ffn=g_ffn, g_ple=g_ple, g_final=g_final.reshape(1, -1))
    loss, grad_x, gs = _step(x[0], p[0, 0], loss_target[0], small, comm)

    grads, delta, new_m, new_v = {}, {}, {}, {}
    for n in [n for n, _ in BIG if n != "w_in"] + ["w_in"]:
        if n == "w_in":
            _run(comm.tail_rest(), name="rs_chips_w_in")
            comm.sum(("w_in",))
            _run(comm.share(("w_in",)), name="rs_share_w_in")
        update = _adamw_sc if n in ON_SPARSECORE else _adamw
        g, d, nm, nv = update(weights[n][0], comm.result(n), m_in[n][0], v_in[n][0], name="adamw_" + n)
        grads[n], delta[n], new_m[n], new_v[n] = g[None], d[None], nm[None], nv[None]

    like = [weights[n] for n in SMALL]
    reduced = _small_all_reduce(_pack([gs[n] for n in SMALL] + [loss[:, :1]]), name="small_all_reduce")
    g_small = _unpack(reduced, like + [loss[:, :1]])
    total_loss = g_small[-1].reshape(())
    g_packed = _pack(g_small[:-1])
    _, d_s, m_s, v_s = _adamw(_pack(like), g_packed, _pack([m_in[n] for n in SMALL]), _pack([v_in[n] for n in SMALL]),
                           name="adamw_small")
    for n, g, d, nm, nv in zip(SMALL, g_small[:-1], _unpack(d_s, like), _unpack(m_s, like), _unpack(v_s, like)):
        grads[n], delta[n], new_m[n], new_v[n] = g, d, nm, nv

    return (total_loss, grad_x[None], *[grads[n] for n in ORDER], *[delta[n] for n in ORDER],
            *[new_m[n] for n in ORDER], *[new_v[n] for n in ORDER])
```

```python
import functools
import math

import jax
import jax.numpy as jnp
from jax import lax
from jax.experimental import pallas as pl
from jax.experimental.pallas import tpu as pltpu
from jax.experimental.pallas import tpu_sc as plsc

F32 = jnp.float32
BF16 = jnp.bfloat16

HEAD_DIM = 128
CHUNK = 64
LEFT_CHUNKS = 8
REL_CLIP = 128
N_REL = REL_CLIP + CHUNK
BAND = (LEFT_CHUNKS + 2) * CHUNK
CA_PER_STEP = 4
CA_ROWS = CA_PER_STEP * CHUNK
CA_BAND = BAND + CA_PER_STEP * CHUNK
CA_PAD = BAND
SB_BLOCK = 128
SB_KEYS = 512
SB_GROUPS = SB_KEYS // SB_BLOCK
SB_ROWS = SB_KEYS
EPS = 1e-6
NEG = -1e30

ADAM_LR = 0.001
ADAM_B1 = 0.9
ADAM_B2 = 0.999
ADAM_EPS = 1e-08
ADAM_WD = 0.01
ADAM_STEP = 10

VMEM_LIMIT = 48 * 1024 * 1024
MM_VMEM_BUDGET = 36 * 1024 * 1024
V7X_HBM_BYTES_PER_S = 3.7e12
GRID_STEP_S = 0.35e-6
MESH = pl.DeviceIdType.MESH
N_CHIPS = 4


def _pick(dim, prefs):
    for t in prefs:
        if dim % t == 0:
            return t
    raise ValueError(f"no tile for {dim}")


def _cparams(sem=None):
    return pltpu.CompilerParams(dimension_semantics=sem, vmem_limit_bytes=VMEM_LIMIT)


def _sigmoid(v):
    return 1.0 / (1.0 + jnp.exp(-v))


def _dot(a, b, dims):
    return lax.dot_general(a, b, (dims, ((), ())), preferred_element_type=F32)


def _dot_nn(a, b):
    return _dot(a, b, ((1,), (0,)))


def _dot_nt(a, b):
    return _dot(a, b, ((1,), (1,)))


def _dot_tn(a, b):
    return _dot(a, b, ((0,), (0,)))


HBM = pl.BlockSpec(memory_space=pltpu.HBM)


class _Ride:
    def __init__(self):
        self.items = []

    def add(self, ins, outs, aliases, n_sems, start, finish, sink):
        self.items.append((ins, outs, aliases, n_sems, start, finish, sink))


def _call(body, args, *, name, grid, in_specs, out_specs, out_shape, scratch_shapes=(), sem=None, ride=None,
          scalars=None, onto=()):
    items = ride.items if ride is not None else []
    if onto:
        args, in_specs = list(args) + list(onto), list(in_specs) + [HBM] * len(onto)
        inner, body = body, lambda *refs: inner(*refs[:len(args) - len(onto)], *refs[len(args):])
    n_in, n_out, n_scr = len(args), len(out_shape), len(scratch_shapes)
    r_ins = [a for it in items for a in it[0]]
    r_outs = [o for it in items for o in it[1]]
    updated = [id(it[0][i]) for it in items for i in it[2]]
    assert len(set(updated)) == len(updated), "one call may update a buffer in place only once"
    aliases, a, b = {n_in - len(onto) + t: t for t in range(len(onto))}, n_in, n_out
    for it in items:
        aliases.update({a + i: b + o for i, o in it[2].items()})
        a, b = a + len(it[0]), b + len(it[1])
    sems = [pltpu.SemaphoreType.DMA((it[3],)) for it in items for _ in range(2)]

    def wrapped(*refs):
        head, refs = (refs[:1], refs[1:]) if scalars is not None else ((), refs)
        ins, rin = refs[:n_in], refs[n_in:n_in + len(r_ins)]
        at = n_in + len(r_ins)
        outs, rout = refs[at:at + n_out], refs[at + n_out:at + n_out + len(r_outs)]
        at += n_out + len(r_outs)
        scr, rsem = refs[at:at + n_scr], refs[at + n_scr:]

        def each(which):
            a = b = 0
            for q, it in enumerate(items):
                it[which](rin[a:a + len(it[0])], rout[b:b + len(it[1])], rsem[2 * q], rsem[2 * q + 1])
                a, b = a + len(it[0]), b + len(it[1])

        if items:
            ids = [pl.program_id(d) for d in range(len(grid))]
            first = functools.reduce(jnp.logical_and, [i == 0 for i in ids])
            last = functools.reduce(jnp.logical_and, [i == g - 1 for i, g in zip(ids, grid)])
            pl.when(first)(lambda: each(4))
        body(*head, *ins, *outs, *scr)
        if items:
            pl.when(last)(lambda: each(5))

    specs = dict(grid=grid, in_specs=list(in_specs) + [HBM] * len(r_ins),
                 out_specs=list(out_specs) + [HBM] * len(r_outs), scratch_shapes=list(scratch_shapes) + sems)
    if scalars is not None:
        specs = dict(grid_spec=pltpu.PrefetchScalarGridSpec(num_scalar_prefetch=1, **specs))
        aliases = {i + 1: o for i, o in aliases.items()}
    res = pl.pallas_call(
        wrapped, name=name, **specs,
        out_shape=list(out_shape) + r_outs,
        input_output_aliases=aliases,
        compiler_params=_cparams(("arbitrary",) * len(grid) if items else sem),
    )(*(() if scalars is None else (scalars,)), *args, *r_ins)
    b = n_out
    for it in items:
        it[6](res[b:b + len(it[1])])
        b += len(it[1])
    return list(res[:n_out])


def _mm_tiles(m, n_align, n, k, a_bytes, b_bytes, out_bytes):
    best = None
    tks = sorted({t for t in (k, k // 2, k // 4, 2048, 1024, 512, 256, 128) if t <= k and k % t == 0 and t % 128 == 0})
    for tm in (t for t in (2048, 1024, 512, 256, 128) if m % t == 0):
        for tn in (t for t in (2048, 1024, 512, 256, 128) if n_align % t == 0):
            for tk in tks:
                nk = k // tk
                vmem = 2 * (tm * tk * a_bytes + tk * tn * b_bytes + tm * tn * out_bytes) + tm * tn * 4
                if vmem > MM_VMEM_BUDGET:
                    continue
                traffic = m * k * a_bytes * (n // tn if nk > 1 else 1) + k * n * b_bytes * (m // tm)
                traffic += tm * tk * a_bytes + tk * tn * b_bytes + tm * tn * out_bytes
                traffic += m * n * 4 * nk if nk > 1 else 0
                cost = traffic / V7X_HBM_BYTES_PER_S + (m // tm) * (n // tn) * nk * GRID_STEP_S
                if best is None or cost < best[0]:
                    best = (cost, tm, tn, tk)
    return best[1:]


def _mm(a, b, mode, out_dtypes, *, name, n=None, b_col_off=0, resid=None, ride=None, rows=None, onto=(), m_half=None):
    if mode == "nn":
        m, k = a.shape
        n = b.shape[1] if n is None else n
    elif mode == "nt":
        m, k = a.shape
        n = b.shape[0]
    else:
        k, m = a.shape
        n = b.shape[1]
    if m_half is not None:
        m //= 2
    m_all, (row0, m) = m, (0, m) if rows is None else rows
    n_out = len(out_dtypes)
    has_resid = resid is not None
    out_bytes = sum(jnp.dtype(dt).itemsize for dt in out_dtypes) + (4 if has_resid else 0)
    tm, tn, tk = _mm_tiles(math.gcd(m, row0) if row0 else m, math.gcd(n, b_col_off) if b_col_off else n, n, k,
                           a.dtype.itemsize, b.dtype.itemsize, out_bytes)
    nk = k // tk
    boff, roff = b_col_off // tn, row0 // tm
    dot = {"nn": _dot_nn, "nt": _dot_nt, "tn": _dot_tn}[mode]
    if m_half is None:
        half = lambda: 0
    else:
        half = lambda pos_ref: (pos_ref[1] if m_half[0] else 1 - pos_ref[1]) * (m // tm)

    def body(*refs):
        refs = refs[m_half is not None:]
        a_ref, b_ref = refs[0], refs[1]
        r_ref = refs[2] if has_resid else None
        o_refs = refs[2 + has_resid: 2 + has_resid + n_out]

        def finish(r):
            if has_resid:
                r = r + r_ref[...]
            for o_ref in o_refs:
                o_ref[...] = r.astype(o_ref.dtype)

        part = dot(a_ref[...].astype(BF16), b_ref[...].astype(BF16))
        if nk == 1:
            finish(part)
            return
        acc_ref = refs[-1]
        kk = pl.program_id(2)

        @pl.when(kk == 0)
        def _():
            acc_ref[...] = part

        @pl.when(kk > 0)
        def _():
            acc_ref[...] += part

        @pl.when(kk == nk - 1)
        def _():
            finish(acc_ref[...])

    if mode == "nn":
        a_spec = pl.BlockSpec((tm, tk), lambda i, j, kk, *_: (i + roff, kk))
        b_spec = pl.BlockSpec((tk, tn), lambda i, j, kk, *_: (kk, j + boff))
    elif mode == "nt":
        a_spec = pl.BlockSpec((tm, tk), lambda i, j, kk, *_: (i + roff, kk))
        b_spec = pl.BlockSpec((tn, tk), lambda i, j, kk, *_: (j, kk))
    else:
        a_spec = pl.BlockSpec((tk, tm), lambda i, j, kk, *pos: (kk, i + half(*pos)))
        b_spec = pl.BlockSpec((tk, tn), lambda i, j, kk, *_: (kk, j))
    o_spec = pl.BlockSpec((tm, tn), lambda i, j, kk, *_: (i + roff, j))
    in_specs = [a_spec, b_spec] + ([o_spec] if has_resid else [])
    args = [a, b] + ([resid] if has_resid else [])
    outs = _call(
        body, args, name=name,
        grid=(m // tm, n // tn, nk),
        in_specs=in_specs,
        out_specs=[o_spec] * n_out,
        out_shape=[jax.ShapeDtypeStruct((m_all, n), dt) for dt in out_dtypes],
        scratch_shapes=[pltpu.VMEM((tm, tn), F32)] if nk > 1 else [],
        sem=("parallel", "parallel", "arbitrary"), ride=ride, onto=onto,
        scalars=None if m_half is None else m_half[1])
    return outs[0] if n_out == 1 else tuple(outs)


def _row_tile(s):
    return _pick(s, (256, 128))


def _rms_fwd(x, g, *, name, ride=None):
    s, d = x.shape
    tr = _row_tile(s)

    def body(x_ref, g_ref, o_ref):
        xv = x_ref[...]
        r = lax.rsqrt(jnp.mean(xv * xv, axis=1, keepdims=True) + EPS)
        o_ref[...] = (xv * r * g_ref[...]).astype(o_ref.dtype)

    return _call(
        body, [x, g], name=name, grid=(s // tr,),
        in_specs=[pl.BlockSpec((tr, d), lambda i: (i, 0)), pl.BlockSpec((1, d), lambda i: (0, 0))],
        out_specs=[pl.BlockSpec((tr, d), lambda i: (i, 0))],
        out_shape=[jax.ShapeDtypeStruct((s, d), BF16)], sem=("parallel",), ride=ride)[0]


def _rms_bwd(x, g, dh, dres, *, name, ride=None):
    s, d = x.shape
    tr = _row_tile(s)

    def body(x_ref, g_ref, dh_ref, dres_ref, dx_ref, dx16_ref, dg_ref):
        i = pl.program_id(0)
        xv = x_ref[...]
        r = lax.rsqrt(jnp.mean(xv * xv, axis=1, keepdims=True) + EPS)
        xhat = xv * r
        dhv = dh_ref[...]
        dxhat = dhv * g_ref[...]
        proj = jnp.mean(dxhat * xhat, axis=1, keepdims=True)
        dx = dres_ref[...] + r * (dxhat - xhat * proj)
        dx_ref[...] = dx
        dx16_ref[...] = dx.astype(dx16_ref.dtype)

        @pl.when(i == 0)
        def _():
            dg_ref[...] = jnp.zeros_like(dg_ref)

        dg_ref[...] += jnp.sum(dhv * xhat, axis=0, keepdims=True)

    row = pl.BlockSpec((tr, d), lambda i: (i, 0))
    vec = pl.BlockSpec((1, d), lambda i: (0, 0))
    return _call(
        body, [x, g, dh, dres], name=name, grid=(s // tr,),
        in_specs=[row, vec, row, row],
        out_specs=[row, row, vec],
        out_shape=[jax.ShapeDtypeStruct((s, d), F32), jax.ShapeDtypeStruct((s, d), BF16),
                   jax.ShapeDtypeStruct((1, d), F32)],
        sem=("arbitrary",), ride=ride)


def _ple_loss(x2, gate, pe, g, target, *, name):
    s, d = x2.shape
    tr = _row_tile(s)

    def body(x2_ref, gate_ref, pe_ref, g_ref, t_ref, dx_ref, dgate_ref, dpe_ref, dg_ref, loss_ref):
        i = pl.program_id(0)
        sg = _sigmoid(gate_ref[...].astype(F32))
        pv = pe_ref[...].astype(F32)
        xv = x2_ref[...] + sg * pv
        gv = g_ref[...]
        r = lax.rsqrt(jnp.mean(xv * xv, axis=1, keepdims=True) + EPS)
        xhat = xv * r
        err = xhat * gv - t_ref[...]
        dy = err * (1.0 / d)
        dxhat = dy * gv
        proj = jnp.mean(dxhat * xhat, axis=1, keepdims=True)
        dx = r * (dxhat - xhat * proj)
        dx_ref[...] = dx
        dgate_ref[...] = (dx * pv * sg * (1.0 - sg)).astype(dgate_ref.dtype)
        dpe_ref[...] = (dx * sg).astype(dpe_ref.dtype)

        @pl.when(i == 0)
        def _():
            dg_ref[...] = jnp.zeros_like(dg_ref)
            loss_ref[...] = jnp.zeros_like(loss_ref)

        dg_ref[...] += jnp.sum(dy * xhat, axis=0, keepdims=True)
        part = 0.5 * jnp.sum(jnp.mean(err * err, axis=1, keepdims=True), axis=0, keepdims=True)
        loss_ref[...] += jnp.broadcast_to(part, loss_ref.shape)

    row = pl.BlockSpec((tr, d), lambda i: (i, 0))
    vec = pl.BlockSpec((1, d), lambda i: (0, 0))
    return pl.pallas_call(
        body, name=name, grid=(s // tr,),
        in_specs=[row, row, row, vec, row],
        out_specs=[row, row, row, vec, pl.BlockSpec((1, 128), lambda i: (0, 0))],
        out_shape=[jax.ShapeDtypeStruct((s, d), F32), jax.ShapeDtypeStruct((s, d), BF16),
                   jax.ShapeDtypeStruct((s, d), BF16), jax.ShapeDtypeStruct((1, d), F32),
                   jax.ShapeDtypeStruct((1, 128), F32)],
        compiler_params=_cparams(("arbitrary",)),
    )(x2, gate, pe, g, target)


def _ew(body, ins, in_blocks, outs, out_blocks, grid, *, name, ride=None):
    return _call(body, ins, name=name, grid=grid,
                 in_specs=[pl.BlockSpec(bs, im) for bs, im in in_blocks],
                 out_specs=[pl.BlockSpec(bs, im) for bs, im in out_blocks],
                 out_shape=outs, sem=("parallel",) * len(grid), ride=ride)


def _gate_merge_fwd(gates, o_sb, o_ca, *, name, ride=None):
    s, d = o_sb.shape
    tr, tc = _row_tile(s), _pick(d, (1024, 512, 256, 128))
    nc = d // tc

    def body(gs_ref, gc_ref, os_ref, oc_ref, m_ref):
        f32 = lambda ref: ref[...].astype(F32)
        m = _sigmoid(f32(gs_ref)) * f32(os_ref) + _sigmoid(f32(gc_ref)) * f32(oc_ref)
        m_ref[...] = m.astype(m_ref.dtype)

    blk = ((tr, tc), lambda i, j: (i, j))
    return _ew(body, [gates, gates, o_sb, o_ca],
               [blk, ((tr, tc), lambda i, j: (i, j + nc)), blk, blk],
               [jax.ShapeDtypeStruct((s, d), BF16)], [blk], (s // tr, nc), name=name, ride=ride)[0]


def _gate_merge_bwd(dmerged, gates, o_sb, o_ca, *, name):
    s, d = o_sb.shape
    tr, tc = _row_tile(s), _pick(d, (1024, 512, 256, 128))
    nc = d // tc

    def body(dm_ref, gs_ref, gc_ref, os_ref, oc_ref, dgs_ref, dgc_ref, dos_ref, doc_ref):
        f32 = lambda ref: ref[...].astype(F32)
        dm = dm_ref[...]
        ss = _sigmoid(f32(gs_ref))
        sc = _sigmoid(f32(gc_ref))
        dgs_ref[...] = (dm * f32(os_ref) * ss * (1.0 - ss)).astype(dgs_ref.dtype)
        dgc_ref[...] = (dm * f32(oc_ref) * sc * (1.0 - sc)).astype(dgc_ref.dtype)
        dos_ref[...] = (dm * ss).astype(dos_ref.dtype)
        doc_ref[...] = (dm * sc).astype(doc_ref.dtype)

    blk = ((tr, tc), lambda i, j: (i, j))
    sd = jax.ShapeDtypeStruct((s, d), BF16)
    return _ew(body, [dmerged, gates, gates, o_sb, o_ca],
               [blk, blk, ((tr, tc), lambda i, j: (i, j + nc)), blk, blk],
               [sd, sd, sd, sd], [blk, blk, blk, blk], (s // tr, nc), name=name)


def _swiglu_fwd(gu, *, name, ride=None):
    s, f2 = gu.shape
    f = f2 // 2
    tr, tc = 128, _pick(f, (512, 256, 128))

    def body(gu_ref, a_ref):
        for at in range(0, f, tc):
            gv = gu_ref[:, at:at + tc].astype(F32)
            a_ref[:, at:at + tc] = (gv * _sigmoid(gv) * gu_ref[:, f + at:f + at + tc].astype(F32)).astype(a_ref.dtype)

    row = lambda i: (i, 0)
    return _ew(body, [gu], [((tr, f2), row)], [jax.ShapeDtypeStruct((s, f), BF16)], [((tr, f), row)],
               (s // tr,), name=name, ride=ride)[0]


def _swiglu_bwd(dact, gu, *, name):
    s, f2 = gu.shape
    f = f2 // 2
    tr, tc = 128, _pick(f, (512, 256, 128))

    def body(da_ref, gu_ref, o_ref):
        for at in range(0, f, tc):
            da = da_ref[:, at:at + tc].astype(F32)
            gv = gu_ref[:, at:at + tc].astype(F32)
            sg = _sigmoid(gv)
            uv = gu_ref[:, f + at:f + at + tc].astype(F32)
            o_ref[:, at:at + tc] = (da * uv * sg * (1.0 + gv * (1.0 - sg))).astype(o_ref.dtype)
            o_ref[:, f + at:f + at + tc] = (da * gv * sg).astype(o_ref.dtype)

    row = lambda i: (i, 0)
    return _ew(body, [dact, gu], [((tr, f), row), ((tr, f2), row)], [jax.ShapeDtypeStruct((s, f2), BF16)],
               [((tr, f2), row)], (s // tr,), name=name)[0]


def _concat_cols(parts, *, name):
    s = parts[0].shape[0]
    widths = [p.shape[1] for p in parts]
    tr = 256

    def body(*refs):
        o_ref, at = refs[-1], 0
        for p_ref, width in zip(refs, widths):
            o_ref[:, at:at + width] = p_ref[...]
            at += width

    row = lambda i: (i, 0)
    return _ew(body, list(parts), [((tr, width), row) for width in widths],
               [jax.ShapeDtypeStruct((s, sum(widths)), parts[0].dtype)], [((tr, sum(widths)), row)],
               (s // tr,), name=name)[0]


def _sb_tri(later):
    row = lax.broadcasted_iota(jnp.int32, (SB_BLOCK, SB_BLOCK), 0)
    col = lax.broadcasted_iota(jnp.int32, (SB_BLOCK, SB_BLOCK), 1)
    tri = (row > col) if later else (row < col)
    return jnp.concatenate([tri.astype(BF16), jnp.ones((SB_BLOCK, SB_BLOCK), BF16)], axis=1)


def _sb_valid(i, j, own):
    if not own:
        return None
    qi = i * SB_ROWS + lax.broadcasted_iota(jnp.int32, (SB_ROWS, SB_KEYS), 0)
    ki = j * SB_KEYS + lax.broadcasted_iota(jnp.int32, (SB_ROWS, SB_KEYS), 1)
    return ki < qi


def _sb_scan(v, tri, run, later):
    hi = v.astype(BF16)
    lo = (v - hi.astype(F32)).astype(BF16)
    outs = [None] * SB_GROUPS
    for b in (reversed(range(SB_GROUPS)) if later else range(SB_GROUPS)):
        cols = slice(b * SB_BLOCK, (b + 1) * SB_BLOCK)
        r = _dot_nn(hi[:, cols], tri) + _dot_nn(lo[:, cols], tri)
        outs[b] = r[:, :SB_BLOCK] + run
        run = run + r[:, SB_BLOCK:]
    return jnp.concatenate(outs, axis=1), run


def _masked(valid, v):
    return v if valid is None else jnp.where(valid, v, 0.0)


def _sb_scores(q, kj, scale, valid):
    z = _dot_nt(q, kj) * scale
    t = jnp.log(1.0 + jnp.exp(-jnp.abs(z)))
    return jnp.minimum(z, 0.0) - t, _masked(valid, -jnp.maximum(z, 0.0) - t)


def _sb_specs(h_count, s, col0):
    q_spec = pl.BlockSpec((SB_ROWS, HEAD_DIM), lambda h, i: (i, col0 + h))
    k_spec = pl.BlockSpec((s, HEAD_DIM), lambda h, i: (0, col0 + h_count + h))
    v_spec = pl.BlockSpec((s, HEAD_DIM), lambda h, i: (0, col0 + 2 * h_count + h))
    return q_spec, k_spec, v_spec


def _sb_fwd(qkv, n_heads, col0, *, name, ride=None):
    s = qkv.shape[0]
    nq = s // SB_ROWS
    scale = HEAD_DIM ** -0.5

    def body(q_ref, k_ref, v_ref, o_ref):
        i = pl.program_id(1)
        q = q_ref[...]
        tri = _sb_tri(later=True)

        def step(j, carry, own):
            run, acc = carry
            off = pl.multiple_of(j * SB_KEYS, SB_KEYS)
            valid = _sb_valid(i, j, own)
            ls, lk = _sb_scores(q, k_ref[pl.ds(off, SB_KEYS), :], scale, valid)
            between, run = _sb_scan(lk, tri, run, later=True)
            a = _masked(valid, jnp.exp(ls + between))
            return run, acc + _dot_nn(a.astype(BF16), v_ref[pl.ds(off, SB_KEYS), :])

        carry = step(i, (jnp.zeros((SB_ROWS, SB_BLOCK), F32), jnp.zeros((SB_ROWS, HEAD_DIM), F32)), True)
        _, acc = lax.fori_loop(0, i, lambda jj, c: step(i - 1 - jj, c, False), carry)
        o_ref[...] = acc.astype(o_ref.dtype)

    q_spec, k_spec, v_spec = _sb_specs(n_heads, s, col0)
    return _call(
        body, [qkv, qkv, qkv], name=name, grid=(n_heads, nq),
        in_specs=[q_spec, k_spec, v_spec],
        out_specs=[pl.BlockSpec((SB_ROWS, HEAD_DIM), lambda h, i: (i, h))],
        out_shape=[jax.ShapeDtypeStruct((s, n_heads * HEAD_DIM), BF16)],
        sem=("parallel", "arbitrary"), ride=ride)[0]


def _sb_bwd(qkv, dy, n_heads, col0, *, name, ride=None):
    s = qkv.shape[0]
    nq = s // SB_ROWS
    scale = HEAD_DIM ** -0.5

    def body(q_ref, k_ref, v_ref, dy_ref, dq_ref, dk_ref, dv_ref, e_scr, sg_scr, dk_acc, dv_acc):
        i = pl.program_id(1)
        q = q_ref[...]
        dyv = dy_ref[...]

        @pl.when(i == 0)
        def _():
            dk_acc[...] = jnp.zeros_like(dk_acc)
            dv_acc[...] = jnp.zeros_like(dv_acc)

        tri_later = _sb_tri(later=True)

        def pass1(j, run, own):
            off = pl.multiple_of(j * SB_KEYS, SB_KEYS)
            valid = _sb_valid(i, j, own)
            ls, lk = _sb_scores(q, k_ref[pl.ds(off, SB_KEYS), :], scale, valid)
            between, run = _sb_scan(lk, tri_later, run, later=True)
            a = _masked(valid, jnp.exp(ls + between))
            e_scr[j] = a * _dot_nt(dyv, v_ref[pl.ds(off, SB_KEYS), :])
            sg_scr[j] = jnp.exp(ls)
            dv_acc[pl.ds(off, SB_KEYS), :] += _dot_tn(a.astype(BF16), dyv)
            return run

        lax.fori_loop(0, i, lambda jj, run: pass1(i - 1 - jj, run, False),
                      pass1(i, jnp.zeros((SB_ROWS, SB_BLOCK), F32), True))

        tri_earlier = _sb_tri(later=False)

        def pass2(j, carry, own):
            run, dq = carry
            off = pl.multiple_of(j * SB_KEYS, SB_KEYS)
            kj = k_ref[pl.ds(off, SB_KEYS), :]
            sg = sg_scr[j]
            e = e_scr[j]
            before, run = _sb_scan(e, tri_earlier, run, later=False)
            dz = _masked(_sb_valid(i, j, own), e * (1.0 - sg) - sg * before) * scale
            dzb = dz.astype(BF16)
            dk_acc[pl.ds(off, SB_KEYS), :] += _dot_tn(dzb, q)
            return run, dq + _dot_nn(dzb, kj)

        init = (jnp.zeros((SB_ROWS, SB_BLOCK), F32), jnp.zeros((SB_ROWS, HEAD_DIM), F32))
        _, dq = pass2(i, lax.fori_loop(0, i, lambda j, c: pass2(j, c, False), init), True)
        dq_ref[...] = dq.astype(dq_ref.dtype)

        @pl.when(i == nq - 1)
        def _():
            dk_ref[...] = dk_acc[...].astype(dk_ref.dtype)
            dv_ref[...] = dv_acc[...].astype(dv_ref.dtype)

    q_spec, k_spec, v_spec = _sb_specs(n_heads, s, col0)
    blk = pl.BlockSpec((SB_ROWS, HEAD_DIM), lambda h, i: (i, h))
    full = pl.BlockSpec((s, HEAD_DIM), lambda h, i: (0, h))
    sd = jax.ShapeDtypeStruct((s, n_heads * HEAD_DIM), BF16)
    return _call(
        body, [qkv, qkv, qkv, dy], name=name, grid=(n_heads, nq),
        in_specs=[q_spec, k_spec, v_spec, blk],
        out_specs=[blk, full, full],
        out_shape=[sd, sd, sd],
        scratch_shapes=[pltpu.VMEM((s // SB_KEYS, SB_ROWS, SB_KEYS), F32), pltpu.VMEM((s // SB_KEYS, SB_ROWS, SB_KEYS), F32),
                        pltpu.VMEM((s, HEAD_DIM), F32), pltpu.VMEM((s, HEAD_DIM), F32)],
        sem=("parallel", "arbitrary"), ride=ride)


def _band_bias(rel_bias):
    h = rel_bias.shape[0]
    width = BAND + CHUNK
    first = width - 1 - N_REL
    line = jnp.concatenate([jnp.broadcast_to(rel_bias[:, :1], (h, first)), rel_bias], axis=1)
    tiled = jnp.broadcast_to(line[:, None, :], (h, CHUNK, width - 1)).reshape(h, CHUNK * (width - 1))
    skew = jnp.pad(tiled, ((0, 0), (0, CHUNK))).reshape(h, CHUNK, width)[:, ::-1, :BAND]
    seen = jnp.arange(BAND) >= CHUNK
    return jnp.where(seen[None, None, :], skew, NEG)


def _band_bias_grad(dbias):
    h = dbias.shape[0]
    width = BAND + CHUNK
    flipped = jnp.pad(dbias[:, ::-1, :], ((0, 0), (0, 0), (0, CHUNK)))
    skew = flipped.reshape(h, CHUNK * width)[:, :CHUNK * (width - 1)].reshape(h, CHUNK, width - 1)
    diag = jnp.sum(skew, axis=1)
    first = width - 1 - N_REL
    clipped = jnp.sum(diag[:, :first + 1], axis=1, keepdims=True)
    return jnp.concatenate([clipped, diag[:, first + 1:]], axis=1)


def _group_bias(band):
    return jnp.concatenate([jnp.pad(band, ((0, 0), (0, 0), ((u + 1) * CHUNK, (CA_PER_STEP - 1 - u) * CHUNK)),
                                    constant_values=NEG) for u in range(CA_PER_STEP)], axis=1)


def _group_bias_grad(dgroup):
    return sum(dgroup[:, u * CHUNK:(u + 1) * CHUNK, (u + 1) * CHUNK:(u + 1) * CHUNK + BAND] for u in range(CA_PER_STEP))


def _ca_load_padded(k_ref, v_ref, kp, vp, s):
    kp[pl.ds(0, CA_PAD), :] = jnp.zeros((CA_PAD, HEAD_DIM), kp.dtype)
    vp[pl.ds(0, CA_PAD), :] = jnp.zeros((CA_PAD, HEAD_DIM), vp.dtype)
    kp[pl.ds(CA_PAD, s), :] = k_ref[...]
    vp[pl.ds(CA_PAD, s), :] = v_ref[...]


def _ca_weights(q, kb, bias, off, scale):
    z = _dot_nt(q, kb) * scale + bias
    pos = off + lax.broadcasted_iota(jnp.int32, (CA_ROWS, CA_BAND), 1)
    z = jnp.where(pos >= CA_PAD, z, NEG)
    p = jnp.exp(z - jnp.max(z, axis=1, keepdims=True))
    return p / jnp.sum(p, axis=1, keepdims=True)


def _ca_specs(h_count, s, col0):
    q_spec = pl.BlockSpec((CA_ROWS, HEAD_DIM), lambda h, c: (c, col0 + h))
    k_spec = pl.BlockSpec((s, HEAD_DIM), lambda h, c: (0, col0 + h_count + h))
    v_spec = pl.BlockSpec((s, HEAD_DIM), lambda h, c: (0, col0 + 2 * h_count + h))
    b_spec = pl.BlockSpec((1, CA_ROWS, CA_BAND), lambda h, c: (h, 0, 0))
    return q_spec, k_spec, v_spec, b_spec


def _ca_fwd(qkv, bias, n_heads, col0, *, name, ride=None):
    s = qkv.shape[0]
    nc = s // CA_ROWS
    scale = HEAD_DIM ** -0.5

    def body(q_ref, k_ref, v_ref, b_ref, o_ref, kp, vp):
        c = pl.program_id(1)

        @pl.when(c == 0)
        def _():
            _ca_load_padded(k_ref, v_ref, kp, vp, s)

        off = pl.multiple_of(c * CA_ROWS, CA_ROWS)
        w = _ca_weights(q_ref[...], kp[pl.ds(off, CA_BAND), :], b_ref[0], off, scale)
        o_ref[...] = _dot_nn(w.astype(BF16), vp[pl.ds(off, CA_BAND), :]).astype(o_ref.dtype)

    q_spec, k_spec, v_spec, b_spec = _ca_specs(n_heads, s, col0)
    return _call(
        body, [qkv, qkv, qkv, bias], name=name, grid=(n_heads, nc),
        in_specs=[q_spec, k_spec, v_spec, b_spec],
        out_specs=[pl.BlockSpec((CA_ROWS, HEAD_DIM), lambda h, c: (c, h))],
        out_shape=[jax.ShapeDtypeStruct((s, n_heads * HEAD_DIM), BF16)],
        scratch_shapes=[pltpu.VMEM((s + CA_PAD, HEAD_DIM), BF16), pltpu.VMEM((s + CA_PAD, HEAD_DIM), BF16)],
        sem=("parallel", "arbitrary"), ride=ride)[0]


def _ca_bwd(qkv, bias, dy, n_heads, col0, *, name, ride=None):
    s = qkv.shape[0]
    nc = s // CA_ROWS
    scale = HEAD_DIM ** -0.5

    def body(q_ref, k_ref, v_ref, b_ref, dy_ref, dq_ref, dk_ref, dv_ref, db_ref, kp, vp, dkp, dvp):
        c = pl.program_id(1)

        @pl.when(c == 0)
        def _():
            _ca_load_padded(k_ref, v_ref, kp, vp, s)
            dkp[...] = jnp.zeros_like(dkp)
            dvp[...] = jnp.zeros_like(dvp)
            db_ref[...] = jnp.zeros_like(db_ref)

        off = pl.multiple_of(c * CA_ROWS, CA_ROWS)
        band = pl.ds(off, CA_BAND)
        q = q_ref[...]
        dyv = dy_ref[...]
        kb = kp[band, :]
        w = _ca_weights(q, kb, b_ref[0], off, scale)
        dw = _dot_nt(dyv, vp[band, :])
        dvp[band, :] += _dot_tn(w.astype(BF16), dyv)
        dz = w * (dw - jnp.sum(w * dw, axis=1, keepdims=True))
        db_ref[0] += dz
        dzs = (dz * scale).astype(BF16)
        dq_ref[...] = _dot_nn(dzs, kb).astype(dq_ref.dtype)
        dkp[band, :] += _dot_tn(dzs, q)

        @pl.when(c == nc - 1)
        def _():
            dk_ref[...] = dkp[pl.ds(CA_PAD, s), :].astype(dk_ref.dtype)
            dv_ref[...] = dvp[pl.ds(CA_PAD, s), :].astype(dv_ref.dtype)

    q_spec, k_spec, v_spec, b_spec = _ca_specs(n_heads, s, col0)
    blk = pl.BlockSpec((CA_ROWS, HEAD_DIM), lambda h, c: (c, h))
    full = pl.BlockSpec((s, HEAD_DIM), lambda h, c: (0, h))
    sd = jax.ShapeDtypeStruct((s, n_heads * HEAD_DIM), BF16)
    return _call(
        body, [qkv, qkv, qkv, bias, dy], name=name, grid=(n_heads, nc),
        in_specs=[q_spec, k_spec, v_spec, b_spec, blk],
        out_specs=[blk, full, full, b_spec],
        out_shape=[sd, sd, sd, jax.ShapeDtypeStruct((n_heads, CA_ROWS, CA_BAND), F32)],
        scratch_shapes=[pltpu.VMEM((s + CA_PAD, HEAD_DIM), BF16), pltpu.VMEM((s + CA_PAD, HEAD_DIM), BF16),
                        pltpu.VMEM((s + CA_PAD, HEAD_DIM), F32), pltpu.VMEM((s + CA_PAD, HEAD_DIM), F32)],
        sem=("parallel", "arbitrary"), ride=ride)


EARLY = ("w_sb_out", "w_ca_out", "w_mix_out")


def _step(x, p, target, small, comm):
    w = comm.w
    d = x.shape[1]
    n_sb = w["w_sb_out"].shape[0] // HEAD_DIM
    n_ca = w["w_ca_out"].shape[0] // HEAD_DIM
    qkv_cols = 3 * HEAD_DIM * (n_sb + n_ca)
    ca_col0 = 3 * n_sb
    both = (F32, BF16)

    h1 = _rms_fwd(x, small["g_mix"], name="rms_mix")
    ffn, ple = ("w_ffn_in",), ("w_ple_gate", "w_ple_in")
    qkv = _mm(h1, w["w_in"], "nn", (BF16,), name="proj_qkv", n=qkv_cols, ride=comm.gather(EARLY, "near"))
    gates = _mm(h1, w["w_in"], "nn", (BF16,), name="proj_gates", n=2 * d, b_col_off=qkv_cols,
                ride=comm.gather(ffn, "near", comm.gather(EARLY, "far"), (0, 8)))
    bias = _group_bias(_band_bias(small["rel_bias"]))
    y_sb = _sb_fwd(qkv, n_sb, 0, name="sb_fwd", ride=comm.gather(ffn, "near", comm.gather(EARLY, "pair"), (1, 8, 7)))
    y_ca = _ca_fwd(qkv, bias, n_ca, ca_col0, name="ca_fwd", ride=comm.gather(ffn, "far"))
    out = ("w_ffn_out",)
    o_sb = _mm(y_sb, w["w_sb_out"], "nn", (BF16,), name="sb_out", ride=comm.gather(out, "near", part=(0, 4)))
    o_ca = _mm(y_ca, w["w_ca_out"], "nn", (BF16,), name="ca_out", ride=comm.gather(out, "near", part=(1, 4)))
    merged = _gate_merge_fwd(gates, o_sb, o_ca, name="gate_merge", ride=comm.gather(out, "near", part=(2, 4)))
    x1 = _mm(merged, w["w_mix_out"], "nn", (F32,), name="mix_out", resid=x,
             ride=comm.gather(out, "near", comm.gather(ffn, "pair"), (3, 4)))
    h2 = _rms_fwd(x1, small["g_ffn"], name="rms_ffn")
    gu = _mm(h2, w["w_ffn_in"], "nn", (BF16,), name="ffn_in", ride=comm.gather(ple, "near", comm.gather(out, "far")))
    act = _swiglu_fwd(gu, name="swiglu", ride=comm.gather(ple, "far", comm.gather(out, "pair")))
    x2 = _mm(act, w["w_ffn_out"], "nn", (F32,), name="ffn_out", resid=x1, ride=comm.gather(ple, "pair"))
    h3 = _rms_fwd(x2, small["g_ple"], name="rms_ple")
    t = _mm(h3, w["w_ple_gate"], "nn", (BF16,), name="ple_gate")
    pe = _mm(p, w["w_ple_in"], "nn", (BF16,), name="ple_in")

    def halves(n, acts, dout, ride, name):
        if comm.pos is None:
            return comm.grad(n, *_mm(acts, dout, "tn", both, name=name))
        g16 = _mm(acts, dout, "tn", (BF16,), name=name + "_other", m_half=(False, comm.pos), ride=ride)
        comm.grad(n, None, g16, half=True)
        g32 = _mm(acts, dout, "tn", (F32,), name=name + "_own", m_half=(True, comm.pos), ride=comm.pair((n,)))
        comm.grad(n, g32, g16, half=True)

    gs = {}
    dx3, dt, dpe, gs["g_final"], loss = _ple_loss(x2, t, pe, small["g_final"], target, name="ple_loss")
    comm.grad("w_ple_in", *_mm(p, dpe, "tn", both, name="dw_ple_in"))
    comm.grad("w_ple_gate", *_mm(h3, dt, "tn", both, name="dw_ple_gate"))
    ple = ("w_ple_in", "w_ple_gate")
    dh3 = _mm(dt, w["w_ple_gate"], "nt", (F32,), name="dh_ple", ride=comm.pair(ple))
    dx2, dx2_16, gs["g_ple"] = _rms_bwd(x2, small["g_ple"], dh3, dx3, name="rms_ple_bwd")
    comm.add(ple)
    comm.grad("w_ffn_out", *_mm(act, dx2_16, "tn", both, name="dw_ffn_out", ride=comm.chips(ple)))
    dact = _mm(dx2_16, w["w_ffn_out"], "nt", (BF16,), name="dact", ride=comm.pair(("w_ffn_out",)))
    dgu = _swiglu_bwd(dact, gu, name="swiglu_bwd")
    comm.sum(ple)
    comm.add(("w_ffn_out",))
    comm.grad("w_ffn_in", *_mm(h2, dgu, "tn", both, name="dw_ffn_in",
                               ride=comm.share(ple, comm.chips(("w_ffn_out",)))))
    dh2 = _mm(dgu, w["w_ffn_in"], "nt", (F32,), name="dh_ffn", ride=comm.pair(("w_ffn_in",)))
    dx1, dx1_16, gs["g_ffn"] = _rms_bwd(x1, small["g_ffn"], dh2, dx2, name="rms_ffn_bwd")
    comm.add(("w_ffn_in",))
    comm.sum(("w_ffn_out",))
    comm.grad("w_mix_out", *_mm(merged, dx1_16, "tn", both, name="dw_mix_out", ride=comm.share(("w_ffn_out",))))
    dmerged = _mm(dx1_16, w["w_mix_out"], "nt", (F32,), name="dmerged", ride=comm.pair(("w_mix_out",)))
    dg_sb, dg_ca, do_sb, do_ca = _gate_merge_bwd(dmerged, gates, o_sb, o_ca, name="gate_merge_bwd")
    comm.add(("w_mix_out",))
    comm.grad("w_sb_out", *_mm(y_sb, do_sb, "tn", both, name="dw_sb_out"))
    comm.grad("w_ca_out", *_mm(y_ca, do_ca, "tn", both, name="dw_ca_out"))
    outs = ("w_sb_out", "w_ca_out")
    dy_sb = _mm(do_sb, w["w_sb_out"], "nt", (BF16,), name="dy_sb", ride=comm.pair(outs))
    dy_ca = _mm(do_ca, w["w_ca_out"], "nt", (BF16,), name="dy_ca")
    comm.add(outs)
    dq_sb, dk_sb, dv_sb = _sb_bwd(qkv, dy_sb, n_sb, 0, name="sb_bwd", ride=comm.chips(("w_ffn_in",)))
    comm.sum(("w_ffn_in",))
    late = ("w_mix_out",) + outs
    dq_ca, dk_ca, dv_ca, dbias = _ca_bwd(qkv, bias, dy_ca, n_ca, ca_col0, name="ca_bwd",
                                         ride=comm.chips(late, comm.share(("w_ffn_in",))))
    comm.sum(late)
    gs["rel_bias"] = _band_bias_grad(_group_bias_grad(dbias))
    dproj = _concat_cols([dq_sb, dk_sb, dv_sb, dq_ca, dk_ca, dv_ca, dg_sb, dg_ca], name="dproj")
    halves("w_in", h1, dproj, comm.share(late), "dw_in")
    comm.add(("w_in",))
    half = x.shape[0] // 2
    dh1 = _mm(dproj, w["w_in"], "nt", (F32,), name="dh_mix_top", rows=(0, half), ride=comm.tail(TAIL_SECOND))
    dh1 = _mm(dproj, w["w_in"], "nt", (F32,), name="dh_mix_bottom", rows=(half, half), onto=(dh1,),
              ride=comm.tail(TAIL_SECOND))
    grad_x, _, gs["g_mix"] = _rms_bwd(x, small["g_mix"], dh1, dx1, name="rms_mix_bwd", ride=comm.tail(TAIL_FIRST))
    return loss, grad_x, gs


def _position():
    x, y, c = lax.axis_index("x"), lax.axis_index("y"), lax.axis_index("c")
    chips = [(1 - x, y), (x, 1 - y), (1 - x, 1 - y)]
    return x, y, c, chips


def _aligned(v, m):
    return v if isinstance(v, int) else pl.multiple_of(v, m)


def _piece_dims(shape, axis):
    k, n = shape
    return (k // 2, n // N_CHIPS) if axis == 1 else (k // N_CHIPS // 2, n)


def _piece(ref, shape, axis, j, h, part=(0, 1)):
    pr, pc = _piece_dims(shape, axis)
    nr = pr // part[1] * (part[2] if len(part) > 2 else 1)
    r0 = part[0] * (pr // part[1])
    if axis == 1:
        return ref.at[pl.ds(_aligned(h * pr + r0, 16), nr), pl.ds(_aligned(j * pc, 128), pc)]
    return ref.at[pl.ds(_aligned((2 * j + h) * pr + r0, 16), nr), :]


def _shard_half(ref, h):
    rows = ref.shape[0] // 2
    return ref.at[pl.ds(_aligned(h * rows, 16), rows), :]


def _remote(src, dst, send_sems, recv_sems, k, to):
    return pltpu.make_async_remote_copy(src_ref=src, dst_ref=dst, send_sem=send_sems.at[k],
                                        recv_sem=recv_sems.at[k], device_id=to, device_id_type=MESH)


def _prefetch_call(body, scalars, ins, in_specs, out_shape, out_specs, grid, *, name, ride=None):
    single = not isinstance(out_shape, (list, tuple))
    outs = _call(body, ins, name=name, grid=grid, in_specs=in_specs,
                 out_specs=[out_specs] if single else out_specs, out_shape=[out_shape] if single else out_shape,
                 sem=("parallel",) * len(grid), ride=ride, scalars=scalars)
    return outs[0] if single else outs


def _slab_tiles(pr, pc):
    tc = pc if pc <= 4096 else _pick(pc, (2048, 1024, 512, 256, 128))
    tr = next(t for t in (1024, 512, 256, 128, 64, 32, 16) if pr % t == 0 and t * tc <= 512 * 1024)
    return tr, tc


def _cast_place(w, axis, pos, *, name, ride=None):
    ks, ns = w.shape
    shape = (ks, ns * N_CHIPS) if axis == 1 else (ks * N_CHIPS, ns)
    tr, tc = _slab_tiles(ks, ns)
    nr, nc = ks // tr, ns // tc

    def body(pos_ref, w_ref, o_ref):
        o_ref[...] = w_ref[...].astype(o_ref.dtype)

    if axis == 1:
        out_map = lambda i, j, pos_ref: (i, pos_ref[0] * nc + j)
    else:
        out_map = lambda i, j, pos_ref: (pos_ref[0] * nr + i, j)
    return _prefetch_call(body, pos, [w], [pl.BlockSpec((tr, tc), lambda i, j, pos_ref: (i, j))],
                          jax.ShapeDtypeStruct(shape, BF16), pl.BlockSpec((tr, tc), out_map), (nr, nc), name=name, ride=ride)


def _run(ride, *, name):
    if ride is None:
        return

    def body(o_ref):
        o_ref[...] = jnp.zeros_like(o_ref)

    _call(body, [], name=name, grid=(1,), in_specs=[], out_specs=[pl.BlockSpec((8, 128), lambda i: (0, 0))],
          out_shape=[jax.ShapeDtypeStruct((8, 128), F32)], ride=ride)


def _ride_gather(ride, w, n, axis, stage, part=(0, 1)):
    shape = w[n].shape
    piece = functools.partial(_piece, shape=shape, axis=axis)
    span = part[2] if len(part) > 2 else 1
    halves = [(2 * part[0] + t * span, 2 * part[1], span) for t in range(2)]

    def copies(ins, outs, send_sems, recv_sems, arriving):
        x, y, c, chips = _position()
        me, (xn, yn, dn) = 2 * x + y, [2 * px + py for px, py in chips]
        if stage == "near":
            plan = [(me, c, part, (1 - x, y, c), xn, c, part), (me, c, part, (x, 1 - y, c), yn, c, part)]
        else:
            plan = []
        if stage == "far":
            plan = [(yn, c, halves[1], (1 - x, y, c), dn, c, halves[1]), (xn, c, halves[0], (x, 1 - y, c), dn, c, halves[0])]
        to_sibling = {"far": (xn, yn), "pair": (dn,)}.get(stage, ())
        plan += [(j, c, part, (x, y, 1 - c), j, 1 - c, part) for j in to_sibling]
        out = []
        for k, (chip, h, rows, to, from_chip, from_h, from_rows) in enumerate(plan):
            if arriving:
                lands = piece(outs[0], j=from_chip, h=from_h, part=from_rows)
                out.append(_remote(lands, lands, send_sems, recv_sems, k, to))
            else:
                out.append(_remote(piece(ins[0], j=chip, h=h, part=rows), piece(outs[0], j=chip, h=h, part=rows),
                                   send_sems, recv_sems, k, to))
        return out

    def start(*refs):
        for cp in copies(*refs, arriving=False):
            cp.start()

    def finish(*refs):
        for cp in copies(*refs, arriving=True):
            cp.wait_recv()
        for cp in copies(*refs, arriving=False):
            cp.wait_send()

    ride.add([w[n]], [jax.ShapeDtypeStruct(shape, w[n].dtype)], {0: 0}, 4, start, finish,
             lambda outs: w.__setitem__(n, outs[0]))


def _ride_pair(ride, st, axis):
    shape = st["g16"].shape
    pr, pc = (shape[0], shape[1] // N_CHIPS) if st.get("half") else _piece_dims(shape, axis)

    def copies(ins, outs, send_sems, recv_sems):
        x, y, c, _ = _position()
        if st.get("half"):
            pieces = [ins[0].at[:, pl.ds(j * pc, pc)] for j in range(N_CHIPS)]
        else:
            pieces = [_piece(ins[0], shape, axis, j, 1 - c) for j in range(N_CHIPS)]
        return [_remote(pieces[j], outs[0].at[j], send_sems, recv_sems, j, (x, y, 1 - c)) for j in range(N_CHIPS)]

    def start(*refs):
        for cp in copies(*refs):
            cp.start()

    def finish(*refs):
        for cp in copies(*refs):
            cp.wait()

    ride.add([st["g16"]], [jax.ShapeDtypeStruct((N_CHIPS, pr, pc), BF16)], {}, N_CHIPS, start, finish,
             lambda outs: st.__setitem__("sib", outs[0]))


def _ride_chips(ride, st, rows=None):
    _, pr, pc = st["s16"].shape
    r0, nr = (0, pr) if rows is None else rows

    def copies(ins, outs, send_sems, recv_sems):
        x, y, c, chips = _position()
        return [_remote(ins[0].at[2 * px + py, pl.ds(r0, nr), :], outs[0].at[k, pl.ds(r0, nr), :],
                        send_sems, recv_sems, k, (px, py, c)) for k, (px, py) in enumerate(chips)]

    def start(*refs):
        for cp in copies(*refs):
            cp.start()

    def finish(*refs):
        for cp in copies(*refs):
            cp.wait()

    ins, aliases = ([st["s16"], st["recv"]], {1: 0}) if "recv" in st else ([st["s16"]], {})
    ride.add(ins, [jax.ShapeDtypeStruct((3, pr, pc), BF16)], aliases, 3, start, finish,
             lambda outs: st.__setitem__("recv", outs[0]))


def _ride_share(ride, st):
    def sent(ins, outs, send_sems, recv_sems):
        x, y, c, _ = _position()
        return _remote(_shard_half(ins[0], c), _shard_half(outs[0], c), send_sems, recv_sems, 0, (x, y, 1 - c))

    def landed(ins, outs, send_sems, recv_sems):
        x, y, c, _ = _position()
        other = _shard_half(outs[0], 1 - c)
        return _remote(other, other, send_sems, recv_sems, 0, (x, y, 1 - c))

    def start(*refs):
        sent(*refs).start()

    def finish(*refs):
        landed(*refs).wait_recv()
        sent(*refs).wait_send()

    ride.add([st["shard"]], [jax.ShapeDtypeStruct(st["shard"].shape, F32)], {0: 0}, 1, start, finish,
             lambda outs: st.__setitem__("g", outs[0]))


def _piece_block(axis, nr, nc, chip, half=False):
    if half:
        return lambda *a: (a[-3], chip(a) * nc + a[-2])
    if axis == 1:
        return lambda *a: ((a[-1][1] * nr + a[-3]), chip(a) * nc + a[-2])
    return lambda *a: ((2 * chip(a) + a[-1][1]) * nr + a[-3], a[-2])


def _pair_add(g32, sib, axis, pos, *, name, half=False):
    _, pr, pc = sib.shape
    tr, tc = _slab_tiles(pr, pc)
    nr, nc = pr // tr, pc // tc

    def body(pos_ref, g_ref, b_ref, o16_ref):
        o16_ref[0] = (g_ref[...] + b_ref[0].astype(F32)).astype(o16_ref.dtype)

    other = lambda a: (a[-1][0] + 1 + a[0]) % N_CHIPS
    blk = pl.BlockSpec((1, tr, tc), lambda *a: (other(a), a[1], a[2]))
    return _prefetch_call(body, pos, [g32, sib], [pl.BlockSpec((tr, tc), _piece_block(axis, nr, nc, other, half)), blk],
                          jax.ShapeDtypeStruct(sib.shape, BF16), blk, (N_CHIPS - 1, nr, nc), name=name)


def _chip_sum(g32, sib, recv, axis, pos, *, name, half=False):
    _, pr, pc = sib.shape
    tr, tc = _slab_tiles(pr, pc)
    nr, nc = pr // tr, pc // tc

    def body(pos_ref, g_ref, b_ref, r_ref, o_ref):
        pair = g_ref[...] + b_ref[0].astype(F32)
        o_ref[...] = ((pair + r_ref[0].astype(F32)) + r_ref[1].astype(F32)) + r_ref[2].astype(F32)

    return _prefetch_call(
        body, pos, [g32, sib, recv],
        [pl.BlockSpec((tr, tc), _piece_block(axis, nr, nc, lambda a: a[-1][0], half)),
         pl.BlockSpec((1, tr, tc), lambda i, k, pos_ref: (pos_ref[0], i, k)),
         pl.BlockSpec((3, tr, tc), lambda i, k, pos_ref: (0, i, k))],
        jax.ShapeDtypeStruct((2 * pr, pc), F32),
        pl.BlockSpec((tr, tc), lambda i, k, pos_ref: (pos_ref[1] * nr + i, k)), (nr, nc), name=name)


class _Comm:
    def __init__(self, pos, w):
        self.pos, self.w, self.st = pos, w, {n: {} for n, _ in BIG}

    def gather(self, names, stage, ride=None, part=(0, 1)):
        ride = _Ride() if ride is None else ride
        for n in names:
            _ride_gather(ride, self.w, n, AXIS[n], stage, part)
        return ride

    def grad(self, n, g32, g16, half=False):
        self.st[n].update(g32=g32, g16=g16, half=half)

    def pair(self, names, ride=None):
        ride = _Ride() if ride is None else ride
        for n in names:
            _ride_pair(ride, self.st[n], AXIS[n])
        return ride

    def add(self, names):
        for n in names:
            st = self.st[n]
            st["s16"] = _pair_add(st["g32"], st["sib"], AXIS[n], self.pos, name="rs_add_" + n, half=st["half"])

    def chips(self, names, ride=None, rows=None):
        ride = _Ride() if ride is None else ride
        for n in names:
            _ride_chips(ride, self.st[n], rows)
        return ride

    def sum(self, names):
        for n in names:
            st = self.st[n]
            st["shard"] = _chip_sum(st["g32"], st["sib"], st["recv"], AXIS[n], self.pos, name="rs_sum_" + n,
                                    half=st["half"])

    def share(self, names, ride=None):
        ride = _Ride() if ride is None else ride
        for n in names:
            _ride_share(ride, self.st[n])
        return ride

    def tail(self, count):
        st = self.st["w_in"]
        rows, at = st["s16"].shape[1], st.get("at", 0)
        st["at"] = at + count
        return self.chips(("w_in",), rows=(at * rows // TAIL_PARTS, count * rows // TAIL_PARTS))

    def tail_rest(self):
        return self.tail(TAIL_PARTS - self.st["w_in"].get("at", 0))

    def result(self, n):
        return self.st[n]["g"]


class _NoComm:
    pos = None

    def __init__(self, w):
        self.w, self.st = w, {}

    def grad(self, n, g32, g16, half=False):
        self.st[n] = (g32, g16)

    def result(self, n):
        return self.st[n]

    def add(self, names):
        pass

    sum = add

    def gather(self, names, *args, **kwargs):
        return None

    pair = chips = share = tail = gather


def _small_all_reduce(vec, *, name):
    r = vec.shape[0]

    def body(vec_ref, out_ref, slots, send_sems, recv_sems):
        x, y, c, _ = _position()
        me = 4 * x + 2 * y + c
        slots[me] = vec_ref[...]
        sends = []
        for k in range(1, 8):
            to = (x ^ (k >> 2), y ^ ((k >> 1) & 1), c ^ (k & 1))
            cp = _remote(slots.at[me], slots.at[me], send_sems, recv_sems, k - 1, to)
            cp.start()
            sends.append(cp)
        for k in range(1, 8):
            frm = 4 * (x ^ (k >> 2)) + 2 * (y ^ ((k >> 1) & 1)) + (c ^ (k & 1))
            _remote(slots.at[frm], slots.at[frm], send_sems, recv_sems, k - 1, (x, y, c)).wait_recv()
        for cp in sends:
            cp.wait_send()
        total = slots[0]
        for d in range(1, 8):
            total = total + slots[d]
        out_ref[...] = total

    return pl.pallas_call(
        body, name=name,
        in_specs=[pl.BlockSpec(memory_space=pltpu.VMEM)], out_specs=pl.BlockSpec(memory_space=pltpu.VMEM),
        out_shape=jax.ShapeDtypeStruct((r, 128), F32),
        scratch_shapes=[pltpu.VMEM((8, r, 128), F32), pltpu.SemaphoreType.DMA((7,)), pltpu.SemaphoreType.DMA((7,))],
    )(vec)


SC_TILES = 32
SC_LANES = 16
SC_TILE_BUDGET = 400 * 1024


def _adamw_update(wv, gv, mv, vv):
    nm = ADAM_B1 * mv + (1.0 - ADAM_B1) * gv
    nv = ADAM_B2 * vv + (1.0 - ADAM_B2) * (gv * gv)
    m_hat = nm / (1.0 - ADAM_B1 ** ADAM_STEP)
    v_hat = nv / (1.0 - ADAM_B2 ** ADAM_STEP)
    return -ADAM_LR * (m_hat / (jnp.sqrt(v_hat) + ADAM_EPS) + ADAM_WD * wv), nm, nv


def _adamw_sc(w, g, m, v, *, name):
    r, c = w.shape
    groups = r // 8
    per_tile = -(-groups // SC_TILES)
    cb = c if 4 * 8 * c * 4 <= SC_TILE_BUDGET else _pick(c, (2048, 1024, 512, 256, 128))

    def body(w_hbm, g_hbm, m_hbm, v_hbm, go_hbm, d_hbm, nm_hbm, nv_hbm, wb, gb, mb, vb):
        tile = lax.axis_index("sc_tile") * 2 + lax.axis_index("sc_core")

        def update(group):
            for c0 in range(0, c, cb):
                at = (pl.ds(group * 8, 8), pl.ds(c0, cb))
                for hbm, buf in ((w_hbm, wb), (g_hbm, gb), (m_hbm, mb), (v_hbm, vb)):
                    pltpu.sync_copy(hbm.at[at], buf)
                pltpu.sync_copy(gb, go_hbm.at[at])

                @pl.loop(0, 8)
                def _(rr):
                    @pl.loop(0, cb, step=SC_LANES)
                    def _(i):
                        lanes = (rr, pl.ds(i, SC_LANES))
                        wb[lanes], mb[lanes], vb[lanes] = _adamw_update(wb[lanes], gb[lanes], mb[lanes], vb[lanes])

                for buf, hbm in ((wb, d_hbm), (mb, nm_hbm), (vb, nv_hbm)):
                    pltpu.sync_copy(buf, hbm.at[at])

        @pl.loop(0, per_tile)
        def _(k):
            group = k * SC_TILES + tile
            if groups % SC_TILES:
                pl.when(group < groups)(lambda: update(group))
            else:
                update(group)

    sd = jax.ShapeDtypeStruct((r, c), F32)
    return pl.kernel(body, name=name, out_type=[sd, sd, sd, sd],
                     mesh=plsc.VectorSubcoreMesh(core_axis_name="sc_core", subcore_axis_name="sc_tile"),
                     scratch_types=[pltpu.VMEM((8, cb), F32)] * 4)(w, g, m, v)


def _adamw(w, g, m, v, *, name, ride=None):
    r, c = w.shape
    tc = c if c <= 4096 else _pick(c, (2048, 1024, 512, 256, 128))
    tr = next(t for t in (512, 256, 128, 64, 32, 16, 8) if r % t == 0 and t * tc <= 256 * 1024)

    def body(w_ref, g_ref, m_ref, v_ref, go_ref, d_ref, nm_ref, nv_ref):
        go_ref[...] = g_ref[...]
        d_ref[...], nm_ref[...], nv_ref[...] = _adamw_update(w_ref[...], g_ref[...], m_ref[...], v_ref[...])

    blk = ((tr, tc), lambda i, j: (i, j))
    sd = jax.ShapeDtypeStruct((r, c), F32)
    return _ew(body, [w, g, m, v], [blk] * 4, [sd] * 4, [blk] * 4, (r // tr, c // tc), name=name, ride=ride)


BIG = (("w_in", 1), ("w_sb_out", 1), ("w_ca_out", 1), ("w_mix_out", 0), ("w_ffn_in", 1), ("w_ffn_out", 0),
       ("w_ple_in", 1), ("w_ple_gate", 0))
AXIS = dict(BIG)
HEAD_PARTS = 8
HEAD_HOSTS = ("w_ffn_in", "w_ffn_out")
TAIL_PARTS = 16
TAIL_SECOND = 5
TAIL_FIRST = 2
ON_SPARSECORE = tuple(n for n, _ in BIG if n != "w_in")
SMALL = ("rel_bias", "g_mix", "g_ffn", "g_ple", "g_final")
ORDER = ("w_in", "w_sb_out", "w_ca_out", "w_mix_out", "rel_bias", "g_mix", "g_ffn", "g_ple", "g_final",
         "w_ffn_in", "w_ffn_out", "w_ple_in", "w_ple_gate")


def _pack(parts):
    flat = jnp.concatenate([a.reshape(-1) for a in parts])
    rows = -(-flat.shape[0] // 1024) * 8
    return jnp.pad(flat, (0, rows * 128 - flat.shape[0])).reshape(rows, 128)


def _unpack(packed, like):
    flat, out, at = packed.reshape(-1), [], 0
    for a in like:
        out.append(flat[at:at + a.size].reshape(a.shape))
        at += a.size
    return out


def kernel(x, p, w_in, w_sb_out, w_ca_out, w_mix_out, rel_bias, g_mix, g_ffn, g_ple, g_final, w_ffn_in, w_ffn_out, w_ple_in, w_ple_gate, loss_target, m_w_in, m_w_sb_out, m_w_ca_out, m_w_mix_out, m_rel_bias, m_g_mix, m_g_ffn, m_g_ple, m_g_final, m_w_ffn_in, m_w_ffn_out, m_w_ple_in, m_w_ple_gate, v_w_in, v_w_sb_out, v_w_ca_out, v_w_mix_out, v_rel_bias, v_g_mix, v_g_ffn, v_g_ple, v_g_final, v_w_ffn_in, v_w_ffn_out, v_w_ple_in, v_w_ple_gate):
    weights = dict(w_in=w_in, w_sb_out=w_sb_out, w_ca_out=w_ca_out, w_mix_out=w_mix_out, rel_bias=rel_bias,
                   g_mix=g_mix, g_ffn=g_ffn, g_ple=g_ple, g_final=g_final, w_ffn_in=w_ffn_in,
                   w_ffn_out=w_ffn_out, w_ple_in=w_ple_in, w_ple_gate=w_ple_gate)
    m_in = dict(w_in=m_w_in, w_sb_out=m_w_sb_out, w_ca_out=m_w_ca_out, w_mix_out=m_w_mix_out, rel_bias=m_rel_bias,
                g_mix=m_g_mix, g_ffn=m_g_ffn, g_ple=m_g_ple, g_final=m_g_final, w_ffn_in=m_w_ffn_in,
                w_ffn_out=m_w_ffn_out, w_ple_in=m_w_ple_in, w_ple_gate=m_w_ple_gate)
    v_in = dict(w_in=v_w_in, w_sb_out=v_w_sb_out, w_ca_out=v_w_ca_out, w_mix_out=v_w_mix_out, rel_bias=v_rel_bias,
                g_mix=v_g_mix, g_ffn=v_g_ffn, g_ple=v_g_ple, g_final=v_g_final, w_ffn_in=v_w_ffn_in,
                w_ffn_out=v_w_ffn_out, w_ple_in=v_w_ple_in, w_ple_gate=v_w_ple_gate)

    pos = jnp.stack([2 * lax.axis_index("x") + lax.axis_index("y"), lax.axis_index("c")]).astype(jnp.int32)
    comm = _Comm(pos, {"w_in": _cast_place(w_in[0], AXIS["w_in"], pos, name="cast_w_in")})
    at = 0
    for n in HEAD_HOSTS:
        ride = comm.gather(("w_in",), "near", part=(at, HEAD_PARTS))
        comm.w[n] = _cast_place(weights[n][0], AXIS[n], pos, name="cast_" + n, ride=ride)
        at += 1
    for n, axis in BIG:
        if n not in comm.w:
            comm.w[n] = _cast_place(weights[n][0], axis, pos, name="cast_" + n)
    _run(comm.gather(("w_in",), "near", part=(at, HEAD_PARTS, HEAD_PARTS - at)), name="gather_w_in_near")
    _run(comm.gather(("w_in",), "far"), name="gather_w_in_far")
    _run(comm.gather(("w_in",), "pair"), name="gather_w_in_pair")
    small = dict(rel_bias=rel_bias[0], g_mix=g_mix, g_ffn=g_ffn, g_ple=g_ple, g_final=g_final.reshape(1, -1))
    loss, grad_x, gs = _step(x[0], p[0, 0], loss_target[0], small, comm)

    grads, delta, new_m, new_v = {}, {}, {}, {}
    for n in [n for n, _ in BIG if n != "w_in"] + ["w_in"]:
        if n == "w_in":
            _run(comm.tail_rest(), name="rs_chips_w_in")
            comm.sum(("w_in",))
            _run(comm.share(("w_in",)), name="rs_share_w_in")
        update = _adamw_sc if n in ON_SPARSECORE else _adamw
        g, d, nm, nv = update(weights[n][0], comm.result(n), m_in[n][0], v_in[n][0], name="adamw_" + n)
        grads[n], delta[n], new_m[n], new_v[n] = g[None], d[None], nm[None], nv[None]

    like = [weights[n] for n in SMALL]
    reduced = _small_all_reduce(_pack([gs[n] for n in SMALL] + [loss[:, :1]]), name="small_all_reduce")
    g_small = _unpack(reduced, like + [loss[:, :1]])
    total_loss = g_small[-1].reshape(())
    g_packed = _pack(g_small[:-1])
    _, d_s, m_s, v_s = _adamw(_pack(like), g_packed, _pack([m_in[n] for n in SMALL]), _pack([v_in[n] for n in SMALL]),
                           name="adamw_small")
    for n, g, d, nm, nv in zip(SMALL, g_small[:-1], _unpack(d_s, like), _unpack(m_s, like), _unpack(v_s, like)):
        grads[n], delta[n], new_m[n], new_v[n] = g, d, nm, nv

    return (total_loss, grad_x[None], *[grads[n] for n in ORDER], *[delta[n] for n in ORDER],
            *[new_m[n] for n in ORDER], *[new_v[n] for n in ORDER])
```

```python
import functools
import math

import jax
import jax.numpy as jnp
from jax import lax
from jax.experimental import pallas as pl
from jax.experimental.pallas import tpu as pltpu
from jax.experimental.pallas import tpu_sc as plsc

F32 = jnp.float32
BF16 = jnp.bfloat16

HEAD_DIM = 128
CHUNK = 64
LEFT_CHUNKS = 8
REL_CLIP = 128
N_REL = REL_CLIP + CHUNK
BAND = (LEFT_CHUNKS + 2) * CHUNK
CA_PER_STEP = 4
CA_ROWS = CA_PER_STEP * CHUNK
CA_BAND = BAND + CA_PER_STEP * CHUNK
CA_PAD = BAND
SB_BLOCK = 128
SB_KEYS = 512
SB_GROUPS = SB_KEYS // SB_BLOCK
SB_ROWS = SB_KEYS
EPS = 1e-6
NEG = -1e30

ADAM_LR = 0.001
ADAM_B1 = 0.9
ADAM_B2 = 0.999
ADAM_EPS = 1e-08
ADAM_WD = 0.01
ADAM_STEP = 10

VMEM_LIMIT = 48 * 1024 * 1024
MM_VMEM_BUDGET = 36 * 1024 * 1024
V7X_HBM_BYTES_PER_S = 3.7e12
GRID_STEP_S = 0.35e-6
MESH = pl.DeviceIdType.MESH
N_CHIPS = 4


def _pick(dim, prefs):
    for t in prefs:
        if dim % t == 0:
            return t
    raise ValueError(f"no tile for {dim}")


def _cparams(sem=None):
    return pltpu.CompilerParams(dimension_semantics=sem, vmem_limit_bytes=VMEM_LIMIT)


def _sigmoid(v):
    return 1.0 / (1.0 + jnp.exp(-v))


def _dot(a, b, dims):
    return lax.dot_general(a, b, (dims, ((), ())), preferred_element_type=F32)


def _dot_nn(a, b):
    return _dot(a, b, ((1,), (0,)))


def _dot_nt(a, b):
    return _dot(a, b, ((1,), (1,)))


def _dot_tn(a, b):
    return _dot(a, b, ((0,), (0,)))


HBM = pl.BlockSpec(memory_space=pltpu.HBM)


class _Ride:
    def __init__(self):
        self.items = []

    def add(self, ins, outs, aliases, n_sems, start, finish, sink):
        self.items.append((ins, outs, aliases, n_sems, start, finish, sink))


def _call(body, args, *, name, grid, in_specs, out_specs, out_shape, scratch_shapes=(), sem=None, ride=None,
          scalars=None, onto=()):
    items = ride.items if ride is not None else []
    if onto:
        args, in_specs = list(args) + list(onto), list(in_specs) + [HBM] * len(onto)
        inner, body = body, lambda *refs: inner(*refs[:len(args) - len(onto)], *refs[len(args):])
    n_in, n_out, n_scr = len(args), len(out_shape), len(scratch_shapes)
    r_ins = [a for it in items for a in it[0]]
    r_outs = [o for it in items for o in it[1]]
    updated = [id(it[0][i]) for it in items for i in it[2]]
    assert len(set(updated)) == len(updated), "one call may update a buffer in place only once"
    aliases, a, b = {n_in - len(onto) + t: t for t in range(len(onto))}, n_in, n_out
    for it in items:
        aliases.update({a + i: b + o for i, o in it[2].items()})
        a, b = a + len(it[0]), b + len(it[1])
    sems = [pltpu.SemaphoreType.DMA((it[3],)) for it in items for _ in range(2)]

    def wrapped(*refs):
        head, refs = (refs[:1], refs[1:]) if scalars is not None else ((), refs)
        ins, rin = refs[:n_in], refs[n_in:n_in + len(r_ins)]
        at = n_in + len(r_ins)
        outs, rout = refs[at:at + n_out], refs[at + n_out:at + n_out + len(r_outs)]
        at += n_out + len(r_outs)
        scr, rsem = refs[at:at + n_scr], refs[at + n_scr:]

        def each(which):
            a = b = 0
            for q, it in enumerate(items):
                it[which](rin[a:a + len(it[0])], rout[b:b + len(it[1])], rsem[2 * q], rsem[2 * q + 1])
                a, b = a + len(it[0]), b + len(it[1])

        if items:
            ids = [pl.program_id(d) for d in range(len(grid))]
            first = functools.reduce(jnp.logical_and, [i == 0 for i in ids])
            last = functools.reduce(jnp.logical_and, [i == g - 1 for i, g in zip(ids, grid)])
            pl.when(first)(lambda: each(4))
        body(*head, *ins, *outs, *scr)
        if items:
            pl.when(last)(lambda: each(5))

    specs = dict(grid=grid, in_specs=list(in_specs) + [HBM] * len(r_ins),
                 out_specs=list(out_specs) + [HBM] * len(r_outs), scratch_shapes=list(scratch_shapes) + sems)
    if scalars is not None:
        specs = dict(grid_spec=pltpu.PrefetchScalarGridSpec(num_scalar_prefetch=1, **specs))
        aliases = {i + 1: o for i, o in aliases.items()}
    res = pl.pallas_call(
        wrapped, name=name, **specs,
        out_shape=list(out_shape) + r_outs,
        input_output_aliases=aliases,
        compiler_params=_cparams(("arbitrary",) * len(grid) if items else sem),
    )(*(() if scalars is None else (scalars,)), *args, *r_ins)
    b = n_out
    for it in items:
        it[6](res[b:b + len(it[1])])
        b += len(it[1])
    return list(res[:n_out])


def _mm_tiles(m, n_align, n, k, a_bytes, b_bytes, out_bytes):
    best = None
    tks = sorted({t for t in (k, k // 2, k // 4, 2048, 1024, 512, 256, 128) if t <= k and k % t == 0 and t % 128 == 0})
    for tm in (t for t in (2048, 1024, 512, 256, 128) if m % t == 0):
        for tn in (t for t in (2048, 1024, 512, 256, 128) if n_align % t == 0):
            for tk in tks:
                nk = k // tk
                vmem = 2 * (tm * tk * a_bytes + tk * tn * b_bytes + tm * tn * out_bytes) + tm * tn * 4
                if vmem > MM_VMEM_BUDGET:
                    continue
                traffic = m * k * a_bytes * (n // tn if nk > 1 else 1) + k * n * b_bytes * (m // tm)
                traffic += tm * tk * a_bytes + tk * tn * b_bytes + tm * tn * out_bytes
                traffic += m * n * 4 * nk if nk > 1 else 0
                cost = traffic / V7X_HBM_BYTES_PER_S + (m // tm) * (n // tn) * nk * GRID_STEP_S
                if best is None or cost < best[0]:
                    best = (cost, tm, tn, tk)
    return best[1:]


def _mm(a, b, mode, out_dtypes, *, name, n=None, b_col_off=0, resid=None, ride=None, rows=None, onto=(), m_half=None):
    if mode == "nn":
        m, k = a.shape
        n = b.shape[1] if n is None else n
    elif mode == "nt":
        m, k = a.shape
        n = b.shape[0]
    else:
        k, m = a.shape
        n = b.shape[1]
    if m_half is not None:
        m //= 2
    m_all, (row0, m) = m, (0, m) if rows is None else rows
    n_out = len(out_dtypes)
    has_resid = resid is not None
    out_bytes = sum(jnp.dtype(dt).itemsize for dt in out_dtypes) + (4 if has_resid else 0)
    tm, tn, tk = _mm_tiles(math.gcd(m, row0) if row0 else m, math.gcd(n, b_col_off) if b_col_off else n, n, k,
                           a.dtype.itemsize, b.dtype.itemsize, out_bytes)
    nk = k // tk
    boff, roff = b_col_off // tn, row0 // tm
    dot = {"nn": _dot_nn, "nt": _dot_nt, "tn": _dot_tn}[mode]
    if m_half is None:
        half = lambda: 0
    else:
        half = lambda pos_ref: (pos_ref[1] if m_half[0] else 1 - pos_ref[1]) * (m // tm)

    def body(*refs):
        refs = refs[m_half is not None:]
        a_ref, b_ref = refs[0], refs[1]
        r_ref = refs[2] if has_resid else None
        o_refs = refs[2 + has_resid: 2 + has_resid + n_out]

        def finish(r):
            if has_resid:
                r = r + r_ref[...]
            for o_ref in o_refs:
                o_ref[...] = r.astype(o_ref.dtype)

        part = dot(a_ref[...].astype(BF16), b_ref[...].astype(BF16))
        if nk == 1:
            finish(part)
            return
        acc_ref = refs[-1]
        kk = pl.program_id(2)

        @pl.when(kk == 0)
        def _():
            acc_ref[...] = part

        @pl.when(kk > 0)
        def _():
            acc_ref[...] += part

        @pl.when(kk == nk - 1)
        def _():
            finish(acc_ref[...])

    if mode == "nn":
        a_spec = pl.BlockSpec((tm, tk), lambda i, j, kk, *_: (i + roff, kk))
        b_spec = pl.BlockSpec((tk, tn), lambda i, j, kk, *_: (kk, j + boff))
    elif mode == "nt":
        a_spec = pl.BlockSpec((tm, tk), lambda i, j, kk, *_: (i + roff, kk))
        b_spec = pl.BlockSpec((tn, tk), lambda i, j, kk, *_: (j, kk))
    else:
        a_spec = pl.BlockSpec((tk, tm), lambda i, j, kk, *pos: (kk, i + half(*pos)))
        b_spec = pl.BlockSpec((tk, tn), lambda i, j, kk, *_: (kk, j))
    o_spec = pl.BlockSpec((tm, tn), lambda i, j, kk, *_: (i + roff, j))
    in_specs = [a_spec, b_spec] + ([o_spec] if has_resid else [])
    args = [a, b] + ([resid] if has_resid else [])
    outs = _call(
        body, args, name=name,
        grid=(m // tm, n // tn, nk),
        in_specs=in_specs,
        out_specs=[o_spec] * n_out,
        out_shape=[jax.ShapeDtypeStruct((m_all, n), dt) for dt in out_dtypes],
        scratch_shapes=[pltpu.VMEM((tm, tn), F32)] if nk > 1 else [],
        sem=("parallel", "parallel", "arbitrary"), ride=ride, onto=onto,
        scalars=None if m_half is None else m_half[1])
    return outs[0] if n_out == 1 else tuple(outs)


def _row_tile(s):
    return _pick(s, (256, 128))


def _rms_fwd(x, g, *, name, ride=None):
    s, d = x.shape
    tr = _row_tile(s)

    def body(x_ref, g_ref, o_ref):
        xv = x_ref[...]
        r = lax.rsqrt(jnp.mean(xv * xv, axis=1, keepdims=True) + EPS)
        o_ref[...] = (xv * r * g_ref[...]).astype(o_ref.dtype)

    return _call(
        body, [x, g], name=name, grid=(s // tr,),
        in_specs=[pl.BlockSpec((tr, d), lambda i: (i, 0)), pl.BlockSpec((1, d), lambda i: (0, 0))],
        out_specs=[pl.BlockSpec((tr, d), lambda i: (i, 0))],
        out_shape=[jax.ShapeDtypeStruct((s, d), BF16)], sem=("parallel",), ride=ride)[0]


def _rms_bwd(x, g, dh, dres, *, name, ride=None):
    s, d = x.shape
    tr = _row_tile(s)

    def body(x_ref, g_ref, dh_ref, dres_ref, dx_ref, dx16_ref, dg_ref):
        i = pl.program_id(0)
        xv = x_ref[...]
        r = lax.rsqrt(jnp.mean(xv * xv, axis=1, keepdims=True) + EPS)
        xhat = xv * r
        dhv = dh_ref[...]
        dxhat = dhv * g_ref[...]
        proj = jnp.mean(dxhat * xhat, axis=1, keepdims=True)
        dx = dres_ref[...] + r * (dxhat - xhat * proj)
        dx_ref[...] = dx
        dx16_ref[...] = dx.astype(dx16_ref.dtype)

        @pl.when(i == 0)
        def _():
            dg_ref[...] = jnp.zeros_like(dg_ref)

        dg_ref[...] += jnp.sum(dhv * xhat, axis=0, keepdims=True)

    row = pl.BlockSpec((tr, d), lambda i: (i, 0))
    vec = pl.BlockSpec((1, d), lambda i: (0, 0))
    return _call(
        body, [x, g, dh, dres], name=name, grid=(s // tr,),
        in_specs=[row, vec, row, row],
        out_specs=[row, row, vec],
        out_shape=[jax.ShapeDtypeStruct((s, d), F32), jax.ShapeDtypeStruct((s, d), BF16),
                   jax.ShapeDtypeStruct((1, d), F32)],
        sem=("arbitrary",), ride=ride)


def _ple_loss(x2, gate, pe, g, target, *, name):
    s, d = x2.shape
    tr = _row_tile(s)

    def body(x2_ref, gate_ref, pe_ref, g_ref, t_ref, dx_ref, dgate_ref, dpe_ref, dg_ref, loss_ref):
        i = pl.program_id(0)
        sg = _sigmoid(gate_ref[...].astype(F32))
        pv = pe_ref[...].astype(F32)
        xv = x2_ref[...] + sg * pv
        gv = g_ref[...]
        r = lax.rsqrt(jnp.mean(xv * xv, axis=1, keepdims=True) + EPS)
        xhat = xv * r
        err = xhat * gv - t_ref[...]
        dy = err * (1.0 / d)
        dxhat = dy * gv
        proj = jnp.mean(dxhat * xhat, axis=1, keepdims=True)
        dx = r * (dxhat - xhat * proj)
        dx_ref[...] = dx
        dgate_ref[...] = (dx * pv * sg * (1.0 - sg)).astype(dgate_ref.dtype)
        dpe_ref[...] = (dx * sg).astype(dpe_ref.dtype)

        @pl.when(i == 0)
        def _():
            dg_ref[...] = jnp.zeros_like(dg_ref)
            loss_ref[...] = jnp.zeros_like(loss_ref)

        dg_ref[...] += jnp.sum(dy * xhat, axis=0, keepdims=True)
        part = 0.5 * jnp.sum(jnp.mean(err * err, axis=1, keepdims=True), axis=0, keepdims=True)
        loss_ref[...] += jnp.broadcast_to(part, loss_ref.shape)

    row = pl.BlockSpec((tr, d), lambda i: (i, 0))
    vec = pl.BlockSpec((1, d), lambda i: (0, 0))
    return pl.pallas_call(
        body, name=name, grid=(s // tr,),
        in_specs=[row, row, row, vec, row],
        out_specs=[row, row, row, vec, pl.BlockSpec((1, 128), lambda i: (0, 0))],
        out_shape=[jax.ShapeDtypeStruct((s, d), F32), jax.ShapeDtypeStruct((s, d), BF16),
                   jax.ShapeDtypeStruct((s, d), BF16), jax.ShapeDtypeStruct((1, d), F32),
                   jax.ShapeDtypeStruct((1, 128), F32)],
        compiler_params=_cparams(("arbitrary",)),
    )(x2, gate, pe, g, target)


def _ew(body, ins, in_blocks, outs, out_blocks, grid, *, name, ride=None):
    return _call(body, ins, name=name, grid=grid,
                 in_specs=[pl.BlockSpec(bs, im) for bs, im in in_blocks],
                 out_specs=[pl.BlockSpec(bs, im) for bs, im in out_blocks],
                 out_shape=outs, sem=("parallel",) * len(grid), ride=ride)


def _gate_merge_fwd(gates, o_sb, o_ca, *, name, ride=None):
    s, d = o_sb.shape
    tr, tc = _row_tile(s), _pick(d, (1024, 512, 256, 128))
    nc = d // tc

    def body(gs_ref, gc_ref, os_ref, oc_ref, m_ref):
        f32 = lambda ref: ref[...].astype(F32)
        m = _sigmoid(f32(gs_ref)) * f32(os_ref) + _sigmoid(f32(gc_ref)) * f32(oc_ref)
        m_ref[...] = m.astype(m_ref.dtype)

    blk = ((tr, tc), lambda i, j: (i, j))
    return _ew(body, [gates, gates, o_sb, o_ca],
               [blk, ((tr, tc), lambda i, j: (i, j + nc)), blk, blk],
               [jax.ShapeDtypeStruct((s, d), BF16)], [blk], (s // tr, nc), name=name, ride=ride)[0]


def _gate_merge_bwd(dmerged, gates, o_sb, o_ca, *, name):
    s, d = o_sb.shape
    tr, tc = _row_tile(s), _pick(d, (1024, 512, 256, 128))
    nc = d // tc

    def body(dm_ref, gs_ref, gc_ref, os_ref, oc_ref, dgs_ref, dgc_ref, dos_ref, doc_ref):
        f32 = lambda ref: ref[...].astype(F32)
        dm = f32(dm_ref)
        ss = _sigmoid(f32(gs_ref))
        sc = _sigmoid(f32(gc_ref))
        dgs_ref[...] = (dm * f32(os_ref) * ss * (1.0 - ss)).astype(dgs_ref.dtype)
        dgc_ref[...] = (dm * f32(oc_ref) * sc * (1.0 - sc)).astype(dgc_ref.dtype)
        dos_ref[...] = (dm * ss).astype(dos_ref.dtype)
        doc_ref[...] = (dm * sc).astype(doc_ref.dtype)

    blk = ((tr, tc), lambda i, j: (i, j))
    sd = jax.ShapeDtypeStruct((s, d), BF16)
    return _ew(body, [dmerged, gates, gates, o_sb, o_ca],
               [blk, blk, ((tr, tc), lambda i, j: (i, j + nc)), blk, blk],
               [sd, sd, sd, sd], [blk, blk, blk, blk], (s // tr, nc), name=name)


def _swiglu_fwd(gu, *, name, ride=None):
    s, f2 = gu.shape
    f = f2 // 2
    tr, tc = 128, _pick(f, (512, 256, 128))

    def body(gu_ref, a_ref):
        for at in range(0, f, tc):
            gv = gu_ref[:, at:at + tc].astype(F32)
            a_ref[:, at:at + tc] = (gv * _sigmoid(gv) * gu_ref[:, f + at:f + at + tc].astype(F32)).astype(a_ref.dtype)

    row = lambda i: (i, 0)
    return _ew(body, [gu], [((tr, f2), row)], [jax.ShapeDtypeStruct((s, f), BF16)], [((tr, f), row)],
               (s // tr,), name=name, ride=ride)[0]


def _swiglu_bwd(dact, gu, *, name):
    s, f2 = gu.shape
    f = f2 // 2
    tr, tc = 128, _pick(f, (512, 256, 128))

    def body(da_ref, gu_ref, o_ref):
        for at in range(0, f, tc):
            da = da_ref[:, at:at + tc].astype(F32)
            gv = gu_ref[:, at:at + tc].astype(F32)
            sg = _sigmoid(gv)
            uv = gu_ref[:, f + at:f + at + tc].astype(F32)
            o_ref[:, at:at + tc] = (da * uv * sg * (1.0 + gv * (1.0 - sg))).astype(o_ref.dtype)
            o_ref[:, f + at:f + at + tc] = (da * gv * sg).astype(o_ref.dtype)

    row = lambda i: (i, 0)
    return _ew(body, [dact, gu], [((tr, f), row), ((tr, f2), row)], [jax.ShapeDtypeStruct((s, f2), BF16)],
               [((tr, f2), row)], (s // tr,), name=name)[0]


def _concat_cols(parts, *, name):
    s = parts[0].shape[0]
    widths = [p.shape[1] for p in parts]
    tr = 256

    def body(*refs):
        o_ref, at = refs[-1], 0
        for p_ref, width in zip(refs, widths):
            o_ref[:, at:at + width] = p_ref[...]
            at += width

    row = lambda i: (i, 0)
    return _ew(body, list(parts), [((tr, width), row) for width in widths],
               [jax.ShapeDtypeStruct((s, sum(widths)), parts[0].dtype)], [((tr, sum(widths)), row)],
               (s // tr,), name=name)[0]


def _sb_tri(later):
    row = lax.broadcasted_iota(jnp.int32, (SB_BLOCK, SB_BLOCK), 0)
    col = lax.broadcasted_iota(jnp.int32, (SB_BLOCK, SB_BLOCK), 1)
    tri = (row > col) if later else (row < col)
    return jnp.concatenate([tri.astype(BF16), jnp.ones((SB_BLOCK, SB_BLOCK), BF16)], axis=1)


def _sb_valid(i, j, own):
    if not own:
        return None
    qi = i * SB_ROWS + lax.broadcasted_iota(jnp.int32, (SB_ROWS, SB_KEYS), 0)
    ki = j * SB_KEYS + lax.broadcasted_iota(jnp.int32, (SB_ROWS, SB_KEYS), 1)
    return ki < qi


def _sb_scan(v, tri, run, later):
    hi = v.astype(BF16)
    lo = (v - hi.astype(F32)).astype(BF16)
    outs = [None] * SB_GROUPS
    for b in (reversed(range(SB_GROUPS)) if later else range(SB_GROUPS)):
        cols = slice(b * SB_BLOCK, (b + 1) * SB_BLOCK)
        r = _dot_nn(hi[:, cols], tri) + _dot_nn(lo[:, cols], tri)
        outs[b] = r[:, :SB_BLOCK] + run
        run = run + r[:, SB_BLOCK:]
    return jnp.concatenate(outs, axis=1), run


def _masked(valid, v):
    return v if valid is None else jnp.where(valid, v, 0.0)


def _sb_scores(q, kj, scale, valid):
    z = _dot_nt(q, kj) * scale
    t = jnp.log(1.0 + jnp.exp(-jnp.abs(z)))
    return jnp.minimum(z, 0.0) - t, _masked(valid, -jnp.maximum(z, 0.0) - t)


def _sb_specs(h_count, s, col0):
    q_spec = pl.BlockSpec((SB_ROWS, HEAD_DIM), lambda h, i: (i, col0 + h))
    k_spec = pl.BlockSpec((s, HEAD_DIM), lambda h, i: (0, col0 + h_count + h))
    v_spec = pl.BlockSpec((s, HEAD_DIM), lambda h, i: (0, col0 + 2 * h_count + h))
    return q_spec, k_spec, v_spec


def _sb_fwd(qkv, n_heads, col0, *, name, ride=None):
    s = qkv.shape[0]
    nq = s // SB_ROWS
    scale = HEAD_DIM ** -0.5

    def body(q_ref, k_ref, v_ref, o_ref):
        i = pl.program_id(1)
        q = q_ref[...]
        tri = _sb_tri(later=True)

        def step(j, carry, own):
            run, acc = carry
            off = pl.multiple_of(j * SB_KEYS, SB_KEYS)
            valid = _sb_valid(i, j, own)
            ls, lk = _sb_scores(q, k_ref[pl.ds(off, SB_KEYS), :], scale, valid)
            between, run = _sb_scan(lk, tri, run, later=True)
            a = _masked(valid, jnp.exp(ls + between))
            return run, acc + _dot_nn(a.astype(BF16), v_ref[pl.ds(off, SB_KEYS), :])

        carry = step(i, (jnp.zeros((SB_ROWS, SB_BLOCK), F32), jnp.zeros((SB_ROWS, HEAD_DIM), F32)), True)
        _, acc = lax.fori_loop(0, i, lambda jj, c: step(i - 1 - jj, c, False), carry)
        o_ref[...] = acc.astype(o_ref.dtype)

    q_spec, k_spec, v_spec = _sb_specs(n_heads, s, col0)
    return _call(
        body, [qkv, qkv, qkv], name=name, grid=(n_heads, nq),
        in_specs=[q_spec, k_spec, v_spec],
        out_specs=[pl.BlockSpec((SB_ROWS, HEAD_DIM), lambda h, i: (i, h))],
        out_shape=[jax.ShapeDtypeStruct((s, n_heads * HEAD_DIM), BF16)],
        sem=("parallel", "arbitrary"), ride=ride)[0]


def _sb_bwd(qkv, dy, n_heads, col0, *, name, ride=None):
    s = qkv.shape[0]
    nq = s // SB_ROWS
    scale = HEAD_DIM ** -0.5

    def body(q_ref, k_ref, v_ref, dy_ref, dq_ref, dk_ref, dv_ref, e_scr, sg_scr, dk_acc, dv_acc):
        i = pl.program_id(1)
        q = q_ref[...]
        dyv = dy_ref[...]

        @pl.when(i == 0)
        def _():
            dk_acc[...] = jnp.zeros_like(dk_acc)
            dv_acc[...] = jnp.zeros_like(dv_acc)

        tri_later = _sb_tri(later=True)

        def pass1(j, run, own):
            off = pl.multiple_of(j * SB_KEYS, SB_KEYS)
            valid = _sb_valid(i, j, own)
            ls, lk = _sb_scores(q, k_ref[pl.ds(off, SB_KEYS), :], scale, valid)
            between, run = _sb_scan(lk, tri_later, run, later=True)
            a = _masked(valid, jnp.exp(ls + between))
            e_scr[j] = a * _dot_nt(dyv, v_ref[pl.ds(off, SB_KEYS), :])
            sg_scr[j] = jnp.exp(ls)
            dv_acc[pl.ds(off, SB_KEYS), :] += _dot_tn(a.astype(BF16), dyv)
            return run

        lax.fori_loop(0, i, lambda jj, run: pass1(i - 1 - jj, run, False),
                      pass1(i, jnp.zeros((SB_ROWS, SB_BLOCK), F32), True))

        tri_earlier = _sb_tri(later=False)

        def pass2(j, carry, own):
            run, dq = carry
            off = pl.multiple_of(j * SB_KEYS, SB_KEYS)
            kj = k_ref[pl.ds(off, SB_KEYS), :]
            sg = sg_scr[j]
            e = e_scr[j]
            before, run = _sb_scan(e, tri_earlier, run, later=False)
            dz = _masked(_sb_valid(i, j, own), e * (1.0 - sg) - sg * before) * scale
            dzb = dz.astype(BF16)
            dk_acc[pl.ds(off, SB_KEYS), :] += _dot_tn(dzb, q)
            return run, dq + _dot_nn(dzb, kj)

        init = (jnp.zeros((SB_ROWS, SB_BLOCK), F32), jnp.zeros((SB_ROWS, HEAD_DIM), F32))
        _, dq = pass2(i, lax.fori_loop(0, i, lambda j, c: pass2(j, c, False), init), True)
        dq_ref[...] = dq.astype(dq_ref.dtype)

        @pl.when(i == nq - 1)
        def _():
            dk_ref[...] = dk_acc[...].astype(dk_ref.dtype)
            dv_ref[...] = dv_acc[...].astype(dv_ref.dtype)

    q_spec, k_spec, v_spec = _sb_specs(n_heads, s, col0)
    blk = pl.BlockSpec((SB_ROWS, HEAD_DIM), lambda h, i: (i, h))
    full = pl.BlockSpec((s, HEAD_DIM), lambda h, i: (0, h))
    sd = jax.ShapeDtypeStruct((s, n_heads * HEAD_DIM), BF16)
    return _call(
        body, [qkv, qkv, qkv, dy], name=name, grid=(n_heads, nq),
        in_specs=[q_spec, k_spec, v_spec, blk],
        out_specs=[blk, full, full],
        out_shape=[sd, sd, sd],
        scratch_shapes=[pltpu.VMEM((s // SB_KEYS, SB_ROWS, SB_KEYS), F32), pltpu.VMEM((s // SB_KEYS, SB_ROWS, SB_KEYS), F32),
                        pltpu.VMEM((s, HEAD_DIM), F32), pltpu.VMEM((s, HEAD_DIM), F32)],
        sem=("parallel", "arbitrary"), ride=ride)


def _band_bias(rel_bias):
    h = rel_bias.shape[0]
    width = BAND + CHUNK
    first = width - 1 - N_REL
    line = jnp.concatenate([jnp.broadcast_to(rel_bias[:, :1], (h, first)), rel_bias], axis=1)
    tiled = jnp.broadcast_to(line[:, None, :], (h, CHUNK, width - 1)).reshape(h, CHUNK * (width - 1))
    skew = jnp.pad(tiled, ((0, 0), (0, CHUNK))).reshape(h, CHUNK, width)[:, ::-1, :BAND]
    seen = jnp.arange(BAND) >= CHUNK
    return jnp.where(seen[None, None, :], skew, NEG)


def _band_bias_grad(dbias):
    h = dbias.shape[0]
    width = BAND + CHUNK
    flipped = jnp.pad(dbias[:, ::-1, :], ((0, 0), (0, 0), (0, CHUNK)))
    skew = flipped.reshape(h, CHUNK * width)[:, :CHUNK * (width - 1)].reshape(h, CHUNK, width - 1)
    diag = jnp.sum(skew, axis=1)
    first = width - 1 - N_REL
    clipped = jnp.sum(diag[:, :first + 1], axis=1, keepdims=True)
    return jnp.concatenate([clipped, diag[:, first + 1:]], axis=1)


def _group_bias(band):
    return jnp.concatenate([jnp.pad(band, ((0, 0), (0, 0), ((u + 1) * CHUNK, (CA_PER_STEP - 1 - u) * CHUNK)),
                                    constant_values=NEG) for u in range(CA_PER_STEP)], axis=1)


def _group_bias_grad(dgroup):
    return sum(dgroup[:, u * CHUNK:(u + 1) * CHUNK, (u + 1) * CHUNK:(u + 1) * CHUNK + BAND] for u in range(CA_PER_STEP))


def _ca_load_padded(k_ref, v_ref, kp, vp, s):
    kp[pl.ds(0, CA_PAD), :] = jnp.zeros((CA_PAD, HEAD_DIM), kp.dtype)
    vp[pl.ds(0, CA_PAD), :] = jnp.zeros((CA_PAD, HEAD_DIM), vp.dtype)
    kp[pl.ds(CA_PAD, s), :] = k_ref[...]
    vp[pl.ds(CA_PAD, s), :] = v_ref[...]


def _ca_weights(q, kb, bias, off, scale):
    z = _dot_nt(q, kb) * scale + bias
    pos = off + lax.broadcasted_iota(jnp.int32, (CA_ROWS, CA_BAND), 1)
    z = jnp.where(pos >= CA_PAD, z, NEG)
    p = jnp.exp(z - jnp.max(z, axis=1, keepdims=True))
    return p / jnp.sum(p, axis=1, keepdims=True)


def _ca_specs(h_count, s, col0):
    q_spec = pl.BlockSpec((CA_ROWS, HEAD_DIM), lambda h, c: (c, col0 + h))
    k_spec = pl.BlockSpec((s, HEAD_DIM), lambda h, c: (0, col0 + h_count + h))
    v_spec = pl.BlockSpec((s, HEAD_DIM), lambda h, c: (0, col0 + 2 * h_count + h))
    b_spec = pl.BlockSpec((1, CA_ROWS, CA_BAND), lambda h, c: (h, 0, 0))
    return q_spec, k_spec, v_spec, b_spec


def _ca_fwd(qkv, bias, n_heads, col0, *, name, ride=None):
    s = qkv.shape[0]
    nc = s // CA_ROWS
    scale = HEAD_DIM ** -0.5

    def body(q_ref, k_ref, v_ref, b_ref, o_ref, kp, vp):
        c = pl.program_id(1)

        @pl.when(c == 0)
        def _():
            _ca_load_padded(k_ref, v_ref, kp, vp, s)

        off = pl.multiple_of(c * CA_ROWS, CA_ROWS)
        w = _ca_weights(q_ref[...], kp[pl.ds(off, CA_BAND), :], b_ref[0], off, scale)
        o_ref[...] = _dot_nn(w.astype(BF16), vp[pl.ds(off, CA_BAND), :]).astype(o_ref.dtype)

    q_spec, k_spec, v_spec, b_spec = _ca_specs(n_heads, s, col0)
    return _call(
        body, [qkv, qkv, qkv, bias], name=name, grid=(n_heads, nc),
        in_specs=[q_spec, k_spec, v_spec, b_spec],
        out_specs=[pl.BlockSpec((CA_ROWS, HEAD_DIM), lambda h, c: (c, h))],
        out_shape=[jax.ShapeDtypeStruct((s, n_heads * HEAD_DIM), BF16)],
        scratch_shapes=[pltpu.VMEM((s + CA_PAD, HEAD_DIM), BF16), pltpu.VMEM((s + CA_PAD, HEAD_DIM), BF16)],
        sem=("parallel", "arbitrary"), ride=ride)[0]


def _ca_bwd(qkv, bias, dy, n_heads, col0, *, name, ride=None):
    s = qkv.shape[0]
    nc = s // CA_ROWS
    scale = HEAD_DIM ** -0.5

    def body(q_ref, k_ref, v_ref, b_ref, dy_ref, dq_ref, dk_ref, dv_ref, db_ref, kp, vp, dkp, dvp):
        c = pl.program_id(1)

        @pl.when(c == 0)
        def _():
            _ca_load_padded(k_ref, v_ref, kp, vp, s)
            dkp[...] = jnp.zeros_like(dkp)
            dvp[...] = jnp.zeros_like(dvp)
            db_ref[...] = jnp.zeros_like(db_ref)

        off = pl.multiple_of(c * CA_ROWS, CA_ROWS)
        band = pl.ds(off, CA_BAND)
        q = q_ref[...]
        dyv = dy_ref[...]
        kb = kp[band, :]
        w = _ca_weights(q, kb, b_ref[0], off, scale)
        dw = _dot_nt(dyv, vp[band, :])
        dvp[band, :] += _dot_tn(w.astype(BF16), dyv)
        dz = w * (dw - jnp.sum(w * dw, axis=1, keepdims=True))
        db_ref[0] += dz
        dzs = (dz * scale).astype(BF16)
        dq_ref[...] = _dot_nn(dzs, kb).astype(dq_ref.dtype)
        dkp[band, :] += _dot_tn(dzs, q)

        @pl.when(c == nc - 1)
        def _():
            dk_ref[...] = dkp[pl.ds(CA_PAD, s), :].astype(dk_ref.dtype)
            dv_ref[...] = dvp[pl.ds(CA_PAD, s), :].astype(dv_ref.dtype)

    q_spec, k_spec, v_spec, b_spec = _ca_specs(n_heads, s, col0)
    blk = pl.BlockSpec((CA_ROWS, HEAD_DIM), lambda h, c: (c, h))
    full = pl.BlockSpec((s, HEAD_DIM), lambda h, c: (0, h))
    sd = jax.ShapeDtypeStruct((s, n_heads * HEAD_DIM), BF16)
    return _call(
        body, [qkv, qkv, qkv, bias, dy], name=name, grid=(n_heads, nc),
        in_specs=[q_spec, k_spec, v_spec, b_spec, blk],
        out_specs=[blk, full, full, b_spec],
        out_shape=[sd, sd, sd, jax.ShapeDtypeStruct((n_heads, CA_ROWS, CA_BAND), F32)],
        scratch_shapes=[pltpu.VMEM((s + CA_PAD, HEAD_DIM), BF16), pltpu.VMEM((s + CA_PAD, HEAD_DIM), BF16),
                        pltpu.VMEM((s + CA_PAD, HEAD_DIM), F32), pltpu.VMEM((s + CA_PAD, HEAD_DIM), F32)],
        sem=("parallel", "arbitrary"), ride=ride)


EARLY = ("w_sb_out", "w_ca_out", "w_mix_out")


def _step(x, p, target, small, comm):
    w = comm.w
    d = x.shape[1]
    n_sb = w["w_sb_out"].shape[0] // HEAD_DIM
    n_ca = w["w_ca_out"].shape[0] // HEAD_DIM
    qkv_cols = 3 * HEAD_DIM * (n_sb + n_ca)
    ca_col0 = 3 * n_sb
    both = (F32, BF16)

    h1 = _rms_fwd(x, small["g_mix"], name="rms_mix")
    ffn, ple = ("w_ffn_in",), ("w_ple_gate", "w_ple_in")
    qkv = _mm(h1, w["w_in"], "nn", (BF16,), name="proj_qkv", n=qkv_cols, ride=comm.gather(EARLY, "near"))
    gates = _mm(h1, w["w_in"], "nn", (BF16,), name="proj_gates", n=2 * d, b_col_off=qkv_cols,
                ride=comm.gather(ffn, "near", comm.gather(EARLY, "far"), (0, 8)))
    bias = _group_bias(_band_bias(small["rel_bias"]))
    y_sb = _sb_fwd(qkv, n_sb, 0, name="sb_fwd", ride=comm.gather(ffn, "near", comm.gather(EARLY, "pair"), (1, 8, 7)))
    y_ca = _ca_fwd(qkv, bias, n_ca, ca_col0, name="ca_fwd", ride=comm.gather(ffn, "far"))
    out = ("w_ffn_out",)
    o_sb = _mm(y_sb, w["w_sb_out"], "nn", (BF16,), name="sb_out", ride=comm.gather(out, "near", part=(0, 4)))
    o_ca = _mm(y_ca, w["w_ca_out"], "nn", (BF16,), name="ca_out", ride=comm.gather(out, "near", part=(1, 4)))
    merged = _gate_merge_fwd(gates, o_sb, o_ca, name="gate_merge", ride=comm.gather(out, "near", part=(2, 4)))
    x1 = _mm(merged, w["w_mix_out"], "nn", (F32,), name="mix_out", resid=x,
             ride=comm.gather(out, "near", comm.gather(ffn, "pair"), (3, 4)))
    h2 = _rms_fwd(x1, small["g_ffn"], name="rms_ffn")
    gu = _mm(h2, w["w_ffn_in"], "nn", (BF16,), name="ffn_in", ride=comm.gather(ple, "near", comm.gather(out, "far")))
    act = _swiglu_fwd(gu, name="swiglu", ride=comm.gather(ple, "far", comm.gather(out, "pair")))
    x2 = _mm(act, w["w_ffn_out"], "nn", (F32,), name="ffn_out", resid=x1, ride=comm.gather(ple, "pair"))
    h3 = _rms_fwd(x2, small["g_ple"], name="rms_ple")
    t = _mm(h3, w["w_ple_gate"], "nn", (BF16,), name="ple_gate")
    pe = _mm(p, w["w_ple_in"], "nn", (BF16,), name="ple_in")

    def halves(n, acts, dout, ride, name):
        if comm.pos is None:
            return comm.grad(n, *_mm(acts, dout, "tn", both, name=name))
        g16 = _mm(acts, dout, "tn", (BF16,), name=name + "_other", m_half=(False, comm.pos), ride=ride)
        comm.grad(n, None, g16, half=True)
        g32 = _mm(acts, dout, "tn", (F32,), name=name + "_own", m_half=(True, comm.pos), ride=comm.pair((n,)))
        comm.grad(n, g32, g16, half=True)

    gs = {}
    dx3, dt, dpe, gs["g_final"], loss = _ple_loss(x2, t, pe, small["g_final"], target, name="ple_loss")
    comm.grad("w_ple_in", *_mm(p, dpe, "tn", both, name="dw_ple_in"))
    comm.grad("w_ple_gate", *_mm(h3, dt, "tn", both, name="dw_ple_gate"))
    ple = ("w_ple_in", "w_ple_gate")
    dh3 = _mm(dt, w["w_ple_gate"], "nt", (F32,), name="dh_ple", ride=comm.pair(ple))
    dx2, dx2_16, gs["g_ple"] = _rms_bwd(x2, small["g_ple"], dh3, dx3, name="rms_ple_bwd")
    comm.add(ple)
    comm.grad("w_ffn_out", *_mm(act, dx2_16, "tn", both, name="dw_ffn_out", ride=comm.chips(ple)))
    dact = _mm(dx2_16, w["w_ffn_out"], "nt", (BF16,), name="dact", ride=comm.pair(("w_ffn_out",)))
    dgu = _swiglu_bwd(dact, gu, name="swiglu_bwd")
    comm.sum(ple)
    comm.add(("w_ffn_out",))
    comm.grad("w_ffn_in", *_mm(h2, dgu, "tn", both, name="dw_ffn_in",
                               ride=comm.share(ple, comm.chips(("w_ffn_out",)))))
    dh2 = _mm(dgu, w["w_ffn_in"], "nt", (F32,), name="dh_ffn", ride=comm.pair(("w_ffn_in",)))
    dx1, dx1_16, gs["g_ffn"] = _rms_bwd(x1, small["g_ffn"], dh2, dx2, name="rms_ffn_bwd")
    comm.add(("w_ffn_in",))
    comm.sum(("w_ffn_out",))
    comm.grad("w_mix_out", *_mm(merged, dx1_16, "tn", both, name="dw_mix_out", ride=comm.share(("w_ffn_out",))))
    dmerged = _mm(dx1_16, w["w_mix_out"], "nt", (BF16,), name="dmerged", ride=comm.pair(("w_mix_out",)))
    dg_sb, dg_ca, do_sb, do_ca = _gate_merge_bwd(dmerged, gates, o_sb, o_ca, name="gate_merge_bwd")
    comm.add(("w_mix_out",))
    comm.grad("w_sb_out", *_mm(y_sb, do_sb, "tn", both, name="dw_sb_out"))
    comm.grad("w_ca_out", *_mm(y_ca, do_ca, "tn", both, name="dw_ca_out"))
    outs = ("w_sb_out", "w_ca_out")
    dy_sb = _mm(do_sb, w["w_sb_out"], "nt", (BF16,), name="dy_sb", ride=comm.pair(outs))
    dy_ca = _mm(do_ca, w["w_ca_out"], "nt", (BF16,), name="dy_ca")
    comm.add(outs)
    dq_sb, dk_sb, dv_sb = _sb_bwd(qkv, dy_sb, n_sb, 0, name="sb_bwd", ride=comm.chips(("w_ffn_in",)))
    comm.sum(("w_ffn_in",))
    late = ("w_mix_out",) + outs
    dq_ca, dk_ca, dv_ca, dbias = _ca_bwd(qkv, bias, dy_ca, n_ca, ca_col0, name="ca_bwd",
                                         ride=comm.chips(late, comm.share(("w_ffn_in",))))
    comm.sum(late)
    gs["rel_bias"] = _band_bias_grad(_group_bias_grad(dbias))
    dproj = _concat_cols([dq_sb, dk_sb, dv_sb, dq_ca, dk_ca, dv_ca, dg_sb, dg_ca], name="dproj")
    halves("w_in", h1, dproj, comm.share(late), "dw_in")
    comm.add(("w_in",))
    half = x.shape[0] // 2
    dh1 = _mm(dproj, w["w_in"], "nt", (F32,), name="dh_mix_top", rows=(0, half), ride=comm.tail(TAIL_SECOND))
    dh1 = _mm(dproj, w["w_in"], "nt", (F32,), name="dh_mix_bottom", rows=(half, half), onto=(dh1,),
              ride=comm.tail(TAIL_SECOND))
    grad_x, _, gs["g_mix"] = _rms_bwd(x, small["g_mix"], dh1, dx1, name="rms_mix_bwd", ride=comm.tail(TAIL_FIRST))
    return loss, grad_x, gs


def _position():
    x, y, c = lax.axis_index("x"), lax.axis_index("y"), lax.axis_index("c")
    chips = [(1 - x, y), (x, 1 - y), (1 - x, 1 - y)]
    return x, y, c, chips


def _aligned(v, m):
    return v if isinstance(v, int) else pl.multiple_of(v, m)


def _piece_dims(shape, axis):
    k, n = shape
    return (k // 2, n // N_CHIPS) if axis == 1 else (k // N_CHIPS // 2, n)


def _piece(ref, shape, axis, j, h, part=(0, 1)):
    pr, pc = _piece_dims(shape, axis)
    nr = pr // part[1] * (part[2] if len(part) > 2 else 1)
    r0 = part[0] * (pr // part[1])
    if axis == 1:
        return ref.at[pl.ds(_aligned(h * pr + r0, 16), nr), pl.ds(_aligned(j * pc, 128), pc)]
    return ref.at[pl.ds(_aligned((2 * j + h) * pr + r0, 16), nr), :]


def _shard_half(ref, h):
    rows = ref.shape[0] // 2
    return ref.at[pl.ds(_aligned(h * rows, 16), rows), :]


def _remote(src, dst, send_sems, recv_sems, k, to):
    return pltpu.make_async_remote_copy(src_ref=src, dst_ref=dst, send_sem=send_sems.at[k],
                                        recv_sem=recv_sems.at[k], device_id=to, device_id_type=MESH)


def _prefetch_call(body, scalars, ins, in_specs, out_shape, out_specs, grid, *, name, ride=None):
    single = not isinstance(out_shape, (list, tuple))
    outs = _call(body, ins, name=name, grid=grid, in_specs=in_specs,
                 out_specs=[out_specs] if single else out_specs, out_shape=[out_shape] if single else out_shape,
                 sem=("parallel",) * len(grid), ride=ride, scalars=scalars)
    return outs[0] if single else outs


def _slab_tiles(pr, pc):
    tc = pc if pc <= 4096 else _pick(pc, (2048, 1024, 512, 256, 128))
    tr = next(t for t in (1024, 512, 256, 128, 64, 32, 16) if pr % t == 0 and t * tc <= 512 * 1024)
    return tr, tc


def _cast_place(w, axis, pos, *, name, ride=None):
    ks, ns = w.shape
    shape = (ks, ns * N_CHIPS) if axis == 1 else (ks * N_CHIPS, ns)
    tr, tc = _slab_tiles(ks, ns)
    nr, nc = ks // tr, ns // tc

    def body(pos_ref, w_ref, o_ref):
        o_ref[...] = w_ref[...].astype(o_ref.dtype)

    if axis == 1:
        out_map = lambda i, j, pos_ref: (i, pos_ref[0] * nc + j)
    else:
        out_map = lambda i, j, pos_ref: (pos_ref[0] * nr + i, j)
    return _prefetch_call(body, pos, [w], [pl.BlockSpec((tr, tc), lambda i, j, pos_ref: (i, j))],
                          jax.ShapeDtypeStruct(shape, BF16), pl.BlockSpec((tr, tc), out_map), (nr, nc), name=name, ride=ride)


def _run(ride, *, name):
    if ride is None:
        return

    def body(o_ref):
        o_ref[...] = jnp.zeros_like(o_ref)

    _call(body, [], name=name, grid=(1,), in_specs=[], out_specs=[pl.BlockSpec((8, 128), lambda i: (0, 0))],
          out_shape=[jax.ShapeDtypeStruct((8, 128), F32)], ride=ride)


def _ride_gather(ride, w, n, axis, stage, part=(0, 1)):
    shape = w[n].shape
    piece = functools.partial(_piece, shape=shape, axis=axis)
    span = part[2] if len(part) > 2 else 1
    halves = [(2 * part[0] + t * span, 2 * part[1], span) for t in range(2)]

    def copies(ins, outs, send_sems, recv_sems, arriving):
        x, y, c, chips = _position()
        me, (xn, yn, dn) = 2 * x + y, [2 * px + py for px, py in chips]
        if stage == "near":
            plan = [(me, c, part, (1 - x, y, c), xn, c, part), (me, c, part, (x, 1 - y, c), yn, c, part)]
        else:
            plan = []
        if stage == "far":
            plan = [(yn, c, halves[1], (1 - x, y, c), dn, c, halves[1]), (xn, c, halves[0], (x, 1 - y, c), dn, c, halves[0])]
        to_sibling = {"far": (xn, yn), "pair": (dn,)}.get(stage, ())
        plan += [(j, c, part, (x, y, 1 - c), j, 1 - c, part) for j in to_sibling]
        out = []
        for k, (chip, h, rows, to, from_chip, from_h, from_rows) in enumerate(plan):
            if arriving:
                lands = piece(outs[0], j=from_chip, h=from_h, part=from_rows)
                out.append(_remote(lands, lands, send_sems, recv_sems, k, to))
            else:
                out.append(_remote(piece(ins[0], j=chip, h=h, part=rows), piece(outs[0], j=chip, h=h, part=rows),
                                   send_sems, recv_sems, k, to))
        return out

    def start(*refs):
        for cp in copies(*refs, arriving=False):
            cp.start()

    def finish(*refs):
        for cp in copies(*refs, arriving=True):
            cp.wait_recv()
        for cp in copies(*refs, arriving=False):
            cp.wait_send()

    ride.add([w[n]], [jax.ShapeDtypeStruct(shape, w[n].dtype)], {0: 0}, 4, start, finish,
             lambda outs: w.__setitem__(n, outs[0]))


def _ride_pair(ride, st, axis):
    shape = st["g16"].shape
    pr, pc = (shape[0], shape[1] // N_CHIPS) if st.get("half") else _piece_dims(shape, axis)

    def copies(ins, outs, send_sems, recv_sems):
        x, y, c, _ = _position()
        if st.get("half"):
            pieces = [ins[0].at[:, pl.ds(j * pc, pc)] for j in range(N_CHIPS)]
        else:
            pieces = [_piece(ins[0], shape, axis, j, 1 - c) for j in range(N_CHIPS)]
        return [_remote(pieces[j], outs[0].at[j], send_sems, recv_sems, j, (x, y, 1 - c)) for j in range(N_CHIPS)]

    def start(*refs):
        for cp in copies(*refs):
            cp.start()

    def finish(*refs):
        for cp in copies(*refs):
            cp.wait()

    ride.add([st["g16"]], [jax.ShapeDtypeStruct((N_CHIPS, pr, pc), BF16)], {}, N_CHIPS, start, finish,
             lambda outs: st.__setitem__("sib", outs[0]))


def _ride_chips(ride, st, rows=None):
    _, pr, pc = st["s16"].shape
    r0, nr = (0, pr) if rows is None else rows

    def copies(ins, outs, send_sems, recv_sems):
        x, y, c, chips = _position()
        return [_remote(ins[0].at[2 * px + py, pl.ds(r0, nr), :], outs[0].at[k, pl.ds(r0, nr), :],
                        send_sems, recv_sems, k, (px, py, c)) for k, (px, py) in enumerate(chips)]

    def start(*refs):
        for cp in copies(*refs):
            cp.start()

    def finish(*refs):
        for cp in copies(*refs):
            cp.wait()

    ins, aliases = ([st["s16"], st["recv"]], {1: 0}) if "recv" in st else ([st["s16"]], {})
    ride.add(ins, [jax.ShapeDtypeStruct((3, pr, pc), BF16)], aliases, 3, start, finish,
             lambda outs: st.__setitem__("recv", outs[0]))


def _ride_share(ride, st):
    def sent(ins, outs, send_sems, recv_sems):
        x, y, c, _ = _position()
        return _remote(_shard_half(ins[0], c), _shard_half(outs[0], c), send_sems, recv_sems, 0, (x, y, 1 - c))

    def landed(ins, outs, send_sems, recv_sems):
        x, y, c, _ = _position()
        other = _shard_half(outs[0], 1 - c)
        return _remote(other, other, send_sems, recv_sems, 0, (x, y, 1 - c))

    def start(*refs):
        sent(*refs).start()

    def finish(*refs):
        landed(*refs).wait_recv()
        sent(*refs).wait_send()

    ride.add([st["shard"]], [jax.ShapeDtypeStruct(st["shard"].shape, F32)], {0: 0}, 1, start, finish,
             lambda outs: st.__setitem__("g", outs[0]))


def _piece_block(axis, nr, nc, chip, half=False):
    if half:
        return lambda *a: (a[-3], chip(a) * nc + a[-2])
    if axis == 1:
        return lambda *a: ((a[-1][1] * nr + a[-3]), chip(a) * nc + a[-2])
    return lambda *a: ((2 * chip(a) + a[-1][1]) * nr + a[-3], a[-2])


def _pair_add(g32, sib, axis, pos, *, name, half=False):
    _, pr, pc = sib.shape
    tr, tc = _slab_tiles(pr, pc)
    nr, nc = pr // tr, pc // tc

    def body(pos_ref, g_ref, b_ref, o16_ref):
        o16_ref[0] = (g_ref[...] + b_ref[0].astype(F32)).astype(o16_ref.dtype)

    other = lambda a: (a[-1][0] + 1 + a[0]) % N_CHIPS
    blk = pl.BlockSpec((1, tr, tc), lambda *a: (other(a), a[1], a[2]))
    return _prefetch_call(body, pos, [g32, sib], [pl.BlockSpec((tr, tc), _piece_block(axis, nr, nc, other, half)), blk],
                          jax.ShapeDtypeStruct(sib.shape, BF16), blk, (N_CHIPS - 1, nr, nc), name=name)


def _chip_sum(g32, sib, recv, axis, pos, *, name, half=False):
    _, pr, pc = sib.shape
    tr, tc = _slab_tiles(pr, pc)
    nr, nc = pr // tr, pc // tc

    def body(pos_ref, g_ref, b_ref, r_ref, o_ref):
        pair = g_ref[...] + b_ref[0].astype(F32)
        o_ref[...] = ((pair + r_ref[0].astype(F32)) + r_ref[1].astype(F32)) + r_ref[2].astype(F32)

    return _prefetch_call(
        body, pos, [g32, sib, recv],
        [pl.BlockSpec((tr, tc), _piece_block(axis, nr, nc, lambda a: a[-1][0], half)),
         pl.BlockSpec((1, tr, tc), lambda i, k, pos_ref: (pos_ref[0], i, k)),
         pl.BlockSpec((3, tr, tc), lambda i, k, pos_ref: (0, i, k))],
        jax.ShapeDtypeStruct((2 * pr, pc), F32),
        pl.BlockSpec((tr, tc), lambda i, k, pos_ref: (pos_ref[1] * nr + i, k)), (nr, nc), name=name)


class _Comm:
    def __init__(self, pos, w):
        self.pos, self.w, self.st = pos, w, {n: {} for n, _ in BIG}

    def gather(self, names, stage, ride=None, part=(0, 1)):
        ride = _Ride() if ride is None else ride
        for n in names:
            _ride_gather(ride, self.w, n, AXIS[n], stage, part)
        return ride

    def grad(self, n, g32, g16, half=False):
        self.st[n].update(g32=g32, g16=g16, half=half)

    def pair(self, names, ride=None):
        ride = _Ride() if ride is None else ride
        for n in names:
            _ride_pair(ride, self.st[n], AXIS[n])
        return ride

    def add(self, names):
        for n in names:
            st = self.st[n]
            st["s16"] = _pair_add(st["g32"], st["sib"], AXIS[n], self.pos, name="rs_add_" + n, half=st["half"])

    def chips(self, names, ride=None, rows=None):
        ride = _Ride() if ride is None else ride
        for n in names:
            _ride_chips(ride, self.st[n], rows)
        return ride

    def sum(self, names):
        for n in names:
            st = self.st[n]
            st["shard"] = _chip_sum(st["g32"], st["sib"], st["recv"], AXIS[n], self.pos, name="rs_sum_" + n,
                                    half=st["half"])

    def share(self, names, ride=None):
        ride = _Ride() if ride is None else ride
        for n in names:
            _ride_share(ride, self.st[n])
        return ride

    def tail(self, count):
        st = self.st["w_in"]
        rows, at = st["s16"].shape[1], st.get("at", 0)
        st["at"] = at + count
        return self.chips(("w_in",), rows=(at * rows // TAIL_PARTS, count * rows // TAIL_PARTS))

    def tail_rest(self):
        return self.tail(TAIL_PARTS - self.st["w_in"].get("at", 0))

    def result(self, n):
        return self.st[n]["g"]


class _NoComm:
    pos = None

    def __init__(self, w):
        self.w, self.st = w, {}

    def grad(self, n, g32, g16, half=False):
        self.st[n] = (g32, g16)

    def result(self, n):
        return self.st[n]

    def add(self, names):
        pass

    sum = add

    def gather(self, names, *args, **kwargs):
        return None

    pair = chips = share = tail = gather


def _small_all_reduce(vec, *, name):
    r = vec.shape[0]

    def body(vec_ref, out_ref, slots, send_sems, recv_sems):
        x, y, c, _ = _position()
        me = 4 * x + 2 * y + c
        slots[me] = vec_ref[...]
        sends = []
        for k in range(1, 8):
            to = (x ^ (k >> 2), y ^ ((k >> 1) & 1), c ^ (k & 1))
            cp = _remote(slots.at[me], slots.at[me], send_sems, recv_sems, k - 1, to)
            cp.start()
            sends.append(cp)
        for k in range(1, 8):
            frm = 4 * (x ^ (k >> 2)) + 2 * (y ^ ((k >> 1) & 1)) + (c ^ (k & 1))
            _remote(slots.at[frm], slots.at[frm], send_sems, recv_sems, k - 1, (x, y, c)).wait_recv()
        for cp in sends:
            cp.wait_send()
        total = slots[0]
        for d in range(1, 8):
            total = total + slots[d]
        out_ref[...] = total

    return pl.pallas_call(
        body, name=name,
        in_specs=[pl.BlockSpec(memory_space=pltpu.VMEM)], out_specs=pl.BlockSpec(memory_space=pltpu.VMEM),
        out_shape=jax.ShapeDtypeStruct((r, 128), F32),
        scratch_shapes=[pltpu.VMEM((8, r, 128), F32), pltpu.SemaphoreType.DMA((7,)), pltpu.SemaphoreType.DMA((7,))],
    )(vec)


SC_TILES = 32
SC_LANES = 16
SC_TILE_BUDGET = 400 * 1024


def _adamw_update(wv, gv, mv, vv):
    nm = ADAM_B1 * mv + (1.0 - ADAM_B1) * gv
    nv = ADAM_B2 * vv + (1.0 - ADAM_B2) * (gv * gv)
    m_hat = nm / (1.0 - ADAM_B1 ** ADAM_STEP)
    v_hat = nv / (1.0 - ADAM_B2 ** ADAM_STEP)
    return -ADAM_LR * (m_hat / (jnp.sqrt(v_hat) + ADAM_EPS) + ADAM_WD * wv), nm, nv


def _adamw_sc(w, g, m, v, *, name):
    r, c = w.shape
    groups = r // 8
    per_tile = -(-groups // SC_TILES)
    cb = c if 4 * 8 * c * 4 <= SC_TILE_BUDGET else _pick(c, (2048, 1024, 512, 256, 128))

    def body(w_hbm, g_hbm, m_hbm, v_hbm, go_hbm, d_hbm, nm_hbm, nv_hbm, wb, gb, mb, vb):
        tile = lax.axis_index("sc_tile") * 2 + lax.axis_index("sc_core")

        def update(group):
            for c0 in range(0, c, cb):
                at = (pl.ds(group * 8, 8), pl.ds(c0, cb))
                for hbm, buf in ((w_hbm, wb), (g_hbm, gb), (m_hbm, mb), (v_hbm, vb)):
                    pltpu.sync_copy(hbm.at[at], buf)
                pltpu.sync_copy(gb, go_hbm.at[at])

                @pl.loop(0, 8)
                def _(rr):
                    @pl.loop(0, cb, step=SC_LANES)
                    def _(i):
                        lanes = (rr, pl.ds(i, SC_LANES))
                        wb[lanes], mb[lanes], vb[lanes] = _adamw_update(wb[lanes], gb[lanes], mb[lanes], vb[lanes])

                for buf, hbm in ((wb, d_hbm), (mb, nm_hbm), (vb, nv_hbm)):
                    pltpu.sync_copy(buf, hbm.at[at])

        @pl.loop(0, per_tile)
        def _(k):
            group = k * SC_TILES + tile
            if groups % SC_TILES:
                pl.when(group < groups)(lambda: update(group))
            else:
                update(group)

    sd = jax.ShapeDtypeStruct((r, c), F32)
    return pl.kernel(body, name=name, out_type=[sd, sd, sd, sd],
                     mesh=plsc.VectorSubcoreMesh(core_axis_name="sc_core", subcore_axis_name="sc_tile"),
                     scratch_types=[pltpu.VMEM((8, cb), F32)] * 4)(w, g, m, v)


def _adamw(w, g, m, v, *, name, ride=None):
    r, c = w.shape
    tc = c if c <= 4096 else _pick(c, (2048, 1024, 512, 256, 128))
    tr = next(t for t in (512, 256, 128, 64, 32, 16, 8) if r % t == 0 and t * tc <= 256 * 1024)

    def body(w_ref, g_ref, m_ref, v_ref, go_ref, d_ref, nm_ref, nv_ref):
        go_ref[...] = g_ref[...]
        d_ref[...], nm_ref[...], nv_ref[...] = _adamw_update(w_ref[...], g_ref[...], m_ref[...], v_ref[...])

    blk = ((tr, tc), lambda i, j: (i, j))
    sd = jax.ShapeDtypeStruct((r, c), F32)
    return _ew(body, [w, g, m, v], [blk] * 4, [sd] * 4, [blk] * 4, (r // tr, c // tc), name=name, ride=ride)


BIG = (("w_in", 1), ("w_sb_out", 1), ("w_ca_out", 1), ("w_mix_out", 0), ("w_ffn_in", 1), ("w_ffn_out", 0),
       ("w_ple_in", 1), ("w_ple_gate", 0))
AXIS = dict(BIG)
HEAD_PARTS = 8
HEAD_HOSTS = ("w_ffn_in", "w_ffn_out")
TAIL_PARTS = 16
TAIL_SECOND = 5
TAIL_FIRST = 2
ON_SPARSECORE = tuple(n for n, _ in BIG if n != "w_in")
SMALL = ("rel_bias", "g_mix", "g_ffn", "g_ple", "g_final")
ORDER = ("w_in", "w_sb_out", "w_ca_out", "w_mix_out", "rel_bias", "g_mix", "g_ffn", "g_ple", "g_final",
         "w_ffn_in", "w_ffn_out", "w_ple_in", "w_ple_gate")


def _pack(parts):
    flat = jnp.concatenate([a.reshape(-1) for a in parts])
    rows = -(-flat.shape[0] // 1024) * 8
    return jnp.pad(flat, (0, rows * 128 - flat.shape[0])).reshape(rows, 128)


def _unpack(packed, like):
    flat, out, at = packed.reshape(-1), [], 0
    for a in like:
        out.append(flat[at:at + a.size].reshape(a.shape))
        at += a.size
    return out


def kernel(x, p, w_in, w_sb_out, w_ca_out, w_mix_out, rel_bias, g_mix, g_ffn, g_ple, g_final, w_ffn_in, w_ffn_out, w_ple_in, w_ple_gate, loss_target, m_w_in, m_w_sb_out, m_w_ca_out, m_w_mix_out, m_rel_bias, m_g_mix, m_g_ffn, m_g_ple, m_g_final, m_w_ffn_in, m_w_ffn_out, m_w_ple_in, m_w_ple_gate, v_w_in, v_w_sb_out, v_w_ca_out, v_w_mix_out, v_rel_bias, v_g_mix, v_g_ffn, v_g_ple, v_g_final, v_w_ffn_in, v_w_ffn_out, v_w_ple_in, v_w_ple_gate):
    weights = dict(w_in=w_in, w_sb_out=w_sb_out, w_ca_out=w_ca_out, w_mix_out=w_mix_out, rel_bias=rel_bias,
                   g_mix=g_mix, g_ffn=g_ffn, g_ple=g_ple, g_final=g_final, w_ffn_in=w_ffn_in,
                   w_ffn_out=w_ffn_out, w_ple_in=w_ple_in, w_ple_gate=w_ple_gate)
    m_in = dict(w_in=m_w_in, w_sb_out=m_w_sb_out, w_ca_out=m_w_ca_out, w_mix_out=m_w_mix_out, rel_bias=m_rel_bias,
                g_mix=m_g_mix, g_ffn=m_g_ffn, g_ple=m_g_ple, g_final=m_g_final, w_ffn_in=m_w_ffn_in,
                w_ffn_out=m_w_ffn_out, w_ple_in=m_w_ple_in, w_ple_gate=m_w_ple_gate)
    v_in = dict(w_in=v_w_in, w_sb_out=v_w_sb_out, w_ca_out=v_w_ca_out, w_mix_out=v_w_mix_out, rel_bias=v_rel_bias,
                g_mix=v_g_mix, g_ffn=v_g_ffn, g_ple=v_g_ple, g_final=v_g_final, w_ffn_in=v_w_ffn_in,
                w_ffn_out=v_w_ffn_out, w_ple_in=v_w_ple_in, w_ple_gate=v_w_ple_gate)

    pos = jnp.stack([2 * lax.axis_index("x") + lax.axis_index("y"), lax.axis_index("c")]).astype(jnp.int32)
    comm = _Comm(pos, {"w_in": _cast_place(w_in[0], AXIS["w_in"], pos, name="cast_w_in")})
    at = 0
    for n in HEAD_HOSTS:
        ride = comm.gather(("w_in",), "near", part=(at, HEAD_PARTS))
        comm.w[n] = _cast_place(weights[n][0], AXIS[n], pos, name="cast_" + n, ride=ride)
        at += 1
    for n, axis in BIG:
        if n not in comm.w:
            comm.w[n] = _cast_place(weights[n][0], axis, pos, name="cast_" + n)
    _run(comm.gather(("w_in",), "near", part=(at, HEAD_PARTS, HEAD_PARTS - at)), name="gather_w_in_near")
    _run(comm.gather(("w_in",), "far"), name="gather_w_in_far")
    _run(comm.gather(("w_in",), "pair"), name="gather_w_in_pair")
    small = dict(rel_bias=rel_bias[0], g_mix=g_mix, g_ffn=g_ffn, g_ple=g_ple, g_final=g_final.reshape(1, -1))
    loss, grad_x, gs = _step(x[0], p[0, 0], loss_target[0], small, comm)

    grads, delta, new_m, new_v = {}, {}, {}, {}
    for n in [n for n, _ in BIG if n != "w_in"] + ["w_in"]:
        if n == "w_in":
            _run(comm.tail_rest(), name="rs_chips_w_in")
            comm.sum(("w_in",))
            _run(comm.share(("w_in",)), name="rs_share_w_in")
        update = _adamw_sc if n in ON_SPARSECORE else _adamw
        g, d, nm, nv = update(weights[n][0], comm.result(n), m_in[n][0], v_in[n][0], name="adamw_" + n)
        grads[n], delta[n], new_m[n], new_v[n] = g[None], d[None], nm[None], nv[None]

    like = [weights[n] for n in SMALL]
    reduced = _small_all_reduce(_pack([gs[n] for n in SMALL] + [loss[:, :1]]), name="small_all_reduce")
    g_small = _unpack(reduced, like + [loss[:, :1]])
    total_loss = g_small[-1].reshape(())
    g_packed = _pack(g_small[:-1])
    _, d_s, m_s, v_s = _adamw(_pack(like), g_packed, _pack([m_in[n] for n in SMALL]), _pack([v_in[n] for n in SMALL]),
                           name="adamw_small")
    for n, g, d, nm, nv in zip(SMALL, g_small[:-1], _unpack(d_s, like), _unpack(m_s, like), _unpack(v_s, like)):
        grads[n], delta[n], new_m[n], new_v[n] = g, d, nm, nv

    return (total_loss, grad_x[None], *[grads[n] for n in ORDER], *[delta[n] for n in ORDER],
            *[new_m[n] for n in ORDER], *[new_v[n] for n in ORDER])
```

```python
import functools
import math

import jax
import jax.numpy as jnp
from jax import lax
from jax.experimental import pallas as pl
from jax.experimental.pallas import tpu as pltpu
from jax.experimental.pallas import tpu_sc as plsc

F32 = jnp.float32
BF16 = jnp.bfloat16

HEAD_DIM = 128
CHUNK = 64
LEFT_CHUNKS = 8
REL_CLIP = 128
N_REL = REL_CLIP + CHUNK
BAND = (LEFT_CHUNKS + 2) * CHUNK
CA_PER_STEP = 4
CA_ROWS = CA_PER_STEP * CHUNK
CA_BAND = BAND + CA_PER_STEP * CHUNK
CA_PAD = BAND
SB_BLOCK = 128
SB_KEYS = 512
SB_GROUPS = SB_KEYS // SB_BLOCK
SB_ROWS = SB_KEYS
EPS = 1e-6
NEG = -1e30

ADAM_LR = 0.001
ADAM_B1 = 0.9
ADAM_B2 = 0.999
ADAM_EPS = 1e-08
ADAM_WD = 0.01
ADAM_STEP = 10

VMEM_LIMIT = 48 * 1024 * 1024
MM_VMEM_BUDGET = 36 * 1024 * 1024
V7X_HBM_BYTES_PER_S = 3.7e12
GRID_STEP_S = 0.35e-6
MESH = pl.DeviceIdType.MESH
N_CHIPS = 4


def _pick(dim, prefs):
    for t in prefs:
        if dim % t == 0:
            return t
    raise ValueError(f"no tile for {dim}")


def _cparams(sem=None):
    return pltpu.CompilerParams(dimension_semantics=sem, vmem_limit_bytes=VMEM_LIMIT)


def _sigmoid(v):
    return 1.0 / (1.0 + jnp.exp(-v))


def _dot(a, b, dims):
    return lax.dot_general(a, b, (dims, ((), ())), preferred_element_type=F32)


def _dot_nn(a, b):
    return _dot(a, b, ((1,), (0,)))


def _dot_nt(a, b):
    return _dot(a, b, ((1,), (1,)))


def _dot_tn(a, b):
    return _dot(a, b, ((0,), (0,)))


HBM = pl.BlockSpec(memory_space=pltpu.HBM)


class _Ride:
    def __init__(self):
        self.items = []

    def add(self, ins, outs, aliases, n_sems, start, finish, sink):
        self.items.append((ins, outs, aliases, n_sems, start, finish, sink))


def _call(body, args, *, name, grid, in_specs, out_specs, out_shape, scratch_shapes=(), sem=None, ride=None,
          scalars=None, onto=()):
    items = ride.items if ride is not None else []
    if onto:
        args, in_specs = list(args) + list(onto), list(in_specs) + [HBM] * len(onto)
        inner, body = body, lambda *refs: inner(*refs[:len(args) - len(onto)], *refs[len(args):])
    n_in, n_out, n_scr = len(args), len(out_shape), len(scratch_shapes)
    r_ins = [a for it in items for a in it[0]]
    r_outs = [o for it in items for o in it[1]]
    updated = [id(it[0][i]) for it in items for i in it[2]]
    assert len(set(updated)) == len(updated), "one call may update a buffer in place only once"
    aliases, a, b = {n_in - len(onto) + t: t for t in range(len(onto))}, n_in, n_out
    for it in items:
        aliases.update({a + i: b + o for i, o in it[2].items()})
        a, b = a + len(it[0]), b + len(it[1])
    sems = [pltpu.SemaphoreType.DMA((it[3],)) for it in items for _ in range(2)]

    def wrapped(*refs):
        head, refs = (refs[:1], refs[1:]) if scalars is not None else ((), refs)
        ins, rin = refs[:n_in], refs[n_in:n_in + len(r_ins)]
        at = n_in + len(r_ins)
        outs, rout = refs[at:at + n_out], refs[at + n_out:at + n_out + len(r_outs)]
        at += n_out + len(r_outs)
        scr, rsem = refs[at:at + n_scr], refs[at + n_scr:]

        def each(which):
            a = b = 0
            for q, it in enumerate(items):
                it[which](rin[a:a + len(it[0])], rout[b:b + len(it[1])], rsem[2 * q], rsem[2 * q + 1])
                a, b = a + len(it[0]), b + len(it[1])

        if items:
            ids = [pl.program_id(d) for d in range(len(grid))]
            first = functools.reduce(jnp.logical_and, [i == 0 for i in ids])
            last = functools.reduce(jnp.logical_and, [i == g - 1 for i, g in zip(ids, grid)])
            pl.when(first)(lambda: each(4))
        body(*head, *ins, *outs, *scr)
        if items:
            pl.when(last)(lambda: each(5))

    specs = dict(grid=grid, in_specs=list(in_specs) + [HBM] * len(r_ins),
                 out_specs=list(out_specs) + [HBM] * len(r_outs), scratch_shapes=list(scratch_shapes) + sems)
    if scalars is not None:
        specs = dict(grid_spec=pltpu.PrefetchScalarGridSpec(num_scalar_prefetch=1, **specs))
        aliases = {i + 1: o for i, o in aliases.items()}
    res = pl.pallas_call(
        wrapped, name=name, **specs,
        out_shape=list(out_shape) + r_outs,
        input_output_aliases=aliases,
        compiler_params=_cparams(("arbitrary",) * len(grid) if items else sem),
    )(*(() if scalars is None else (scalars,)), *args, *r_ins)
    b = n_out
    for it in items:
        it[6](res[b:b + len(it[1])])
        b += len(it[1])
    return list(res[:n_out])


def _mm_tiles(m, n_align, n, k, a_bytes, b_bytes, out_bytes):
    best = None
    tks = sorted({t for t in (k, k // 2, k // 4, 2048, 1024, 512, 256, 128) if t <= k and k % t == 0 and t % 128 == 0})
    for tm in (t for t in (2048, 1024, 512, 256, 128) if m % t == 0):
        for tn in (t for t in (2048, 1024, 512, 256, 128) if n_align % t == 0):
            for tk in tks:
                nk = k // tk
                vmem = 2 * (tm * tk * a_bytes + tk * tn * b_bytes + tm * tn * out_bytes) + tm * tn * 4
                if vmem > MM_VMEM_BUDGET:
                    continue
                traffic = m * k * a_bytes * (n // tn if nk > 1 else 1) + k * n * b_bytes * (m // tm)
                traffic += tm * tk * a_bytes + tk * tn * b_bytes + tm * tn * out_bytes
                traffic += m * n * 4 * nk if nk > 1 else 0
                cost = traffic / V7X_HBM_BYTES_PER_S + (m // tm) * (n // tn) * nk * GRID_STEP_S
                if best is None or cost < best[0]:
                    best = (cost, tm, tn, tk)
    return best[1:]


def _mm(a, b, mode, out_dtypes, *, name, n=None, b_col_off=0, resid=None, ride=None, rows=None, onto=(), m_half=None):
    if mode == "nn":
        m, k = a.shape
        n = b.shape[1] if n is None else n
    elif mode == "nt":
        m, k = a.shape
        n = b.shape[0]
    else:
        k, m = a.shape
        n = b.shape[1]
    if m_half is not None:
        m //= 2
    m_all, (row0, m) = m, (0, m) if rows is None else rows
    n_out = len(out_dtypes)
    has_resid = resid is not None
    out_bytes = sum(jnp.dtype(dt).itemsize for dt in out_dtypes) + (4 if has_resid else 0)
    tm, tn, tk = _mm_tiles(math.gcd(m, row0) if row0 else m, math.gcd(n, b_col_off) if b_col_off else n, n, k,
                           a.dtype.itemsize, b.dtype.itemsize, out_bytes)
    nk = k // tk
    boff, roff = b_col_off // tn, row0 // tm
    dot = {"nn": _dot_nn, "nt": _dot_nt, "tn": _dot_tn}[mode]
    if m_half is None:
        half = lambda: 0
    else:
        half = lambda pos_ref: (pos_ref[1] if m_half[0] else 1 - pos_ref[1]) * (m // tm)

    def body(*refs):
        refs = refs[m_half is not None:]
        a_ref, b_ref = refs[0], refs[1]
        r_ref = refs[2] if has_resid else None
        o_refs = refs[2 + has_resid: 2 + has_resid + n_out]

        def finish(r):
            if has_resid:
                r = r + r_ref[...]
            for o_ref in o_refs:
                o_ref[...] = r.astype(o_ref.dtype)

        part = dot(a_ref[...].astype(BF16), b_ref[...].astype(BF16))
        if nk == 1:
            finish(part)
            return
        acc_ref = refs[-1]
        kk = pl.program_id(2)

        @pl.when(kk == 0)
        def _():
            acc_ref[...] = part

        @pl.when(kk > 0)
        def _():
            acc_ref[...] += part

        @pl.when(kk == nk - 1)
        def _():
            finish(acc_ref[...])

    if mode == "nn":
        a_spec = pl.BlockSpec((tm, tk), lambda i, j, kk, *_: (i + roff, kk))
        b_spec = pl.BlockSpec((tk, tn), lambda i, j, kk, *_: (kk, j + boff))
    elif mode == "nt":
        a_spec = pl.BlockSpec((tm, tk), lambda i, j, kk, *_: (i + roff, kk))
        b_spec = pl.BlockSpec((tn, tk), lambda i, j, kk, *_: (j, kk))
    else:
        a_spec = pl.BlockSpec((tk, tm), lambda i, j, kk, *pos: (kk, i + half(*pos)))
        b_spec = pl.BlockSpec((tk, tn), lambda i, j, kk, *_: (kk, j))
    o_spec = pl.BlockSpec((tm, tn), lambda i, j, kk, *_: (i + roff, j))
    in_specs = [a_spec, b_spec] + ([o_spec] if has_resid else [])
    args = [a, b] + ([resid] if has_resid else [])
    outs = _call(
        body, args, name=name,
        grid=(m // tm, n // tn, nk),
        in_specs=in_specs,
        out_specs=[o_spec] * n_out,
        out_shape=[jax.ShapeDtypeStruct((m_all, n), dt) for dt in out_dtypes],
        scratch_shapes=[pltpu.VMEM((tm, tn), F32)] if nk > 1 else [],
        sem=("parallel", "parallel", "arbitrary"), ride=ride, onto=onto,
        scalars=None if m_half is None else m_half[1])
    return outs[0] if n_out == 1 else tuple(outs)


def _row_tile(s):
    return _pick(s, (256, 128))


def _rms_fwd(x, g, *, name, ride=None):
    s, d = x.shape
    tr = _row_tile(s)

    def body(x_ref, g_ref, o_ref):
        xv = x_ref[...]
        r = lax.rsqrt(jnp.mean(xv * xv, axis=1, keepdims=True) + EPS)
        o_ref[...] = (xv * r * g_ref[...]).astype(o_ref.dtype)

    return _call(
        body, [x, g], name=name, grid=(s // tr,),
        in_specs=[pl.BlockSpec((tr, d), lambda i: (i, 0)), pl.BlockSpec((1, d), lambda i: (0, 0))],
        out_specs=[pl.BlockSpec((tr, d), lambda i: (i, 0))],
        out_shape=[jax.ShapeDtypeStruct((s, d), BF16)], sem=("parallel",), ride=ride)[0]


def _rms_bwd(x, g, dh, dres, *, name, ride=None):
    s, d = x.shape
    tr = _row_tile(s)

    def body(x_ref, g_ref, dh_ref, dres_ref, dx_ref, dx16_ref, dg_ref):
        i = pl.program_id(0)
        xv = x_ref[...]
        r = lax.rsqrt(jnp.mean(xv * xv, axis=1, keepdims=True) + EPS)
        xhat = xv * r
        dhv = dh_ref[...].astype(F32)
        dxhat = dhv * g_ref[...]
        proj = jnp.mean(dxhat * xhat, axis=1, keepdims=True)
        dx = dres_ref[...] + r * (dxhat - xhat * proj)
        dx_ref[...] = dx
        dx16_ref[...] = dx.astype(dx16_ref.dtype)

        @pl.when(i == 0)
        def _():
            dg_ref[...] = jnp.zeros_like(dg_ref)

        dg_ref[...] += jnp.sum(dhv * xhat, axis=0, keepdims=True)

    row = pl.BlockSpec((tr, d), lambda i: (i, 0))
    vec = pl.BlockSpec((1, d), lambda i: (0, 0))
    return _call(
        body, [x, g, dh, dres], name=name, grid=(s // tr,),
        in_specs=[row, vec, row, row],
        out_specs=[row, row, vec],
        out_shape=[jax.ShapeDtypeStruct((s, d), F32), jax.ShapeDtypeStruct((s, d), BF16),
                   jax.ShapeDtypeStruct((1, d), F32)],
        sem=("arbitrary",), ride=ride)


def _ple_loss(x2, gate, pe, g, target, *, name):
    s, d = x2.shape
    tr = _row_tile(s)

    def body(x2_ref, gate_ref, pe_ref, g_ref, t_ref, dx_ref, dgate_ref, dpe_ref, dg_ref, loss_ref):
        i = pl.program_id(0)
        sg = _sigmoid(gate_ref[...].astype(F32))
        pv = pe_ref[...].astype(F32)
        xv = x2_ref[...] + sg * pv
        gv = g_ref[...]
        r = lax.rsqrt(jnp.mean(xv * xv, axis=1, keepdims=True) + EPS)
        xhat = xv * r
        err = xhat * gv - t_ref[...]
        dy = err * (1.0 / d)
        dxhat = dy * gv
        proj = jnp.mean(dxhat * xhat, axis=1, keepdims=True)
        dx = r * (dxhat - xhat * proj)
        dx_ref[...] = dx
        dgate_ref[...] = (dx * pv * sg * (1.0 - sg)).astype(dgate_ref.dtype)
        dpe_ref[...] = (dx * sg).astype(dpe_ref.dtype)

        @pl.when(i == 0)
        def _():
            dg_ref[...] = jnp.zeros_like(dg_ref)
            loss_ref[...] = jnp.zeros_like(loss_ref)

        dg_ref[...] += jnp.sum(dy * xhat, axis=0, keepdims=True)
        part = 0.5 * jnp.sum(jnp.mean(err * err, axis=1, keepdims=True), axis=0, keepdims=True)
        loss_ref[...] += jnp.broadcast_to(part, loss_ref.shape)

    row = pl.BlockSpec((tr, d), lambda i: (i, 0))
    vec = pl.BlockSpec((1, d), lambda i: (0, 0))
    return pl.pallas_call(
        body, name=name, grid=(s // tr,),
        in_specs=[row, row, row, vec, row],
        out_specs=[row, row, row, vec, pl.BlockSpec((1, 128), lambda i: (0, 0))],
        out_shape=[jax.ShapeDtypeStruct((s, d), F32), jax.ShapeDtypeStruct((s, d), BF16),
                   jax.ShapeDtypeStruct((s, d), BF16), jax.ShapeDtypeStruct((1, d), F32),
                   jax.ShapeDtypeStruct((1, 128), F32)],
        compiler_params=_cparams(("arbitrary",)),
    )(x2, gate, pe, g, target)


def _ew(body, ins, in_blocks, outs, out_blocks, grid, *, name, ride=None):
    return _call(body, ins, name=name, grid=grid,
                 in_specs=[pl.BlockSpec(bs, im) for bs, im in in_blocks],
                 out_specs=[pl.BlockSpec(bs, im) for bs, im in out_blocks],
                 out_shape=outs, sem=("parallel",) * len(grid), ride=ride)


def _gate_merge_fwd(gates, o_sb, o_ca, *, name, ride=None):
    s, d = o_sb.shape
    tr, tc = _row_tile(s), _pick(d, (1024, 512, 256, 128))
    nc = d // tc

    def body(gs_ref, gc_ref, os_ref, oc_ref, m_ref):
        f32 = lambda ref: ref[...].astype(F32)
        m = _sigmoid(f32(gs_ref)) * f32(os_ref) + _sigmoid(f32(gc_ref)) * f32(oc_ref)
        m_ref[...] = m.astype(m_ref.dtype)

    blk = ((tr, tc), lambda i, j: (i, j))
    return _ew(body, [gates, gates, o_sb, o_ca],
               [blk, ((tr, tc), lambda i, j: (i, j + nc)), blk, blk],
               [jax.ShapeDtypeStruct((s, d), BF16)], [blk], (s // tr, nc), name=name, ride=ride)[0]


def _gate_merge_bwd(dmerged, gates, o_sb, o_ca, *, name):
    s, d = o_sb.shape
    tr, tc = _row_tile(s), _pick(d, (1024, 512, 256, 128))
    nc = d // tc

    def body(dm_ref, gs_ref, gc_ref, os_ref, oc_ref, dgs_ref, dgc_ref, dos_ref, doc_ref):
        f32 = lambda ref: ref[...].astype(F32)
        dm = f32(dm_ref)
        ss = _sigmoid(f32(gs_ref))
        sc = _sigmoid(f32(gc_ref))
        dgs_ref[...] = (dm * f32(os_ref) * ss * (1.0 - ss)).astype(dgs_ref.dtype)
        dgc_ref[...] = (dm * f32(oc_ref) * sc * (1.0 - sc)).astype(dgc_ref.dtype)
        dos_ref[...] = (dm * ss).astype(dos_ref.dtype)
        doc_ref[...] = (dm * sc).astype(doc_ref.dtype)

    blk = ((tr, tc), lambda i, j: (i, j))
    sd = jax.ShapeDtypeStruct((s, d), BF16)
    return _ew(body, [dmerged, gates, gates, o_sb, o_ca],
               [blk, blk, ((tr, tc), lambda i, j: (i, j + nc)), blk, blk],
               [sd, sd, sd, sd], [blk, blk, blk, blk], (s // tr, nc), name=name)


def _swiglu_fwd(gu, *, name, ride=None):
    s, f2 = gu.shape
    f = f2 // 2
    tr, tc = 128, _pick(f, (512, 256, 128))

    def body(gu_ref, a_ref):
        for at in range(0, f, tc):
            gv = gu_ref[:, at:at + tc].astype(F32)
            a_ref[:, at:at + tc] = (gv * _sigmoid(gv) * gu_ref[:, f + at:f + at + tc].astype(F32)).astype(a_ref.dtype)

    row = lambda i: (i, 0)
    return _ew(body, [gu], [((tr, f2), row)], [jax.ShapeDtypeStruct((s, f), BF16)], [((tr, f), row)],
               (s // tr,), name=name, ride=ride)[0]


def _swiglu_bwd(dact, gu, *, name):
    s, f2 = gu.shape
    f = f2 // 2
    tr, tc = 128, _pick(f, (512, 256, 128))

    def body(da_ref, gu_ref, o_ref):
        for at in range(0, f, tc):
            da = da_ref[:, at:at + tc].astype(F32)
            gv = gu_ref[:, at:at + tc].astype(F32)
            sg = _sigmoid(gv)
            uv = gu_ref[:, f + at:f + at + tc].astype(F32)
            o_ref[:, at:at + tc] = (da * uv * sg * (1.0 + gv * (1.0 - sg))).astype(o_ref.dtype)
            o_ref[:, f + at:f + at + tc] = (da * gv * sg).astype(o_ref.dtype)

    row = lambda i: (i, 0)
    return _ew(body, [dact, gu], [((tr, f), row), ((tr, f2), row)], [jax.ShapeDtypeStruct((s, f2), BF16)],
               [((tr, f2), row)], (s // tr,), name=name)[0]


def _concat_cols(parts, *, name):
    s = parts[0].shape[0]
    widths = [p.shape[1] for p in parts]
    tr = 256

    def body(*refs):
        o_ref, at = refs[-1], 0
        for p_ref, width in zip(refs, widths):
            o_ref[:, at:at + width] = p_ref[...]
            at += width

    row = lambda i: (i, 0)
    return _ew(body, list(parts), [((tr, width), row) for width in widths],
               [jax.ShapeDtypeStruct((s, sum(widths)), parts[0].dtype)], [((tr, sum(widths)), row)],
               (s // tr,), name=name)[0]


def _sb_tri(later):
    row = lax.broadcasted_iota(jnp.int32, (SB_BLOCK, SB_BLOCK), 0)
    col = lax.broadcasted_iota(jnp.int32, (SB_BLOCK, SB_BLOCK), 1)
    tri = (row > col) if later else (row < col)
    return jnp.concatenate([tri.astype(BF16), jnp.ones((SB_BLOCK, SB_BLOCK), BF16)], axis=1)


def _sb_valid(i, j, own):
    if not own:
        return None
    qi = i * SB_ROWS + lax.broadcasted_iota(jnp.int32, (SB_ROWS, SB_KEYS), 0)
    ki = j * SB_KEYS + lax.broadcasted_iota(jnp.int32, (SB_ROWS, SB_KEYS), 1)
    return ki < qi


def _sb_scan(v, tri, run, later):
    hi = v.astype(BF16)
    lo = (v - hi.astype(F32)).astype(BF16)
    outs = [None] * SB_GROUPS
    for b in (reversed(range(SB_GROUPS)) if later else range(SB_GROUPS)):
        cols = slice(b * SB_BLOCK, (b + 1) * SB_BLOCK)
        r = _dot_nn(hi[:, cols], tri) + _dot_nn(lo[:, cols], tri)
        outs[b] = r[:, :SB_BLOCK] + run
        run = run + r[:, SB_BLOCK:]
    return jnp.concatenate(outs, axis=1), run


def _masked(valid, v):
    return v if valid is None else jnp.where(valid, v, 0.0)


def _sb_scores(q, kj, scale, valid):
    z = _dot_nt(q, kj) * scale
    t = jnp.log(1.0 + jnp.exp(-jnp.abs(z)))
    return jnp.minimum(z, 0.0) - t, _masked(valid, -jnp.maximum(z, 0.0) - t)


def _sb_specs(h_count, s, col0):
    q_spec = pl.BlockSpec((SB_ROWS, HEAD_DIM), lambda h, i: (i, col0 + h))
    k_spec = pl.BlockSpec((s, HEAD_DIM), lambda h, i: (0, col0 + h_count + h))
    v_spec = pl.BlockSpec((s, HEAD_DIM), lambda h, i: (0, col0 + 2 * h_count + h))
    return q_spec, k_spec, v_spec


def _sb_fwd(qkv, n_heads, col0, *, name, ride=None):
    s = qkv.shape[0]
    nq = s // SB_ROWS
    scale = HEAD_DIM ** -0.5

    def body(q_ref, k_ref, v_ref, o_ref):
        i = pl.program_id(1)
        q = q_ref[...]
        tri = _sb_tri(later=True)

        def step(j, carry, own):
            run, acc = carry
            off = pl.multiple_of(j * SB_KEYS, SB_KEYS)
            valid = _sb_valid(i, j, own)
            ls, lk = _sb_scores(q, k_ref[pl.ds(off, SB_KEYS), :], scale, valid)
            between, run = _sb_scan(lk, tri, run, later=True)
            a = _masked(valid, jnp.exp(ls + between))
            return run, acc + _dot_nn(a.astype(BF16), v_ref[pl.ds(off, SB_KEYS), :])

        carry = step(i, (jnp.zeros((SB_ROWS, SB_BLOCK), F32), jnp.zeros((SB_ROWS, HEAD_DIM), F32)), True)
        _, acc = lax.fori_loop(0, i, lambda jj, c: step(i - 1 - jj, c, False), carry)
        o_ref[...] = acc.astype(o_ref.dtype)

    q_spec, k_spec, v_spec = _sb_specs(n_heads, s, col0)
    return _call(
        body, [qkv, qkv, qkv], name=name, grid=(n_heads, nq),
        in_specs=[q_spec, k_spec, v_spec],
        out_specs=[pl.BlockSpec((SB_ROWS, HEAD_DIM), lambda h, i: (i, h))],
        out_shape=[jax.ShapeDtypeStruct((s, n_heads * HEAD_DIM), BF16)],
        sem=("parallel", "arbitrary"), ride=ride)[0]


def _sb_bwd(qkv, dy, n_heads, col0, *, name, ride=None):
    s = qkv.shape[0]
    nq = s // SB_ROWS
    scale = HEAD_DIM ** -0.5

    def body(q_ref, k_ref, v_ref, dy_ref, dq_ref, dk_ref, dv_ref, e_scr, sg_scr, dk_acc, dv_acc):
        i = pl.program_id(1)
        q = q_ref[...]
        dyv = dy_ref[...]

        @pl.when(i == 0)
        def _():
            dk_acc[...] = jnp.zeros_like(dk_acc)
            dv_acc[...] = jnp.zeros_like(dv_acc)

        tri_later = _sb_tri(later=True)

        def pass1(j, run, own):
            off = pl.multiple_of(j * SB_KEYS, SB_KEYS)
            valid = _sb_valid(i, j, own)
            ls, lk = _sb_scores(q, k_ref[pl.ds(off, SB_KEYS), :], scale, valid)
            between, run = _sb_scan(lk, tri_later, run, later=True)
            a = _masked(valid, jnp.exp(ls + between))
            e_scr[j] = a * _dot_nt(dyv, v_ref[pl.ds(off, SB_KEYS), :])
            sg_scr[j] = jnp.exp(ls)
            dv_acc[pl.ds(off, SB_KEYS), :] += _dot_tn(a.astype(BF16), dyv)
            return run

        lax.fori_loop(0, i, lambda jj, run: pass1(i - 1 - jj, run, False),
                      pass1(i, jnp.zeros((SB_ROWS, SB_BLOCK), F32), True))

        tri_earlier = _sb_tri(later=False)

        def pass2(j, carry, own):
            run, dq = carry
            off = pl.multiple_of(j * SB_KEYS, SB_KEYS)
            kj = k_ref[pl.ds(off, SB_KEYS), :]
            sg = sg_scr[j]
            e = e_scr[j]
            before, run = _sb_scan(e, tri_earlier, run, later=False)
            dz = _masked(_sb_valid(i, j, own), e * (1.0 - sg) - sg * before) * scale
            dzb = dz.astype(BF16)
            dk_acc[pl.ds(off, SB_KEYS), :] += _dot_tn(dzb, q)
            return run, dq + _dot_nn(dzb, kj)

        init = (jnp.zeros((SB_ROWS, SB_BLOCK), F32), jnp.zeros((SB_ROWS, HEAD_DIM), F32))
        _, dq = pass2(i, lax.fori_loop(0, i, lambda j, c: pass2(j, c, False), init), True)
        dq_ref[...] = dq.astype(dq_ref.dtype)

        @pl.when(i == nq - 1)
        def _():
            dk_ref[...] = dk_acc[...].astype(dk_ref.dtype)
            dv_ref[...] = dv_acc[...].astype(dv_ref.dtype)

    q_spec, k_spec, v_spec = _sb_specs(n_heads, s, col0)
    blk = pl.BlockSpec((SB_ROWS, HEAD_DIM), lambda h, i: (i, h))
    full = pl.BlockSpec((s, HEAD_DIM), lambda h, i: (0, h))
    sd = jax.ShapeDtypeStruct((s, n_heads * HEAD_DIM), BF16)
    return _call(
        body, [qkv, qkv, qkv, dy], name=name, grid=(n_heads, nq),
        in_specs=[q_spec, k_spec, v_spec, blk],
        out_specs=[blk, full, full],
        out_shape=[sd, sd, sd],
        scratch_shapes=[pltpu.VMEM((s // SB_KEYS, SB_ROWS, SB_KEYS), F32), pltpu.VMEM((s // SB_KEYS, SB_ROWS, SB_KEYS), F32),
                        pltpu.VMEM((s, HEAD_DIM), F32), pltpu.VMEM((s, HEAD_DIM), F32)],
        sem=("parallel", "arbitrary"), ride=ride)


def _band_bias(rel_bias):
    h = rel_bias.shape[0]
    width = BAND + CHUNK
    first = width - 1 - N_REL
    line = jnp.concatenate([jnp.broadcast_to(rel_bias[:, :1], (h, first)), rel_bias], axis=1)
    tiled = jnp.broadcast_to(line[:, None, :], (h, CHUNK, width - 1)).reshape(h, CHUNK * (width - 1))
    skew = jnp.pad(tiled, ((0, 0), (0, CHUNK))).reshape(h, CHUNK, width)[:, ::-1, :BAND]
    seen = jnp.arange(BAND) >= CHUNK
    return jnp.where(seen[None, None, :], skew, NEG)


def _band_bias_grad(dbias):
    h = dbias.shape[0]
    width = BAND + CHUNK
    flipped = jnp.pad(dbias[:, ::-1, :], ((0, 0), (0, 0), (0, CHUNK)))
    skew = flipped.reshape(h, CHUNK * width)[:, :CHUNK * (width - 1)].reshape(h, CHUNK, width - 1)
    diag = jnp.sum(skew, axis=1)
    first = width - 1 - N_REL
    clipped = jnp.sum(diag[:, :first + 1], axis=1, keepdims=True)
    return jnp.concatenate([clipped, diag[:, first + 1:]], axis=1)


def _group_bias(band):
    return jnp.concatenate([jnp.pad(band, ((0, 0), (0, 0), ((u + 1) * CHUNK, (CA_PER_STEP - 1 - u) * CHUNK)),
                                    constant_values=NEG) for u in range(CA_PER_STEP)], axis=1)


def _group_bias_grad(dgroup):
    return sum(dgroup[:, u * CHUNK:(u + 1) * CHUNK, (u + 1) * CHUNK:(u + 1) * CHUNK + BAND] for u in range(CA_PER_STEP))


def _ca_load_padded(k_ref, v_ref, kp, vp, s):
    kp[pl.ds(0, CA_PAD), :] = jnp.zeros((CA_PAD, HEAD_DIM), kp.dtype)
    vp[pl.ds(0, CA_PAD), :] = jnp.zeros((CA_PAD, HEAD_DIM), vp.dtype)
    kp[pl.ds(CA_PAD, s), :] = k_ref[...]
    vp[pl.ds(CA_PAD, s), :] = v_ref[...]


def _ca_weights(q, kb, bias, off, scale):
    z = _dot_nt(q, kb) * scale + bias
    pos = off + lax.broadcasted_iota(jnp.int32, (CA_ROWS, CA_BAND), 1)
    z = jnp.where(pos >= CA_PAD, z, NEG)
    p = jnp.exp(z - jnp.max(z, axis=1, keepdims=True))
    return p / jnp.sum(p, axis=1, keepdims=True)


def _ca_specs(h_count, s, col0):
    q_spec = pl.BlockSpec((CA_ROWS, HEAD_DIM), lambda h, c: (c, col0 + h))
    k_spec = pl.BlockSpec((s, HEAD_DIM), lambda h, c: (0, col0 + h_count + h))
    v_spec = pl.BlockSpec((s, HEAD_DIM), lambda h, c: (0, col0 + 2 * h_count + h))
    b_spec = pl.BlockSpec((1, CA_ROWS, CA_BAND), lambda h, c: (h, 0, 0))
    return q_spec, k_spec, v_spec, b_spec


def _ca_fwd(qkv, bias, n_heads, col0, *, name, ride=None):
    s = qkv.shape[0]
    nc = s // CA_ROWS
    scale = HEAD_DIM ** -0.5

    def body(q_ref, k_ref, v_ref, b_ref, o_ref, kp, vp):
        c = pl.program_id(1)

        @pl.when(c == 0)
        def _():
            _ca_load_padded(k_ref, v_ref, kp, vp, s)

        off = pl.multiple_of(c * CA_ROWS, CA_ROWS)
        w = _ca_weights(q_ref[...], kp[pl.ds(off, CA_BAND), :], b_ref[0], off, scale)
        o_ref[...] = _dot_nn(w.astype(BF16), vp[pl.ds(off, CA_BAND), :]).astype(o_ref.dtype)

    q_spec, k_spec, v_spec, b_spec = _ca_specs(n_heads, s, col0)
    return _call(
        body, [qkv, qkv, qkv, bias], name=name, grid=(n_heads, nc),
        in_specs=[q_spec, k_spec, v_spec, b_spec],
        out_specs=[pl.BlockSpec((CA_ROWS, HEAD_DIM), lambda h, c: (c, h))],
        out_shape=[jax.ShapeDtypeStruct((s, n_heads * HEAD_DIM), BF16)],
        scratch_shapes=[pltpu.VMEM((s + CA_PAD, HEAD_DIM), BF16), pltpu.VMEM((s + CA_PAD, HEAD_DIM), BF16)],
        sem=("parallel", "arbitrary"), ride=ride)[0]


def _ca_bwd(qkv, bias, dy, n_heads, col0, *, name, ride=None):
    s = qkv.shape[0]
    nc = s // CA_ROWS
    scale = HEAD_DIM ** -0.5

    def body(q_ref, k_ref, v_ref, b_ref, dy_ref, dq_ref, dk_ref, dv_ref, db_ref, kp, vp, dkp, dvp):
        c = pl.program_id(1)

        @pl.when(c == 0)
        def _():
            _ca_load_padded(k_ref, v_ref, kp, vp, s)
            dkp[...] = jnp.zeros_like(dkp)
            dvp[...] = jnp.zeros_like(dvp)
            db_ref[...] = jnp.zeros_like(db_ref)

        off = pl.multiple_of(c * CA_ROWS, CA_ROWS)
        band = pl.ds(off, CA_BAND)
        q = q_ref[...]
        dyv = dy_ref[...]
        kb = kp[band, :]
        w = _ca_weights(q, kb, b_ref[0], off, scale)
        dw = _dot_nt(dyv, vp[band, :])
        dvp[band, :] += _dot_tn(w.astype(BF16), dyv)
        dz = w * (dw - jnp.sum(w * dw, axis=1, keepdims=True))
        db_ref[0] += dz
        dzs = (dz * scale).astype(BF16)
        dq_ref[...] = _dot_nn(dzs, kb).astype(dq_ref.dtype)
        dkp[band, :] += _dot_tn(dzs, q)

        @pl.when(c == nc - 1)
        def _():
            dk_ref[...] = dkp[pl.ds(CA_PAD, s), :].astype(dk_ref.dtype)
            dv_ref[...] = dvp[pl.ds(CA_PAD, s), :].astype(dv_ref.dtype)

    q_spec, k_spec, v_spec, b_spec = _ca_specs(n_heads, s, col0)
    blk = pl.BlockSpec((CA_ROWS, HEAD_DIM), lambda h, c: (c, h))
    full = pl.BlockSpec((s, HEAD_DIM), lambda h, c: (0, h))
    sd = jax.ShapeDtypeStruct((s, n_heads * HEAD_DIM), BF16)
    return _call(
        body, [qkv, qkv, qkv, bias, dy], name=name, grid=(n_heads, nc),
        in_specs=[q_spec, k_spec, v_spec, b_spec, blk],
        out_specs=[blk, full, full, b_spec],
        out_shape=[sd, sd, sd, jax.ShapeDtypeStruct((n_heads, CA_ROWS, CA_BAND), F32)],
        scratch_shapes=[pltpu.VMEM((s + CA_PAD, HEAD_DIM), BF16), pltpu.VMEM((s + CA_PAD, HEAD_DIM), BF16),
                        pltpu.VMEM((s + CA_PAD, HEAD_DIM), F32), pltpu.VMEM((s + CA_PAD, HEAD_DIM), F32)],
        sem=("parallel", "arbitrary"), ride=ride)


EARLY = ("w_sb_out", "w_ca_out", "w_mix_out")


def _step(x, p, target, small, comm):
    w = comm.w
    d = x.shape[1]
    n_sb = w["w_sb_out"].shape[0] // HEAD_DIM
    n_ca = w["w_ca_out"].shape[0] // HEAD_DIM
    qkv_cols = 3 * HEAD_DIM * (n_sb + n_ca)
    ca_col0 = 3 * n_sb
    both = (F32, BF16)

    h1 = _rms_fwd(x, small["g_mix"], name="rms_mix")
    ffn, ple = ("w_ffn_in",), ("w_ple_gate", "w_ple_in")
    qkv = _mm(h1, w["w_in"], "nn", (BF16,), name="proj_qkv", n=qkv_cols, ride=comm.gather(EARLY, "near"))
    gates = _mm(h1, w["w_in"], "nn", (BF16,), name="proj_gates", n=2 * d, b_col_off=qkv_cols,
                ride=comm.gather(ffn, "near", comm.gather(EARLY, "far"), (0, 8)))
    bias = _group_bias(_band_bias(small["rel_bias"]))
    y_sb = _sb_fwd(qkv, n_sb, 0, name="sb_fwd", ride=comm.gather(ffn, "near", comm.gather(EARLY, "pair"), (1, 8, 7)))
    y_ca = _ca_fwd(qkv, bias, n_ca, ca_col0, name="ca_fwd", ride=comm.gather(ffn, "far"))
    out = ("w_ffn_out",)
    o_sb = _mm(y_sb, w["w_sb_out"], "nn", (BF16,), name="sb_out", ride=comm.gather(out, "near", part=(0, 4)))
    o_ca = _mm(y_ca, w["w_ca_out"], "nn", (BF16,), name="ca_out", ride=comm.gather(out, "near", part=(1, 4)))
    merged = _gate_merge_fwd(gates, o_sb, o_ca, name="gate_merge", ride=comm.gather(out, "near", part=(2, 4)))
    x1 = _mm(merged, w["w_mix_out"], "nn", (F32,), name="mix_out", resid=x,
             ride=comm.gather(out, "near", comm.gather(ffn, "pair"), (3, 4)))
    h2 = _rms_fwd(x1, small["g_ffn"], name="rms_ffn")
    gu = _mm(h2, w["w_ffn_in"], "nn", (BF16,), name="ffn_in", ride=comm.gather(ple, "near", comm.gather(out, "far")))
    act = _swiglu_fwd(gu, name="swiglu", ride=comm.gather(ple, "far", comm.gather(out, "pair")))
    x2 = _mm(act, w["w_ffn_out"], "nn", (F32,), name="ffn_out", resid=x1, ride=comm.gather(ple, "pair"))
    h3 = _rms_fwd(x2, small["g_ple"], name="rms_ple")
    t = _mm(h3, w["w_ple_gate"], "nn", (BF16,), name="ple_gate")
    pe = _mm(p, w["w_ple_in"], "nn", (BF16,), name="ple_in")

    def halves(n, acts, dout, ride, name):
        if comm.pos is None:
            return comm.grad(n, *_mm(acts, dout, "tn", both, name=name))
        g16 = _mm(acts, dout, "tn", (BF16,), name=name + "_other", m_half=(False, comm.pos), ride=ride)
        comm.grad(n, None, g16, half=True)
        g32 = _mm(acts, dout, "tn", (F32,), name=name + "_own", m_half=(True, comm.pos), ride=comm.pair((n,)))
        comm.grad(n, g32, g16, half=True)

    gs = {}
    dx3, dt, dpe, gs["g_final"], loss = _ple_loss(x2, t, pe, small["g_final"], target, name="ple_loss")
    comm.grad("w_ple_in", *_mm(p, dpe, "tn", both, name="dw_ple_in"))
    comm.grad("w_ple_gate", *_mm(h3, dt, "tn", both, name="dw_ple_gate"))
    ple = ("w_ple_in", "w_ple_gate")
    dh3 = _mm(dt, w["w_ple_gate"], "nt", (BF16,), name="dh_ple", ride=comm.pair(ple))
    dx2, dx2_16, gs["g_ple"] = _rms_bwd(x2, small["g_ple"], dh3, dx3, name="rms_ple_bwd")
    comm.add(ple)
    comm.grad("w_ffn_out", *_mm(act, dx2_16, "tn", both, name="dw_ffn_out", ride=comm.chips(ple)))
    dact = _mm(dx2_16, w["w_ffn_out"], "nt", (BF16,), name="dact", ride=comm.pair(("w_ffn_out",)))
    dgu = _swiglu_bwd(dact, gu, name="swiglu_bwd")
    comm.sum(ple)
    comm.add(("w_ffn_out",))
    comm.grad("w_ffn_in", *_mm(h2, dgu, "tn", both, name="dw_ffn_in",
                               ride=comm.share(ple, comm.chips(("w_ffn_out",)))))
    dh2 = _mm(dgu, w["w_ffn_in"], "nt", (BF16,), name="dh_ffn", ride=comm.pair(("w_ffn_in",)))
    dx1, dx1_16, gs["g_ffn"] = _rms_bwd(x1, small["g_ffn"], dh2, dx2, name="rms_ffn_bwd")
    comm.add(("w_ffn_in",))
    comm.sum(("w_ffn_out",))
    comm.grad("w_mix_out", *_mm(merged, dx1_16, "tn", both, name="dw_mix_out", ride=comm.share(("w_ffn_out",))))
    dmerged = _mm(dx1_16, w["w_mix_out"], "nt", (BF16,), name="dmerged", ride=comm.pair(("w_mix_out",)))
    dg_sb, dg_ca, do_sb, do_ca = _gate_merge_bwd(dmerged, gates, o_sb, o_ca, name="gate_merge_bwd")
    comm.add(("w_mix_out",))
    comm.grad("w_sb_out", *_mm(y_sb, do_sb, "tn", both, name="dw_sb_out"))
    comm.grad("w_ca_out", *_mm(y_ca, do_ca, "tn", both, name="dw_ca_out"))
    outs = ("w_sb_out", "w_ca_out")
    dy_sb = _mm(do_sb, w["w_sb_out"], "nt", (BF16,), name="dy_sb", ride=comm.pair(outs))
    dy_ca = _mm(do_ca, w["w_ca_out"], "nt", (BF16,), name="dy_ca")
    comm.add(outs)
    dq_sb, dk_sb, dv_sb = _sb_bwd(qkv, dy_sb, n_sb, 0, name="sb_bwd", ride=comm.chips(("w_ffn_in",)))
    comm.sum(("w_ffn_in",))
    late = ("w_mix_out",) + outs
    dq_ca, dk_ca, dv_ca, dbias = _ca_bwd(qkv, bias, dy_ca, n_ca, ca_col0, name="ca_bwd",
                                         ride=comm.chips(late, comm.share(("w_ffn_in",))))
    comm.sum(late)
    gs["rel_bias"] = _band_bias_grad(_group_bias_grad(dbias))
    dproj = _concat_cols([dq_sb, dk_sb, dv_sb, dq_ca, dk_ca, dv_ca, dg_sb, dg_ca], name="dproj")
    halves("w_in", h1, dproj, comm.share(late), "dw_in")
    comm.add(("w_in",))
    half = x.shape[0] // 2
    dh1 = _mm(dproj, w["w_in"], "nt", (F32,), name="dh_mix_top", rows=(0, half), ride=comm.tail(TAIL_SECOND))
    dh1 = _mm(dproj, w["w_in"], "nt", (F32,), name="dh_mix_bottom", rows=(half, half), onto=(dh1,),
              ride=comm.tail(TAIL_SECOND))
    grad_x, _, gs["g_mix"] = _rms_bwd(x, small["g_mix"], dh1, dx1, name="rms_mix_bwd", ride=comm.tail(TAIL_FIRST))
    return loss, grad_x, gs


def _position():
    x, y, c = lax.axis_index("x"), lax.axis_index("y"), lax.axis_index("c")
    chips = [(1 - x, y), (x, 1 - y), (1 - x, 1 - y)]
    return x, y, c, chips


def _aligned(v, m):
    return v if isinstance(v, int) else pl.multiple_of(v, m)


def _piece_dims(shape, axis):
    k, n = shape
    return (k // 2, n // N_CHIPS) if axis == 1 else (k // N_CHIPS // 2, n)


def _piece(ref, shape, axis, j, h, part=(0, 1)):
    pr, pc = _piece_dims(shape, axis)
    nr = pr // part[1] * (part[2] if len(part) > 2 else 1)
    r0 = part[0] * (pr // part[1])
    if axis == 1:
        return ref.at[pl.ds(_aligned(h * pr + r0, 16), nr), pl.ds(_aligned(j * pc, 128), pc)]
    return ref.at[pl.ds(_aligned((2 * j + h) * pr + r0, 16), nr), :]


def _shard_half(ref, h):
    rows = ref.shape[0] // 2
    return ref.at[pl.ds(_aligned(h * rows, 16), rows), :]


def _remote(src, dst, send_sems, recv_sems, k, to):
    return pltpu.make_async_remote_copy(src_ref=src, dst_ref=dst, send_sem=send_sems.at[k],
                                        recv_sem=recv_sems.at[k], device_id=to, device_id_type=MESH)


def _prefetch_call(body, scalars, ins, in_specs, out_shape, out_specs, grid, *, name, ride=None):
    single = not isinstance(out_shape, (list, tuple))
    outs = _call(body, ins, name=name, grid=grid, in_specs=in_specs,
                 out_specs=[out_specs] if single else out_specs, out_shape=[out_shape] if single else out_shape,
                 sem=("parallel",) * len(grid), ride=ride, scalars=scalars)
    return outs[0] if single else outs


def _slab_tiles(pr, pc):
    tc = pc if pc <= 4096 else _pick(pc, (2048, 1024, 512, 256, 128))
    tr = next(t for t in (1024, 512, 256, 128, 64, 32, 16) if pr % t == 0 and t * tc <= 512 * 1024)
    return tr, tc


def _cast_place(w, axis, pos, *, name, ride=None):
    ks, ns = w.shape
    shape = (ks, ns * N_CHIPS) if axis == 1 else (ks * N_CHIPS, ns)
    tr, tc = _slab_tiles(ks, ns)
    nr, nc = ks // tr, ns // tc

    def body(pos_ref, w_ref, o_ref):
        o_ref[...] = w_ref[...].astype(o_ref.dtype)

    if axis == 1:
        out_map = lambda i, j, pos_ref: (i, pos_ref[0] * nc + j)
    else:
        out_map = lambda i, j, pos_ref: (pos_ref[0] * nr + i, j)
    return _prefetch_call(body, pos, [w], [pl.BlockSpec((tr, tc), lambda i, j, pos_ref: (i, j))],
                          jax.ShapeDtypeStruct(shape, BF16), pl.BlockSpec((tr, tc), out_map), (nr, nc), name=name, ride=ride)


def _run(ride, *, name):
    if ride is None:
        return

    def body(o_ref):
        o_ref[...] = jnp.zeros_like(o_ref)

    _call(body, [], name=name, grid=(1,), in_specs=[], out_specs=[pl.BlockSpec((8, 128), lambda i: (0, 0))],
          out_shape=[jax.ShapeDtypeStruct((8, 128), F32)], ride=ride)


def _ride_gather(ride, w, n, axis, stage, part=(0, 1)):
    shape = w[n].shape
    piece = functools.partial(_piece, shape=shape, axis=axis)
    span = part[2] if len(part) > 2 else 1
    halves = [(2 * part[0] + t * span, 2 * part[1], span) for t in range(2)]

    def copies(ins, outs, send_sems, recv_sems, arriving):
        x, y, c, chips = _position()
        me, (xn, yn, dn) = 2 * x + y, [2 * px + py for px, py in chips]
        if stage == "near":
            plan = [(me, c, part, (1 - x, y, c), xn, c, part), (me, c, part, (x, 1 - y, c), yn, c, part)]
        else:
            plan = []
        if stage == "far":
            plan = [(yn, c, halves[1], (1 - x, y, c), dn, c, halves[1]), (xn, c, halves[0], (x, 1 - y, c), dn, c, halves[0])]
        to_sibling = {"far": (xn, yn), "pair": (dn,)}.get(stage, ())
        plan += [(j, c, part, (x, y, 1 - c), j, 1 - c, part) for j in to_sibling]
        out = []
        for k, (chip, h, rows, to, from_chip, from_h, from_rows) in enumerate(plan):
            if arriving:
                lands = piece(outs[0], j=from_chip, h=from_h, part=from_rows)
                out.append(_remote(lands, lands, send_sems, recv_sems, k, to))
            else:
                out.append(_remote(piece(ins[0], j=chip, h=h, part=rows), piece(outs[0], j=chip, h=h, part=rows),
                                   send_sems, recv_sems, k, to))
        return out

    def start(*refs):
        for cp in copies(*refs, arriving=False):
            cp.start()

    def finish(*refs):
        for cp in copies(*refs, arriving=True):
            cp.wait_recv()
        for cp in copies(*refs, arriving=False):
            cp.wait_send()

    ride.add([w[n]], [jax.ShapeDtypeStruct(shape, w[n].dtype)], {0: 0}, 4, start, finish,
             lambda outs: w.__setitem__(n, outs[0]))


def _ride_pair(ride, st, axis):
    shape = st["g16"].shape
    pr, pc = (shape[0], shape[1] // N_CHIPS) if st.get("half") else _piece_dims(shape, axis)

    def copies(ins, outs, send_sems, recv_sems):
        x, y, c, _ = _position()
        if st.get("half"):
            pieces = [ins[0].at[:, pl.ds(j * pc, pc)] for j in range(N_CHIPS)]
        else:
            pieces = [_piece(ins[0], shape, axis, j, 1 - c) for j in range(N_CHIPS)]
        return [_remote(pieces[j], outs[0].at[j], send_sems, recv_sems, j, (x, y, 1 - c)) for j in range(N_CHIPS)]

    def start(*refs):
        for cp in copies(*refs):
            cp.start()

    def finish(*refs):
        for cp in copies(*refs):
            cp.wait()

    ride.add([st["g16"]], [jax.ShapeDtypeStruct((N_CHIPS, pr, pc), BF16)], {}, N_CHIPS, start, finish,
             lambda outs: st.__setitem__("sib", outs[0]))


def _ride_chips(ride, st, rows=None):
    _, pr, pc = st["s16"].shape
    r0, nr = (0, pr) if rows is None else rows

    def copies(ins, outs, send_sems, recv_sems):
        x, y, c, chips = _position()
        return [_remote(ins[0].at[2 * px + py, pl.ds(r0, nr), :], outs[0].at[k, pl.ds(r0, nr), :],
                        send_sems, recv_sems, k, (px, py, c)) for k, (px, py) in enumerate(chips)]

    def start(*refs):
        for cp in copies(*refs):
            cp.start()

    def finish(*refs):
        for cp in copies(*refs):
            cp.wait()

    ins, aliases = ([st["s16"], st["recv"]], {1: 0}) if "recv" in st else ([st["s16"]], {})
    ride.add(ins, [jax.ShapeDtypeStruct((3, pr, pc), BF16)], aliases, 3, start, finish,
             lambda outs: st.__setitem__("recv", outs[0]))


def _ride_share(ride, st):
    def sent(ins, outs, send_sems, recv_sems):
        x, y, c, _ = _position()
        return _remote(_shard_half(ins[0], c), _shard_half(outs[0], c), send_sems, recv_sems, 0, (x, y, 1 - c))

    def landed(ins, outs, send_sems, recv_sems):
        x, y, c, _ = _position()
        other = _shard_half(outs[0], 1 - c)
        return _remote(other, other, send_sems, recv_sems, 0, (x, y, 1 - c))

    def start(*refs):
        sent(*refs).start()

    def finish(*refs):
        landed(*refs).wait_recv()
        sent(*refs).wait_send()

    ride.add([st["shard"]], [jax.ShapeDtypeStruct(st["shard"].shape, F32)], {0: 0}, 1, start, finish,
             lambda outs: st.__setitem__("g", outs[0]))


def _piece_block(axis, nr, nc, chip, half=False):
    if half:
        return lambda *a: (a[-3], chip(a) * nc + a[-2])
    if axis == 1:
        return lambda *a: ((a[-1][1] * nr + a[-3]), chip(a) * nc + a[-2])
    return lambda *a: ((2 * chip(a) + a[-1][1]) * nr + a[-3], a[-2])


def _pair_add(g32, sib, axis, pos, *, name, half=False):
    _, pr, pc = sib.shape
    tr, tc = _slab_tiles(pr, pc)
    nr, nc = pr // tr, pc // tc

    def body(pos_ref, g_ref, b_ref, o16_ref):
        o16_ref[0] = (g_ref[...] + b_ref[0].astype(F32)).astype(o16_ref.dtype)

    other = lambda a: (a[-1][0] + 1 + a[0]) % N_CHIPS
    blk = pl.BlockSpec((1, tr, tc), lambda *a: (other(a), a[1], a[2]))
    return _prefetch_call(body, pos, [g32, sib], [pl.BlockSpec((tr, tc), _piece_block(axis, nr, nc, other, half)), blk],
                          jax.ShapeDtypeStruct(sib.shape, BF16), blk, (N_CHIPS - 1, nr, nc), name=name)


def _chip_sum(g32, sib, recv, axis, pos, *, name, half=False):
    _, pr, pc = sib.shape
    tr, tc = _slab_tiles(pr, pc)
    nr, nc = pr // tr, pc // tc

    def body(pos_ref, g_ref, b_ref, r_ref, o_ref):
        pair = g_ref[...] + b_ref[0].astype(F32)
        o_ref[...] = ((pair + r_ref[0].astype(F32)) + r_ref[1].astype(F32)) + r_ref[2].astype(F32)

    return _prefetch_call(
        body, pos, [g32, sib, recv],
        [pl.BlockSpec((tr, tc), _piece_block(axis, nr, nc, lambda a: a[-1][0], half)),
         pl.BlockSpec((1, tr, tc), lambda i, k, pos_ref: (pos_ref[0], i, k)),
         pl.BlockSpec((3, tr, tc), lambda i, k, pos_ref: (0, i, k))],
        jax.ShapeDtypeStruct((2 * pr, pc), F32),
        pl.BlockSpec((tr, tc), lambda i, k, pos_ref: (pos_ref[1] * nr + i, k)), (nr, nc), name=name)


class _Comm:
    def __init__(self, pos, w):
        self.pos, self.w, self.st = pos, w, {n: {} for n, _ in BIG}

    def gather(self, names, stage, ride=None, part=(0, 1)):
        ride = _Ride() if ride is None else ride
        for n in names:
            _ride_gather(ride, self.w, n, AXIS[n], stage, part)
        return ride

    def grad(self, n, g32, g16, half=False):
        self.st[n].update(g32=g32, g16=g16, half=half)

    def pair(self, names, ride=None):
        ride = _Ride() if ride is None else ride
        for n in names:
            _ride_pair(ride, self.st[n], AXIS[n])
        return ride

    def add(self, names):
        for n in names:
            st = self.st[n]
            st["s16"] = _pair_add(st["g32"], st["sib"], AXIS[n], self.pos, name="rs_add_" + n, half=st["half"])

    def chips(self, names, ride=None, rows=None):
        ride = _Ride() if ride is None else ride
        for n in names:
            _ride_chips(ride, self.st[n], rows)
        return ride

    def sum(self, names):
        for n in names:
            st = self.st[n]
            st["shard"] = _chip_sum(st["g32"], st["sib"], st["recv"], AXIS[n], self.pos, name="rs_sum_" + n,
                                    half=st["half"])

    def share(self, names, ride=None):
        ride = _Ride() if ride is None else ride
        for n in names:
            _ride_share(ride, self.st[n])
        return ride

    def tail(self, count):
        st = self.st["w_in"]
        rows, at = st["s16"].shape[1], st.get("at", 0)
        st["at"] = at + count
        return self.chips(("w_in",), rows=(at * rows // TAIL_PARTS, count * rows // TAIL_PARTS))

    def tail_rest(self):
        return self.tail(TAIL_PARTS - self.st["w_in"].get("at", 0))

    def result(self, n):
        return self.st[n]["g"]


class _NoComm:
    pos = None

    def __init__(self, w):
        self.w, self.st = w, {}

    def grad(self, n, g32, g16, half=False):
        self.st[n] = (g32, g16)

    def result(self, n):
        return self.st[n]

    def add(self, names):
        pass

    sum = add

    def gather(self, names, *args, **kwargs):
        return None

    pair = chips = share = tail = gather


def _small_all_reduce(vec, *, name):
    r = vec.shape[0]

    def body(vec_ref, out_ref, slots, send_sems, recv_sems):
        x, y, c, _ = _position()
        me = 4 * x + 2 * y + c
        slots[me] = vec_ref[...]
        sends = []
        for k in range(1, 8):
            to = (x ^ (k >> 2), y ^ ((k >> 1) & 1), c ^ (k & 1))
            cp = _remote(slots.at[me], slots.at[me], send_sems, recv_sems, k - 1, to)
            cp.start()
            sends.append(cp)
        for k in range(1, 8):
            frm = 4 * (x ^ (k >> 2)) + 2 * (y ^ ((k >> 1) & 1)) + (c ^ (k & 1))
            _remote(slots.at[frm], slots.at[frm], send_sems, recv_sems, k - 1, (x, y, c)).wait_recv()
        for cp in sends:
            cp.wait_send()
        total = slots[0]
        for d in range(1, 8):
            total = total + slots[d]
        out_ref[...] = total

    return pl.pallas_call(
        body, name=name,
        in_specs=[pl.BlockSpec(memory_space=pltpu.VMEM)], out_specs=pl.BlockSpec(memory_space=pltpu.VMEM),
        out_shape=jax.ShapeDtypeStruct((r, 128), F32),
        scratch_shapes=[pltpu.VMEM((8, r, 128), F32), pltpu.SemaphoreType.DMA((7,)), pltpu.SemaphoreType.DMA((7,))],
    )(vec)


SC_TILES = 32
SC_LANES = 16
SC_TILE_BUDGET = 400 * 1024


def _adamw_update(wv, gv, mv, vv):
    nm = ADAM_B1 * mv + (1.0 - ADAM_B1) * gv
    nv = ADAM_B2 * vv + (1.0 - ADAM_B2) * (gv * gv)
    m_hat = nm / (1.0 - ADAM_B1 ** ADAM_STEP)
    v_hat = nv / (1.0 - ADAM_B2 ** ADAM_STEP)
    return -ADAM_LR * (m_hat / (jnp.sqrt(v_hat) + ADAM_EPS) + ADAM_WD * wv), nm, nv


def _adamw_sc(w, g, m, v, *, name):
    r, c = w.shape
    groups = r // 8
    per_tile = -(-groups // SC_TILES)
    cb = c if 4 * 8 * c * 4 <= SC_TILE_BUDGET else _pick(c, (2048, 1024, 512, 256, 128))

    def body(w_hbm, g_hbm, m_hbm, v_hbm, go_hbm, d_hbm, nm_hbm, nv_hbm, wb, gb, mb, vb):
        tile = lax.axis_index("sc_tile") * 2 + lax.axis_index("sc_core")

        def update(group):
            for c0 in range(0, c, cb):
                at = (pl.ds(group * 8, 8), pl.ds(c0, cb))
                for hbm, buf in ((w_hbm, wb), (g_hbm, gb), (m_hbm, mb), (v_hbm, vb)):
                    pltpu.sync_copy(hbm.at[at], buf)
                pltpu.sync_copy(gb, go_hbm.at[at])

                @pl.loop(0, 8)
                def _(rr):
                    @pl.loop(0, cb, step=SC_LANES)
                    def _(i):
                        lanes = (rr, pl.ds(i, SC_LANES))
                        wb[lanes], mb[lanes], vb[lanes] = _adamw_update(wb[lanes], gb[lanes], mb[lanes], vb[lanes])

                for buf, hbm in ((wb, d_hbm), (mb, nm_hbm), (vb, nv_hbm)):
                    pltpu.sync_copy(buf, hbm.at[at])

        @pl.loop(0, per_tile)
        def _(k):
            group = k * SC_TILES + tile
            if groups % SC_TILES:
                pl.when(group < groups)(lambda: update(group))
            else:
                update(group)

    sd = jax.ShapeDtypeStruct((r, c), F32)
    return pl.kernel(body, name=name, out_type=[sd, sd, sd, sd],
                     mesh=plsc.VectorSubcoreMesh(core_axis_name="sc_core", subcore_axis_name="sc_tile"),
                     scratch_types=[pltpu.VMEM((8, cb), F32)] * 4)(w, g, m, v)


def _adamw(w, g, m, v, *, name, ride=None):
    r, c = w.shape
    tc = c if c <= 4096 else _pick(c, (2048, 1024, 512, 256, 128))
    tr = next(t for t in (512, 256, 128, 64, 32, 16, 8) if r % t == 0 and t * tc <= 256 * 1024)

    def body(w_ref, g_ref, m_ref, v_ref, go_ref, d_ref, nm_ref, nv_ref):
        go_ref[...] = g_ref[...]
        d_ref[...], nm_ref[...], nv_ref[...] = _adamw_update(w_ref[...], g_ref[...], m_ref[...], v_ref[...])

    blk = ((tr, tc), lambda i, j: (i, j))
    sd = jax.ShapeDtypeStruct((r, c), F32)
    return _ew(body, [w, g, m, v], [blk] * 4, [sd] * 4, [blk] * 4, (r // tr, c // tc), name=name, ride=ride)


BIG = (("w_in", 1), ("w_sb_out", 1), ("w_ca_out", 1), ("w_mix_out", 0), ("w_ffn_in", 1), ("w_ffn_out", 0),
       ("w_ple_in", 1), ("w_ple_gate", 0))
AXIS = dict(BIG)
HEAD_PARTS = 8
HEAD_HOSTS = ("w_ffn_in", "w_ffn_out")
TAIL_PARTS = 16
TAIL_SECOND = 5
TAIL_FIRST = 2
ON_SPARSECORE = tuple(n for n, _ in BIG if n != "w_in")
SMALL = ("rel_bias", "g_mix", "g_ffn", "g_ple", "g_final")
ORDER = ("w_in", "w_sb_out", "w_ca_out", "w_mix_out", "rel_bias", "g_mix", "g_ffn", "g_ple", "g_final",
         "w_ffn_in", "w_ffn_out", "w_ple_in", "w_ple_gate")


def _pack(parts):
    flat = jnp.concatenate([a.reshape(-1) for a in parts])
    rows = -(-flat.shape[0] // 1024) * 8
    return jnp.pad(flat, (0, rows * 128 - flat.shape[0])).reshape(rows, 128)


def _unpack(packed, like):
    flat, out, at = packed.reshape(-1), [], 0
    for a in like:
        out.append(flat[at:at + a.size].reshape(a.shape))
        at += a.size
    return out


def kernel(x, p, w_in, w_sb_out, w_ca_out, w_mix_out, rel_bias, g_mix, g_ffn, g_ple, g_final, w_ffn_in, w_ffn_out, w_ple_in, w_ple_gate, loss_target, m_w_in, m_w_sb_out, m_w_ca_out, m_w_mix_out, m_rel_bias, m_g_mix, m_g_ffn, m_g_ple, m_g_final, m_w_ffn_in, m_w_ffn_out, m_w_ple_in, m_w_ple_gate, v_w_in, v_w_sb_out, v_w_ca_out, v_w_mix_out, v_rel_bias, v_g_mix, v_g_ffn, v_g_ple, v_g_final, v_w_ffn_in, v_w_ffn_out, v_w_ple_in, v_w_ple_gate):
    weights = dict(w_in=w_in, w_sb_out=w_sb_out, w_ca_out=w_ca_out, w_mix_out=w_mix_out, rel_bias=rel_bias,
                   g_mix=g_mix, g_ffn=g_ffn, g_ple=g_ple, g_final=g_final, w_ffn_in=w_ffn_in,
                   w_ffn_out=w_ffn_out, w_ple_in=w_ple_in, w_ple_gate=w_ple_gate)
    m_in = dict(w_in=m_w_in, w_sb_out=m_w_sb_out, w_ca_out=m_w_ca_out, w_mix_out=m_w_mix_out, rel_bias=m_rel_bias,
                g_mix=m_g_mix, g_ffn=m_g_ffn, g_ple=m_g_ple, g_final=m_g_final, w_ffn_in=m_w_ffn_in,
                w_ffn_out=m_w_ffn_out, w_ple_in=m_w_ple_in, w_ple_gate=m_w_ple_gate)
    v_in = dict(w_in=v_w_in, w_sb_out=v_w_sb_out, w_ca_out=v_w_ca_out, w_mix_out=v_w_mix_out, rel_bias=v_rel_bias,
                g_mix=v_g_mix, g_ffn=v_g_ffn, g_ple=v_g_ple, g_final=v_g_final, w_ffn_in=v_w_ffn_in,
                w_ffn_out=v_w_ffn_out, w_ple_in=v_w_ple_in, w_ple_gate=v_w_ple_gate)

    pos = jnp.stack([2 * lax.axis_index("x") + lax.axis_index("y"), lax.axis_index("c")]).astype(jnp.int32)
    comm = _Comm(pos, {"w_in": _cast_place(w_in[0], AXIS["w_in"], pos, name="cast_w_in")})
    at = 0
    for n in HEAD_HOSTS:
        ride = comm.gather(("w_in",), "near", part=(at, HEAD_PARTS))
        comm.w[n] = _cast_place(weights[n][0], AXIS[n], pos, name="cast_" + n, ride=ride)
        at += 1
    for n, axis in BIG:
        if n not in comm.w:
            comm.w[n] = _cast_place(weights[n][0], axis, pos, name="cast_" + n)
    _run(comm.gather(("w_in",), "near", part=(at, HEAD_PARTS, HEAD_PARTS - at)), name="gather_w_in_near")
    _run(comm.gather(("w_in",), "far"), name="gather_w_in_far")
    _run(comm.gather(("w_in",), "pair"), name="gather_w_in_pair")
    small = dict(rel_bias=rel_bias[0], g_mix=g_mix, g_ffn=g_ffn, g_ple=g_ple, g_final=g_final.reshape(1, -1))
    loss, grad_x, gs = _step(x[0], p[0, 0], loss_target[0], small, comm)

    grads, delta, new_m, new_v = {}, {}, {}, {}
    for n in [n for n, _ in BIG if n != "w_in"] + ["w_in"]:
        if n == "w_in":
            _run(comm.tail_rest(), name="rs_chips_w_in")
            comm.sum(("w_in",))
            _run(comm.share(("w_in",)), name="rs_share_w_in")
        update = _adamw_sc if n in ON_SPARSECORE else _adamw
        g, d, nm, nv = update(weights[n][0], comm.result(n), m_in[n][0], v_in[n][0], name="adamw_" + n)
        grads[n], delta[n], new_m[n], new_v[n] = g[None], d[None], nm[None], nv[None]

    like = [weights[n] for n in SMALL]
    reduced = _small_all_reduce(_pack([gs[n] for n in SMALL] + [loss[:, :1]]), name="small_all_reduce")
    g_small = _unpack(reduced, like + [loss[:, :1]])
    total_loss = g_small[-1].reshape(())
    g_packed = _pack(g_small[:-1])
    _, d_s, m_s, v_s = _adamw(_pack(like), g_packed, _pack([m_in[n] for n in SMALL]), _pack([v_in[n] for n in SMALL]),
                           name="adamw_small")
    for n, g, d, nm, nv in zip(SMALL, g_small[:-1], _unpack(d_s, like), _unpack(m_s, like), _unpack(v_s, like)):
        grads[n], delta[n], new_m[n], new_v[n] = g, d, nm, nv

    return (total_loss, grad_x[None], *[grads[n] for n in ORDER], *[delta[n] for n in ORDER],
            *[new_m[n] for n in ORDER], *[new_v[n] for n in ORDER])
```
